```python
import math, functools
import jax, jax.numpy as jnp
from jax import lax
import numpy as np

D_MODEL = 2048
BATCH = 1
SEQ = 8192
DEPTH = 1
DEC_BATCH = 128
DEC_SEQ = 4
PAST_LEN = 16384
PAGE_SIZE = 128

ATT_HEADS = 16
ATT_KV_HEADS = 4
ATT_HEAD_DIM = 64
ATT_GROUP = ATT_HEADS // ATT_KV_HEADS
WINDOW = 128
ROPE_THETA = 500000.0
ROPE_DIM = ATT_HEAD_DIM // 4
MLSTM_HEADS = 4
MLSTM_HEAD_DIM = 256
MLSTM_CHUNK = 64
FGATE_BIAS_LO = 3.0
FGATE_BIAS_HI = 6.0
ATT_WIDTH = ATT_HEADS * ATT_HEAD_DIM
KV_WIDTH = ATT_KV_HEADS * ATT_HEAD_DIM
MLSTM_WIDTH = MLSTM_HEADS * MLSTM_HEAD_DIM
MIX_WIDTH = ATT_WIDTH + MLSTM_WIDTH
IN_WIDTH = ATT_WIDTH + 2 * KV_WIDTH + 4 * MLSTM_WIDTH + 2 * MLSTM_HEADS
D_FF = ((8 * D_MODEL // 3 + 255) // 256) * 256
N_MOD = 6
EPS = 1e-6

kernel_name = 'hymba_mlstm_swa_sink_decoder_step'


def rmsnorm(x, w):
    xf = x.astype(jnp.float32)
    y = xf * lax.rsqrt(jnp.mean(xf * xf, axis=-1, keepdims=True) + EPS)
    return (y * w.astype(jnp.float32)).astype(x.dtype)


def partial_rope(x, pos):
    half = ROPE_DIM // 2
    inv = ROPE_THETA ** (-jnp.arange(0, ROPE_DIM, 2, dtype=jnp.float32) / ROPE_DIM)
    ang = pos.astype(jnp.float32)[:, None] * inv[None, :]
    cos = jnp.cos(ang)[None, :, None, :]
    sin = jnp.sin(ang)[None, :, None, :]
    xr = x[..., :ROPE_DIM].astype(jnp.float32)
    x1, x2 = xr[..., :half], xr[..., half:]
    rot = jnp.concatenate([x1 * cos - x2 * sin, x2 * cos + x1 * sin], axis=-1)
    return jnp.concatenate([rot.astype(x.dtype), x[..., ROPE_DIM:]], axis=-1)


def sink_attention(q, k, v, allowed, sinks):
    s = jnp.einsum('bnqhgd,bnkhd->bnhgqk', q, k).astype(jnp.float32) * (ATT_HEAD_DIM ** -0.5)
    s = jnp.where(allowed[None, :, None, None], s, -jnp.inf)
    sink = jnp.broadcast_to(sinks.astype(jnp.float32).reshape(1, 1, ATT_KV_HEADS, ATT_GROUP, 1, 1),
                            s.shape[:-1] + (1,))
    p = jax.nn.softmax(jnp.concatenate([s, sink], axis=-1), axis=-1)[..., :-1]
    return jnp.einsum('bnhgqk,bnkhd->bnqhgd', p.astype(v.dtype), v)


def swa_prompt(q, k, v, sinks):
    B, S = q.shape[0], q.shape[1]
    nb = S // WINDOW
    qb = q.reshape(B, nb, WINDOW, ATT_KV_HEADS, ATT_GROUP, ATT_HEAD_DIM)
    pad = jnp.zeros((B, WINDOW, ATT_KV_HEADS, ATT_HEAD_DIM), k.dtype)
    kp = jnp.concatenate([pad, k], axis=1).reshape(B, nb + 1, WINDOW, ATT_KV_HEADS, ATT_HEAD_DIM)
    vp = jnp.concatenate([pad.astype(v.dtype), v], axis=1).reshape(B, nb + 1, WINDOW, ATT_KV_HEADS, ATT_HEAD_DIM)
    kb = jnp.concatenate([kp[:, :-1], kp[:, 1:]], axis=2)
    vb = jnp.concatenate([vp[:, :-1], vp[:, 1:]], axis=2)
    qi = jnp.arange(WINDOW)[:, None]
    kj = jnp.arange(2 * WINDOW)[None, :]
    diff = qi + WINDOW - kj
    key_pos = jnp.arange(nb)[:, None, None] * WINDOW - WINDOW + kj[None]
    allowed = ((diff >= 0) & (diff < WINDOW))[None] & (key_pos >= 0)
    o = sink_attention(qb, kb, vb, allowed, sinks)
    return o.reshape(B, S, ATT_WIDTH), k[:, S - WINDOW:], v[:, S - WINDOW:]


def swa_sample(q, k, v, k_buf, v_buf, sinks):
    B, T = q.shape[0], q.shape[1]
    kk = jnp.concatenate([k_buf.astype(k.dtype), k], axis=1)
    vv = jnp.concatenate([v_buf.astype(v.dtype), v], axis=1)
    qi = jnp.arange(T)[:, None]
    kj = jnp.arange(WINDOW + T)[None, :]
    diff = qi + WINDOW - kj
    allowed = ((diff >= 0) & (diff < WINDOW))[None]
    o = sink_attention(q.reshape(B, 1, T, ATT_KV_HEADS, ATT_GROUP, ATT_HEAD_DIM),
                       kk[:, None], vv[:, None], allowed, sinks)
    return o.reshape(B, T, ATT_WIDTH), kk[:, T:], vv[:, T:]


def mlstm(q, k, v, ig, fg, C0, n0, m0):
    B, S, H, DH = q.shape
    L = math.gcd(S, MLSTM_CHUNK)
    nc = S // L
    f32 = jnp.float32

    def chunks(a):
        return jnp.transpose(a.astype(f32).reshape(B, nc, L, H, DH), (1, 0, 3, 2, 4))

    def gchunks(a):
        return jnp.transpose(a.astype(f32).reshape(B, nc, L, H), (1, 0, 3, 2))

    qc = chunks(q) * (DH ** -0.5)
    kc, vc = chunks(k), chunks(v)
    li = gchunks(ig)
    lf = jax.nn.log_sigmoid(gchunks(fg))
    causal = jnp.tril(jnp.ones((L, L), dtype=bool))

    def step(carry, xs):
        C, n, m = carry
        qt, kt, vt, lit, lft = xs
        b = jnp.cumsum(lft, axis=-1)
        dm = jnp.where(causal, b[..., :, None] - b[..., None, :] + lit[..., None, :], -jnp.inf)
        m_inter = b + m[..., None]
        m_t = jnp.maximum(m_inter, jnp.max(dm, axis=-1))
        sm = jnp.einsum('bhtd,bhsd->bhts', qt, kt) * jnp.exp(dm - m_t[..., None])
        a = jnp.exp(m_inter - m_t)
        num = a[..., None] * jnp.einsum('bhtd,bhvd->bhtv', qt, C) + jnp.einsum('bhts,bhsv->bhtv', sm, vt)
        den = a * jnp.einsum('bhtd,bhd->bht', qt, n) + jnp.sum(sm, axis=-1)
        h = num / jnp.maximum(jnp.abs(den), jnp.exp(-m_t))[..., None]
        m_new = m_t[..., -1]
        g = jnp.exp(b[..., -1:] - b + lit - m_new[..., None])
        decay = jnp.exp(b[..., -1] + m - m_new)
        C_new = decay[..., None, None] * C + jnp.einsum('bhs,bhsv,bhsd->bhvd', g, vt, kt)
        n_new = decay[..., None] * n + jnp.einsum('bhs,bhsd->bhd', g, kt)
        return (C_new, n_new, m_new), h

    (C, n, m), hs = lax.scan(step, (C0.astype(f32), n0.astype(f32), m0.astype(f32)), (qc, kc, vc, li, lf))
    h = jnp.transpose(hs, (1, 0, 3, 2, 4)).reshape(B, S, H, DH)
    return h.astype(q.dtype), C, n, m


def trunk_layer(x, c, pos, attn_fn, C0, n0, m0, norm1_w, norm2_w, w_ada, b_ada, w_in, b_ig, b_fg,
                mh_norm_w, w_out, w_gate, w_up, w_down):
    B, S = x.shape[0], x.shape[1]
    mod = (jax.nn.silu(c) @ w_ada + b_ada)[:, None, :]
    sh1, sc1, g1, sh2, sc2, g2 = jnp.split(mod, N_MOD, axis=-1)
    h = rmsnorm(x, norm1_w) * (1 + sc1) + sh1
    p = h @ w_in
    cuts = [int(s) for s in np.cumsum([ATT_WIDTH, KV_WIDTH, KV_WIDTH, MLSTM_WIDTH, MLSTM_WIDTH,
                                       MLSTM_WIDTH, MLSTM_WIDTH, MLSTM_HEADS])]
    aq, ak, av, mq, mk, mv, mo, mi, mf = jnp.split(p, cuts, axis=-1)
    aq = partial_rope(aq.reshape(B, S, ATT_HEADS, ATT_HEAD_DIM), pos)
    ak = partial_rope(ak.reshape(B, S, ATT_KV_HEADS, ATT_HEAD_DIM), pos)
    av = av.reshape(B, S, ATT_KV_HEADS, ATT_HEAD_DIM)
    att_o, k_state, v_state = attn_fn(aq, ak, av)
    mshape = (B, S, MLSTM_HEADS, MLSTM_HEAD_DIM)
    mh, C, n, m = mlstm(mq.reshape(mshape), mk.reshape(mshape), mv.reshape(mshape),
                        mi + b_ig, mf + b_fg, C0, n0, m0)
    mh = rmsnorm(mh, mh_norm_w.reshape(MLSTM_HEADS, MLSTM_HEAD_DIM)).reshape(B, S, MLSTM_WIDTH) * jax.nn.sigmoid(mo)
    x = x + g1 * (jnp.concatenate([att_o, mh], axis=-1) @ w_out)
    h = rmsnorm(x, norm2_w) * (1 + sc2) + sh2
    x = x + g2 * ((jax.nn.silu(h @ w_gate) * (h @ w_up)) @ w_down)
    return x, k_state, v_state, C, n, m


def setup_inputs(seed: int = 0) -> dict:
    key = jax.random.key(seed)
    ks = jax.random.split(key, 24)
    f32 = jnp.float32

    def nrm(k, shape, scale):
        return jax.random.normal(k, shape, f32) * scale

    return {
        'x_prompt': nrm(ks[0], (BATCH, SEQ, D_MODEL), 1.0),
        'x_sample': nrm(ks[1], (DEC_BATCH, DEC_SEQ, D_MODEL), 1.0),
        'cache_k_win': nrm(ks[2], (DEPTH, DEC_BATCH, WINDOW, ATT_KV_HEADS, ATT_HEAD_DIM), 1.0),
        'cache_v_win': nrm(ks[3], (DEPTH, DEC_BATCH, WINDOW, ATT_KV_HEADS, ATT_HEAD_DIM), 1.0),
        'state_C': nrm(ks[4], (DEPTH, DEC_BATCH, MLSTM_HEADS, MLSTM_HEAD_DIM, MLSTM_HEAD_DIM), 0.1),
        'state_n': nrm(ks[5], (DEPTH, DEC_BATCH, MLSTM_HEADS, MLSTM_HEAD_DIM), 0.5),
        'state_m': nrm(ks[6], (DEPTH, DEC_BATCH, MLSTM_HEADS), 1.0),
        'c_prompt': nrm(ks[7], (BATCH, D_MODEL), 1.0),
        'c_sample': nrm(ks[8], (DEC_BATCH, D_MODEL), 1.0),
        'norm1_w': 1.0 + nrm(ks[9], (DEPTH, D_MODEL), 0.02),
        'norm2_w': 1.0 + nrm(ks[10], (DEPTH, D_MODEL), 0.02),
        'final_norm_w': 1.0 + nrm(ks[11], (D_MODEL,), 0.02),
        'w_ada': nrm(ks[12], (DEPTH, D_MODEL, N_MOD * D_MODEL), 0.5 * D_MODEL ** -0.5),
        'b_ada': nrm(ks[13], (DEPTH, N_MOD * D_MODEL), 0.02),
        'w_in': nrm(ks[14], (DEPTH, D_MODEL, IN_WIDTH), D_MODEL ** -0.5),
        'b_ig': nrm(ks[15], (DEPTH, MLSTM_HEADS), 0.1),
        'b_fg': jnp.linspace(FGATE_BIAS_LO, FGATE_BIAS_HI, MLSTM_HEADS, dtype=f32)[None, :] + nrm(ks[16], (DEPTH, MLSTM_HEADS), 0.1),
        'attn_sinks': nrm(ks[17], (DEPTH, ATT_HEADS), 0.5),
        'mh_norm_w': 1.0 + nrm(ks[18], (DEPTH, MLSTM_WIDTH), 0.02),
        'w_out': nrm(ks[19], (DEPTH, MIX_WIDTH, D_MODEL), MIX_WIDTH ** -0.5),
        'w_gate': nrm(ks[20], (DEPTH, D_MODEL, D_FF), D_MODEL ** -0.5),
        'w_up': nrm(ks[21], (DEPTH, D_MODEL, D_FF), D_MODEL ** -0.5),
        'w_down': nrm(ks[22], (DEPTH, D_FF, D_MODEL), D_FF ** -0.5),
    }


def reference(x_prompt, x_sample, cache_k_win, cache_v_win, state_C, state_n, state_m, c_prompt, c_sample,
              norm1_w, norm2_w, final_norm_w, w_ada, b_ada, w_in, b_ig, b_fg, attn_sinks, mh_norm_w,
              w_out, w_gate, w_up, w_down):
    f32 = jnp.float32
    pos_p = jnp.arange(SEQ, dtype=jnp.int32)
    pos_s = PAST_LEN + jnp.arange(DEC_SEQ, dtype=jnp.int32)
    C0 = jnp.zeros((BATCH, MLSTM_HEADS, MLSTM_HEAD_DIM, MLSTM_HEAD_DIM), f32)
    n0 = jnp.zeros((BATCH, MLSTM_HEADS, MLSTM_HEAD_DIM), f32)
    m0 = jnp.zeros((BATCH, MLSTM_HEADS), f32)
    xp, xs = x_prompt, x_sample
    kp_l, vp_l, Cp_l, np_l, mp_l = [], [], [], [], []
    ks_l, vs_l, Cs_l, ns_l, ms_l = [], [], [], [], []
    for l in range(DEPTH):
        shared = (norm1_w[l], norm2_w[l], w_ada[l], b_ada[l], w_in[l], b_ig[l], b_fg[l],
                  mh_norm_w[l], w_out[l], w_gate[l], w_up[l], w_down[l])
        attn_p = functools.partial(swa_prompt, sinks=attn_sinks[l])
        attn_s = functools.partial(swa_sample, k_buf=cache_k_win[l], v_buf=cache_v_win[l], sinks=attn_sinks[l])
        xp, kp, vp, Cp, npr, mp = trunk_layer(xp, c_prompt, pos_p, attn_p, C0, n0, m0, *shared)
        xs, ksm, vsm, Cs, ns, ms = trunk_layer(xs, c_sample, pos_s, attn_s,
                                               state_C[l], state_n[l], state_m[l], *shared)
        kp_l.append(kp); vp_l.append(vp); Cp_l.append(Cp); np_l.append(npr); mp_l.append(mp)
        ks_l.append(ksm); vs_l.append(vsm); Cs_l.append(Cs); ns_l.append(ns); ms_l.append(ms)
    y_prompt = rmsnorm(xp, final_norm_w)
    y_sample = rmsnorm(xs, final_norm_w)
    return (y_prompt, y_sample,
            jnp.stack(kp_l), jnp.stack(vp_l), jnp.stack(Cp_l), jnp.stack(np_l), jnp.stack(mp_l),
            jnp.stack(ks_l), jnp.stack(vs_l), jnp.stack(Cs_l), jnp.stack(ns_l), jnp.stack(ms_l))
```

```python
import functools

import jax
import jax.numpy as jnp
from jax import lax
from jax.experimental import pallas as pl
from jax.experimental.pallas import tpu as pltpu

F32 = jnp.float32
BF16 = jnp.bfloat16

D_MODEL = 2048
SEQ = 8192
DEC_BATCH = 128
DEC_SEQ = 4
S_ROWS = DEC_BATCH * DEC_SEQ
PAST_LEN = 16384
ATT_HEADS = 16
ATT_KV_HEADS = 4
ATT_GROUP = 4
ATT_HEAD_DIM = 64
WINDOW = 128
ROPE_THETA = 500000.0
ROPE_DIM = 16
MLSTM_HEADS = 4
MLSTM_HEAD_DIM = 256
ATT_WIDTH = 1024
KV_WIDTH = 256
MLSTM_WIDTH = 1024
MAIN_WIDTH = ATT_WIDTH + 2 * KV_WIDTH + 4 * MLSTM_WIDTH
D_FF = 5632
N_MOD = 6
EPS = 1e-6

LANES = 128
MLSTM_CHUNK_P = 256
VMEM_LIMIT = 56 * 1024 * 1024

NT_DIMS = (((1,), (1,)), ((), ()))
TN_DIMS = (((0,), (0,)), ((), ()))
HI = lax.Precision.HIGHEST


def _cparams(sem):
    return pltpu.CompilerParams(dimension_semantics=sem, vmem_limit_bytes=VMEM_LIMIT)


def _dot(a, b):
    return jnp.dot(a, b, preferred_element_type=F32)


def _dot_nt(a, b):
    return lax.dot_general(a, b, NT_DIMS, preferred_element_type=F32)


def _dot_tn(a, b):
    return lax.dot_general(a, b, TN_DIMS, preferred_element_type=F32)


def _sigmoid(x):
    return 1.0 / (1.0 + jnp.exp(-x))


def _log_sigmoid(x):
    return jnp.minimum(x, 0.0) - jnp.log(1.0 + jnp.exp(-jnp.abs(x)))


def _mod_row(ref, per_row):
    return ref[...] if per_row else ref[0:1, :]


def _ada_kernel(c_ref, w_ref, b_ref, o_ref, s_scr):
    @pl.when(pl.program_id(0) == 0)
    def _():
        c = c_ref[...]
        s_scr[...] = (c * _sigmoid(c)).astype(BF16)

    o_ref[...] = _dot(s_scr[...], w_ref[...].astype(BF16)) + b_ref[...]


def _ada(c_all, w_ada, b_ada):
    m = c_all.shape[0]
    n = w_ada.shape[1]
    tn = 1024
    return pl.pallas_call(
        _ada_kernel,
        grid=(n // tn,),
        in_specs=[
            pl.BlockSpec((m, D_MODEL), lambda j: (0, 0)),
            pl.BlockSpec((D_MODEL, tn), lambda j: (0, j)),
            pl.BlockSpec((1, tn), lambda j: (0, j)),
        ],
        out_specs=pl.BlockSpec((m, tn), lambda j: (0, j)),
        out_shape=jax.ShapeDtypeStruct((m, n), F32),
        scratch_shapes=[pltpu.VMEM((m, D_MODEL), BF16)],
        compiler_params=_cparams(("arbitrary",)),
        name="ada",
    )(c_all, w_ada, b_ada)


def _rope_store(acc, cos, sa, sb, out_ref, ncols, scale):
    for c in range(ncols // LANES):
        xc = acc[:, LANES * c:LANES * (c + 1)]
        rot = xc * cos + pltpu.roll(xc, LANES - 8, 1) * sa + pltpu.roll(xc, 8, 1) * sb
        if scale != 1.0:
            rot = rot * scale
        out_ref[:, LANES * c:LANES * (c + 1)] = rot.astype(out_ref.dtype)


def _inproj_kernel(x_ref, sh_ref, sc_ref, nw_ref, wq_ref, win_ref, wg_ref, wgt_ref, cos_ref, sa_ref, sb_ref,
                   q_ref, kv_ref, kv32_ref, m_ref, g_ref, gt_ref, h_scr, *, per_row):
    j = pl.program_id(1)

    @pl.when(j == 0)
    def _():
        x = x_ref[...]
        r = lax.rsqrt(jnp.mean(x * x, axis=-1, keepdims=True) + EPS)
        y = x * r * nw_ref[...]
        h = (y * (1.0 + _mod_row(sc_ref, per_row)) + _mod_row(sh_ref, per_row)).astype(BF16)
        h_scr[...] = h
        g_ref[...] = _dot(h, wg_ref[...])
        gt_ref[...] = _dot_nt(wgt_ref[...], h)

    @pl.when(j < 2)
    def _():
        acc = _dot(h_scr[...], wq_ref[...])
        _rope_store(acc, cos_ref[...], sa_ref[...], sb_ref[...], q_ref, 512, ATT_HEAD_DIM ** -0.5)

    @pl.when(j == 2)
    def _():
        acc = _dot(h_scr[...], win_ref[...].astype(BF16))
        _rope_store(acc, cos_ref[...], sa_ref[...], sb_ref[...], kv32_ref, KV_WIDTH, 1.0)
        kv32_ref[:, KV_WIDTH:] = acc[:, KV_WIDTH:]
        kv_ref[...] = kv32_ref[...].astype(BF16)

    @pl.when(j > 2)
    def _():
        m_ref[...] = _dot(h_scr[...], win_ref[...].astype(BF16)).astype(BF16)


def _inproj(x, mod, mod_row_block, norm_w, wq, w_in, wg, wgt, cos, sa, sb, *, tm, per_row):
    rows = x.shape[0]
    tn = 512
    nj = MAIN_WIDTH // tn
    mod_rows = tm if per_row else 8
    mod_idx = (lambda i: i) if per_row else (lambda i: mod_row_block)
    kern = functools.partial(_inproj_kernel, per_row=per_row)
    return pl.pallas_call(
        kern,
        grid=(rows // tm, nj),
        in_specs=[
            pl.BlockSpec((tm, D_MODEL), lambda i, j: (i, 0)),
            pl.BlockSpec((mod_rows, D_MODEL), lambda i, j: (mod_idx(i), 0)),
            pl.BlockSpec((mod_rows, D_MODEL), lambda i, j: (mod_idx(i), 1)),
            pl.BlockSpec((1, D_MODEL), lambda i, j: (0, 0)),
            pl.BlockSpec((D_MODEL, tn), lambda i, j: (0, jnp.minimum(j, 1))),
            pl.BlockSpec((D_MODEL, tn), lambda i, j: (0, jnp.maximum(j, 2))),
            pl.BlockSpec((D_MODEL, LANES), lambda i, j: (0, 0)),
            pl.BlockSpec((16, D_MODEL), lambda i, j: (0, 0)),
            pl.BlockSpec((tm, LANES), lambda i, j: (i, 0)),
            pl.BlockSpec((tm, LANES), lambda i, j: (i, 0)),
            pl.BlockSpec((tm, LANES), lambda i, j: (i, 0)),
        ],
        out_specs=[
            pl.BlockSpec((tm, tn), lambda i, j: (i, jnp.minimum(j, 1))),
            pl.BlockSpec((tm, tn), lambda i, j: (i, 0)),
            pl.BlockSpec((tm, tn), lambda i, j: (0, 0)),
            pl.BlockSpec((tm, tn), lambda i, j: (i, jnp.clip(j - 3, 0, 7))),
            pl.BlockSpec((tm, LANES), lambda i, j: (i, 0)),
            pl.BlockSpec((16, tm), lambda i, j: (0, i)),
        ],
        out_shape=[
            jax.ShapeDtypeStruct((rows, ATT_WIDTH), BF16),
            jax.ShapeDtypeStruct((rows, 2 * KV_WIDTH), BF16),
            jax.ShapeDtypeStruct((tm, 2 * KV_WIDTH), F32),
            jax.ShapeDtypeStruct((rows, 4 * MLSTM_WIDTH), BF16),
            jax.ShapeDtypeStruct((rows, LANES), F32),
            jax.ShapeDtypeStruct((16, rows), F32),
        ],
        scratch_shapes=[pltpu.VMEM((tm, D_MODEL), BF16)],
        compiler_params=_cparams(("arbitrary", "arbitrary")),
        name="inproj_s" if per_row else "inproj_p",
    )(x, mod, mod, norm_w, wq, w_in, wg, wgt, cos, sa, sb)


def _attn_p_kernel(sink_ref, q_ref, kvp_ref, kvc_ref, o_ref):
    n = pl.program_id(0)
    w = WINDOW
    qi = lax.broadcasted_iota(jnp.int32, (w, 2 * w), 0)
    kj = lax.broadcasted_iota(jnp.int32, (w, 2 * w), 1)
    prev_off = jnp.where(n > 0, 0, 4 * w)
    allowed = ((kj < w) & (kj > qi + prev_off)) | ((kj >= w) & (kj - w <= qi))
    low = lax.broadcasted_iota(jnp.int32, (2 * w, LANES), 1) < ATT_HEAD_DIM
    kv2 = jnp.concatenate([kvp_ref[...], kvc_ref[...]], axis=0)
    zero = jnp.zeros((2 * w, LANES), BF16)
    for cp in range(2):
        k128 = kv2[:, LANES * cp:LANES * (cp + 1)]
        v128 = kv2[:, KV_WIDTH + LANES * cp:KV_WIDTH + LANES * (cp + 1)]
        kbd = jnp.concatenate([jnp.where(low, k128, zero), jnp.where(low, zero, k128)], axis=0)
        vbd = jnp.concatenate([jnp.where(low, v128, zero), jnp.where(low, zero, v128)], axis=0)
        for r in range(ATT_GROUP):
            c0 = 256 * r + LANES * cp
            s = _dot_nt(q_ref[:, c0:c0 + LANES], kbd)
            ps = []
            for half in range(2):
                sh = jnp.where(allowed, s[:, 2 * w * half:2 * w * (half + 1)], -jnp.inf)
                sink = sink_ref[(2 * cp + half) * ATT_GROUP + r]
                m = jnp.maximum(jnp.max(sh, axis=-1, keepdims=True), sink)
                e = jnp.exp(sh - m)
                l = jnp.sum(e, axis=-1, keepdims=True) + jnp.exp(sink - m)
                ps.append((e / l).astype(BF16))
            p = jnp.concatenate(ps, axis=1)
            o_ref[:, c0:c0 + LANES] = _dot(p, vbd).astype(BF16)


def _attn_p(sinks, q, kv):
    nb = SEQ // WINDOW
    return pl.pallas_call(
        _attn_p_kernel,
        grid=(nb,),
        in_specs=[
            pl.BlockSpec(memory_space=pltpu.SMEM),
            pl.BlockSpec((WINDOW, ATT_WIDTH), lambda n: (n, 0)),
            pl.BlockSpec((WINDOW, 2 * KV_WIDTH), lambda n: (jnp.maximum(n - 1, 0), 0)),
            pl.BlockSpec((WINDOW, 2 * KV_WIDTH), lambda n: (n, 0)),
        ],
        out_specs=pl.BlockSpec((WINDOW, ATT_WIDTH), lambda n: (n, 0)),
        out_shape=jax.ShapeDtypeStruct((SEQ, ATT_WIDTH), BF16),
        compiler_params=_cparams(("arbitrary",)),
        name="attn_p",
    )(sinks, q, kv, kv)


ATT_S_BB = 8


def _attn_s_kernel(sink_ref, q_ref, kv32_ref, ck_ref, cv_ref, o_ref, ko_ref, vo_ref, q32_scr):
    t_new = DEC_SEQ
    w = WINDOW
    q32_scr[...] = q_ref[...].astype(F32)
    rows = 4 * 4 * 8
    row = lax.broadcasted_iota(jnp.int32, (rows, w), 0)
    slot = lax.broadcasted_iota(jnp.int32, (rows, w), 1)
    t_row = row % t_new
    second = (row % 8) >= t_new
    win_ok = (slot < w - t_new) | (slot - (w - t_new) <= t_row)
    old_ok = (slot >= 1) & (slot < t_new) & (slot > t_row)
    lane256 = lax.broadcasted_iota(jnp.int32, (32, 2 * LANES), 1)
    sink = sink_ref[...][:, 0:1]
    for b in range(ATT_S_BB):
        ko_ref[b, 0:w - t_new, :] = ck_ref[b, t_new:w, :]
        vo_ref[b, 0:w - t_new, :] = cv_ref[b, t_new:w, :]
        ko_ref[b, w - t_new:w, :] = kv32_ref[t_new * b:t_new * (b + 1), 0:KV_WIDTH]
        vo_ref[b, w - t_new:w, :] = kv32_ref[t_new * b:t_new * (b + 1), KV_WIDTH:]
    for pair in range(ATT_S_BB // 2):
        b0, b1 = 2 * pair, 2 * pair + 1
        q32 = jnp.concatenate([q32_scr[8 * pair:8 * (pair + 1), 256 * r:256 * (r + 1)] for r in range(ATT_GROUP)],
                              axis=0)
        qpad = jnp.concatenate(
            [jnp.where((lane256 // ATT_HEAD_DIM) == g, q32, 0.0) for g in range(ATT_KV_HEADS)], axis=0).astype(BF16)
        kw = [ko_ref[b].astype(BF16) for b in (b0, b1)]
        vw = [vo_ref[b].astype(BF16) for b in (b0, b1)]
        kc = [ck_ref[b].astype(BF16) for b in (b0, b1)]
        vc = [cv_ref[b].astype(BF16) for b in (b0, b1)]
        s_w = jnp.where(second, _dot_nt(qpad, kw[1]), _dot_nt(qpad, kw[0]))
        s_c = jnp.where(second, _dot_nt(qpad, kc[1]), _dot_nt(qpad, kc[0]))
        s_w = jnp.where(win_ok, s_w, -jnp.inf)
        s_c = jnp.where(old_ok, s_c, -jnp.inf)
        m = jnp.maximum(jnp.maximum(jnp.max(s_w, axis=-1, keepdims=True), jnp.max(s_c, axis=-1, keepdims=True)), sink)
        e_w = jnp.exp(s_w - m)
        e_c = jnp.exp(s_c - m)
        l = jnp.sum(e_w, axis=-1, keepdims=True) + jnp.sum(e_c, axis=-1, keepdims=True) + jnp.exp(sink - m)
        p_w = e_w / l
        p_c = e_c / l
        zero = jnp.zeros_like(p_w)
        o = (_dot(jnp.where(second, zero, p_w).astype(BF16), vw[0]) + _dot(jnp.where(second, p_w, zero).astype(BF16), vw[1])
             + _dot(jnp.where(second, zero, p_c).astype(BF16), vc[0]) + _dot(jnp.where(second, p_c, zero).astype(BF16), vc[1]))
        o32 = jnp.zeros((32, 2 * LANES), F32)
        for g in range(ATT_KV_HEADS):
            o32 = jnp.where((lane256 // ATT_HEAD_DIM) == g, o[32 * g:32 * (g + 1), :], o32)
        for r in range(ATT_GROUP):
            o_ref[8 * pair:8 * (pair + 1), 256 * r:256 * (r + 1)] = o32[8 * r:8 * (r + 1), :]


def _attn_s(sink_col, q, kv32, ck, cv):
    bb = ATT_S_BB
    rows = bb * DEC_SEQ
    return pl.pallas_call(
        _attn_s_kernel,
        grid=(DEC_BATCH // bb,),
        in_specs=[
            pl.BlockSpec((128, LANES), lambda i: (0, 0)),
            pl.BlockSpec((rows, ATT_WIDTH), lambda i: (i, 0)),
            pl.BlockSpec((rows, 2 * KV_WIDTH), lambda i: (i, 0)),
            pl.BlockSpec((bb, WINDOW, KV_WIDTH), lambda i: (i, 0, 0)),
            pl.BlockSpec((bb, WINDOW, KV_WIDTH), lambda i: (i, 0, 0)),
        ],
        out_specs=[
            pl.BlockSpec((rows, ATT_WIDTH), lambda i: (i, 0)),
            pl.BlockSpec((bb, WINDOW, KV_WIDTH), lambda i: (i, 0, 0)),
            pl.BlockSpec((bb, WINDOW, KV_WIDTH), lambda i: (i, 0, 0)),
        ],
        out_shape=[
            jax.ShapeDtypeStruct((S_ROWS, ATT_WIDTH), F32),
            jax.ShapeDtypeStruct((DEC_BATCH, WINDOW, KV_WIDTH), F32),
            jax.ShapeDtypeStruct((DEC_BATCH, WINDOW, KV_WIDTH), F32),
        ],
        scratch_shapes=[pltpu.VMEM((rows, ATT_WIDTH), F32)],
        compiler_params=_cparams(("arbitrary",)),
        name="attn_s",
    )(sink_col, q, kv32, ck, cv)


def _head_norm_gate(h, nw, mo):
    hn = h * lax.rsqrt(jnp.mean(h * h, axis=-1, keepdims=True) + EPS) * nw
    return hn * _sigmoid(mo.astype(F32))


def _mlstm_p_kernel(q_ref, k_ref, v_ref, mo_ref, g_ref, gt_ref, brow_ref, bcol_ref, nw_ref,
                    mh_ref, c_out, n_out, m_out, c_scr, n_scr, m_scr):
    c = pl.program_id(0)
    L = MLSTM_CHUNK_P
    dh = MLSTM_HEAD_DIM

    @pl.when(c == 0)
    def _():
        c_scr[...] = jnp.zeros_like(c_scr)
        n_scr[...] = jnp.zeros_like(n_scr)
        m_scr[...] = jnp.zeros_like(m_scr)

    ti = lax.broadcasted_iota(jnp.int32, (L, L), 0)
    si = lax.broadcasted_iota(jnp.int32, (L, L), 1)
    causal = si <= ti
    tri = causal.astype(F32)
    tri_t = (ti <= si).astype(F32)
    gates = g_ref[...] + brow_ref[...]
    gates_t = gt_ref[...] + bcol_ref[...]
    b_col = jnp.dot(tri, _log_sigmoid(gates), precision=HI, preferred_element_type=F32)
    b_row = jnp.dot(_log_sigmoid(gates_t), tri_t, precision=HI, preferred_element_type=F32)
    for hd in range(MLSTM_HEADS):
        cs = slice(dh * hd, dh * (hd + 1))
        b_c = b_col[:, MLSTM_HEADS + hd:MLSTM_HEADS + hd + 1]
        li_c = gates[:, hd:hd + 1]
        b_r = b_row[MLSTM_HEADS + hd:MLSTM_HEADS + hd + 1, :]
        li_r = gates_t[hd:hd + 1, :]
        dm = jnp.where(causal, b_c - b_r + li_r, -jnp.inf)
        m_prev = m_scr[hd:hd + 1, 0:1]
        m_inter = b_c + m_prev
        m_t = jnp.maximum(m_inter, jnp.max(dm, axis=-1, keepdims=True))
        q = q_ref[:, cs] * (dh ** -0.5)
        k = k_ref[:, cs]
        v = v_ref[:, cs]
        sm = _dot_nt(q, k) * jnp.exp(dm - m_t)
        a = jnp.exp(m_inter - m_t)
        c_old = c_scr[hd]
        n_old = n_scr[hd:hd + 1, :]
        num = a * _dot_nt(q, c_old.astype(BF16)) + _dot(sm.astype(BF16), v)
        qn = jnp.sum(q.astype(F32) * n_old, axis=-1, keepdims=True)
        den = a * qn + jnp.sum(sm, axis=-1, keepdims=True)
        h = num / jnp.maximum(jnp.abs(den), jnp.exp(-m_t))
        mh_ref[:, cs] = _head_norm_gate(h, nw_ref[:, cs], mo_ref[:, cs]).astype(BF16)
        m_new = m_t[L - 1:L, :]
        b_last = b_c[L - 1:L, :]
        g = jnp.exp(b_last - b_c + li_c - m_new)
        decay = jnp.exp(b_last + m_prev - m_new)
        gv = (g * v.astype(F32)).astype(BF16)
        c_scr[hd] = decay * c_old + _dot_tn(gv, k)
        n_scr[hd:hd + 1, :] = decay * n_old + jnp.sum(g * k.astype(F32), axis=0, keepdims=True)
        m_scr[hd:hd + 1, :] = jnp.broadcast_to(m_new, (1, LANES))

    @pl.when(c == pl.num_programs(0) - 1)
    def _():
        c_out[...] = c_scr[...]
        n_out[...] = n_scr[...]
        m_out[...] = m_scr[...]


def _mlstm_p(m_all, gates, gates_t, brow, bcol, nw):
    L = MLSTM_CHUNK_P
    nc = SEQ // L
    dh = MLSTM_HEAD_DIM
    return pl.pallas_call(
        _mlstm_p_kernel,
        grid=(nc,),
        in_specs=[
            pl.BlockSpec((L, MLSTM_WIDTH), lambda c: (c, 0)),
            pl.BlockSpec((L, MLSTM_WIDTH), lambda c: (c, 1)),
            pl.BlockSpec((L, MLSTM_WIDTH), lambda c: (c, 2)),
            pl.BlockSpec((L, MLSTM_WIDTH), lambda c: (c, 3)),
            pl.BlockSpec((L, LANES), lambda c: (c, 0)),
            pl.BlockSpec((16, L), lambda c: (0, c)),
            pl.BlockSpec((1, LANES), lambda c: (0, 0)),
            pl.BlockSpec((16, L), lambda c: (0, 0)),
            pl.BlockSpec((1, MLSTM_WIDTH), lambda c: (0, 0)),
        ],
        out_specs=[
            pl.BlockSpec((L, MLSTM_WIDTH), lambda c: (c, 0)),
            pl.BlockSpec((MLSTM_HEADS, dh, dh), lambda c: (0, 0, 0)),
            pl.BlockSpec((8, dh), lambda c: (0, 0)),
            pl.BlockSpec((8, LANES), lambda c: (0, 0)),
        ],
        out_shape=[
            jax.ShapeDtypeStruct((SEQ, MLSTM_WIDTH), BF16),
            jax.ShapeDtypeStruct((MLSTM_HEADS, dh, dh), F32),
            jax.ShapeDtypeStruct((8, dh), F32),
            jax.ShapeDtypeStruct((8, LANES), F32),
        ],
        scratch_shapes=[
            pltpu.VMEM((MLSTM_HEADS, dh, dh), F32),
            pltpu.VMEM((8, dh), F32),
            pltpu.VMEM((8, LANES), F32),
        ],
        compiler_params=_cparams(("arbitrary",)),
        name="mlstm_p",
    )(m_all, m_all, m_all, m_all, gates, gates_t, brow, bcol, nw)


MLSTM_S_BB = 4


def _mlstm_s_kernel(q_ref, k_ref, v_ref, mo_ref, g_ref, nrep_ref, mrep_ref, brow_ref, nw_ref, c_ref,
                    mh_ref, c_out, nrow_ref, mrow_ref):
    T = DEC_SEQ
    R = MLSTM_S_BB * T
    H = MLSTM_HEADS
    dh = MLSTM_HEAD_DIM

    def shift(x, d):
        return pltpu.roll(x, d, 0)

    lanes = lax.broadcasted_iota(jnp.int32, (R, LANES), 1)
    tmod = lax.broadcasted_iota(jnp.int32, (R, LANES), 0) % T
    tmod_w = lax.broadcasted_iota(jnp.int32, (R, MLSTM_WIDTH), 0) % T
    head_ok = lanes < H
    gates = g_ref[...] + brow_ref[...]
    li = jnp.where(head_ok, gates, 0.0)
    lf = jnp.where(head_ok, pltpu.roll(_log_sigmoid(gates), LANES - H, 1), 0.0)
    bcum = lf
    for d in range(1, T):
        bcum = bcum + jnp.where(tmod >= d, shift(lf, d), 0.0)
    m0 = mrep_ref[...]
    m_inter = bcum + m0
    dms = [li] + [jnp.where(tmod >= d, bcum - shift(bcum, d) + shift(li, d), -jnp.inf) for d in range(1, T)]
    m_t = m_inter
    for dm in dms:
        m_t = jnp.maximum(m_t, dm)
    a = jnp.exp(m_inter - m_t)
    ws = [jnp.exp(dm - m_t) for dm in dms]

    q_bf = q_ref[...] * (dh ** -0.5)
    q = q_bf.astype(F32)
    k = k_ref[...].astype(F32)
    v = v_ref[...].astype(F32)
    seg = (lax.broadcasted_iota(jnp.int32, (MLSTM_WIDTH, LANES), 0) // dh
           == lax.broadcasted_iota(jnp.int32, (MLSTM_WIDTH, LANES), 1)).astype(F32)
    ex = (lax.broadcasted_iota(jnp.int32, (LANES, MLSTM_WIDTH), 1) // dh
          == lax.broadcasted_iota(jnp.int32, (LANES, MLSTM_WIDTH), 0)).astype(F32)

    def segsum(x):
        return jnp.dot(x, seg, precision=HI, preferred_element_type=F32)

    def expand(x):
        return jnp.dot(x, ex, precision=HI, preferred_element_type=F32)

    ks = [k] + [shift(k, d) for d in range(1, T)]
    vs = [v] + [shift(v, d) for d in range(1, T)]
    sms = [segsum(q * ks[d]) * ws[d] for d in range(T)]
    den = a * segsum(q * nrep_ref[...])
    for sm in sms:
        den = den + sm
    inv = 1.0 / jnp.maximum(jnp.abs(den), jnp.exp(-m_t))

    def last(x):
        out = jnp.zeros_like(x)
        for jj in range(T):
            out = jnp.where(tmod == T - 1 - jj, x if jj == 0 else pltpu.roll(x, R - jj, 0), out)
        return out

    m_new = last(m_t)
    b_last = last(bcum)
    g = jnp.where(head_ok, jnp.exp(b_last - bcum + li - m_new), 0.0)
    decay = jnp.where(head_ok, jnp.exp(b_last + m0 - m_new), 0.0)
    a_f = expand(a * inv)
    w_f = [expand(sm * inv) for sm in sms]
    g_f = expand(g)
    d_f = expand(decay)
    gv = (g_f * v).astype(BF16)
    rowb = lax.broadcasted_iota(jnp.int32, (R, dh), 0) // T
    for hd in range(H):
        cs = slice(dh * hd, dh * (hd + 1))
        qh = q_bf[:, cs]
        kh = k_ref[:, cs]
        gvh = gv[:, cs]
        qc = jnp.zeros((R, dh), F32)
        for bb in range(MLSTM_S_BB):
            c_old = c_ref[bb, hd]
            qc = jnp.where(rowb == bb, _dot_nt(qh, c_old.astype(BF16)), qc)
            upd = _dot_tn(jnp.where(rowb == bb, gvh, jnp.zeros_like(gvh)), kh)
            c_out[bb, hd] = d_f[T * bb:T * bb + 1, cs] * c_old + upd
        h = a_f[:, cs] * qc
        for d in range(T):
            h = h + w_f[d][:, cs] * vs[d][:, cs]
        mh_ref[:, cs] = _head_norm_gate(h, nw_ref[:, cs], mo_ref[:, cs]).astype(BF16)
    gk = g_f * k
    nsum = gk
    for d in range(1, T):
        nsum = nsum + jnp.where(tmod_w >= d, shift(gk, d), 0.0)
    nrow_ref[...] = d_f * nrep_ref[...] + nsum
    mrow_ref[...] = m_t


def _mlstm_s(m_all, gates, n_rep, m_rep, brow, nw, state_c):
    bb = MLSTM_S_BB
    R = bb * DEC_SEQ
    dh = MLSTM_HEAD_DIM
    H = MLSTM_HEADS
    return pl.pallas_call(
        _mlstm_s_kernel,
        grid=(DEC_BATCH // bb,),
        in_specs=[
            pl.BlockSpec((R, MLSTM_WIDTH), lambda i: (i, 0)),
            pl.BlockSpec((R, MLSTM_WIDTH), lambda i: (i, 1)),
            pl.BlockSpec((R, MLSTM_WIDTH), lambda i: (i, 2)),
            pl.BlockSpec((R, MLSTM_WIDTH), lambda i: (i, 3)),
            pl.BlockSpec((R, LANES), lambda i: (i, 0)),
            pl.BlockSpec((R, MLSTM_WIDTH), lambda i: (i, 0)),
            pl.BlockSpec((R, LANES), lambda i: (i, 0)),
            pl.BlockSpec((1, LANES), lambda i: (0, 0)),
            pl.BlockSpec((1, MLSTM_WIDTH), lambda i: (0, 0)),
            pl.BlockSpec((bb, H, dh, dh), lambda i: (i, 0, 0, 0)),
        ],
        out_specs=[
            pl.BlockSpec((R, MLSTM_WIDTH), lambda i: (i, 0)),
            pl.BlockSpec((bb, H, dh, dh), lambda i: (i, 0, 0, 0)),
            pl.BlockSpec((R, MLSTM_WIDTH), lambda i: (i, 0)),
            pl.BlockSpec((R, LANES), lambda i: (i, 0)),
        ],
        out_shape=[
            jax.ShapeDtypeStruct((S_ROWS, MLSTM_WIDTH), BF16),
            jax.ShapeDtypeStruct((DEC_BATCH, H, dh, dh), F32),
            jax.ShapeDtypeStruct((S_ROWS, MLSTM_WIDTH), F32),
            jax.ShapeDtypeStruct((S_ROWS, LANES), F32),
        ],
        compiler_params=_cparams(("arbitrary",)),
        name="mlstm_s",
    )(m_all, m_all, m_all, m_all, gates, n_rep, m_rep, brow, nw, state_c)


def _outproj_kernel(att_ref, mh_ref, x_ref, g1_ref, sh_ref, sc_ref, nw_ref, w_ref, x1_ref, h2_ref, *, per_row):
    y = _dot(att_ref[...].astype(BF16), w_ref[0:ATT_WIDTH, :]) + _dot(mh_ref[...], w_ref[ATT_WIDTH:, :])
    x1 = x_ref[...] + _mod_row(g1_ref, per_row) * y
    x1_ref[...] = x1
    r = lax.rsqrt(jnp.mean(x1 * x1, axis=-1, keepdims=True) + EPS)
    h2 = (x1 * r * nw_ref[...]) * (1.0 + _mod_row(sc_ref, per_row)) + _mod_row(sh_ref, per_row)
    h2_ref[...] = h2.astype(BF16)


def _outproj(att, mh, x, mod, mod_row_block, norm_w, w_out, *, tm, per_row):
    rows = x.shape[0]
    mod_rows = tm if per_row else 8
    mod_idx = (lambda i: i) if per_row else (lambda i: mod_row_block)
    kern = functools.partial(_outproj_kernel, per_row=per_row)
    return pl.pallas_call(
        kern,
        grid=(rows // tm,),
        in_specs=[
            pl.BlockSpec((tm, ATT_WIDTH), lambda i: (i, 0)),
            pl.BlockSpec((tm, MLSTM_WIDTH), lambda i: (i, 0)),
            pl.BlockSpec((tm, D_MODEL), lambda i: (i, 0)),
            pl.BlockSpec((mod_rows, D_MODEL), lambda i: (mod_idx(i), 2)),
            pl.BlockSpec((mod_rows, D_MODEL), lambda i: (mod_idx(i), 3)),
            pl.BlockSpec((mod_rows, D_MODEL), lambda i: (mod_idx(i), 4)),
            pl.BlockSpec((1, D_MODEL), lambda i: (0, 0)),
            pl.BlockSpec((D_MODEL, D_MODEL), lambda i: (0, 0)),
        ],
        out_specs=[
            pl.BlockSpec((tm, D_MODEL), lambda i: (i, 0)),
            pl.BlockSpec((tm, D_MODEL), lambda i: (i, 0)),
        ],
        out_shape=[
            jax.ShapeDtypeStruct((rows, D_MODEL), F32),
            jax.ShapeDtypeStruct((rows, D_MODEL), BF16),
        ],
        compiler_params=_cparams(("arbitrary",)),
        name="outproj_s" if per_row else "outproj_p",
    )(att, mh, x, mod, mod, mod, norm_w, w_out)


def _ffn_a_kernel(h_ref, wg_ref, wu_ref, a_ref):
    h = h_ref[...]
    g = _dot(h, wg_ref[...].astype(BF16))
    u = _dot(h, wu_ref[...].astype(BF16))
    a_ref[...] = (g * _sigmoid(g) * u).astype(BF16)


def _ffn_a(h2, w_gate, w_up, *, tm):
    rows = h2.shape[0]
    tn = 512
    return pl.pallas_call(
        _ffn_a_kernel,
        grid=(rows // tm, D_FF // tn),
        in_specs=[
            pl.BlockSpec((tm, D_MODEL), lambda i, j: (i, 0)),
            pl.BlockSpec((D_MODEL, tn), lambda i, j: (0, j)),
            pl.BlockSpec((D_MODEL, tn), lambda i, j: (0, j)),
        ],
        out_specs=pl.BlockSpec((tm, tn), lambda i, j: (i, j)),
        out_shape=jax.ShapeDtypeStruct((rows, D_FF), BF16),
        compiler_params=_cparams(("arbitrary", "arbitrary")),
        name="ffn_a",
    )(h2, w_gate, w_up)


def _ffn_b_kernel(a_ref, wd_ref, x1_ref, g2_ref, fw_ref, y_ref, *, per_row):
    kk = pl.program_id(1)
    p = _dot(a_ref[...], wd_ref[...].astype(BF16))

    @pl.when(kk == 0)
    def _():
        y_ref[...] = p

    @pl.when(kk > 0)
    def _():
        y_ref[...] += p

    @pl.when(kk == pl.num_programs(1) - 1)
    def _():
        x2 = x1_ref[...] + _mod_row(g2_ref, per_row) * y_ref[...]
        y_ref[...] = x2 * lax.rsqrt(jnp.mean(x2 * x2, axis=-1, keepdims=True) + EPS) * fw_ref[...]


def _ffn_b(a, w_down, x1, mod, mod_row_block, final_w, *, tm, per_row):
    rows = a.shape[0]
    tk = 512
    mod_rows = tm if per_row else 8
    mod_idx = (lambda i: i) if per_row else (lambda i: mod_row_block)
    kern = functools.partial(_ffn_b_kernel, per_row=per_row)
    return pl.pallas_call(
        kern,
        grid=(rows // tm, D_FF // tk),
        in_specs=[
            pl.BlockSpec((tm, tk), lambda i, k: (i, k)),
            pl.BlockSpec((tk, D_MODEL), lambda i, k: (k, 0)),
            pl.BlockSpec((tm, D_MODEL), lambda i, k: (i, 0), pipeline_mode=pl.Buffered(1)),
            pl.BlockSpec((mod_rows, D_MODEL), lambda i, k: (mod_idx(i), 5)),
            pl.BlockSpec((1, D_MODEL), lambda i, k: (0, 0)),
        ],
        out_specs=pl.BlockSpec((tm, D_MODEL), lambda i, k: (i, 0)),
        out_shape=jax.ShapeDtypeStruct((rows, D_MODEL), F32),
        compiler_params=_cparams(("arbitrary", "arbitrary")),
        name="ffn_b_s" if per_row else "ffn_b_p",
    )(a, w_down, x1, mod, final_w)


def _rope_tables(pos):
    half = ROPE_DIM // 2
    inv = ROPE_THETA ** (-jnp.arange(0, ROPE_DIM, 2, dtype=F32) / ROPE_DIM)
    ang = pos.astype(F32)[:, None] * inv[None, :]
    cos, sin = jnp.cos(ang), jnp.sin(ang)
    n = pos.shape[0]
    ones = jnp.ones((n, ATT_HEAD_DIM - ROPE_DIM), F32)
    zeros = jnp.zeros((n, ATT_HEAD_DIM - ROPE_DIM), F32)
    zh = jnp.zeros((n, half), F32)
    c64 = jnp.concatenate([cos, cos, ones], axis=1)
    a64 = jnp.concatenate([-sin, zh, zeros], axis=1)
    b64 = jnp.concatenate([zh, sin, zeros], axis=1)
    return tuple(jnp.tile(t, (1, LANES // ATT_HEAD_DIM)) for t in (c64, a64, b64))


def kernel(x_prompt, x_sample, cache_k_win, cache_v_win, state_C, state_n, state_m, c_prompt, c_sample,
           norm1_w, norm2_w, final_norm_w, w_ada, b_ada, w_in, b_ig, b_fg, attn_sinks, mh_norm_w,
           w_out, w_gate, w_up, w_down):
    assert w_in.shape[0] == 1, "single-layer trunk"
    T = DEC_SEQ
    xp = x_prompt[0]
    xs = x_sample.reshape(S_ROWS, D_MODEL)

    c_all = jnp.concatenate([jnp.repeat(c_sample, T, axis=0), c_prompt, jnp.zeros((15, D_MODEL), F32)], axis=0)
    mod = _ada(c_all, w_ada[0], b_ada)
    prompt_mod_block = S_ROWS // 8

    w_in0 = w_in[0]
    wq = (w_in0[:, :ATT_WIDTH].reshape(D_MODEL, ATT_KV_HEADS, ATT_GROUP, ATT_HEAD_DIM)
          .transpose(0, 2, 1, 3).reshape(D_MODEL, ATT_WIDTH).astype(BF16))
    w_gates = w_in0[:, MAIN_WIDTH:]
    wg = jnp.pad(w_gates, ((0, 0), (0, LANES - 2 * MLSTM_HEADS))).astype(BF16)
    wgt = jnp.pad(w_gates.T, ((0, 16 - 2 * MLSTM_HEADS), (0, 0))).astype(BF16)
    w_out0 = w_out[0]
    w_out_att = (w_out0[:ATT_WIDTH].reshape(ATT_KV_HEADS, ATT_GROUP, ATT_HEAD_DIM, D_MODEL)
                 .transpose(1, 0, 2, 3).reshape(ATT_WIDTH, D_MODEL))
    w_out_bf = jnp.concatenate([w_out_att, w_out0[ATT_WIDTH:]], axis=0).astype(BF16)
    n1 = norm1_w.reshape(1, D_MODEL)
    n2 = norm2_w.reshape(1, D_MODEL)
    fw = final_norm_w.reshape(1, D_MODEL)
    nw = mh_norm_w.reshape(1, MLSTM_WIDTH)
    gate_bias = jnp.concatenate([b_ig[0], b_fg[0]])
    brow = jnp.pad(gate_bias, (0, LANES - 2 * MLSTM_HEADS)).reshape(1, LANES)
    bcol = jnp.broadcast_to(jnp.pad(gate_bias, (0, 16 - 2 * MLSTM_HEADS))[:, None], (16, MLSTM_CHUNK_P))

    rope_p = _rope_tables(jnp.arange(SEQ, dtype=jnp.int32))
    rope_s = _rope_tables(jnp.tile(PAST_LEN + jnp.arange(T, dtype=jnp.int32), DEC_BATCH))

    tm_p = 1024
    q_p, kv_p, kv32_p, m_p, g_p, gt_p = _inproj(xp, mod, prompt_mod_block, n1, wq, w_in0, wg, wgt, *rope_p,
                                                tm=tm_p, per_row=False)
    q_s, kv_s, kv32_s, m_s, g_s, _ = _inproj(xs, mod, 0, n1, wq, w_in0, wg, wgt, *rope_s,
                                              tm=S_ROWS, per_row=True)
    del kv_s

    sinks = attn_sinks[0]
    att_p = _attn_p(sinks, q_p, kv_p)
    sink_col = jnp.broadcast_to(sinks.reshape(ATT_HEADS, 1, 1), (ATT_HEADS, 8, LANES)).reshape(128, LANES)
    ck = cache_k_win.reshape(DEC_BATCH, WINDOW, KV_WIDTH)
    cv = cache_v_win.reshape(DEC_BATCH, WINDOW, KV_WIDTH)
    att_s, kwin_s, vwin_s = _attn_s(sink_col, q_s, kv32_s, ck, cv)

    mh_p, c_p, n_p, mm_p = _mlstm_p(m_p, g_p, gt_p, brow, bcol, nw)
    n_rep = jnp.repeat(state_n[0].reshape(DEC_BATCH, MLSTM_WIDTH), T, axis=0)
    m_rep = jnp.pad(jnp.repeat(state_m[0], T, axis=0), ((0, 0), (0, LANES - MLSTM_HEADS)))
    mh_s, c_s, nrow_s, mrow_s = _mlstm_s(m_s, g_s, n_rep, m_rep, brow, nw, state_C[0])

    x1_p, h2_p = _outproj(att_p, mh_p, xp, mod, prompt_mod_block, n2, w_out_bf, tm=512, per_row=False)
    x1_s, h2_s = _outproj(att_s, mh_s, xs, mod, 0, n2, w_out_bf, tm=S_ROWS, per_row=True)
    a_p = _ffn_a(h2_p, w_gate[0], w_up[0], tm=tm_p)
    a_s = _ffn_a(h2_s, w_gate[0], w_up[0], tm=S_ROWS)
    y_p = _ffn_b(a_p, w_down[0], x1_p, mod, prompt_mod_block, fw, tm=tm_p, per_row=False)
    y_s = _ffn_b(a_s, w_down[0], x1_s, mod, 0, fw, tm=S_ROWS, per_row=True)

    kv_shape = (1, 1, WINDOW, ATT_KV_HEADS, ATT_HEAD_DIM)
    kv_last = kv32_p[tm_p - WINDOW:]
    dh = MLSTM_HEAD_DIM
    return (
        y_p.reshape(1, SEQ, D_MODEL),
        y_s.reshape(DEC_BATCH, T, D_MODEL),
        kv_last[:, :KV_WIDTH].reshape(kv_shape),
        kv_last[:, KV_WIDTH:].reshape(kv_shape),
        c_p.reshape(1, 1, MLSTM_HEADS, dh, dh),
        n_p[:MLSTM_HEADS].reshape(1, 1, MLSTM_HEADS, dh),
        mm_p[:MLSTM_HEADS, 0].reshape(1, 1, MLSTM_HEADS),
        kwin_s.reshape(1, DEC_BATCH, WINDOW, ATT_KV_HEADS, ATT_HEAD_DIM),
        vwin_s.reshape(1, DEC_BATCH, WINDOW, ATT_KV_HEADS, ATT_HEAD_DIM),
        c_s.reshape(1, DEC_BATCH, MLSTM_HEADS, dh, dh),
        nrow_s[T - 1::T].reshape(1, DEC_BATCH, MLSTM_HEADS, dh),
        mrow_s[T - 1::T, :MLSTM_HEADS].reshape(1, DEC_BATCH, MLSTM_HEADS),
    )
```

```python
import functools

import jax
import jax.numpy as jnp
from jax import lax
from jax.experimental import pallas as pl
from jax.experimental.pallas import tpu as pltpu

F32 = jnp.float32
BF16 = jnp.bfloat16

D_MODEL = 2048
SEQ = 8192
DEC_BATCH = 128
DEC_SEQ = 4
S_ROWS = DEC_BATCH * DEC_SEQ
PAST_LEN = 16384
ATT_HEADS = 16
ATT_KV_HEADS = 4
ATT_GROUP = 4
ATT_HEAD_DIM = 64
WINDOW = 128
ROPE_THETA = 500000.0
ROPE_DIM = 16
MLSTM_HEADS = 4
MLSTM_HEAD_DIM = 256
ATT_WIDTH = 1024
KV_WIDTH = 256
MLSTM_WIDTH = 1024
MAIN_WIDTH = ATT_WIDTH + 2 * KV_WIDTH + 4 * MLSTM_WIDTH
D_FF = 5632
N_MOD = 6
EPS = 1e-6

LANES = 128
MLSTM_CHUNK_P = 256
VMEM_LIMIT = 56 * 1024 * 1024

NT_DIMS = (((1,), (1,)), ((), ()))
TN_DIMS = (((0,), (0,)), ((), ()))
HI = lax.Precision.HIGHEST


def _cparams(sem):
    return pltpu.CompilerParams(dimension_semantics=sem, vmem_limit_bytes=VMEM_LIMIT)


def _dot(a, b):
    return jnp.dot(a, b, preferred_element_type=F32)


def _dot_nt(a, b):
    return lax.dot_general(a, b, NT_DIMS, preferred_element_type=F32)


def _dot_tn(a, b):
    return lax.dot_general(a, b, TN_DIMS, preferred_element_type=F32)


def _sigmoid(x):
    return 1.0 / (1.0 + jnp.exp(-x))


def _log_sigmoid(x):
    return jnp.minimum(x, 0.0) - jnp.log(1.0 + jnp.exp(-jnp.abs(x)))


def _mod_row(ref, per_row):
    return ref[...] if per_row else ref[0:1, :]


def _ada_kernel(c_ref, w_ref, b_ref, o_ref, s_scr):
    @pl.when(pl.program_id(0) == 0)
    def _():
        c = c_ref[...]
        s_scr[...] = (c * _sigmoid(c)).astype(BF16)

    o_ref[...] = _dot(s_scr[...], w_ref[...].astype(BF16)) + b_ref[...]


def _ada(c_all, w_ada, b_ada):
    m = c_all.shape[0]
    n = w_ada.shape[1]
    tn = 1024
    return pl.pallas_call(
        _ada_kernel,
        grid=(n // tn,),
        in_specs=[
            pl.BlockSpec((m, D_MODEL), lambda j: (0, 0)),
            pl.BlockSpec((D_MODEL, tn), lambda j: (0, j)),
            pl.BlockSpec((1, tn), lambda j: (0, j)),
        ],
        out_specs=pl.BlockSpec((m, tn), lambda j: (0, j)),
        out_shape=jax.ShapeDtypeStruct((m, n), F32),
        scratch_shapes=[pltpu.VMEM((m, D_MODEL), BF16)],
        compiler_params=_cparams(("arbitrary",)),
        name="ada",
    )(c_all, w_ada, b_ada)


def _rope_store(acc, cos, sa, sb, out_ref, ncols, scale):
    for c in range(ncols // LANES):
        xc = acc[:, LANES * c:LANES * (c + 1)]
        rot = xc * cos + pltpu.roll(xc, LANES - 8, 1) * sa + pltpu.roll(xc, 8, 1) * sb
        if scale != 1.0:
            rot = rot * scale
        out_ref[:, LANES * c:LANES * (c + 1)] = rot.astype(out_ref.dtype)


def _inproj_kernel(x_ref, sh_ref, sc_ref, nw_ref, wq_ref, win_ref, wg_ref, wgt_ref, cos_ref, sa_ref, sb_ref,
                   q_ref, kv_ref, kv32_ref, m_ref, g_ref, gt_ref, h_scr, *, per_row):
    j = pl.program_id(1)

    @pl.when(j == 0)
    def _():
        x = x_ref[...]
        r = lax.rsqrt(jnp.mean(x * x, axis=-1, keepdims=True) + EPS)
        y = x * r * nw_ref[...]
        h = (y * (1.0 + _mod_row(sc_ref, per_row)) + _mod_row(sh_ref, per_row)).astype(BF16)
        h_scr[...] = h
        g_ref[...] = _dot(h, wg_ref[...])
        gt_ref[...] = _dot_nt(wgt_ref[...], h)

    @pl.when(j < 2)
    def _():
        acc = _dot(h_scr[...], wq_ref[...])
        _rope_store(acc, cos_ref[...], sa_ref[...], sb_ref[...], q_ref, 512, ATT_HEAD_DIM ** -0.5)

    @pl.when(j == 2)
    def _():
        acc = _dot(h_scr[...], win_ref[...].astype(BF16))
        _rope_store(acc, cos_ref[...], sa_ref[...], sb_ref[...], kv32_ref, KV_WIDTH, 1.0)
        kv32_ref[:, KV_WIDTH:] = acc[:, KV_WIDTH:]
        kv_ref[...] = kv32_ref[...].astype(BF16)

    @pl.when(j > 2)
    def _():
        m_ref[...] = _dot(h_scr[...], win_ref[...].astype(BF16)).astype(BF16)


def _inproj(x, mod, mod_row_block, norm_w, wq, w_in, wg, wgt, cos, sa, sb, *, tm, per_row):
    rows = x.shape[0]
    tn = 512
    nj = MAIN_WIDTH // tn
    mod_rows = tm if per_row else 8
    mod_idx = (lambda i: i) if per_row else (lambda i: mod_row_block)
    kern = functools.partial(_inproj_kernel, per_row=per_row)
    return pl.pallas_call(
        kern,
        grid=(rows // tm, nj),
        in_specs=[
            pl.BlockSpec((tm, D_MODEL), lambda i, j: (i, 0)),
            pl.BlockSpec((mod_rows, D_MODEL), lambda i, j: (mod_idx(i), 0)),
            pl.BlockSpec((mod_rows, D_MODEL), lambda i, j: (mod_idx(i), 1)),
            pl.BlockSpec((1, D_MODEL), lambda i, j: (0, 0)),
            pl.BlockSpec((D_MODEL, tn), lambda i, j: (0, jnp.minimum(j, 1))),
            pl.BlockSpec((None, D_MODEL, tn), lambda i, j: (0, 0, jnp.maximum(j, 2))),
            pl.BlockSpec((D_MODEL, LANES), lambda i, j: (0, 0)),
            pl.BlockSpec((16, D_MODEL), lambda i, j: (0, 0)),
            pl.BlockSpec((tm, LANES), lambda i, j: (i, 0)),
            pl.BlockSpec((tm, LANES), lambda i, j: (i, 0)),
            pl.BlockSpec((tm, LANES), lambda i, j: (i, 0)),
        ],
        out_specs=[
            pl.BlockSpec((tm, tn), lambda i, j: (i, jnp.minimum(j, 1))),
            pl.BlockSpec((tm, tn), lambda i, j: (i, 0)),
            pl.BlockSpec((tm, tn), lambda i, j: (0, 0)),
            pl.BlockSpec((tm, tn), lambda i, j: (i, jnp.clip(j - 3, 0, 7))),
            pl.BlockSpec((tm, LANES), lambda i, j: (i, 0)),
            pl.BlockSpec((16, tm), lambda i, j: (0, i)),
        ],
        out_shape=[
            jax.ShapeDtypeStruct((rows, ATT_WIDTH), BF16),
            jax.ShapeDtypeStruct((rows, 2 * KV_WIDTH), BF16),
            jax.ShapeDtypeStruct((tm, 2 * KV_WIDTH), F32),
            jax.ShapeDtypeStruct((rows, 4 * MLSTM_WIDTH), BF16),
            jax.ShapeDtypeStruct((rows, LANES), F32),
            jax.ShapeDtypeStruct((16, rows), F32),
        ],
        scratch_shapes=[pltpu.VMEM((tm, D_MODEL), BF16)],
        compiler_params=_cparams(("arbitrary", "arbitrary")),
        name="inproj_s" if per_row else "inproj_p",
    )(x, mod, mod, norm_w, wq, w_in, wg, wgt, cos, sa, sb)


def _attn_p_kernel(sink_ref, q_ref, kvp_ref, kvc_ref, o_ref):
    n = pl.program_id(0)
    w = WINDOW
    qi = lax.broadcasted_iota(jnp.int32, (w, 2 * w), 0)
    kj = lax.broadcasted_iota(jnp.int32, (w, 2 * w), 1)
    prev_off = jnp.where(n > 0, 0, 4 * w)
    allowed = ((kj < w) & (kj > qi + prev_off)) | ((kj >= w) & (kj - w <= qi))
    low = lax.broadcasted_iota(jnp.int32, (2 * w, LANES), 1) < ATT_HEAD_DIM
    kv2 = jnp.concatenate([kvp_ref[...], kvc_ref[...]], axis=0)
    zero = jnp.zeros((2 * w, LANES), BF16)
    for cp in range(2):
        k128 = kv2[:, LANES * cp:LANES * (cp + 1)]
        v128 = kv2[:, KV_WIDTH + LANES * cp:KV_WIDTH + LANES * (cp + 1)]
        kbd = jnp.concatenate([jnp.where(low, k128, zero), jnp.where(low, zero, k128)], axis=0)
        vbd = jnp.concatenate([jnp.where(low, v128, zero), jnp.where(low, zero, v128)], axis=0)
        for r in range(ATT_GROUP):
            c0 = 256 * r + LANES * cp
            s = _dot_nt(q_ref[:, c0:c0 + LANES], kbd)
            ps = []
            for half in range(2):
                sh = jnp.where(allowed, s[:, 2 * w * half:2 * w * (half + 1)], -jnp.inf)
                sink = sink_ref[(2 * cp + half) * ATT_GROUP + r]
                m = jnp.maximum(jnp.max(sh, axis=-1, keepdims=True), sink)
                e = jnp.exp(sh - m)
                l = jnp.sum(e, axis=-1, keepdims=True) + jnp.exp(sink - m)
                ps.append((e / l).astype(BF16))
            p = jnp.concatenate(ps, axis=1)
            o_ref[:, c0:c0 + LANES] = _dot(p, vbd).astype(BF16)


def _attn_p(sinks, q, kv):
    nb = SEQ // WINDOW
    return pl.pallas_call(
        _attn_p_kernel,
        grid=(nb,),
        in_specs=[
            pl.BlockSpec(memory_space=pltpu.SMEM),
            pl.BlockSpec((WINDOW, ATT_WIDTH), lambda n: (n, 0)),
            pl.BlockSpec((WINDOW, 2 * KV_WIDTH), lambda n: (jnp.maximum(n - 1, 0), 0)),
            pl.BlockSpec((WINDOW, 2 * KV_WIDTH), lambda n: (n, 0)),
        ],
        out_specs=pl.BlockSpec((WINDOW, ATT_WIDTH), lambda n: (n, 0)),
        out_shape=jax.ShapeDtypeStruct((SEQ, ATT_WIDTH), BF16),
        compiler_params=_cparams(("arbitrary",)),
        name="attn_p",
    )(sinks, q, kv, kv)


ATT_S_BB = 8


def _attn_s_kernel(sink_ref, q_ref, kv32_ref, ck_ref, cv_ref, o_ref, ko_ref, vo_ref, q32_scr):
    t_new = DEC_SEQ
    w = WINDOW
    q32_scr[...] = q_ref[...].astype(F32)
    rows = 4 * 4 * 8
    row = lax.broadcasted_iota(jnp.int32, (rows, w), 0)
    slot = lax.broadcasted_iota(jnp.int32, (rows, w), 1)
    t_row = row % t_new
    second = (row % 8) >= t_new
    win_ok = (slot < w - t_new) | (slot - (w - t_new) <= t_row)
    old_ok = (slot >= 1) & (slot < t_new) & (slot > t_row)
    lane256 = lax.broadcasted_iota(jnp.int32, (32, 2 * LANES), 1)
    sink = sink_ref[...][:, 0:1]
    for b in range(ATT_S_BB):
        ko_ref[b, 0:w - t_new, :] = ck_ref[b, t_new:w, :]
        vo_ref[b, 0:w - t_new, :] = cv_ref[b, t_new:w, :]
        ko_ref[b, w - t_new:w, :] = kv32_ref[t_new * b:t_new * (b + 1), 0:KV_WIDTH]
        vo_ref[b, w - t_new:w, :] = kv32_ref[t_new * b:t_new * (b + 1), KV_WIDTH:]
    for pair in range(ATT_S_BB // 2):
        b0, b1 = 2 * pair, 2 * pair + 1
        q32 = jnp.concatenate([q32_scr[8 * pair:8 * (pair + 1), 256 * r:256 * (r + 1)] for r in range(ATT_GROUP)],
                              axis=0)
        qpad = jnp.concatenate(
            [jnp.where((lane256 // ATT_HEAD_DIM) == g, q32, 0.0) for g in range(ATT_KV_HEADS)], axis=0).astype(BF16)
        kw = [ko_ref[b].astype(BF16) for b in (b0, b1)]
        vw = [vo_ref[b].astype(BF16) for b in (b0, b1)]
        kc = [ck_ref[b].astype(BF16) for b in (b0, b1)]
        vc = [cv_ref[b].astype(BF16) for b in (b0, b1)]
        s_w = jnp.where(second, _dot_nt(qpad, kw[1]), _dot_nt(qpad, kw[0]))
        s_c = jnp.where(second, _dot_nt(qpad, kc[1]), _dot_nt(qpad, kc[0]))
        s_w = jnp.where(win_ok, s_w, -jnp.inf)
        s_c = jnp.where(old_ok, s_c, -jnp.inf)
        m = jnp.maximum(jnp.maximum(jnp.max(s_w, axis=-1, keepdims=True), jnp.max(s_c, axis=-1, keepdims=True)), sink)
        e_w = jnp.exp(s_w - m)
        e_c = jnp.exp(s_c - m)
        l = jnp.sum(e_w, axis=-1, keepdims=True) + jnp.sum(e_c, axis=-1, keepdims=True) + jnp.exp(sink - m)
        p_w = e_w / l
        p_c = e_c / l
        zero = jnp.zeros_like(p_w)
        o = (_dot(jnp.where(second, zero, p_w).astype(BF16), vw[0]) + _dot(jnp.where(second, p_w, zero).astype(BF16), vw[1])
             + _dot(jnp.where(second, zero, p_c).astype(BF16), vc[0]) + _dot(jnp.where(second, p_c, zero).astype(BF16), vc[1]))
        o32 = jnp.zeros((32, 2 * LANES), F32)
        for g in range(ATT_KV_HEADS):
            o32 = jnp.where((lane256 // ATT_HEAD_DIM) == g, o[32 * g:32 * (g + 1), :], o32)
        for r in range(ATT_GROUP):
            o_ref[8 * pair:8 * (pair + 1), 256 * r:256 * (r + 1)] = o32[8 * r:8 * (r + 1), :]


def _attn_s(sink_col, q, kv32, ck, cv):
    bb = ATT_S_BB
    rows = bb * DEC_SEQ
    return pl.pallas_call(
        _attn_s_kernel,
        grid=(DEC_BATCH // bb,),
        in_specs=[
            pl.BlockSpec((128, LANES), lambda i: (0, 0)),
            pl.BlockSpec((rows, ATT_WIDTH), lambda i: (i, 0)),
            pl.BlockSpec((rows, 2 * KV_WIDTH), lambda i: (i, 0)),
            pl.BlockSpec((bb, WINDOW, KV_WIDTH), lambda i: (i, 0, 0)),
            pl.BlockSpec((bb, WINDOW, KV_WIDTH), lambda i: (i, 0, 0)),
        ],
        out_specs=[
            pl.BlockSpec((rows, ATT_WIDTH), lambda i: (i, 0)),
            pl.BlockSpec((bb, WINDOW, KV_WIDTH), lambda i: (i, 0, 0)),
            pl.BlockSpec((bb, WINDOW, KV_WIDTH), lambda i: (i, 0, 0)),
        ],
        out_shape=[
            jax.ShapeDtypeStruct((S_ROWS, ATT_WIDTH), F32),
            jax.ShapeDtypeStruct((DEC_BATCH, WINDOW, KV_WIDTH), F32),
            jax.ShapeDtypeStruct((DEC_BATCH, WINDOW, KV_WIDTH), F32),
        ],
        scratch_shapes=[pltpu.VMEM((rows, ATT_WIDTH), F32)],
        compiler_params=_cparams(("arbitrary",)),
        name="attn_s",
    )(sink_col, q, kv32, ck, cv)


def _head_norm_gate(h, nw, mo):
    hn = h * lax.rsqrt(jnp.mean(h * h, axis=-1, keepdims=True) + EPS) * nw
    return hn * _sigmoid(mo.astype(F32))


def _mlstm_p_kernel(q_ref, k_ref, v_ref, mo_ref, g_ref, gt_ref, brow_ref, bcol_ref, nw_ref,
                    mh_ref, c_out, n_out, m_out, c_scr, n_scr, m_scr):
    c = pl.program_id(0)
    L = MLSTM_CHUNK_P
    dh = MLSTM_HEAD_DIM

    @pl.when(c == 0)
    def _():
        c_scr[...] = jnp.zeros_like(c_scr)
        n_scr[...] = jnp.zeros_like(n_scr)
        m_scr[...] = jnp.zeros_like(m_scr)

    ti = lax.broadcasted_iota(jnp.int32, (L, L), 0)
    si = lax.broadcasted_iota(jnp.int32, (L, L), 1)
    causal = si <= ti
    tri = causal.astype(F32)
    tri_t = (ti <= si).astype(F32)
    gates = g_ref[...] + brow_ref[...]
    gates_t = gt_ref[...] + bcol_ref[...]
    b_col = jnp.dot(tri, _log_sigmoid(gates), precision=HI, preferred_element_type=F32)
    b_row = jnp.dot(_log_sigmoid(gates_t), tri_t, precision=HI, preferred_element_type=F32)
    for hd in range(MLSTM_HEADS):
        cs = slice(dh * hd, dh * (hd + 1))
        b_c = b_col[:, MLSTM_HEADS + hd:MLSTM_HEADS + hd + 1]
        li_c = gates[:, hd:hd + 1]
        b_r = b_row[MLSTM_HEADS + hd:MLSTM_HEADS + hd + 1, :]
        li_r = gates_t[hd:hd + 1, :]
        dm = jnp.where(causal, b_c - b_r + li_r, -jnp.inf)
        m_prev = m_scr[hd:hd + 1, 0:1]
        m_inter = b_c + m_prev
        m_t = jnp.maximum(m_inter, jnp.max(dm, axis=-1, keepdims=True))
        q = q_ref[:, cs] * (dh ** -0.5)
        k = k_ref[:, cs]
        v = v_ref[:, cs]
        sm = _dot_nt(q, k) * jnp.exp(dm - m_t)
        a = jnp.exp(m_inter - m_t)
        c_old = c_scr[hd]
        n_old = n_scr[hd:hd + 1, :]
        num = a * _dot_nt(q, c_old.astype(BF16)) + _dot(sm.astype(BF16), v)
        qn = jnp.sum(q.astype(F32) * n_old, axis=-1, keepdims=True)
        den = a * qn + jnp.sum(sm, axis=-1, keepdims=True)
        h = num / jnp.maximum(jnp.abs(den), jnp.exp(-m_t))
        mh_ref[:, cs] = _head_norm_gate(h, nw_ref[:, cs], mo_ref[:, cs]).astype(BF16)
        m_new = m_t[L - 1:L, :]
        b_last = b_c[L - 1:L, :]
        g = jnp.exp(b_last - b_c + li_c - m_new)
        decay = jnp.exp(b_last + m_prev - m_new)
        gv = (g * v.astype(F32)).astype(BF16)
        c_scr[hd] = decay * c_old + _dot_tn(gv, k)
        n_scr[hd:hd + 1, :] = decay * n_old + jnp.sum(g * k.astype(F32), axis=0, keepdims=True)
        m_scr[hd:hd + 1, :] = jnp.broadcast_to(m_new, (1, LANES))

    @pl.when(c == pl.num_programs(0) - 1)
    def _():
        c_out[...] = c_scr[...]
        n_out[...] = n_scr[...]
        m_out[...] = m_scr[...]


def _mlstm_p(m_all, gates, gates_t, brow, bcol, nw):
    L = MLSTM_CHUNK_P
    nc = SEQ // L
    dh = MLSTM_HEAD_DIM
    return pl.pallas_call(
        _mlstm_p_kernel,
        grid=(nc,),
        in_specs=[
            pl.BlockSpec((L, MLSTM_WIDTH), lambda c: (c, 0)),
            pl.BlockSpec((L, MLSTM_WIDTH), lambda c: (c, 1)),
            pl.BlockSpec((L, MLSTM_WIDTH), lambda c: (c, 2)),
            pl.BlockSpec((L, MLSTM_WIDTH), lambda c: (c, 3)),
            pl.BlockSpec((L, LANES), lambda c: (c, 0)),
            pl.BlockSpec((16, L), lambda c: (0, c)),
            pl.BlockSpec((1, LANES), lambda c: (0, 0)),
            pl.BlockSpec((16, L), lambda c: (0, 0)),
            pl.BlockSpec((1, MLSTM_WIDTH), lambda c: (0, 0)),
        ],
        out_specs=[
            pl.BlockSpec((L, MLSTM_WIDTH), lambda c: (c, 0)),
            pl.BlockSpec((MLSTM_HEADS, dh, dh), lambda c: (0, 0, 0)),
            pl.BlockSpec((8, dh), lambda c: (0, 0)),
            pl.BlockSpec((8, LANES), lambda c: (0, 0)),
        ],
        out_shape=[
            jax.ShapeDtypeStruct((SEQ, MLSTM_WIDTH), BF16),
            jax.ShapeDtypeStruct((MLSTM_HEADS, dh, dh), F32),
            jax.ShapeDtypeStruct((8, dh), F32),
            jax.ShapeDtypeStruct((8, LANES), F32),
        ],
        scratch_shapes=[
            pltpu.VMEM((MLSTM_HEADS, dh, dh), F32),
            pltpu.VMEM((8, dh), F32),
            pltpu.VMEM((8, LANES), F32),
        ],
        compiler_params=_cparams(("arbitrary",)),
        name="mlstm_p",
    )(m_all, m_all, m_all, m_all, gates, gates_t, brow, bcol, nw)


MLSTM_S_BB = 4


def _mlstm_s_kernel(q_ref, k_ref, v_ref, mo_ref, g_ref, nrep_ref, mrep_ref, brow_ref, nw_ref, c_ref,
                    mh_ref, c_out, nrow_ref, mrow_ref):
    T = DEC_SEQ
    R = MLSTM_S_BB * T
    H = MLSTM_HEADS
    dh = MLSTM_HEAD_DIM

    def shift(x, d):
        return pltpu.roll(x, d, 0)

    lanes = lax.broadcasted_iota(jnp.int32, (R, LANES), 1)
    tmod = lax.broadcasted_iota(jnp.int32, (R, LANES), 0) % T
    tmod_w = lax.broadcasted_iota(jnp.int32, (R, MLSTM_WIDTH), 0) % T
    head_ok = lanes < H
    gates = g_ref[...] + brow_ref[...]
    li = jnp.where(head_ok, gates, 0.0)
    lf = jnp.where(head_ok, pltpu.roll(_log_sigmoid(gates), LANES - H, 1), 0.0)
    bcum = lf
    for d in range(1, T):
        bcum = bcum + jnp.where(tmod >= d, shift(lf, d), 0.0)
    m0 = mrep_ref[...]
    m_inter = bcum + m0
    dms = [li] + [jnp.where(tmod >= d, bcum - shift(bcum, d) + shift(li, d), -jnp.inf) for d in range(1, T)]
    m_t = m_inter
    for dm in dms:
        m_t = jnp.maximum(m_t, dm)
    a = jnp.exp(m_inter - m_t)
    ws = [jnp.exp(dm - m_t) for dm in dms]

    q_bf = q_ref[...] * (dh ** -0.5)
    q = q_bf.astype(F32)
    k = k_ref[...].astype(F32)
    v = v_ref[...].astype(F32)
    seg = (lax.broadcasted_iota(jnp.int32, (MLSTM_WIDTH, LANES), 0) // dh
           == lax.broadcasted_iota(jnp.int32, (MLSTM_WIDTH, LANES), 1)).astype(F32)
    ex = (lax.broadcasted_iota(jnp.int32, (LANES, MLSTM_WIDTH), 1) // dh
          == lax.broadcasted_iota(jnp.int32, (LANES, MLSTM_WIDTH), 0)).astype(F32)

    def segsum(x):
        return jnp.dot(x, seg, precision=HI, preferred_element_type=F32)

    def expand(x):
        return jnp.dot(x, ex, precision=HI, preferred_element_type=F32)

    ks = [k] + [shift(k, d) for d in range(1, T)]
    vs = [v] + [shift(v, d) for d in range(1, T)]
    sms = [segsum(q * ks[d]) * ws[d] for d in range(T)]
    den = a * segsum(q * nrep_ref[...])
    for sm in sms:
        den = den + sm
    inv = 1.0 / jnp.maximum(jnp.abs(den), jnp.exp(-m_t))

    def last(x):
        out = jnp.zeros_like(x)
        for jj in range(T):
            out = jnp.where(tmod == T - 1 - jj, x if jj == 0 else pltpu.roll(x, R - jj, 0), out)
        return out

    m_new = last(m_t)
    b_last = last(bcum)
    g = jnp.where(head_ok, jnp.exp(b_last - bcum + li - m_new), 0.0)
    decay = jnp.where(head_ok, jnp.exp(b_last + m0 - m_new), 0.0)
    a_f = expand(a * inv)
    w_f = [expand(sm * inv) for sm in sms]
    g_f = expand(g)
    d_f = expand(decay)
    gv = (g_f * v).astype(BF16)
    rowb = lax.broadcasted_iota(jnp.int32, (R, dh), 0) // T
    for hd in range(H):
        cs = slice(dh * hd, dh * (hd + 1))
        qh = q_bf[:, cs]
        kh = k_ref[:, cs]
        gvh = gv[:, cs]
        qc = jnp.zeros((R, dh), F32)
        for bb in range(MLSTM_S_BB):
            c_old = c_ref[bb, hd]
            qc = jnp.where(rowb == bb, _dot_nt(qh, c_old.astype(BF16)), qc)
            upd = _dot_tn(jnp.where(rowb == bb, gvh, jnp.zeros_like(gvh)), kh)
            c_out[bb, hd] = d_f[T * bb:T * bb + 1, cs] * c_old + upd
        h = a_f[:, cs] * qc
        for d in range(T):
            h = h + w_f[d][:, cs] * vs[d][:, cs]
        mh_ref[:, cs] = _head_norm_gate(h, nw_ref[:, cs], mo_ref[:, cs]).astype(BF16)
    gk = g_f * k
    nsum = gk
    for d in range(1, T):
        nsum = nsum + jnp.where(tmod_w >= d, shift(gk, d), 0.0)
    nrow_ref[...] = d_f * nrep_ref[...] + nsum
    mrow_ref[...] = m_t


def _mlstm_s(m_all, gates, n_rep, m_rep, brow, nw, state_c):
    bb = MLSTM_S_BB
    R = bb * DEC_SEQ
    dh = MLSTM_HEAD_DIM
    H = MLSTM_HEADS
    return pl.pallas_call(
        _mlstm_s_kernel,
        grid=(DEC_BATCH // bb,),
        in_specs=[
            pl.BlockSpec((R, MLSTM_WIDTH), lambda i: (i, 0)),
            pl.BlockSpec((R, MLSTM_WIDTH), lambda i: (i, 1)),
            pl.BlockSpec((R, MLSTM_WIDTH), lambda i: (i, 2)),
            pl.BlockSpec((R, MLSTM_WIDTH), lambda i: (i, 3)),
            pl.BlockSpec((R, LANES), lambda i: (i, 0)),
            pl.BlockSpec((R, MLSTM_WIDTH), lambda i: (i, 0)),
            pl.BlockSpec((R, LANES), lambda i: (i, 0)),
            pl.BlockSpec((1, LANES), lambda i: (0, 0)),
            pl.BlockSpec((1, MLSTM_WIDTH), lambda i: (0, 0)),
            pl.BlockSpec((bb, H, dh, dh), lambda i: (i, 0, 0, 0)),
        ],
        out_specs=[
            pl.BlockSpec((R, MLSTM_WIDTH), lambda i: (i, 0)),
            pl.BlockSpec((bb, H, dh, dh), lambda i: (i, 0, 0, 0)),
            pl.BlockSpec((R, MLSTM_WIDTH), lambda i: (i, 0)),
            pl.BlockSpec((R, LANES), lambda i: (i, 0)),
        ],
        out_shape=[
            jax.ShapeDtypeStruct((S_ROWS, MLSTM_WIDTH), BF16),
            jax.ShapeDtypeStruct((DEC_BATCH, H, dh, dh), F32),
            jax.ShapeDtypeStruct((S_ROWS, MLSTM_WIDTH), F32),
            jax.ShapeDtypeStruct((S_ROWS, LANES), F32),
        ],
        compiler_params=_cparams(("arbitrary",)),
        name="mlstm_s",
    )(m_all, m_all, m_all, m_all, gates, n_rep, m_rep, brow, nw, state_c)


def _outproj_kernel(att_ref, mh_ref, x_ref, g1_ref, sh_ref, sc_ref, nw_ref, wa_ref, wm_ref, x1_ref, h2_ref, *,
                    per_row):
    y = _dot(att_ref[...].astype(BF16), wa_ref[...]) + _dot(mh_ref[...], wm_ref[...])
    x1 = x_ref[...] + _mod_row(g1_ref, per_row) * y
    x1_ref[...] = x1
    r = lax.rsqrt(jnp.mean(x1 * x1, axis=-1, keepdims=True) + EPS)
    h2 = (x1 * r * nw_ref[...]) * (1.0 + _mod_row(sc_ref, per_row)) + _mod_row(sh_ref, per_row)
    h2_ref[...] = h2.astype(BF16)


def _outproj(att, mh, x, mod, mod_row_block, norm_w, w_att, w_m, *, tm, per_row):
    rows = x.shape[0]
    mod_rows = tm if per_row else 8
    mod_idx = (lambda i: i) if per_row else (lambda i: mod_row_block)
    kern = functools.partial(_outproj_kernel, per_row=per_row)
    return pl.pallas_call(
        kern,
        grid=(rows // tm,),
        in_specs=[
            pl.BlockSpec((tm, ATT_WIDTH), lambda i: (i, 0)),
            pl.BlockSpec((tm, MLSTM_WIDTH), lambda i: (i, 0)),
            pl.BlockSpec((tm, D_MODEL), lambda i: (i, 0)),
            pl.BlockSpec((mod_rows, D_MODEL), lambda i: (mod_idx(i), 2)),
            pl.BlockSpec((mod_rows, D_MODEL), lambda i: (mod_idx(i), 3)),
            pl.BlockSpec((mod_rows, D_MODEL), lambda i: (mod_idx(i), 4)),
            pl.BlockSpec((1, D_MODEL), lambda i: (0, 0)),
            pl.BlockSpec((ATT_WIDTH, D_MODEL), lambda i: (0, 0)),
            pl.BlockSpec((MLSTM_WIDTH, D_MODEL), lambda i: (0, 0)),
        ],
        out_specs=[
            pl.BlockSpec((tm, D_MODEL), lambda i: (i, 0)),
            pl.BlockSpec((tm, D_MODEL), lambda i: (i, 0)),
        ],
        out_shape=[
            jax.ShapeDtypeStruct((rows, D_MODEL), F32),
            jax.ShapeDtypeStruct((rows, D_MODEL), BF16),
        ],
        compiler_params=_cparams(("arbitrary",)),
        name="outproj_s" if per_row else "outproj_p",
    )(att, mh, x, mod, mod, mod, norm_w, w_att, w_m)


def _ffn_a_kernel(h_ref, wg_ref, wu_ref, a_ref):
    h = h_ref[...]
    g = _dot(h, wg_ref[...].astype(BF16))
    u = _dot(h, wu_ref[...].astype(BF16))
    a_ref[...] = (g * _sigmoid(g) * u).astype(BF16)


def _ffn_a(h2, w_gate, w_up, *, tm):
    rows = h2.shape[0]
    tn = 512
    return pl.pallas_call(
        _ffn_a_kernel,
        grid=(rows // tm, D_FF // tn),
        in_specs=[
            pl.BlockSpec((tm, D_MODEL), lambda i, j: (i, 0)),
            pl.BlockSpec((D_MODEL, tn), lambda i, j: (0, j)),
            pl.BlockSpec((D_MODEL, tn), lambda i, j: (0, j)),
        ],
        out_specs=pl.BlockSpec((tm, tn), lambda i, j: (i, j)),
        out_shape=jax.ShapeDtypeStruct((rows, D_FF), BF16),
        compiler_params=_cparams(("arbitrary", "arbitrary")),
        name="ffn_a",
    )(h2, w_gate, w_up)


def _ffn_b_kernel(a_ref, wd_ref, x1_ref, g2_ref, fw_ref, y_ref, *, per_row):
    kk = pl.program_id(1)
    @pl.when(kk == 0)
    def _():
        y_ref[...] = _dot(a_ref[...], wd_ref[...].astype(BF16))

    @pl.when(kk > 0)
    def _():
        y_ref[...] += _dot(a_ref[...], wd_ref[...].astype(BF16))

    @pl.when(kk == pl.num_programs(1) - 1)
    def _():
        x2 = x1_ref[...] + _mod_row(g2_ref, per_row) * y_ref[...]
        y_ref[...] = x2 * lax.rsqrt(jnp.mean(x2 * x2, axis=-1, keepdims=True) + EPS) * fw_ref[...]


def _ffn_b(a, w_down, x1, mod, mod_row_block, final_w, *, tm, per_row):
    rows = a.shape[0]
    tk = 512
    mod_rows = tm if per_row else 8
    mod_idx = (lambda i: i) if per_row else (lambda i: mod_row_block)
    kern = functools.partial(_ffn_b_kernel, per_row=per_row)
    return pl.pallas_call(
        kern,
        grid=(rows // tm, D_FF // tk),
        in_specs=[
            pl.BlockSpec((tm, tk), lambda i, k: (i, k)),
            pl.BlockSpec((tk, D_MODEL), lambda i, k: (k, 0)),
            pl.BlockSpec((tm, D_MODEL), lambda i, k: (i, 0), pipeline_mode=pl.Buffered(1)),
            pl.BlockSpec((mod_rows, D_MODEL), lambda i, k: (mod_idx(i), 5)),
            pl.BlockSpec((1, D_MODEL), lambda i, k: (0, 0)),
        ],
        out_specs=pl.BlockSpec((tm, D_MODEL), lambda i, k: (i, 0)),
        out_shape=jax.ShapeDtypeStruct((rows, D_MODEL), F32),
        compiler_params=_cparams(("arbitrary", "arbitrary")),
        name="ffn_b_s" if per_row else "ffn_b_p",
    )(a, w_down, x1, mod, final_w)


def _rope_tables(pos):
    half = ROPE_DIM // 2
    inv = ROPE_THETA ** (-jnp.arange(0, ROPE_DIM, 2, dtype=F32) / ROPE_DIM)
    d = jnp.arange(LANES, dtype=jnp.int32) % ATT_HEAD_DIM
    ang = pos.astype(F32)[:, None] * inv[d % half][None, :]
    cos, sin = jnp.cos(ang), jnp.sin(ang)
    d = d[None, :]
    return (jnp.where(d < ROPE_DIM, cos, 1.0), jnp.where(d < half, -sin, 0.0),
            jnp.where((d >= half) & (d < ROPE_DIM), sin, 0.0))


def kernel(x_prompt, x_sample, cache_k_win, cache_v_win, state_C, state_n, state_m, c_prompt, c_sample,
           norm1_w, norm2_w, final_norm_w, w_ada, b_ada, w_in, b_ig, b_fg, attn_sinks, mh_norm_w,
           w_out, w_gate, w_up, w_down):
    assert w_in.shape[0] == 1, "single-layer trunk"
    T = DEC_SEQ
    xp = x_prompt[0]
    xs = x_sample.reshape(S_ROWS, D_MODEL)

    c_all = jnp.concatenate([jnp.repeat(c_sample, T, axis=0), c_prompt, jnp.zeros((15, D_MODEL), F32)], axis=0)
    mod = _ada(c_all, w_ada[0], b_ada)
    prompt_mod_block = S_ROWS // 8

    wq = (w_in[0, :, :ATT_WIDTH].reshape(D_MODEL, ATT_KV_HEADS, ATT_GROUP, ATT_HEAD_DIM)
          .transpose(0, 2, 1, 3).reshape(D_MODEL, ATT_WIDTH).astype(BF16))
    w_gates = w_in[0, :, MAIN_WIDTH:]
    wg = jnp.pad(w_gates, ((0, 0), (0, LANES - 2 * MLSTM_HEADS))).astype(BF16)
    wgt = jnp.pad(w_gates.T, ((0, 16 - 2 * MLSTM_HEADS), (0, 0))).astype(BF16)
    w_out_att = (w_out[0, :ATT_WIDTH].reshape(ATT_KV_HEADS, ATT_GROUP, ATT_HEAD_DIM, D_MODEL)
                 .transpose(1, 0, 2, 3).reshape(ATT_WIDTH, D_MODEL).astype(BF16))
    w_out_m = w_out[0, ATT_WIDTH:].astype(BF16)
    n1 = norm1_w.reshape(1, D_MODEL)
    n2 = norm2_w.reshape(1, D_MODEL)
    fw = final_norm_w.reshape(1, D_MODEL)
    nw = mh_norm_w.reshape(1, MLSTM_WIDTH)
    gate_bias = jnp.concatenate([b_ig[0], b_fg[0]])
    brow = jnp.pad(gate_bias, (0, LANES - 2 * MLSTM_HEADS)).reshape(1, LANES)
    bcol = jnp.broadcast_to(jnp.pad(gate_bias, (0, 16 - 2 * MLSTM_HEADS))[:, None], (16, MLSTM_CHUNK_P))

    rope_p = _rope_tables(jnp.arange(SEQ, dtype=jnp.int32))
    rope_s = _rope_tables(jnp.tile(PAST_LEN + jnp.arange(T, dtype=jnp.int32), DEC_BATCH))

    tm_p = 1024
    q_p, kv_p, kv32_p, m_p, g_p, gt_p = _inproj(xp, mod, prompt_mod_block, n1, wq, w_in, wg, wgt, *rope_p,
                                                tm=tm_p, per_row=False)
    q_s, kv_s, kv32_s, m_s, g_s, _ = _inproj(xs, mod, 0, n1, wq, w_in, wg, wgt, *rope_s,
                                              tm=S_ROWS, per_row=True)
    del kv_s

    sinks = attn_sinks[0]
    att_p = _attn_p(sinks, q_p, kv_p)
    sink_col = jnp.broadcast_to(sinks.reshape(ATT_HEADS, 1, 1), (ATT_HEADS, 8, LANES)).reshape(128, LANES)
    ck = cache_k_win.reshape(DEC_BATCH, WINDOW, KV_WIDTH)
    cv = cache_v_win.reshape(DEC_BATCH, WINDOW, KV_WIDTH)
    att_s, kwin_s, vwin_s = _attn_s(sink_col, q_s, kv32_s, ck, cv)

    mh_p, c_p, n_p, mm_p = _mlstm_p(m_p, g_p, gt_p, brow, bcol, nw)
    n_rep = jnp.repeat(state_n[0].reshape(DEC_BATCH, MLSTM_WIDTH), T, axis=0)
    m_rep = jnp.pad(jnp.repeat(state_m[0], T, axis=0), ((0, 0), (0, LANES - MLSTM_HEADS)))
    mh_s, c_s, nrow_s, mrow_s = _mlstm_s(m_s, g_s, n_rep, m_rep, brow, nw, state_C[0])

    x1_p, h2_p = _outproj(att_p, mh_p, xp, mod, prompt_mod_block, n2, w_out_att, w_out_m, tm=512, per_row=False)
    x1_s, h2_s = _outproj(att_s, mh_s, xs, mod, 0, n2, w_out_att, w_out_m, tm=S_ROWS, per_row=True)
    a_p = _ffn_a(h2_p, w_gate[0], w_up[0], tm=tm_p)
    a_s = _ffn_a(h2_s, w_gate[0], w_up[0], tm=S_ROWS)
    y_p = _ffn_b(a_p, w_down[0], x1_p, mod, prompt_mod_block, fw, tm=tm_p, per_row=False)
    y_s = _ffn_b(a_s, w_down[0], x1_s, mod, 0, fw, tm=S_ROWS, per_row=True)

    kv_shape = (1, 1, WINDOW, ATT_KV_HEADS, ATT_HEAD_DIM)
    kv_last = kv32_p[tm_p - WINDOW:]
    dh = MLSTM_HEAD_DIM
    return (
        y_p.reshape(1, SEQ, D_MODEL),
        y_s.reshape(DEC_BATCH, T, D_MODEL),
        kv_last[:, :KV_WIDTH].reshape(kv_shape),
        kv_last[:, KV_WIDTH:].reshape(kv_shape),
        c_p.reshape(1, 1, MLSTM_HEADS, dh, dh),
        n_p[:MLSTM_HEADS].reshape(1, 1, MLSTM_HEADS, dh),
        mm_p[:MLSTM_HEADS, 0].reshape(1, 1, MLSTM_HEADS),
        kwin_s.reshape(1, DEC_BATCH, WINDOW, ATT_KV_HEADS, ATT_HEAD_DIM),
        vwin_s.reshape(1, DEC_BATCH, WINDOW, ATT_KV_HEADS, ATT_HEAD_DIM),
        c_s.reshape(1, DEC_BATCH, MLSTM_HEADS, dh, dh),
        nrow_s[T - 1::T].reshape(1, DEC_BATCH, MLSTM_HEADS, dh),
        mrow_s[T - 1::T, :MLSTM_HEADS].reshape(1, DEC_BATCH, MLSTM_HEADS),
    )
```

```python
import functools

import jax
import jax.numpy as jnp
import numpy as np
from jax import lax
from jax.experimental import pallas as pl
from jax.experimental.pallas import tpu as pltpu

F32 = jnp.float32
BF16 = jnp.bfloat16

D_MODEL = 2048
SEQ = 8192
DEC_BATCH = 128
DEC_SEQ = 4
S_ROWS = DEC_BATCH * DEC_SEQ
PAST_LEN = 16384
ATT_HEADS = 16
ATT_KV_HEADS = 4
ATT_GROUP = 4
ATT_HEAD_DIM = 64
WINDOW = 128
ROPE_THETA = 500000.0
ROPE_DIM = 16
MLSTM_HEADS = 4
MLSTM_HEAD_DIM = 256
ATT_WIDTH = 1024
KV_WIDTH = 256
MLSTM_WIDTH = 1024
MAIN_WIDTH = ATT_WIDTH + 2 * KV_WIDTH + 4 * MLSTM_WIDTH
D_FF = 5632
N_MOD = 6
EPS = 1e-6

LANES = 128
MLSTM_CHUNK_P = 256
VMEM_LIMIT = 56 * 1024 * 1024

NT_DIMS = (((1,), (1,)), ((), ()))
TN_DIMS = (((0,), (0,)), ((), ()))
HI = lax.Precision.HIGHEST


def _cparams(sem):
    return pltpu.CompilerParams(dimension_semantics=sem, vmem_limit_bytes=VMEM_LIMIT)


def _dot(a, b):
    return jnp.dot(a, b, preferred_element_type=F32)


def _dot_nt(a, b):
    return lax.dot_general(a, b, NT_DIMS, preferred_element_type=F32)


def _dot_tn(a, b):
    return lax.dot_general(a, b, TN_DIMS, preferred_element_type=F32)


def _sigmoid(x):
    return 1.0 / (1.0 + jnp.exp(-x))


def _log_sigmoid(x):
    return jnp.minimum(x, 0.0) - jnp.log(1.0 + jnp.exp(-jnp.abs(x)))


def _mod_row(ref, per_row):
    return ref[...] if per_row else ref[0:1, :]


def _ada_kernel(c_ref, w_ref, b_ref, o_ref, s_scr):
    @pl.when(pl.program_id(0) == 0)
    def _():
        c = c_ref[...]
        s_scr[...] = (c * _sigmoid(c)).astype(BF16)

    o_ref[...] = _dot(s_scr[...], w_ref[...].astype(BF16)) + b_ref[...]


def _ada(c_all, w_ada, b_ada):
    m = c_all.shape[0]
    n = w_ada.shape[1]
    tn = 1024
    return pl.pallas_call(
        _ada_kernel,
        grid=(n // tn,),
        in_specs=[
            pl.BlockSpec((m, D_MODEL), lambda j: (0, 0)),
            pl.BlockSpec((D_MODEL, tn), lambda j: (0, j)),
            pl.BlockSpec((1, tn), lambda j: (0, j)),
        ],
        out_specs=pl.BlockSpec((m, tn), lambda j: (0, j)),
        out_shape=jax.ShapeDtypeStruct((m, n), F32),
        scratch_shapes=[pltpu.VMEM((m, D_MODEL), BF16)],
        compiler_params=_cparams(("arbitrary",)),
        name="ada",
    )(c_all, w_ada, b_ada)


def _rope_store(acc, cos, sa, sb, out_ref, ncols, scale):
    for c in range(ncols // LANES):
        xc = acc[:, LANES * c:LANES * (c + 1)]
        rot = xc * cos + pltpu.roll(xc, LANES - 8, 1) * sa + pltpu.roll(xc, 8, 1) * sb
        if scale != 1.0:
            rot = rot * scale
        out_ref[:, LANES * c:LANES * (c + 1)] = rot.astype(out_ref.dtype)


def _inproj_kernel(x_ref, sh_ref, sc_ref, nw_ref, wq_ref, win_ref, wg_ref, wgt_ref, cos_ref, sa_ref, sb_ref,
                   q_ref, kv_ref, kv32_ref, m_ref, g_ref, gt_ref, h_scr, *, per_row):
    j = pl.program_id(1)

    @pl.when(j == 0)
    def _():
        x = x_ref[...]
        r = lax.rsqrt(jnp.mean(x * x, axis=-1, keepdims=True) + EPS)
        y = x * r * nw_ref[...]
        h = (y * (1.0 + _mod_row(sc_ref, per_row)) + _mod_row(sh_ref, per_row)).astype(BF16)
        h_scr[...] = h
        g_ref[...] = _dot(h, wg_ref[...])
        gt_ref[...] = _dot_nt(wgt_ref[...], h)

    @pl.when(j < 2)
    def _():
        acc = _dot_nt(h_scr[...], wq_ref[...])
        _rope_store(acc, cos_ref[...], sa_ref[...], sb_ref[...], q_ref, 512, ATT_HEAD_DIM ** -0.5)

    @pl.when(j == 2)
    def _():
        acc = _dot_nt(h_scr[...], win_ref[...].astype(BF16))
        _rope_store(acc, cos_ref[...], sa_ref[...], sb_ref[...], kv32_ref, KV_WIDTH, 1.0)
        kv32_ref[:, KV_WIDTH:] = acc[:, KV_WIDTH:]
        kv_ref[...] = kv32_ref[...].astype(BF16)

    @pl.when(j > 2)
    def _():
        m_ref[...] = _dot_nt(h_scr[...], win_ref[...].astype(BF16)).astype(BF16)


def _inproj(x, mod, mod_row_block, norm_w, wq_t, w_in_t, wg, wgt, cos, sa, sb, *, tm, per_row):
    rows = x.shape[0]
    tn = 512
    nj = MAIN_WIDTH // tn
    mod_rows = tm if per_row else 8
    mod_idx = (lambda i: i) if per_row else (lambda i: mod_row_block)
    kern = functools.partial(_inproj_kernel, per_row=per_row)
    return pl.pallas_call(
        kern,
        grid=(rows // tm, nj),
        in_specs=[
            pl.BlockSpec((tm, D_MODEL), lambda i, j: (i, 0)),
            pl.BlockSpec((mod_rows, D_MODEL), lambda i, j: (mod_idx(i), 0)),
            pl.BlockSpec((mod_rows, D_MODEL), lambda i, j: (mod_idx(i), 1)),
            pl.BlockSpec((1, D_MODEL), lambda i, j: (0, 0)),
            pl.BlockSpec((tn, D_MODEL), lambda i, j: (jnp.minimum(j, 1), 0)),
            pl.BlockSpec((tn, D_MODEL), lambda i, j: (jnp.maximum(j, 2), 0)),
            pl.BlockSpec((D_MODEL, LANES), lambda i, j: (0, 0)),
            pl.BlockSpec((16, D_MODEL), lambda i, j: (0, 0)),
            pl.BlockSpec((tm, LANES), lambda i, j: (i, 0)),
            pl.BlockSpec((tm, LANES), lambda i, j: (i, 0)),
            pl.BlockSpec((tm, LANES), lambda i, j: (i, 0)),
        ],
        out_specs=[
            pl.BlockSpec((tm, tn), lambda i, j: (i, jnp.minimum(j, 1))),
            pl.BlockSpec((tm, tn), lambda i, j: (i, 0)),
            pl.BlockSpec((tm, tn), lambda i, j: (0, 0)),
            pl.BlockSpec((tm, tn), lambda i, j: (i, jnp.clip(j - 3, 0, 7))),
            pl.BlockSpec((tm, LANES), lambda i, j: (i, 0)),
            pl.BlockSpec((16, tm), lambda i, j: (0, i)),
        ],
        out_shape=[
            jax.ShapeDtypeStruct((rows, ATT_WIDTH), BF16),
            jax.ShapeDtypeStruct((rows, 2 * KV_WIDTH), BF16),
            jax.ShapeDtypeStruct((tm, 2 * KV_WIDTH), F32),
            jax.ShapeDtypeStruct((rows, 4 * MLSTM_WIDTH), BF16),
            jax.ShapeDtypeStruct((rows, LANES), F32),
            jax.ShapeDtypeStruct((16, rows), F32),
        ],
        scratch_shapes=[pltpu.VMEM((tm, D_MODEL), BF16)],
        compiler_params=_cparams(("arbitrary", "arbitrary")),
        name="inproj_s" if per_row else "inproj_p",
    )(x, mod, mod, norm_w, wq_t, w_in_t, wg, wgt, cos, sa, sb)


def _attn_p_kernel(sink_ref, q_ref, kvp_ref, kvc_ref, o_ref):
    n = pl.program_id(0)
    w = WINDOW
    qi = lax.broadcasted_iota(jnp.int32, (w, 2 * w), 0)
    kj = lax.broadcasted_iota(jnp.int32, (w, 2 * w), 1)
    prev_off = jnp.where(n > 0, 0, 4 * w)
    allowed = ((kj < w) & (kj > qi + prev_off)) | ((kj >= w) & (kj - w <= qi))
    low = lax.broadcasted_iota(jnp.int32, (2 * w, LANES), 1) < ATT_HEAD_DIM
    kv2 = jnp.concatenate([kvp_ref[...], kvc_ref[...]], axis=0)
    zero = jnp.zeros((2 * w, LANES), BF16)
    for cp in range(2):
        k128 = kv2[:, LANES * cp:LANES * (cp + 1)]
        v128 = kv2[:, KV_WIDTH + LANES * cp:KV_WIDTH + LANES * (cp + 1)]
        kbd = jnp.concatenate([jnp.where(low, k128, zero), jnp.where(low, zero, k128)], axis=0)
        vbd = jnp.concatenate([jnp.where(low, v128, zero), jnp.where(low, zero, v128)], axis=0)
        for r in range(ATT_GROUP):
            c0 = 256 * r + LANES * cp
            s = _dot_nt(q_ref[:, c0:c0 + LANES], kbd)
            ps = []
            for half in range(2):
                sh = jnp.where(allowed, s[:, 2 * w * half:2 * w * (half + 1)], -jnp.inf)
                sink = sink_ref[(2 * cp + half) * ATT_GROUP + r]
                m = jnp.maximum(jnp.max(sh, axis=-1, keepdims=True), sink)
                e = jnp.exp(sh - m)
                l = jnp.sum(e, axis=-1, keepdims=True) + jnp.exp(sink - m)
                ps.append((e / l).astype(BF16))
            p = jnp.concatenate(ps, axis=1)
            o_ref[:, c0:c0 + LANES] = _dot(p, vbd).astype(BF16)


def _attn_p(sinks, q, kv):
    nb = SEQ // WINDOW
    return pl.pallas_call(
        _attn_p_kernel,
        grid=(nb,),
        in_specs=[
            pl.BlockSpec(memory_space=pltpu.SMEM),
            pl.BlockSpec((WINDOW, ATT_WIDTH), lambda n: (n, 0)),
            pl.BlockSpec((WINDOW, 2 * KV_WIDTH), lambda n: (jnp.maximum(n - 1, 0), 0)),
            pl.BlockSpec((WINDOW, 2 * KV_WIDTH), lambda n: (n, 0)),
        ],
        out_specs=pl.BlockSpec((WINDOW, ATT_WIDTH), lambda n: (n, 0)),
        out_shape=jax.ShapeDtypeStruct((SEQ, ATT_WIDTH), BF16),
        compiler_params=_cparams(("arbitrary",)),
        name="attn_p",
    )(sinks, q, kv, kv)


ATT_S_BB = 8


def _attn_s_kernel(sink_ref, q_ref, kv32_ref, ck_ref, cv_ref, o_ref, ko_ref, vo_ref, q32_scr):
    t_new = DEC_SEQ
    w = WINDOW
    q32_scr[...] = q_ref[...].astype(F32)
    rows = 4 * 4 * 8
    row = lax.broadcasted_iota(jnp.int32, (rows, w), 0)
    slot = lax.broadcasted_iota(jnp.int32, (rows, w), 1)
    t_row = row % t_new
    second = (row % 8) >= t_new
    win_ok = (slot < w - t_new) | (slot - (w - t_new) <= t_row)
    old_ok = (slot >= 1) & (slot < t_new) & (slot > t_row)
    lane256 = lax.broadcasted_iota(jnp.int32, (32, 2 * LANES), 1)
    sink = sink_ref[...][:, 0:1]
    kv_new = kv32_ref[...]
    tok = lax.broadcasted_iota(jnp.int32, (ATT_S_BB * t_new, w), 0)
    tok_slot = lax.broadcasted_iota(jnp.int32, (ATT_S_BB * t_new, w), 1)
    new_slot = lax.broadcasted_iota(jnp.int32, (KV_WIDTH, w), 1) >= w - t_new
    for b in range(ATT_S_BB):
        place = (tok == t_new * b + tok_slot - (w - t_new)).astype(F32)
        cols = lax.dot_general(kv_new, place, TN_DIMS, precision=HI, preferred_element_type=F32)
        k_shift = pltpu.roll(ck_ref[b].reshape(KV_WIDTH, w), w - t_new, 1)
        v_shift = pltpu.roll(cv_ref[b].reshape(KV_WIDTH, w), w - t_new, 1)
        ko_ref[b] = jnp.where(new_slot, cols[:KV_WIDTH], k_shift).reshape(ATT_KV_HEADS, ATT_HEAD_DIM, w)
        vo_ref[b] = jnp.where(new_slot, cols[KV_WIDTH:], v_shift).reshape(ATT_KV_HEADS, ATT_HEAD_DIM, w)
    for pair in range(ATT_S_BB // 2):
        b0, b1 = 2 * pair, 2 * pair + 1
        q32 = jnp.concatenate([q32_scr[8 * pair:8 * (pair + 1), 256 * r:256 * (r + 1)] for r in range(ATT_GROUP)],
                              axis=0)
        qpad = jnp.concatenate(
            [jnp.where((lane256 // ATT_HEAD_DIM) == g, q32, 0.0) for g in range(ATT_KV_HEADS)], axis=0).astype(BF16)
        kw = [ko_ref[b].reshape(KV_WIDTH, w).astype(BF16) for b in (b0, b1)]
        vw = [vo_ref[b].reshape(KV_WIDTH, w).astype(BF16) for b in (b0, b1)]
        kc = [ck_ref[b].reshape(KV_WIDTH, w).astype(BF16) for b in (b0, b1)]
        vc = [cv_ref[b].reshape(KV_WIDTH, w).astype(BF16) for b in (b0, b1)]
        s_w = jnp.where(second, _dot(qpad, kw[1]), _dot(qpad, kw[0]))
        s_c = jnp.where(second, _dot(qpad, kc[1]), _dot(qpad, kc[0]))
        s_w = jnp.where(win_ok, s_w, -jnp.inf)
        s_c = jnp.where(old_ok, s_c, -jnp.inf)
        m = jnp.maximum(jnp.maximum(jnp.max(s_w, axis=-1, keepdims=True), jnp.max(s_c, axis=-1, keepdims=True)), sink)
        e_w = jnp.exp(s_w - m)
        e_c = jnp.exp(s_c - m)
        l = jnp.sum(e_w, axis=-1, keepdims=True) + jnp.sum(e_c, axis=-1, keepdims=True) + jnp.exp(sink - m)
        p_w = e_w / l
        p_c = e_c / l
        zero = jnp.zeros_like(p_w)
        o = (_dot_nt(jnp.where(second, zero, p_w).astype(BF16), vw[0])
             + _dot_nt(jnp.where(second, p_w, zero).astype(BF16), vw[1])
             + _dot_nt(jnp.where(second, zero, p_c).astype(BF16), vc[0])
             + _dot_nt(jnp.where(second, p_c, zero).astype(BF16), vc[1]))
        o32 = jnp.zeros((32, 2 * LANES), F32)
        for g in range(ATT_KV_HEADS):
            o32 = jnp.where((lane256 // ATT_HEAD_DIM) == g, o[32 * g:32 * (g + 1), :], o32)
        for r in range(ATT_GROUP):
            o_ref[8 * pair:8 * (pair + 1), 256 * r:256 * (r + 1)] = o32[8 * r:8 * (r + 1), :]


def _attn_s(sink_col, q, kv32, ck, cv):
    bb = ATT_S_BB
    rows = bb * DEC_SEQ
    cache_block = (bb, ATT_KV_HEADS, ATT_HEAD_DIM, WINDOW)
    cache_shape = (DEC_BATCH, ATT_KV_HEADS, ATT_HEAD_DIM, WINDOW)
    return pl.pallas_call(
        _attn_s_kernel,
        grid=(DEC_BATCH // bb,),
        in_specs=[
            pl.BlockSpec((128, LANES), lambda i: (0, 0)),
            pl.BlockSpec((rows, ATT_WIDTH), lambda i: (i, 0)),
            pl.BlockSpec((rows, 2 * KV_WIDTH), lambda i: (i, 0)),
            pl.BlockSpec(cache_block, lambda i: (i, 0, 0, 0)),
            pl.BlockSpec(cache_block, lambda i: (i, 0, 0, 0)),
        ],
        out_specs=[
            pl.BlockSpec((rows, ATT_WIDTH), lambda i: (i, 0)),
            pl.BlockSpec(cache_block, lambda i: (i, 0, 0, 0)),
            pl.BlockSpec(cache_block, lambda i: (i, 0, 0, 0)),
        ],
        out_shape=[
            jax.ShapeDtypeStruct((S_ROWS, ATT_WIDTH), F32),
            jax.ShapeDtypeStruct(cache_shape, F32),
            jax.ShapeDtypeStruct(cache_shape, F32),
        ],
        scratch_shapes=[pltpu.VMEM((rows, ATT_WIDTH), F32)],
        compiler_params=_cparams(("arbitrary",)),
        name="attn_s",
    )(sink_col, q, kv32, ck, cv)


def _head_norm_gate(h, nw, mo):
    hn = h * lax.rsqrt(jnp.mean(h * h, axis=-1, keepdims=True) + EPS) * nw
    return hn * _sigmoid(mo.astype(F32))


def _mlstm_p_kernel(q_ref, k_ref, v_ref, mo_ref, g_ref, gt_ref, brow_ref, bcol_ref, nw_ref,
                    mh_ref, c_out, n_out, m_out, c_scr, n_scr, m_scr):
    c = pl.program_id(0)
    L = MLSTM_CHUNK_P
    dh = MLSTM_HEAD_DIM

    @pl.when(c == 0)
    def _():
        c_scr[...] = jnp.zeros_like(c_scr)
        n_scr[...] = jnp.zeros_like(n_scr)
        m_scr[...] = jnp.zeros_like(m_scr)

    ti = lax.broadcasted_iota(jnp.int32, (L, L), 0)
    si = lax.broadcasted_iota(jnp.int32, (L, L), 1)
    causal = si <= ti
    tri = causal.astype(F32)
    tri_t = (ti <= si).astype(F32)
    gates = g_ref[...] + brow_ref[...]
    gates_t = gt_ref[...] + bcol_ref[...]
    b_col = jnp.dot(tri, _log_sigmoid(gates), precision=HI, preferred_element_type=F32)
    b_row = jnp.dot(_log_sigmoid(gates_t), tri_t, precision=HI, preferred_element_type=F32)
    for hd in range(MLSTM_HEADS):
        cs = slice(dh * hd, dh * (hd + 1))
        b_c = b_col[:, MLSTM_HEADS + hd:MLSTM_HEADS + hd + 1]
        li_c = gates[:, hd:hd + 1]
        b_r = b_row[MLSTM_HEADS + hd:MLSTM_HEADS + hd + 1, :]
        li_r = gates_t[hd:hd + 1, :]
        dm = jnp.where(causal, b_c - b_r + li_r, -jnp.inf)
        m_prev = m_scr[hd:hd + 1, 0:1]
        m_inter = b_c + m_prev
        m_t = jnp.maximum(m_inter, jnp.max(dm, axis=-1, keepdims=True))
        q = q_ref[:, cs] * (dh ** -0.5)
        k = k_ref[:, cs]
        v = v_ref[:, cs]
        sm = _dot_nt(q, k) * jnp.exp(dm - m_t)
        a = jnp.exp(m_inter - m_t)
        c_old = c_scr[hd]
        n_old = n_scr[hd:hd + 1, :]
        num = a * _dot_nt(q, c_old.astype(BF16)) + _dot(sm.astype(BF16), v)
        qn = jnp.sum(q.astype(F32) * n_old, axis=-1, keepdims=True)
        den = a * qn + jnp.sum(sm, axis=-1, keepdims=True)
        h = num / jnp.maximum(jnp.abs(den), jnp.exp(-m_t))
        mh_ref[:, cs] = _head_norm_gate(h, nw_ref[:, cs], mo_ref[:, cs]).astype(BF16)
        m_new = m_t[L - 1:L, :]
        b_last = b_c[L - 1:L, :]
        g = jnp.exp(b_last - b_c + li_c - m_new)
        decay = jnp.exp(b_last + m_prev - m_new)
        gv = (g * v.astype(F32)).astype(BF16)
        c_scr[hd] = decay * c_old + _dot_tn(gv, k)
        n_scr[hd:hd + 1, :] = decay * n_old + jnp.sum(g * k.astype(F32), axis=0, keepdims=True)
        m_scr[hd:hd + 1, :] = jnp.broadcast_to(m_new, (1, LANES))

    @pl.when(c == pl.num_programs(0) - 1)
    def _():
        c_out[...] = c_scr[...]
        n_out[...] = n_scr[...]
        m_out[...] = m_scr[...]


def _mlstm_p(m_all, gates, gates_t, brow, bcol, nw):
    L = MLSTM_CHUNK_P
    nc = SEQ // L
    dh = MLSTM_HEAD_DIM
    return pl.pallas_call(
        _mlstm_p_kernel,
        grid=(nc,),
        in_specs=[
            pl.BlockSpec((L, MLSTM_WIDTH), lambda c: (c, 0)),
            pl.BlockSpec((L, MLSTM_WIDTH), lambda c: (c, 1)),
            pl.BlockSpec((L, MLSTM_WIDTH), lambda c: (c, 2)),
            pl.BlockSpec((L, MLSTM_WIDTH), lambda c: (c, 3)),
            pl.BlockSpec((L, LANES), lambda c: (c, 0)),
            pl.BlockSpec((16, L), lambda c: (0, c)),
            pl.BlockSpec((1, LANES), lambda c: (0, 0)),
            pl.BlockSpec((16, L), lambda c: (0, 0)),
            pl.BlockSpec((1, MLSTM_WIDTH), lambda c: (0, 0)),
        ],
        out_specs=[
            pl.BlockSpec((L, MLSTM_WIDTH), lambda c: (c, 0)),
            pl.BlockSpec((MLSTM_HEADS, dh, dh), lambda c: (0, 0, 0)),
            pl.BlockSpec((8, dh), lambda c: (0, 0)),
            pl.BlockSpec((8, LANES), lambda c: (0, 0)),
        ],
        out_shape=[
            jax.ShapeDtypeStruct((SEQ, MLSTM_WIDTH), BF16),
            jax.ShapeDtypeStruct((MLSTM_HEADS, dh, dh), F32),
            jax.ShapeDtypeStruct((8, dh), F32),
            jax.ShapeDtypeStruct((8, LANES), F32),
        ],
        scratch_shapes=[
            pltpu.VMEM((MLSTM_HEADS, dh, dh), F32),
            pltpu.VMEM((8, dh), F32),
            pltpu.VMEM((8, LANES), F32),
        ],
        compiler_params=_cparams(("arbitrary",)),
        name="mlstm_p",
    )(m_all, m_all, m_all, m_all, gates, gates_t, brow, bcol, nw)


MLSTM_S_BB = 4


def _mlstm_s_kernel(q_ref, k_ref, v_ref, mo_ref, g_ref, nrep_ref, mrep_ref, brow_ref, nw_ref, c_ref,
                    mh_ref, c_out, nrow_ref, mrow_ref):
    T = DEC_SEQ
    R = MLSTM_S_BB * T
    H = MLSTM_HEADS
    dh = MLSTM_HEAD_DIM

    def shift(x, d):
        return pltpu.roll(x, d, 0)

    lanes = lax.broadcasted_iota(jnp.int32, (R, LANES), 1)
    tmod = lax.broadcasted_iota(jnp.int32, (R, LANES), 0) % T
    tmod_w = lax.broadcasted_iota(jnp.int32, (R, MLSTM_WIDTH), 0) % T
    head_ok = lanes < H
    gates = g_ref[...] + brow_ref[...]
    li = jnp.where(head_ok, gates, 0.0)
    lf = jnp.where(head_ok, pltpu.roll(_log_sigmoid(gates), LANES - H, 1), 0.0)
    bcum = lf
    for d in range(1, T):
        bcum = bcum + jnp.where(tmod >= d, shift(lf, d), 0.0)
    m0 = mrep_ref[...]
    m_inter = bcum + m0
    dms = [li] + [jnp.where(tmod >= d, bcum - shift(bcum, d) + shift(li, d), -jnp.inf) for d in range(1, T)]
    m_t = m_inter
    for dm in dms:
        m_t = jnp.maximum(m_t, dm)
    a = jnp.exp(m_inter - m_t)
    ws = [jnp.exp(dm - m_t) for dm in dms]

    q_bf = q_ref[...] * (dh ** -0.5)
    q = q_bf.astype(F32)
    k = k_ref[...].astype(F32)
    v = v_ref[...].astype(F32)
    seg = (lax.broadcasted_iota(jnp.int32, (MLSTM_WIDTH, LANES), 0) // dh
           == lax.broadcasted_iota(jnp.int32, (MLSTM_WIDTH, LANES), 1)).astype(F32)
    ex = (lax.broadcasted_iota(jnp.int32, (LANES, MLSTM_WIDTH), 1) // dh
          == lax.broadcasted_iota(jnp.int32, (LANES, MLSTM_WIDTH), 0)).astype(F32)

    def segsum(x):
        return jnp.dot(x, seg, precision=HI, preferred_element_type=F32)

    def expand(x):
        return jnp.dot(x, ex, precision=HI, preferred_element_type=F32)

    ks = [k] + [shift(k, d) for d in range(1, T)]
    vs = [v] + [shift(v, d) for d in range(1, T)]
    sms = [segsum(q * ks[d]) * ws[d] for d in range(T)]
    den = a * segsum(q * nrep_ref[...])
    for sm in sms:
        den = den + sm
    inv = 1.0 / jnp.maximum(jnp.abs(den), jnp.exp(-m_t))

    def last(x):
        out = jnp.zeros_like(x)
        for jj in range(T):
            out = jnp.where(tmod == T - 1 - jj, x if jj == 0 else pltpu.roll(x, R - jj, 0), out)
        return out

    m_new = last(m_t)
    b_last = last(bcum)
    g = jnp.where(head_ok, jnp.exp(b_last - bcum + li - m_new), 0.0)
    decay = jnp.where(head_ok, jnp.exp(b_last + m0 - m_new), 0.0)
    a_f = expand(a * inv)
    w_f = [expand(sm * inv) for sm in sms]
    g_f = expand(g)
    d_f = expand(decay)
    gv = (g_f * v).astype(BF16)
    rowb = lax.broadcasted_iota(jnp.int32, (R, dh), 0) // T
    for hd in range(H):
        cs = slice(dh * hd, dh * (hd + 1))
        qh = q_bf[:, cs]
        kh = k_ref[:, cs]
        gvh = gv[:, cs]
        qc = jnp.zeros((R, dh), F32)
        for bb in range(MLSTM_S_BB):
            c_old = c_ref[bb, hd]
            qc = jnp.where(rowb == bb, _dot_nt(qh, c_old.astype(BF16)), qc)
            upd = _dot_tn(jnp.where(rowb == bb, gvh, jnp.zeros_like(gvh)), kh)
            c_out[bb, hd] = d_f[T * bb:T * bb + 1, cs] * c_old + upd
        h = a_f[:, cs] * qc
        for d in range(T):
            h = h + w_f[d][:, cs] * vs[d][:, cs]
        mh_ref[:, cs] = _head_norm_gate(h, nw_ref[:, cs], mo_ref[:, cs]).astype(BF16)
    gk = g_f * k
    nsum = gk
    for d in range(1, T):
        nsum = nsum + jnp.where(tmod_w >= d, shift(gk, d), 0.0)
    nrow_ref[...] = d_f * nrep_ref[...] + nsum
    mrow_ref[...] = m_t


def _mlstm_s(m_all, gates, n_rep, m_rep, brow, nw, state_c):
    bb = MLSTM_S_BB
    R = bb * DEC_SEQ
    dh = MLSTM_HEAD_DIM
    H = MLSTM_HEADS
    return pl.pallas_call(
        _mlstm_s_kernel,
        grid=(DEC_BATCH // bb,),
        in_specs=[
            pl.BlockSpec((R, MLSTM_WIDTH), lambda i: (i, 0)),
            pl.BlockSpec((R, MLSTM_WIDTH), lambda i: (i, 1)),
            pl.BlockSpec((R, MLSTM_WIDTH), lambda i: (i, 2)),
            pl.BlockSpec((R, MLSTM_WIDTH), lambda i: (i, 3)),
            pl.BlockSpec((R, LANES), lambda i: (i, 0)),
            pl.BlockSpec((R, MLSTM_WIDTH), lambda i: (i, 0)),
            pl.BlockSpec((R, LANES), lambda i: (i, 0)),
            pl.BlockSpec((1, LANES), lambda i: (0, 0)),
            pl.BlockSpec((1, MLSTM_WIDTH), lambda i: (0, 0)),
            pl.BlockSpec((bb, H, dh, dh), lambda i: (i, 0, 0, 0)),
        ],
        out_specs=[
            pl.BlockSpec((R, MLSTM_WIDTH), lambda i: (i, 0)),
            pl.BlockSpec((bb, H, dh, dh), lambda i: (i, 0, 0, 0)),
            pl.BlockSpec((R, MLSTM_WIDTH), lambda i: (i, 0)),
            pl.BlockSpec((R, LANES), lambda i: (i, 0)),
        ],
        out_shape=[
            jax.ShapeDtypeStruct((S_ROWS, MLSTM_WIDTH), BF16),
            jax.ShapeDtypeStruct((DEC_BATCH, H, dh, dh), F32),
            jax.ShapeDtypeStruct((S_ROWS, MLSTM_WIDTH), F32),
            jax.ShapeDtypeStruct((S_ROWS, LANES), F32),
        ],
        compiler_params=_cparams(("arbitrary",)),
        name="mlstm_s",
    )(m_all, m_all, m_all, m_all, gates, n_rep, m_rep, brow, nw, state_c)


def _outproj_kernel(att_ref, mh_ref, x_ref, g1_ref, sh_ref, sc_ref, nw_ref, wa_ref, wm_ref, x1_ref, h2_ref, *,
                    per_row):
    y = _dot(att_ref[...].astype(BF16), wa_ref[...]) + _dot(mh_ref[...], wm_ref[...])
    x1 = x_ref[...] + _mod_row(g1_ref, per_row) * y
    x1_ref[...] = x1
    r = lax.rsqrt(jnp.mean(x1 * x1, axis=-1, keepdims=True) + EPS)
    h2 = (x1 * r * nw_ref[...]) * (1.0 + _mod_row(sc_ref, per_row)) + _mod_row(sh_ref, per_row)
    h2_ref[...] = h2.astype(BF16)


def _outproj(att, mh, x, mod, mod_row_block, norm_w, w_att, w_m, *, tm, per_row):
    rows = x.shape[0]
    mod_rows = tm if per_row else 8
    mod_idx = (lambda i: i) if per_row else (lambda i: mod_row_block)
    kern = functools.partial(_outproj_kernel, per_row=per_row)
    return pl.pallas_call(
        kern,
        grid=(rows // tm,),
        in_specs=[
            pl.BlockSpec((tm, ATT_WIDTH), lambda i: (i, 0)),
            pl.BlockSpec((tm, MLSTM_WIDTH), lambda i: (i, 0)),
            pl.BlockSpec((tm, D_MODEL), lambda i: (i, 0)),
            pl.BlockSpec((mod_rows, D_MODEL), lambda i: (mod_idx(i), 2)),
            pl.BlockSpec((mod_rows, D_MODEL), lambda i: (mod_idx(i), 3)),
            pl.BlockSpec((mod_rows, D_MODEL), lambda i: (mod_idx(i), 4)),
            pl.BlockSpec((1, D_MODEL), lambda i: (0, 0)),
            pl.BlockSpec((ATT_WIDTH, D_MODEL), lambda i: (0, 0)),
            pl.BlockSpec((MLSTM_WIDTH, D_MODEL), lambda i: (0, 0)),
        ],
        out_specs=[
            pl.BlockSpec((tm, D_MODEL), lambda i: (i, 0)),
            pl.BlockSpec((tm, D_MODEL), lambda i: (i, 0)),
        ],
        out_shape=[
            jax.ShapeDtypeStruct((rows, D_MODEL), F32),
            jax.ShapeDtypeStruct((rows, D_MODEL), BF16),
        ],
        compiler_params=_cparams(("arbitrary",)),
        name="outproj_s" if per_row else "outproj_p",
    )(att, mh, x, mod, mod, mod, norm_w, w_att, w_m)


def _ffn_a_kernel(h_ref, wg_ref, wu_ref, a_ref):
    h = h_ref[...]
    g = _dot(h, wg_ref[...].astype(BF16))
    u = _dot(h, wu_ref[...].astype(BF16))
    a_ref[...] = (g * _sigmoid(g) * u).astype(BF16)


def _ffn_a(h2, w_gate, w_up, *, tm):
    rows = h2.shape[0]
    tn = 512
    return pl.pallas_call(
        _ffn_a_kernel,
        grid=(rows // tm, D_FF // tn),
        in_specs=[
            pl.BlockSpec((tm, D_MODEL), lambda i, j: (i, 0)),
            pl.BlockSpec((D_MODEL, tn), lambda i, j: (0, j)),
            pl.BlockSpec((D_MODEL, tn), lambda i, j: (0, j)),
        ],
        out_specs=pl.BlockSpec((tm, tn), lambda i, j: (i, j)),
        out_shape=jax.ShapeDtypeStruct((rows, D_FF), BF16),
        compiler_params=_cparams(("arbitrary", "arbitrary")),
        name="ffn_a",
    )(h2, w_gate, w_up)


def _ffn_b_kernel(a_ref, wd_ref, x1_ref, g2_ref, fw_ref, y_ref, *, per_row):
    kk = pl.program_id(1)
    @pl.when(kk == 0)
    def _():
        y_ref[...] = _dot(a_ref[...], wd_ref[...].astype(BF16))

    @pl.when(kk > 0)
    def _():
        y_ref[...] += _dot(a_ref[...], wd_ref[...].astype(BF16))

    @pl.when(kk == pl.num_programs(1) - 1)
    def _():
        x2 = x1_ref[...] + _mod_row(g2_ref, per_row) * y_ref[...]
        y_ref[...] = x2 * lax.rsqrt(jnp.mean(x2 * x2, axis=-1, keepdims=True) + EPS) * fw_ref[...]


def _ffn_b(a, w_down, x1, mod, mod_row_block, final_w, *, tm, per_row):
    rows = a.shape[0]
    tk = 512
    mod_rows = tm if per_row else 8
    mod_idx = (lambda i: i) if per_row else (lambda i: mod_row_block)
    kern = functools.partial(_ffn_b_kernel, per_row=per_row)
    return pl.pallas_call(
        kern,
        grid=(rows // tm, D_FF // tk),
        in_specs=[
            pl.BlockSpec((tm, tk), lambda i, k: (i, k)),
            pl.BlockSpec((tk, D_MODEL), lambda i, k: (k, 0)),
            pl.BlockSpec((tm, D_MODEL), lambda i, k: (i, 0), pipeline_mode=pl.Buffered(1)),
            pl.BlockSpec((mod_rows, D_MODEL), lambda i, k: (mod_idx(i), 5)),
            pl.BlockSpec((1, D_MODEL), lambda i, k: (0, 0)),
        ],
        out_specs=pl.BlockSpec((tm, D_MODEL), lambda i, k: (i, 0)),
        out_shape=jax.ShapeDtypeStruct((rows, D_MODEL), F32),
        compiler_params=_cparams(("arbitrary", "arbitrary")),
        name="ffn_b_s" if per_row else "ffn_b_p",
    )(a, w_down, x1, mod, final_w)


def _rope_tables(pos):
    half = ROPE_DIM // 2
    inv = np.float32(ROPE_THETA) ** (-np.arange(0, ROPE_DIM, 2, dtype=np.float32) / np.float32(ROPE_DIM))
    d = np.arange(LANES) % ATT_HEAD_DIM
    ang = pos.astype(np.float32)[:, None] * inv[d % half][None, :].astype(np.float32)
    cos, sin = np.cos(ang), np.sin(ang)
    d = d[None, :]
    tables = (np.where(d < ROPE_DIM, cos, 1.0), np.where(d < half, -sin, 0.0),
              np.where((d >= half) & (d < ROPE_DIM), sin, 0.0))
    return tuple(jnp.asarray(t.astype(np.float32)) for t in tables)


def kernel(x_prompt, x_sample, cache_k_win, cache_v_win, state_C, state_n, state_m, c_prompt, c_sample,
           norm1_w, norm2_w, final_norm_w, w_ada, b_ada, w_in, b_ig, b_fg, attn_sinks, mh_norm_w,
           w_out, w_gate, w_up, w_down):
    assert w_in.shape[0] == 1, "single-layer trunk"
    T = DEC_SEQ
    xp = x_prompt[0]
    xs = x_sample.reshape(S_ROWS, D_MODEL)

    c_all = jnp.concatenate([jnp.repeat(c_sample, T, axis=0), c_prompt, jnp.zeros((15, D_MODEL), F32)], axis=0)
    mod = _ada(c_all, w_ada[0], b_ada)
    prompt_mod_block = S_ROWS // 8

    w_in_t = jnp.transpose(w_in[0])
    wq_t = (w_in_t[:ATT_WIDTH].reshape(ATT_KV_HEADS, ATT_GROUP, ATT_HEAD_DIM, D_MODEL)
            .transpose(1, 0, 2, 3).reshape(ATT_WIDTH, D_MODEL).astype(BF16))
    w_gates_t = w_in_t[MAIN_WIDTH:]
    wg = jnp.pad(w_gates_t.T, ((0, 0), (0, LANES - 2 * MLSTM_HEADS))).astype(BF16)
    wgt = jnp.pad(w_gates_t, ((0, 16 - 2 * MLSTM_HEADS), (0, 0))).astype(BF16)
    w_out_att = (w_out[0, :ATT_WIDTH].reshape(ATT_KV_HEADS, ATT_GROUP, ATT_HEAD_DIM, D_MODEL)
                 .transpose(1, 0, 2, 3).reshape(ATT_WIDTH, D_MODEL).astype(BF16))
    w_out_m = w_out[0, ATT_WIDTH:].astype(BF16)
    n1 = norm1_w.reshape(1, D_MODEL)
    n2 = norm2_w.reshape(1, D_MODEL)
    fw = final_norm_w.reshape(1, D_MODEL)
    nw = mh_norm_w.reshape(1, MLSTM_WIDTH)
    gate_bias = jnp.concatenate([b_ig[0], b_fg[0]])
    brow = jnp.pad(gate_bias, (0, LANES - 2 * MLSTM_HEADS)).reshape(1, LANES)
    bcol = jnp.broadcast_to(jnp.pad(gate_bias, (0, 16 - 2 * MLSTM_HEADS))[:, None], (16, MLSTM_CHUNK_P))

    rope_p = _rope_tables(np.arange(SEQ))
    rope_s = _rope_tables(np.tile(PAST_LEN + np.arange(T), DEC_BATCH))

    tm_p = 1024
    q_p, kv_p, kv32_p, m_p, g_p, gt_p = _inproj(xp, mod, prompt_mod_block, n1, wq_t, w_in_t, wg, wgt, *rope_p,
                                                tm=tm_p, per_row=False)
    q_s, kv_s, kv32_s, m_s, g_s, _ = _inproj(xs, mod, 0, n1, wq_t, w_in_t, wg, wgt, *rope_s,
                                              tm=S_ROWS, per_row=True)
    del kv_s

    sinks = attn_sinks[0]
    att_p = _attn_p(sinks, q_p, kv_p)
    sink_col = jnp.broadcast_to(sinks.reshape(ATT_HEADS, 1, 1), (ATT_HEADS, 8, LANES)).reshape(128, LANES)
    ck = jnp.transpose(cache_k_win[0], (0, 2, 3, 1))
    cv = jnp.transpose(cache_v_win[0], (0, 2, 3, 1))
    att_s, kwin_s, vwin_s = _attn_s(sink_col, q_s, kv32_s, ck, cv)

    mh_p, c_p, n_p, mm_p = _mlstm_p(m_p, g_p, gt_p, brow, bcol, nw)
    n_rep = jnp.repeat(state_n[0].reshape(DEC_BATCH, MLSTM_WIDTH), T, axis=0)
    m_rep = jnp.pad(jnp.repeat(state_m[0], T, axis=0), ((0, 0), (0, LANES - MLSTM_HEADS)))
    mh_s, c_s, nrow_s, mrow_s = _mlstm_s(m_s, g_s, n_rep, m_rep, brow, nw, state_C[0])

    x1_p, h2_p = _outproj(att_p, mh_p, xp, mod, prompt_mod_block, n2, w_out_att, w_out_m, tm=512, per_row=False)
    x1_s, h2_s = _outproj(att_s, mh_s, xs, mod, 0, n2, w_out_att, w_out_m, tm=S_ROWS, per_row=True)
    a_p = _ffn_a(h2_p, w_gate[0], w_up[0], tm=tm_p)
    a_s = _ffn_a(h2_s, w_gate[0], w_up[0], tm=S_ROWS)
    y_p = _ffn_b(a_p, w_down[0], x1_p, mod, prompt_mod_block, fw, tm=tm_p, per_row=False)
    y_s = _ffn_b(a_s, w_down[0], x1_s, mod, 0, fw, tm=S_ROWS, per_row=True)

    kv_shape = (1, 1, WINDOW, ATT_KV_HEADS, ATT_HEAD_DIM)
    kv_last = kv32_p[tm_p - WINDOW:]
    dh = MLSTM_HEAD_DIM
    return (
        y_p.reshape(1, SEQ, D_MODEL),
        y_s.reshape(DEC_BATCH, T, D_MODEL),
        kv_last[:, :KV_WIDTH].reshape(kv_shape),
        kv_last[:, KV_WIDTH:].reshape(kv_shape),
        c_p.reshape(1, 1, MLSTM_HEADS, dh, dh),
        n_p[:MLSTM_HEADS].reshape(1, 1, MLSTM_HEADS, dh),
        mm_p[:MLSTM_HEADS, 0].reshape(1, 1, MLSTM_HEADS),
        jnp.transpose(kwin_s, (0, 3, 1, 2))[None],
        jnp.transpose(vwin_s, (0, 3, 1, 2))[None],
        c_s.reshape(1, DEC_BATCH, MLSTM_HEADS, dh, dh),
        nrow_s[T - 1::T].reshape(1, DEC_BATCH, MLSTM_HEADS, dh),
        mrow_s[T - 1::T, :MLSTM_HEADS].reshape(1, DEC_BATCH, MLSTM_HEADS),
    )
```

```python
import functools

import jax
import jax.numpy as jnp
import numpy as np
from jax import lax
from jax.experimental import pallas as pl
from jax.experimental.pallas import tpu as pltpu

F32 = jnp.float32
BF16 = jnp.bfloat16

D_MODEL = 2048
SEQ = 8192
DEC_BATCH = 128
DEC_SEQ = 4
S_ROWS = DEC_BATCH * DEC_SEQ
PAST_LEN = 16384
ATT_HEADS = 16
ATT_KV_HEADS = 4
ATT_GROUP = 4
ATT_HEAD_DIM = 64
WINDOW = 128
ROPE_THETA = 500000.0
ROPE_DIM = 16
MLSTM_HEADS = 4
MLSTM_HEAD_DIM = 256
ATT_WIDTH = 1024
KV_WIDTH = 256
MLSTM_WIDTH = 1024
MAIN_WIDTH = ATT_WIDTH + 2 * KV_WIDTH + 4 * MLSTM_WIDTH
D_FF = 5632
N_MOD = 6
EPS = 1e-6

LANES = 128
MLSTM_CHUNK_P = 256
VMEM_LIMIT = 56 * 1024 * 1024

NT_DIMS = (((1,), (1,)), ((), ()))
TN_DIMS = (((0,), (0,)), ((), ()))
HI = lax.Precision.HIGHEST


def _cparams(sem):
    return pltpu.CompilerParams(dimension_semantics=sem, vmem_limit_bytes=VMEM_LIMIT)


def _dot(a, b):
    return jnp.dot(a, b, preferred_element_type=F32)


def _dot_nt(a, b):
    return lax.dot_general(a, b, NT_DIMS, preferred_element_type=F32)


def _dot_tn(a, b):
    return lax.dot_general(a, b, TN_DIMS, preferred_element_type=F32)


def _sigmoid(x):
    return 1.0 / (1.0 + jnp.exp(-x))


def _log_sigmoid(x):
    return jnp.minimum(x, 0.0) - jnp.log(1.0 + jnp.exp(-jnp.abs(x)))


def _mod_row(ref, per_row):
    return ref[...] if per_row else ref[0:1, :]


def _ada_kernel(c_ref, w_ref, b_ref, o_ref, s_scr):
    @pl.when(pl.program_id(0) == 0)
    def _():
        c = c_ref[...]
        s_scr[...] = (c * _sigmoid(c)).astype(BF16)

    o_ref[...] = _dot(s_scr[...], w_ref[...].astype(BF16)) + b_ref[...]


def _ada(c_all, w_ada, b_ada):
    m = c_all.shape[0]
    n = w_ada.shape[1]
    tn = 1024
    return pl.pallas_call(
        _ada_kernel,
        grid=(n // tn,),
        in_specs=[
            pl.BlockSpec((m, D_MODEL), lambda j: (0, 0)),
            pl.BlockSpec((D_MODEL, tn), lambda j: (0, j)),
            pl.BlockSpec((1, tn), lambda j: (0, j)),
        ],
        out_specs=pl.BlockSpec((m, tn), lambda j: (0, j)),
        out_shape=jax.ShapeDtypeStruct((m, n), F32),
        scratch_shapes=[pltpu.VMEM((m, D_MODEL), BF16)],
        compiler_params=_cparams(("arbitrary",)),
        name="ada",
    )(c_all, w_ada, b_ada)


def _rope_store(acc, cos, sa, sb, out_ref, ncols, scale):
    for c in range(ncols // LANES):
        xc = acc[:, LANES * c:LANES * (c + 1)]
        rot = xc * cos + pltpu.roll(xc, LANES - 8, 1) * sa + pltpu.roll(xc, 8, 1) * sb
        if scale != 1.0:
            rot = rot * scale
        out_ref[:, LANES * c:LANES * (c + 1)] = rot.astype(out_ref.dtype)


def _inproj_kernel(x_ref, sh_ref, sc_ref, nw_ref, wq_ref, win_ref, wg_ref, wgt_ref, cos_ref, sa_ref, sb_ref,
                   q_ref, kv_ref, kv32_ref, m_ref, g_ref, gt_ref, h_scr, *, per_row):
    j = pl.program_id(1)

    @pl.when(j == 0)
    def _():
        x = x_ref[...]
        r = lax.rsqrt(jnp.mean(x * x, axis=-1, keepdims=True) + EPS)
        y = x * r * nw_ref[...]
        h = (y * (1.0 + _mod_row(sc_ref, per_row)) + _mod_row(sh_ref, per_row)).astype(BF16)
        h_scr[...] = h
        g_ref[...] = _dot(h, wg_ref[...])
        gt_ref[...] = _dot_nt(wgt_ref[...], h)

    @pl.when(j < 2)
    def _():
        acc = _dot_nt(h_scr[...], wq_ref[...])
        _rope_store(acc, cos_ref[...], sa_ref[...], sb_ref[...], q_ref, 512, ATT_HEAD_DIM ** -0.5)

    @pl.when(j == 2)
    def _():
        acc = _dot_nt(h_scr[...], win_ref[...].astype(BF16))
        _rope_store(acc, cos_ref[...], sa_ref[...], sb_ref[...], kv32_ref, KV_WIDTH, 1.0)
        kv32_ref[:, KV_WIDTH:] = acc[:, KV_WIDTH:]
        kv_ref[...] = kv32_ref[...].astype(BF16)

    @pl.when(j > 2)
    def _():
        m_ref[...] = _dot_nt(h_scr[...], win_ref[...].astype(BF16)).astype(BF16)


def _inproj(x, mod, mod_row_block, norm_w, wq_t, w_in_t, wg, wgt, cos, sa, sb, *, tm, per_row):
    rows = x.shape[0]
    tn = 512
    nj = MAIN_WIDTH // tn
    mod_rows = tm if per_row else 8
    mod_idx = (lambda i: i) if per_row else (lambda i: mod_row_block)
    kern = functools.partial(_inproj_kernel, per_row=per_row)
    return pl.pallas_call(
        kern,
        grid=(rows // tm, nj),
        in_specs=[
            pl.BlockSpec((tm, D_MODEL), lambda i, j: (i, 0)),
            pl.BlockSpec((mod_rows, D_MODEL), lambda i, j: (mod_idx(i), 0)),
            pl.BlockSpec((mod_rows, D_MODEL), lambda i, j: (mod_idx(i), 1)),
            pl.BlockSpec((1, D_MODEL), lambda i, j: (0, 0)),
            pl.BlockSpec((tn, D_MODEL), lambda i, j: (jnp.minimum(j, 1), 0)),
            pl.BlockSpec((tn, D_MODEL), lambda i, j: (jnp.maximum(j, 2), 0)),
            pl.BlockSpec((D_MODEL, LANES), lambda i, j: (0, 0)),
            pl.BlockSpec((16, D_MODEL), lambda i, j: (0, 0)),
            pl.BlockSpec((tm, LANES), lambda i, j: (i, 0)),
            pl.BlockSpec((tm, LANES), lambda i, j: (i, 0)),
            pl.BlockSpec((tm, LANES), lambda i, j: (i, 0)),
        ],
        out_specs=[
            pl.BlockSpec((tm, tn), lambda i, j: (i, jnp.minimum(j, 1))),
            pl.BlockSpec((tm, tn), lambda i, j: (i, 0)),
            pl.BlockSpec((tm, tn), lambda i, j: (0, 0)),
            pl.BlockSpec((tm, tn), lambda i, j: (i, jnp.clip(j - 3, 0, 7))),
            pl.BlockSpec((tm, LANES), lambda i, j: (i, 0)),
            pl.BlockSpec((16, tm), lambda i, j: (0, i)),
        ],
        out_shape=[
            jax.ShapeDtypeStruct((rows, ATT_WIDTH), BF16),
            jax.ShapeDtypeStruct((rows, 2 * KV_WIDTH), BF16),
            jax.ShapeDtypeStruct((tm, 2 * KV_WIDTH), F32),
            jax.ShapeDtypeStruct((rows, 4 * MLSTM_WIDTH), BF16),
            jax.ShapeDtypeStruct((rows, LANES), F32),
            jax.ShapeDtypeStruct((16, rows), F32),
        ],
        scratch_shapes=[pltpu.VMEM((tm, D_MODEL), BF16)],
        compiler_params=_cparams(("arbitrary", "arbitrary")),
        name="inproj_s" if per_row else "inproj_p",
    )(x, mod, mod, norm_w, wq_t, w_in_t, wg, wgt, cos, sa, sb)


def _attn_p_kernel(sink_ref, q_ref, kvp_ref, kvc_ref, o_ref):
    n = pl.program_id(0)
    w = WINDOW
    grp = ATT_GROUP
    rows = grp * w
    qi = lax.broadcasted_iota(jnp.int32, (rows, 2 * w), 0) % w
    kj = lax.broadcasted_iota(jnp.int32, (rows, 2 * w), 1)
    prev_off = jnp.where(n > 0, 0, 4 * w)
    allowed = ((kj < w) & (kj > qi + prev_off)) | ((kj >= w) & (kj - w <= qi))
    member = lax.broadcasted_iota(jnp.int32, (rows, 1), 0) // w
    low = lax.broadcasted_iota(jnp.int32, (2 * w, LANES), 1) < ATT_HEAD_DIM
    low_o = lax.broadcasted_iota(jnp.int32, (rows, LANES), 1) < ATT_HEAD_DIM
    key_row = lax.broadcasted_iota(jnp.int32, (4 * w, LANES), 0)
    key_lane = lax.broadcasted_iota(jnp.int32, (4 * w, LANES), 1)
    ones_bd = (((key_row < 2 * w) & (key_lane < ATT_HEAD_DIM))
               | ((key_row >= 2 * w) & (key_lane >= ATT_HEAD_DIM))).astype(BF16)
    kv2 = jnp.concatenate([kvp_ref[...], kvc_ref[...]], axis=0)
    zero = jnp.zeros((2 * w, LANES), BF16)
    for cp in range(2):
        k128 = kv2[:, LANES * cp:LANES * (cp + 1)]
        v128 = kv2[:, KV_WIDTH + LANES * cp:KV_WIDTH + LANES * (cp + 1)]
        kbd = jnp.concatenate([jnp.where(low, k128, zero), jnp.where(low, zero, k128)], axis=0)
        vbd = jnp.concatenate([jnp.where(low, v128, zero), jnp.where(low, zero, v128)], axis=0)
        v_aug = jnp.concatenate([vbd, ones_bd], axis=1)
        q4 = jnp.concatenate([q_ref[:, 256 * r + LANES * cp:256 * r + LANES * (cp + 1)] for r in range(grp)], axis=0)
        s = _dot_nt(q4, kbd)
        es, tails = [], []
        for half in range(2):
            sh = jnp.where(allowed, s[:, 2 * w * half:2 * w * (half + 1)], -jnp.inf)
            head0 = (2 * cp + half) * grp
            sink = jnp.full((rows, 1), sink_ref[head0], F32)
            for r in range(1, grp):
                sink = jnp.where(member == r, sink_ref[head0 + r], sink)
            m = jnp.maximum(jnp.max(sh, axis=-1, keepdims=True), sink)
            es.append(jnp.exp(sh - m).astype(BF16))
            tails.append(jnp.exp(sink - m))
        oa = _dot(jnp.concatenate(es, axis=1), v_aug)
        l = oa[:, LANES:] + jnp.where(low_o, tails[0], tails[1])
        o = (oa[:, :LANES] / l).astype(BF16)
        for r in range(grp):
            o_ref[:, 256 * r + LANES * cp:256 * r + LANES * (cp + 1)] = o[w * r:w * (r + 1)]


def _attn_p(sinks, q, kv):
    nb = SEQ // WINDOW
    return pl.pallas_call(
        _attn_p_kernel,
        grid=(nb,),
        in_specs=[
            pl.BlockSpec(memory_space=pltpu.SMEM),
            pl.BlockSpec((WINDOW, ATT_WIDTH), lambda n: (n, 0)),
            pl.BlockSpec((WINDOW, 2 * KV_WIDTH), lambda n: (jnp.maximum(n - 1, 0), 0)),
            pl.BlockSpec((WINDOW, 2 * KV_WIDTH), lambda n: (n, 0)),
        ],
        out_specs=pl.BlockSpec((WINDOW, ATT_WIDTH), lambda n: (n, 0)),
        out_shape=jax.ShapeDtypeStruct((SEQ, ATT_WIDTH), BF16),
        compiler_params=_cparams(("arbitrary",)),
        name="attn_p",
    )(sinks, q, kv, kv)


ATT_S_BB = 8


def _attn_s_kernel(sink_ref, q_ref, kv32_ref, ck_ref, cv_ref, o_ref, ko_ref, vo_ref, q32_scr):
    t_new = DEC_SEQ
    w = WINDOW
    q32_scr[...] = q_ref[...].astype(F32)
    rows = 4 * 4 * 8
    row = lax.broadcasted_iota(jnp.int32, (rows, w), 0)
    slot = lax.broadcasted_iota(jnp.int32, (rows, w), 1)
    t_row = row % t_new
    second = (row % 8) >= t_new
    win_ok = (slot < w - t_new) | (slot - (w - t_new) <= t_row)
    old_ok = (slot >= 1) & (slot < t_new) & (slot > t_row)
    lane256 = lax.broadcasted_iota(jnp.int32, (32, 2 * LANES), 1)
    sink = sink_ref[...][:, 0:1]
    kv_new = jnp.concatenate([kv32_ref[...], jnp.zeros((w - ATT_S_BB * t_new, 2 * KV_WIDTH), F32)], axis=0)
    kv_t = kv_new.T
    new_slot = lax.broadcasted_iota(jnp.int32, (KV_WIDTH, w), 1) >= w - t_new
    for b in range(ATT_S_BB):
        cols = pltpu.roll(kv_t, w - t_new - t_new * b, 1)
        k_shift = pltpu.roll(ck_ref[b].reshape(KV_WIDTH, w), w - t_new, 1)
        v_shift = pltpu.roll(cv_ref[b].reshape(KV_WIDTH, w), w - t_new, 1)
        ko_ref[b] = jnp.where(new_slot, cols[:KV_WIDTH], k_shift).reshape(ATT_KV_HEADS, ATT_HEAD_DIM, w)
        vo_ref[b] = jnp.where(new_slot, cols[KV_WIDTH:], v_shift).reshape(ATT_KV_HEADS, ATT_HEAD_DIM, w)
    for pair in range(ATT_S_BB // 2):
        b0, b1 = 2 * pair, 2 * pair + 1
        q32 = jnp.concatenate([q32_scr[8 * pair:8 * (pair + 1), 256 * r:256 * (r + 1)] for r in range(ATT_GROUP)],
                              axis=0)
        qpad = jnp.concatenate(
            [jnp.where((lane256 // ATT_HEAD_DIM) == g, q32, 0.0) for g in range(ATT_KV_HEADS)], axis=0).astype(BF16)
        kw = [ko_ref[b].reshape(KV_WIDTH, w).astype(BF16) for b in (b0, b1)]
        vw = [vo_ref[b].reshape(KV_WIDTH, w).astype(BF16) for b in (b0, b1)]
        kc = [ck_ref[b].reshape(KV_WIDTH, w).astype(BF16) for b in (b0, b1)]
        vc = [cv_ref[b].reshape(KV_WIDTH, w).astype(BF16) for b in (b0, b1)]
        s_w = jnp.where(second, _dot(qpad, kw[1]), _dot(qpad, kw[0]))
        s_c = jnp.where(second, _dot(qpad, kc[1]), _dot(qpad, kc[0]))
        s_w = jnp.where(win_ok, s_w, -jnp.inf)
        s_c = jnp.where(old_ok, s_c, -jnp.inf)
        m = jnp.maximum(jnp.maximum(jnp.max(s_w, axis=-1, keepdims=True), jnp.max(s_c, axis=-1, keepdims=True)), sink)
        e_w = jnp.exp(s_w - m)
        e_c = jnp.exp(s_c - m)
        l = jnp.sum(e_w, axis=-1, keepdims=True) + jnp.sum(e_c, axis=-1, keepdims=True) + jnp.exp(sink - m)
        p_w = e_w / l
        p_c = e_c / l
        zero = jnp.zeros_like(p_w)
        o = (_dot_nt(jnp.where(second, zero, p_w).astype(BF16), vw[0])
             + _dot_nt(jnp.where(second, p_w, zero).astype(BF16), vw[1])
             + _dot_nt(jnp.where(second, zero, p_c).astype(BF16), vc[0])
             + _dot_nt(jnp.where(second, p_c, zero).astype(BF16), vc[1]))
        o32 = jnp.zeros((32, 2 * LANES), F32)
        for g in range(ATT_KV_HEADS):
            o32 = jnp.where((lane256 // ATT_HEAD_DIM) == g, o[32 * g:32 * (g + 1), :], o32)
        for r in range(ATT_GROUP):
            o_ref[8 * pair:8 * (pair + 1), 256 * r:256 * (r + 1)] = o32[8 * r:8 * (r + 1), :]


def _attn_s(sink_col, q, kv32, ck, cv):
    bb = ATT_S_BB
    rows = bb * DEC_SEQ
    cache_block = (bb, ATT_KV_HEADS, ATT_HEAD_DIM, WINDOW)
    cache_shape = (DEC_BATCH, ATT_KV_HEADS, ATT_HEAD_DIM, WINDOW)
    return pl.pallas_call(
        _attn_s_kernel,
        grid=(DEC_BATCH // bb,),
        in_specs=[
            pl.BlockSpec((128, LANES), lambda i: (0, 0)),
            pl.BlockSpec((rows, ATT_WIDTH), lambda i: (i, 0)),
            pl.BlockSpec((rows, 2 * KV_WIDTH), lambda i: (i, 0)),
            pl.BlockSpec(cache_block, lambda i: (i, 0, 0, 0)),
            pl.BlockSpec(cache_block, lambda i: (i, 0, 0, 0)),
        ],
        out_specs=[
            pl.BlockSpec((rows, ATT_WIDTH), lambda i: (i, 0)),
            pl.BlockSpec(cache_block, lambda i: (i, 0, 0, 0)),
            pl.BlockSpec(cache_block, lambda i: (i, 0, 0, 0)),
        ],
        out_shape=[
            jax.ShapeDtypeStruct((S_ROWS, ATT_WIDTH), F32),
            jax.ShapeDtypeStruct(cache_shape, F32),
            jax.ShapeDtypeStruct(cache_shape, F32),
        ],
        scratch_shapes=[pltpu.VMEM((rows, ATT_WIDTH), F32)],
        compiler_params=_cparams(("arbitrary",)),
        name="attn_s",
    )(sink_col, q, kv32, ck, cv)


def _head_norm_gate(h, nw, mo):
    hn = h * lax.rsqrt(jnp.mean(h * h, axis=-1, keepdims=True) + EPS) * nw
    return hn * _sigmoid(mo.astype(F32))


def _mlstm_p_kernel(q_ref, k_ref, v_ref, mo_ref, g_ref, gt_ref, brow_ref, bcol_ref, nw_ref,
                    mh_ref, c_out, n_out, m_out, c_scr, n_scr, m_scr):
    c = pl.program_id(0)
    L = MLSTM_CHUNK_P
    dh = MLSTM_HEAD_DIM

    @pl.when(c == 0)
    def _():
        c_scr[...] = jnp.zeros_like(c_scr)
        n_scr[...] = jnp.zeros_like(n_scr)
        m_scr[...] = jnp.zeros_like(m_scr)

    ti = lax.broadcasted_iota(jnp.int32, (L, L), 0)
    si = lax.broadcasted_iota(jnp.int32, (L, L), 1)
    causal = si <= ti
    tri = causal.astype(F32)
    tri_t = (ti <= si).astype(F32)
    gates = g_ref[...] + brow_ref[...]
    gates_t = gt_ref[...] + bcol_ref[...]
    b_col = jnp.dot(tri, _log_sigmoid(gates), precision=HI, preferred_element_type=F32)
    b_row = jnp.dot(_log_sigmoid(gates_t), tri_t, precision=HI, preferred_element_type=F32)
    for hd in range(MLSTM_HEADS):
        cs = slice(dh * hd, dh * (hd + 1))
        b_c = b_col[:, MLSTM_HEADS + hd:MLSTM_HEADS + hd + 1]
        li_c = gates[:, hd:hd + 1]
        b_r = b_row[MLSTM_HEADS + hd:MLSTM_HEADS + hd + 1, :]
        li_r = gates_t[hd:hd + 1, :]
        dm = jnp.where(causal, b_c - b_r + li_r, -jnp.inf)
        m_prev = m_scr[hd:hd + 1, 0:1]
        m_inter = b_c + m_prev
        m_t = jnp.maximum(m_inter, jnp.max(dm, axis=-1, keepdims=True))
        q = q_ref[:, cs] * (dh ** -0.5)
        k = k_ref[:, cs]
        v = v_ref[:, cs]
        sm = _dot_nt(q, k) * jnp.exp(dm - m_t)
        a = jnp.exp(m_inter - m_t)
        c_old = c_scr[hd]
        n_old = n_scr[hd:hd + 1, :]
        num = a * _dot_nt(q, c_old.astype(BF16)) + _dot(sm.astype(BF16), v)
        qn = jnp.sum(q.astype(F32) * n_old, axis=-1, keepdims=True)
        den = a * qn + jnp.sum(sm, axis=-1, keepdims=True)
        h = num / jnp.maximum(jnp.abs(den), jnp.exp(-m_t))
        mh_ref[:, cs] = _head_norm_gate(h, nw_ref[:, cs], mo_ref[:, cs]).astype(BF16)
        m_new = m_t[L - 1:L, :]
        b_last = b_c[L - 1:L, :]
        g = jnp.exp(b_last - b_c + li_c - m_new)
        decay = jnp.exp(b_last + m_prev - m_new)
        gv = (g * v.astype(F32)).astype(BF16)
        c_scr[hd] = decay * c_old + _dot_tn(gv, k)
        n_scr[hd:hd + 1, :] = decay * n_old + jnp.sum(g * k.astype(F32), axis=0, keepdims=True)
        m_scr[hd:hd + 1, :] = jnp.broadcast_to(m_new, (1, LANES))

    @pl.when(c == pl.num_programs(0) - 1)
    def _():
        c_out[...] = c_scr[...]
        n_out[...] = n_scr[...]
        m_out[...] = m_scr[...]


def _mlstm_p(m_all, gates, gates_t, brow, bcol, nw):
    L = MLSTM_CHUNK_P
    nc = SEQ // L
    dh = MLSTM_HEAD_DIM
    return pl.pallas_call(
        _mlstm_p_kernel,
        grid=(nc,),
        in_specs=[
            pl.BlockSpec((L, MLSTM_WIDTH), lambda c: (c, 0)),
            pl.BlockSpec((L, MLSTM_WIDTH), lambda c: (c, 1)),
            pl.BlockSpec((L, MLSTM_WIDTH), lambda c: (c, 2)),
            pl.BlockSpec((L, MLSTM_WIDTH), lambda c: (c, 3)),
            pl.BlockSpec((L, LANES), lambda c: (c, 0)),
            pl.BlockSpec((16, L), lambda c: (0, c)),
            pl.BlockSpec((1, LANES), lambda c: (0, 0)),
            pl.BlockSpec((16, L), lambda c: (0, 0)),
            pl.BlockSpec((1, MLSTM_WIDTH), lambda c: (0, 0)),
        ],
        out_specs=[
            pl.BlockSpec((L, MLSTM_WIDTH), lambda c: (c, 0)),
            pl.BlockSpec((MLSTM_HEADS, dh, dh), lambda c: (0, 0, 0)),
            pl.BlockSpec((8, dh), lambda c: (0, 0)),
            pl.BlockSpec((8, LANES), lambda c: (0, 0)),
        ],
        out_shape=[
            jax.ShapeDtypeStruct((SEQ, MLSTM_WIDTH), BF16),
            jax.ShapeDtypeStruct((MLSTM_HEADS, dh, dh), F32),
            jax.ShapeDtypeStruct((8, dh), F32),
            jax.ShapeDtypeStruct((8, LANES), F32),
        ],
        scratch_shapes=[
            pltpu.VMEM((MLSTM_HEADS, dh, dh), F32),
            pltpu.VMEM((8, dh), F32),
            pltpu.VMEM((8, LANES), F32),
        ],
        compiler_params=_cparams(("arbitrary",)),
        name="mlstm_p",
    )(m_all, m_all, m_all, m_all, gates, gates_t, brow, bcol, nw)


MLSTM_S_BB = 4


def _mlstm_s_kernel(q_ref, k_ref, v_ref, mo_ref, g_ref, nrep_ref, mrep_ref, brow_ref, nw_ref, c_ref,
                    mh_ref, c_out, nrow_ref, mrow_ref):
    T = DEC_SEQ
    R = MLSTM_S_BB * T
    H = MLSTM_HEADS
    dh = MLSTM_HEAD_DIM

    def shift(x, d):
        return pltpu.roll(x, d, 0)

    lanes = lax.broadcasted_iota(jnp.int32, (R, LANES), 1)
    tmod = lax.broadcasted_iota(jnp.int32, (R, LANES), 0) % T
    tmod_w = lax.broadcasted_iota(jnp.int32, (R, MLSTM_WIDTH), 0) % T
    head_ok = lanes < H
    gates = g_ref[...] + brow_ref[...]
    li = jnp.where(head_ok, gates, 0.0)
    lf = jnp.where(head_ok, pltpu.roll(_log_sigmoid(gates), LANES - H, 1), 0.0)
    bcum = lf
    for d in range(1, T):
        bcum = bcum + jnp.where(tmod >= d, shift(lf, d), 0.0)
    m0 = mrep_ref[...]
    m_inter = bcum + m0
    dms = [li] + [jnp.where(tmod >= d, bcum - shift(bcum, d) + shift(li, d), -jnp.inf) for d in range(1, T)]
    m_t = m_inter
    for dm in dms:
        m_t = jnp.maximum(m_t, dm)
    a = jnp.exp(m_inter - m_t)
    ws = [jnp.exp(dm - m_t) for dm in dms]

    q_bf = q_ref[...] * (dh ** -0.5)
    q = q_bf.astype(F32)
    k = k_ref[...].astype(F32)
    v = v_ref[...].astype(F32)
    seg = (lax.broadcasted_iota(jnp.int32, (MLSTM_WIDTH, LANES), 0) // dh
           == lax.broadcasted_iota(jnp.int32, (MLSTM_WIDTH, LANES), 1)).astype(F32)
    ex = (lax.broadcasted_iota(jnp.int32, (LANES, MLSTM_WIDTH), 1) // dh
          == lax.broadcasted_iota(jnp.int32, (LANES, MLSTM_WIDTH), 0)).astype(F32)

    def segsum(x):
        return jnp.dot(x, seg, precision=HI, preferred_element_type=F32)

    def expand(x):
        return jnp.dot(x, ex, precision=HI, preferred_element_type=F32)

    ks = [k] + [shift(k, d) for d in range(1, T)]
    vs = [v] + [shift(v, d) for d in range(1, T)]
    sms = [segsum(q * ks[d]) * ws[d] for d in range(T)]
    den = a * segsum(q * nrep_ref[...])
    for sm in sms:
        den = den + sm
    inv = 1.0 / jnp.maximum(jnp.abs(den), jnp.exp(-m_t))

    def last(x):
        out = jnp.zeros_like(x)
        for jj in range(T):
            out = jnp.where(tmod == T - 1 - jj, x if jj == 0 else pltpu.roll(x, R - jj, 0), out)
        return out

    m_new = last(m_t)
    b_last = last(bcum)
    g = jnp.where(head_ok, jnp.exp(b_last - bcum + li - m_new), 0.0)
    decay = jnp.where(head_ok, jnp.exp(b_last + m0 - m_new), 0.0)
    a_f = expand(a * inv)
    w_f = [expand(sm * inv) for sm in sms]
    g_f = expand(g)
    d_f = expand(decay)
    gv = (g_f * v).astype(BF16)
    rowb = lax.broadcasted_iota(jnp.int32, (R, dh), 0) // T
    for hd in range(H):
        cs = slice(dh * hd, dh * (hd + 1))
        qh = q_bf[:, cs]
        kh = k_ref[:, cs]
        gvh = gv[:, cs]
        qc = jnp.zeros((R, dh), F32)
        for bb in range(MLSTM_S_BB):
            c_old = c_ref[bb, hd]
            qc = jnp.where(rowb == bb, _dot_nt(qh, c_old.astype(BF16)), qc)
            upd = _dot_tn(jnp.where(rowb == bb, gvh, jnp.zeros_like(gvh)), kh)
            c_out[bb, hd] = d_f[T * bb:T * bb + 1, cs] * c_old + upd
        h = a_f[:, cs] * qc
        for d in range(T):
            h = h + w_f[d][:, cs] * vs[d][:, cs]
        mh_ref[:, cs] = _head_norm_gate(h, nw_ref[:, cs], mo_ref[:, cs]).astype(BF16)
    gk = g_f * k
    nsum = gk
    for d in range(1, T):
        nsum = nsum + jnp.where(tmod_w >= d, shift(gk, d), 0.0)
    nrow_ref[...] = d_f * nrep_ref[...] + nsum
    mrow_ref[...] = m_t


def _mlstm_s(m_all, gates, n_rep, m_rep, brow, nw, state_c):
    bb = MLSTM_S_BB
    R = bb * DEC_SEQ
    dh = MLSTM_HEAD_DIM
    H = MLSTM_HEADS
    return pl.pallas_call(
        _mlstm_s_kernel,
        grid=(DEC_BATCH // bb,),
        in_specs=[
            pl.BlockSpec((R, MLSTM_WIDTH), lambda i: (i, 0)),
            pl.BlockSpec((R, MLSTM_WIDTH), lambda i: (i, 1)),
            pl.BlockSpec((R, MLSTM_WIDTH), lambda i: (i, 2)),
            pl.BlockSpec((R, MLSTM_WIDTH), lambda i: (i, 3)),
            pl.BlockSpec((R, LANES), lambda i: (i, 0)),
            pl.BlockSpec((R, MLSTM_WIDTH), lambda i: (i, 0)),
            pl.BlockSpec((R, LANES), lambda i: (i, 0)),
            pl.BlockSpec((1, LANES), lambda i: (0, 0)),
            pl.BlockSpec((1, MLSTM_WIDTH), lambda i: (0, 0)),
            pl.BlockSpec((bb, H, dh, dh), lambda i: (i, 0, 0, 0)),
        ],
        out_specs=[
            pl.BlockSpec((R, MLSTM_WIDTH), lambda i: (i, 0)),
            pl.BlockSpec((bb, H, dh, dh), lambda i: (i, 0, 0, 0)),
            pl.BlockSpec((R, MLSTM_WIDTH), lambda i: (i, 0)),
            pl.BlockSpec((R, LANES), lambda i: (i, 0)),
        ],
        out_shape=[
            jax.ShapeDtypeStruct((S_ROWS, MLSTM_WIDTH), BF16),
            jax.ShapeDtypeStruct((DEC_BATCH, H, dh, dh), F32),
            jax.ShapeDtypeStruct((S_ROWS, MLSTM_WIDTH), F32),
            jax.ShapeDtypeStruct((S_ROWS, LANES), F32),
        ],
        compiler_params=_cparams(("arbitrary",)),
        name="mlstm_s",
    )(m_all, m_all, m_all, m_all, gates, n_rep, m_rep, brow, nw, state_c)


def _outproj_kernel(att_ref, mh_ref, x_ref, g1_ref, sh_ref, sc_ref, nw_ref, wa_ref, wm_ref, x1_ref, h2_ref, *,
                    per_row):
    y = _dot(att_ref[...].astype(BF16), wa_ref[...]) + _dot(mh_ref[...], wm_ref[...])
    x1 = x_ref[...] + _mod_row(g1_ref, per_row) * y
    x1_ref[...] = x1
    r = lax.rsqrt(jnp.mean(x1 * x1, axis=-1, keepdims=True) + EPS)
    h2 = (x1 * r * nw_ref[...]) * (1.0 + _mod_row(sc_ref, per_row)) + _mod_row(sh_ref, per_row)
    h2_ref[...] = h2.astype(BF16)


def _outproj(att, mh, x, mod, mod_row_block, norm_w, w_att, w_m, *, tm, per_row):
    rows = x.shape[0]
    mod_rows = tm if per_row else 8
    mod_idx = (lambda i: i) if per_row else (lambda i: mod_row_block)
    kern = functools.partial(_outproj_kernel, per_row=per_row)
    return pl.pallas_call(
        kern,
        grid=(rows // tm,),
        in_specs=[
            pl.BlockSpec((tm, ATT_WIDTH), lambda i: (i, 0)),
            pl.BlockSpec((tm, MLSTM_WIDTH), lambda i: (i, 0)),
            pl.BlockSpec((tm, D_MODEL), lambda i: (i, 0)),
            pl.BlockSpec((mod_rows, D_MODEL), lambda i: (mod_idx(i), 2)),
            pl.BlockSpec((mod_rows, D_MODEL), lambda i: (mod_idx(i), 3)),
            pl.BlockSpec((mod_rows, D_MODEL), lambda i: (mod_idx(i), 4)),
            pl.BlockSpec((1, D_MODEL), lambda i: (0, 0)),
            pl.BlockSpec((ATT_WIDTH, D_MODEL), lambda i: (0, 0)),
            pl.BlockSpec((MLSTM_WIDTH, D_MODEL), lambda i: (0, 0)),
        ],
        out_specs=[
            pl.BlockSpec((tm, D_MODEL), lambda i: (i, 0)),
            pl.BlockSpec((tm, D_MODEL), lambda i: (i, 0)),
        ],
        out_shape=[
            jax.ShapeDtypeStruct((rows, D_MODEL), F32),
            jax.ShapeDtypeStruct((rows, D_MODEL), BF16),
        ],
        compiler_params=_cparams(("arbitrary",)),
        name="outproj_s" if per_row else "outproj_p",
    )(att, mh, x, mod, mod, mod, norm_w, w_att, w_m)


def _ffn_a_kernel(h_ref, wg_ref, wu_ref, a_ref):
    h = h_ref[...]
    g = _dot(h, wg_ref[...].astype(BF16))
    u = _dot(h, wu_ref[...].astype(BF16))
    a_ref[...] = (g * _sigmoid(g) * u).astype(BF16)


def _ffn_a(h2, w_gate, w_up, *, tm):
    rows = h2.shape[0]
    tn = 512
    return pl.pallas_call(
        _ffn_a_kernel,
        grid=(rows // tm, D_FF // tn),
        in_specs=[
            pl.BlockSpec((tm, D_MODEL), lambda i, j: (i, 0)),
            pl.BlockSpec((D_MODEL, tn), lambda i, j: (0, j)),
            pl.BlockSpec((D_MODEL, tn), lambda i, j: (0, j)),
        ],
        out_specs=pl.BlockSpec((tm, tn), lambda i, j: (i, j)),
        out_shape=jax.ShapeDtypeStruct((rows, D_FF), BF16),
        compiler_params=_cparams(("arbitrary", "arbitrary")),
        name="ffn_a",
    )(h2, w_gate, w_up)


def _ffn_b_kernel(a_ref, wd_ref, x1_ref, g2_ref, fw_ref, y_ref, *, per_row):
    kk = pl.program_id(1)
    @pl.when(kk == 0)
    def _():
        y_ref[...] = _dot(a_ref[...], wd_ref[...].astype(BF16))

    @pl.when(kk > 0)
    def _():
        y_ref[...] += _dot(a_ref[...], wd_ref[...].astype(BF16))

    @pl.when(kk == pl.num_programs(1) - 1)
    def _():
        x2 = x1_ref[...] + _mod_row(g2_ref, per_row) * y_ref[...]
        y_ref[...] = x2 * lax.rsqrt(jnp.mean(x2 * x2, axis=-1, keepdims=True) + EPS) * fw_ref[...]


def _ffn_b(a, w_down, x1, mod, mod_row_block, final_w, *, tm, per_row):
    rows = a.shape[0]
    tk = 512
    mod_rows = tm if per_row else 8
    mod_idx = (lambda i: i) if per_row else (lambda i: mod_row_block)
    kern = functools.partial(_ffn_b_kernel, per_row=per_row)
    return pl.pallas_call(
        kern,
        grid=(rows // tm, D_FF // tk),
        in_specs=[
            pl.BlockSpec((tm, tk), lambda i, k: (i, k)),
            pl.BlockSpec((tk, D_MODEL), lambda i, k: (k, 0)),
            pl.BlockSpec((tm, D_MODEL), lambda i, k: (i, 0), pipeline_mode=pl.Buffered(1)),
            pl.BlockSpec((mod_rows, D_MODEL), lambda i, k: (mod_idx(i), 5)),
            pl.BlockSpec((1, D_MODEL), lambda i, k: (0, 0)),
        ],
        out_specs=pl.BlockSpec((tm, D_MODEL), lambda i, k: (i, 0)),
        out_shape=jax.ShapeDtypeStruct((rows, D_MODEL), F32),
        compiler_params=_cparams(("arbitrary", "arbitrary")),
        name="ffn_b_s" if per_row else "ffn_b_p",
    )(a, w_down, x1, mod, final_w)


def _rope_tables(pos):
    half = ROPE_DIM // 2
    inv = np.float32(ROPE_THETA) ** (-np.arange(0, ROPE_DIM, 2, dtype=np.float32) / np.float32(ROPE_DIM))
    d = np.arange(LANES) % ATT_HEAD_DIM
    ang = pos.astype(np.float32)[:, None] * inv[d % half][None, :].astype(np.float32)
    cos, sin = np.cos(ang), np.sin(ang)
    d = d[None, :]
    tables = (np.where(d < ROPE_DIM, cos, 1.0), np.where(d < half, -sin, 0.0),
              np.where((d >= half) & (d < ROPE_DIM), sin, 0.0))
    return tuple(jnp.asarray(t.astype(np.float32)) for t in tables)


def kernel(x_prompt, x_sample, cache_k_win, cache_v_win, state_C, state_n, state_m, c_prompt, c_sample,
           norm1_w, norm2_w, final_norm_w, w_ada, b_ada, w_in, b_ig, b_fg, attn_sinks, mh_norm_w,
           w_out, w_gate, w_up, w_down):
    assert w_in.shape[0] == 1, "single-layer trunk"
    T = DEC_SEQ
    xp = x_prompt[0]
    xs = x_sample.reshape(S_ROWS, D_MODEL)

    c_all = jnp.concatenate([jnp.repeat(c_sample, T, axis=0), c_prompt, jnp.zeros((15, D_MODEL), F32)], axis=0)
    mod = _ada(c_all, w_ada[0], b_ada)
    prompt_mod_block = S_ROWS // 8

    w_in_t = jnp.transpose(w_in[0])
    wq_t = (w_in_t[:ATT_WIDTH].reshape(ATT_KV_HEADS, ATT_GROUP, ATT_HEAD_DIM, D_MODEL)
            .transpose(1, 0, 2, 3).reshape(ATT_WIDTH, D_MODEL).astype(BF16))
    w_gates_t = w_in_t[MAIN_WIDTH:]
    wg = jnp.pad(w_gates_t.T, ((0, 0), (0, LANES - 2 * MLSTM_HEADS))).astype(BF16)
    wgt = jnp.pad(w_gates_t, ((0, 16 - 2 * MLSTM_HEADS), (0, 0))).astype(BF16)
    w_out_att = (w_out[0, :ATT_WIDTH].reshape(ATT_KV_HEADS, ATT_GROUP, ATT_HEAD_DIM, D_MODEL)
                 .transpose(1, 0, 2, 3).reshape(ATT_WIDTH, D_MODEL).astype(BF16))
    w_out_m = w_out[0, ATT_WIDTH:].astype(BF16)
    n1 = norm1_w.reshape(1, D_MODEL)
    n2 = norm2_w.reshape(1, D_MODEL)
    fw = final_norm_w.reshape(1, D_MODEL)
    nw = mh_norm_w.reshape(1, MLSTM_WIDTH)
    gate_bias = jnp.concatenate([b_ig[0], b_fg[0]])
    brow = jnp.pad(gate_bias, (0, LANES - 2 * MLSTM_HEADS)).reshape(1, LANES)
    bcol = jnp.broadcast_to(jnp.pad(gate_bias, (0, 16 - 2 * MLSTM_HEADS))[:, None], (16, MLSTM_CHUNK_P))

    rope_p = _rope_tables(np.arange(SEQ))
    rope_s = _rope_tables(np.tile(PAST_LEN + np.arange(T), DEC_BATCH))

    tm_p = 1024
    q_p, kv_p, kv32_p, m_p, g_p, gt_p = _inproj(xp, mod, prompt_mod_block, n1, wq_t, w_in_t, wg, wgt, *rope_p,
                                                tm=tm_p, per_row=False)
    q_s, kv_s, kv32_s, m_s, g_s, _ = _inproj(xs, mod, 0, n1, wq_t, w_in_t, wg, wgt, *rope_s,
                                              tm=S_ROWS, per_row=True)
    del kv_s

    sinks = attn_sinks[0]
    att_p = _attn_p(sinks, q_p, kv_p)
    sink_col = jnp.broadcast_to(sinks.reshape(ATT_HEADS, 1, 1), (ATT_HEADS, 8, LANES)).reshape(128, LANES)
    ck = jnp.transpose(cache_k_win[0], (0, 2, 3, 1))
    cv = jnp.transpose(cache_v_win[0], (0, 2, 3, 1))
    att_s, kwin_s, vwin_s = _attn_s(sink_col, q_s, kv32_s, ck, cv)

    mh_p, c_p, n_p, mm_p = _mlstm_p(m_p, g_p, gt_p, brow, bcol, nw)
    n_rep = jnp.repeat(state_n[0].reshape(DEC_BATCH, MLSTM_WIDTH), T, axis=0)
    m_rep = jnp.pad(jnp.repeat(state_m[0], T, axis=0), ((0, 0), (0, LANES - MLSTM_HEADS)))
    mh_s, c_s, nrow_s, mrow_s = _mlstm_s(m_s, g_s, n_rep, m_rep, brow, nw, state_C[0])

    x1_p, h2_p = _outproj(att_p, mh_p, xp, mod, prompt_mod_block, n2, w_out_att, w_out_m, tm=512, per_row=False)
    x1_s, h2_s = _outproj(att_s, mh_s, xs, mod, 0, n2, w_out_att, w_out_m, tm=S_ROWS, per_row=True)
    a_p = _ffn_a(h2_p, w_gate[0], w_up[0], tm=tm_p)
    a_s = _ffn_a(h2_s, w_gate[0], w_up[0], tm=S_ROWS)
    y_p = _ffn_b(a_p, w_down[0], x1_p, mod, prompt_mod_block, fw, tm=tm_p, per_row=False)
    y_s = _ffn_b(a_s, w_down[0], x1_s, mod, 0, fw, tm=S_ROWS, per_row=True)

    kv_shape = (1, 1, WINDOW, ATT_KV_HEADS, ATT_HEAD_DIM)
    kv_last = kv32_p[tm_p - WINDOW:]
    dh = MLSTM_HEAD_DIM
    return (
        y_p.reshape(1, SEQ, D_MODEL),
        y_s.reshape(DEC_BATCH, T, D_MODEL),
        kv_last[:, :KV_WIDTH].reshape(kv_shape),
        kv_last[:, KV_WIDTH:].reshape(kv_shape),
        c_p.reshape(1, 1, MLSTM_HEADS, dh, dh),
        n_p[:MLSTM_HEADS].reshape(1, 1, MLSTM_HEADS, dh),
        mm_p[:MLSTM_HEADS, 0].reshape(1, 1, MLSTM_HEADS),
        jnp.transpose(kwin_s, (0, 3, 1, 2))[None],
        jnp.transpose(vwin_s, (0, 3, 1, 2))[None],
        c_s.reshape(1, DEC_BATCH, MLSTM_HEADS, dh, dh),
        nrow_s[T - 1::T].reshape(1, DEC_BATCH, MLSTM_HEADS, dh),
        mrow_s[T - 1::T, :MLSTM_HEADS].reshape(1, DEC_BATCH, MLSTM_HEADS),
    )
```

```python
import functools

import jax
import jax.numpy as jnp
import numpy as np
from jax import lax
from jax.experimental import pallas as pl
from jax.experimental.pallas import tpu as pltpu

F32 = jnp.float32
BF16 = jnp.bfloat16

D_MODEL = 2048
SEQ = 8192
DEC_BATCH = 128
DEC_SEQ = 4
S_ROWS = DEC_BATCH * DEC_SEQ
PAST_LEN = 16384
ATT_HEADS = 16
ATT_KV_HEADS = 4
ATT_GROUP = 4
ATT_HEAD_DIM = 64
WINDOW = 128
ROPE_THETA = 500000.0
ROPE_DIM = 16
MLSTM_HEADS = 4
MLSTM_HEAD_DIM = 256
ATT_WIDTH = 1024
KV_WIDTH = 256
MLSTM_WIDTH = 1024
MAIN_WIDTH = ATT_WIDTH + 2 * KV_WIDTH + 4 * MLSTM_WIDTH
D_FF = 5632
N_MOD = 6
EPS = 1e-6

LANES = 128
MLSTM_CHUNK_P = 256
VMEM_LIMIT = 56 * 1024 * 1024

NT_DIMS = (((1,), (1,)), ((), ()))
TN_DIMS = (((0,), (0,)), ((), ()))
HI = lax.Precision.HIGHEST


def _cparams(sem):
    return pltpu.CompilerParams(dimension_semantics=sem, vmem_limit_bytes=VMEM_LIMIT)


def _dot(a, b):
    return jnp.dot(a, b, preferred_element_type=F32)


def _dot_nt(a, b):
    return lax.dot_general(a, b, NT_DIMS, preferred_element_type=F32)


def _dot_tn(a, b):
    return lax.dot_general(a, b, TN_DIMS, preferred_element_type=F32)


def _sigmoid(x):
    return 1.0 / (1.0 + jnp.exp(-x))


def _log_sigmoid(x):
    return jnp.minimum(x, 0.0) - jnp.log(1.0 + jnp.exp(-jnp.abs(x)))


def _mod_row(ref, per_row):
    return ref[...] if per_row else ref[0:1, :]


def _ada_kernel(c_ref, w_ref, b_ref, o_ref, s_scr):
    @pl.when(pl.program_id(0) == 0)
    def _():
        c = c_ref[...]
        s_scr[...] = (c * _sigmoid(c)).astype(BF16)

    o_ref[...] = _dot(s_scr[...], w_ref[...].astype(BF16)) + b_ref[...]


def _ada(c_all, w_ada, b_ada):
    m = c_all.shape[0]
    n = w_ada.shape[1]
    tn = 1024
    return pl.pallas_call(
        _ada_kernel,
        grid=(n // tn,),
        in_specs=[
            pl.BlockSpec((m, D_MODEL), lambda j: (0, 0)),
            pl.BlockSpec((D_MODEL, tn), lambda j: (0, j)),
            pl.BlockSpec((1, tn), lambda j: (0, j)),
        ],
        out_specs=pl.BlockSpec((m, tn), lambda j: (0, j)),
        out_shape=jax.ShapeDtypeStruct((m, n), F32),
        scratch_shapes=[pltpu.VMEM((m, D_MODEL), BF16)],
        compiler_params=_cparams(("arbitrary",)),
        name="ada",
    )(c_all, w_ada, b_ada)


def _rope_store(acc, cos, sa, sb, out_ref, ncols, scale):
    for c in range(ncols // LANES):
        xc = acc[:, LANES * c:LANES * (c + 1)]
        rot = xc * cos + pltpu.roll(xc, LANES - 8, 1) * sa + pltpu.roll(xc, 8, 1) * sb
        if scale != 1.0:
            rot = rot * scale
        out_ref[:, LANES * c:LANES * (c + 1)] = rot.astype(out_ref.dtype)


def _inproj_kernel(x_ref, sh_ref, sc_ref, nw_ref, wq_ref, win_ref, wg_ref, wgt_ref, cos_ref, sa_ref, sb_ref,
                   q_ref, kv_ref, kv32_ref, m_ref, g_ref, gt_ref, *rest, per_row, emit_w):
    h_scr = rest[-1]
    j = pl.program_id(1)

    def weight_tile():
        wt = win_ref[...].astype(BF16)
        if emit_w:
            rest[0][...] = wt
        return wt

    @pl.when(j == 0)
    def _():
        x = x_ref[...]
        r = lax.rsqrt(jnp.mean(x * x, axis=-1, keepdims=True) + EPS)
        y = x * r * nw_ref[...]
        h = (y * (1.0 + _mod_row(sc_ref, per_row)) + _mod_row(sh_ref, per_row)).astype(BF16)
        h_scr[...] = h
        g_ref[...] = _dot(h, wg_ref[...])
        gt_ref[...] = _dot_nt(wgt_ref[...], h)

    @pl.when(j < 2)
    def _():
        acc = _dot_nt(h_scr[...], wq_ref[...])
        _rope_store(acc, cos_ref[...], sa_ref[...], sb_ref[...], q_ref, 512, ATT_HEAD_DIM ** -0.5)

    @pl.when(j == 2)
    def _():
        acc = _dot_nt(h_scr[...], weight_tile())
        _rope_store(acc, cos_ref[...], sa_ref[...], sb_ref[...], kv32_ref, KV_WIDTH, 1.0)
        kv32_ref[:, KV_WIDTH:] = acc[:, KV_WIDTH:]
        kv_ref[...] = kv32_ref[...].astype(BF16)

    @pl.when(j > 2)
    def _():
        m_ref[...] = _dot_nt(h_scr[...], weight_tile()).astype(BF16)


def _inproj(x, mod, mod_row_block, norm_w, wq_t, w_in_t, wg, wgt, cos, sa, sb, *, tm, per_row, emit_w):
    rows = x.shape[0]
    tn = 512
    nj = MAIN_WIDTH // tn
    mod_rows = tm if per_row else 8
    mod_idx = (lambda i: i) if per_row else (lambda i: mod_row_block)
    kern = functools.partial(_inproj_kernel, per_row=per_row, emit_w=emit_w)
    w_spec = pl.BlockSpec((tn, D_MODEL), lambda i, j: (jnp.maximum(j, 2), 0))
    extra_specs = [w_spec] if emit_w else []
    extra_shapes = [jax.ShapeDtypeStruct((MAIN_WIDTH, D_MODEL), BF16)] if emit_w else []
    return pl.pallas_call(
        kern,
        grid=(rows // tm, nj),
        in_specs=[
            pl.BlockSpec((tm, D_MODEL), lambda i, j: (i, 0)),
            pl.BlockSpec((mod_rows, D_MODEL), lambda i, j: (mod_idx(i), 0)),
            pl.BlockSpec((mod_rows, D_MODEL), lambda i, j: (mod_idx(i), 1)),
            pl.BlockSpec((1, D_MODEL), lambda i, j: (0, 0)),
            pl.BlockSpec((tn, D_MODEL), lambda i, j: (jnp.minimum(j, 1), 0)),
            w_spec,
            pl.BlockSpec((D_MODEL, LANES), lambda i, j: (0, 0)),
            pl.BlockSpec((16, D_MODEL), lambda i, j: (0, 0)),
            pl.BlockSpec((tm, LANES), lambda i, j: (i, 0)),
            pl.BlockSpec((tm, LANES), lambda i, j: (i, 0)),
            pl.BlockSpec((tm, LANES), lambda i, j: (i, 0)),
        ],
        out_specs=[
            pl.BlockSpec((tm, tn), lambda i, j: (i, jnp.minimum(j, 1))),
            pl.BlockSpec((tm, tn), lambda i, j: (i, 0)),
            pl.BlockSpec((tm, tn), lambda i, j: (0, 0)),
            pl.BlockSpec((tm, tn), lambda i, j: (i, jnp.clip(j - 3, 0, 7))),
            pl.BlockSpec((tm, LANES), lambda i, j: (i, 0)),
            pl.BlockSpec((16, tm), lambda i, j: (0, i)),
        ] + extra_specs,
        out_shape=[
            jax.ShapeDtypeStruct((rows, ATT_WIDTH), BF16),
            jax.ShapeDtypeStruct((rows, 2 * KV_WIDTH), BF16),
            jax.ShapeDtypeStruct((tm, 2 * KV_WIDTH), F32),
            jax.ShapeDtypeStruct((rows, 4 * MLSTM_WIDTH), BF16),
            jax.ShapeDtypeStruct((rows, LANES), F32),
            jax.ShapeDtypeStruct((16, rows), F32),
        ] + extra_shapes,
        scratch_shapes=[pltpu.VMEM((tm, D_MODEL), BF16)],
        compiler_params=_cparams(("arbitrary", "arbitrary")),
        name="inproj_s" if per_row else "inproj_p",
    )(x, mod, mod, norm_w, wq_t, w_in_t, wg, wgt, cos, sa, sb)


def _attn_p_kernel(sink_ref, q_ref, kvp_ref, kvc_ref, o_ref):
    n = pl.program_id(0)
    w = WINDOW
    grp = ATT_GROUP
    rows = grp * w
    qi = lax.broadcasted_iota(jnp.int32, (rows, 2 * w), 0) % w
    kj = lax.broadcasted_iota(jnp.int32, (rows, 2 * w), 1)
    prev_off = jnp.where(n > 0, 0, 4 * w)
    allowed = ((kj < w) & (kj > qi + prev_off)) | ((kj >= w) & (kj - w <= qi))
    member = lax.broadcasted_iota(jnp.int32, (rows, 1), 0) // w
    low = lax.broadcasted_iota(jnp.int32, (2 * w, LANES), 1) < ATT_HEAD_DIM
    low_o = lax.broadcasted_iota(jnp.int32, (rows, LANES), 1) < ATT_HEAD_DIM
    key_row = lax.broadcasted_iota(jnp.int32, (4 * w, LANES), 0)
    key_lane = lax.broadcasted_iota(jnp.int32, (4 * w, LANES), 1)
    ones_bd = (((key_row < 2 * w) & (key_lane < ATT_HEAD_DIM))
               | ((key_row >= 2 * w) & (key_lane >= ATT_HEAD_DIM))).astype(BF16)
    kv2 = jnp.concatenate([kvp_ref[...], kvc_ref[...]], axis=0)
    zero = jnp.zeros((2 * w, LANES), BF16)
    for cp in range(2):
        k128 = kv2[:, LANES * cp:LANES * (cp + 1)]
        v128 = kv2[:, KV_WIDTH + LANES * cp:KV_WIDTH + LANES * (cp + 1)]
        kbd = jnp.concatenate([jnp.where(low, k128, zero), jnp.where(low, zero, k128)], axis=0)
        vbd = jnp.concatenate([jnp.where(low, v128, zero), jnp.where(low, zero, v128)], axis=0)
        v_aug = jnp.concatenate([vbd, ones_bd], axis=1)
        q4 = jnp.concatenate([q_ref[:, 256 * r + LANES * cp:256 * r + LANES * (cp + 1)] for r in range(grp)], axis=0)
        s = _dot_nt(q4, kbd)
        es, tails = [], []
        for half in range(2):
            sh = jnp.where(allowed, s[:, 2 * w * half:2 * w * (half + 1)], -jnp.inf)
            head0 = (2 * cp + half) * grp
            sink = jnp.full((rows, 1), sink_ref[head0], F32)
            for r in range(1, grp):
                sink = jnp.where(member == r, sink_ref[head0 + r], sink)
            m = jnp.maximum(jnp.max(sh, axis=-1, keepdims=True), sink)
            es.append(jnp.exp(sh - m).astype(BF16))
            tails.append(jnp.exp(sink - m))
        oa = _dot(jnp.concatenate(es, axis=1), v_aug)
        l = oa[:, LANES:] + jnp.where(low_o, tails[0], tails[1])
        o = (oa[:, :LANES] / l).astype(BF16)
        for r in range(grp):
            o_ref[:, 256 * r + LANES * cp:256 * r + LANES * (cp + 1)] = o[w * r:w * (r + 1)]


def _attn_p(sinks, q, kv):
    nb = SEQ // WINDOW
    return pl.pallas_call(
        _attn_p_kernel,
        grid=(nb,),
        in_specs=[
            pl.BlockSpec(memory_space=pltpu.SMEM),
            pl.BlockSpec((WINDOW, ATT_WIDTH), lambda n: (n, 0)),
            pl.BlockSpec((WINDOW, 2 * KV_WIDTH), lambda n: (jnp.maximum(n - 1, 0), 0)),
            pl.BlockSpec((WINDOW, 2 * KV_WIDTH), lambda n: (n, 0)),
        ],
        out_specs=pl.BlockSpec((WINDOW, ATT_WIDTH), lambda n: (n, 0)),
        out_shape=jax.ShapeDtypeStruct((SEQ, ATT_WIDTH), BF16),
        compiler_params=_cparams(("arbitrary",)),
        name="attn_p",
    )(sinks, q, kv, kv)


ATT_S_BB = 8


def _attn_s_kernel(sink_ref, q_ref, kv32_ref, ck_ref, cv_ref, o_ref, ko_ref, vo_ref, q32_scr):
    t_new = DEC_SEQ
    w = WINDOW
    q32_scr[...] = q_ref[...].astype(F32)
    rows = 4 * 4 * 8
    row = lax.broadcasted_iota(jnp.int32, (rows, w), 0)
    slot = lax.broadcasted_iota(jnp.int32, (rows, w), 1)
    t_row = row % t_new
    second = (row % 8) >= t_new
    win_ok = (slot < w - t_new) | (slot - (w - t_new) <= t_row)
    old_ok = (slot >= 1) & (slot < t_new) & (slot > t_row)
    lane256 = lax.broadcasted_iota(jnp.int32, (32, 2 * LANES), 1)
    sink = sink_ref[...][:, 0:1]
    kv_new = jnp.concatenate([kv32_ref[...], jnp.zeros((w - ATT_S_BB * t_new, 2 * KV_WIDTH), F32)], axis=0)
    kv_t = kv_new.T
    new_slot = lax.broadcasted_iota(jnp.int32, (KV_WIDTH, w), 1) >= w - t_new
    for b in range(ATT_S_BB):
        cols = pltpu.roll(kv_t, w - t_new - t_new * b, 1)
        k_shift = pltpu.roll(ck_ref[b].reshape(KV_WIDTH, w), w - t_new, 1)
        v_shift = pltpu.roll(cv_ref[b].reshape(KV_WIDTH, w), w - t_new, 1)
        ko_ref[b] = jnp.where(new_slot, cols[:KV_WIDTH], k_shift).reshape(ATT_KV_HEADS, ATT_HEAD_DIM, w)
        vo_ref[b] = jnp.where(new_slot, cols[KV_WIDTH:], v_shift).reshape(ATT_KV_HEADS, ATT_HEAD_DIM, w)
    for pair in range(ATT_S_BB // 2):
        b0, b1 = 2 * pair, 2 * pair + 1
        q32 = jnp.concatenate([q32_scr[8 * pair:8 * (pair + 1), 256 * r:256 * (r + 1)] for r in range(ATT_GROUP)],
                              axis=0)
        qpad = jnp.concatenate(
            [jnp.where((lane256 // ATT_HEAD_DIM) == g, q32, 0.0) for g in range(ATT_KV_HEADS)], axis=0).astype(BF16)
        kw = [ko_ref[b].reshape(KV_WIDTH, w).astype(BF16) for b in (b0, b1)]
        vw = [vo_ref[b].reshape(KV_WIDTH, w).astype(BF16) for b in (b0, b1)]
        kc = [ck_ref[b].reshape(KV_WIDTH, w).astype(BF16) for b in (b0, b1)]
        vc = [cv_ref[b].reshape(KV_WIDTH, w).astype(BF16) for b in (b0, b1)]
        s_w = jnp.where(second, _dot(qpad, kw[1]), _dot(qpad, kw[0]))
        s_c = jnp.where(second, _dot(qpad, kc[1]), _dot(qpad, kc[0]))
        s_w = jnp.where(win_ok, s_w, -jnp.inf)
        s_c = jnp.where(old_ok, s_c, -jnp.inf)
        m = jnp.maximum(jnp.maximum(jnp.max(s_w, axis=-1, keepdims=True), jnp.max(s_c, axis=-1, keepdims=True)), sink)
        e_w = jnp.exp(s_w - m)
        e_c = jnp.exp(s_c - m)
        l = jnp.sum(e_w, axis=-1, keepdims=True) + jnp.sum(e_c, axis=-1, keepdims=True) + jnp.exp(sink - m)
        p_w = e_w / l
        p_c = e_c / l
        zero = jnp.zeros_like(p_w)
        o = (_dot_nt(jnp.where(second, zero, p_w).astype(BF16), vw[0])
             + _dot_nt(jnp.where(second, p_w, zero).astype(BF16), vw[1])
             + _dot_nt(jnp.where(second, zero, p_c).astype(BF16), vc[0])
             + _dot_nt(jnp.where(second, p_c, zero).astype(BF16), vc[1]))
        o32 = jnp.zeros((32, 2 * LANES), F32)
        for g in range(ATT_KV_HEADS):
            o32 = jnp.where((lane256 // ATT_HEAD_DIM) == g, o[32 * g:32 * (g + 1), :], o32)
        for r in range(ATT_GROUP):
            o_ref[8 * pair:8 * (pair + 1), 256 * r:256 * (r + 1)] = o32[8 * r:8 * (r + 1), :]


def _attn_s(sink_col, q, kv32, ck, cv):
    bb = ATT_S_BB
    rows = bb * DEC_SEQ
    cache_block = (bb, ATT_KV_HEADS, ATT_HEAD_DIM, WINDOW)
    cache_shape = (DEC_BATCH, ATT_KV_HEADS, ATT_HEAD_DIM, WINDOW)
    return pl.pallas_call(
        _attn_s_kernel,
        grid=(DEC_BATCH // bb,),
        in_specs=[
            pl.BlockSpec((128, LANES), lambda i: (0, 0)),
            pl.BlockSpec((rows, ATT_WIDTH), lambda i: (i, 0)),
            pl.BlockSpec((rows, 2 * KV_WIDTH), lambda i: (i, 0)),
            pl.BlockSpec(cache_block, lambda i: (i, 0, 0, 0)),
            pl.BlockSpec(cache_block, lambda i: (i, 0, 0, 0)),
        ],
        out_specs=[
            pl.BlockSpec((rows, ATT_WIDTH), lambda i: (i, 0)),
            pl.BlockSpec(cache_block, lambda i: (i, 0, 0, 0)),
            pl.BlockSpec(cache_block, lambda i: (i, 0, 0, 0)),
        ],
        out_shape=[
            jax.ShapeDtypeStruct((S_ROWS, ATT_WIDTH), F32),
            jax.ShapeDtypeStruct(cache_shape, F32),
            jax.ShapeDtypeStruct(cache_shape, F32),
        ],
        scratch_shapes=[pltpu.VMEM((rows, ATT_WIDTH), F32)],
        compiler_params=_cparams(("arbitrary",)),
        name="attn_s",
    )(sink_col, q, kv32, ck, cv)


def _head_norm_gate(h, nw, mo):
    hn = h * lax.rsqrt(jnp.mean(h * h, axis=-1, keepdims=True) + EPS) * nw
    return hn * _sigmoid(mo.astype(F32))


def _mlstm_p_kernel(q_ref, k_ref, v_ref, mo_ref, g_ref, gt_ref, brow_ref, bcol_ref, nw_ref,
                    mh_ref, c_out, n_out, m_out, c_scr, n_scr, m_scr):
    c = pl.program_id(0)
    L = MLSTM_CHUNK_P
    dh = MLSTM_HEAD_DIM

    @pl.when(c == 0)
    def _():
        c_scr[...] = jnp.zeros_like(c_scr)
        n_scr[...] = jnp.zeros_like(n_scr)
        m_scr[...] = jnp.zeros_like(m_scr)

    ti = lax.broadcasted_iota(jnp.int32, (L, L), 0)
    si = lax.broadcasted_iota(jnp.int32, (L, L), 1)
    causal = si <= ti
    tri = causal.astype(F32)
    tri_t = (ti <= si).astype(F32)
    gates = g_ref[...] + brow_ref[...]
    gates_t = gt_ref[...] + bcol_ref[...]
    b_col = jnp.dot(tri, _log_sigmoid(gates), precision=HI, preferred_element_type=F32)
    b_row = jnp.dot(_log_sigmoid(gates_t), tri_t, precision=HI, preferred_element_type=F32)
    for hd in range(MLSTM_HEADS):
        cs = slice(dh * hd, dh * (hd + 1))
        b_c = b_col[:, MLSTM_HEADS + hd:MLSTM_HEADS + hd + 1]
        li_c = gates[:, hd:hd + 1]
        b_r = b_row[MLSTM_HEADS + hd:MLSTM_HEADS + hd + 1, :]
        li_r = gates_t[hd:hd + 1, :]
        dm = jnp.where(causal, b_c - b_r + li_r, -jnp.inf)
        m_prev = m_scr[hd:hd + 1, 0:1]
        m_inter = b_c + m_prev
        m_t = jnp.maximum(m_inter, jnp.max(dm, axis=-1, keepdims=True))
        q = q_ref[:, cs] * (dh ** -0.5)
        k = k_ref[:, cs]
        v = v_ref[:, cs]
        sm = _dot_nt(q, k) * jnp.exp(dm - m_t)
        a = jnp.exp(m_inter - m_t)
        c_old = c_scr[hd]
        n_old = n_scr[hd:hd + 1, :]
        num = a * _dot_nt(q, c_old.astype(BF16)) + _dot(sm.astype(BF16), v)
        qn = jnp.sum(q.astype(F32) * n_old, axis=-1, keepdims=True)
        den = a * qn + jnp.sum(sm, axis=-1, keepdims=True)
        h = num / jnp.maximum(jnp.abs(den), jnp.exp(-m_t))
        mh_ref[:, cs] = _head_norm_gate(h, nw_ref[:, cs], mo_ref[:, cs]).astype(BF16)
        m_new = m_t[L - 1:L, :]
        b_last = b_c[L - 1:L, :]
        g = jnp.exp(b_last - b_c + li_c - m_new)
        decay = jnp.exp(b_last + m_prev - m_new)
        gv = (g * v.astype(F32)).astype(BF16)
        c_scr[hd] = decay * c_old + _dot_tn(gv, k)
        n_scr[hd:hd + 1, :] = decay * n_old + jnp.sum(g * k.astype(F32), axis=0, keepdims=True)
        m_scr[hd:hd + 1, :] = jnp.broadcast_to(m_new, (1, LANES))

    @pl.when(c == pl.num_programs(0) - 1)
    def _():
        c_out[...] = c_scr[...]
        n_out[...] = n_scr[...]
        m_out[...] = m_scr[...]


def _mlstm_p(m_all, gates, gates_t, brow, bcol, nw):
    L = MLSTM_CHUNK_P
    nc = SEQ // L
    dh = MLSTM_HEAD_DIM
    return pl.pallas_call(
        _mlstm_p_kernel,
        grid=(nc,),
        in_specs=[
            pl.BlockSpec((L, MLSTM_WIDTH), lambda c: (c, 0)),
            pl.BlockSpec((L, MLSTM_WIDTH), lambda c: (c, 1)),
            pl.BlockSpec((L, MLSTM_WIDTH), lambda c: (c, 2)),
            pl.BlockSpec((L, MLSTM_WIDTH), lambda c: (c, 3)),
            pl.BlockSpec((L, LANES), lambda c: (c, 0)),
            pl.BlockSpec((16, L), lambda c: (0, c)),
            pl.BlockSpec((1, LANES), lambda c: (0, 0)),
            pl.BlockSpec((16, L), lambda c: (0, 0)),
            pl.BlockSpec((1, MLSTM_WIDTH), lambda c: (0, 0)),
        ],
        out_specs=[
            pl.BlockSpec((L, MLSTM_WIDTH), lambda c: (c, 0)),
            pl.BlockSpec((MLSTM_HEADS, dh, dh), lambda c: (0, 0, 0)),
            pl.BlockSpec((8, dh), lambda c: (0, 0)),
            pl.BlockSpec((8, LANES), lambda c: (0, 0)),
        ],
        out_shape=[
            jax.ShapeDtypeStruct((SEQ, MLSTM_WIDTH), BF16),
            jax.ShapeDtypeStruct((MLSTM_HEADS, dh, dh), F32),
            jax.ShapeDtypeStruct((8, dh), F32),
            jax.ShapeDtypeStruct((8, LANES), F32),
        ],
        scratch_shapes=[
            pltpu.VMEM((MLSTM_HEADS, dh, dh), F32),
            pltpu.VMEM((8, dh), F32),
            pltpu.VMEM((8, LANES), F32),
        ],
        compiler_params=_cparams(("arbitrary",)),
        name="mlstm_p",
    )(m_all, m_all, m_all, m_all, gates, gates_t, brow, bcol, nw)


MLSTM_S_BB = 4


def _mlstm_s_kernel(q_ref, k_ref, v_ref, mo_ref, g_ref, nrep_ref, mrep_ref, brow_ref, nw_ref, c_ref,
                    mh_ref, c_out, nrow_ref, mrow_ref):
    T = DEC_SEQ
    R = MLSTM_S_BB * T
    H = MLSTM_HEADS
    dh = MLSTM_HEAD_DIM

    def shift(x, d):
        return pltpu.roll(x, d, 0)

    lanes = lax.broadcasted_iota(jnp.int32, (R, LANES), 1)
    tmod = lax.broadcasted_iota(jnp.int32, (R, LANES), 0) % T
    tmod_w = lax.broadcasted_iota(jnp.int32, (R, MLSTM_WIDTH), 0) % T
    head_ok = lanes < H
    gates = g_ref[...] + brow_ref[...]
    li = jnp.where(head_ok, gates, 0.0)
    lf = jnp.where(head_ok, pltpu.roll(_log_sigmoid(gates), LANES - H, 1), 0.0)
    bcum = lf
    for d in range(1, T):
        bcum = bcum + jnp.where(tmod >= d, shift(lf, d), 0.0)
    m0 = mrep_ref[...]
    m_inter = bcum + m0
    dms = [li] + [jnp.where(tmod >= d, bcum - shift(bcum, d) + shift(li, d), -jnp.inf) for d in range(1, T)]
    m_t = m_inter
    for dm in dms:
        m_t = jnp.maximum(m_t, dm)
    a = jnp.exp(m_inter - m_t)
    ws = [jnp.exp(dm - m_t) for dm in dms]

    q_bf = q_ref[...] * (dh ** -0.5)
    q = q_bf.astype(F32)
    k = k_ref[...].astype(F32)
    v = v_ref[...].astype(F32)
    seg = (lax.broadcasted_iota(jnp.int32, (MLSTM_WIDTH, LANES), 0) // dh
           == lax.broadcasted_iota(jnp.int32, (MLSTM_WIDTH, LANES), 1)).astype(F32)
    ex = (lax.broadcasted_iota(jnp.int32, (LANES, MLSTM_WIDTH), 1) // dh
          == lax.broadcasted_iota(jnp.int32, (LANES, MLSTM_WIDTH), 0)).astype(F32)

    def segsum(x):
        return jnp.dot(x, seg, precision=HI, preferred_element_type=F32)

    def expand(x):
        return jnp.dot(x, ex, precision=HI, preferred_element_type=F32)

    ks = [k] + [shift(k, d) for d in range(1, T)]
    vs = [v] + [shift(v, d) for d in range(1, T)]
    sms = [segsum(q * ks[d]) * ws[d] for d in range(T)]
    den = a * segsum(q * nrep_ref[...])
    for sm in sms:
        den = den + sm
    inv = 1.0 / jnp.maximum(jnp.abs(den), jnp.exp(-m_t))

    def last(x):
        out = jnp.zeros_like(x)
        for jj in range(T):
            out = jnp.where(tmod == T - 1 - jj, x if jj == 0 else pltpu.roll(x, R - jj, 0), out)
        return out

    m_new = last(m_t)
    b_last = last(bcum)
    g = jnp.where(head_ok, jnp.exp(b_last - bcum + li - m_new), 0.0)
    decay = jnp.where(head_ok, jnp.exp(b_last + m0 - m_new), 0.0)
    a_f = expand(a * inv)
    w_f = [expand(sm * inv) for sm in sms]
    g_f = expand(g)
    d_f = expand(decay)
    gv = (g_f * v).astype(BF16)
    rowb = lax.broadcasted_iota(jnp.int32, (R, dh), 0) // T
    for hd in range(H):
        cs = slice(dh * hd, dh * (hd + 1))
        qh = q_bf[:, cs]
        kh = k_ref[:, cs]
        gvh = gv[:, cs]
        qc = jnp.zeros((R, dh), F32)
        for bb in range(MLSTM_S_BB):
            c_old = c_ref[bb, hd]
            qc = jnp.where(rowb == bb, _dot_nt(qh, c_old.astype(BF16)), qc)
            upd = _dot_tn(jnp.where(rowb == bb, gvh, jnp.zeros_like(gvh)), kh)
            c_out[bb, hd] = d_f[T * bb:T * bb + 1, cs] * c_old + upd
        h = a_f[:, cs] * qc
        for d in range(T):
            h = h + w_f[d][:, cs] * vs[d][:, cs]
        mh_ref[:, cs] = _head_norm_gate(h, nw_ref[:, cs], mo_ref[:, cs]).astype(BF16)
    gk = g_f * k
    nsum = gk
    for d in range(1, T):
        nsum = nsum + jnp.where(tmod_w >= d, shift(gk, d), 0.0)
    nrow_ref[...] = d_f * nrep_ref[...] + nsum
    mrow_ref[...] = m_t


def _mlstm_s(m_all, gates, n_rep, m_rep, brow, nw, state_c):
    bb = MLSTM_S_BB
    R = bb * DEC_SEQ
    dh = MLSTM_HEAD_DIM
    H = MLSTM_HEADS
    return pl.pallas_call(
        _mlstm_s_kernel,
        grid=(DEC_BATCH // bb,),
        in_specs=[
            pl.BlockSpec((R, MLSTM_WIDTH), lambda i: (i, 0)),
            pl.BlockSpec((R, MLSTM_WIDTH), lambda i: (i, 1)),
            pl.BlockSpec((R, MLSTM_WIDTH), lambda i: (i, 2)),
            pl.BlockSpec((R, MLSTM_WIDTH), lambda i: (i, 3)),
            pl.BlockSpec((R, LANES), lambda i: (i, 0)),
            pl.BlockSpec((R, MLSTM_WIDTH), lambda i: (i, 0)),
            pl.BlockSpec((R, LANES), lambda i: (i, 0)),
            pl.BlockSpec((1, LANES), lambda i: (0, 0)),
            pl.BlockSpec((1, MLSTM_WIDTH), lambda i: (0, 0)),
            pl.BlockSpec((bb, H, dh, dh), lambda i: (i, 0, 0, 0)),
        ],
        out_specs=[
            pl.BlockSpec((R, MLSTM_WIDTH), lambda i: (i, 0)),
            pl.BlockSpec((bb, H, dh, dh), lambda i: (i, 0, 0, 0)),
            pl.BlockSpec((R, MLSTM_WIDTH), lambda i: (i, 0)),
            pl.BlockSpec((R, LANES), lambda i: (i, 0)),
        ],
        out_shape=[
            jax.ShapeDtypeStruct((S_ROWS, MLSTM_WIDTH), BF16),
            jax.ShapeDtypeStruct((DEC_BATCH, H, dh, dh), F32),
            jax.ShapeDtypeStruct((S_ROWS, MLSTM_WIDTH), F32),
            jax.ShapeDtypeStruct((S_ROWS, LANES), F32),
        ],
        compiler_params=_cparams(("arbitrary",)),
        name="mlstm_s",
    )(m_all, m_all, m_all, m_all, gates, n_rep, m_rep, brow, nw, state_c)


def _outproj_kernel(att_ref, mh_ref, x_ref, g1_ref, sh_ref, sc_ref, nw_ref, wa_ref, wm_ref, x1_ref, h2_ref, *,
                    per_row):
    y = _dot(att_ref[...].astype(BF16), wa_ref[...]) + _dot(mh_ref[...], wm_ref[...])
    x1 = x_ref[...] + _mod_row(g1_ref, per_row) * y
    x1_ref[...] = x1
    r = lax.rsqrt(jnp.mean(x1 * x1, axis=-1, keepdims=True) + EPS)
    h2 = (x1 * r * nw_ref[...]) * (1.0 + _mod_row(sc_ref, per_row)) + _mod_row(sh_ref, per_row)
    h2_ref[...] = h2.astype(BF16)


def _outproj(att, mh, x, mod, mod_row_block, norm_w, w_att, w_m, *, tm, per_row):
    rows = x.shape[0]
    mod_rows = tm if per_row else 8
    mod_idx = (lambda i: i) if per_row else (lambda i: mod_row_block)
    kern = functools.partial(_outproj_kernel, per_row=per_row)
    return pl.pallas_call(
        kern,
        grid=(rows // tm,),
        in_specs=[
            pl.BlockSpec((tm, ATT_WIDTH), lambda i: (i, 0)),
            pl.BlockSpec((tm, MLSTM_WIDTH), lambda i: (i, 0)),
            pl.BlockSpec((tm, D_MODEL), lambda i: (i, 0)),
            pl.BlockSpec((mod_rows, D_MODEL), lambda i: (mod_idx(i), 2)),
            pl.BlockSpec((mod_rows, D_MODEL), lambda i: (mod_idx(i), 3)),
            pl.BlockSpec((mod_rows, D_MODEL), lambda i: (mod_idx(i), 4)),
            pl.BlockSpec((1, D_MODEL), lambda i: (0, 0)),
            pl.BlockSpec((ATT_WIDTH, D_MODEL), lambda i: (0, 0)),
            pl.BlockSpec((MLSTM_WIDTH, D_MODEL), lambda i: (0, 0)),
        ],
        out_specs=[
            pl.BlockSpec((tm, D_MODEL), lambda i: (i, 0)),
            pl.BlockSpec((tm, D_MODEL), lambda i: (i, 0)),
        ],
        out_shape=[
            jax.ShapeDtypeStruct((rows, D_MODEL), F32),
            jax.ShapeDtypeStruct((rows, D_MODEL), BF16),
        ],
        compiler_params=_cparams(("arbitrary",)),
        name="outproj_s" if per_row else "outproj_p",
    )(att, mh, x, mod, mod, mod, norm_w, w_att, w_m)


def _ffn_a_kernel(h_ref, wg_ref, wu_ref, a_ref, wg_scr, wu_scr):
    @pl.when(pl.program_id(1) == 0)
    def _():
        wg_scr[...] = wg_ref[...].astype(BF16)
        wu_scr[...] = wu_ref[...].astype(BF16)

    h = h_ref[...]
    g = _dot(h, wg_scr[...])
    u = _dot(h, wu_scr[...])
    a_ref[...] = (g * _sigmoid(g) * u).astype(BF16)


def _ffn_a(h2, w_gate, w_up, *, tm):
    rows = h2.shape[0]
    tn = 512
    return pl.pallas_call(
        _ffn_a_kernel,
        grid=(D_FF // tn, rows // tm),
        in_specs=[
            pl.BlockSpec((tm, D_MODEL), lambda j, i: (i, 0)),
            pl.BlockSpec((D_MODEL, tn), lambda j, i: (0, j)),
            pl.BlockSpec((D_MODEL, tn), lambda j, i: (0, j)),
        ],
        out_specs=pl.BlockSpec((tm, tn), lambda j, i: (i, j)),
        out_shape=jax.ShapeDtypeStruct((rows, D_FF), BF16),
        scratch_shapes=[pltpu.VMEM((D_MODEL, tn), BF16), pltpu.VMEM((D_MODEL, tn), BF16)],
        compiler_params=_cparams(("arbitrary", "arbitrary")),
        name="ffn_a",
    )(h2, w_gate, w_up)


def _ffn_b_kernel(a_ref, wd_ref, x1_ref, g2_ref, fw_ref, y_ref, *rest, per_row, emit_w):
    kk = pl.program_id(1)

    def weight_tile():
        wt = wd_ref[...].astype(BF16)
        if emit_w:
            rest[0][...] = wt
        return wt

    @pl.when(kk == 0)
    def _():
        y_ref[...] = _dot(a_ref[...], weight_tile())

    @pl.when(kk > 0)
    def _():
        y_ref[...] += _dot(a_ref[...], weight_tile())

    @pl.when(kk == pl.num_programs(1) - 1)
    def _():
        x2 = x1_ref[...] + _mod_row(g2_ref, per_row) * y_ref[...]
        y_ref[...] = x2 * lax.rsqrt(jnp.mean(x2 * x2, axis=-1, keepdims=True) + EPS) * fw_ref[...]


def _ffn_b(a, w_down, x1, mod, mod_row_block, final_w, *, tm, per_row, emit_w):
    rows = a.shape[0]
    tk = 512
    mod_rows = tm if per_row else 8
    mod_idx = (lambda i: i) if per_row else (lambda i: mod_row_block)
    kern = functools.partial(_ffn_b_kernel, per_row=per_row, emit_w=emit_w)
    w_spec = pl.BlockSpec((tk, D_MODEL), lambda i, k: (k, 0))
    y_spec = pl.BlockSpec((tm, D_MODEL), lambda i, k: (i, 0))
    y_shape = jax.ShapeDtypeStruct((rows, D_MODEL), F32)
    return pl.pallas_call(
        kern,
        grid=(rows // tm, D_FF // tk),
        in_specs=[
            pl.BlockSpec((tm, tk), lambda i, k: (i, k)),
            w_spec,
            pl.BlockSpec((tm, D_MODEL), lambda i, k: (i, 0), pipeline_mode=pl.Buffered(1)),
            pl.BlockSpec((mod_rows, D_MODEL), lambda i, k: (mod_idx(i), 5)),
            pl.BlockSpec((1, D_MODEL), lambda i, k: (0, 0)),
        ],
        out_specs=[y_spec, w_spec] if emit_w else y_spec,
        out_shape=[y_shape, jax.ShapeDtypeStruct((D_FF, D_MODEL), BF16)] if emit_w else y_shape,
        compiler_params=_cparams(("arbitrary", "arbitrary")),
        name="ffn_b_s" if per_row else "ffn_b_p",
    )(a, w_down, x1, mod, final_w)


def _rope_tables(pos):
    half = ROPE_DIM // 2
    inv = np.float32(ROPE_THETA) ** (-np.arange(0, ROPE_DIM, 2, dtype=np.float32) / np.float32(ROPE_DIM))
    d = np.arange(LANES) % ATT_HEAD_DIM
    ang = pos.astype(np.float32)[:, None] * inv[d % half][None, :].astype(np.float32)
    cos, sin = np.cos(ang), np.sin(ang)
    d = d[None, :]
    tables = (np.where(d < ROPE_DIM, cos, 1.0), np.where(d < half, -sin, 0.0),
              np.where((d >= half) & (d < ROPE_DIM), sin, 0.0))
    return tuple(jnp.asarray(t.astype(np.float32)) for t in tables)


def kernel(x_prompt, x_sample, cache_k_win, cache_v_win, state_C, state_n, state_m, c_prompt, c_sample,
           norm1_w, norm2_w, final_norm_w, w_ada, b_ada, w_in, b_ig, b_fg, attn_sinks, mh_norm_w,
           w_out, w_gate, w_up, w_down):
    assert w_in.shape[0] == 1, "single-layer trunk"
    T = DEC_SEQ
    xp = x_prompt[0]
    xs = x_sample.reshape(S_ROWS, D_MODEL)

    c_all = jnp.concatenate([jnp.repeat(c_sample, T, axis=0), c_prompt, jnp.zeros((15, D_MODEL), F32)], axis=0)
    mod = _ada(c_all, w_ada[0], b_ada)
    prompt_mod_block = S_ROWS // 8

    w_in_t = jnp.transpose(w_in[0])
    wq_t = (w_in_t[:ATT_WIDTH].reshape(ATT_KV_HEADS, ATT_GROUP, ATT_HEAD_DIM, D_MODEL)
            .transpose(1, 0, 2, 3).reshape(ATT_WIDTH, D_MODEL).astype(BF16))
    w_gates_t = w_in_t[MAIN_WIDTH:]
    wg = jnp.pad(w_gates_t.T, ((0, 0), (0, LANES - 2 * MLSTM_HEADS))).astype(BF16)
    wgt = jnp.pad(w_gates_t, ((0, 16 - 2 * MLSTM_HEADS), (0, 0))).astype(BF16)
    w_out_att = (w_out[0, :ATT_WIDTH].reshape(ATT_KV_HEADS, ATT_GROUP, ATT_HEAD_DIM, D_MODEL)
                 .transpose(1, 0, 2, 3).reshape(ATT_WIDTH, D_MODEL).astype(BF16))
    w_out_m = w_out[0, ATT_WIDTH:].astype(BF16)
    n1 = norm1_w.reshape(1, D_MODEL)
    n2 = norm2_w.reshape(1, D_MODEL)
    fw = final_norm_w.reshape(1, D_MODEL)
    nw = mh_norm_w.reshape(1, MLSTM_WIDTH)
    gate_bias = jnp.concatenate([b_ig[0], b_fg[0]])
    brow = jnp.pad(gate_bias, (0, LANES - 2 * MLSTM_HEADS)).reshape(1, LANES)
    bcol = jnp.broadcast_to(jnp.pad(gate_bias, (0, 16 - 2 * MLSTM_HEADS))[:, None], (16, MLSTM_CHUNK_P))

    rope_p = _rope_tables(np.arange(SEQ))
    rope_s = _rope_tables(np.tile(PAST_LEN + np.arange(T), DEC_BATCH))

    tm_p = 1024
    q_s, _, kv32_s, m_s, g_s, _, w_in_bf = _inproj(xs, mod, 0, n1, wq_t, w_in_t, wg, wgt, *rope_s,
                                                   tm=S_ROWS, per_row=True, emit_w=True)
    q_p, kv_p, kv32_p, m_p, g_p, gt_p = _inproj(xp, mod, prompt_mod_block, n1, wq_t, w_in_bf, wg, wgt, *rope_p,
                                                tm=tm_p, per_row=False, emit_w=False)

    sinks = attn_sinks[0]
    att_p = _attn_p(sinks, q_p, kv_p)
    sink_col = jnp.broadcast_to(sinks.reshape(ATT_HEADS, 1, 1), (ATT_HEADS, 8, LANES)).reshape(128, LANES)
    ck = jnp.transpose(cache_k_win[0], (0, 2, 3, 1))
    cv = jnp.transpose(cache_v_win[0], (0, 2, 3, 1))
    att_s, kwin_s, vwin_s = _attn_s(sink_col, q_s, kv32_s, ck, cv)

    mh_p, c_p, n_p, mm_p = _mlstm_p(m_p, g_p, gt_p, brow, bcol, nw)
    n_rep = jnp.repeat(state_n[0].reshape(DEC_BATCH, MLSTM_WIDTH), T, axis=0)
    m_rep = jnp.pad(jnp.repeat(state_m[0], T, axis=0), ((0, 0), (0, LANES - MLSTM_HEADS)))
    mh_s, c_s, nrow_s, mrow_s = _mlstm_s(m_s, g_s, n_rep, m_rep, brow, nw, state_C[0])

    x1_p, h2_p = _outproj(att_p, mh_p, xp, mod, prompt_mod_block, n2, w_out_att, w_out_m, tm=512, per_row=False)
    x1_s, h2_s = _outproj(att_s, mh_s, xs, mod, 0, n2, w_out_att, w_out_m, tm=S_ROWS, per_row=True)
    a_p = _ffn_a(h2_p, w_gate[0], w_up[0], tm=tm_p)
    a_s = _ffn_a(h2_s, w_gate[0], w_up[0], tm=S_ROWS)
    y_s, w_down_bf = _ffn_b(a_s, w_down[0], x1_s, mod, 0, fw, tm=S_ROWS, per_row=True, emit_w=True)
    y_p = _ffn_b(a_p, w_down_bf, x1_p, mod, prompt_mod_block, fw, tm=tm_p, per_row=False, emit_w=False)

    kv_shape = (1, 1, WINDOW, ATT_KV_HEADS, ATT_HEAD_DIM)
    kv_last = kv32_p[tm_p - WINDOW:]
    dh = MLSTM_HEAD_DIM
    return (
        y_p.reshape(1, SEQ, D_MODEL),
        y_s.reshape(DEC_BATCH, T, D_MODEL),
        kv_last[:, :KV_WIDTH].reshape(kv_shape),
        kv_last[:, KV_WIDTH:].reshape(kv_shape),
        c_p.reshape(1, 1, MLSTM_HEADS, dh, dh),
        n_p[:MLSTM_HEADS].reshape(1, 1, MLSTM_HEADS, dh),
        mm_p[:MLSTM_HEADS, 0].reshape(1, 1, MLSTM_HEADS),
        jnp.transpose(kwin_s, (0, 3, 1, 2))[None],
        jnp.transpose(vwin_s, (0, 3, 1, 2))[None],
        c_s.reshape(1, DEC_BATCH, MLSTM_HEADS, dh, dh),
        nrow_s[T - 1::T].reshape(1, DEC_BATCH, MLSTM_HEADS, dh),
        mrow_s[T - 1::T, :MLSTM_HEADS].reshape(1, DEC_BATCH, MLSTM_HEADS),
    )
```

```python
import functools

import jax
import jax.numpy as jnp
import numpy as np
from jax import lax
from jax.experimental import pallas as pl
from jax.experimental.pallas import tpu as pltpu

F32 = jnp.float32
BF16 = jnp.bfloat16

D_MODEL = 2048
SEQ = 8192
DEC_BATCH = 128
DEC_SEQ = 4
S_ROWS = DEC_BATCH * DEC_SEQ
PAST_LEN = 16384
ATT_HEADS = 16
ATT_KV_HEADS = 4
ATT_GROUP = 4
ATT_HEAD_DIM = 64
WINDOW = 128
ROPE_THETA = 500000.0
ROPE_DIM = 16
MLSTM_HEADS = 4
MLSTM_HEAD_DIM = 256
ATT_WIDTH = 1024
KV_WIDTH = 256
MLSTM_WIDTH = 1024
MAIN_WIDTH = ATT_WIDTH + 2 * KV_WIDTH + 4 * MLSTM_WIDTH
D_FF = 5632
N_MOD = 6
EPS = 1e-6

LANES = 128
MLSTM_CHUNK_P = 256
VMEM_LIMIT = 56 * 1024 * 1024

NT_DIMS = (((1,), (1,)), ((), ()))
TN_DIMS = (((0,), (0,)), ((), ()))
HI = lax.Precision.HIGHEST


def _cparams(sem):
    return pltpu.CompilerParams(dimension_semantics=sem, vmem_limit_bytes=VMEM_LIMIT)


def _dot(a, b):
    return jnp.dot(a, b, preferred_element_type=F32)


def _dot_nt(a, b):
    return lax.dot_general(a, b, NT_DIMS, preferred_element_type=F32)


def _dot_tn(a, b):
    return lax.dot_general(a, b, TN_DIMS, preferred_element_type=F32)


def _sigmoid(x):
    return 1.0 / (1.0 + jnp.exp(-x))


def _log_sigmoid(x):
    return jnp.minimum(x, 0.0) - jnp.log(1.0 + jnp.exp(-jnp.abs(x)))


def _mod_row(ref, per_row):
    return ref[...] if per_row else ref[0:1, :]


def _ada_kernel(c_ref, w_ref, b_ref, o_ref, s_scr):
    @pl.when(pl.program_id(0) == 0)
    def _():
        c = c_ref[...]
        s_scr[...] = (c * _sigmoid(c)).astype(BF16)

    o_ref[...] = _dot(s_scr[...], w_ref[...].astype(BF16)) + b_ref[...]


def _ada(c_all, w_ada, b_ada):
    m = c_all.shape[0]
    n = w_ada.shape[1]
    tn = 1024
    return pl.pallas_call(
        _ada_kernel,
        grid=(n // tn,),
        in_specs=[
            pl.BlockSpec((m, D_MODEL), lambda j: (0, 0)),
            pl.BlockSpec((D_MODEL, tn), lambda j: (0, j)),
            pl.BlockSpec((1, tn), lambda j: (0, j)),
        ],
        out_specs=pl.BlockSpec((m, tn), lambda j: (0, j)),
        out_shape=jax.ShapeDtypeStruct((m, n), F32),
        scratch_shapes=[pltpu.VMEM((m, D_MODEL), BF16)],
        compiler_params=_cparams(("arbitrary",)),
        name="ada",
    )(c_all, w_ada, b_ada)


def _rope_store(acc, cos, sa, sb, out_ref, ncols, scale):
    for c in range(ncols // LANES):
        xc = acc[:, LANES * c:LANES * (c + 1)]
        rot = xc * cos + pltpu.roll(xc, LANES - 8, 1) * sa + pltpu.roll(xc, 8, 1) * sb
        if scale != 1.0:
            rot = rot * scale
        out_ref[:, LANES * c:LANES * (c + 1)] = rot.astype(out_ref.dtype)


def _inproj_kernel(x_ref, sh_ref, sc_ref, nw_ref, wq_ref, win_ref, wg_ref, wgt_ref, cos_ref, sa_ref, sb_ref,
                   q_ref, kv_ref, kv32_ref, m_ref, g_ref, gt_ref, *rest, per_row, emit_w):
    h_scr = rest[-1]
    j = pl.program_id(1)

    def weight_tile():
        wt = win_ref[...].astype(BF16)
        if emit_w:
            rest[0][...] = wt
        return wt

    @pl.when(j == 0)
    def _():
        x = x_ref[...]
        r = lax.rsqrt(jnp.mean(x * x, axis=-1, keepdims=True) + EPS)
        y = x * r * nw_ref[...]
        h = (y * (1.0 + _mod_row(sc_ref, per_row)) + _mod_row(sh_ref, per_row)).astype(BF16)
        h_scr[...] = h
        g_ref[...] = _dot(h, wg_ref[...])
        gt_ref[...] = _dot_nt(wgt_ref[...], h)

    @pl.when(j < 2)
    def _():
        acc = _dot_nt(h_scr[...], wq_ref[...])
        _rope_store(acc, cos_ref[...], sa_ref[...], sb_ref[...], q_ref, 512, ATT_HEAD_DIM ** -0.5)

    @pl.when(j == 2)
    def _():
        acc = _dot_nt(h_scr[...], weight_tile())
        _rope_store(acc, cos_ref[...], sa_ref[...], sb_ref[...], kv32_ref, KV_WIDTH, 1.0)
        kv32_ref[:, KV_WIDTH:] = acc[:, KV_WIDTH:]
        kv_ref[...] = kv32_ref[...].astype(BF16)

    @pl.when(j > 2)
    def _():
        m_ref[...] = _dot_nt(h_scr[...], weight_tile()).astype(BF16)


def _inproj(x, mod, mod_row_block, norm_w, wq_t, w_in_t, wg, wgt, cos, sa, sb, *, tm, per_row, emit_w):
    rows = x.shape[0]
    tn = 512
    nj = MAIN_WIDTH // tn
    mod_rows = tm if per_row else 8
    mod_idx = (lambda i: i) if per_row else (lambda i: mod_row_block)
    kern = functools.partial(_inproj_kernel, per_row=per_row, emit_w=emit_w)
    w_spec = pl.BlockSpec((tn, D_MODEL), lambda i, j: (jnp.maximum(j, 2), 0))
    extra_specs = [w_spec] if emit_w else []
    extra_shapes = [jax.ShapeDtypeStruct((MAIN_WIDTH, D_MODEL), BF16)] if emit_w else []
    return pl.pallas_call(
        kern,
        grid=(rows // tm, nj),
        in_specs=[
            pl.BlockSpec((tm, D_MODEL), lambda i, j: (i, 0)),
            pl.BlockSpec((mod_rows, D_MODEL), lambda i, j: (mod_idx(i), 0)),
            pl.BlockSpec((mod_rows, D_MODEL), lambda i, j: (mod_idx(i), 1)),
            pl.BlockSpec((1, D_MODEL), lambda i, j: (0, 0)),
            pl.BlockSpec((tn, D_MODEL), lambda i, j: (jnp.minimum(j, 1), 0)),
            w_spec,
            pl.BlockSpec((D_MODEL, LANES), lambda i, j: (0, 0)),
            pl.BlockSpec((16, D_MODEL), lambda i, j: (0, 0)),
            pl.BlockSpec((tm, LANES), lambda i, j: (i, 0)),
            pl.BlockSpec((tm, LANES), lambda i, j: (i, 0)),
            pl.BlockSpec((tm, LANES), lambda i, j: (i, 0)),
        ],
        out_specs=[
            pl.BlockSpec((tm, tn), lambda i, j: (i, jnp.minimum(j, 1))),
            pl.BlockSpec((tm, tn), lambda i, j: (i, 0)),
            pl.BlockSpec((tm, tn), lambda i, j: (0, 0)),
            pl.BlockSpec((tm, tn), lambda i, j: (i, jnp.clip(j - 3, 0, 7))),
            pl.BlockSpec((tm, LANES), lambda i, j: (i, 0)),
            pl.BlockSpec((16, tm), lambda i, j: (0, i)),
        ] + extra_specs,
        out_shape=[
            jax.ShapeDtypeStruct((rows, ATT_WIDTH), BF16),
            jax.ShapeDtypeStruct((rows, 2 * KV_WIDTH), BF16),
            jax.ShapeDtypeStruct((tm, 2 * KV_WIDTH), F32),
            jax.ShapeDtypeStruct((rows, 4 * MLSTM_WIDTH), BF16),
            jax.ShapeDtypeStruct((rows, LANES), F32),
            jax.ShapeDtypeStruct((16, rows), F32),
        ] + extra_shapes,
        scratch_shapes=[pltpu.VMEM((tm, D_MODEL), BF16)],
        compiler_params=_cparams(("arbitrary", "arbitrary")),
        name="inproj_s" if per_row else "inproj_p",
    )(x, mod, mod, norm_w, wq_t, w_in_t, wg, wgt, cos, sa, sb)


def _attn_block(sink_ref, q_ref, row0, kv2, allowed, store):
    w = WINDOW
    grp = ATT_GROUP
    rows = grp * w
    member = lax.broadcasted_iota(jnp.int32, (rows, 1), 0) // w
    low = lax.broadcasted_iota(jnp.int32, (2 * w, LANES), 1) < ATT_HEAD_DIM
    low_o = lax.broadcasted_iota(jnp.int32, (rows, LANES), 1) < ATT_HEAD_DIM
    key_row = lax.broadcasted_iota(jnp.int32, (4 * w, LANES), 0)
    key_lane = lax.broadcasted_iota(jnp.int32, (4 * w, LANES), 1)
    ones_bd = (((key_row < 2 * w) & (key_lane < ATT_HEAD_DIM))
               | ((key_row >= 2 * w) & (key_lane >= ATT_HEAD_DIM))).astype(BF16)
    zero = jnp.zeros((2 * w, LANES), BF16)
    for cp in range(2):
        k128 = kv2[:, LANES * cp:LANES * (cp + 1)]
        v128 = kv2[:, KV_WIDTH + LANES * cp:KV_WIDTH + LANES * (cp + 1)]
        kbd = jnp.concatenate([jnp.where(low, k128, zero), jnp.where(low, zero, k128)], axis=0)
        vbd = jnp.concatenate([jnp.where(low, v128, zero), jnp.where(low, zero, v128)], axis=0)
        v_aug = jnp.concatenate([vbd, ones_bd], axis=1)
        q4 = jnp.concatenate([q_ref[row0:row0 + w, 256 * r + LANES * cp:256 * r + LANES * (cp + 1)]
                              for r in range(grp)], axis=0)
        s = _dot_nt(q4, kbd)
        es, tails = [], []
        for half in range(2):
            sh = jnp.where(allowed, s[:, 2 * w * half:2 * w * (half + 1)], -jnp.inf)
            head0 = (2 * cp + half) * grp
            sink = jnp.full((rows, 1), sink_ref[head0], F32)
            for r in range(1, grp):
                sink = jnp.where(member == r, sink_ref[head0 + r], sink)
            m = jnp.maximum(jnp.max(sh, axis=-1, keepdims=True), sink)
            es.append(jnp.exp(sh - m).astype(BF16))
            tails.append(jnp.exp(sink - m))
        oa = _dot(jnp.concatenate(es, axis=1), v_aug)
        l = oa[:, LANES:] + jnp.where(low_o, tails[0], tails[1])
        o = (oa[:, :LANES] / l).astype(BF16)
        for r in range(grp):
            store(256 * r + LANES * cp, o[w * r:w * (r + 1)])


MIX_TM = 512


def _mix_p_kernel(sink_ref, q_ref, kvp_ref, kvc_ref, mh_ref, x_ref, g1_ref, sh_ref, sc_ref, nw_ref, wa_ref, wm_ref,
                  x1_ref, h2_ref, att_scr):
    s = pl.program_id(0)
    w = WINDOW

    @pl.when(s == 0)
    def _():
        att_scr[...] = jnp.zeros_like(att_scr)

    slot = s % 2
    y = _dot(att_scr[1 - slot], wa_ref[...]) + _dot(mh_ref[...], wm_ref[...])
    x1 = x_ref[...] + g1_ref[0:1, :] * y
    x1_ref[...] = x1
    r = lax.rsqrt(jnp.mean(x1 * x1, axis=-1, keepdims=True) + EPS)
    h2_ref[...] = ((x1 * r * nw_ref[...]) * (1.0 + sc_ref[0:1, :]) + sh_ref[0:1, :]).astype(BF16)

    rows = ATT_GROUP * w
    qi = lax.broadcasted_iota(jnp.int32, (rows, 2 * w), 0) % w
    kj = lax.broadcasted_iota(jnp.int32, (rows, 2 * w), 1)
    first_off = jnp.where(s > 0, 0, 4 * w)
    causal = (kj >= w) & (kj - w <= qi)
    for blk in range(MIX_TM // w):
        prev = kvp_ref[...] if blk == 0 else kvc_ref[w * (blk - 1):w * blk, :]
        kv2 = jnp.concatenate([prev, kvc_ref[w * blk:w * (blk + 1), :]], axis=0)
        allowed = ((kj < w) & (kj > qi + (first_off if blk == 0 else 0))) | causal

        def store(c0, val, blk=blk):
            att_scr[slot, w * blk:w * (blk + 1), c0:c0 + LANES] = val

        _attn_block(sink_ref, q_ref, w * blk, kv2, allowed, store)


def _mix_p(sinks, q, kv, mh, x, mod, mod_row_block, norm_w, w_att, w_m):
    tm = MIX_TM
    tiles = SEQ // tm
    bpt = tm // WINDOW
    att_tile = lambda s: jnp.minimum(s, tiles - 1)
    out_tile = lambda s: jnp.maximum(s - 1, 0)
    return pl.pallas_call(
        _mix_p_kernel,
        grid=(tiles + 1,),
        in_specs=[
            pl.BlockSpec(memory_space=pltpu.SMEM),
            pl.BlockSpec((tm, ATT_WIDTH), lambda s: (att_tile(s), 0)),
            pl.BlockSpec((WINDOW, 2 * KV_WIDTH), lambda s: (jnp.maximum(bpt * att_tile(s) - 1, 0), 0)),
            pl.BlockSpec((tm, 2 * KV_WIDTH), lambda s: (att_tile(s), 0)),
            pl.BlockSpec((tm, MLSTM_WIDTH), lambda s: (out_tile(s), 0)),
            pl.BlockSpec((tm, D_MODEL), lambda s: (out_tile(s), 0)),
            pl.BlockSpec((8, D_MODEL), lambda s: (mod_row_block, 2)),
            pl.BlockSpec((8, D_MODEL), lambda s: (mod_row_block, 3)),
            pl.BlockSpec((8, D_MODEL), lambda s: (mod_row_block, 4)),
            pl.BlockSpec((1, D_MODEL), lambda s: (0, 0)),
            pl.BlockSpec((ATT_WIDTH, D_MODEL), lambda s: (0, 0)),
            pl.BlockSpec((MLSTM_WIDTH, D_MODEL), lambda s: (0, 0)),
        ],
        out_specs=[
            pl.BlockSpec((tm, D_MODEL), lambda s: (out_tile(s), 0)),
            pl.BlockSpec((tm, D_MODEL), lambda s: (out_tile(s), 0)),
        ],
        out_shape=[
            jax.ShapeDtypeStruct((SEQ, D_MODEL), F32),
            jax.ShapeDtypeStruct((SEQ, D_MODEL), BF16),
        ],
        scratch_shapes=[pltpu.VMEM((2, tm, ATT_WIDTH), BF16)],
        compiler_params=_cparams(("arbitrary",)),
        name="mix_p",
    )(sinks, q, kv, kv, mh, x, mod, mod, mod, norm_w, w_att, w_m)


ATT_S_BB = 8


def _attn_s_kernel(sink_ref, q_ref, kv32_ref, ck_ref, cv_ref, o_ref, ko_ref, vo_ref, q32_scr):
    t_new = DEC_SEQ
    w = WINDOW
    q32_scr[...] = q_ref[...].astype(F32)
    rows = 4 * 4 * 8
    row = lax.broadcasted_iota(jnp.int32, (rows, w), 0)
    slot = lax.broadcasted_iota(jnp.int32, (rows, w), 1)
    t_row = row % t_new
    second = (row % 8) >= t_new
    win_ok = (slot < w - t_new) | (slot - (w - t_new) <= t_row)
    old_ok = (slot >= 1) & (slot < t_new) & (slot > t_row)
    lane256 = lax.broadcasted_iota(jnp.int32, (32, 2 * LANES), 1)
    sink = sink_ref[...][:, 0:1]
    kv_new = jnp.concatenate([kv32_ref[...], jnp.zeros((w - ATT_S_BB * t_new, 2 * KV_WIDTH), F32)], axis=0)
    kv_t = kv_new.T
    new_slot = lax.broadcasted_iota(jnp.int32, (KV_WIDTH, w), 1) >= w - t_new
    for b in range(ATT_S_BB):
        cols = pltpu.roll(kv_t, w - t_new - t_new * b, 1)
        k_shift = pltpu.roll(ck_ref[b].reshape(KV_WIDTH, w), w - t_new, 1)
        v_shift = pltpu.roll(cv_ref[b].reshape(KV_WIDTH, w), w - t_new, 1)
        ko_ref[b] = jnp.where(new_slot, cols[:KV_WIDTH], k_shift).reshape(ATT_KV_HEADS, ATT_HEAD_DIM, w)
        vo_ref[b] = jnp.where(new_slot, cols[KV_WIDTH:], v_shift).reshape(ATT_KV_HEADS, ATT_HEAD_DIM, w)
    for pair in range(ATT_S_BB // 2):
        b0, b1 = 2 * pair, 2 * pair + 1
        q32 = jnp.concatenate([q32_scr[8 * pair:8 * (pair + 1), 256 * r:256 * (r + 1)] for r in range(ATT_GROUP)],
                              axis=0)
        qpad = jnp.concatenate(
            [jnp.where((lane256 // ATT_HEAD_DIM) == g, q32, 0.0) for g in range(ATT_KV_HEADS)], axis=0).astype(BF16)
        kw = [ko_ref[b].reshape(KV_WIDTH, w).astype(BF16) for b in (b0, b1)]
        vw = [vo_ref[b].reshape(KV_WIDTH, w).astype(BF16) for b in (b0, b1)]
        kc = [ck_ref[b].reshape(KV_WIDTH, w).astype(BF16) for b in (b0, b1)]
        vc = [cv_ref[b].reshape(KV_WIDTH, w).astype(BF16) for b in (b0, b1)]
        s_w = jnp.where(second, _dot(qpad, kw[1]), _dot(qpad, kw[0]))
        s_c = jnp.where(second, _dot(qpad, kc[1]), _dot(qpad, kc[0]))
        s_w = jnp.where(win_ok, s_w, -jnp.inf)
        s_c = jnp.where(old_ok, s_c, -jnp.inf)
        m = jnp.maximum(jnp.maximum(jnp.max(s_w, axis=-1, keepdims=True), jnp.max(s_c, axis=-1, keepdims=True)), sink)
        e_w = jnp.exp(s_w - m)
        e_c = jnp.exp(s_c - m)
        l = jnp.sum(e_w, axis=-1, keepdims=True) + jnp.sum(e_c, axis=-1, keepdims=True) + jnp.exp(sink - m)
        p_w = e_w / l
        p_c = e_c / l
        zero = jnp.zeros_like(p_w)
        o = (_dot_nt(jnp.where(second, zero, p_w).astype(BF16), vw[0])
             + _dot_nt(jnp.where(second, p_w, zero).astype(BF16), vw[1])
             + _dot_nt(jnp.where(second, zero, p_c).astype(BF16), vc[0])
             + _dot_nt(jnp.where(second, p_c, zero).astype(BF16), vc[1]))
        o32 = jnp.zeros((32, 2 * LANES), F32)
        for g in range(ATT_KV_HEADS):
            o32 = jnp.where((lane256 // ATT_HEAD_DIM) == g, o[32 * g:32 * (g + 1), :], o32)
        for r in range(ATT_GROUP):
            o_ref[8 * pair:8 * (pair + 1), 256 * r:256 * (r + 1)] = o32[8 * r:8 * (r + 1), :]


def _attn_s(sink_col, q, kv32, ck, cv):
    bb = ATT_S_BB
    rows = bb * DEC_SEQ
    cache_block = (bb, ATT_KV_HEADS, ATT_HEAD_DIM, WINDOW)
    cache_shape = (DEC_BATCH, ATT_KV_HEADS, ATT_HEAD_DIM, WINDOW)
    return pl.pallas_call(
        _attn_s_kernel,
        grid=(DEC_BATCH // bb,),
        in_specs=[
            pl.BlockSpec((128, LANES), lambda i: (0, 0)),
            pl.BlockSpec((rows, ATT_WIDTH), lambda i: (i, 0)),
            pl.BlockSpec((rows, 2 * KV_WIDTH), lambda i: (i, 0)),
            pl.BlockSpec(cache_block, lambda i: (i, 0, 0, 0)),
            pl.BlockSpec(cache_block, lambda i: (i, 0, 0, 0)),
        ],
        out_specs=[
            pl.BlockSpec((rows, ATT_WIDTH), lambda i: (i, 0)),
            pl.BlockSpec(cache_block, lambda i: (i, 0, 0, 0)),
            pl.BlockSpec(cache_block, lambda i: (i, 0, 0, 0)),
        ],
        out_shape=[
            jax.ShapeDtypeStruct((S_ROWS, ATT_WIDTH), F32),
            jax.ShapeDtypeStruct(cache_shape, F32),
            jax.ShapeDtypeStruct(cache_shape, F32),
        ],
        scratch_shapes=[pltpu.VMEM((rows, ATT_WIDTH), F32)],
        compiler_params=_cparams(("arbitrary",)),
        name="attn_s",
    )(sink_col, q, kv32, ck, cv)


def _head_norm_gate(h, nw, mo):
    hn = h * lax.rsqrt(jnp.mean(h * h, axis=-1, keepdims=True) + EPS) * nw
    return hn * _sigmoid(mo.astype(F32))


def _mlstm_p_kernel(q_ref, k_ref, v_ref, mo_ref, g_ref, gt_ref, brow_ref, bcol_ref, nw_ref,
                    mh_ref, c_out, n_out, m_out, c_scr, n_scr, m_scr):
    c = pl.program_id(0)
    L = MLSTM_CHUNK_P
    dh = MLSTM_HEAD_DIM

    @pl.when(c == 0)
    def _():
        c_scr[...] = jnp.zeros_like(c_scr)
        n_scr[...] = jnp.zeros_like(n_scr)
        m_scr[...] = jnp.zeros_like(m_scr)

    ti = lax.broadcasted_iota(jnp.int32, (L, L), 0)
    si = lax.broadcasted_iota(jnp.int32, (L, L), 1)
    causal = si <= ti
    tri = causal.astype(F32)
    tri_t = (ti <= si).astype(F32)
    gates = g_ref[...] + brow_ref[...]
    gates_t = gt_ref[...] + bcol_ref[...]
    b_col = jnp.dot(tri, _log_sigmoid(gates), precision=HI, preferred_element_type=F32)
    b_row = jnp.dot(_log_sigmoid(gates_t), tri_t, precision=HI, preferred_element_type=F32)
    for hd in range(MLSTM_HEADS):
        cs = slice(dh * hd, dh * (hd + 1))
        b_c = b_col[:, MLSTM_HEADS + hd:MLSTM_HEADS + hd + 1]
        li_c = gates[:, hd:hd + 1]
        b_r = b_row[MLSTM_HEADS + hd:MLSTM_HEADS + hd + 1, :]
        li_r = gates_t[hd:hd + 1, :]
        dm = jnp.where(causal, b_c - b_r + li_r, -jnp.inf)
        m_prev = m_scr[hd:hd + 1, 0:1]
        m_inter = b_c + m_prev
        m_t = jnp.maximum(m_inter, jnp.max(dm, axis=-1, keepdims=True))
        q = q_ref[:, cs] * (dh ** -0.5)
        k = k_ref[:, cs]
        v = v_ref[:, cs]
        sm = _dot_nt(q, k) * jnp.exp(dm - m_t)
        a = jnp.exp(m_inter - m_t)
        c_old = c_scr[hd]
        n_old = n_scr[hd:hd + 1, :]
        num = a * _dot_nt(q, c_old.astype(BF16)) + _dot(sm.astype(BF16), v)
        qn = jnp.sum(q.astype(F32) * n_old, axis=-1, keepdims=True)
        den = a * qn + jnp.sum(sm, axis=-1, keepdims=True)
        h = num / jnp.maximum(jnp.abs(den), jnp.exp(-m_t))
        mh_ref[:, cs] = _head_norm_gate(h, nw_ref[:, cs], mo_ref[:, cs]).astype(BF16)
        m_new = m_t[L - 1:L, :]
        b_last = b_c[L - 1:L, :]
        g = jnp.exp(b_last - b_c + li_c - m_new)
        decay = jnp.exp(b_last + m_prev - m_new)
        gv = (g * v.astype(F32)).astype(BF16)
        c_scr[hd] = decay * c_old + _dot_tn(gv, k)
        n_scr[hd:hd + 1, :] = decay * n_old + jnp.sum(g * k.astype(F32), axis=0, keepdims=True)
        m_scr[hd:hd + 1, :] = jnp.broadcast_to(m_new, (1, LANES))

    @pl.when(c == pl.num_programs(0) - 1)
    def _():
        c_out[...] = c_scr[...]
        n_out[...] = n_scr[...]
        m_out[...] = m_scr[...]


def _mlstm_p(m_all, gates, gates_t, brow, bcol, nw):
    L = MLSTM_CHUNK_P
    nc = SEQ // L
    dh = MLSTM_HEAD_DIM
    return pl.pallas_call(
        _mlstm_p_kernel,
        grid=(nc,),
        in_specs=[
            pl.BlockSpec((L, MLSTM_WIDTH), lambda c: (c, 0)),
            pl.BlockSpec((L, MLSTM_WIDTH), lambda c: (c, 1)),
            pl.BlockSpec((L, MLSTM_WIDTH), lambda c: (c, 2)),
            pl.BlockSpec((L, MLSTM_WIDTH), lambda c: (c, 3)),
            pl.BlockSpec((L, LANES), lambda c: (c, 0)),
            pl.BlockSpec((16, L), lambda c: (0, c)),
            pl.BlockSpec((1, LANES), lambda c: (0, 0)),
            pl.BlockSpec((16, L), lambda c: (0, 0)),
            pl.BlockSpec((1, MLSTM_WIDTH), lambda c: (0, 0)),
        ],
        out_specs=[
            pl.BlockSpec((L, MLSTM_WIDTH), lambda c: (c, 0)),
            pl.BlockSpec((MLSTM_HEADS, dh, dh), lambda c: (0, 0, 0)),
            pl.BlockSpec((8, dh), lambda c: (0, 0)),
            pl.BlockSpec((8, LANES), lambda c: (0, 0)),
        ],
        out_shape=[
            jax.ShapeDtypeStruct((SEQ, MLSTM_WIDTH), BF16),
            jax.ShapeDtypeStruct((MLSTM_HEADS, dh, dh), F32),
            jax.ShapeDtypeStruct((8, dh), F32),
            jax.ShapeDtypeStruct((8, LANES), F32),
        ],
        scratch_shapes=[
            pltpu.VMEM((MLSTM_HEADS, dh, dh), F32),
            pltpu.VMEM((8, dh), F32),
            pltpu.VMEM((8, LANES), F32),
        ],
        compiler_params=_cparams(("arbitrary",)),
        name="mlstm_p",
    )(m_all, m_all, m_all, m_all, gates, gates_t, brow, bcol, nw)


MLSTM_S_BB = 4


def _mlstm_s_kernel(q_ref, k_ref, v_ref, mo_ref, g_ref, nrep_ref, mrep_ref, brow_ref, nw_ref, c_ref,
                    mh_ref, c_out, nrow_ref, mrow_ref):
    T = DEC_SEQ
    R = MLSTM_S_BB * T
    H = MLSTM_HEADS
    dh = MLSTM_HEAD_DIM

    def shift(x, d):
        return pltpu.roll(x, d, 0)

    lanes = lax.broadcasted_iota(jnp.int32, (R, LANES), 1)
    tmod = lax.broadcasted_iota(jnp.int32, (R, LANES), 0) % T
    tmod_w = lax.broadcasted_iota(jnp.int32, (R, MLSTM_WIDTH), 0) % T
    head_ok = lanes < H
    gates = g_ref[...] + brow_ref[...]
    li = jnp.where(head_ok, gates, 0.0)
    lf = jnp.where(head_ok, pltpu.roll(_log_sigmoid(gates), LANES - H, 1), 0.0)
    bcum = lf
    for d in range(1, T):
        bcum = bcum + jnp.where(tmod >= d, shift(lf, d), 0.0)
    m0 = mrep_ref[...]
    m_inter = bcum + m0
    dms = [li] + [jnp.where(tmod >= d, bcum - shift(bcum, d) + shift(li, d), -jnp.inf) for d in range(1, T)]
    m_t = m_inter
    for dm in dms:
        m_t = jnp.maximum(m_t, dm)
    a = jnp.exp(m_inter - m_t)
    ws = [jnp.exp(dm - m_t) for dm in dms]

    q_bf = q_ref[...] * (dh ** -0.5)
    q = q_bf.astype(F32)
    k = k_ref[...].astype(F32)
    v = v_ref[...].astype(F32)
    seg = (lax.broadcasted_iota(jnp.int32, (MLSTM_WIDTH, LANES), 0) // dh
           == lax.broadcasted_iota(jnp.int32, (MLSTM_WIDTH, LANES), 1)).astype(F32)
    ex = (lax.broadcasted_iota(jnp.int32, (LANES, MLSTM_WIDTH), 1) // dh
          == lax.broadcasted_iota(jnp.int32, (LANES, MLSTM_WIDTH), 0)).astype(F32)

    def segsum(x):
        return jnp.dot(x, seg, precision=HI, preferred_element_type=F32)

    def expand(x):
        return jnp.dot(x, ex, precision=HI, preferred_element_type=F32)

    ks = [k] + [shift(k, d) for d in range(1, T)]
    vs = [v] + [shift(v, d) for d in range(1, T)]
    sms = [segsum(q * ks[d]) * ws[d] for d in range(T)]
    den = a * segsum(q * nrep_ref[...])
    for sm in sms:
        den = den + sm
    inv = 1.0 / jnp.maximum(jnp.abs(den), jnp.exp(-m_t))

    def last(x):
        out = jnp.zeros_like(x)
        for jj in range(T):
            out = jnp.where(tmod == T - 1 - jj, x if jj == 0 else pltpu.roll(x, R - jj, 0), out)
        return out

    m_new = last(m_t)
    b_last = last(bcum)
    g = jnp.where(head_ok, jnp.exp(b_last - bcum + li - m_new), 0.0)
    decay = jnp.where(head_ok, jnp.exp(b_last + m0 - m_new), 0.0)
    a_f = expand(a * inv)
    w_f = [expand(sm * inv) for sm in sms]
    g_f = expand(g)
    d_f = expand(decay)
    gv = (g_f * v).astype(BF16)
    rowb = lax.broadcasted_iota(jnp.int32, (R, dh), 0) // T
    for hd in range(H):
        cs = slice(dh * hd, dh * (hd + 1))
        qh = q_bf[:, cs]
        kh = k_ref[:, cs]
        gvh = gv[:, cs]
        qc = jnp.zeros((R, dh), F32)
        for bb in range(MLSTM_S_BB):
            c_old = c_ref[bb, hd]
            qc = jnp.where(rowb == bb, _dot_nt(qh, c_old.astype(BF16)), qc)
            upd = _dot_tn(jnp.where(rowb == bb, gvh, jnp.zeros_like(gvh)), kh)
            c_out[bb, hd] = d_f[T * bb:T * bb + 1, cs] * c_old + upd
        h = a_f[:, cs] * qc
        for d in range(T):
            h = h + w_f[d][:, cs] * vs[d][:, cs]
        mh_ref[:, cs] = _head_norm_gate(h, nw_ref[:, cs], mo_ref[:, cs]).astype(BF16)
    gk = g_f * k
    nsum = gk
    for d in range(1, T):
        nsum = nsum + jnp.where(tmod_w >= d, shift(gk, d), 0.0)
    nrow_ref[...] = d_f * nrep_ref[...] + nsum
    mrow_ref[...] = m_t


def _mlstm_s(m_all, gates, n_rep, m_rep, brow, nw, state_c):
    bb = MLSTM_S_BB
    R = bb * DEC_SEQ
    dh = MLSTM_HEAD_DIM
    H = MLSTM_HEADS
    return pl.pallas_call(
        _mlstm_s_kernel,
        grid=(DEC_BATCH // bb,),
        in_specs=[
            pl.BlockSpec((R, MLSTM_WIDTH), lambda i: (i, 0)),
            pl.BlockSpec((R, MLSTM_WIDTH), lambda i: (i, 1)),
            pl.BlockSpec((R, MLSTM_WIDTH), lambda i: (i, 2)),
            pl.BlockSpec((R, MLSTM_WIDTH), lambda i: (i, 3)),
            pl.BlockSpec((R, LANES), lambda i: (i, 0)),
            pl.BlockSpec((R, MLSTM_WIDTH), lambda i: (i, 0)),
            pl.BlockSpec((R, LANES), lambda i: (i, 0)),
            pl.BlockSpec((1, LANES), lambda i: (0, 0)),
            pl.BlockSpec((1, MLSTM_WIDTH), lambda i: (0, 0)),
            pl.BlockSpec((bb, H, dh, dh), lambda i: (i, 0, 0, 0)),
        ],
        out_specs=[
            pl.BlockSpec((R, MLSTM_WIDTH), lambda i: (i, 0)),
            pl.BlockSpec((bb, H, dh, dh), lambda i: (i, 0, 0, 0)),
            pl.BlockSpec((R, MLSTM_WIDTH), lambda i: (i, 0)),
            pl.BlockSpec((R, LANES), lambda i: (i, 0)),
        ],
        out_shape=[
            jax.ShapeDtypeStruct((S_ROWS, MLSTM_WIDTH), BF16),
            jax.ShapeDtypeStruct((DEC_BATCH, H, dh, dh), F32),
            jax.ShapeDtypeStruct((S_ROWS, MLSTM_WIDTH), F32),
            jax.ShapeDtypeStruct((S_ROWS, LANES), F32),
        ],
        compiler_params=_cparams(("arbitrary",)),
        name="mlstm_s",
    )(m_all, m_all, m_all, m_all, gates, n_rep, m_rep, brow, nw, state_c)


def _outproj_kernel(att_ref, mh_ref, x_ref, g1_ref, sh_ref, sc_ref, nw_ref, wa_ref, wm_ref, x1_ref, h2_ref, *,
                    per_row):
    y = _dot(att_ref[...].astype(BF16), wa_ref[...]) + _dot(mh_ref[...], wm_ref[...])
    x1 = x_ref[...] + _mod_row(g1_ref, per_row) * y
    x1_ref[...] = x1
    r = lax.rsqrt(jnp.mean(x1 * x1, axis=-1, keepdims=True) + EPS)
    h2 = (x1 * r * nw_ref[...]) * (1.0 + _mod_row(sc_ref, per_row)) + _mod_row(sh_ref, per_row)
    h2_ref[...] = h2.astype(BF16)


def _outproj(att, mh, x, mod, mod_row_block, norm_w, w_att, w_m, *, tm, per_row):
    rows = x.shape[0]
    mod_rows = tm if per_row else 8
    mod_idx = (lambda i: i) if per_row else (lambda i: mod_row_block)
    kern = functools.partial(_outproj_kernel, per_row=per_row)
    return pl.pallas_call(
        kern,
        grid=(rows // tm,),
        in_specs=[
            pl.BlockSpec((tm, ATT_WIDTH), lambda i: (i, 0)),
            pl.BlockSpec((tm, MLSTM_WIDTH), lambda i: (i, 0)),
            pl.BlockSpec((tm, D_MODEL), lambda i: (i, 0)),
            pl.BlockSpec((mod_rows, D_MODEL), lambda i: (mod_idx(i), 2)),
            pl.BlockSpec((mod_rows, D_MODEL), lambda i: (mod_idx(i), 3)),
            pl.BlockSpec((mod_rows, D_MODEL), lambda i: (mod_idx(i), 4)),
            pl.BlockSpec((1, D_MODEL), lambda i: (0, 0)),
            pl.BlockSpec((ATT_WIDTH, D_MODEL), lambda i: (0, 0)),
            pl.BlockSpec((MLSTM_WIDTH, D_MODEL), lambda i: (0, 0)),
        ],
        out_specs=[
            pl.BlockSpec((tm, D_MODEL), lambda i: (i, 0)),
            pl.BlockSpec((tm, D_MODEL), lambda i: (i, 0)),
        ],
        out_shape=[
            jax.ShapeDtypeStruct((rows, D_MODEL), F32),
            jax.ShapeDtypeStruct((rows, D_MODEL), BF16),
        ],
        compiler_params=_cparams(("arbitrary",)),
        name="outproj_s" if per_row else "outproj_p",
    )(att, mh, x, mod, mod, mod, norm_w, w_att, w_m)


def _ffn_a_kernel(h_ref, wg_ref, wu_ref, a_ref, wg_scr, wu_scr):
    @pl.when(pl.program_id(1) == 0)
    def _():
        wg_scr[...] = wg_ref[...].astype(BF16)
        wu_scr[...] = wu_ref[...].astype(BF16)

    h = h_ref[...]
    g = _dot(h, wg_scr[...])
    u = _dot(h, wu_scr[...])
    a_ref[...] = (g * _sigmoid(g) * u).astype(BF16)


def _ffn_a(h2, w_gate, w_up, *, tm):
    rows = h2.shape[0]
    tn = 512
    return pl.pallas_call(
        _ffn_a_kernel,
        grid=(D_FF // tn, rows // tm),
        in_specs=[
            pl.BlockSpec((tm, D_MODEL), lambda j, i: (i, 0)),
            pl.BlockSpec((D_MODEL, tn), lambda j, i: (0, j)),
            pl.BlockSpec((D_MODEL, tn), lambda j, i: (0, j)),
        ],
        out_specs=pl.BlockSpec((tm, tn), lambda j, i: (i, j)),
        out_shape=jax.ShapeDtypeStruct((rows, D_FF), BF16),
        scratch_shapes=[pltpu.VMEM((D_MODEL, tn), BF16), pltpu.VMEM((D_MODEL, tn), BF16)],
        compiler_params=_cparams(("arbitrary", "arbitrary")),
        name="ffn_a",
    )(h2, w_gate, w_up)


def _ffn_b_kernel(a_ref, wd_ref, x1_ref, g2_ref, fw_ref, y_ref, *rest, per_row, emit_w):
    kk = pl.program_id(1)

    def weight_tile():
        wt = wd_ref[...].astype(BF16)
        if emit_w:
            rest[0][...] = wt
        return wt

    @pl.when(kk == 0)
    def _():
        y_ref[...] = _dot(a_ref[...], weight_tile())

    @pl.when(kk > 0)
    def _():
        y_ref[...] += _dot(a_ref[...], weight_tile())

    @pl.when(kk == pl.num_programs(1) - 1)
    def _():
        x2 = x1_ref[...] + _mod_row(g2_ref, per_row) * y_ref[...]
        y_ref[...] = x2 * lax.rsqrt(jnp.mean(x2 * x2, axis=-1, keepdims=True) + EPS) * fw_ref[...]


def _ffn_b(a, w_down, x1, mod, mod_row_block, final_w, *, tm, per_row, emit_w):
    rows = a.shape[0]
    tk = 512
    mod_rows = tm if per_row else 8
    mod_idx = (lambda i: i) if per_row else (lambda i: mod_row_block)
    kern = functools.partial(_ffn_b_kernel, per_row=per_row, emit_w=emit_w)
    w_spec = pl.BlockSpec((tk, D_MODEL), lambda i, k: (k, 0))
    y_spec = pl.BlockSpec((tm, D_MODEL), lambda i, k: (i, 0))
    y_shape = jax.ShapeDtypeStruct((rows, D_MODEL), F32)
    return pl.pallas_call(
        kern,
        grid=(rows // tm, D_FF // tk),
        in_specs=[
            pl.BlockSpec((tm, tk), lambda i, k: (i, k)),
            w_spec,
            pl.BlockSpec((tm, D_MODEL), lambda i, k: (i, 0)),
            pl.BlockSpec((mod_rows, D_MODEL), lambda i, k: (mod_idx(i), 5)),
            pl.BlockSpec((1, D_MODEL), lambda i, k: (0, 0)),
        ],
        out_specs=[y_spec, w_spec] if emit_w else y_spec,
        out_shape=[y_shape, jax.ShapeDtypeStruct((D_FF, D_MODEL), BF16)] if emit_w else y_shape,
        compiler_params=_cparams(("arbitrary", "arbitrary")),
        name="ffn_b_s" if per_row else "ffn_b_p",
    )(a, w_down, x1, mod, final_w)


def _rope_tables(pos):
    half = ROPE_DIM // 2
    inv = np.float32(ROPE_THETA) ** (-np.arange(0, ROPE_DIM, 2, dtype=np.float32) / np.float32(ROPE_DIM))
    d = np.arange(LANES) % ATT_HEAD_DIM
    ang = pos.astype(np.float32)[:, None] * inv[d % half][None, :].astype(np.float32)
    cos, sin = np.cos(ang), np.sin(ang)
    d = d[None, :]
    tables = (np.where(d < ROPE_DIM, cos, 1.0), np.where(d < half, -sin, 0.0),
              np.where((d >= half) & (d < ROPE_DIM), sin, 0.0))
    return tuple(jnp.asarray(t.astype(np.float32)) for t in tables)


def kernel(x_prompt, x_sample, cache_k_win, cache_v_win, state_C, state_n, state_m, c_prompt, c_sample,
           norm1_w, norm2_w, final_norm_w, w_ada, b_ada, w_in, b_ig, b_fg, attn_sinks, mh_norm_w,
           w_out, w_gate, w_up, w_down):
    assert w_in.shape[0] == 1, "single-layer trunk"
    T = DEC_SEQ
    xp = x_prompt[0]
    xs = x_sample.reshape(S_ROWS, D_MODEL)

    c_all = jnp.concatenate([jnp.repeat(c_sample, T, axis=0), c_prompt, jnp.zeros((15, D_MODEL), F32)], axis=0)
    mod = _ada(c_all, w_ada[0], b_ada)
    prompt_mod_block = S_ROWS // 8

    w_in_t = jnp.transpose(w_in[0])
    wq_t = (w_in_t[:ATT_WIDTH].reshape(ATT_KV_HEADS, ATT_GROUP, ATT_HEAD_DIM, D_MODEL)
            .transpose(1, 0, 2, 3).reshape(ATT_WIDTH, D_MODEL).astype(BF16))
    w_gates_t = w_in_t[MAIN_WIDTH:]
    wg = jnp.pad(w_gates_t.T, ((0, 0), (0, LANES - 2 * MLSTM_HEADS))).astype(BF16)
    wgt = jnp.pad(w_gates_t, ((0, 16 - 2 * MLSTM_HEADS), (0, 0))).astype(BF16)
    w_out_att = (w_out[0, :ATT_WIDTH].reshape(ATT_KV_HEADS, ATT_GROUP, ATT_HEAD_DIM, D_MODEL)
                 .transpose(1, 0, 2, 3).reshape(ATT_WIDTH, D_MODEL).astype(BF16))
    w_out_m = w_out[0, ATT_WIDTH:].astype(BF16)
    n1 = norm1_w.reshape(1, D_MODEL)
    n2 = norm2_w.reshape(1, D_MODEL)
    fw = final_norm_w.reshape(1, D_MODEL)
    nw = mh_norm_w.reshape(1, MLSTM_WIDTH)
    gate_bias = jnp.concatenate([b_ig[0], b_fg[0]])
    brow = jnp.pad(gate_bias, (0, LANES - 2 * MLSTM_HEADS)).reshape(1, LANES)
    bcol = jnp.broadcast_to(jnp.pad(gate_bias, (0, 16 - 2 * MLSTM_HEADS))[:, None], (16, MLSTM_CHUNK_P))

    rope_p = _rope_tables(np.arange(SEQ))
    rope_s = _rope_tables(np.tile(PAST_LEN + np.arange(T), DEC_BATCH))

    tm_p = 1024
    q_s, _, kv32_s, m_s, g_s, _, w_in_bf = _inproj(xs, mod, 0, n1, wq_t, w_in_t, wg, wgt, *rope_s,
                                                   tm=S_ROWS, per_row=True, emit_w=True)
    q_p, kv_p, kv32_p, m_p, g_p, gt_p = _inproj(xp, mod, prompt_mod_block, n1, wq_t, w_in_bf, wg, wgt, *rope_p,
                                                tm=tm_p, per_row=False, emit_w=False)

    sinks = attn_sinks[0]
    sink_col = jnp.broadcast_to(sinks.reshape(ATT_HEADS, 1, 1), (ATT_HEADS, 8, LANES)).reshape(128, LANES)
    ck = jnp.transpose(cache_k_win[0], (0, 2, 3, 1))
    cv = jnp.transpose(cache_v_win[0], (0, 2, 3, 1))
    att_s, kwin_s, vwin_s = _attn_s(sink_col, q_s, kv32_s, ck, cv)

    mh_p, c_p, n_p, mm_p = _mlstm_p(m_p, g_p, gt_p, brow, bcol, nw)
    n_rep = jnp.repeat(state_n[0].reshape(DEC_BATCH, MLSTM_WIDTH), T, axis=0)
    m_rep = jnp.pad(jnp.repeat(state_m[0], T, axis=0), ((0, 0), (0, LANES - MLSTM_HEADS)))
    mh_s, c_s, nrow_s, mrow_s = _mlstm_s(m_s, g_s, n_rep, m_rep, brow, nw, state_C[0])

    x1_p, h2_p = _mix_p(sinks, q_p, kv_p, mh_p, xp, mod, prompt_mod_block, n2, w_out_att, w_out_m)
    x1_s, h2_s = _outproj(att_s, mh_s, xs, mod, 0, n2, w_out_att, w_out_m, tm=S_ROWS, per_row=True)
    a_p = _ffn_a(h2_p, w_gate[0], w_up[0], tm=tm_p)
    a_s = _ffn_a(h2_s, w_gate[0], w_up[0], tm=S_ROWS)
    y_s, w_down_bf = _ffn_b(a_s, w_down[0], x1_s, mod, 0, fw, tm=S_ROWS, per_row=True, emit_w=True)
    y_p = _ffn_b(a_p, w_down_bf, x1_p, mod, prompt_mod_block, fw, tm=tm_p, per_row=False, emit_w=False)

    kv_shape = (1, 1, WINDOW, ATT_KV_HEADS, ATT_HEAD_DIM)
    kv_last = kv32_p[tm_p - WINDOW:]
    dh = MLSTM_HEAD_DIM
    return (
        y_p.reshape(1, SEQ, D_MODEL),
        y_s.reshape(DEC_BATCH, T, D_MODEL),
        kv_last[:, :KV_WIDTH].reshape(kv_shape),
        kv_last[:, KV_WIDTH:].reshape(kv_shape),
        c_p.reshape(1, 1, MLSTM_HEADS, dh, dh),
        n_p[:MLSTM_HEADS].reshape(1, 1, MLSTM_HEADS, dh),
        mm_p[:MLSTM_HEADS, 0].reshape(1, 1, MLSTM_HEADS),
        jnp.transpose(kwin_s, (0, 3, 1, 2))[None],
        jnp.transpose(vwin_s, (0, 3, 1, 2))[None],
        c_s.reshape(1, DEC_BATCH, MLSTM_HEADS, dh, dh),
        nrow_s[T - 1::T].reshape(1, DEC_BATCH, MLSTM_HEADS, dh),
        mrow_s[T - 1::T, :MLSTM_HEADS].reshape(1, DEC_BATCH, MLSTM_HEADS),
    )
```

```python
import functools

import jax
import jax.numpy as jnp
import numpy as np
from jax import lax
from jax.experimental import pallas as pl
from jax.experimental.pallas import tpu as pltpu

F32 = jnp.float32
BF16 = jnp.bfloat16

D_MODEL = 2048
SEQ = 8192
DEC_BATCH = 128
DEC_SEQ = 4
S_ROWS = DEC_BATCH * DEC_SEQ
PAST_LEN = 16384
ATT_HEADS = 16
ATT_KV_HEADS = 4
ATT_GROUP = 4
ATT_HEAD_DIM = 64
WINDOW = 128
ROPE_THETA = 500000.0
ROPE_DIM = 16
MLSTM_HEADS = 4
MLSTM_HEAD_DIM = 256
ATT_WIDTH = 1024
KV_WIDTH = 256
MLSTM_WIDTH = 1024
MAIN_WIDTH = ATT_WIDTH + 2 * KV_WIDTH + 4 * MLSTM_WIDTH
D_FF = 5632
N_MOD = 6
EPS = 1e-6

LANES = 128
MLSTM_CHUNK_P = 256
VMEM_LIMIT = 56 * 1024 * 1024

NT_DIMS = (((1,), (1,)), ((), ()))
TN_DIMS = (((0,), (0,)), ((), ()))
HI = lax.Precision.HIGHEST


def _cparams(sem):
    return pltpu.CompilerParams(dimension_semantics=sem, vmem_limit_bytes=VMEM_LIMIT)


def _dot(a, b):
    return jnp.dot(a, b, preferred_element_type=F32)


def _dot_nt(a, b):
    return lax.dot_general(a, b, NT_DIMS, preferred_element_type=F32)


def _dot_tn(a, b):
    return lax.dot_general(a, b, TN_DIMS, preferred_element_type=F32)


def _sigmoid(x):
    return 1.0 / (1.0 + jnp.exp(-x))


def _log_sigmoid(x):
    return jnp.minimum(x, 0.0) - jnp.log(1.0 + jnp.exp(-jnp.abs(x)))


def _mod_row(ref, per_row):
    return ref[...] if per_row else ref[0:1, :]


def _ada_kernel(c_ref, w_ref, b_ref, o_ref, s_scr):
    @pl.when(pl.program_id(0) == 0)
    def _():
        c = c_ref[...]
        s_scr[...] = (c * _sigmoid(c)).astype(BF16)

    o_ref[...] = _dot(s_scr[...], w_ref[...].astype(BF16)) + b_ref[...]


def _ada(c_all, w_ada, b_ada):
    m = c_all.shape[0]
    n = w_ada.shape[1]
    tn = 1024
    return pl.pallas_call(
        _ada_kernel,
        grid=(n // tn,),
        in_specs=[
            pl.BlockSpec((m, D_MODEL), lambda j: (0, 0)),
            pl.BlockSpec((D_MODEL, tn), lambda j: (0, j)),
            pl.BlockSpec((1, tn), lambda j: (0, j)),
        ],
        out_specs=pl.BlockSpec((m, tn), lambda j: (0, j)),
        out_shape=jax.ShapeDtypeStruct((m, n), F32),
        scratch_shapes=[pltpu.VMEM((m, D_MODEL), BF16)],
        compiler_params=_cparams(("arbitrary",)),
        name="ada",
    )(c_all, w_ada, b_ada)


def _rope_store(acc, cos, sa, sb, out_ref, ncols, scale):
    for c in range(ncols // LANES):
        xc = acc[:, LANES * c:LANES * (c + 1)]
        rot = xc * cos + pltpu.roll(xc, LANES - 8, 1) * sa + pltpu.roll(xc, 8, 1) * sb
        if scale != 1.0:
            rot = rot * scale
        out_ref[:, LANES * c:LANES * (c + 1)] = rot.astype(out_ref.dtype)


def _inproj_kernel(x_ref, sh_ref, sc_ref, nw_ref, wq_ref, win_ref, wg_ref, wgt_ref, cos_ref, sa_ref, sb_ref,
                   q_ref, kv_ref, kv32_ref, m_ref, g_ref, gt_ref, *rest, per_row, emit_w):
    h_scr = rest[-1]
    j = pl.program_id(1)

    def weight_tile():
        wt = win_ref[...].astype(BF16)
        if emit_w:
            rest[0][...] = wt
        return wt

    @pl.when(j == 0)
    def _():
        x = x_ref[...]
        r = lax.rsqrt(jnp.mean(x * x, axis=-1, keepdims=True) + EPS)
        y = x * r * nw_ref[...]
        h = (y * (1.0 + _mod_row(sc_ref, per_row)) + _mod_row(sh_ref, per_row)).astype(BF16)
        h_scr[...] = h
        g_ref[...] = _dot(h, wg_ref[...])
        gt_ref[...] = _dot_nt(wgt_ref[...], h)

    @pl.when(j < 2)
    def _():
        acc = _dot_nt(h_scr[...], wq_ref[...])
        _rope_store(acc, cos_ref[...], sa_ref[...], sb_ref[...], q_ref, 512, ATT_HEAD_DIM ** -0.5)

    @pl.when(j == 2)
    def _():
        acc = _dot_nt(h_scr[...], weight_tile())
        _rope_store(acc, cos_ref[...], sa_ref[...], sb_ref[...], kv32_ref, KV_WIDTH, 1.0)
        kv32_ref[:, KV_WIDTH:] = acc[:, KV_WIDTH:]
        kv_ref[...] = kv32_ref[...].astype(BF16)

    @pl.when(j > 2)
    def _():
        m_ref[...] = _dot_nt(h_scr[...], weight_tile()).astype(BF16)


def _inproj(x, mod, mod_row_block, norm_w, wq_t, w_in_t, wg, wgt, cos, sa, sb, *, tm, per_row, emit_w):
    rows = x.shape[0]
    tn = 512
    nj = MAIN_WIDTH // tn
    mod_rows = tm if per_row else 8
    mod_idx = (lambda i: i) if per_row else (lambda i: mod_row_block)
    kern = functools.partial(_inproj_kernel, per_row=per_row, emit_w=emit_w)
    w_spec = pl.BlockSpec((tn, D_MODEL), lambda i, j: (jnp.maximum(j, 2), 0))
    extra_specs = [w_spec] if emit_w else []
    extra_shapes = [jax.ShapeDtypeStruct((MAIN_WIDTH, D_MODEL), BF16)] if emit_w else []
    return pl.pallas_call(
        kern,
        grid=(rows // tm, nj),
        in_specs=[
            pl.BlockSpec((tm, D_MODEL), lambda i, j: (i, 0)),
            pl.BlockSpec((mod_rows, D_MODEL), lambda i, j: (mod_idx(i), 0)),
            pl.BlockSpec((mod_rows, D_MODEL), lambda i, j: (mod_idx(i), 1)),
            pl.BlockSpec((1, D_MODEL), lambda i, j: (0, 0)),
            pl.BlockSpec((tn, D_MODEL), lambda i, j: (jnp.minimum(j, 1), 0)),
            w_spec,
            pl.BlockSpec((D_MODEL, LANES), lambda i, j: (0, 0)),
            pl.BlockSpec((16, D_MODEL), lambda i, j: (0, 0)),
            pl.BlockSpec((tm, LANES), lambda i, j: (i, 0)),
            pl.BlockSpec((tm, LANES), lambda i, j: (i, 0)),
            pl.BlockSpec((tm, LANES), lambda i, j: (i, 0)),
        ],
        out_specs=[
            pl.BlockSpec((tm, tn), lambda i, j: (i, jnp.minimum(j, 1))),
            pl.BlockSpec((tm, tn), lambda i, j: (i, 0)),
            pl.BlockSpec((tm, tn), lambda i, j: (0, 0)),
            pl.BlockSpec((tm, tn), lambda i, j: (i, jnp.clip(j - 3, 0, 7))),
            pl.BlockSpec((tm, LANES), lambda i, j: (i, 0)),
            pl.BlockSpec((16, tm), lambda i, j: (0, i)),
        ] + extra_specs,
        out_shape=[
            jax.ShapeDtypeStruct((rows, ATT_WIDTH), BF16),
            jax.ShapeDtypeStruct((rows, 2 * KV_WIDTH), BF16),
            jax.ShapeDtypeStruct((tm, 2 * KV_WIDTH), F32),
            jax.ShapeDtypeStruct((rows, 4 * MLSTM_WIDTH), BF16),
            jax.ShapeDtypeStruct((rows, LANES), F32),
            jax.ShapeDtypeStruct((16, rows), F32),
        ] + extra_shapes,
        scratch_shapes=[pltpu.VMEM((tm, D_MODEL), BF16)],
        compiler_params=_cparams(("arbitrary", "arbitrary")),
        name="inproj_s" if per_row else "inproj_p",
    )(x, mod, mod, norm_w, wq_t, w_in_t, wg, wgt, cos, sa, sb)


def _attn_block(sink_ref, q_ref, row0, kv2, allowed, store):
    w = WINDOW
    grp = ATT_GROUP
    rows = grp * w
    member = lax.broadcasted_iota(jnp.int32, (rows, 1), 0) // w
    low = lax.broadcasted_iota(jnp.int32, (2 * w, LANES), 1) < ATT_HEAD_DIM
    low_o = lax.broadcasted_iota(jnp.int32, (rows, LANES), 1) < ATT_HEAD_DIM
    key_row = lax.broadcasted_iota(jnp.int32, (4 * w, LANES), 0)
    key_lane = lax.broadcasted_iota(jnp.int32, (4 * w, LANES), 1)
    ones_bd = (((key_row < 2 * w) & (key_lane < ATT_HEAD_DIM))
               | ((key_row >= 2 * w) & (key_lane >= ATT_HEAD_DIM))).astype(BF16)
    zero = jnp.zeros((2 * w, LANES), BF16)
    for cp in range(2):
        k128 = kv2[:, LANES * cp:LANES * (cp + 1)]
        v128 = kv2[:, KV_WIDTH + LANES * cp:KV_WIDTH + LANES * (cp + 1)]
        kbd = jnp.concatenate([jnp.where(low, k128, zero), jnp.where(low, zero, k128)], axis=0)
        vbd = jnp.concatenate([jnp.where(low, v128, zero), jnp.where(low, zero, v128)], axis=0)
        v_aug = jnp.concatenate([vbd, ones_bd], axis=1)
        q4 = jnp.concatenate([q_ref[row0:row0 + w, 256 * r + LANES * cp:256 * r + LANES * (cp + 1)]
                              for r in range(grp)], axis=0)
        s = _dot_nt(q4, kbd)
        es, tails = [], []
        for half in range(2):
            sh = jnp.where(allowed, s[:, 2 * w * half:2 * w * (half + 1)], -jnp.inf)
            head0 = (2 * cp + half) * grp
            sink = jnp.full((rows, 1), sink_ref[head0], F32)
            for r in range(1, grp):
                sink = jnp.where(member == r, sink_ref[head0 + r], sink)
            m = jnp.maximum(jnp.max(sh, axis=-1, keepdims=True), sink)
            es.append(jnp.exp(sh - m).astype(BF16))
            tails.append(jnp.exp(sink - m))
        oa = _dot(jnp.concatenate(es, axis=1), v_aug)
        l = oa[:, LANES:] + jnp.where(low_o, tails[0], tails[1])
        o = (oa[:, :LANES] / l).astype(BF16)
        for r in range(grp):
            store(256 * r + LANES * cp, o[w * r:w * (r + 1)])


MIX_TM = 512


def _mix_p_kernel(sink_ref, q_ref, kvp_ref, kvc_ref, mq_ref, mk_ref, mv_ref, mo_ref, g_ref, gt_ref, brow_ref, bcol_ref,
                  mnw_ref, x_ref, g1_ref, sh_ref, sc_ref, nw_ref, wa_ref, wm_ref,
                  x1_ref, h2_ref, c_out, n_out, m_out, att_scr, mh_scr, c_scr, n_scr, m_scr):
    s = pl.program_id(0)
    w = WINDOW
    last_tile = pl.num_programs(0) - 2

    @pl.when(s == 0)
    def _():
        att_scr[...] = jnp.zeros_like(att_scr)
        mh_scr[...] = jnp.zeros_like(mh_scr)
        c_scr[...] = jnp.zeros_like(c_scr)
        n_scr[...] = jnp.zeros_like(n_scr)
        m_scr[...] = jnp.zeros_like(m_scr)

    slot = s % 2
    y = _dot(att_scr[1 - slot], wa_ref[...]) + _dot(mh_scr[1 - slot], wm_ref[...])
    x1 = x_ref[...] + g1_ref[0:1, :] * y
    x1_ref[...] = x1
    r = lax.rsqrt(jnp.mean(x1 * x1, axis=-1, keepdims=True) + EPS)
    h2_ref[...] = ((x1 * r * nw_ref[...]) * (1.0 + sc_ref[0:1, :]) + sh_ref[0:1, :]).astype(BF16)

    rows = ATT_GROUP * w
    qi = lax.broadcasted_iota(jnp.int32, (rows, 2 * w), 0) % w
    kj = lax.broadcasted_iota(jnp.int32, (rows, 2 * w), 1)
    first_off = jnp.where(s > 0, 0, 4 * w)
    causal = (kj >= w) & (kj - w <= qi)
    for blk in range(MIX_TM // w):
        prev = kvp_ref[...] if blk == 0 else kvc_ref[w * (blk - 1):w * blk, :]
        kv2 = jnp.concatenate([prev, kvc_ref[w * blk:w * (blk + 1), :]], axis=0)
        allowed = ((kj < w) & (kj > qi + (first_off if blk == 0 else 0))) | causal

        def store(c0, val, blk=blk):
            att_scr[slot, w * blk:w * (blk + 1), c0:c0 + LANES] = val

        _attn_block(sink_ref, q_ref, w * blk, kv2, allowed, store)

    for ch in range(MIX_TM // MLSTM_CHUNK_P):
        r0 = MLSTM_CHUNK_P * ch

        def store_mh(cs, val, r0=r0):
            mh_scr[slot, r0:r0 + MLSTM_CHUNK_P, cs] = val

        _mlstm_chunk(mq_ref, mk_ref, mv_ref, mo_ref, g_ref, gt_ref, brow_ref, bcol_ref, mnw_ref,
                     c_scr, n_scr, m_scr, r0, store_mh)

    @pl.when(s == last_tile)
    def _():
        c_out[...] = c_scr[...]
        n_out[...] = n_scr[...]
        m_out[...] = m_scr[...]


def _mix_p(sinks, q, kv, m_all, gates, gates_t, brow, bcol, mnw, x, mod, mod_row_block, norm_w, w_att, w_m):
    tm = MIX_TM
    tiles = SEQ // tm
    bpt = tm // WINDOW
    dh = MLSTM_HEAD_DIM
    att_tile = lambda s: jnp.minimum(s, tiles - 1)
    out_tile = lambda s: jnp.maximum(s - 1, 0)
    m_spec = lambda col: pl.BlockSpec((tm, MLSTM_WIDTH), lambda s: (att_tile(s), col))
    return pl.pallas_call(
        _mix_p_kernel,
        grid=(tiles + 1,),
        in_specs=[
            pl.BlockSpec(memory_space=pltpu.SMEM),
            pl.BlockSpec((tm, ATT_WIDTH), lambda s: (att_tile(s), 0)),
            pl.BlockSpec((WINDOW, 2 * KV_WIDTH), lambda s: (jnp.maximum(bpt * att_tile(s) - 1, 0), 0)),
            pl.BlockSpec((tm, 2 * KV_WIDTH), lambda s: (att_tile(s), 0)),
            m_spec(0), m_spec(1), m_spec(2), m_spec(3),
            pl.BlockSpec((tm, LANES), lambda s: (att_tile(s), 0)),
            pl.BlockSpec((16, tm), lambda s: (0, att_tile(s))),
            pl.BlockSpec((1, LANES), lambda s: (0, 0)),
            pl.BlockSpec((16, MLSTM_CHUNK_P), lambda s: (0, 0)),
            pl.BlockSpec((1, MLSTM_WIDTH), lambda s: (0, 0)),
            pl.BlockSpec((tm, D_MODEL), lambda s: (out_tile(s), 0)),
            pl.BlockSpec((8, D_MODEL), lambda s: (mod_row_block, 2)),
            pl.BlockSpec((8, D_MODEL), lambda s: (mod_row_block, 3)),
            pl.BlockSpec((8, D_MODEL), lambda s: (mod_row_block, 4)),
            pl.BlockSpec((1, D_MODEL), lambda s: (0, 0)),
            pl.BlockSpec((ATT_WIDTH, D_MODEL), lambda s: (0, 0)),
            pl.BlockSpec((MLSTM_WIDTH, D_MODEL), lambda s: (0, 0)),
        ],
        out_specs=[
            pl.BlockSpec((tm, D_MODEL), lambda s: (out_tile(s), 0)),
            pl.BlockSpec((tm, D_MODEL), lambda s: (out_tile(s), 0)),
            pl.BlockSpec((MLSTM_HEADS, dh, dh), lambda s: (0, 0, 0)),
            pl.BlockSpec((8, dh), lambda s: (0, 0)),
            pl.BlockSpec((8, LANES), lambda s: (0, 0)),
        ],
        out_shape=[
            jax.ShapeDtypeStruct((SEQ, D_MODEL), F32),
            jax.ShapeDtypeStruct((SEQ, D_MODEL), BF16),
            jax.ShapeDtypeStruct((MLSTM_HEADS, dh, dh), F32),
            jax.ShapeDtypeStruct((8, dh), F32),
            jax.ShapeDtypeStruct((8, LANES), F32),
        ],
        scratch_shapes=[
            pltpu.VMEM((2, tm, ATT_WIDTH), BF16),
            pltpu.VMEM((2, tm, MLSTM_WIDTH), BF16),
            pltpu.VMEM((MLSTM_HEADS, dh, dh), F32),
            pltpu.VMEM((8, dh), F32),
            pltpu.VMEM((8, LANES), F32),
        ],
        compiler_params=_cparams(("arbitrary",)),
        name="mix_p",
    )(sinks, q, kv, kv, m_all, m_all, m_all, m_all, gates, gates_t, brow, bcol, mnw, x, mod, mod, mod, norm_w,
      w_att, w_m)


ATT_S_BB = 8


def _attn_s_kernel(sink_ref, q_ref, kv32_ref, ck_ref, cv_ref, o_ref, ko_ref, vo_ref, q32_scr):
    t_new = DEC_SEQ
    w = WINDOW
    q32_scr[...] = q_ref[...].astype(F32)
    rows = 4 * 4 * 8
    row = lax.broadcasted_iota(jnp.int32, (rows, w), 0)
    slot = lax.broadcasted_iota(jnp.int32, (rows, w), 1)
    t_row = row % t_new
    second = (row % 8) >= t_new
    win_ok = (slot < w - t_new) | (slot - (w - t_new) <= t_row)
    old_ok = (slot >= 1) & (slot < t_new) & (slot > t_row)
    lane256 = lax.broadcasted_iota(jnp.int32, (32, 2 * LANES), 1)
    sink = sink_ref[...][:, 0:1]
    kv_new = jnp.concatenate([kv32_ref[...], jnp.zeros((w - ATT_S_BB * t_new, 2 * KV_WIDTH), F32)], axis=0)
    kv_t = kv_new.T
    new_slot = lax.broadcasted_iota(jnp.int32, (KV_WIDTH, w), 1) >= w - t_new
    for b in range(ATT_S_BB):
        cols = pltpu.roll(kv_t, w - t_new - t_new * b, 1)
        k_shift = pltpu.roll(ck_ref[b].reshape(KV_WIDTH, w), w - t_new, 1)
        v_shift = pltpu.roll(cv_ref[b].reshape(KV_WIDTH, w), w - t_new, 1)
        ko_ref[b] = jnp.where(new_slot, cols[:KV_WIDTH], k_shift).reshape(ATT_KV_HEADS, ATT_HEAD_DIM, w)
        vo_ref[b] = jnp.where(new_slot, cols[KV_WIDTH:], v_shift).reshape(ATT_KV_HEADS, ATT_HEAD_DIM, w)
    for pair in range(ATT_S_BB // 2):
        b0, b1 = 2 * pair, 2 * pair + 1
        q32 = jnp.concatenate([q32_scr[8 * pair:8 * (pair + 1), 256 * r:256 * (r + 1)] for r in range(ATT_GROUP)],
                              axis=0)
        qpad = jnp.concatenate(
            [jnp.where((lane256 // ATT_HEAD_DIM) == g, q32, 0.0) for g in range(ATT_KV_HEADS)], axis=0).astype(BF16)
        kw = [ko_ref[b].reshape(KV_WIDTH, w).astype(BF16) for b in (b0, b1)]
        vw = [vo_ref[b].reshape(KV_WIDTH, w).astype(BF16) for b in (b0, b1)]
        kc = [ck_ref[b].reshape(KV_WIDTH, w).astype(BF16) for b in (b0, b1)]
        vc = [cv_ref[b].reshape(KV_WIDTH, w).astype(BF16) for b in (b0, b1)]
        s_w = jnp.where(second, _dot(qpad, kw[1]), _dot(qpad, kw[0]))
        s_c = jnp.where(second, _dot(qpad, kc[1]), _dot(qpad, kc[0]))
        s_w = jnp.where(win_ok, s_w, -jnp.inf)
        s_c = jnp.where(old_ok, s_c, -jnp.inf)
        m = jnp.maximum(jnp.maximum(jnp.max(s_w, axis=-1, keepdims=True), jnp.max(s_c, axis=-1, keepdims=True)), sink)
        e_w = jnp.exp(s_w - m)
        e_c = jnp.exp(s_c - m)
        l = jnp.sum(e_w, axis=-1, keepdims=True) + jnp.sum(e_c, axis=-1, keepdims=True) + jnp.exp(sink - m)
        p_w = e_w / l
        p_c = e_c / l
        zero = jnp.zeros_like(p_w)
        o = (_dot_nt(jnp.where(second, zero, p_w).astype(BF16), vw[0])
             + _dot_nt(jnp.where(second, p_w, zero).astype(BF16), vw[1])
             + _dot_nt(jnp.where(second, zero, p_c).astype(BF16), vc[0])
             + _dot_nt(jnp.where(second, p_c, zero).astype(BF16), vc[1]))
        o32 = jnp.zeros((32, 2 * LANES), F32)
        for g in range(ATT_KV_HEADS):
            o32 = jnp.where((lane256 // ATT_HEAD_DIM) == g, o[32 * g:32 * (g + 1), :], o32)
        for r in range(ATT_GROUP):
            o_ref[8 * pair:8 * (pair + 1), 256 * r:256 * (r + 1)] = o32[8 * r:8 * (r + 1), :]


def _attn_s(sink_col, q, kv32, ck, cv):
    bb = ATT_S_BB
    rows = bb * DEC_SEQ
    cache_block = (bb, ATT_KV_HEADS, ATT_HEAD_DIM, WINDOW)
    cache_shape = (DEC_BATCH, ATT_KV_HEADS, ATT_HEAD_DIM, WINDOW)
    return pl.pallas_call(
        _attn_s_kernel,
        grid=(DEC_BATCH // bb,),
        in_specs=[
            pl.BlockSpec((128, LANES), lambda i: (0, 0)),
            pl.BlockSpec((rows, ATT_WIDTH), lambda i: (i, 0)),
            pl.BlockSpec((rows, 2 * KV_WIDTH), lambda i: (i, 0)),
            pl.BlockSpec(cache_block, lambda i: (i, 0, 0, 0)),
            pl.BlockSpec(cache_block, lambda i: (i, 0, 0, 0)),
        ],
        out_specs=[
            pl.BlockSpec((rows, ATT_WIDTH), lambda i: (i, 0)),
            pl.BlockSpec(cache_block, lambda i: (i, 0, 0, 0)),
            pl.BlockSpec(cache_block, lambda i: (i, 0, 0, 0)),
        ],
        out_shape=[
            jax.ShapeDtypeStruct((S_ROWS, ATT_WIDTH), F32),
            jax.ShapeDtypeStruct(cache_shape, F32),
            jax.ShapeDtypeStruct(cache_shape, F32),
        ],
        scratch_shapes=[pltpu.VMEM((rows, ATT_WIDTH), F32)],
        compiler_params=_cparams(("arbitrary",)),
        name="attn_s",
    )(sink_col, q, kv32, ck, cv)


def _head_norm_gate(h, nw, mo):
    hn = h * lax.rsqrt(jnp.mean(h * h, axis=-1, keepdims=True) + EPS) * nw
    return hn * _sigmoid(mo.astype(F32))


def _mlstm_chunk(q_ref, k_ref, v_ref, mo_ref, g_ref, gt_ref, brow_ref, bcol_ref, nw_ref, c_scr, n_scr, m_scr,
                 r0, store):
    L = MLSTM_CHUNK_P
    dh = MLSTM_HEAD_DIM
    rs = slice(r0, r0 + L)
    ti = lax.broadcasted_iota(jnp.int32, (L, L), 0)
    si = lax.broadcasted_iota(jnp.int32, (L, L), 1)
    causal = si <= ti
    tri = causal.astype(F32)
    tri_t = (ti <= si).astype(F32)
    gates = g_ref[rs, :] + brow_ref[...]
    gates_t = gt_ref[:, rs] + bcol_ref[...]
    b_col = jnp.dot(tri, _log_sigmoid(gates), precision=HI, preferred_element_type=F32)
    b_row = jnp.dot(_log_sigmoid(gates_t), tri_t, precision=HI, preferred_element_type=F32)
    for hd in range(MLSTM_HEADS):
        cs = slice(dh * hd, dh * (hd + 1))
        b_c = b_col[:, MLSTM_HEADS + hd:MLSTM_HEADS + hd + 1]
        li_c = gates[:, hd:hd + 1]
        b_r = b_row[MLSTM_HEADS + hd:MLSTM_HEADS + hd + 1, :]
        li_r = gates_t[hd:hd + 1, :]
        dm = jnp.where(causal, b_c - b_r + li_r, -jnp.inf)
        m_prev = m_scr[hd:hd + 1, 0:1]
        m_inter = b_c + m_prev
        m_t = jnp.maximum(m_inter, jnp.max(dm, axis=-1, keepdims=True))
        q = q_ref[rs, cs] * (dh ** -0.5)
        k = k_ref[rs, cs]
        v = v_ref[rs, cs]
        sm = _dot_nt(q, k) * jnp.exp(dm - m_t)
        a = jnp.exp(m_inter - m_t)
        c_old = c_scr[hd]
        n_old = n_scr[hd:hd + 1, :]
        num = a * _dot_nt(q, c_old.astype(BF16)) + _dot(sm.astype(BF16), v)
        qn = jnp.sum(q.astype(F32) * n_old, axis=-1, keepdims=True)
        den = a * qn + jnp.sum(sm, axis=-1, keepdims=True)
        h = num / jnp.maximum(jnp.abs(den), jnp.exp(-m_t))
        store(cs, _head_norm_gate(h, nw_ref[:, cs], mo_ref[rs, cs]).astype(BF16))
        m_new = m_t[L - 1:L, :]
        b_last = b_c[L - 1:L, :]
        g = jnp.exp(b_last - b_c + li_c - m_new)
        decay = jnp.exp(b_last + m_prev - m_new)
        gv = (g * v.astype(F32)).astype(BF16)
        c_scr[hd] = decay * c_old + _dot_tn(gv, k)
        n_scr[hd:hd + 1, :] = decay * n_old + jnp.sum(g * k.astype(F32), axis=0, keepdims=True)
        m_scr[hd:hd + 1, :] = jnp.broadcast_to(m_new, (1, LANES))


MLSTM_S_BB = 4


def _mlstm_s_kernel(q_ref, k_ref, v_ref, mo_ref, g_ref, nrep_ref, mrep_ref, brow_ref, nw_ref, c_ref,
                    mh_ref, c_out, nrow_ref, mrow_ref):
    T = DEC_SEQ
    R = MLSTM_S_BB * T
    H = MLSTM_HEADS
    dh = MLSTM_HEAD_DIM

    def shift(x, d):
        return pltpu.roll(x, d, 0)

    lanes = lax.broadcasted_iota(jnp.int32, (R, LANES), 1)
    tmod = lax.broadcasted_iota(jnp.int32, (R, LANES), 0) % T
    tmod_w = lax.broadcasted_iota(jnp.int32, (R, MLSTM_WIDTH), 0) % T
    head_ok = lanes < H
    gates = g_ref[...] + brow_ref[...]
    li = jnp.where(head_ok, gates, 0.0)
    lf = jnp.where(head_ok, pltpu.roll(_log_sigmoid(gates), LANES - H, 1), 0.0)
    bcum = lf
    for d in range(1, T):
        bcum = bcum + jnp.where(tmod >= d, shift(lf, d), 0.0)
    m0 = mrep_ref[...]
    m_inter = bcum + m0
    dms = [li] + [jnp.where(tmod >= d, bcum - shift(bcum, d) + shift(li, d), -jnp.inf) for d in range(1, T)]
    m_t = m_inter
    for dm in dms:
        m_t = jnp.maximum(m_t, dm)
    a = jnp.exp(m_inter - m_t)
    ws = [jnp.exp(dm - m_t) for dm in dms]

    q_bf = q_ref[...] * (dh ** -0.5)
    q = q_bf.astype(F32)
    k = k_ref[...].astype(F32)
    v = v_ref[...].astype(F32)
    seg = (lax.broadcasted_iota(jnp.int32, (MLSTM_WIDTH, LANES), 0) // dh
           == lax.broadcasted_iota(jnp.int32, (MLSTM_WIDTH, LANES), 1)).astype(F32)
    ex = (lax.broadcasted_iota(jnp.int32, (LANES, MLSTM_WIDTH), 1) // dh
          == lax.broadcasted_iota(jnp.int32, (LANES, MLSTM_WIDTH), 0)).astype(F32)

    def segsum(x):
        return jnp.dot(x, seg, precision=HI, preferred_element_type=F32)

    def expand(x):
        return jnp.dot(x, ex, precision=HI, preferred_element_type=F32)

    ks = [k] + [shift(k, d) for d in range(1, T)]
    vs = [v] + [shift(v, d) for d in range(1, T)]
    sms = [segsum(q * ks[d]) * ws[d] for d in range(T)]
    den = a * segsum(q * nrep_ref[...])
    for sm in sms:
        den = den + sm
    inv = 1.0 / jnp.maximum(jnp.abs(den), jnp.exp(-m_t))

    def last(x):
        out = jnp.zeros_like(x)
        for jj in range(T):
            out = jnp.where(tmod == T - 1 - jj, x if jj == 0 else pltpu.roll(x, R - jj, 0), out)
        return out

    m_new = last(m_t)
    b_last = last(bcum)
    g = jnp.where(head_ok, jnp.exp(b_last - bcum + li - m_new), 0.0)
    decay = jnp.where(head_ok, jnp.exp(b_last + m0 - m_new), 0.0)
    a_f = expand(a * inv)
    w_f = [expand(sm * inv) for sm in sms]
    g_f = expand(g)
    d_f = expand(decay)
    gv = (g_f * v).astype(BF16)
    rowb = lax.broadcasted_iota(jnp.int32, (R, dh), 0) // T
    for hd in range(H):
        cs = slice(dh * hd, dh * (hd + 1))
        qh = q_bf[:, cs]
        kh = k_ref[:, cs]
        gvh = gv[:, cs]
        qc = jnp.zeros((R, dh), F32)
        for bb in range(MLSTM_S_BB):
            c_old = c_ref[bb, hd]
            qc = jnp.where(rowb == bb, _dot_nt(qh, c_old.astype(BF16)), qc)
            upd = _dot_tn(jnp.where(rowb == bb, gvh, jnp.zeros_like(gvh)), kh)
            c_out[bb, hd] = d_f[T * bb:T * bb + 1, cs] * c_old + upd
        h = a_f[:, cs] * qc
        for d in range(T):
            h = h + w_f[d][:, cs] * vs[d][:, cs]
        mh_ref[:, cs] = _head_norm_gate(h, nw_ref[:, cs], mo_ref[:, cs]).astype(BF16)
    gk = g_f * k
    nsum = gk
    for d in range(1, T):
        nsum = nsum + jnp.where(tmod_w >= d, shift(gk, d), 0.0)
    nrow_ref[...] = d_f * nrep_ref[...] + nsum
    mrow_ref[...] = m_t


def _mlstm_s(m_all, gates, n_rep, m_rep, brow, nw, state_c):
    bb = MLSTM_S_BB
    R = bb * DEC_SEQ
    dh = MLSTM_HEAD_DIM
    H = MLSTM_HEADS
    return pl.pallas_call(
        _mlstm_s_kernel,
        grid=(DEC_BATCH // bb,),
        in_specs=[
            pl.BlockSpec((R, MLSTM_WIDTH), lambda i: (i, 0)),
            pl.BlockSpec((R, MLSTM_WIDTH), lambda i: (i, 1)),
            pl.BlockSpec((R, MLSTM_WIDTH), lambda i: (i, 2)),
            pl.BlockSpec((R, MLSTM_WIDTH), lambda i: (i, 3)),
            pl.BlockSpec((R, LANES), lambda i: (i, 0)),
            pl.BlockSpec((R, MLSTM_WIDTH), lambda i: (i, 0)),
            pl.BlockSpec((R, LANES), lambda i: (i, 0)),
            pl.BlockSpec((1, LANES), lambda i: (0, 0)),
            pl.BlockSpec((1, MLSTM_WIDTH), lambda i: (0, 0)),
            pl.BlockSpec((bb, H, dh, dh), lambda i: (i, 0, 0, 0)),
        ],
        out_specs=[
            pl.BlockSpec((R, MLSTM_WIDTH), lambda i: (i, 0)),
            pl.BlockSpec((bb, H, dh, dh), lambda i: (i, 0, 0, 0)),
            pl.BlockSpec((R, MLSTM_WIDTH), lambda i: (i, 0)),
            pl.BlockSpec((R, LANES), lambda i: (i, 0)),
        ],
        out_shape=[
            jax.ShapeDtypeStruct((S_ROWS, MLSTM_WIDTH), BF16),
            jax.ShapeDtypeStruct((DEC_BATCH, H, dh, dh), F32),
            jax.ShapeDtypeStruct((S_ROWS, MLSTM_WIDTH), F32),
            jax.ShapeDtypeStruct((S_ROWS, LANES), F32),
        ],
        compiler_params=_cparams(("arbitrary",)),
        name="mlstm_s",
    )(m_all, m_all, m_all, m_all, gates, n_rep, m_rep, brow, nw, state_c)


def _outproj_kernel(att_ref, mh_ref, x_ref, g1_ref, sh_ref, sc_ref, nw_ref, wa_ref, wm_ref, x1_ref, h2_ref, *,
                    per_row):
    y = _dot(att_ref[...].astype(BF16), wa_ref[...]) + _dot(mh_ref[...], wm_ref[...])
    x1 = x_ref[...] + _mod_row(g1_ref, per_row) * y
    x1_ref[...] = x1
    r = lax.rsqrt(jnp.mean(x1 * x1, axis=-1, keepdims=True) + EPS)
    h2 = (x1 * r * nw_ref[...]) * (1.0 + _mod_row(sc_ref, per_row)) + _mod_row(sh_ref, per_row)
    h2_ref[...] = h2.astype(BF16)


def _outproj(att, mh, x, mod, mod_row_block, norm_w, w_att, w_m, *, tm, per_row):
    rows = x.shape[0]
    mod_rows = tm if per_row else 8
    mod_idx = (lambda i: i) if per_row else (lambda i: mod_row_block)
    kern = functools.partial(_outproj_kernel, per_row=per_row)
    return pl.pallas_call(
        kern,
        grid=(rows // tm,),
        in_specs=[
            pl.BlockSpec((tm, ATT_WIDTH), lambda i: (i, 0)),
            pl.BlockSpec((tm, MLSTM_WIDTH), lambda i: (i, 0)),
            pl.BlockSpec((tm, D_MODEL), lambda i: (i, 0)),
            pl.BlockSpec((mod_rows, D_MODEL), lambda i: (mod_idx(i), 2)),
            pl.BlockSpec((mod_rows, D_MODEL), lambda i: (mod_idx(i), 3)),
            pl.BlockSpec((mod_rows, D_MODEL), lambda i: (mod_idx(i), 4)),
            pl.BlockSpec((1, D_MODEL), lambda i: (0, 0)),
            pl.BlockSpec((ATT_WIDTH, D_MODEL), lambda i: (0, 0)),
            pl.BlockSpec((MLSTM_WIDTH, D_MODEL), lambda i: (0, 0)),
        ],
        out_specs=[
            pl.BlockSpec((tm, D_MODEL), lambda i: (i, 0)),
            pl.BlockSpec((tm, D_MODEL), lambda i: (i, 0)),
        ],
        out_shape=[
            jax.ShapeDtypeStruct((rows, D_MODEL), F32),
            jax.ShapeDtypeStruct((rows, D_MODEL), BF16),
        ],
        compiler_params=_cparams(("arbitrary",)),
        name="outproj_s" if per_row else "outproj_p",
    )(att, mh, x, mod, mod, mod, norm_w, w_att, w_m)


def _ffn_a_kernel(h_ref, wg_ref, wu_ref, a_ref, wg_scr, wu_scr):
    @pl.when(pl.program_id(1) == 0)
    def _():
        wg_scr[...] = wg_ref[...].astype(BF16)
        wu_scr[...] = wu_ref[...].astype(BF16)

    h = h_ref[...]
    g = _dot(h, wg_scr[...])
    u = _dot(h, wu_scr[...])
    a_ref[...] = (g * _sigmoid(g) * u).astype(BF16)


def _ffn_a(h2, w_gate, w_up, *, tm):
    rows = h2.shape[0]
    tn = 512
    return pl.pallas_call(
        _ffn_a_kernel,
        grid=(D_FF // tn, rows // tm),
        in_specs=[
            pl.BlockSpec((tm, D_MODEL), lambda j, i: (i, 0)),
            pl.BlockSpec((D_MODEL, tn), lambda j, i: (0, j)),
            pl.BlockSpec((D_MODEL, tn), lambda j, i: (0, j)),
        ],
        out_specs=pl.BlockSpec((tm, tn), lambda j, i: (i, j)),
        out_shape=jax.ShapeDtypeStruct((rows, D_FF), BF16),
        scratch_shapes=[pltpu.VMEM((D_MODEL, tn), BF16), pltpu.VMEM((D_MODEL, tn), BF16)],
        compiler_params=_cparams(("arbitrary", "arbitrary")),
        name="ffn_a",
    )(h2, w_gate, w_up)


def _ffn_b_kernel(a_ref, wd_ref, x1_ref, g2_ref, fw_ref, y_ref, *rest, per_row, emit_w):
    kk = pl.program_id(1)

    def weight_tile():
        wt = wd_ref[...].astype(BF16)
        if emit_w:
            rest[0][...] = wt
        return wt

    @pl.when(kk == 0)
    def _():
        y_ref[...] = _dot(a_ref[...], weight_tile())

    @pl.when(kk > 0)
    def _():
        y_ref[...] += _dot(a_ref[...], weight_tile())

    @pl.when(kk == pl.num_programs(1) - 1)
    def _():
        x2 = x1_ref[...] + _mod_row(g2_ref, per_row) * y_ref[...]
        y_ref[...] = x2 * lax.rsqrt(jnp.mean(x2 * x2, axis=-1, keepdims=True) + EPS) * fw_ref[...]


def _ffn_b(a, w_down, x1, mod, mod_row_block, final_w, *, tm, per_row, emit_w):
    rows = a.shape[0]
    tk = 512
    mod_rows = tm if per_row else 8
    mod_idx = (lambda i: i) if per_row else (lambda i: mod_row_block)
    kern = functools.partial(_ffn_b_kernel, per_row=per_row, emit_w=emit_w)
    w_spec = pl.BlockSpec((tk, D_MODEL), lambda i, k: (k, 0))
    y_spec = pl.BlockSpec((tm, D_MODEL), lambda i, k: (i, 0))
    y_shape = jax.ShapeDtypeStruct((rows, D_MODEL), F32)
    return pl.pallas_call(
        kern,
        grid=(rows // tm, D_FF // tk),
        in_specs=[
            pl.BlockSpec((tm, tk), lambda i, k: (i, k)),
            w_spec,
            pl.BlockSpec((tm, D_MODEL), lambda i, k: (i, 0)),
            pl.BlockSpec((mod_rows, D_MODEL), lambda i, k: (mod_idx(i), 5)),
            pl.BlockSpec((1, D_MODEL), lambda i, k: (0, 0)),
        ],
        out_specs=[y_spec, w_spec] if emit_w else y_spec,
        out_shape=[y_shape, jax.ShapeDtypeStruct((D_FF, D_MODEL), BF16)] if emit_w else y_shape,
        compiler_params=_cparams(("arbitrary", "arbitrary")),
        name="ffn_b_s" if per_row else "ffn_b_p",
    )(a, w_down, x1, mod, final_w)


def _rope_tables(pos):
    half = ROPE_DIM // 2
    inv = np.float32(ROPE_THETA) ** (-np.arange(0, ROPE_DIM, 2, dtype=np.float32) / np.float32(ROPE_DIM))
    d = np.arange(LANES) % ATT_HEAD_DIM
    ang = pos.astype(np.float32)[:, None] * inv[d % half][None, :].astype(np.float32)
    cos, sin = np.cos(ang), np.sin(ang)
    d = d[None, :]
    tables = (np.where(d < ROPE_DIM, cos, 1.0), np.where(d < half, -sin, 0.0),
              np.where((d >= half) & (d < ROPE_DIM), sin, 0.0))
    return tuple(jnp.asarray(t.astype(np.float32)) for t in tables)


def kernel(x_prompt, x_sample, cache_k_win, cache_v_win, state_C, state_n, state_m, c_prompt, c_sample,
           norm1_w, norm2_w, final_norm_w, w_ada, b_ada, w_in, b_ig, b_fg, attn_sinks, mh_norm_w,
           w_out, w_gate, w_up, w_down):
    assert w_in.shape[0] == 1, "single-layer trunk"
    T = DEC_SEQ
    xp = x_prompt[0]
    xs = x_sample.reshape(S_ROWS, D_MODEL)

    c_all = jnp.concatenate([jnp.repeat(c_sample, T, axis=0), c_prompt, jnp.zeros((15, D_MODEL), F32)], axis=0)
    mod = _ada(c_all, w_ada[0], b_ada)
    prompt_mod_block = S_ROWS // 8

    w_in_t = jnp.transpose(w_in[0])
    wq_t = (w_in_t[:ATT_WIDTH].reshape(ATT_KV_HEADS, ATT_GROUP, ATT_HEAD_DIM, D_MODEL)
            .transpose(1, 0, 2, 3).reshape(ATT_WIDTH, D_MODEL).astype(BF16))
    w_gates_t = w_in_t[MAIN_WIDTH:]
    wg = jnp.pad(w_gates_t.T, ((0, 0), (0, LANES - 2 * MLSTM_HEADS))).astype(BF16)
    wgt = jnp.pad(w_gates_t, ((0, 16 - 2 * MLSTM_HEADS), (0, 0))).astype(BF16)
    w_out_att = (w_out[0, :ATT_WIDTH].reshape(ATT_KV_HEADS, ATT_GROUP, ATT_HEAD_DIM, D_MODEL)
                 .transpose(1, 0, 2, 3).reshape(ATT_WIDTH, D_MODEL).astype(BF16))
    w_out_m = w_out[0, ATT_WIDTH:].astype(BF16)
    n1 = norm1_w.reshape(1, D_MODEL)
    n2 = norm2_w.reshape(1, D_MODEL)
    fw = final_norm_w.reshape(1, D_MODEL)
    nw = mh_norm_w.reshape(1, MLSTM_WIDTH)
    gate_bias = jnp.concatenate([b_ig[0], b_fg[0]])
    brow = jnp.pad(gate_bias, (0, LANES - 2 * MLSTM_HEADS)).reshape(1, LANES)
    bcol = jnp.broadcast_to(jnp.pad(gate_bias, (0, 16 - 2 * MLSTM_HEADS))[:, None], (16, MLSTM_CHUNK_P))

    rope_p = _rope_tables(np.arange(SEQ))
    rope_s = _rope_tables(np.tile(PAST_LEN + np.arange(T), DEC_BATCH))

    tm_p = 1024
    q_s, _, kv32_s, m_s, g_s, _, w_in_bf = _inproj(xs, mod, 0, n1, wq_t, w_in_t, wg, wgt, *rope_s,
                                                   tm=S_ROWS, per_row=True, emit_w=True)
    q_p, kv_p, kv32_p, m_p, g_p, gt_p = _inproj(xp, mod, prompt_mod_block, n1, wq_t, w_in_bf, wg, wgt, *rope_p,
                                                tm=tm_p, per_row=False, emit_w=False)

    sinks = attn_sinks[0]
    sink_col = jnp.broadcast_to(sinks.reshape(ATT_HEADS, 1, 1), (ATT_HEADS, 8, LANES)).reshape(128, LANES)
    ck = jnp.transpose(cache_k_win[0], (0, 2, 3, 1))
    cv = jnp.transpose(cache_v_win[0], (0, 2, 3, 1))
    att_s, kwin_s, vwin_s = _attn_s(sink_col, q_s, kv32_s, ck, cv)

    n_rep = jnp.repeat(state_n[0].reshape(DEC_BATCH, MLSTM_WIDTH), T, axis=0)
    m_rep = jnp.pad(jnp.repeat(state_m[0], T, axis=0), ((0, 0), (0, LANES - MLSTM_HEADS)))
    mh_s, c_s, nrow_s, mrow_s = _mlstm_s(m_s, g_s, n_rep, m_rep, brow, nw, state_C[0])

    x1_p, h2_p, c_p, n_p, mm_p = _mix_p(sinks, q_p, kv_p, m_p, g_p, gt_p, brow, bcol, nw, xp, mod, prompt_mod_block,
                                        n2, w_out_att, w_out_m)
    x1_s, h2_s = _outproj(att_s, mh_s, xs, mod, 0, n2, w_out_att, w_out_m, tm=S_ROWS, per_row=True)
    a_p = _ffn_a(h2_p, w_gate[0], w_up[0], tm=tm_p)
    a_s = _ffn_a(h2_s, w_gate[0], w_up[0], tm=S_ROWS)
    y_s, w_down_bf = _ffn_b(a_s, w_down[0], x1_s, mod, 0, fw, tm=S_ROWS, per_row=True, emit_w=True)
    y_p = _ffn_b(a_p, w_down_bf, x1_p, mod, prompt_mod_block, fw, tm=tm_p, per_row=False, emit_w=False)

    kv_shape = (1, 1, WINDOW, ATT_KV_HEADS, ATT_HEAD_DIM)
    kv_last = kv32_p[tm_p - WINDOW:]
    dh = MLSTM_HEAD_DIM
    return (
        y_p.reshape(1, SEQ, D_MODEL),
        y_s.reshape(DEC_BATCH, T, D_MODEL),
        kv_last[:, :KV_WIDTH].reshape(kv_shape),
        kv_last[:, KV_WIDTH:].reshape(kv_shape),
        c_p.reshape(1, 1, MLSTM_HEADS, dh, dh),
        n_p[:MLSTM_HEADS].reshape(1, 1, MLSTM_HEADS, dh),
        mm_p[:MLSTM_HEADS, 0].reshape(1, 1, MLSTM_HEADS),
        jnp.transpose(kwin_s, (0, 3, 1, 2))[None],
        jnp.transpose(vwin_s, (0, 3, 1, 2))[None],
        c_s.reshape(1, DEC_BATCH, MLSTM_HEADS, dh, dh),
        nrow_s[T - 1::T].reshape(1, DEC_BATCH, MLSTM_HEADS, dh),
        mrow_s[T - 1::T, :MLSTM_HEADS].reshape(1, DEC_BATCH, MLSTM_HEADS),
    )
```

```python
import functools

import jax
import jax.numpy as jnp
import numpy as np
from jax import lax
from jax.experimental import pallas as pl
from jax.experimental.pallas import tpu as pltpu

F32 = jnp.float32
BF16 = jnp.bfloat16

D_MODEL = 2048
SEQ = 8192
DEC_BATCH = 128
DEC_SEQ = 4
S_ROWS = DEC_BATCH * DEC_SEQ
PAST_LEN = 16384
ATT_HEADS = 16
ATT_KV_HEADS = 4
ATT_GROUP = 4
ATT_HEAD_DIM = 64
WINDOW = 128
ROPE_THETA = 500000.0
ROPE_DIM = 16
MLSTM_HEADS = 4
MLSTM_HEAD_DIM = 256
ATT_WIDTH = 1024
KV_WIDTH = 256
MLSTM_WIDTH = 1024
MAIN_WIDTH = ATT_WIDTH + 2 * KV_WIDTH + 4 * MLSTM_WIDTH
D_FF = 5632
N_MOD = 6
EPS = 1e-6

LANES = 128
MLSTM_CHUNK_P = 256
VMEM_LIMIT = 56 * 1024 * 1024

NT_DIMS = (((1,), (1,)), ((), ()))
TN_DIMS = (((0,), (0,)), ((), ()))
HI = lax.Precision.HIGHEST


def _cparams(sem):
    return pltpu.CompilerParams(dimension_semantics=sem, vmem_limit_bytes=VMEM_LIMIT)


def _dot(a, b):
    return jnp.dot(a, b, preferred_element_type=F32)


def _dot_nt(a, b):
    return lax.dot_general(a, b, NT_DIMS, preferred_element_type=F32)


def _dot_tn(a, b):
    return lax.dot_general(a, b, TN_DIMS, preferred_element_type=F32)


def _sigmoid(x):
    return 1.0 / (1.0 + jnp.exp(-x))


def _log_sigmoid(x):
    return jnp.minimum(x, 0.0) - jnp.log(1.0 + jnp.exp(-jnp.abs(x)))


def _mod_row(ref, per_row):
    return ref[...] if per_row else ref[0:1, :]


def _ada_kernel(c_ref, w_ref, b_ref, o_ref, s_scr):
    @pl.when(pl.program_id(0) == 0)
    def _():
        c = c_ref[...]
        s_scr[...] = (c * _sigmoid(c)).astype(BF16)

    o_ref[...] = _dot(s_scr[...], w_ref[...].astype(BF16)) + b_ref[...]


def _ada(c_all, w_ada, b_ada):
    m = c_all.shape[0]
    n = w_ada.shape[1]
    tn = 1024
    return pl.pallas_call(
        _ada_kernel,
        grid=(n // tn,),
        in_specs=[
            pl.BlockSpec((m, D_MODEL), lambda j: (0, 0)),
            pl.BlockSpec((D_MODEL, tn), lambda j: (0, j)),
            pl.BlockSpec((1, tn), lambda j: (0, j)),
        ],
        out_specs=pl.BlockSpec((m, tn), lambda j: (0, j)),
        out_shape=jax.ShapeDtypeStruct((m, n), F32),
        scratch_shapes=[pltpu.VMEM((m, D_MODEL), BF16)],
        compiler_params=_cparams(("arbitrary",)),
        name="ada",
    )(c_all, w_ada, b_ada)


def _rope_store(acc, cos, sa, sb, out_ref, ncols, scale):
    for c in range(ncols // LANES):
        xc = acc[:, LANES * c:LANES * (c + 1)]
        rot = xc * cos + pltpu.roll(xc, LANES - 8, 1) * sa + pltpu.roll(xc, 8, 1) * sb
        if scale != 1.0:
            rot = rot * scale
        out_ref[:, LANES * c:LANES * (c + 1)] = rot.astype(out_ref.dtype)


def _inproj_kernel(x_ref, sh_ref, sc_ref, nw_ref, wq_ref, win_ref, wg_ref, wgt_ref, cos_ref, sa_ref, sb_ref,
                   q_ref, kv_ref, kv32_ref, m_ref, g_ref, gt_ref, *rest, per_row, emit_w):
    h_scr = rest[-1]
    j = pl.program_id(1)

    def weight_tile():
        wt = win_ref[...].astype(BF16)
        if emit_w:
            rest[0][...] = wt
        return wt

    @pl.when(j == 0)
    def _():
        x = x_ref[...]
        r = lax.rsqrt(jnp.mean(x * x, axis=-1, keepdims=True) + EPS)
        y = x * r * nw_ref[...]
        h = (y * (1.0 + _mod_row(sc_ref, per_row)) + _mod_row(sh_ref, per_row)).astype(BF16)
        h_scr[...] = h
        g_ref[...] = _dot(h, wg_ref[...])
        gt_ref[...] = _dot_nt(wgt_ref[...], h)

    @pl.when(j < 2)
    def _():
        acc = _dot_nt(h_scr[...], wq_ref[...])
        _rope_store(acc, cos_ref[...], sa_ref[...], sb_ref[...], q_ref, 512, ATT_HEAD_DIM ** -0.5)

    @pl.when(j == 2)
    def _():
        acc = _dot_nt(h_scr[...], weight_tile())
        _rope_store(acc, cos_ref[...], sa_ref[...], sb_ref[...], kv32_ref, KV_WIDTH, 1.0)
        kv32_ref[:, KV_WIDTH:] = acc[:, KV_WIDTH:]
        kv_ref[...] = kv32_ref[...].astype(BF16)

    @pl.when(j > 2)
    def _():
        m_ref[...] = _dot_nt(h_scr[...], weight_tile()).astype(BF16)


def _inproj(x, mod, mod_row_block, norm_w, wq_t, w_in_t, wg, wgt, cos, sa, sb, *, tm, per_row, emit_w):
    rows = x.shape[0]
    tn = 512
    nj = MAIN_WIDTH // tn
    mod_rows = tm if per_row else 8
    mod_idx = (lambda i: i) if per_row else (lambda i: mod_row_block)
    kern = functools.partial(_inproj_kernel, per_row=per_row, emit_w=emit_w)
    copy_spec = pl.BlockSpec((tn, D_MODEL), lambda i, j: (jnp.maximum(j, 2) - 2, 0))
    w_spec = pl.BlockSpec((tn, D_MODEL), lambda i, j: (jnp.maximum(j, 2), 0)) if emit_w else copy_spec
    extra_specs = [copy_spec] if emit_w else []
    extra_shapes = [jax.ShapeDtypeStruct((MAIN_WIDTH - ATT_WIDTH, D_MODEL), BF16)] if emit_w else []
    return pl.pallas_call(
        kern,
        grid=(rows // tm, nj),
        in_specs=[
            pl.BlockSpec((tm, D_MODEL), lambda i, j: (i, 0)),
            pl.BlockSpec((mod_rows, D_MODEL), lambda i, j: (mod_idx(i), 0)),
            pl.BlockSpec((mod_rows, D_MODEL), lambda i, j: (mod_idx(i), 1)),
            pl.BlockSpec((1, D_MODEL), lambda i, j: (0, 0)),
            pl.BlockSpec((tn, D_MODEL), lambda i, j: (jnp.minimum(j, 1), 0)),
            w_spec,
            pl.BlockSpec((D_MODEL, LANES), lambda i, j: (0, 0)),
            pl.BlockSpec((16, D_MODEL), lambda i, j: (0, 0)),
            pl.BlockSpec((tm, LANES), lambda i, j: (i, 0)),
            pl.BlockSpec((tm, LANES), lambda i, j: (i, 0)),
            pl.BlockSpec((tm, LANES), lambda i, j: (i, 0)),
        ],
        out_specs=[
            pl.BlockSpec((tm, tn), lambda i, j: (i, jnp.minimum(j, 1))),
            pl.BlockSpec((tm, tn), lambda i, j: (i, 0)),
            pl.BlockSpec((tm, tn), lambda i, j: (0, 0)),
            pl.BlockSpec((tm, tn), lambda i, j: (i, jnp.clip(j - 3, 0, 7))),
            pl.BlockSpec((tm, LANES), lambda i, j: (i, 0)),
            pl.BlockSpec((16, tm), lambda i, j: (0, i)),
        ] + extra_specs,
        out_shape=[
            jax.ShapeDtypeStruct((rows, ATT_WIDTH), BF16),
            jax.ShapeDtypeStruct((rows, 2 * KV_WIDTH), BF16),
            jax.ShapeDtypeStruct((tm, 2 * KV_WIDTH), F32),
            jax.ShapeDtypeStruct((rows, 4 * MLSTM_WIDTH), BF16),
            jax.ShapeDtypeStruct((rows, LANES), F32),
            jax.ShapeDtypeStruct((16, rows), F32),
        ] + extra_shapes,
        scratch_shapes=[pltpu.VMEM((tm, D_MODEL), BF16)],
        compiler_params=_cparams(("arbitrary", "arbitrary")),
        name="inproj_s" if per_row else "inproj_p",
    )(x, mod, mod, norm_w, wq_t, w_in_t, wg, wgt, cos, sa, sb)


def _attn_block(sink_ref, q_ref, row0, kv2, allowed, store):
    w = WINDOW
    grp = ATT_GROUP
    rows = grp * w
    member = lax.broadcasted_iota(jnp.int32, (rows, 1), 0) // w
    low = lax.broadcasted_iota(jnp.int32, (2 * w, LANES), 1) < ATT_HEAD_DIM
    low_o = lax.broadcasted_iota(jnp.int32, (rows, LANES), 1) < ATT_HEAD_DIM
    key_row = lax.broadcasted_iota(jnp.int32, (4 * w, LANES), 0)
    key_lane = lax.broadcasted_iota(jnp.int32, (4 * w, LANES), 1)
    ones_bd = (((key_row < 2 * w) & (key_lane < ATT_HEAD_DIM))
               | ((key_row >= 2 * w) & (key_lane >= ATT_HEAD_DIM))).astype(BF16)
    zero = jnp.zeros((2 * w, LANES), BF16)
    for cp in range(2):
        k128 = kv2[:, LANES * cp:LANES * (cp + 1)]
        v128 = kv2[:, KV_WIDTH + LANES * cp:KV_WIDTH + LANES * (cp + 1)]
        kbd = jnp.concatenate([jnp.where(low, k128, zero), jnp.where(low, zero, k128)], axis=0)
        vbd = jnp.concatenate([jnp.where(low, v128, zero), jnp.where(low, zero, v128)], axis=0)
        v_aug = jnp.concatenate([vbd, ones_bd], axis=1)
        q4 = jnp.concatenate([q_ref[row0:row0 + w, 256 * r + LANES * cp:256 * r + LANES * (cp + 1)]
                              for r in range(grp)], axis=0)
        s = _dot_nt(q4, kbd)
        es, tails = [], []
        for half in range(2):
            sh = jnp.where(allowed, s[:, 2 * w * half:2 * w * (half + 1)], -jnp.inf)
            head0 = (2 * cp + half) * grp
            sink = jnp.full((rows, 1), sink_ref[head0], F32)
            for r in range(1, grp):
                sink = jnp.where(member == r, sink_ref[head0 + r], sink)
            m = jnp.maximum(jnp.max(sh, axis=-1, keepdims=True), sink)
            es.append(jnp.exp(sh - m).astype(BF16))
            tails.append(jnp.exp(sink - m))
        oa = _dot(jnp.concatenate(es, axis=1), v_aug)
        l = oa[:, LANES:] + jnp.where(low_o, tails[0], tails[1])
        o = (oa[:, :LANES] / l).astype(BF16)
        for r in range(grp):
            store(256 * r + LANES * cp, o[w * r:w * (r + 1)])


MIX_TM = 512


def _mix_p_kernel(sink_ref, q_ref, kvp_ref, kvc_ref, mq_ref, mk_ref, mv_ref, mo_ref, g_ref, gt_ref, brow_ref, bcol_ref,
                  mnw_ref, x_ref, g1_ref, sh_ref, sc_ref, nw_ref, wa_ref, wm_ref,
                  x1_ref, h2_ref, c_out, n_out, m_out, att_scr, mh_scr, c_scr, n_scr, m_scr):
    s = pl.program_id(0)
    w = WINDOW
    last_tile = pl.num_programs(0) - 2

    @pl.when(s == 0)
    def _():
        att_scr[...] = jnp.zeros_like(att_scr)
        mh_scr[...] = jnp.zeros_like(mh_scr)
        c_scr[...] = jnp.zeros_like(c_scr)
        n_scr[...] = jnp.zeros_like(n_scr)
        m_scr[...] = jnp.zeros_like(m_scr)

    slot = s % 2
    y = _dot(att_scr[1 - slot], wa_ref[...]) + _dot(mh_scr[1 - slot], wm_ref[...])
    x1 = x_ref[...] + g1_ref[0:1, :] * y
    x1_ref[...] = x1
    r = lax.rsqrt(jnp.mean(x1 * x1, axis=-1, keepdims=True) + EPS)
    h2_ref[...] = ((x1 * r * nw_ref[...]) * (1.0 + sc_ref[0:1, :]) + sh_ref[0:1, :]).astype(BF16)

    rows = ATT_GROUP * w
    qi = lax.broadcasted_iota(jnp.int32, (rows, 2 * w), 0) % w
    kj = lax.broadcasted_iota(jnp.int32, (rows, 2 * w), 1)
    first_off = jnp.where(s > 0, 0, 4 * w)
    causal = (kj >= w) & (kj - w <= qi)
    for blk in range(MIX_TM // w):
        prev = kvp_ref[...] if blk == 0 else kvc_ref[w * (blk - 1):w * blk, :]
        kv2 = jnp.concatenate([prev, kvc_ref[w * blk:w * (blk + 1), :]], axis=0)
        allowed = ((kj < w) & (kj > qi + (first_off if blk == 0 else 0))) | causal

        def store(c0, val, blk=blk):
            att_scr[slot, w * blk:w * (blk + 1), c0:c0 + LANES] = val

        _attn_block(sink_ref, q_ref, w * blk, kv2, allowed, store)

    for ch in range(MIX_TM // MLSTM_CHUNK_P):
        r0 = MLSTM_CHUNK_P * ch

        def store_mh(cs, val, r0=r0):
            mh_scr[slot, r0:r0 + MLSTM_CHUNK_P, cs] = val

        _mlstm_chunk(mq_ref, mk_ref, mv_ref, mo_ref, g_ref, gt_ref, brow_ref, bcol_ref, mnw_ref,
                     c_scr, n_scr, m_scr, r0, store_mh)

    @pl.when(s == last_tile)
    def _():
        c_out[...] = c_scr[...]
        n_out[...] = n_scr[...]
        m_out[...] = m_scr[...]


def _mix_p(sinks, q, kv, m_all, gates, gates_t, brow, bcol, mnw, x, mod, mod_row_block, norm_w, w_att, w_m):
    tm = MIX_TM
    tiles = SEQ // tm
    bpt = tm // WINDOW
    dh = MLSTM_HEAD_DIM
    att_tile = lambda s: jnp.minimum(s, tiles - 1)
    out_tile = lambda s: jnp.maximum(s - 1, 0)
    m_spec = lambda col: pl.BlockSpec((tm, MLSTM_WIDTH), lambda s: (att_tile(s), col))
    return pl.pallas_call(
        _mix_p_kernel,
        grid=(tiles + 1,),
        in_specs=[
            pl.BlockSpec(memory_space=pltpu.SMEM),
            pl.BlockSpec((tm, ATT_WIDTH), lambda s: (att_tile(s), 0)),
            pl.BlockSpec((WINDOW, 2 * KV_WIDTH), lambda s: (jnp.maximum(bpt * att_tile(s) - 1, 0), 0)),
            pl.BlockSpec((tm, 2 * KV_WIDTH), lambda s: (att_tile(s), 0)),
            m_spec(0), m_spec(1), m_spec(2), m_spec(3),
            pl.BlockSpec((tm, LANES), lambda s: (att_tile(s), 0)),
            pl.BlockSpec((16, tm), lambda s: (0, att_tile(s))),
            pl.BlockSpec((1, LANES), lambda s: (0, 0)),
            pl.BlockSpec((16, MLSTM_CHUNK_P), lambda s: (0, 0)),
            pl.BlockSpec((1, MLSTM_WIDTH), lambda s: (0, 0)),
            pl.BlockSpec((tm, D_MODEL), lambda s: (out_tile(s), 0)),
            pl.BlockSpec((8, D_MODEL), lambda s: (mod_row_block, 2)),
            pl.BlockSpec((8, D_MODEL), lambda s: (mod_row_block, 3)),
            pl.BlockSpec((8, D_MODEL), lambda s: (mod_row_block, 4)),
            pl.BlockSpec((1, D_MODEL), lambda s: (0, 0)),
            pl.BlockSpec((ATT_WIDTH, D_MODEL), lambda s: (0, 0)),
            pl.BlockSpec((MLSTM_WIDTH, D_MODEL), lambda s: (0, 0)),
        ],
        out_specs=[
            pl.BlockSpec((tm, D_MODEL), lambda s: (out_tile(s), 0)),
            pl.BlockSpec((tm, D_MODEL), lambda s: (out_tile(s), 0)),
            pl.BlockSpec((MLSTM_HEADS, dh, dh), lambda s: (0, 0, 0)),
            pl.BlockSpec((8, dh), lambda s: (0, 0)),
            pl.BlockSpec((8, LANES), lambda s: (0, 0)),
        ],
        out_shape=[
            jax.ShapeDtypeStruct((SEQ, D_MODEL), F32),
            jax.ShapeDtypeStruct((SEQ, D_MODEL), BF16),
            jax.ShapeDtypeStruct((MLSTM_HEADS, dh, dh), F32),
            jax.ShapeDtypeStruct((8, dh), F32),
            jax.ShapeDtypeStruct((8, LANES), F32),
        ],
        scratch_shapes=[
            pltpu.VMEM((2, tm, ATT_WIDTH), BF16),
            pltpu.VMEM((2, tm, MLSTM_WIDTH), BF16),
            pltpu.VMEM((MLSTM_HEADS, dh, dh), F32),
            pltpu.VMEM((8, dh), F32),
            pltpu.VMEM((8, LANES), F32),
        ],
        compiler_params=_cparams(("arbitrary",)),
        name="mix_p",
    )(sinks, q, kv, kv, m_all, m_all, m_all, m_all, gates, gates_t, brow, bcol, mnw, x, mod, mod, mod, norm_w,
      w_att, w_m)


ATT_S_BB = 8


def _attn_s_kernel(sink_ref, q_ref, kv32_ref, ck_ref, cv_ref, o_ref, ko_ref, vo_ref, q32_scr):
    t_new = DEC_SEQ
    w = WINDOW
    q32_scr[...] = q_ref[...].astype(F32)
    rows = 4 * 4 * 8
    row = lax.broadcasted_iota(jnp.int32, (rows, w), 0)
    slot = lax.broadcasted_iota(jnp.int32, (rows, w), 1)
    t_row = row % t_new
    second = (row % 8) >= t_new
    win_ok = (slot < w - t_new) | (slot - (w - t_new) <= t_row)
    old_ok = (slot >= 1) & (slot < t_new) & (slot > t_row)
    lane256 = lax.broadcasted_iota(jnp.int32, (32, 2 * LANES), 1)
    sink = sink_ref[...][:, 0:1]
    kv_new = jnp.concatenate([kv32_ref[...], jnp.zeros((w - ATT_S_BB * t_new, 2 * KV_WIDTH), F32)], axis=0)
    kv_t = kv_new.T
    new_slot = lax.broadcasted_iota(jnp.int32, (KV_WIDTH, w), 1) >= w - t_new
    for b in range(ATT_S_BB):
        cols = pltpu.roll(kv_t, w - t_new - t_new * b, 1)
        k_shift = pltpu.roll(ck_ref[b].reshape(KV_WIDTH, w), w - t_new, 1)
        v_shift = pltpu.roll(cv_ref[b].reshape(KV_WIDTH, w), w - t_new, 1)
        ko_ref[b] = jnp.where(new_slot, cols[:KV_WIDTH], k_shift).reshape(ATT_KV_HEADS, ATT_HEAD_DIM, w)
        vo_ref[b] = jnp.where(new_slot, cols[KV_WIDTH:], v_shift).reshape(ATT_KV_HEADS, ATT_HEAD_DIM, w)
    for pair in range(ATT_S_BB // 2):
        b0, b1 = 2 * pair, 2 * pair + 1
        q32 = jnp.concatenate([q32_scr[8 * pair:8 * (pair + 1), 256 * r:256 * (r + 1)] for r in range(ATT_GROUP)],
                              axis=0)
        qpad = jnp.concatenate(
            [jnp.where((lane256 // ATT_HEAD_DIM) == g, q32, 0.0) for g in range(ATT_KV_HEADS)], axis=0).astype(BF16)
        kw = [ko_ref[b].reshape(KV_WIDTH, w).astype(BF16) for b in (b0, b1)]
        vw = [vo_ref[b].reshape(KV_WIDTH, w).astype(BF16) for b in (b0, b1)]
        kc = [ck_ref[b].reshape(KV_WIDTH, w).astype(BF16) for b in (b0, b1)]
        vc = [cv_ref[b].reshape(KV_WIDTH, w).astype(BF16) for b in (b0, b1)]
        s_w = jnp.where(second, _dot(qpad, kw[1]), _dot(qpad, kw[0]))
        s_c = jnp.where(second, _dot(qpad, kc[1]), _dot(qpad, kc[0]))
        s_w = jnp.where(win_ok, s_w, -jnp.inf)
        s_c = jnp.where(old_ok, s_c, -jnp.inf)
        m = jnp.maximum(jnp.maximum(jnp.max(s_w, axis=-1, keepdims=True), jnp.max(s_c, axis=-1, keepdims=True)), sink)
        e_w = jnp.exp(s_w - m)
        e_c = jnp.exp(s_c - m)
        l = jnp.sum(e_w, axis=-1, keepdims=True) + jnp.sum(e_c, axis=-1, keepdims=True) + jnp.exp(sink - m)
        p_w = e_w / l
        p_c = e_c / l
        zero = jnp.zeros_like(p_w)
        o = (_dot_nt(jnp.where(second, zero, p_w).astype(BF16), vw[0])
             + _dot_nt(jnp.where(second, p_w, zero).astype(BF16), vw[1])
             + _dot_nt(jnp.where(second, zero, p_c).astype(BF16), vc[0])
             + _dot_nt(jnp.where(second, p_c, zero).astype(BF16), vc[1]))
        o32 = jnp.zeros((32, 2 * LANES), F32)
        for g in range(ATT_KV_HEADS):
            o32 = jnp.where((lane256 // ATT_HEAD_DIM) == g, o[32 * g:32 * (g + 1), :], o32)
        for r in range(ATT_GROUP):
            o_ref[8 * pair:8 * (pair + 1), 256 * r:256 * (r + 1)] = o32[8 * r:8 * (r + 1), :]


def _attn_s(sink_col, q, kv32, ck, cv):
    bb = ATT_S_BB
    rows = bb * DEC_SEQ
    cache_block = (bb, ATT_KV_HEADS, ATT_HEAD_DIM, WINDOW)
    cache_shape = (DEC_BATCH, ATT_KV_HEADS, ATT_HEAD_DIM, WINDOW)
    return pl.pallas_call(
        _attn_s_kernel,
        grid=(DEC_BATCH // bb,),
        in_specs=[
            pl.BlockSpec((128, LANES), lambda i: (0, 0)),
            pl.BlockSpec((rows, ATT_WIDTH), lambda i: (i, 0)),
            pl.BlockSpec((rows, 2 * KV_WIDTH), lambda i: (i, 0)),
            pl.BlockSpec(cache_block, lambda i: (i, 0, 0, 0)),
            pl.BlockSpec(cache_block, lambda i: (i, 0, 0, 0)),
        ],
        out_specs=[
            pl.BlockSpec((rows, ATT_WIDTH), lambda i: (i, 0)),
            pl.BlockSpec(cache_block, lambda i: (i, 0, 0, 0)),
            pl.BlockSpec(cache_block, lambda i: (i, 0, 0, 0)),
        ],
        out_shape=[
            jax.ShapeDtypeStruct((S_ROWS, ATT_WIDTH), F32),
            jax.ShapeDtypeStruct(cache_shape, F32),
            jax.ShapeDtypeStruct(cache_shape, F32),
        ],
        scratch_shapes=[pltpu.VMEM((rows, ATT_WIDTH), F32)],
        compiler_params=_cparams(("arbitrary",)),
        name="attn_s",
    )(sink_col, q, kv32, ck, cv)


def _head_norm_gate(h, nw, mo):
    hn = h * lax.rsqrt(jnp.mean(h * h, axis=-1, keepdims=True) + EPS) * nw
    return hn * _sigmoid(mo.astype(F32))


def _mlstm_chunk(q_ref, k_ref, v_ref, mo_ref, g_ref, gt_ref, brow_ref, bcol_ref, nw_ref, c_scr, n_scr, m_scr,
                 r0, store):
    L = MLSTM_CHUNK_P
    dh = MLSTM_HEAD_DIM
    rs = slice(r0, r0 + L)
    ti = lax.broadcasted_iota(jnp.int32, (L, L), 0)
    si = lax.broadcasted_iota(jnp.int32, (L, L), 1)
    causal = si <= ti
    tri = causal.astype(F32)
    tri_t = (ti <= si).astype(F32)
    gates = g_ref[rs, :] + brow_ref[...]
    gates_t = gt_ref[:, rs] + bcol_ref[...]
    b_col = jnp.dot(tri, _log_sigmoid(gates), precision=HI, preferred_element_type=F32)
    b_row = jnp.dot(_log_sigmoid(gates_t), tri_t, precision=HI, preferred_element_type=F32)
    for hd in range(MLSTM_HEADS):
        cs = slice(dh * hd, dh * (hd + 1))
        b_c = b_col[:, MLSTM_HEADS + hd:MLSTM_HEADS + hd + 1]
        li_c = gates[:, hd:hd + 1]
        b_r = b_row[MLSTM_HEADS + hd:MLSTM_HEADS + hd + 1, :]
        li_r = gates_t[hd:hd + 1, :]
        dm = jnp.where(causal, b_c - b_r + li_r, -jnp.inf)
        m_prev = m_scr[hd:hd + 1, 0:1]
        m_inter = b_c + m_prev
        m_t = jnp.maximum(m_inter, jnp.max(dm, axis=-1, keepdims=True))
        q = q_ref[rs, cs] * (dh ** -0.5)
        k = k_ref[rs, cs]
        v = v_ref[rs, cs]
        sm = _dot_nt(q, k) * jnp.exp(dm - m_t)
        a = jnp.exp(m_inter - m_t)
        c_old = c_scr[hd]
        n_old = n_scr[hd:hd + 1, :]
        num = a * _dot_nt(q, c_old.astype(BF16)) + _dot(sm.astype(BF16), v)
        qn = jnp.sum(q.astype(F32) * n_old, axis=-1, keepdims=True)
        den = a * qn + jnp.sum(sm, axis=-1, keepdims=True)
        h = num / jnp.maximum(jnp.abs(den), jnp.exp(-m_t))
        store(cs, _head_norm_gate(h, nw_ref[:, cs], mo_ref[rs, cs]).astype(BF16))
        m_new = m_t[L - 1:L, :]
        b_last = b_c[L - 1:L, :]
        g = jnp.exp(b_last - b_c + li_c - m_new)
        decay = jnp.exp(b_last + m_prev - m_new)
        gv = (g * v.astype(F32)).astype(BF16)
        c_scr[hd] = decay * c_old + _dot_tn(gv, k)
        n_scr[hd:hd + 1, :] = decay * n_old + jnp.sum(g * k.astype(F32), axis=0, keepdims=True)
        m_scr[hd:hd + 1, :] = jnp.broadcast_to(m_new, (1, LANES))


MLSTM_S_BB = 4


def _mlstm_s_kernel(q_ref, k_ref, v_ref, mo_ref, g_ref, nrep_ref, mrep_ref, brow_ref, nw_ref, c_ref,
                    mh_ref, c_out, nrow_ref, mrow_ref):
    T = DEC_SEQ
    R = MLSTM_S_BB * T
    H = MLSTM_HEADS
    dh = MLSTM_HEAD_DIM

    def shift(x, d):
        return pltpu.roll(x, d, 0)

    lanes = lax.broadcasted_iota(jnp.int32, (R, LANES), 1)
    tmod = lax.broadcasted_iota(jnp.int32, (R, LANES), 0) % T
    tmod_w = lax.broadcasted_iota(jnp.int32, (R, MLSTM_WIDTH), 0) % T
    head_ok = lanes < H
    gates = g_ref[...] + brow_ref[...]
    li = jnp.where(head_ok, gates, 0.0)
    lf = jnp.where(head_ok, pltpu.roll(_log_sigmoid(gates), LANES - H, 1), 0.0)
    bcum = lf
    for d in range(1, T):
        bcum = bcum + jnp.where(tmod >= d, shift(lf, d), 0.0)
    m0 = mrep_ref[...]
    m_inter = bcum + m0
    dms = [li] + [jnp.where(tmod >= d, bcum - shift(bcum, d) + shift(li, d), -jnp.inf) for d in range(1, T)]
    m_t = m_inter
    for dm in dms:
        m_t = jnp.maximum(m_t, dm)
    a = jnp.exp(m_inter - m_t)
    ws = [jnp.exp(dm - m_t) for dm in dms]

    q_bf = q_ref[...] * (dh ** -0.5)
    q = q_bf.astype(F32)
    k = k_ref[...].astype(F32)
    v = v_ref[...].astype(F32)
    seg = (lax.broadcasted_iota(jnp.int32, (MLSTM_WIDTH, LANES), 0) // dh
           == lax.broadcasted_iota(jnp.int32, (MLSTM_WIDTH, LANES), 1)).astype(F32)
    ex = (lax.broadcasted_iota(jnp.int32, (LANES, MLSTM_WIDTH), 1) // dh
          == lax.broadcasted_iota(jnp.int32, (LANES, MLSTM_WIDTH), 0)).astype(F32)

    def segsum(x):
        return jnp.dot(x, seg, precision=HI, preferred_element_type=F32)

    def expand(x):
        return jnp.dot(x, ex, precision=HI, preferred_element_type=F32)

    ks = [k] + [shift(k, d) for d in range(1, T)]
    vs = [v] + [shift(v, d) for d in range(1, T)]
    sms = [segsum(q * ks[d]) * ws[d] for d in range(T)]
    den = a * segsum(q * nrep_ref[...])
    for sm in sms:
        den = den + sm
    inv = 1.0 / jnp.maximum(jnp.abs(den), jnp.exp(-m_t))

    def last(x):
        out = jnp.zeros_like(x)
        for jj in range(T):
            out = jnp.where(tmod == T - 1 - jj, x if jj == 0 else pltpu.roll(x, R - jj, 0), out)
        return out

    m_new = last(m_t)
    b_last = last(bcum)
    g = jnp.where(head_ok, jnp.exp(b_last - bcum + li - m_new), 0.0)
    decay = jnp.where(head_ok, jnp.exp(b_last + m0 - m_new), 0.0)
    a_f = expand(a * inv)
    w_f = [expand(sm * inv) for sm in sms]
    g_f = expand(g)
    d_f = expand(decay)
    gv = (g_f * v).astype(BF16)
    rowb = lax.broadcasted_iota(jnp.int32, (R, dh), 0) // T
    for hd in range(H):
        cs = slice(dh * hd, dh * (hd + 1))
        qh = q_bf[:, cs]
        kh = k_ref[:, cs]
        gvh = gv[:, cs]
        qc = jnp.zeros((R, dh), F32)
        for bb in range(MLSTM_S_BB):
            c_old = c_ref[bb, hd]
            qc = jnp.where(rowb == bb, _dot_nt(qh, c_old.astype(BF16)), qc)
            upd = _dot_tn(jnp.where(rowb == bb, gvh, jnp.zeros_like(gvh)), kh)
            c_out[bb, hd] = d_f[T * bb:T * bb + 1, cs] * c_old + upd
        h = a_f[:, cs] * qc
        for d in range(T):
            h = h + w_f[d][:, cs] * vs[d][:, cs]
        mh_ref[:, cs] = _head_norm_gate(h, nw_ref[:, cs], mo_ref[:, cs]).astype(BF16)
    gk = g_f * k
    nsum = gk
    for d in range(1, T):
        nsum = nsum + jnp.where(tmod_w >= d, shift(gk, d), 0.0)
    nrow_ref[...] = d_f * nrep_ref[...] + nsum
    mrow_ref[...] = m_t


def _mlstm_s(m_all, gates, n_rep, m_rep, brow, nw, state_c):
    bb = MLSTM_S_BB
    R = bb * DEC_SEQ
    dh = MLSTM_HEAD_DIM
    H = MLSTM_HEADS
    return pl.pallas_call(
        _mlstm_s_kernel,
        grid=(DEC_BATCH // bb,),
        in_specs=[
            pl.BlockSpec((R, MLSTM_WIDTH), lambda i: (i, 0)),
            pl.BlockSpec((R, MLSTM_WIDTH), lambda i: (i, 1)),
            pl.BlockSpec((R, MLSTM_WIDTH), lambda i: (i, 2)),
            pl.BlockSpec((R, MLSTM_WIDTH), lambda i: (i, 3)),
            pl.BlockSpec((R, LANES), lambda i: (i, 0)),
            pl.BlockSpec((R, MLSTM_WIDTH), lambda i: (i, 0)),
            pl.BlockSpec((R, LANES), lambda i: (i, 0)),
            pl.BlockSpec((1, LANES), lambda i: (0, 0)),
            pl.BlockSpec((1, MLSTM_WIDTH), lambda i: (0, 0)),
            pl.BlockSpec((bb, H, dh, dh), lambda i: (i, 0, 0, 0)),
        ],
        out_specs=[
            pl.BlockSpec((R, MLSTM_WIDTH), lambda i: (i, 0)),
            pl.BlockSpec((bb, H, dh, dh), lambda i: (i, 0, 0, 0)),
            pl.BlockSpec((R, MLSTM_WIDTH), lambda i: (i, 0)),
            pl.BlockSpec((R, LANES), lambda i: (i, 0)),
        ],
        out_shape=[
            jax.ShapeDtypeStruct((S_ROWS, MLSTM_WIDTH), BF16),
            jax.ShapeDtypeStruct((DEC_BATCH, H, dh, dh), F32),
            jax.ShapeDtypeStruct((S_ROWS, MLSTM_WIDTH), F32),
            jax.ShapeDtypeStruct((S_ROWS, LANES), F32),
        ],
        compiler_params=_cparams(("arbitrary",)),
        name="mlstm_s",
    )(m_all, m_all, m_all, m_all, gates, n_rep, m_rep, brow, nw, state_c)


def _outproj_kernel(att_ref, mh_ref, x_ref, g1_ref, sh_ref, sc_ref, nw_ref, wa_ref, wm_ref, x1_ref, h2_ref, *,
                    per_row):
    y = _dot(att_ref[...].astype(BF16), wa_ref[...]) + _dot(mh_ref[...], wm_ref[...])
    x1 = x_ref[...] + _mod_row(g1_ref, per_row) * y
    x1_ref[...] = x1
    r = lax.rsqrt(jnp.mean(x1 * x1, axis=-1, keepdims=True) + EPS)
    h2 = (x1 * r * nw_ref[...]) * (1.0 + _mod_row(sc_ref, per_row)) + _mod_row(sh_ref, per_row)
    h2_ref[...] = h2.astype(BF16)


def _outproj(att, mh, x, mod, mod_row_block, norm_w, w_att, w_m, *, tm, per_row):
    rows = x.shape[0]
    mod_rows = tm if per_row else 8
    mod_idx = (lambda i: i) if per_row else (lambda i: mod_row_block)
    kern = functools.partial(_outproj_kernel, per_row=per_row)
    return pl.pallas_call(
        kern,
        grid=(rows // tm,),
        in_specs=[
            pl.BlockSpec((tm, ATT_WIDTH), lambda i: (i, 0)),
            pl.BlockSpec((tm, MLSTM_WIDTH), lambda i: (i, 0)),
            pl.BlockSpec((tm, D_MODEL), lambda i: (i, 0)),
            pl.BlockSpec((mod_rows, D_MODEL), lambda i: (mod_idx(i), 2)),
            pl.BlockSpec((mod_rows, D_MODEL), lambda i: (mod_idx(i), 3)),
            pl.BlockSpec((mod_rows, D_MODEL), lambda i: (mod_idx(i), 4)),
            pl.BlockSpec((1, D_MODEL), lambda i: (0, 0)),
            pl.BlockSpec((ATT_WIDTH, D_MODEL), lambda i: (0, 0)),
            pl.BlockSpec((MLSTM_WIDTH, D_MODEL), lambda i: (0, 0)),
        ],
        out_specs=[
            pl.BlockSpec((tm, D_MODEL), lambda i: (i, 0)),
            pl.BlockSpec((tm, D_MODEL), lambda i: (i, 0)),
        ],
        out_shape=[
            jax.ShapeDtypeStruct((rows, D_MODEL), F32),
            jax.ShapeDtypeStruct((rows, D_MODEL), BF16),
        ],
        compiler_params=_cparams(("arbitrary",)),
        name="outproj_s" if per_row else "outproj_p",
    )(att, mh, x, mod, mod, mod, norm_w, w_att, w_m)


def _ffn_a_kernel(hp_ref, hs_ref, wg_ref, wu_ref, ap_ref, as_ref, wg_scr, wu_scr):
    i = pl.program_id(1)
    last = pl.num_programs(1) - 1

    @pl.when(i == 0)
    def _():
        wg_scr[...] = wg_ref[...].astype(BF16)
        wu_scr[...] = wu_ref[...].astype(BF16)

    def swiglu(h_ref, a_ref):
        h = h_ref[...]
        g = _dot(h, wg_scr[...])
        u = _dot(h, wu_scr[...])
        a_ref[...] = (g * _sigmoid(g) * u).astype(BF16)

    @pl.when(i < last)
    def _():
        swiglu(hp_ref, ap_ref)

    @pl.when(i == last)
    def _():
        swiglu(hs_ref, as_ref)


def _ffn_a(h2_p, h2_s, w_gate, w_up, *, tm):
    rows_p, rows_s = h2_p.shape[0], h2_s.shape[0]
    tn = 512
    ni = rows_p // tm
    tile = lambda i: jnp.minimum(i, ni - 1)
    return pl.pallas_call(
        _ffn_a_kernel,
        grid=(D_FF // tn, ni + 1),
        in_specs=[
            pl.BlockSpec((tm, D_MODEL), lambda j, i: (tile(i), 0)),
            pl.BlockSpec((rows_s, D_MODEL), lambda j, i: (0, 0)),
            pl.BlockSpec((D_MODEL, tn), lambda j, i: (0, j)),
            pl.BlockSpec((D_MODEL, tn), lambda j, i: (0, j)),
        ],
        out_specs=[
            pl.BlockSpec((tm, tn), lambda j, i: (tile(i), j)),
            pl.BlockSpec((rows_s, tn), lambda j, i: (0, j)),
        ],
        out_shape=[
            jax.ShapeDtypeStruct((rows_p, D_FF), BF16),
            jax.ShapeDtypeStruct((rows_s, D_FF), BF16),
        ],
        scratch_shapes=[pltpu.VMEM((D_MODEL, tn), BF16), pltpu.VMEM((D_MODEL, tn), BF16)],
        compiler_params=_cparams(("arbitrary", "arbitrary")),
        name="ffn_a",
    )(h2_p, h2_s, w_gate, w_up)


def _ffn_b_kernel(a_ref, wd_ref, x1_ref, g2_ref, fw_ref, y_ref, *rest, per_row, emit_w):
    kk = pl.program_id(1)

    def weight_tile():
        wt = wd_ref[...].astype(BF16)
        if emit_w:
            rest[0][...] = wt
        return wt

    @pl.when(kk == 0)
    def _():
        y_ref[...] = _dot(a_ref[...], weight_tile())

    @pl.when(kk > 0)
    def _():
        y_ref[...] += _dot(a_ref[...], weight_tile())

    @pl.when(kk == pl.num_programs(1) - 1)
    def _():
        x2 = x1_ref[...] + _mod_row(g2_ref, per_row) * y_ref[...]
        y_ref[...] = x2 * lax.rsqrt(jnp.mean(x2 * x2, axis=-1, keepdims=True) + EPS) * fw_ref[...]


def _ffn_b(a, w_down, x1, mod, mod_row_block, final_w, *, tm, per_row, emit_w):
    rows = a.shape[0]
    tk = 512
    mod_rows = tm if per_row else 8
    mod_idx = (lambda i: i) if per_row else (lambda i: mod_row_block)
    kern = functools.partial(_ffn_b_kernel, per_row=per_row, emit_w=emit_w)
    w_spec = pl.BlockSpec((tk, D_MODEL), lambda i, k: (k, 0))
    y_spec = pl.BlockSpec((tm, D_MODEL), lambda i, k: (i, 0))
    y_shape = jax.ShapeDtypeStruct((rows, D_MODEL), F32)
    return pl.pallas_call(
        kern,
        grid=(rows // tm, D_FF // tk),
        in_specs=[
            pl.BlockSpec((tm, tk), lambda i, k: (i, k)),
            w_spec,
            pl.BlockSpec((tm, D_MODEL), lambda i, k: (i, 0)),
            pl.BlockSpec((mod_rows, D_MODEL), lambda i, k: (mod_idx(i), 5)),
            pl.BlockSpec((1, D_MODEL), lambda i, k: (0, 0)),
        ],
        out_specs=[y_spec, w_spec] if emit_w else y_spec,
        out_shape=[y_shape, jax.ShapeDtypeStruct((D_FF, D_MODEL), BF16)] if emit_w else y_shape,
        compiler_params=_cparams(("arbitrary", "arbitrary")),
        name="ffn_b_s" if per_row else "ffn_b_p",
    )(a, w_down, x1, mod, final_w)


def _rope_tables(pos):
    half = ROPE_DIM // 2
    inv = np.float32(ROPE_THETA) ** (-np.arange(0, ROPE_DIM, 2, dtype=np.float32) / np.float32(ROPE_DIM))
    d = np.arange(LANES) % ATT_HEAD_DIM
    ang = pos.astype(np.float32)[:, None] * inv[d % half][None, :].astype(np.float32)
    cos, sin = np.cos(ang), np.sin(ang)
    d = d[None, :]
    tables = (np.where(d < ROPE_DIM, cos, 1.0), np.where(d < half, -sin, 0.0),
              np.where((d >= half) & (d < ROPE_DIM), sin, 0.0))
    return tuple(jnp.asarray(t.astype(np.float32)) for t in tables)


def kernel(x_prompt, x_sample, cache_k_win, cache_v_win, state_C, state_n, state_m, c_prompt, c_sample,
           norm1_w, norm2_w, final_norm_w, w_ada, b_ada, w_in, b_ig, b_fg, attn_sinks, mh_norm_w,
           w_out, w_gate, w_up, w_down):
    assert w_in.shape[0] == 1, "single-layer trunk"
    T = DEC_SEQ
    xp = x_prompt[0]
    xs = x_sample.reshape(S_ROWS, D_MODEL)

    c_all = jnp.concatenate([jnp.repeat(c_sample, T, axis=0), c_prompt, jnp.zeros((15, D_MODEL), F32)], axis=0)
    mod = _ada(c_all, w_ada[0], b_ada)
    prompt_mod_block = S_ROWS // 8

    w_in_t = jnp.transpose(w_in[0])
    wq_t = (w_in_t[:ATT_WIDTH].reshape(ATT_KV_HEADS, ATT_GROUP, ATT_HEAD_DIM, D_MODEL)
            .transpose(1, 0, 2, 3).reshape(ATT_WIDTH, D_MODEL).astype(BF16))
    w_gates_t = w_in_t[MAIN_WIDTH:]
    wg = jnp.pad(w_gates_t.T, ((0, 0), (0, LANES - 2 * MLSTM_HEADS))).astype(BF16)
    wgt = jnp.pad(w_gates_t, ((0, 16 - 2 * MLSTM_HEADS), (0, 0))).astype(BF16)
    w_out_att = (w_out[0, :ATT_WIDTH].reshape(ATT_KV_HEADS, ATT_GROUP, ATT_HEAD_DIM, D_MODEL)
                 .transpose(1, 0, 2, 3).reshape(ATT_WIDTH, D_MODEL).astype(BF16))
    w_out_m = w_out[0, ATT_WIDTH:].astype(BF16)
    n1 = norm1_w.reshape(1, D_MODEL)
    n2 = norm2_w.reshape(1, D_MODEL)
    fw = final_norm_w.reshape(1, D_MODEL)
    nw = mh_norm_w.reshape(1, MLSTM_WIDTH)
    gate_bias = jnp.concatenate([b_ig[0], b_fg[0]])
    brow = jnp.pad(gate_bias, (0, LANES - 2 * MLSTM_HEADS)).reshape(1, LANES)
    bcol = jnp.broadcast_to(jnp.pad(gate_bias, (0, 16 - 2 * MLSTM_HEADS))[:, None], (16, MLSTM_CHUNK_P))

    rope_p = _rope_tables(np.arange(SEQ))
    rope_s = _rope_tables(np.tile(PAST_LEN + np.arange(T), DEC_BATCH))

    tm_p = 1024
    q_s, _, kv32_s, m_s, g_s, _, w_in_bf = _inproj(xs, mod, 0, n1, wq_t, w_in_t, wg, wgt, *rope_s,
                                                   tm=S_ROWS, per_row=True, emit_w=True)
    q_p, kv_p, kv32_p, m_p, g_p, gt_p = _inproj(xp, mod, prompt_mod_block, n1, wq_t, w_in_bf, wg, wgt, *rope_p,
                                                tm=tm_p, per_row=False, emit_w=False)

    sinks = attn_sinks[0]
    sink_col = jnp.broadcast_to(sinks.reshape(ATT_HEADS, 1, 1), (ATT_HEADS, 8, LANES)).reshape(128, LANES)
    ck = jnp.transpose(cache_k_win[0], (0, 2, 3, 1))
    cv = jnp.transpose(cache_v_win[0], (0, 2, 3, 1))
    att_s, kwin_s, vwin_s = _attn_s(sink_col, q_s, kv32_s, ck, cv)

    n_rep = jnp.repeat(state_n[0].reshape(DEC_BATCH, MLSTM_WIDTH), T, axis=0)
    m_rep = jnp.pad(jnp.repeat(state_m[0], T, axis=0), ((0, 0), (0, LANES - MLSTM_HEADS)))
    mh_s, c_s, nrow_s, mrow_s = _mlstm_s(m_s, g_s, n_rep, m_rep, brow, nw, state_C[0])

    x1_p, h2_p, c_p, n_p, mm_p = _mix_p(sinks, q_p, kv_p, m_p, g_p, gt_p, brow, bcol, nw, xp, mod, prompt_mod_block,
                                        n2, w_out_att, w_out_m)
    x1_s, h2_s = _outproj(att_s, mh_s, xs, mod, 0, n2, w_out_att, w_out_m, tm=S_ROWS, per_row=True)
    a_p, a_s = _ffn_a(h2_p, h2_s, w_gate[0], w_up[0], tm=tm_p)
    y_s, w_down_bf = _ffn_b(a_s, w_down[0], x1_s, mod, 0, fw, tm=S_ROWS, per_row=True, emit_w=True)
    y_p = _ffn_b(a_p, w_down_bf, x1_p, mod, prompt_mod_block, fw, tm=tm_p, per_row=False, emit_w=False)

    kv_shape = (1, 1, WINDOW, ATT_KV_HEADS, ATT_HEAD_DIM)
    kv_last = kv32_p[tm_p - WINDOW:]
    dh = MLSTM_HEAD_DIM
    return (
        y_p.reshape(1, SEQ, D_MODEL),
        y_s.reshape(DEC_BATCH, T, D_MODEL),
        kv_last[:, :KV_WIDTH].reshape(kv_shape),
        kv_last[:, KV_WIDTH:].reshape(kv_shape),
        c_p.reshape(1, 1, MLSTM_HEADS, dh, dh),
        n_p[:MLSTM_HEADS].reshape(1, 1, MLSTM_HEADS, dh),
        mm_p[:MLSTM_HEADS, 0].reshape(1, 1, MLSTM_HEADS),
        jnp.transpose(kwin_s, (0, 3, 1, 2))[None],
        jnp.transpose(vwin_s, (0, 3, 1, 2))[None],
        c_s.reshape(1, DEC_BATCH, MLSTM_HEADS, dh, dh),
        nrow_s[T - 1::T].reshape(1, DEC_BATCH, MLSTM_HEADS, dh),
        mrow_s[T - 1::T, :MLSTM_HEADS].reshape(1, DEC_BATCH, MLSTM_HEADS),
    )
```

```python
import functools

import jax
import jax.numpy as jnp
import numpy as np
from jax import lax
from jax.experimental import pallas as pl
from jax.experimental.pallas import tpu as pltpu

F32 = jnp.float32
BF16 = jnp.bfloat16

D_MODEL = 2048
SEQ = 8192
DEC_BATCH = 128
DEC_SEQ = 4
S_ROWS = DEC_BATCH * DEC_SEQ
PAST_LEN = 16384
ATT_HEADS = 16
ATT_KV_HEADS = 4
ATT_GROUP = 4
ATT_HEAD_DIM = 64
WINDOW = 128
ROPE_THETA = 500000.0
ROPE_DIM = 16
MLSTM_HEADS = 4
MLSTM_HEAD_DIM = 256
ATT_WIDTH = 1024
KV_WIDTH = 256
MLSTM_WIDTH = 1024
MAIN_WIDTH = ATT_WIDTH + 2 * KV_WIDTH + 4 * MLSTM_WIDTH
D_FF = 5632
N_MOD = 6
EPS = 1e-6

LANES = 128
MLSTM_CHUNK_P = 256
VMEM_LIMIT = 56 * 1024 * 1024

NT_DIMS = (((1,), (1,)), ((), ()))
TN_DIMS = (((0,), (0,)), ((), ()))
HI = lax.Precision.HIGHEST


def _cparams(sem):
    return pltpu.CompilerParams(dimension_semantics=sem, vmem_limit_bytes=VMEM_LIMIT)


def _dot(a, b):
    return jnp.dot(a, b, preferred_element_type=F32)


def _dot_nt(a, b):
    return lax.dot_general(a, b, NT_DIMS, preferred_element_type=F32)


def _dot_tn(a, b):
    return lax.dot_general(a, b, TN_DIMS, preferred_element_type=F32)


def _sigmoid(x):
    return 1.0 / (1.0 + jnp.exp(-x))


def _log_sigmoid(x):
    return jnp.minimum(x, 0.0) - jnp.log(1.0 + jnp.exp(-jnp.abs(x)))


def _mod_row(ref, per_row):
    return ref[...] if per_row else ref[0:1, :]


def _ada_kernel(c_ref, w_ref, b_ref, o_ref, s_scr):
    @pl.when(pl.program_id(0) == 0)
    def _():
        c = c_ref[...]
        s_scr[...] = (c * _sigmoid(c)).astype(BF16)

    o_ref[...] = _dot(s_scr[...], w_ref[...].astype(BF16)) + b_ref[...]


def _ada(c_all, w_ada, b_ada):
    m = c_all.shape[0]
    n = w_ada.shape[1]
    tn = 1024
    return pl.pallas_call(
        _ada_kernel,
        grid=(n // tn,),
        in_specs=[
            pl.BlockSpec((m, D_MODEL), lambda j: (0, 0)),
            pl.BlockSpec((D_MODEL, tn), lambda j: (0, j)),
            pl.BlockSpec((1, tn), lambda j: (0, j)),
        ],
        out_specs=pl.BlockSpec((m, tn), lambda j: (0, j)),
        out_shape=jax.ShapeDtypeStruct((m, n), F32),
        scratch_shapes=[pltpu.VMEM((m, D_MODEL), BF16)],
        compiler_params=_cparams(("arbitrary",)),
        name="ada",
    )(c_all, w_ada, b_ada)


def _rope_store(acc, cos, sa, sb, out_ref, ncols, scale):
    for c in range(ncols // LANES):
        xc = acc[:, LANES * c:LANES * (c + 1)]
        rot = xc * cos + pltpu.roll(xc, LANES - 8, 1) * sa + pltpu.roll(xc, 8, 1) * sb
        if scale != 1.0:
            rot = rot * scale
        out_ref[:, LANES * c:LANES * (c + 1)] = rot.astype(out_ref.dtype)


def _inproj_kernel(x_ref, sh_ref, sc_ref, nw_ref, wq_ref, win_ref, wg_ref, wgt_ref, cos_ref, sa_ref, sb_ref,
                   q_ref, kv_ref, kv32_ref, m_ref, g_ref, gt_ref, *rest, per_row, emit_w):
    h_scr = rest[-1]
    j = pl.program_id(1)

    def weight_tile():
        wt = win_ref[...].astype(BF16)
        if emit_w:
            rest[0][...] = wt
        return wt

    @pl.when(j == 0)
    def _():
        x = x_ref[...]
        r = lax.rsqrt(jnp.mean(x * x, axis=-1, keepdims=True) + EPS)
        y = x * r * nw_ref[...]
        h = (y * (1.0 + _mod_row(sc_ref, per_row)) + _mod_row(sh_ref, per_row)).astype(BF16)
        h_scr[...] = h
        g_ref[...] = _dot(h, wg_ref[...])
        gt_ref[...] = _dot_nt(wgt_ref[...], h)

    @pl.when(j < 2)
    def _():
        acc = _dot_nt(h_scr[...], wq_ref[...])
        _rope_store(acc, cos_ref[...], sa_ref[...], sb_ref[...], q_ref, 512, ATT_HEAD_DIM ** -0.5)

    @pl.when(j == 2)
    def _():
        acc = _dot_nt(h_scr[...], weight_tile())
        _rope_store(acc, cos_ref[...], sa_ref[...], sb_ref[...], kv32_ref, KV_WIDTH, 1.0)
        kv32_ref[:, KV_WIDTH:] = acc[:, KV_WIDTH:]
        kv_ref[...] = kv32_ref[...].astype(BF16)

    @pl.when(j > 2)
    def _():
        m_ref[...] = _dot_nt(h_scr[...], weight_tile()).astype(BF16)


def _inproj(x, mod, mod_row_block, norm_w, wq_t, w_in_t, wg, wgt, cos, sa, sb, *, tm, per_row, emit_w):
    rows = x.shape[0]
    tn = 512
    nj = MAIN_WIDTH // tn
    mod_rows = tm if per_row else 8
    mod_idx = (lambda i: i) if per_row else (lambda i: mod_row_block)
    kern = functools.partial(_inproj_kernel, per_row=per_row, emit_w=emit_w)
    copy_spec = pl.BlockSpec((tn, D_MODEL), lambda i, j: (jnp.maximum(j, 2) - 2, 0))
    w_spec = pl.BlockSpec((tn, D_MODEL), lambda i, j: (jnp.maximum(j, 2), 0)) if emit_w else copy_spec
    extra_specs = [copy_spec] if emit_w else []
    extra_shapes = [jax.ShapeDtypeStruct((MAIN_WIDTH - ATT_WIDTH, D_MODEL), BF16)] if emit_w else []
    return pl.pallas_call(
        kern,
        grid=(rows // tm, nj),
        in_specs=[
            pl.BlockSpec((tm, D_MODEL), lambda i, j: (jnp.minimum(i + jnp.minimum(j, 1), rows // tm - 1), 0)),
            pl.BlockSpec((mod_rows, D_MODEL), lambda i, j: (mod_idx(i), 0)),
            pl.BlockSpec((mod_rows, D_MODEL), lambda i, j: (mod_idx(i), 1)),
            pl.BlockSpec((1, D_MODEL), lambda i, j: (0, 0)),
            pl.BlockSpec((tn, D_MODEL), lambda i, j: (jnp.minimum(j, 1), 0)),
            w_spec,
            pl.BlockSpec((D_MODEL, LANES), lambda i, j: (0, 0)),
            pl.BlockSpec((16, D_MODEL), lambda i, j: (0, 0)),
            pl.BlockSpec((tm, LANES), lambda i, j: (i, 0)),
            pl.BlockSpec((tm, LANES), lambda i, j: (i, 0)),
            pl.BlockSpec((tm, LANES), lambda i, j: (i, 0)),
        ],
        out_specs=[
            pl.BlockSpec((tm, tn), lambda i, j: (i, jnp.minimum(j, 1))),
            pl.BlockSpec((tm, tn), lambda i, j: (i, 0)),
            pl.BlockSpec((tm, tn), lambda i, j: (0, 0)),
            pl.BlockSpec((tm, tn), lambda i, j: (i, jnp.clip(j - 3, 0, 7))),
            pl.BlockSpec((tm, LANES), lambda i, j: (i, 0)),
            pl.BlockSpec((16, tm), lambda i, j: (0, i)),
        ] + extra_specs,
        out_shape=[
            jax.ShapeDtypeStruct((rows, ATT_WIDTH), BF16),
            jax.ShapeDtypeStruct((rows, 2 * KV_WIDTH), BF16),
            jax.ShapeDtypeStruct((tm, 2 * KV_WIDTH), F32),
            jax.ShapeDtypeStruct((rows, 4 * MLSTM_WIDTH), BF16),
            jax.ShapeDtypeStruct((rows, LANES), F32),
            jax.ShapeDtypeStruct((16, rows), F32),
        ] + extra_shapes,
        scratch_shapes=[pltpu.VMEM((tm, D_MODEL), BF16)],
        compiler_params=_cparams(("arbitrary", "arbitrary")),
        name="inproj_s" if per_row else "inproj_p",
    )(x, mod, mod, norm_w, wq_t, w_in_t, wg, wgt, cos, sa, sb)


def _attn_block(sink_ref, q_ref, row0, kv2, allowed, store):
    w = WINDOW
    grp = ATT_GROUP
    rows = grp * w
    member = lax.broadcasted_iota(jnp.int32, (rows, 1), 0) // w
    low = lax.broadcasted_iota(jnp.int32, (2 * w, LANES), 1) < ATT_HEAD_DIM
    low_o = lax.broadcasted_iota(jnp.int32, (rows, LANES), 1) < ATT_HEAD_DIM
    key_row = lax.broadcasted_iota(jnp.int32, (4 * w, LANES), 0)
    key_lane = lax.broadcasted_iota(jnp.int32, (4 * w, LANES), 1)
    ones_bd = (((key_row < 2 * w) & (key_lane < ATT_HEAD_DIM))
               | ((key_row >= 2 * w) & (key_lane >= ATT_HEAD_DIM))).astype(BF16)
    zero = jnp.zeros((2 * w, LANES), BF16)
    for cp in range(2):
        k128 = kv2[:, LANES * cp:LANES * (cp + 1)]
        v128 = kv2[:, KV_WIDTH + LANES * cp:KV_WIDTH + LANES * (cp + 1)]
        kbd = jnp.concatenate([jnp.where(low, k128, zero), jnp.where(low, zero, k128)], axis=0)
        vbd = jnp.concatenate([jnp.where(low, v128, zero), jnp.where(low, zero, v128)], axis=0)
        v_aug = jnp.concatenate([vbd, ones_bd], axis=1)
        q4 = jnp.concatenate([q_ref[row0:row0 + w, 256 * r + LANES * cp:256 * r + LANES * (cp + 1)]
                              for r in range(grp)], axis=0)
        s = _dot_nt(q4, kbd)
        es, tails = [], []
        for half in range(2):
            sh = jnp.where(allowed, s[:, 2 * w * half:2 * w * (half + 1)], -jnp.inf)
            head0 = (2 * cp + half) * grp
            sink = jnp.full((rows, 1), sink_ref[head0], F32)
            for r in range(1, grp):
                sink = jnp.where(member == r, sink_ref[head0 + r], sink)
            m = jnp.maximum(jnp.max(sh, axis=-1, keepdims=True), sink)
            es.append(jnp.exp(sh - m).astype(BF16))
            tails.append(jnp.exp(sink - m))
        oa = _dot(jnp.concatenate(es, axis=1), v_aug)
        l = oa[:, LANES:] + jnp.where(low_o, tails[0], tails[1])
        o = (oa[:, :LANES] / l).astype(BF16)
        for r in range(grp):
            store(256 * r + LANES * cp, o[w * r:w * (r + 1)])


MIX_TM = 512


def _mix_p_kernel(sink_ref, q_ref, kvp_ref, kvc_ref, mq_ref, mk_ref, mv_ref, mo_ref, g_ref, gt_ref, brow_ref, bcol_ref,
                  mnw_ref, x_ref, g1_ref, sh_ref, sc_ref, nw_ref, wa_ref, wm_ref,
                  x1_ref, h2_ref, c_out, n_out, m_out, att_scr, mh_scr, c_scr, n_scr, m_scr):
    s = pl.program_id(0)
    w = WINDOW
    last_tile = pl.num_programs(0) - 2

    @pl.when(s == 0)
    def _():
        att_scr[...] = jnp.zeros_like(att_scr)
        mh_scr[...] = jnp.zeros_like(mh_scr)
        c_scr[...] = jnp.zeros_like(c_scr)
        n_scr[...] = jnp.zeros_like(n_scr)
        m_scr[...] = jnp.zeros_like(m_scr)

    slot = s % 2
    y = _dot(att_scr[1 - slot], wa_ref[...]) + _dot(mh_scr[1 - slot], wm_ref[...])
    x1 = x_ref[...] + g1_ref[0:1, :] * y
    x1_ref[...] = x1
    r = lax.rsqrt(jnp.mean(x1 * x1, axis=-1, keepdims=True) + EPS)
    h2_ref[...] = ((x1 * r * nw_ref[...]) * (1.0 + sc_ref[0:1, :]) + sh_ref[0:1, :]).astype(BF16)

    rows = ATT_GROUP * w
    qi = lax.broadcasted_iota(jnp.int32, (rows, 2 * w), 0) % w
    kj = lax.broadcasted_iota(jnp.int32, (rows, 2 * w), 1)
    first_off = jnp.where(s > 0, 0, 4 * w)
    causal = (kj >= w) & (kj - w <= qi)
    for blk in range(MIX_TM // w):
        prev = kvp_ref[...] if blk == 0 else kvc_ref[w * (blk - 1):w * blk, :]
        kv2 = jnp.concatenate([prev, kvc_ref[w * blk:w * (blk + 1), :]], axis=0)
        allowed = ((kj < w) & (kj > qi + (first_off if blk == 0 else 0))) | causal

        def store(c0, val, blk=blk):
            att_scr[slot, w * blk:w * (blk + 1), c0:c0 + LANES] = val

        _attn_block(sink_ref, q_ref, w * blk, kv2, allowed, store)

    for ch in range(MIX_TM // MLSTM_CHUNK_P):
        r0 = MLSTM_CHUNK_P * ch

        def store_mh(cs, val, r0=r0):
            mh_scr[slot, r0:r0 + MLSTM_CHUNK_P, cs] = val

        _mlstm_chunk(mq_ref, mk_ref, mv_ref, mo_ref, g_ref, gt_ref, brow_ref, bcol_ref, mnw_ref,
                     c_scr, n_scr, m_scr, r0, store_mh)

    @pl.when(s == last_tile)
    def _():
        c_out[...] = c_scr[...]
        n_out[...] = n_scr[...]
        m_out[...] = m_scr[...]


def _mix_p(sinks, q, kv, m_all, gates, gates_t, brow, bcol, mnw, x, mod, mod_row_block, norm_w, w_att, w_m):
    tm = MIX_TM
    tiles = SEQ // tm
    bpt = tm // WINDOW
    dh = MLSTM_HEAD_DIM
    att_tile = lambda s: jnp.minimum(s, tiles - 1)
    out_tile = lambda s: jnp.maximum(s - 1, 0)
    m_spec = lambda col: pl.BlockSpec((tm, MLSTM_WIDTH), lambda s: (att_tile(s), col))
    return pl.pallas_call(
        _mix_p_kernel,
        grid=(tiles + 1,),
        in_specs=[
            pl.BlockSpec(memory_space=pltpu.SMEM),
            pl.BlockSpec((tm, ATT_WIDTH), lambda s: (att_tile(s), 0)),
            pl.BlockSpec((WINDOW, 2 * KV_WIDTH), lambda s: (jnp.maximum(bpt * att_tile(s) - 1, 0), 0)),
            pl.BlockSpec((tm, 2 * KV_WIDTH), lambda s: (att_tile(s), 0)),
            m_spec(0), m_spec(1), m_spec(2), m_spec(3),
            pl.BlockSpec((tm, LANES), lambda s: (att_tile(s), 0)),
            pl.BlockSpec((16, tm), lambda s: (0, att_tile(s))),
            pl.BlockSpec((1, LANES), lambda s: (0, 0)),
            pl.BlockSpec((16, MLSTM_CHUNK_P), lambda s: (0, 0)),
            pl.BlockSpec((1, MLSTM_WIDTH), lambda s: (0, 0)),
            pl.BlockSpec((tm, D_MODEL), lambda s: (out_tile(s), 0)),
            pl.BlockSpec((8, D_MODEL), lambda s: (mod_row_block, 2)),
            pl.BlockSpec((8, D_MODEL), lambda s: (mod_row_block, 3)),
            pl.BlockSpec((8, D_MODEL), lambda s: (mod_row_block, 4)),
            pl.BlockSpec((1, D_MODEL), lambda s: (0, 0)),
            pl.BlockSpec((ATT_WIDTH, D_MODEL), lambda s: (0, 0)),
            pl.BlockSpec((MLSTM_WIDTH, D_MODEL), lambda s: (0, 0)),
        ],
        out_specs=[
            pl.BlockSpec((tm, D_MODEL), lambda s: (out_tile(s), 0)),
            pl.BlockSpec((tm, D_MODEL), lambda s: (out_tile(s), 0)),
            pl.BlockSpec((MLSTM_HEADS, dh, dh), lambda s: (0, 0, 0)),
            pl.BlockSpec((8, dh), lambda s: (0, 0)),
            pl.BlockSpec((8, LANES), lambda s: (0, 0)),
        ],
        out_shape=[
            jax.ShapeDtypeStruct((SEQ, D_MODEL), F32),
            jax.ShapeDtypeStruct((SEQ, D_MODEL), BF16),
            jax.ShapeDtypeStruct((MLSTM_HEADS, dh, dh), F32),
            jax.ShapeDtypeStruct((8, dh), F32),
            jax.ShapeDtypeStruct((8, LANES), F32),
        ],
        scratch_shapes=[
            pltpu.VMEM((2, tm, ATT_WIDTH), BF16),
            pltpu.VMEM((2, tm, MLSTM_WIDTH), BF16),
            pltpu.VMEM((MLSTM_HEADS, dh, dh), F32),
            pltpu.VMEM((8, dh), F32),
            pltpu.VMEM((8, LANES), F32),
        ],
        compiler_params=_cparams(("arbitrary",)),
        name="mix_p",
    )(sinks, q, kv, kv, m_all, m_all, m_all, m_all, gates, gates_t, brow, bcol, mnw, x, mod, mod, mod, norm_w,
      w_att, w_m)


ATT_S_BB = 8


def _attn_s_kernel(sink_ref, q_ref, kv32_ref, ck_ref, cv_ref, o_ref, ko_ref, vo_ref, q32_scr):
    t_new = DEC_SEQ
    w = WINDOW
    q32_scr[...] = q_ref[...].astype(F32)
    rows = 4 * 4 * 8
    row = lax.broadcasted_iota(jnp.int32, (rows, w), 0)
    slot = lax.broadcasted_iota(jnp.int32, (rows, w), 1)
    t_row = row % t_new
    second = (row % 8) >= t_new
    win_ok = (slot < w - t_new) | (slot - (w - t_new) <= t_row)
    old_ok = (slot >= 1) & (slot < t_new) & (slot > t_row)
    lane256 = lax.broadcasted_iota(jnp.int32, (32, 2 * LANES), 1)
    sink = sink_ref[...][:, 0:1]
    kv_new = jnp.concatenate([kv32_ref[...], jnp.zeros((w - ATT_S_BB * t_new, 2 * KV_WIDTH), F32)], axis=0)
    kv_t = kv_new.T
    new_slot = lax.broadcasted_iota(jnp.int32, (KV_WIDTH, w), 1) >= w - t_new
    for b in range(ATT_S_BB):
        cols = pltpu.roll(kv_t, w - t_new - t_new * b, 1)
        k_shift = pltpu.roll(ck_ref[b].reshape(KV_WIDTH, w), w - t_new, 1)
        v_shift = pltpu.roll(cv_ref[b].reshape(KV_WIDTH, w), w - t_new, 1)
        ko_ref[b] = jnp.where(new_slot, cols[:KV_WIDTH], k_shift).reshape(ATT_KV_HEADS, ATT_HEAD_DIM, w)
        vo_ref[b] = jnp.where(new_slot, cols[KV_WIDTH:], v_shift).reshape(ATT_KV_HEADS, ATT_HEAD_DIM, w)
    for pair in range(ATT_S_BB // 2):
        b0, b1 = 2 * pair, 2 * pair + 1
        q32 = jnp.concatenate([q32_scr[8 * pair:8 * (pair + 1), 256 * r:256 * (r + 1)] for r in range(ATT_GROUP)],
                              axis=0)
        qpad = jnp.concatenate(
            [jnp.where((lane256 // ATT_HEAD_DIM) == g, q32, 0.0) for g in range(ATT_KV_HEADS)], axis=0).astype(BF16)
        kw = [ko_ref[b].reshape(KV_WIDTH, w).astype(BF16) for b in (b0, b1)]
        vw = [vo_ref[b].reshape(KV_WIDTH, w).astype(BF16) for b in (b0, b1)]
        kc = [ck_ref[b].reshape(KV_WIDTH, w).astype(BF16) for b in (b0, b1)]
        vc = [cv_ref[b].reshape(KV_WIDTH, w).astype(BF16) for b in (b0, b1)]
        s_w = jnp.where(second, _dot(qpad, kw[1]), _dot(qpad, kw[0]))
        s_c = jnp.where(second, _dot(qpad, kc[1]), _dot(qpad, kc[0]))
        s_w = jnp.where(win_ok, s_w, -jnp.inf)
        s_c = jnp.where(old_ok, s_c, -jnp.inf)
        m = jnp.maximum(jnp.maximum(jnp.max(s_w, axis=-1, keepdims=True), jnp.max(s_c, axis=-1, keepdims=True)), sink)
        e_w = jnp.exp(s_w - m)
        e_c = jnp.exp(s_c - m)
        l = jnp.sum(e_w, axis=-1, keepdims=True) + jnp.sum(e_c, axis=-1, keepdims=True) + jnp.exp(sink - m)
        p_w = e_w / l
        p_c = e_c / l
        zero = jnp.zeros_like(p_w)
        o = (_dot_nt(jnp.where(second, zero, p_w).astype(BF16), vw[0])
             + _dot_nt(jnp.where(second, p_w, zero).astype(BF16), vw[1])
             + _dot_nt(jnp.where(second, zero, p_c).astype(BF16), vc[0])
             + _dot_nt(jnp.where(second, p_c, zero).astype(BF16), vc[1]))
        o32 = jnp.zeros((32, 2 * LANES), F32)
        for g in range(ATT_KV_HEADS):
            o32 = jnp.where((lane256 // ATT_HEAD_DIM) == g, o[32 * g:32 * (g + 1), :], o32)
        for r in range(ATT_GROUP):
            o_ref[8 * pair:8 * (pair + 1), 256 * r:256 * (r + 1)] = o32[8 * r:8 * (r + 1), :]


def _attn_s(sink_col, q, kv32, ck, cv):
    bb = ATT_S_BB
    rows = bb * DEC_SEQ
    cache_block = (bb, ATT_KV_HEADS, ATT_HEAD_DIM, WINDOW)
    cache_shape = (DEC_BATCH, ATT_KV_HEADS, ATT_HEAD_DIM, WINDOW)
    return pl.pallas_call(
        _attn_s_kernel,
        grid=(DEC_BATCH // bb,),
        in_specs=[
            pl.BlockSpec((128, LANES), lambda i: (0, 0)),
            pl.BlockSpec((rows, ATT_WIDTH), lambda i: (i, 0)),
            pl.BlockSpec((rows, 2 * KV_WIDTH), lambda i: (i, 0)),
            pl.BlockSpec(cache_block, lambda i: (i, 0, 0, 0)),
            pl.BlockSpec(cache_block, lambda i: (i, 0, 0, 0)),
        ],
        out_specs=[
            pl.BlockSpec((rows, ATT_WIDTH), lambda i: (i, 0)),
            pl.BlockSpec(cache_block, lambda i: (i, 0, 0, 0)),
            pl.BlockSpec(cache_block, lambda i: (i, 0, 0, 0)),
        ],
        out_shape=[
            jax.ShapeDtypeStruct((S_ROWS, ATT_WIDTH), F32),
            jax.ShapeDtypeStruct(cache_shape, F32),
            jax.ShapeDtypeStruct(cache_shape, F32),
        ],
        scratch_shapes=[pltpu.VMEM((rows, ATT_WIDTH), F32)],
        compiler_params=_cparams(("arbitrary",)),
        name="attn_s",
    )(sink_col, q, kv32, ck, cv)


def _head_norm_gate(h, nw, mo):
    hn = h * lax.rsqrt(jnp.mean(h * h, axis=-1, keepdims=True) + EPS) * nw
    return hn * _sigmoid(mo.astype(F32))


def _mlstm_chunk(q_ref, k_ref, v_ref, mo_ref, g_ref, gt_ref, brow_ref, bcol_ref, nw_ref, c_scr, n_scr, m_scr,
                 r0, store):
    L = MLSTM_CHUNK_P
    dh = MLSTM_HEAD_DIM
    rs = slice(r0, r0 + L)
    ti = lax.broadcasted_iota(jnp.int32, (L, L), 0)
    si = lax.broadcasted_iota(jnp.int32, (L, L), 1)
    causal = si <= ti
    tri = causal.astype(F32)
    tri_t = (ti <= si).astype(F32)
    gates = g_ref[rs, :] + brow_ref[...]
    gates_t = gt_ref[:, rs] + bcol_ref[...]
    b_col = jnp.dot(tri, _log_sigmoid(gates), precision=HI, preferred_element_type=F32)
    b_row = jnp.dot(_log_sigmoid(gates_t), tri_t, precision=HI, preferred_element_type=F32)
    for hd in range(MLSTM_HEADS):
        cs = slice(dh * hd, dh * (hd + 1))
        b_c = b_col[:, MLSTM_HEADS + hd:MLSTM_HEADS + hd + 1]
        li_c = gates[:, hd:hd + 1]
        b_r = b_row[MLSTM_HEADS + hd:MLSTM_HEADS + hd + 1, :]
        li_r = gates_t[hd:hd + 1, :]
        dm = jnp.where(causal, b_c - b_r + li_r, -jnp.inf)
        m_prev = m_scr[hd:hd + 1, 0:1]
        m_inter = b_c + m_prev
        m_t = jnp.maximum(m_inter, jnp.max(dm, axis=-1, keepdims=True))
        q = q_ref[rs, cs] * (dh ** -0.5)
        k = k_ref[rs, cs]
        v = v_ref[rs, cs]
        sm = _dot_nt(q, k) * jnp.exp(dm - m_t)
        a = jnp.exp(m_inter - m_t)
        c_old = c_scr[hd]
        n_old = n_scr[hd:hd + 1, :]
        num = a * _dot_nt(q, c_old.astype(BF16)) + _dot(sm.astype(BF16), v)
        qn = jnp.sum(q.astype(F32) * n_old, axis=-1, keepdims=True)
        den = a * qn + jnp.sum(sm, axis=-1, keepdims=True)
        h = num / jnp.maximum(jnp.abs(den), jnp.exp(-m_t))
        store(cs, _head_norm_gate(h, nw_ref[:, cs], mo_ref[rs, cs]).astype(BF16))
        m_new = m_t[L - 1:L, :]
        b_last = b_c[L - 1:L, :]
        g = jnp.exp(b_last - b_c + li_c - m_new)
        decay = jnp.exp(b_last + m_prev - m_new)
        gv = (g * v.astype(F32)).astype(BF16)
        c_scr[hd] = decay * c_old + _dot_tn(gv, k)
        n_scr[hd:hd + 1, :] = decay * n_old + jnp.sum(g * k.astype(F32), axis=0, keepdims=True)
        m_scr[hd:hd + 1, :] = jnp.broadcast_to(m_new, (1, LANES))


MLSTM_S_BB = 4


def _mlstm_s_kernel(q_ref, k_ref, v_ref, mo_ref, g_ref, nrep_ref, mrep_ref, brow_ref, nw_ref, c_ref,
                    mh_ref, c_out, nrow_ref, mrow_ref):
    T = DEC_SEQ
    R = MLSTM_S_BB * T
    H = MLSTM_HEADS
    dh = MLSTM_HEAD_DIM

    def shift(x, d):
        return pltpu.roll(x, d, 0)

    lanes = lax.broadcasted_iota(jnp.int32, (R, LANES), 1)
    tmod = lax.broadcasted_iota(jnp.int32, (R, LANES), 0) % T
    tmod_w = lax.broadcasted_iota(jnp.int32, (R, MLSTM_WIDTH), 0) % T
    head_ok = lanes < H
    gates = g_ref[...] + brow_ref[...]
    li = jnp.where(head_ok, gates, 0.0)
    lf = jnp.where(head_ok, pltpu.roll(_log_sigmoid(gates), LANES - H, 1), 0.0)
    bcum = lf
    for d in range(1, T):
        bcum = bcum + jnp.where(tmod >= d, shift(lf, d), 0.0)
    m0 = mrep_ref[...]
    m_inter = bcum + m0
    dms = [li] + [jnp.where(tmod >= d, bcum - shift(bcum, d) + shift(li, d), -jnp.inf) for d in range(1, T)]
    m_t = m_inter
    for dm in dms:
        m_t = jnp.maximum(m_t, dm)
    a = jnp.exp(m_inter - m_t)
    ws = [jnp.exp(dm - m_t) for dm in dms]

    q_bf = q_ref[...] * (dh ** -0.5)
    q = q_bf.astype(F32)
    k = k_ref[...].astype(F32)
    v = v_ref[...].astype(F32)
    seg = (lax.broadcasted_iota(jnp.int32, (MLSTM_WIDTH, LANES), 0) // dh
           == lax.broadcasted_iota(jnp.int32, (MLSTM_WIDTH, LANES), 1)).astype(F32)
    ex = (lax.broadcasted_iota(jnp.int32, (LANES, MLSTM_WIDTH), 1) // dh
          == lax.broadcasted_iota(jnp.int32, (LANES, MLSTM_WIDTH), 0)).astype(F32)

    def segsum(x):
        return jnp.dot(x, seg, precision=HI, preferred_element_type=F32)

    def expand(x):
        return jnp.dot(x, ex, precision=HI, preferred_element_type=F32)

    ks = [k] + [shift(k, d) for d in range(1, T)]
    vs = [v] + [shift(v, d) for d in range(1, T)]
    sms = [segsum(q * ks[d]) * ws[d] for d in range(T)]
    den = a * segsum(q * nrep_ref[...])
    for sm in sms:
        den = den + sm
    inv = 1.0 / jnp.maximum(jnp.abs(den), jnp.exp(-m_t))

    def last(x):
        out = jnp.zeros_like(x)
        for jj in range(T):
            out = jnp.where(tmod == T - 1 - jj, x if jj == 0 else pltpu.roll(x, R - jj, 0), out)
        return out

    m_new = last(m_t)
    b_last = last(bcum)
    g = jnp.where(head_ok, jnp.exp(b_last - bcum + li - m_new), 0.0)
    decay = jnp.where(head_ok, jnp.exp(b_last + m0 - m_new), 0.0)
    a_f = expand(a * inv)
    w_f = [expand(sm * inv) for sm in sms]
    g_f = expand(g)
    d_f = expand(decay)
    gv = (g_f * v).astype(BF16)
    rowb = lax.broadcasted_iota(jnp.int32, (R, dh), 0) // T
    for hd in range(H):
        cs = slice(dh * hd, dh * (hd + 1))
        qh = q_bf[:, cs]
        kh = k_ref[:, cs]
        gvh = gv[:, cs]
        qc = jnp.zeros((R, dh), F32)
        for bb in range(MLSTM_S_BB):
            c_old = c_ref[bb, hd]
            qc = jnp.where(rowb == bb, _dot_nt(qh, c_old.astype(BF16)), qc)
            upd = _dot_tn(jnp.where(rowb == bb, gvh, jnp.zeros_like(gvh)), kh)
            c_out[bb, hd] = d_f[T * bb:T * bb + 1, cs] * c_old + upd
        h = a_f[:, cs] * qc
        for d in range(T):
            h = h + w_f[d][:, cs] * vs[d][:, cs]
        mh_ref[:, cs] = _head_norm_gate(h, nw_ref[:, cs], mo_ref[:, cs]).astype(BF16)
    gk = g_f * k
    nsum = gk
    for d in range(1, T):
        nsum = nsum + jnp.where(tmod_w >= d, shift(gk, d), 0.0)
    nrow_ref[...] = d_f * nrep_ref[...] + nsum
    mrow_ref[...] = m_t


def _mlstm_s(m_all, gates, n_rep, m_rep, brow, nw, state_c):
    bb = MLSTM_S_BB
    R = bb * DEC_SEQ
    dh = MLSTM_HEAD_DIM
    H = MLSTM_HEADS
    return pl.pallas_call(
        _mlstm_s_kernel,
        grid=(DEC_BATCH // bb,),
        in_specs=[
            pl.BlockSpec((R, MLSTM_WIDTH), lambda i: (i, 0)),
            pl.BlockSpec((R, MLSTM_WIDTH), lambda i: (i, 1)),
            pl.BlockSpec((R, MLSTM_WIDTH), lambda i: (i, 2)),
            pl.BlockSpec((R, MLSTM_WIDTH), lambda i: (i, 3)),
            pl.BlockSpec((R, LANES), lambda i: (i, 0)),
            pl.BlockSpec((R, MLSTM_WIDTH), lambda i: (i, 0)),
            pl.BlockSpec((R, LANES), lambda i: (i, 0)),
            pl.BlockSpec((1, LANES), lambda i: (0, 0)),
            pl.BlockSpec((1, MLSTM_WIDTH), lambda i: (0, 0)),
            pl.BlockSpec((bb, H, dh, dh), lambda i: (i, 0, 0, 0)),
        ],
        out_specs=[
            pl.BlockSpec((R, MLSTM_WIDTH), lambda i: (i, 0)),
            pl.BlockSpec((bb, H, dh, dh), lambda i: (i, 0, 0, 0)),
            pl.BlockSpec((R, MLSTM_WIDTH), lambda i: (i, 0)),
            pl.BlockSpec((R, LANES), lambda i: (i, 0)),
        ],
        out_shape=[
            jax.ShapeDtypeStruct((S_ROWS, MLSTM_WIDTH), BF16),
            jax.ShapeDtypeStruct((DEC_BATCH, H, dh, dh), F32),
            jax.ShapeDtypeStruct((S_ROWS, MLSTM_WIDTH), F32),
            jax.ShapeDtypeStruct((S_ROWS, LANES), F32),
        ],
        compiler_params=_cparams(("arbitrary",)),
        name="mlstm_s",
    )(m_all, m_all, m_all, m_all, gates, n_rep, m_rep, brow, nw, state_c)


def _outproj_kernel(att_ref, mh_ref, x_ref, g1_ref, sh_ref, sc_ref, nw_ref, wa_ref, wm_ref, x1_ref, h2_ref, *,
                    per_row):
    y = _dot(att_ref[...].astype(BF16), wa_ref[...]) + _dot(mh_ref[...], wm_ref[...])
    x1 = x_ref[...] + _mod_row(g1_ref, per_row) * y
    x1_ref[...] = x1
    r = lax.rsqrt(jnp.mean(x1 * x1, axis=-1, keepdims=True) + EPS)
    h2 = (x1 * r * nw_ref[...]) * (1.0 + _mod_row(sc_ref, per_row)) + _mod_row(sh_ref, per_row)
    h2_ref[...] = h2.astype(BF16)


def _outproj(att, mh, x, mod, mod_row_block, norm_w, w_att, w_m, *, tm, per_row):
    rows = x.shape[0]
    mod_rows = tm if per_row else 8
    mod_idx = (lambda i: i) if per_row else (lambda i: mod_row_block)
    kern = functools.partial(_outproj_kernel, per_row=per_row)
    return pl.pallas_call(
        kern,
        grid=(rows // tm,),
        in_specs=[
            pl.BlockSpec((tm, ATT_WIDTH), lambda i: (i, 0)),
            pl.BlockSpec((tm, MLSTM_WIDTH), lambda i: (i, 0)),
            pl.BlockSpec((tm, D_MODEL), lambda i: (i, 0)),
            pl.BlockSpec((mod_rows, D_MODEL), lambda i: (mod_idx(i), 2)),
            pl.BlockSpec((mod_rows, D_MODEL), lambda i: (mod_idx(i), 3)),
            pl.BlockSpec((mod_rows, D_MODEL), lambda i: (mod_idx(i), 4)),
            pl.BlockSpec((1, D_MODEL), lambda i: (0, 0)),
            pl.BlockSpec((ATT_WIDTH, D_MODEL), lambda i: (0, 0)),
            pl.BlockSpec((MLSTM_WIDTH, D_MODEL), lambda i: (0, 0)),
        ],
        out_specs=[
            pl.BlockSpec((tm, D_MODEL), lambda i: (i, 0)),
            pl.BlockSpec((tm, D_MODEL), lambda i: (i, 0)),
        ],
        out_shape=[
            jax.ShapeDtypeStruct((rows, D_MODEL), F32),
            jax.ShapeDtypeStruct((rows, D_MODEL), BF16),
        ],
        compiler_params=_cparams(("arbitrary",)),
        name="outproj_s" if per_row else "outproj_p",
    )(att, mh, x, mod, mod, mod, norm_w, w_att, w_m)


def _ffn_a_kernel(hp_ref, hs_ref, wg_ref, wu_ref, ap_ref, as_ref, wg_scr, wu_scr):
    i = pl.program_id(1)

    def swiglu(h_ref, a_ref):
        h = h_ref[...]
        g = _dot(h, wg_scr[...])
        u = _dot(h, wu_scr[...])
        a_ref[...] = (g * _sigmoid(g) * u).astype(BF16)

    @pl.when(i == 0)
    def _():
        wg_scr[...] = wg_ref[...].astype(BF16)
        wu_scr[...] = wu_ref[...].astype(BF16)
        swiglu(hs_ref, as_ref)

    @pl.when(i > 0)
    def _():
        swiglu(hp_ref, ap_ref)


def _ffn_a(h2_p, h2_s, w_gate, w_up, *, tm):
    rows_p, rows_s = h2_p.shape[0], h2_s.shape[0]
    tn = 512
    ni = rows_p // tm
    tile = lambda i: jnp.maximum(i - 1, 0)
    return pl.pallas_call(
        _ffn_a_kernel,
        grid=(D_FF // tn, ni + 1),
        in_specs=[
            pl.BlockSpec((tm, D_MODEL), lambda j, i: (tile(i), 0)),
            pl.BlockSpec((rows_s, D_MODEL), lambda j, i: (0, 0)),
            pl.BlockSpec((D_MODEL, tn), lambda j, i: (0, j)),
            pl.BlockSpec((D_MODEL, tn), lambda j, i: (0, j)),
        ],
        out_specs=[
            pl.BlockSpec((tm, tn), lambda j, i: (tile(i), j)),
            pl.BlockSpec((rows_s, tn), lambda j, i: (0, j)),
        ],
        out_shape=[
            jax.ShapeDtypeStruct((rows_p, D_FF), BF16),
            jax.ShapeDtypeStruct((rows_s, D_FF), BF16),
        ],
        scratch_shapes=[pltpu.VMEM((D_MODEL, tn), BF16), pltpu.VMEM((D_MODEL, tn), BF16)],
        compiler_params=_cparams(("arbitrary", "arbitrary")),
        name="ffn_a",
    )(h2_p, h2_s, w_gate, w_up)


def _ffn_b_kernel(a_ref, wd_ref, x1a_ref, x1b_ref, x1c_ref, x1d_ref, g2_ref, fw_ref, y_ref, *rest, per_row, emit_w):
    kk = pl.program_id(1)

    def weight_tile():
        wt = wd_ref[...].astype(BF16)
        if emit_w:
            rest[0][...] = wt
        return wt

    @pl.when(kk == 0)
    def _():
        y_ref[...] = _dot(a_ref[...], weight_tile())

    @pl.when(kk > 0)
    def _():
        y_ref[...] += _dot(a_ref[...], weight_tile())

    @pl.when(kk == pl.num_programs(1) - 1)
    def _():
        x1 = jnp.concatenate([x1a_ref[...], x1b_ref[...], x1c_ref[...], x1d_ref[...]], axis=1)
        x2 = x1 + _mod_row(g2_ref, per_row) * y_ref[...]
        y_ref[...] = x2 * lax.rsqrt(jnp.mean(x2 * x2, axis=-1, keepdims=True) + EPS) * fw_ref[...]


def _ffn_b(a, w_down, x1, mod, mod_row_block, final_w, *, tm, per_row, emit_w):
    rows = a.shape[0]
    tk = 512
    mod_rows = tm if per_row else 8
    mod_idx = (lambda i: i) if per_row else (lambda i: mod_row_block)
    kern = functools.partial(_ffn_b_kernel, per_row=per_row, emit_w=emit_w)
    w_spec = pl.BlockSpec((tk, D_MODEL), lambda i, k: (k, 0))
    y_spec = pl.BlockSpec((tm, D_MODEL), lambda i, k: (i, 0))
    y_shape = jax.ShapeDtypeStruct((rows, D_MODEL), F32)

    def x1_chunk(c):
        return pl.BlockSpec((tm, D_MODEL // 4),
                            lambda i, k: (jnp.where(k >= 2 * c + 2, i, jnp.maximum(i - 1, 0)), c))

    return pl.pallas_call(
        kern,
        grid=(rows // tm, D_FF // tk),
        in_specs=[
            pl.BlockSpec((tm, tk), lambda i, k: (i, k)),
            w_spec,
            x1_chunk(0), x1_chunk(1), x1_chunk(2), x1_chunk(3),
            pl.BlockSpec((mod_rows, D_MODEL), lambda i, k: (mod_idx(i), 5)),
            pl.BlockSpec((1, D_MODEL), lambda i, k: (0, 0)),
        ],
        out_specs=[y_spec, w_spec] if emit_w else y_spec,
        out_shape=[y_shape, jax.ShapeDtypeStruct((D_FF, D_MODEL), BF16)] if emit_w else y_shape,
        compiler_params=_cparams(("arbitrary", "arbitrary")),
        name="ffn_b_s" if per_row else "ffn_b_p",
    )(a, w_down, x1, x1, x1, x1, mod, final_w)


def _rope_tables(pos):
    half = ROPE_DIM // 2
    inv = np.float32(ROPE_THETA) ** (-np.arange(0, ROPE_DIM, 2, dtype=np.float32) / np.float32(ROPE_DIM))
    d = np.arange(LANES) % ATT_HEAD_DIM
    ang = pos.astype(np.float32)[:, None] * inv[d % half][None, :].astype(np.float32)
    cos, sin = np.cos(ang), np.sin(ang)
    d = d[None, :]
    tables = (np.where(d < ROPE_DIM, cos, 1.0), np.where(d < half, -sin, 0.0),
              np.where((d >= half) & (d < ROPE_DIM), sin, 0.0))
    return tuple(jnp.asarray(t.astype(np.float32)) for t in tables)


def kernel(x_prompt, x_sample, cache_k_win, cache_v_win, state_C, state_n, state_m, c_prompt, c_sample,
           norm1_w, norm2_w, final_norm_w, w_ada, b_ada, w_in, b_ig, b_fg, attn_sinks, mh_norm_w,
           w_out, w_gate, w_up, w_down):
    assert w_in.shape[0] == 1, "single-layer trunk"
    T = DEC_SEQ
    xp = x_prompt[0]
    xs = x_sample.reshape(S_ROWS, D_MODEL)

    c_all = jnp.concatenate([jnp.repeat(c_sample, T, axis=0), c_prompt, jnp.zeros((15, D_MODEL), F32)], axis=0)
    mod = _ada(c_all, w_ada[0], b_ada)
    prompt_mod_block = S_ROWS // 8

    w_in_t = jnp.transpose(w_in[0])
    wq_t = (w_in_t[:ATT_WIDTH].reshape(ATT_KV_HEADS, ATT_GROUP, ATT_HEAD_DIM, D_MODEL)
            .transpose(1, 0, 2, 3).reshape(ATT_WIDTH, D_MODEL).astype(BF16))
    w_gates_t = w_in_t[MAIN_WIDTH:]
    wg = jnp.pad(w_gates_t.T, ((0, 0), (0, LANES - 2 * MLSTM_HEADS))).astype(BF16)
    wgt = jnp.pad(w_gates_t, ((0, 16 - 2 * MLSTM_HEADS), (0, 0))).astype(BF16)
    w_out_att = (w_out[0, :ATT_WIDTH].reshape(ATT_KV_HEADS, ATT_GROUP, ATT_HEAD_DIM, D_MODEL)
                 .transpose(1, 0, 2, 3).reshape(ATT_WIDTH, D_MODEL).astype(BF16))
    w_out_m = w_out[0, ATT_WIDTH:].astype(BF16)
    n1 = norm1_w.reshape(1, D_MODEL)
    n2 = norm2_w.reshape(1, D_MODEL)
    fw = final_norm_w.reshape(1, D_MODEL)
    nw = mh_norm_w.reshape(1, MLSTM_WIDTH)
    gate_bias = jnp.concatenate([b_ig[0], b_fg[0]])
    brow = jnp.pad(gate_bias, (0, LANES - 2 * MLSTM_HEADS)).reshape(1, LANES)
    bcol = jnp.broadcast_to(jnp.pad(gate_bias, (0, 16 - 2 * MLSTM_HEADS))[:, None], (16, MLSTM_CHUNK_P))

    rope_p = _rope_tables(np.arange(SEQ))
    rope_s = _rope_tables(np.tile(PAST_LEN + np.arange(T), DEC_BATCH))

    tm_p = 1024
    q_s, _, kv32_s, m_s, g_s, _, w_in_bf = _inproj(xs, mod, 0, n1, wq_t, w_in_t, wg, wgt, *rope_s,
                                                   tm=S_ROWS, per_row=True, emit_w=True)
    q_p, kv_p, kv32_p, m_p, g_p, gt_p = _inproj(xp, mod, prompt_mod_block, n1, wq_t, w_in_bf, wg, wgt, *rope_p,
                                                tm=tm_p, per_row=False, emit_w=False)

    sinks = attn_sinks[0]
    sink_col = jnp.broadcast_to(sinks.reshape(ATT_HEADS, 1, 1), (ATT_HEADS, 8, LANES)).reshape(128, LANES)
    ck = jnp.transpose(cache_k_win[0], (0, 2, 3, 1))
    cv = jnp.transpose(cache_v_win[0], (0, 2, 3, 1))
    att_s, kwin_s, vwin_s = _attn_s(sink_col, q_s, kv32_s, ck, cv)

    n_rep = jnp.repeat(state_n[0].reshape(DEC_BATCH, MLSTM_WIDTH), T, axis=0)
    m_rep = jnp.pad(jnp.repeat(state_m[0], T, axis=0), ((0, 0), (0, LANES - MLSTM_HEADS)))
    mh_s, c_s, nrow_s, mrow_s = _mlstm_s(m_s, g_s, n_rep, m_rep, brow, nw, state_C[0])

    x1_p, h2_p, c_p, n_p, mm_p = _mix_p(sinks, q_p, kv_p, m_p, g_p, gt_p, brow, bcol, nw, xp, mod, prompt_mod_block,
                                        n2, w_out_att, w_out_m)
    x1_s, h2_s = _outproj(att_s, mh_s, xs, mod, 0, n2, w_out_att, w_out_m, tm=S_ROWS, per_row=True)
    a_p, a_s = _ffn_a(h2_p, h2_s, w_gate[0], w_up[0], tm=tm_p)
    y_s, w_down_bf = _ffn_b(a_s, w_down[0], x1_s, mod, 0, fw, tm=S_ROWS, per_row=True, emit_w=True)
    y_p = _ffn_b(a_p, w_down_bf, x1_p, mod, prompt_mod_block, fw, tm=tm_p, per_row=False, emit_w=False)

    kv_shape = (1, 1, WINDOW, ATT_KV_HEADS, ATT_HEAD_DIM)
    kv_last = kv32_p[tm_p - WINDOW:]
    dh = MLSTM_HEAD_DIM
    return (
        y_p.reshape(1, SEQ, D_MODEL),
        y_s.reshape(DEC_BATCH, T, D_MODEL),
        kv_last[:, :KV_WIDTH].reshape(kv_shape),
        kv_last[:, KV_WIDTH:].reshape(kv_shape),
        c_p.reshape(1, 1, MLSTM_HEADS, dh, dh),
        n_p[:MLSTM_HEADS].reshape(1, 1, MLSTM_HEADS, dh),
        mm_p[:MLSTM_HEADS, 0].reshape(1, 1, MLSTM_HEADS),
        jnp.transpose(kwin_s, (0, 3, 1, 2))[None],
        jnp.transpose(vwin_s, (0, 3, 1, 2))[None],
        c_s.reshape(1, DEC_BATCH, MLSTM_HEADS, dh, dh),
        nrow_s[T - 1::T].reshape(1, DEC_BATCH, MLSTM_HEADS, dh),
        mrow_s[T - 1::T, :MLSTM_HEADS].reshape(1, DEC_BATCH, MLSTM_HEADS),
    )
```

```python
import functools

import jax
import jax.numpy as jnp
import numpy as np
from jax import lax
from jax.experimental import pallas as pl
from jax.experimental.pallas import tpu as pltpu

F32 = jnp.float32
BF16 = jnp.bfloat16

D_MODEL = 2048
SEQ = 8192
DEC_BATCH = 128
DEC_SEQ = 4
S_ROWS = DEC_BATCH * DEC_SEQ
PAST_LEN = 16384
ATT_HEADS = 16
ATT_KV_HEADS = 4
ATT_GROUP = 4
ATT_HEAD_DIM = 64
WINDOW = 128
ROPE_THETA = 500000.0
ROPE_DIM = 16
MLSTM_HEADS = 4
MLSTM_HEAD_DIM = 256
ATT_WIDTH = 1024
KV_WIDTH = 256
MLSTM_WIDTH = 1024
MAIN_WIDTH = ATT_WIDTH + 2 * KV_WIDTH + 4 * MLSTM_WIDTH
D_FF = 5632
N_MOD = 6
EPS = 1e-6

LANES = 128
MLSTM_CHUNK_P = 256
VMEM_LIMIT = 56 * 1024 * 1024

NT_DIMS = (((1,), (1,)), ((), ()))
TN_DIMS = (((0,), (0,)), ((), ()))
HI = lax.Precision.HIGHEST


def _cparams(sem):
    return pltpu.CompilerParams(dimension_semantics=sem, vmem_limit_bytes=VMEM_LIMIT)


def _dot(a, b):
    return jnp.dot(a, b, preferred_element_type=F32)


def _dot_nt(a, b):
    return lax.dot_general(a, b, NT_DIMS, preferred_element_type=F32)


def _dot_tn(a, b):
    return lax.dot_general(a, b, TN_DIMS, preferred_element_type=F32)


def _sigmoid(x):
    return 1.0 / (1.0 + jnp.exp(-x))


def _log_sigmoid(x):
    return jnp.minimum(x, 0.0) - jnp.log(1.0 + jnp.exp(-jnp.abs(x)))


def _mod_row(ref, per_row):
    return ref[...] if per_row else ref[0:1, :]


def _ada_kernel(c_ref, w_ref, b_ref, o_ref, s_scr):
    @pl.when(pl.program_id(0) == 0)
    def _():
        c = c_ref[...]
        s_scr[...] = (c * _sigmoid(c)).astype(BF16)

    o_ref[...] = _dot(s_scr[...], w_ref[...].astype(BF16)) + b_ref[...]


def _ada(c_all, w_ada, b_ada):
    m = c_all.shape[0]
    n = w_ada.shape[1]
    tn = 1024
    return pl.pallas_call(
        _ada_kernel,
        grid=(n // tn,),
        in_specs=[
            pl.BlockSpec((m, D_MODEL), lambda j: (0, 0)),
            pl.BlockSpec((D_MODEL, tn), lambda j: (0, j)),
            pl.BlockSpec((1, tn), lambda j: (0, j)),
        ],
        out_specs=pl.BlockSpec((m, tn), lambda j: (0, j)),
        out_shape=jax.ShapeDtypeStruct((m, n), F32),
        scratch_shapes=[pltpu.VMEM((m, D_MODEL), BF16)],
        compiler_params=_cparams(("arbitrary",)),
        name="ada",
    )(c_all, w_ada, b_ada)


def _rope_store(acc, cos, sa, sb, out_ref, ncols, scale):
    for c in range(ncols // LANES):
        xc = acc[:, LANES * c:LANES * (c + 1)]
        rot = xc * cos + pltpu.roll(xc, LANES - 8, 1) * sa + pltpu.roll(xc, 8, 1) * sb
        if scale != 1.0:
            rot = rot * scale
        out_ref[:, LANES * c:LANES * (c + 1)] = rot.astype(out_ref.dtype)


def _inproj_kernel(x_ref, sh_ref, sc_ref, nw_ref, wq_ref, win_ref, wg_ref, wgt_ref, cos_ref, sa_ref, sb_ref,
                   q_ref, kv_ref, kv32_ref, m_ref, g_ref, gt_ref, *rest, per_row, emit_w):
    h_scr = rest[-1]
    j = pl.program_id(1)

    def weight_tile():
        wt = win_ref[...].astype(BF16)
        if emit_w:
            rest[0][...] = wt
        return wt

    @pl.when(j == 0)
    def _():
        x = x_ref[...]
        r = lax.rsqrt(jnp.mean(x * x, axis=-1, keepdims=True) + EPS)
        y = x * r * nw_ref[...]
        h = (y * (1.0 + _mod_row(sc_ref, per_row)) + _mod_row(sh_ref, per_row)).astype(BF16)
        h_scr[...] = h
        g_ref[...] = _dot(h, wg_ref[...])
        gt_ref[...] = _dot_nt(wgt_ref[...], h)

    @pl.when(j < 2)
    def _():
        acc = _dot_nt(h_scr[...], wq_ref[...])
        _rope_store(acc, cos_ref[...], sa_ref[...], sb_ref[...], q_ref, 512, ATT_HEAD_DIM ** -0.5)

    @pl.when(j == 2)
    def _():
        acc = _dot_nt(h_scr[...], weight_tile())
        _rope_store(acc, cos_ref[...], sa_ref[...], sb_ref[...], kv32_ref, KV_WIDTH, 1.0)
        kv32_ref[:, KV_WIDTH:] = acc[:, KV_WIDTH:]
        kv_ref[...] = kv32_ref[...].astype(BF16)

    @pl.when(j > 2)
    def _():
        m_ref[...] = _dot_nt(h_scr[...], weight_tile()).astype(BF16)


def _inproj(x, mod, mod_row_block, norm_w, wq_t, w_in_t, wg, wgt, cos, sa, sb, *, tm, per_row, emit_w):
    rows = x.shape[0]
    tn = 512
    nj = MAIN_WIDTH // tn
    mod_rows = tm if per_row else 8
    mod_idx = (lambda i: i) if per_row else (lambda i: mod_row_block)
    kern = functools.partial(_inproj_kernel, per_row=per_row, emit_w=emit_w)
    copy_spec = pl.BlockSpec((tn, D_MODEL), lambda i, j: (jnp.maximum(j, 2) - 2, 0))
    w_spec = pl.BlockSpec((tn, D_MODEL), lambda i, j: (jnp.maximum(j, 2), 0)) if emit_w else copy_spec
    extra_specs = [copy_spec] if emit_w else []
    extra_shapes = [jax.ShapeDtypeStruct((MAIN_WIDTH - ATT_WIDTH, D_MODEL), BF16)] if emit_w else []
    return pl.pallas_call(
        kern,
        grid=(rows // tm, nj),
        in_specs=[
            pl.BlockSpec((tm, D_MODEL), lambda i, j: (jnp.minimum(i + jnp.minimum(j, 1), rows // tm - 1), 0)),
            pl.BlockSpec((mod_rows, D_MODEL), lambda i, j: (mod_idx(i), 0)),
            pl.BlockSpec((mod_rows, D_MODEL), lambda i, j: (mod_idx(i), 1)),
            pl.BlockSpec((1, D_MODEL), lambda i, j: (0, 0)),
            pl.BlockSpec((tn, D_MODEL), lambda i, j: (jnp.minimum(j, 1), 0)),
            w_spec,
            pl.BlockSpec((D_MODEL, LANES), lambda i, j: (0, 0)),
            pl.BlockSpec((16, D_MODEL), lambda i, j: (0, 0)),
            pl.BlockSpec((tm, LANES), lambda i, j: (i, 0)),
            pl.BlockSpec((tm, LANES), lambda i, j: (i, 0)),
            pl.BlockSpec((tm, LANES), lambda i, j: (i, 0)),
        ],
        out_specs=[
            pl.BlockSpec((tm, tn), lambda i, j: (i, jnp.minimum(j, 1))),
            pl.BlockSpec((tm, tn), lambda i, j: (i, 0)),
            pl.BlockSpec((tm, tn), lambda i, j: (0, 0)),
            pl.BlockSpec((tm, tn), lambda i, j: (i, jnp.clip(j - 3, 0, 7))),
            pl.BlockSpec((tm, LANES), lambda i, j: (i, 0)),
            pl.BlockSpec((16, tm), lambda i, j: (0, i)),
        ] + extra_specs,
        out_shape=[
            jax.ShapeDtypeStruct((rows, ATT_WIDTH), BF16),
            jax.ShapeDtypeStruct((rows, 2 * KV_WIDTH), BF16),
            jax.ShapeDtypeStruct((tm, 2 * KV_WIDTH), F32),
            jax.ShapeDtypeStruct((rows, 4 * MLSTM_WIDTH), BF16),
            jax.ShapeDtypeStruct((rows, LANES), F32),
            jax.ShapeDtypeStruct((16, rows), F32),
        ] + extra_shapes,
        scratch_shapes=[pltpu.VMEM((tm, D_MODEL), BF16)],
        compiler_params=_cparams(("arbitrary", "arbitrary")),
        name="inproj_s" if per_row else "inproj_p",
    )(x, mod, mod, norm_w, wq_t, w_in_t, wg, wgt, cos, sa, sb)


def _attn_block(sink_ref, q_ref, row0, kv2, allowed, store):
    w = WINDOW
    grp = ATT_GROUP
    rows = grp * w
    member = lax.broadcasted_iota(jnp.int32, (rows, 1), 0) // w
    low = lax.broadcasted_iota(jnp.int32, (2 * w, LANES), 1) < ATT_HEAD_DIM
    low_o = lax.broadcasted_iota(jnp.int32, (rows, LANES), 1) < ATT_HEAD_DIM
    key_row = lax.broadcasted_iota(jnp.int32, (4 * w, LANES), 0)
    key_lane = lax.broadcasted_iota(jnp.int32, (4 * w, LANES), 1)
    ones_bd = (((key_row < 2 * w) & (key_lane < ATT_HEAD_DIM))
               | ((key_row >= 2 * w) & (key_lane >= ATT_HEAD_DIM))).astype(BF16)
    zero = jnp.zeros((2 * w, LANES), BF16)
    for cp in range(2):
        k128 = kv2[:, LANES * cp:LANES * (cp + 1)]
        v128 = kv2[:, KV_WIDTH + LANES * cp:KV_WIDTH + LANES * (cp + 1)]
        kbd = jnp.concatenate([jnp.where(low, k128, zero), jnp.where(low, zero, k128)], axis=0)
        vbd = jnp.concatenate([jnp.where(low, v128, zero), jnp.where(low, zero, v128)], axis=0)
        v_aug = jnp.concatenate([vbd, ones_bd], axis=1)
        q4 = jnp.concatenate([q_ref[row0:row0 + w, 256 * r + LANES * cp:256 * r + LANES * (cp + 1)]
                              for r in range(grp)], axis=0)
        s = _dot_nt(q4, kbd)
        es, tails = [], []
        for half in range(2):
            sh = jnp.where(allowed, s[:, 2 * w * half:2 * w * (half + 1)], -jnp.inf)
            head0 = (2 * cp + half) * grp
            sink = jnp.full((rows, 1), sink_ref[head0], F32)
            for r in range(1, grp):
                sink = jnp.where(member == r, sink_ref[head0 + r], sink)
            m = jnp.maximum(jnp.max(sh, axis=-1, keepdims=True), sink)
            es.append(jnp.exp(sh - m).astype(BF16))
            tails.append(jnp.exp(sink - m))
        oa = _dot(jnp.concatenate(es, axis=1), v_aug)
        l = oa[:, LANES:] + jnp.where(low_o, tails[0], tails[1])
        o = (oa[:, :LANES] / l).astype(BF16)
        for r in range(grp):
            store(256 * r + LANES * cp, o[w * r:w * (r + 1)])


MIX_TM = 512


def _mix_p_kernel(sink_ref, q_ref, kvp_ref, kvc_ref, mq_ref, mk_ref, mv_ref, mo_ref, g_ref, gt_ref, brow_ref, bcol_ref,
                  mnw_ref, x_ref, g1_ref, sh_ref, sc_ref, nw_ref, wa_ref, wm_ref,
                  x1_ref, h2_ref, c_out, n_out, m_out, att_scr, mh_scr, c_scr, n_scr, m_scr):
    s = pl.program_id(0)
    w = WINDOW
    tiles = pl.num_programs(0) - 1
    slot = s % 2

    @pl.when(s == 0)
    def _():
        c_scr[...] = jnp.zeros_like(c_scr)
        n_scr[...] = jnp.zeros_like(n_scr)
        m_scr[...] = jnp.zeros_like(m_scr)

    def project():
        y = _dot(att_scr[1 - slot], wa_ref[...]) + _dot(mh_scr[1 - slot], wm_ref[...])
        x1 = x_ref[...] + g1_ref[0:1, :] * y
        x1_ref[...] = x1
        r = lax.rsqrt(jnp.mean(x1 * x1, axis=-1, keepdims=True) + EPS)
        h2_ref[...] = ((x1 * r * nw_ref[...]) * (1.0 + sc_ref[0:1, :]) + sh_ref[0:1, :]).astype(BF16)

    def mixers():
        rows = ATT_GROUP * w
        qi = lax.broadcasted_iota(jnp.int32, (rows, 2 * w), 0) % w
        kj = lax.broadcasted_iota(jnp.int32, (rows, 2 * w), 1)
        first_off = jnp.where(s > 0, 0, 4 * w)
        causal = (kj >= w) & (kj - w <= qi)
        for blk in range(MIX_TM // w):
            prev = kvp_ref[...] if blk == 0 else kvc_ref[w * (blk - 1):w * blk, :]
            kv2 = jnp.concatenate([prev, kvc_ref[w * blk:w * (blk + 1), :]], axis=0)
            allowed = ((kj < w) & (kj > qi + (first_off if blk == 0 else 0))) | causal

            def store(c0, val, blk=blk):
                att_scr[slot, w * blk:w * (blk + 1), c0:c0 + LANES] = val

            _attn_block(sink_ref, q_ref, w * blk, kv2, allowed, store)

        for ch in range(MIX_TM // MLSTM_CHUNK_P):
            r0 = MLSTM_CHUNK_P * ch

            def store_mh(cs, val, r0=r0):
                mh_scr[slot, r0:r0 + MLSTM_CHUNK_P, cs] = val

            _mlstm_chunk(mq_ref, mk_ref, mv_ref, mo_ref, g_ref, gt_ref, brow_ref, bcol_ref, mnw_ref,
                         c_scr, n_scr, m_scr, r0, store_mh)

    @pl.when(s == 0)
    def _():
        mixers()

    @pl.when((s > 0) & (s < tiles))
    def _():
        project()
        mixers()

    @pl.when(s == tiles)
    def _():
        project()
        c_out[...] = c_scr[...]
        n_out[...] = n_scr[...]
        m_out[...] = m_scr[...]


def _mix_p(sinks, q, kv, m_all, gates, gates_t, brow, bcol, mnw, x, mod, mod_row_block, norm_w, w_att, w_m):
    tm = MIX_TM
    tiles = SEQ // tm
    bpt = tm // WINDOW
    dh = MLSTM_HEAD_DIM
    att_tile = lambda s: jnp.minimum(s, tiles - 1)
    out_tile = lambda s: jnp.maximum(s - 1, 0)
    m_spec = lambda col: pl.BlockSpec((tm, MLSTM_WIDTH), lambda s: (att_tile(s), col))
    return pl.pallas_call(
        _mix_p_kernel,
        grid=(tiles + 1,),
        in_specs=[
            pl.BlockSpec(memory_space=pltpu.SMEM),
            pl.BlockSpec((tm, ATT_WIDTH), lambda s: (att_tile(s), 0)),
            pl.BlockSpec((WINDOW, 2 * KV_WIDTH), lambda s: (jnp.maximum(bpt * att_tile(s) - 1, 0), 0)),
            pl.BlockSpec((tm, 2 * KV_WIDTH), lambda s: (att_tile(s), 0)),
            m_spec(0), m_spec(1), m_spec(2), m_spec(3),
            pl.BlockSpec((tm, LANES), lambda s: (att_tile(s), 0)),
            pl.BlockSpec((16, tm), lambda s: (0, att_tile(s))),
            pl.BlockSpec((1, LANES), lambda s: (0, 0)),
            pl.BlockSpec((16, MLSTM_CHUNK_P), lambda s: (0, 0)),
            pl.BlockSpec((1, MLSTM_WIDTH), lambda s: (0, 0)),
            pl.BlockSpec((tm, D_MODEL), lambda s: (out_tile(s), 0)),
            pl.BlockSpec((8, D_MODEL), lambda s: (mod_row_block, 2)),
            pl.BlockSpec((8, D_MODEL), lambda s: (mod_row_block, 3)),
            pl.BlockSpec((8, D_MODEL), lambda s: (mod_row_block, 4)),
            pl.BlockSpec((1, D_MODEL), lambda s: (0, 0)),
            pl.BlockSpec((ATT_WIDTH, D_MODEL), lambda s: (0, 0)),
            pl.BlockSpec((MLSTM_WIDTH, D_MODEL), lambda s: (0, 0)),
        ],
        out_specs=[
            pl.BlockSpec((tm, D_MODEL), lambda s: (out_tile(s), 0)),
            pl.BlockSpec((tm, D_MODEL), lambda s: (out_tile(s), 0)),
            pl.BlockSpec((MLSTM_HEADS, dh, dh), lambda s: (0, 0, 0)),
            pl.BlockSpec((8, dh), lambda s: (0, 0)),
            pl.BlockSpec((8, LANES), lambda s: (0, 0)),
        ],
        out_shape=[
            jax.ShapeDtypeStruct((SEQ, D_MODEL), F32),
            jax.ShapeDtypeStruct((SEQ, D_MODEL), BF16),
            jax.ShapeDtypeStruct((MLSTM_HEADS, dh, dh), F32),
            jax.ShapeDtypeStruct((8, dh), F32),
            jax.ShapeDtypeStruct((8, LANES), F32),
        ],
        scratch_shapes=[
            pltpu.VMEM((2, tm, ATT_WIDTH), BF16),
            pltpu.VMEM((2, tm, MLSTM_WIDTH), BF16),
            pltpu.VMEM((MLSTM_HEADS, dh, dh), F32),
            pltpu.VMEM((8, dh), F32),
            pltpu.VMEM((8, LANES), F32),
        ],
        compiler_params=_cparams(("arbitrary",)),
        name="mix_p",
    )(sinks, q, kv, kv, m_all, m_all, m_all, m_all, gates, gates_t, brow, bcol, mnw, x, mod, mod, mod, norm_w,
      w_att, w_m)


ATT_S_BB = 8


def _attn_s_kernel(sink_ref, q_ref, kv32_ref, ck_ref, cv_ref, o_ref, ko_ref, vo_ref, q32_scr):
    t_new = DEC_SEQ
    w = WINDOW
    q32_scr[...] = q_ref[...].astype(F32)
    rows = 4 * 4 * 8
    row = lax.broadcasted_iota(jnp.int32, (rows, w), 0)
    slot = lax.broadcasted_iota(jnp.int32, (rows, w), 1)
    t_row = row % t_new
    second = (row % 8) >= t_new
    win_ok = (slot < w - t_new) | (slot - (w - t_new) <= t_row)
    old_ok = (slot >= 1) & (slot < t_new) & (slot > t_row)
    lane256 = lax.broadcasted_iota(jnp.int32, (32, 2 * LANES), 1)
    sink = sink_ref[...][:, 0:1]
    kv_new = jnp.concatenate([kv32_ref[...], jnp.zeros((w - ATT_S_BB * t_new, 2 * KV_WIDTH), F32)], axis=0)
    kv_t = kv_new.T
    new_slot = lax.broadcasted_iota(jnp.int32, (KV_WIDTH, w), 1) >= w - t_new
    for b in range(ATT_S_BB):
        cols = pltpu.roll(kv_t, w - t_new - t_new * b, 1)
        k_shift = pltpu.roll(ck_ref[b].reshape(KV_WIDTH, w), w - t_new, 1)
        v_shift = pltpu.roll(cv_ref[b].reshape(KV_WIDTH, w), w - t_new, 1)
        ko_ref[b] = jnp.where(new_slot, cols[:KV_WIDTH], k_shift).reshape(ATT_KV_HEADS, ATT_HEAD_DIM, w)
        vo_ref[b] = jnp.where(new_slot, cols[KV_WIDTH:], v_shift).reshape(ATT_KV_HEADS, ATT_HEAD_DIM, w)
    for pair in range(ATT_S_BB // 2):
        b0, b1 = 2 * pair, 2 * pair + 1
        q32 = jnp.concatenate([q32_scr[8 * pair:8 * (pair + 1), 256 * r:256 * (r + 1)] for r in range(ATT_GROUP)],
                              axis=0)
        qpad = jnp.concatenate(
            [jnp.where((lane256 // ATT_HEAD_DIM) == g, q32, 0.0) for g in range(ATT_KV_HEADS)], axis=0).astype(BF16)
        kw = [ko_ref[b].reshape(KV_WIDTH, w).astype(BF16) for b in (b0, b1)]
        vw = [vo_ref[b].reshape(KV_WIDTH, w).astype(BF16) for b in (b0, b1)]
        kc = [ck_ref[b].reshape(KV_WIDTH, w).astype(BF16) for b in (b0, b1)]
        vc = [cv_ref[b].reshape(KV_WIDTH, w).astype(BF16) for b in (b0, b1)]
        s_w = jnp.where(second, _dot(qpad, kw[1]), _dot(qpad, kw[0]))
        s_c = jnp.where(second, _dot(qpad, kc[1]), _dot(qpad, kc[0]))
        s_w = jnp.where(win_ok, s_w, -jnp.inf)
        s_c = jnp.where(old_ok, s_c, -jnp.inf)
        m = jnp.maximum(jnp.maximum(jnp.max(s_w, axis=-1, keepdims=True), jnp.max(s_c, axis=-1, keepdims=True)), sink)
        e_w = jnp.exp(s_w - m)
        e_c = jnp.exp(s_c - m)
        l = jnp.sum(e_w, axis=-1, keepdims=True) + jnp.sum(e_c, axis=-1, keepdims=True) + jnp.exp(sink - m)
        p_w = e_w / l
        p_c = e_c / l
        zero = jnp.zeros_like(p_w)
        o = (_dot_nt(jnp.where(second, zero, p_w).astype(BF16), vw[0])
             + _dot_nt(jnp.where(second, p_w, zero).astype(BF16), vw[1])
             + _dot_nt(jnp.where(second, zero, p_c).astype(BF16), vc[0])
             + _dot_nt(jnp.where(second, p_c, zero).astype(BF16), vc[1]))
        o32 = jnp.zeros((32, 2 * LANES), F32)
        for g in range(ATT_KV_HEADS):
            o32 = jnp.where((lane256 // ATT_HEAD_DIM) == g, o[32 * g:32 * (g + 1), :], o32)
        for r in range(ATT_GROUP):
            o_ref[8 * pair:8 * (pair + 1), 256 * r:256 * (r + 1)] = o32[8 * r:8 * (r + 1), :]


def _attn_s(sink_col, q, kv32, ck, cv):
    bb = ATT_S_BB
    rows = bb * DEC_SEQ
    cache_block = (bb, ATT_KV_HEADS, ATT_HEAD_DIM, WINDOW)
    cache_shape = (DEC_BATCH, ATT_KV_HEADS, ATT_HEAD_DIM, WINDOW)
    return pl.pallas_call(
        _attn_s_kernel,
        grid=(DEC_BATCH // bb,),
        in_specs=[
            pl.BlockSpec((128, LANES), lambda i: (0, 0)),
            pl.BlockSpec((rows, ATT_WIDTH), lambda i: (i, 0)),
            pl.BlockSpec((rows, 2 * KV_WIDTH), lambda i: (i, 0)),
            pl.BlockSpec(cache_block, lambda i: (i, 0, 0, 0)),
            pl.BlockSpec(cache_block, lambda i: (i, 0, 0, 0)),
        ],
        out_specs=[
            pl.BlockSpec((rows, ATT_WIDTH), lambda i: (i, 0)),
            pl.BlockSpec(cache_block, lambda i: (i, 0, 0, 0)),
            pl.BlockSpec(cache_block, lambda i: (i, 0, 0, 0)),
        ],
        out_shape=[
            jax.ShapeDtypeStruct((S_ROWS, ATT_WIDTH), F32),
            jax.ShapeDtypeStruct(cache_shape, F32),
            jax.ShapeDtypeStruct(cache_shape, F32),
        ],
        scratch_shapes=[pltpu.VMEM((rows, ATT_WIDTH), F32)],
        compiler_params=_cparams(("arbitrary",)),
        name="attn_s",
    )(sink_col, q, kv32, ck, cv)


def _head_norm_gate(h, nw, mo):
    hn = h * lax.rsqrt(jnp.mean(h * h, axis=-1, keepdims=True) + EPS) * nw
    return hn * _sigmoid(mo.astype(F32))


def _mlstm_chunk(q_ref, k_ref, v_ref, mo_ref, g_ref, gt_ref, brow_ref, bcol_ref, nw_ref, c_scr, n_scr, m_scr,
                 r0, store):
    L = MLSTM_CHUNK_P
    dh = MLSTM_HEAD_DIM
    rs = slice(r0, r0 + L)
    ti = lax.broadcasted_iota(jnp.int32, (L, L), 0)
    si = lax.broadcasted_iota(jnp.int32, (L, L), 1)
    causal = si <= ti
    tri = causal.astype(F32)
    tri_t = (ti <= si).astype(F32)
    gates = g_ref[rs, :] + brow_ref[...]
    gates_t = gt_ref[:, rs] + bcol_ref[...]
    b_col = jnp.dot(tri, _log_sigmoid(gates), precision=HI, preferred_element_type=F32)
    b_row = jnp.dot(_log_sigmoid(gates_t), tri_t, precision=HI, preferred_element_type=F32)
    for hd in range(MLSTM_HEADS):
        cs = slice(dh * hd, dh * (hd + 1))
        b_c = b_col[:, MLSTM_HEADS + hd:MLSTM_HEADS + hd + 1]
        li_c = gates[:, hd:hd + 1]
        b_r = b_row[MLSTM_HEADS + hd:MLSTM_HEADS + hd + 1, :]
        li_r = gates_t[hd:hd + 1, :]
        dm = jnp.where(causal, b_c - b_r + li_r, -jnp.inf)
        m_prev = m_scr[hd:hd + 1, 0:1]
        m_inter = b_c + m_prev
        m_t = jnp.maximum(m_inter, jnp.max(dm, axis=-1, keepdims=True))
        q = q_ref[rs, cs] * (dh ** -0.5)
        k = k_ref[rs, cs]
        v = v_ref[rs, cs]
        sm = _dot_nt(q, k) * jnp.exp(dm - m_t)
        a = jnp.exp(m_inter - m_t)
        c_old = c_scr[hd]
        n_old = n_scr[hd:hd + 1, :]
        num = a * _dot_nt(q, c_old.astype(BF16)) + _dot(sm.astype(BF16), v)
        qn = jnp.sum(q.astype(F32) * n_old, axis=-1, keepdims=True)
        den = a * qn + jnp.sum(sm, axis=-1, keepdims=True)
        h = num / jnp.maximum(jnp.abs(den), jnp.exp(-m_t))
        store(cs, _head_norm_gate(h, nw_ref[:, cs], mo_ref[rs, cs]).astype(BF16))
        m_new = m_t[L - 1:L, :]
        b_last = b_c[L - 1:L, :]
        g = jnp.exp(b_last - b_c + li_c - m_new)
        decay = jnp.exp(b_last + m_prev - m_new)
        gv = (g * v.astype(F32)).astype(BF16)
        c_scr[hd] = decay * c_old + _dot_tn(gv, k)
        n_scr[hd:hd + 1, :] = decay * n_old + jnp.sum(g * k.astype(F32), axis=0, keepdims=True)
        m_scr[hd:hd + 1, :] = jnp.broadcast_to(m_new, (1, LANES))


MLSTM_S_BB = 4


def _mlstm_s_kernel(q_ref, k_ref, v_ref, mo_ref, g_ref, nrep_ref, mrep_ref, brow_ref, nw_ref, c_ref,
                    mh_ref, c_out, nrow_ref, mrow_ref):
    T = DEC_SEQ
    R = MLSTM_S_BB * T
    H = MLSTM_HEADS
    dh = MLSTM_HEAD_DIM

    def shift(x, d):
        return pltpu.roll(x, d, 0)

    lanes = lax.broadcasted_iota(jnp.int32, (R, LANES), 1)
    tmod = lax.broadcasted_iota(jnp.int32, (R, LANES), 0) % T
    tmod_w = lax.broadcasted_iota(jnp.int32, (R, MLSTM_WIDTH), 0) % T
    head_ok = lanes < H
    gates = g_ref[...] + brow_ref[...]
    li = jnp.where(head_ok, gates, 0.0)
    lf = jnp.where(head_ok, pltpu.roll(_log_sigmoid(gates), LANES - H, 1), 0.0)
    bcum = lf
    for d in range(1, T):
        bcum = bcum + jnp.where(tmod >= d, shift(lf, d), 0.0)
    m0 = mrep_ref[...]
    m_inter = bcum + m0
    dms = [li] + [jnp.where(tmod >= d, bcum - shift(bcum, d) + shift(li, d), -jnp.inf) for d in range(1, T)]
    m_t = m_inter
    for dm in dms:
        m_t = jnp.maximum(m_t, dm)
    a = jnp.exp(m_inter - m_t)
    ws = [jnp.exp(dm - m_t) for dm in dms]

    q_bf = q_ref[...] * (dh ** -0.5)
    q = q_bf.astype(F32)
    k = k_ref[...].astype(F32)
    v = v_ref[...].astype(F32)
    seg = (lax.broadcasted_iota(jnp.int32, (MLSTM_WIDTH, LANES), 0) // dh
           == lax.broadcasted_iota(jnp.int32, (MLSTM_WIDTH, LANES), 1)).astype(F32)
    ex = (lax.broadcasted_iota(jnp.int32, (LANES, MLSTM_WIDTH), 1) // dh
          == lax.broadcasted_iota(jnp.int32, (LANES, MLSTM_WIDTH), 0)).astype(F32)

    def segsum(x):
        return jnp.dot(x, seg, precision=HI, preferred_element_type=F32)

    def expand(x):
        return jnp.dot(x, ex, precision=HI, preferred_element_type=F32)

    ks = [k] + [shift(k, d) for d in range(1, T)]
    vs = [v] + [shift(v, d) for d in range(1, T)]
    sms = [segsum(q * ks[d]) * ws[d] for d in range(T)]
    den = a * segsum(q * nrep_ref[...])
    for sm in sms:
        den = den + sm
    inv = 1.0 / jnp.maximum(jnp.abs(den), jnp.exp(-m_t))

    def last(x):
        out = jnp.zeros_like(x)
        for jj in range(T):
            out = jnp.where(tmod == T - 1 - jj, x if jj == 0 else pltpu.roll(x, R - jj, 0), out)
        return out

    m_new = last(m_t)
    b_last = last(bcum)
    g = jnp.where(head_ok, jnp.exp(b_last - bcum + li - m_new), 0.0)
    decay = jnp.where(head_ok, jnp.exp(b_last + m0 - m_new), 0.0)
    a_f = expand(a * inv)
    w_f = [expand(sm * inv) for sm in sms]
    g_f = expand(g)
    d_f = expand(decay)
    gv = (g_f * v).astype(BF16)
    rowb = lax.broadcasted_iota(jnp.int32, (R, dh), 0) // T
    for hd in range(H):
        cs = slice(dh * hd, dh * (hd + 1))
        qh = q_bf[:, cs]
        kh = k_ref[:, cs]
        gvh = gv[:, cs]
        qc = jnp.zeros((R, dh), F32)
        for bb in range(MLSTM_S_BB):
            c_old = c_ref[bb, hd]
            qc = jnp.where(rowb == bb, _dot_nt(qh, c_old.astype(BF16)), qc)
            upd = _dot_tn(jnp.where(rowb == bb, gvh, jnp.zeros_like(gvh)), kh)
            c_out[bb, hd] = d_f[T * bb:T * bb + 1, cs] * c_old + upd
        h = a_f[:, cs] * qc
        for d in range(T):
            h = h + w_f[d][:, cs] * vs[d][:, cs]
        mh_ref[:, cs] = _head_norm_gate(h, nw_ref[:, cs], mo_ref[:, cs]).astype(BF16)
    gk = g_f * k
    nsum = gk
    for d in range(1, T):
        nsum = nsum + jnp.where(tmod_w >= d, shift(gk, d), 0.0)
    nrow_ref[...] = d_f * nrep_ref[...] + nsum
    mrow_ref[...] = m_t


def _mlstm_s(m_all, gates, n_rep, m_rep, brow, nw, state_c):
    bb = MLSTM_S_BB
    R = bb * DEC_SEQ
    dh = MLSTM_HEAD_DIM
    H = MLSTM_HEADS
    return pl.pallas_call(
        _mlstm_s_kernel,
        grid=(DEC_BATCH // bb,),
        in_specs=[
            pl.BlockSpec((R, MLSTM_WIDTH), lambda i: (i, 0)),
            pl.BlockSpec((R, MLSTM_WIDTH), lambda i: (i, 1)),
            pl.BlockSpec((R, MLSTM_WIDTH), lambda i: (i, 2)),
            pl.BlockSpec((R, MLSTM_WIDTH), lambda i: (i, 3)),
            pl.BlockSpec((R, LANES), lambda i: (i, 0)),
            pl.BlockSpec((R, MLSTM_WIDTH), lambda i: (i, 0)),
            pl.BlockSpec((R, LANES), lambda i: (i, 0)),
            pl.BlockSpec((1, LANES), lambda i: (0, 0)),
            pl.BlockSpec((1, MLSTM_WIDTH), lambda i: (0, 0)),
            pl.BlockSpec((bb, H, dh, dh), lambda i: (i, 0, 0, 0)),
        ],
        out_specs=[
            pl.BlockSpec((R, MLSTM_WIDTH), lambda i: (i, 0)),
            pl.BlockSpec((bb, H, dh, dh), lambda i: (i, 0, 0, 0)),
            pl.BlockSpec((R, MLSTM_WIDTH), lambda i: (i, 0)),
            pl.BlockSpec((R, LANES), lambda i: (i, 0)),
        ],
        out_shape=[
            jax.ShapeDtypeStruct((S_ROWS, MLSTM_WIDTH), BF16),
            jax.ShapeDtypeStruct((DEC_BATCH, H, dh, dh), F32),
            jax.ShapeDtypeStruct((S_ROWS, MLSTM_WIDTH), F32),
            jax.ShapeDtypeStruct((S_ROWS, LANES), F32),
        ],
        compiler_params=_cparams(("arbitrary",)),
        name="mlstm_s",
    )(m_all, m_all, m_all, m_all, gates, n_rep, m_rep, brow, nw, state_c)


def _outproj_kernel(att_ref, mh_ref, x_ref, g1_ref, sh_ref, sc_ref, nw_ref, wa_ref, wm_ref, x1_ref, h2_ref, *,
                    per_row):
    y = _dot(att_ref[...].astype(BF16), wa_ref[...]) + _dot(mh_ref[...], wm_ref[...])
    x1 = x_ref[...] + _mod_row(g1_ref, per_row) * y
    x1_ref[...] = x1
    r = lax.rsqrt(jnp.mean(x1 * x1, axis=-1, keepdims=True) + EPS)
    h2 = (x1 * r * nw_ref[...]) * (1.0 + _mod_row(sc_ref, per_row)) + _mod_row(sh_ref, per_row)
    h2_ref[...] = h2.astype(BF16)


def _outproj(att, mh, x, mod, mod_row_block, norm_w, w_att, w_m, *, tm, per_row):
    rows = x.shape[0]
    mod_rows = tm if per_row else 8
    mod_idx = (lambda i: i) if per_row else (lambda i: mod_row_block)
    kern = functools.partial(_outproj_kernel, per_row=per_row)
    return pl.pallas_call(
        kern,
        grid=(rows // tm,),
        in_specs=[
            pl.BlockSpec((tm, ATT_WIDTH), lambda i: (i, 0)),
            pl.BlockSpec((tm, MLSTM_WIDTH), lambda i: (i, 0)),
            pl.BlockSpec((tm, D_MODEL), lambda i: (i, 0)),
            pl.BlockSpec((mod_rows, D_MODEL), lambda i: (mod_idx(i), 2)),
            pl.BlockSpec((mod_rows, D_MODEL), lambda i: (mod_idx(i), 3)),
            pl.BlockSpec((mod_rows, D_MODEL), lambda i: (mod_idx(i), 4)),
            pl.BlockSpec((1, D_MODEL), lambda i: (0, 0)),
            pl.BlockSpec((ATT_WIDTH, D_MODEL), lambda i: (0, 0)),
            pl.BlockSpec((MLSTM_WIDTH, D_MODEL), lambda i: (0, 0)),
        ],
        out_specs=[
            pl.BlockSpec((tm, D_MODEL), lambda i: (i, 0)),
            pl.BlockSpec((tm, D_MODEL), lambda i: (i, 0)),
        ],
        out_shape=[
            jax.ShapeDtypeStruct((rows, D_MODEL), F32),
            jax.ShapeDtypeStruct((rows, D_MODEL), BF16),
        ],
        compiler_params=_cparams(("arbitrary",)),
        name="outproj_s" if per_row else "outproj_p",
    )(att, mh, x, mod, mod, mod, norm_w, w_att, w_m)


def _ffn_a_kernel(hp_ref, hs_ref, wg_ref, wu_ref, ap_ref, as_ref, wg_scr, wu_scr):
    i = pl.program_id(1)

    def swiglu(h_ref, a_ref):
        h = h_ref[...]
        g = _dot(h, wg_scr[...])
        u = _dot(h, wu_scr[...])
        a_ref[...] = (g * _sigmoid(g) * u).astype(BF16)

    @pl.when(i == 0)
    def _():
        wg_scr[...] = wg_ref[...].astype(BF16)
        wu_scr[...] = wu_ref[...].astype(BF16)
        swiglu(hs_ref, as_ref)

    @pl.when(i > 0)
    def _():
        swiglu(hp_ref, ap_ref)


def _ffn_a(h2_p, h2_s, w_gate, w_up, *, tm):
    rows_p, rows_s = h2_p.shape[0], h2_s.shape[0]
    tn = 512
    ni = rows_p // tm
    tile = lambda i: jnp.maximum(i - 1, 0)
    return pl.pallas_call(
        _ffn_a_kernel,
        grid=(D_FF // tn, ni + 1),
        in_specs=[
            pl.BlockSpec((tm, D_MODEL), lambda j, i: (tile(i), 0)),
            pl.BlockSpec((rows_s, D_MODEL), lambda j, i: (0, 0)),
            pl.BlockSpec((D_MODEL, tn), lambda j, i: (0, j)),
            pl.BlockSpec((D_MODEL, tn), lambda j, i: (0, j)),
        ],
        out_specs=[
            pl.BlockSpec((tm, tn), lambda j, i: (tile(i), j)),
            pl.BlockSpec((rows_s, tn), lambda j, i: (0, j)),
        ],
        out_shape=[
            jax.ShapeDtypeStruct((rows_p, D_FF), BF16),
            jax.ShapeDtypeStruct((rows_s, D_FF), BF16),
        ],
        scratch_shapes=[pltpu.VMEM((D_MODEL, tn), BF16), pltpu.VMEM((D_MODEL, tn), BF16)],
        compiler_params=_cparams(("arbitrary", "arbitrary")),
        name="ffn_a",
    )(h2_p, h2_s, w_gate, w_up)


def _ffn_b_kernel(a_ref, wd_ref, x1_ref, g2_ref, fw_ref, y_ref, *rest, per_row, emit_w):
    kk = pl.program_id(1)

    def weight_tile():
        wt = wd_ref[...].astype(BF16)
        if emit_w:
            rest[0][...] = wt
        return wt

    @pl.when(kk == 0)
    def _():
        y_ref[...] = _dot(a_ref[...], weight_tile())

    @pl.when(kk > 0)
    def _():
        y_ref[...] += _dot(a_ref[...], weight_tile())

    @pl.when(kk == pl.num_programs(1) - 1)
    def _():
        x2 = x1_ref[...] + _mod_row(g2_ref, per_row) * y_ref[...]
        y_ref[...] = x2 * lax.rsqrt(jnp.mean(x2 * x2, axis=-1, keepdims=True) + EPS) * fw_ref[...]


def _ffn_b(a, w_down, x1, mod, mod_row_block, final_w, *, tm, per_row, emit_w):
    rows = a.shape[0]
    tk = 512
    mod_rows = tm if per_row else 8
    mod_idx = (lambda i: i) if per_row else (lambda i: mod_row_block)
    kern = functools.partial(_ffn_b_kernel, per_row=per_row, emit_w=emit_w)
    w_spec = pl.BlockSpec((tk, D_MODEL), lambda i, k: (k, 0))
    y_spec = pl.BlockSpec((tm, D_MODEL), lambda i, k: (i, 0))
    y_shape = jax.ShapeDtypeStruct((rows, D_MODEL), F32)
    return pl.pallas_call(
        kern,
        grid=(rows // tm, D_FF // tk),
        in_specs=[
            pl.BlockSpec((tm, tk), lambda i, k: (i, k)),
            w_spec,
            pl.BlockSpec((tm, D_MODEL), lambda i, k: (i, 0)),
            pl.BlockSpec((mod_rows, D_MODEL), lambda i, k: (mod_idx(i), 5)),
            pl.BlockSpec((1, D_MODEL), lambda i, k: (0, 0)),
        ],
        out_specs=[y_spec, w_spec] if emit_w else y_spec,
        out_shape=[y_shape, jax.ShapeDtypeStruct((D_FF, D_MODEL), BF16)] if emit_w else y_shape,
        compiler_params=_cparams(("arbitrary", "arbitrary")),
        name="ffn_b_s" if per_row else "ffn_b_p",
    )(a, w_down, x1, mod, final_w)


def _rope_tables(pos):
    half = ROPE_DIM // 2
    inv = np.float32(ROPE_THETA) ** (-np.arange(0, ROPE_DIM, 2, dtype=np.float32) / np.float32(ROPE_DIM))
    d = np.arange(LANES) % ATT_HEAD_DIM
    ang = pos.astype(np.float32)[:, None] * inv[d % half][None, :].astype(np.float32)
    cos, sin = np.cos(ang), np.sin(ang)
    d = d[None, :]
    tables = (np.where(d < ROPE_DIM, cos, 1.0), np.where(d < half, -sin, 0.0),
              np.where((d >= half) & (d < ROPE_DIM), sin, 0.0))
    return tuple(jnp.asarray(t.astype(np.float32)) for t in tables)


def kernel(x_prompt, x_sample, cache_k_win, cache_v_win, state_C, state_n, state_m, c_prompt, c_sample,
           norm1_w, norm2_w, final_norm_w, w_ada, b_ada, w_in, b_ig, b_fg, attn_sinks, mh_norm_w,
           w_out, w_gate, w_up, w_down):
    assert w_in.shape[0] == 1, "single-layer trunk"
    T = DEC_SEQ
    xp = x_prompt[0]
    xs = x_sample.reshape(S_ROWS, D_MODEL)

    c_all = jnp.concatenate([jnp.repeat(c_sample, T, axis=0), c_prompt, jnp.zeros((15, D_MODEL), F32)], axis=0)
    mod = _ada(c_all, w_ada[0], b_ada)
    prompt_mod_block = S_ROWS // 8

    w_in_t = jnp.transpose(w_in[0])
    wq_t = (w_in_t[:ATT_WIDTH].reshape(ATT_KV_HEADS, ATT_GROUP, ATT_HEAD_DIM, D_MODEL)
            .transpose(1, 0, 2, 3).reshape(ATT_WIDTH, D_MODEL).astype(BF16))
    w_gates_t = w_in_t[MAIN_WIDTH:]
    wg = jnp.pad(w_gates_t.T, ((0, 0), (0, LANES - 2 * MLSTM_HEADS))).astype(BF16)
    wgt = jnp.pad(w_gates_t, ((0, 16 - 2 * MLSTM_HEADS), (0, 0))).astype(BF16)
    w_out_att = (w_out[0, :ATT_WIDTH].reshape(ATT_KV_HEADS, ATT_GROUP, ATT_HEAD_DIM, D_MODEL)
                 .transpose(1, 0, 2, 3).reshape(ATT_WIDTH, D_MODEL).astype(BF16))
    w_out_m = w_out[0, ATT_WIDTH:].astype(BF16)
    n1 = norm1_w.reshape(1, D_MODEL)
    n2 = norm2_w.reshape(1, D_MODEL)
    fw = final_norm_w.reshape(1, D_MODEL)
    nw = mh_norm_w.reshape(1, MLSTM_WIDTH)
    gate_bias = jnp.concatenate([b_ig[0], b_fg[0]])
    brow = jnp.pad(gate_bias, (0, LANES - 2 * MLSTM_HEADS)).reshape(1, LANES)
    bcol = jnp.broadcast_to(jnp.pad(gate_bias, (0, 16 - 2 * MLSTM_HEADS))[:, None], (16, MLSTM_CHUNK_P))

    rope_p = _rope_tables(np.arange(SEQ))
    rope_s = _rope_tables(np.tile(PAST_LEN + np.arange(T), DEC_BATCH))

    tm_p = 1024
    q_s, _, kv32_s, m_s, g_s, _, w_in_bf = _inproj(xs, mod, 0, n1, wq_t, w_in_t, wg, wgt, *rope_s,
                                                   tm=S_ROWS, per_row=True, emit_w=True)
    q_p, kv_p, kv32_p, m_p, g_p, gt_p = _inproj(xp, mod, prompt_mod_block, n1, wq_t, w_in_bf, wg, wgt, *rope_p,
                                                tm=tm_p, per_row=False, emit_w=False)

    sinks = attn_sinks[0]
    sink_col = jnp.broadcast_to(sinks.reshape(ATT_HEADS, 1, 1), (ATT_HEADS, 8, LANES)).reshape(128, LANES)
    ck = jnp.transpose(cache_k_win[0], (0, 2, 3, 1))
    cv = jnp.transpose(cache_v_win[0], (0, 2, 3, 1))
    att_s, kwin_s, vwin_s = _attn_s(sink_col, q_s, kv32_s, ck, cv)

    n_rep = jnp.repeat(state_n[0].reshape(DEC_BATCH, MLSTM_WIDTH), T, axis=0)
    m_rep = jnp.pad(jnp.repeat(state_m[0], T, axis=0), ((0, 0), (0, LANES - MLSTM_HEADS)))
    mh_s, c_s, nrow_s, mrow_s = _mlstm_s(m_s, g_s, n_rep, m_rep, brow, nw, state_C[0])

    x1_p, h2_p, c_p, n_p, mm_p = _mix_p(sinks, q_p, kv_p, m_p, g_p, gt_p, brow, bcol, nw, xp, mod, prompt_mod_block,
                                        n2, w_out_att, w_out_m)
    x1_s, h2_s = _outproj(att_s, mh_s, xs, mod, 0, n2, w_out_att, w_out_m, tm=S_ROWS, per_row=True)
    a_p, a_s = _ffn_a(h2_p, h2_s, w_gate[0], w_up[0], tm=tm_p)
    y_s, w_down_bf = _ffn_b(a_s, w_down[0], x1_s, mod, 0, fw, tm=S_ROWS, per_row=True, emit_w=True)
    y_p = _ffn_b(a_p, w_down_bf, x1_p, mod, prompt_mod_block, fw, tm=tm_p, per_row=False, emit_w=False)

    kv_shape = (1, 1, WINDOW, ATT_KV_HEADS, ATT_HEAD_DIM)
    kv_last = kv32_p[tm_p - WINDOW:]
    dh = MLSTM_HEAD_DIM
    return (
        y_p.reshape(1, SEQ, D_MODEL),
        y_s.reshape(DEC_BATCH, T, D_MODEL),
        kv_last[:, :KV_WIDTH].reshape(kv_shape),
        kv_last[:, KV_WIDTH:].reshape(kv_shape),
        c_p.reshape(1, 1, MLSTM_HEADS, dh, dh),
        n_p[:MLSTM_HEADS].reshape(1, 1, MLSTM_HEADS, dh),
        mm_p[:MLSTM_HEADS, 0].reshape(1, 1, MLSTM_HEADS),
        jnp.transpose(kwin_s, (0, 3, 1, 2))[None],
        jnp.transpose(vwin_s, (0, 3, 1, 2))[None],
        c_s.reshape(1, DEC_BATCH, MLSTM_HEADS, dh, dh),
        nrow_s[T - 1::T].reshape(1, DEC_BATCH, MLSTM_HEADS, dh),
        mrow_s[T - 1::T, :MLSTM_HEADS].reshape(1, DEC_BATCH, MLSTM_HEADS),
    )
```

```python
import functools

import jax
import jax.numpy as jnp
import numpy as np
from jax import lax
from jax.experimental import pallas as pl
from jax.experimental.pallas import tpu as pltpu

F32 = jnp.float32
BF16 = jnp.bfloat16

D_MODEL = 2048
SEQ = 8192
DEC_BATCH = 128
DEC_SEQ = 4
S_ROWS = DEC_BATCH * DEC_SEQ
PAST_LEN = 16384
ATT_HEADS = 16
ATT_KV_HEADS = 4
ATT_GROUP = 4
ATT_HEAD_DIM = 64
WINDOW = 128
ROPE_THETA = 500000.0
ROPE_DIM = 16
MLSTM_HEADS = 4
MLSTM_HEAD_DIM = 256
ATT_WIDTH = 1024
KV_WIDTH = 256
MLSTM_WIDTH = 1024
MAIN_WIDTH = ATT_WIDTH + 2 * KV_WIDTH + 4 * MLSTM_WIDTH
D_FF = 5632
N_MOD = 6
EPS = 1e-6

LANES = 128
MLSTM_CHUNK_P = 256
VMEM_LIMIT = 56 * 1024 * 1024

NT_DIMS = (((1,), (1,)), ((), ()))
TN_DIMS = (((0,), (0,)), ((), ()))
HI = lax.Precision.HIGHEST


def _cparams(sem):
    return pltpu.CompilerParams(dimension_semantics=sem, vmem_limit_bytes=VMEM_LIMIT)


def _dot(a, b):
    return jnp.dot(a, b, preferred_element_type=F32)


def _dot_nt(a, b):
    return lax.dot_general(a, b, NT_DIMS, preferred_element_type=F32)


def _dot_tn(a, b):
    return lax.dot_general(a, b, TN_DIMS, preferred_element_type=F32)


def _sigmoid(x):
    return 1.0 / (1.0 + jnp.exp(-x))


def _log_sigmoid(x):
    return jnp.minimum(x, 0.0) - jnp.log(1.0 + jnp.exp(-jnp.abs(x)))


def _mod_row(ref, per_row):
    return ref[...] if per_row else ref[0:1, :]


def _ada_kernel(c_ref, w_ref, b_ref, o_ref, s_scr):
    @pl.when(pl.program_id(0) == 0)
    def _():
        c = c_ref[...]
        s_scr[...] = (c * _sigmoid(c)).astype(BF16)

    o_ref[...] = _dot(s_scr[...], w_ref[...].astype(BF16)) + b_ref[...]


def _ada(c_all, w_ada, b_ada):
    m = c_all.shape[0]
    n = w_ada.shape[1]
    tn = 1024
    return pl.pallas_call(
        _ada_kernel,
        grid=(n // tn,),
        in_specs=[
            pl.BlockSpec((m, D_MODEL), lambda j: (0, 0)),
            pl.BlockSpec((D_MODEL, tn), lambda j: (0, j)),
            pl.BlockSpec((1, tn), lambda j: (0, j)),
        ],
        out_specs=pl.BlockSpec((m, tn), lambda j: (0, j)),
        out_shape=jax.ShapeDtypeStruct((m, n), F32),
        scratch_shapes=[pltpu.VMEM((m, D_MODEL), BF16)],
        compiler_params=_cparams(("arbitrary",)),
        name="ada",
    )(c_all, w_ada, b_ada)


def _rope_store(acc, cos, sa, sb, out_ref, ncols, scale):
    for c in range(ncols // LANES):
        xc = acc[:, LANES * c:LANES * (c + 1)]
        rot = xc * cos + pltpu.roll(xc, LANES - 8, 1) * sa + pltpu.roll(xc, 8, 1) * sb
        if scale != 1.0:
            rot = rot * scale
        out_ref[:, LANES * c:LANES * (c + 1)] = rot.astype(out_ref.dtype)


def _inproj_kernel(x_ref, sh_ref, sc_ref, nw_ref, wq_ref, win_ref, wg_ref, cos_ref, sa_ref, sb_ref,
                   q_ref, kv_ref, kv32_ref, m_ref, g_ref, gt_ref, *rest, per_row, emit_w):
    h_scr = rest[-1]
    j = pl.program_id(1)

    def weight_tile():
        wt = win_ref[...].astype(BF16)
        if emit_w:
            rest[0][...] = wt
        return wt

    @pl.when(j == 0)
    def _():
        x = x_ref[...]
        r = lax.rsqrt(jnp.mean(x * x, axis=-1, keepdims=True) + EPS)
        gain = nw_ref[...] * (1.0 + _mod_row(sc_ref, per_row))
        h = (x * r * gain + _mod_row(sh_ref, per_row)).astype(BF16)
        h_scr[...] = h
        g = _dot(h, wg_ref[...])
        g_ref[...] = g
        gt_ref[...] = g.T[0:16, :]

    @pl.when(j < 2)
    def _():
        acc = _dot_nt(h_scr[...], wq_ref[...])
        _rope_store(acc, cos_ref[...], sa_ref[...], sb_ref[...], q_ref, 512, ATT_HEAD_DIM ** -0.5)

    @pl.when(j == 2)
    def _():
        acc = _dot_nt(h_scr[...], weight_tile())
        _rope_store(acc, cos_ref[...], sa_ref[...], sb_ref[...], kv32_ref, KV_WIDTH, 1.0)
        kv32_ref[:, KV_WIDTH:] = acc[:, KV_WIDTH:]
        kv_ref[...] = kv32_ref[...].astype(BF16)

    @pl.when(j > 2)
    def _():
        m_ref[...] = _dot_nt(h_scr[...], weight_tile()).astype(BF16)


def _inproj(x, mod, mod_row_block, norm_w, wq_t, w_in_t, wg, cos, sa, sb, *, tm, per_row, emit_w):
    rows = x.shape[0]
    tn = 512
    nj = MAIN_WIDTH // tn
    mod_rows = tm if per_row else 8
    mod_idx = (lambda i: i) if per_row else (lambda i: mod_row_block)
    kern = functools.partial(_inproj_kernel, per_row=per_row, emit_w=emit_w)
    copy_spec = pl.BlockSpec((tn, D_MODEL), lambda i, j: (jnp.maximum(j, 2) - 2, 0))
    w_spec = pl.BlockSpec((tn, D_MODEL), lambda i, j: (jnp.maximum(j, 2), 0)) if emit_w else copy_spec
    extra_specs = [copy_spec] if emit_w else []
    extra_shapes = [jax.ShapeDtypeStruct((MAIN_WIDTH - ATT_WIDTH, D_MODEL), BF16)] if emit_w else []
    return pl.pallas_call(
        kern,
        grid=(rows // tm, nj),
        in_specs=[
            pl.BlockSpec((tm, D_MODEL), lambda i, j: (jnp.minimum(i + jnp.minimum(j, 1), rows // tm - 1), 0)),
            pl.BlockSpec((mod_rows, D_MODEL), lambda i, j: (mod_idx(i), 0)),
            pl.BlockSpec((mod_rows, D_MODEL), lambda i, j: (mod_idx(i), 1)),
            pl.BlockSpec((1, D_MODEL), lambda i, j: (0, 0)),
            pl.BlockSpec((tn, D_MODEL), lambda i, j: (jnp.minimum(j, 1), 0)),
            w_spec,
            pl.BlockSpec((D_MODEL, LANES), lambda i, j: (0, 0)),
            pl.BlockSpec((tm, LANES), lambda i, j: (i, 0)),
            pl.BlockSpec((tm, LANES), lambda i, j: (i, 0)),
            pl.BlockSpec((tm, LANES), lambda i, j: (i, 0)),
        ],
        out_specs=[
            pl.BlockSpec((tm, tn), lambda i, j: (i, jnp.minimum(j, 1))),
            pl.BlockSpec((tm, tn), lambda i, j: (i, 0)),
            pl.BlockSpec((tm, tn), lambda i, j: (0, 0)),
            pl.BlockSpec((tm, tn), lambda i, j: (i, jnp.clip(j - 3, 0, 7))),
            pl.BlockSpec((tm, LANES), lambda i, j: (i, 0)),
            pl.BlockSpec((16, tm), lambda i, j: (0, i)),
        ] + extra_specs,
        out_shape=[
            jax.ShapeDtypeStruct((rows, ATT_WIDTH), BF16),
            jax.ShapeDtypeStruct((rows, 2 * KV_WIDTH), BF16),
            jax.ShapeDtypeStruct((tm, 2 * KV_WIDTH), F32),
            jax.ShapeDtypeStruct((rows, 4 * MLSTM_WIDTH), BF16),
            jax.ShapeDtypeStruct((rows, LANES), F32),
            jax.ShapeDtypeStruct((16, rows), F32),
        ] + extra_shapes,
        scratch_shapes=[pltpu.VMEM((tm, D_MODEL), BF16)],
        compiler_params=_cparams(("arbitrary", "arbitrary")),
        name="inproj_s" if per_row else "inproj_p",
    )(x, mod, mod, norm_w, wq_t, w_in_t, wg, cos, sa, sb)


def _attn_block(sink_ref, q_ref, row0, kv2, allowed, store):
    w = WINDOW
    grp = ATT_GROUP
    rows = grp * w
    member = lax.broadcasted_iota(jnp.int32, (rows, 1), 0) // w
    low = lax.broadcasted_iota(jnp.int32, (2 * w, LANES), 1) < ATT_HEAD_DIM
    low_o = lax.broadcasted_iota(jnp.int32, (rows, LANES), 1) < ATT_HEAD_DIM
    key_row = lax.broadcasted_iota(jnp.int32, (4 * w, LANES), 0)
    key_lane = lax.broadcasted_iota(jnp.int32, (4 * w, LANES), 1)
    ones_bd = (((key_row < 2 * w) & (key_lane < ATT_HEAD_DIM))
               | ((key_row >= 2 * w) & (key_lane >= ATT_HEAD_DIM))).astype(BF16)
    zero = jnp.zeros((2 * w, LANES), BF16)
    for cp in range(2):
        k128 = kv2[:, LANES * cp:LANES * (cp + 1)]
        v128 = kv2[:, KV_WIDTH + LANES * cp:KV_WIDTH + LANES * (cp + 1)]
        kbd = jnp.concatenate([jnp.where(low, k128, zero), jnp.where(low, zero, k128)], axis=0)
        vbd = jnp.concatenate([jnp.where(low, v128, zero), jnp.where(low, zero, v128)], axis=0)
        v_aug = jnp.concatenate([vbd, ones_bd], axis=1)
        q4 = jnp.concatenate([q_ref[row0:row0 + w, 256 * r + LANES * cp:256 * r + LANES * (cp + 1)]
                              for r in range(grp)], axis=0)
        s = _dot_nt(q4, kbd)
        es, tails = [], []
        for half in range(2):
            sh = jnp.where(allowed, s[:, 2 * w * half:2 * w * (half + 1)], -jnp.inf)
            head0 = (2 * cp + half) * grp
            sink = jnp.full((rows, 1), sink_ref[head0], F32)
            for r in range(1, grp):
                sink = jnp.where(member == r, sink_ref[head0 + r], sink)
            m = jnp.maximum(jnp.max(sh, axis=-1, keepdims=True), sink)
            es.append(jnp.exp(sh - m).astype(BF16))
            tails.append(jnp.exp(sink - m))
        oa = _dot(jnp.concatenate(es, axis=1), v_aug)
        l = oa[:, LANES:] + jnp.where(low_o, tails[0], tails[1])
        o = (oa[:, :LANES] / l).astype(BF16)
        for r in range(grp):
            store(256 * r + LANES * cp, o[w * r:w * (r + 1)])


MIX_TM = 512


def _mix_p_kernel(sink_ref, q_ref, kvp_ref, kvc_ref, mq_ref, mk_ref, mv_ref, mo_ref, g_ref, gt_ref, brow_ref, bcol_ref,
                  mnw_ref, x_ref, g1_ref, sh_ref, sc_ref, nw_ref, wa_ref, wm_ref,
                  x1_ref, h2_ref, c_out, n_out, m_out, att_scr, mh_scr, c_scr, n_scr, m_scr):
    s = pl.program_id(0)
    w = WINDOW
    tiles = pl.num_programs(0) - 1
    slot = s % 2

    @pl.when(s == 0)
    def _():
        c_scr[...] = jnp.zeros_like(c_scr)
        n_scr[...] = jnp.zeros_like(n_scr)
        m_scr[...] = jnp.zeros_like(m_scr)

    def project():
        y = _dot(att_scr[1 - slot], wa_ref[...]) + _dot(mh_scr[1 - slot], wm_ref[...])
        x1 = x_ref[...] + g1_ref[0:1, :] * y
        x1_ref[...] = x1
        r = lax.rsqrt(jnp.mean(x1 * x1, axis=-1, keepdims=True) + EPS)
        gain = nw_ref[...] * (1.0 + sc_ref[0:1, :])
        h2_ref[...] = (x1 * r * gain + sh_ref[0:1, :]).astype(BF16)

    def mixers():
        rows = ATT_GROUP * w
        qi = lax.broadcasted_iota(jnp.int32, (rows, 2 * w), 0) % w
        kj = lax.broadcasted_iota(jnp.int32, (rows, 2 * w), 1)
        first_off = jnp.where(s > 0, 0, 4 * w)
        causal = (kj >= w) & (kj - w <= qi)
        for blk in range(MIX_TM // w):
            prev = kvp_ref[...] if blk == 0 else kvc_ref[w * (blk - 1):w * blk, :]
            kv2 = jnp.concatenate([prev, kvc_ref[w * blk:w * (blk + 1), :]], axis=0)
            allowed = ((kj < w) & (kj > qi + (first_off if blk == 0 else 0))) | causal

            def store(c0, val, blk=blk):
                att_scr[slot, w * blk:w * (blk + 1), c0:c0 + LANES] = val

            _attn_block(sink_ref, q_ref, w * blk, kv2, allowed, store)

        for ch in range(MIX_TM // MLSTM_CHUNK_P):
            r0 = MLSTM_CHUNK_P * ch

            def store_mh(cs, val, r0=r0):
                mh_scr[slot, r0:r0 + MLSTM_CHUNK_P, cs] = val

            _mlstm_chunk(mq_ref, mk_ref, mv_ref, mo_ref, g_ref, gt_ref, brow_ref, bcol_ref, mnw_ref,
                         c_scr, n_scr, m_scr, r0, store_mh)

    @pl.when(s == 0)
    def _():
        mixers()

    @pl.when((s > 0) & (s < tiles))
    def _():
        project()
        mixers()

    @pl.when(s == tiles)
    def _():
        project()
        c_out[...] = c_scr[...]
        n_out[...] = n_scr[...]
        m_out[...] = m_scr[...]


def _mix_p(sinks, q, kv, m_all, gates, gates_t, brow, bcol, mnw, x, mod, mod_row_block, norm_w, w_att, w_m):
    tm = MIX_TM
    tiles = SEQ // tm
    bpt = tm // WINDOW
    dh = MLSTM_HEAD_DIM
    att_tile = lambda s: jnp.minimum(s, tiles - 1)
    out_tile = lambda s: jnp.maximum(s - 1, 0)
    m_spec = lambda col: pl.BlockSpec((tm, MLSTM_WIDTH), lambda s: (att_tile(s), col))
    return pl.pallas_call(
        _mix_p_kernel,
        grid=(tiles + 1,),
        in_specs=[
            pl.BlockSpec(memory_space=pltpu.SMEM),
            pl.BlockSpec((tm, ATT_WIDTH), lambda s: (att_tile(s), 0)),
            pl.BlockSpec((WINDOW, 2 * KV_WIDTH), lambda s: (jnp.maximum(bpt * att_tile(s) - 1, 0), 0)),
            pl.BlockSpec((tm, 2 * KV_WIDTH), lambda s: (att_tile(s), 0)),
            m_spec(0), m_spec(1), m_spec(2), m_spec(3),
            pl.BlockSpec((tm, LANES), lambda s: (att_tile(s), 0)),
            pl.BlockSpec((16, tm), lambda s: (0, att_tile(s))),
            pl.BlockSpec((1, LANES), lambda s: (0, 0)),
            pl.BlockSpec((16, MLSTM_CHUNK_P), lambda s: (0, 0)),
            pl.BlockSpec((1, MLSTM_WIDTH), lambda s: (0, 0)),
            pl.BlockSpec((tm, D_MODEL), lambda s: (out_tile(s), 0)),
            pl.BlockSpec((8, D_MODEL), lambda s: (mod_row_block, 2)),
            pl.BlockSpec((8, D_MODEL), lambda s: (mod_row_block, 3)),
            pl.BlockSpec((8, D_MODEL), lambda s: (mod_row_block, 4)),
            pl.BlockSpec((1, D_MODEL), lambda s: (0, 0)),
            pl.BlockSpec((ATT_WIDTH, D_MODEL), lambda s: (0, 0)),
            pl.BlockSpec((MLSTM_WIDTH, D_MODEL), lambda s: (0, 0)),
        ],
        out_specs=[
            pl.BlockSpec((tm, D_MODEL), lambda s: (out_tile(s), 0)),
            pl.BlockSpec((tm, D_MODEL), lambda s: (out_tile(s), 0)),
            pl.BlockSpec((MLSTM_HEADS, dh, dh), lambda s: (0, 0, 0)),
            pl.BlockSpec((8, dh), lambda s: (0, 0)),
            pl.BlockSpec((8, LANES), lambda s: (0, 0)),
        ],
        out_shape=[
            jax.ShapeDtypeStruct((SEQ, D_MODEL), F32),
            jax.ShapeDtypeStruct((SEQ, D_MODEL), BF16),
            jax.ShapeDtypeStruct((MLSTM_HEADS, dh, dh), F32),
            jax.ShapeDtypeStruct((8, dh), F32),
            jax.ShapeDtypeStruct((8, LANES), F32),
        ],
        scratch_shapes=[
            pltpu.VMEM((2, tm, ATT_WIDTH), BF16),
            pltpu.VMEM((2, tm, MLSTM_WIDTH), BF16),
            pltpu.VMEM((MLSTM_HEADS, dh, dh), F32),
            pltpu.VMEM((8, dh), F32),
            pltpu.VMEM((8, LANES), F32),
        ],
        compiler_params=_cparams(("arbitrary",)),
        name="mix_p",
    )(sinks, q, kv, kv, m_all, m_all, m_all, m_all, gates, gates_t, brow, bcol, mnw, x, mod, mod, mod, norm_w,
      w_att, w_m)


ATT_S_BB = 8


def _attn_s_kernel(sink_ref, q_ref, kv32_ref, ck_ref, cv_ref, o_ref, ko_ref, vo_ref, q32_scr):
    t_new = DEC_SEQ
    w = WINDOW
    q32_scr[...] = q_ref[...].astype(F32)
    rows = 4 * 4 * 8
    row = lax.broadcasted_iota(jnp.int32, (rows, w), 0)
    slot = lax.broadcasted_iota(jnp.int32, (rows, w), 1)
    t_row = row % t_new
    second = (row % 8) >= t_new
    win_ok = (slot < w - t_new) | (slot - (w - t_new) <= t_row)
    old_ok = (slot >= 1) & (slot < t_new) & (slot > t_row)
    lane256 = lax.broadcasted_iota(jnp.int32, (32, 2 * LANES), 1)
    sink = sink_ref[...][:, 0:1]
    kv_new = jnp.concatenate([kv32_ref[...], jnp.zeros((w - ATT_S_BB * t_new, 2 * KV_WIDTH), F32)], axis=0)
    kv_t = kv_new.T
    new_slot = lax.broadcasted_iota(jnp.int32, (KV_WIDTH, w), 1) >= w - t_new
    for b in range(ATT_S_BB):
        cols = pltpu.roll(kv_t, w - t_new - t_new * b, 1)
        k_shift = pltpu.roll(ck_ref[b].reshape(KV_WIDTH, w), w - t_new, 1)
        v_shift = pltpu.roll(cv_ref[b].reshape(KV_WIDTH, w), w - t_new, 1)
        ko_ref[b] = jnp.where(new_slot, cols[:KV_WIDTH], k_shift).reshape(ATT_KV_HEADS, ATT_HEAD_DIM, w)
        vo_ref[b] = jnp.where(new_slot, cols[KV_WIDTH:], v_shift).reshape(ATT_KV_HEADS, ATT_HEAD_DIM, w)
    for pair in range(ATT_S_BB // 2):
        b0, b1 = 2 * pair, 2 * pair + 1
        q32 = jnp.concatenate([q32_scr[8 * pair:8 * (pair + 1), 256 * r:256 * (r + 1)] for r in range(ATT_GROUP)],
                              axis=0)
        qpad = jnp.concatenate(
            [jnp.where((lane256 // ATT_HEAD_DIM) == g, q32, 0.0) for g in range(ATT_KV_HEADS)], axis=0).astype(BF16)
        kw = [ko_ref[b].reshape(KV_WIDTH, w).astype(BF16) for b in (b0, b1)]
        vw = [vo_ref[b].reshape(KV_WIDTH, w).astype(BF16) for b in (b0, b1)]
        kc = [ck_ref[b].reshape(KV_WIDTH, w).astype(BF16) for b in (b0, b1)]
        vc = [cv_ref[b].reshape(KV_WIDTH, w).astype(BF16) for b in (b0, b1)]
        s_w = jnp.where(second, _dot(qpad, kw[1]), _dot(qpad, kw[0]))
        s_c = jnp.where(second, _dot(qpad, kc[1]), _dot(qpad, kc[0]))
        s_w = jnp.where(win_ok, s_w, -jnp.inf)
        s_c = jnp.where(old_ok, s_c, -jnp.inf)
        m = jnp.maximum(jnp.maximum(jnp.max(s_w, axis=-1, keepdims=True), jnp.max(s_c, axis=-1, keepdims=True)), sink)
        e_w = jnp.exp(s_w - m)
        e_c = jnp.exp(s_c - m)
        l = jnp.sum(e_w, axis=-1, keepdims=True) + jnp.sum(e_c, axis=-1, keepdims=True) + jnp.exp(sink - m)
        p_w = e_w / l
        p_c = e_c / l
        zero = jnp.zeros_like(p_w)
        o = (_dot_nt(jnp.where(second, zero, p_w).astype(BF16), vw[0])
             + _dot_nt(jnp.where(second, p_w, zero).astype(BF16), vw[1])
             + _dot_nt(jnp.where(second, zero, p_c).astype(BF16), vc[0])
             + _dot_nt(jnp.where(second, p_c, zero).astype(BF16), vc[1]))
        o32 = jnp.zeros((32, 2 * LANES), F32)
        for g in range(ATT_KV_HEADS):
            o32 = jnp.where((lane256 // ATT_HEAD_DIM) == g, o[32 * g:32 * (g + 1), :], o32)
        for r in range(ATT_GROUP):
            o_ref[8 * pair:8 * (pair + 1), 256 * r:256 * (r + 1)] = o32[8 * r:8 * (r + 1), :]


def _attn_s(sink_col, q, kv32, ck, cv):
    bb = ATT_S_BB
    rows = bb * DEC_SEQ
    cache_block = (bb, ATT_KV_HEADS, ATT_HEAD_DIM, WINDOW)
    cache_shape = (DEC_BATCH, ATT_KV_HEADS, ATT_HEAD_DIM, WINDOW)
    return pl.pallas_call(
        _attn_s_kernel,
        grid=(DEC_BATCH // bb,),
        in_specs=[
            pl.BlockSpec((128, LANES), lambda i: (0, 0)),
            pl.BlockSpec((rows, ATT_WIDTH), lambda i: (i, 0)),
            pl.BlockSpec((rows, 2 * KV_WIDTH), lambda i: (i, 0)),
            pl.BlockSpec(cache_block, lambda i: (i, 0, 0, 0)),
            pl.BlockSpec(cache_block, lambda i: (i, 0, 0, 0)),
        ],
        out_specs=[
            pl.BlockSpec((rows, ATT_WIDTH), lambda i: (i, 0)),
            pl.BlockSpec(cache_block, lambda i: (i, 0, 0, 0)),
            pl.BlockSpec(cache_block, lambda i: (i, 0, 0, 0)),
        ],
        out_shape=[
            jax.ShapeDtypeStruct((S_ROWS, ATT_WIDTH), F32),
            jax.ShapeDtypeStruct(cache_shape, F32),
            jax.ShapeDtypeStruct(cache_shape, F32),
        ],
        scratch_shapes=[pltpu.VMEM((rows, ATT_WIDTH), F32)],
        compiler_params=_cparams(("arbitrary",)),
        name="attn_s",
    )(sink_col, q, kv32, ck, cv)


def _head_norm_gate(h, nw, mo):
    hn = h * lax.rsqrt(jnp.mean(h * h, axis=-1, keepdims=True) + EPS) * nw
    return hn * _sigmoid(mo.astype(F32))


def _mlstm_chunk(q_ref, k_ref, v_ref, mo_ref, g_ref, gt_ref, brow_ref, bcol_ref, nw_ref, c_scr, n_scr, m_scr,
                 r0, store):
    L = MLSTM_CHUNK_P
    dh = MLSTM_HEAD_DIM
    rs = slice(r0, r0 + L)
    ti = lax.broadcasted_iota(jnp.int32, (L, L), 0)
    si = lax.broadcasted_iota(jnp.int32, (L, L), 1)
    causal = si <= ti
    tri = causal.astype(F32)
    tri_t = (ti <= si).astype(F32)
    gates = g_ref[rs, :] + brow_ref[...]
    gates_t = gt_ref[:, rs] + bcol_ref[...]
    b_col = jnp.dot(tri, _log_sigmoid(gates), precision=HI, preferred_element_type=F32)
    b_row = jnp.dot(_log_sigmoid(gates_t), tri_t, precision=HI, preferred_element_type=F32)
    for hd in range(MLSTM_HEADS):
        cs = slice(dh * hd, dh * (hd + 1))
        b_c = b_col[:, MLSTM_HEADS + hd:MLSTM_HEADS + hd + 1]
        li_c = gates[:, hd:hd + 1]
        b_r = b_row[MLSTM_HEADS + hd:MLSTM_HEADS + hd + 1, :]
        li_r = gates_t[hd:hd + 1, :]
        dm = jnp.where(causal, b_c - b_r + li_r, -jnp.inf)
        m_prev = m_scr[hd:hd + 1, 0:1]
        m_inter = b_c + m_prev
        m_t = jnp.maximum(m_inter, jnp.max(dm, axis=-1, keepdims=True))
        q = q_ref[rs, cs] * (dh ** -0.5)
        k = k_ref[rs, cs]
        v = v_ref[rs, cs]
        sm = _dot_nt(q, k) * jnp.exp(dm - m_t)
        a = jnp.exp(m_inter - m_t)
        c_old = c_scr[hd]
        n_old = n_scr[hd:hd + 1, :]
        num = a * _dot_nt(q, c_old.astype(BF16)) + _dot(sm.astype(BF16), v)
        qn = jnp.sum(q.astype(F32) * n_old, axis=-1, keepdims=True)
        den = a * qn + jnp.sum(sm, axis=-1, keepdims=True)
        h = num / jnp.maximum(jnp.abs(den), jnp.exp(-m_t))
        store(cs, _head_norm_gate(h, nw_ref[:, cs], mo_ref[rs, cs]).astype(BF16))
        m_new = m_t[L - 1:L, :]
        b_last = b_c[L - 1:L, :]
        g = jnp.exp(b_last - b_c + li_c - m_new)
        decay = jnp.exp(b_last + m_prev - m_new)
        gv = (g * v.astype(F32)).astype(BF16)
        c_scr[hd] = decay * c_old + _dot_tn(gv, k)
        n_scr[hd:hd + 1, :] = decay * n_old + jnp.sum(g * k.astype(F32), axis=0, keepdims=True)
        m_scr[hd:hd + 1, :] = jnp.broadcast_to(m_new, (1, LANES))


MLSTM_S_BB = 4


def _mlstm_s_kernel(q_ref, k_ref, v_ref, mo_ref, g_ref, nrep_ref, mrep_ref, brow_ref, nw_ref, c_ref,
                    mh_ref, c_out, nrow_ref, mrow_ref):
    T = DEC_SEQ
    R = MLSTM_S_BB * T
    H = MLSTM_HEADS
    dh = MLSTM_HEAD_DIM

    def shift(x, d):
        return pltpu.roll(x, d, 0)

    lanes = lax.broadcasted_iota(jnp.int32, (R, LANES), 1)
    tmod = lax.broadcasted_iota(jnp.int32, (R, LANES), 0) % T
    tmod_w = lax.broadcasted_iota(jnp.int32, (R, MLSTM_WIDTH), 0) % T
    head_ok = lanes < H
    gates = g_ref[...] + brow_ref[...]
    li = jnp.where(head_ok, gates, 0.0)
    lf = jnp.where(head_ok, pltpu.roll(_log_sigmoid(gates), LANES - H, 1), 0.0)
    bcum = lf
    for d in range(1, T):
        bcum = bcum + jnp.where(tmod >= d, shift(lf, d), 0.0)
    m0 = mrep_ref[...]
    m_inter = bcum + m0
    dms = [li] + [jnp.where(tmod >= d, bcum - shift(bcum, d) + shift(li, d), -jnp.inf) for d in range(1, T)]
    m_t = m_inter
    for dm in dms:
        m_t = jnp.maximum(m_t, dm)
    a = jnp.exp(m_inter - m_t)
    ws = [jnp.exp(dm - m_t) for dm in dms]

    q_bf = q_ref[...] * (dh ** -0.5)
    q = q_bf.astype(F32)
    k = k_ref[...].astype(F32)
    v = v_ref[...].astype(F32)
    seg = (lax.broadcasted_iota(jnp.int32, (MLSTM_WIDTH, LANES), 0) // dh
           == lax.broadcasted_iota(jnp.int32, (MLSTM_WIDTH, LANES), 1)).astype(F32)
    ex = (lax.broadcasted_iota(jnp.int32, (LANES, MLSTM_WIDTH), 1) // dh
          == lax.broadcasted_iota(jnp.int32, (LANES, MLSTM_WIDTH), 0)).astype(F32)

    def segsum(x):
        return jnp.dot(x, seg, precision=HI, preferred_element_type=F32)

    def expand(x):
        return jnp.dot(x, ex, precision=HI, preferred_element_type=F32)

    ks = [k] + [shift(k, d) for d in range(1, T)]
    vs = [v] + [shift(v, d) for d in range(1, T)]
    sms = [segsum(q * ks[d]) * ws[d] for d in range(T)]
    den = a * segsum(q * nrep_ref[...])
    for sm in sms:
        den = den + sm
    inv = 1.0 / jnp.maximum(jnp.abs(den), jnp.exp(-m_t))

    def last(x):
        out = jnp.zeros_like(x)
        for jj in range(T):
            out = jnp.where(tmod == T - 1 - jj, x if jj == 0 else pltpu.roll(x, R - jj, 0), out)
        return out

    m_new = last(m_t)
    b_last = last(bcum)
    g = jnp.where(head_ok, jnp.exp(b_last - bcum + li - m_new), 0.0)
    decay = jnp.where(head_ok, jnp.exp(b_last + m0 - m_new), 0.0)
    a_f = expand(a * inv)
    w_f = [expand(sm * inv) for sm in sms]
    g_f = expand(g)
    d_f = expand(decay)
    gv = (g_f * v).astype(BF16)
    rowb = lax.broadcasted_iota(jnp.int32, (R, dh), 0) // T
    for hd in range(H):
        cs = slice(dh * hd, dh * (hd + 1))
        qh = q_bf[:, cs]
        kh = k_ref[:, cs]
        gvh = gv[:, cs]
        qc = jnp.zeros((R, dh), F32)
        for bb in range(MLSTM_S_BB):
            c_old = c_ref[bb, hd]
            qc = jnp.where(rowb == bb, _dot_nt(qh, c_old.astype(BF16)), qc)
            upd = _dot_tn(jnp.where(rowb == bb, gvh, jnp.zeros_like(gvh)), kh)
            c_out[bb, hd] = d_f[T * bb:T * bb + 1, cs] * c_old + upd
        h = a_f[:, cs] * qc
        for d in range(T):
            h = h + w_f[d][:, cs] * vs[d][:, cs]
        mh_ref[:, cs] = _head_norm_gate(h, nw_ref[:, cs], mo_ref[:, cs]).astype(BF16)
    gk = g_f * k
    nsum = gk
    for d in range(1, T):
        nsum = nsum + jnp.where(tmod_w >= d, shift(gk, d), 0.0)
    nrow_ref[...] = d_f * nrep_ref[...] + nsum
    mrow_ref[...] = m_t


def _mlstm_s(m_all, gates, n_rep, m_rep, brow, nw, state_c):
    bb = MLSTM_S_BB
    R = bb * DEC_SEQ
    dh = MLSTM_HEAD_DIM
    H = MLSTM_HEADS
    return pl.pallas_call(
        _mlstm_s_kernel,
        grid=(DEC_BATCH // bb,),
        in_specs=[
            pl.BlockSpec((R, MLSTM_WIDTH), lambda i: (i, 0)),
            pl.BlockSpec((R, MLSTM_WIDTH), lambda i: (i, 1)),
            pl.BlockSpec((R, MLSTM_WIDTH), lambda i: (i, 2)),
            pl.BlockSpec((R, MLSTM_WIDTH), lambda i: (i, 3)),
            pl.BlockSpec((R, LANES), lambda i: (i, 0)),
            pl.BlockSpec((R, MLSTM_WIDTH), lambda i: (i, 0)),
            pl.BlockSpec((R, LANES), lambda i: (i, 0)),
            pl.BlockSpec((1, LANES), lambda i: (0, 0)),
            pl.BlockSpec((1, MLSTM_WIDTH), lambda i: (0, 0)),
            pl.BlockSpec((bb, H, dh, dh), lambda i: (i, 0, 0, 0)),
        ],
        out_specs=[
            pl.BlockSpec((R, MLSTM_WIDTH), lambda i: (i, 0)),
            pl.BlockSpec((bb, H, dh, dh), lambda i: (i, 0, 0, 0)),
            pl.BlockSpec((R, MLSTM_WIDTH), lambda i: (i, 0)),
            pl.BlockSpec((R, LANES), lambda i: (i, 0)),
        ],
        out_shape=[
            jax.ShapeDtypeStruct((S_ROWS, MLSTM_WIDTH), BF16),
            jax.ShapeDtypeStruct((DEC_BATCH, H, dh, dh), F32),
            jax.ShapeDtypeStruct((S_ROWS, MLSTM_WIDTH), F32),
            jax.ShapeDtypeStruct((S_ROWS, LANES), F32),
        ],
        compiler_params=_cparams(("arbitrary",)),
        name="mlstm_s",
    )(m_all, m_all, m_all, m_all, gates, n_rep, m_rep, brow, nw, state_c)


def _outproj_kernel(att_ref, mh_ref, x_ref, g1_ref, sh_ref, sc_ref, nw_ref, wa_ref, wm_ref, x1_ref, h2_ref, *,
                    per_row):
    y = _dot(att_ref[...].astype(BF16), wa_ref[...]) + _dot(mh_ref[...], wm_ref[...])
    x1 = x_ref[...] + _mod_row(g1_ref, per_row) * y
    x1_ref[...] = x1
    r = lax.rsqrt(jnp.mean(x1 * x1, axis=-1, keepdims=True) + EPS)
    h2 = (x1 * r * nw_ref[...]) * (1.0 + _mod_row(sc_ref, per_row)) + _mod_row(sh_ref, per_row)
    h2_ref[...] = h2.astype(BF16)


def _outproj(att, mh, x, mod, mod_row_block, norm_w, w_att, w_m, *, tm, per_row):
    rows = x.shape[0]
    mod_rows = tm if per_row else 8
    mod_idx = (lambda i: i) if per_row else (lambda i: mod_row_block)
    kern = functools.partial(_outproj_kernel, per_row=per_row)
    return pl.pallas_call(
        kern,
        grid=(rows // tm,),
        in_specs=[
            pl.BlockSpec((tm, ATT_WIDTH), lambda i: (i, 0)),
            pl.BlockSpec((tm, MLSTM_WIDTH), lambda i: (i, 0)),
            pl.BlockSpec((tm, D_MODEL), lambda i: (i, 0)),
            pl.BlockSpec((mod_rows, D_MODEL), lambda i: (mod_idx(i), 2)),
            pl.BlockSpec((mod_rows, D_MODEL), lambda i: (mod_idx(i), 3)),
            pl.BlockSpec((mod_rows, D_MODEL), lambda i: (mod_idx(i), 4)),
            pl.BlockSpec((1, D_MODEL), lambda i: (0, 0)),
            pl.BlockSpec((ATT_WIDTH, D_MODEL), lambda i: (0, 0)),
            pl.BlockSpec((MLSTM_WIDTH, D_MODEL), lambda i: (0, 0)),
        ],
        out_specs=[
            pl.BlockSpec((tm, D_MODEL), lambda i: (i, 0)),
            pl.BlockSpec((tm, D_MODEL), lambda i: (i, 0)),
        ],
        out_shape=[
            jax.ShapeDtypeStruct((rows, D_MODEL), F32),
            jax.ShapeDtypeStruct((rows, D_MODEL), BF16),
        ],
        compiler_params=_cparams(("arbitrary",)),
        name="outproj_s" if per_row else "outproj_p",
    )(att, mh, x, mod, mod, mod, norm_w, w_att, w_m)


def _ffn_a_kernel(hp_ref, hs_ref, wg_ref, wu_ref, ap_ref, as_ref, wg_scr, wu_scr):
    i = pl.program_id(1)

    def swiglu(h_ref, a_ref):
        h = h_ref[...]
        g = _dot(h, wg_scr[...])
        u = _dot(h, wu_scr[...])
        a_ref[...] = (g * _sigmoid(g) * u).astype(BF16)

    @pl.when(i == 0)
    def _():
        wg_scr[...] = wg_ref[...].astype(BF16)
        wu_scr[...] = wu_ref[...].astype(BF16)
        swiglu(hs_ref, as_ref)

    @pl.when(i > 0)
    def _():
        swiglu(hp_ref, ap_ref)


def _ffn_a(h2_p, h2_s, w_gate, w_up, *, tm):
    rows_p, rows_s = h2_p.shape[0], h2_s.shape[0]
    tn = 512
    ni = rows_p // tm
    tile = lambda i: jnp.maximum(i - 1, 0)
    return pl.pallas_call(
        _ffn_a_kernel,
        grid=(D_FF // tn, ni + 1),
        in_specs=[
            pl.BlockSpec((tm, D_MODEL), lambda j, i: (tile(i), 0)),
            pl.BlockSpec((rows_s, D_MODEL), lambda j, i: (0, 0)),
            pl.BlockSpec((D_MODEL, tn), lambda j, i: (0, j)),
            pl.BlockSpec((D_MODEL, tn), lambda j, i: (0, j)),
        ],
        out_specs=[
            pl.BlockSpec((tm, tn), lambda j, i: (tile(i), j)),
            pl.BlockSpec((rows_s, tn), lambda j, i: (0, j)),
        ],
        out_shape=[
            jax.ShapeDtypeStruct((rows_p, D_FF), BF16),
            jax.ShapeDtypeStruct((rows_s, D_FF), BF16),
        ],
        scratch_shapes=[pltpu.VMEM((D_MODEL, tn), BF16), pltpu.VMEM((D_MODEL, tn), BF16)],
        compiler_params=_cparams(("arbitrary", "arbitrary")),
        name="ffn_a",
    )(h2_p, h2_s, w_gate, w_up)


def _ffn_b_kernel(a_ref, wd_ref, x1_ref, g2_ref, fw_ref, y_ref, *rest, per_row, emit_w):
    kk = pl.program_id(1)

    def weight_tile():
        wt = wd_ref[...].astype(BF16)
        if emit_w:
            rest[0][...] = wt
        return wt

    @pl.when(kk == 0)
    def _():
        y_ref[...] = _dot(a_ref[...], weight_tile())

    @pl.when(kk > 0)
    def _():
        y_ref[...] += _dot(a_ref[...], weight_tile())

    @pl.when(kk == pl.num_programs(1) - 1)
    def _():
        x2 = x1_ref[...] + _mod_row(g2_ref, per_row) * y_ref[...]
        y_ref[...] = x2 * lax.rsqrt(jnp.mean(x2 * x2, axis=-1, keepdims=True) + EPS) * fw_ref[...]


def _ffn_b(a, w_down, x1, mod, mod_row_block, final_w, *, tm, per_row, emit_w):
    rows = a.shape[0]
    tk = 512
    mod_rows = tm if per_row else 8
    mod_idx = (lambda i: i) if per_row else (lambda i: mod_row_block)
    kern = functools.partial(_ffn_b_kernel, per_row=per_row, emit_w=emit_w)
    w_spec = pl.BlockSpec((tk, D_MODEL), lambda i, k: (k, 0))
    y_spec = pl.BlockSpec((tm, D_MODEL), lambda i, k: (i, 0))
    y_shape = jax.ShapeDtypeStruct((rows, D_MODEL), F32)
    return pl.pallas_call(
        kern,
        grid=(rows // tm, D_FF // tk),
        in_specs=[
            pl.BlockSpec((tm, tk), lambda i, k: (i, k)),
            w_spec,
            pl.BlockSpec((tm, D_MODEL), lambda i, k: (i, 0)),
            pl.BlockSpec((mod_rows, D_MODEL), lambda i, k: (mod_idx(i), 5)),
            pl.BlockSpec((1, D_MODEL), lambda i, k: (0, 0)),
        ],
        out_specs=[y_spec, w_spec] if emit_w else y_spec,
        out_shape=[y_shape, jax.ShapeDtypeStruct((D_FF, D_MODEL), BF16)] if emit_w else y_shape,
        compiler_params=_cparams(("arbitrary", "arbitrary")),
        name="ffn_b_s" if per_row else "ffn_b_p",
    )(a, w_down, x1, mod, final_w)


def _rope_tables(pos):
    half = ROPE_DIM // 2
    inv = np.float32(ROPE_THETA) ** (-np.arange(0, ROPE_DIM, 2, dtype=np.float32) / np.float32(ROPE_DIM))
    d = np.arange(LANES) % ATT_HEAD_DIM
    ang = pos.astype(np.float32)[:, None] * inv[d % half][None, :].astype(np.float32)
    cos, sin = np.cos(ang), np.sin(ang)
    d = d[None, :]
    tables = (np.where(d < ROPE_DIM, cos, 1.0), np.where(d < half, -sin, 0.0),
              np.where((d >= half) & (d < ROPE_DIM), sin, 0.0))
    return tuple(jnp.asarray(t.astype(np.float32)) for t in tables)


def kernel(x_prompt, x_sample, cache_k_win, cache_v_win, state_C, state_n, state_m, c_prompt, c_sample,
           norm1_w, norm2_w, final_norm_w, w_ada, b_ada, w_in, b_ig, b_fg, attn_sinks, mh_norm_w,
           w_out, w_gate, w_up, w_down):
    assert w_in.shape[0] == 1, "single-layer trunk"
    T = DEC_SEQ
    xp = x_prompt[0]
    xs = x_sample.reshape(S_ROWS, D_MODEL)

    c_all = jnp.concatenate([jnp.repeat(c_sample, T, axis=0), c_prompt, jnp.zeros((15, D_MODEL), F32)], axis=0)
    mod = _ada(c_all, w_ada[0], b_ada)
    prompt_mod_block = S_ROWS // 8

    w_in_t = jnp.transpose(w_in[0])
    wq_t = (w_in_t[:ATT_WIDTH].reshape(ATT_KV_HEADS, ATT_GROUP, ATT_HEAD_DIM, D_MODEL)
            .transpose(1, 0, 2, 3).reshape(ATT_WIDTH, D_MODEL).astype(BF16))
    w_gates_t = w_in_t[MAIN_WIDTH:]
    wg = jnp.pad(w_gates_t.T, ((0, 0), (0, LANES - 2 * MLSTM_HEADS))).astype(BF16)
    w_out_att = (w_out[0, :ATT_WIDTH].reshape(ATT_KV_HEADS, ATT_GROUP, ATT_HEAD_DIM, D_MODEL)
                 .transpose(1, 0, 2, 3).reshape(ATT_WIDTH, D_MODEL).astype(BF16))
    w_out_m = w_out[0, ATT_WIDTH:].astype(BF16)
    n1 = norm1_w.reshape(1, D_MODEL)
    n2 = norm2_w.reshape(1, D_MODEL)
    fw = final_norm_w.reshape(1, D_MODEL)
    nw = mh_norm_w.reshape(1, MLSTM_WIDTH)
    gate_bias = jnp.concatenate([b_ig[0], b_fg[0]])
    brow = jnp.pad(gate_bias, (0, LANES - 2 * MLSTM_HEADS)).reshape(1, LANES)
    bcol = jnp.broadcast_to(jnp.pad(gate_bias, (0, 16 - 2 * MLSTM_HEADS))[:, None], (16, MLSTM_CHUNK_P))

    rope_p = _rope_tables(np.arange(SEQ))
    rope_s = _rope_tables(np.tile(PAST_LEN + np.arange(T), DEC_BATCH))

    tm_p = 1024
    q_s, _, kv32_s, m_s, g_s, _, w_in_bf = _inproj(xs, mod, 0, n1, wq_t, w_in_t, wg, *rope_s,
                                                   tm=S_ROWS, per_row=True, emit_w=True)
    q_p, kv_p, kv32_p, m_p, g_p, gt_p = _inproj(xp, mod, prompt_mod_block, n1, wq_t, w_in_bf, wg, *rope_p,
                                                tm=tm_p, per_row=False, emit_w=False)

    sinks = attn_sinks[0]
    sink_col = jnp.broadcast_to(sinks.reshape(ATT_HEADS, 1, 1), (ATT_HEADS, 8, LANES)).reshape(128, LANES)
    ck = jnp.transpose(cache_k_win[0], (0, 2, 3, 1))
    cv = jnp.transpose(cache_v_win[0], (0, 2, 3, 1))
    att_s, kwin_s, vwin_s = _attn_s(sink_col, q_s, kv32_s, ck, cv)

    n_rep = jnp.repeat(state_n[0].reshape(DEC_BATCH, MLSTM_WIDTH), T, axis=0)
    m_rep = jnp.pad(jnp.repeat(state_m[0], T, axis=0), ((0, 0), (0, LANES - MLSTM_HEADS)))
    mh_s, c_s, nrow_s, mrow_s = _mlstm_s(m_s, g_s, n_rep, m_rep, brow, nw, state_C[0])

    x1_p, h2_p, c_p, n_p, mm_p = _mix_p(sinks, q_p, kv_p, m_p, g_p, gt_p, brow, bcol, nw, xp, mod, prompt_mod_block,
                                        n2, w_out_att, w_out_m)
    x1_s, h2_s = _outproj(att_s, mh_s, xs, mod, 0, n2, w_out_att, w_out_m, tm=S_ROWS, per_row=True)
    a_p, a_s = _ffn_a(h2_p, h2_s, w_gate[0], w_up[0], tm=tm_p)
    y_s, w_down_bf = _ffn_b(a_s, w_down[0], x1_s, mod, 0, fw, tm=S_ROWS, per_row=True, emit_w=True)
    y_p = _ffn_b(a_p, w_down_bf, x1_p, mod, prompt_mod_block, fw, tm=tm_p, per_row=False, emit_w=False)

    kv_shape = (1, 1, WINDOW, ATT_KV_HEADS, ATT_HEAD_DIM)
    kv_last = kv32_p[tm_p - WINDOW:]
    dh = MLSTM_HEAD_DIM
    return (
        y_p.reshape(1, SEQ, D_MODEL),
        y_s.reshape(DEC_BATCH, T, D_MODEL),
        kv_last[:, :KV_WIDTH].reshape(kv_shape),
        kv_last[:, KV_WIDTH:].reshape(kv_shape),
        c_p.reshape(1, 1, MLSTM_HEADS, dh, dh),
        n_p[:MLSTM_HEADS].reshape(1, 1, MLSTM_HEADS, dh),
        mm_p[:MLSTM_HEADS, 0].reshape(1, 1, MLSTM_HEADS),
        jnp.transpose(kwin_s, (0, 3, 1, 2))[None],
        jnp.transpose(vwin_s, (0, 3, 1, 2))[None],
        c_s.reshape(1, DEC_BATCH, MLSTM_HEADS, dh, dh),
        nrow_s[T - 1::T].reshape(1, DEC_BATCH, MLSTM_HEADS, dh),
        mrow_s[T - 1::T, :MLSTM_HEADS].reshape(1, DEC_BATCH, MLSTM_HEADS),
    )
```

```python
import functools

import jax
import jax.numpy as jnp
import numpy as np
from jax import lax
from jax.experimental import pallas as pl
from jax.experimental.pallas import tpu as pltpu

F32 = jnp.float32
BF16 = jnp.bfloat16

D_MODEL = 2048
SEQ = 8192
DEC_BATCH = 128
DEC_SEQ = 4
S_ROWS = DEC_BATCH * DEC_SEQ
PAST_LEN = 16384
ATT_HEADS = 16
ATT_KV_HEADS = 4
ATT_GROUP = 4
ATT_HEAD_DIM = 64
WINDOW = 128
ROPE_THETA = 500000.0
ROPE_DIM = 16
MLSTM_HEADS = 4
MLSTM_HEAD_DIM = 256
ATT_WIDTH = 1024
KV_WIDTH = 256
MLSTM_WIDTH = 1024
MAIN_WIDTH = ATT_WIDTH + 2 * KV_WIDTH + 4 * MLSTM_WIDTH
D_FF = 5632
N_MOD = 6
EPS = 1e-6

LANES = 128
MLSTM_CHUNK_P = 256
VMEM_LIMIT = 56 * 1024 * 1024

NT_DIMS = (((1,), (1,)), ((), ()))
TN_DIMS = (((0,), (0,)), ((), ()))
HI = lax.Precision.HIGHEST


def _cparams(sem):
    return pltpu.CompilerParams(dimension_semantics=sem, vmem_limit_bytes=VMEM_LIMIT)


def _dot(a, b):
    return jnp.dot(a, b, preferred_element_type=F32)


def _dot_nt(a, b):
    return lax.dot_general(a, b, NT_DIMS, preferred_element_type=F32)


def _dot_tn(a, b):
    return lax.dot_general(a, b, TN_DIMS, preferred_element_type=F32)


def _sigmoid(x):
    return 1.0 / (1.0 + jnp.exp(-x))


def _log_sigmoid(x):
    return jnp.minimum(x, 0.0) - jnp.log(1.0 + jnp.exp(-jnp.abs(x)))


def _mod_row(ref, per_row):
    return ref[...] if per_row else ref[0:1, :]


def _ada_kernel(c_ref, w_ref, b_ref, o_ref, s_scr):
    @pl.when(pl.program_id(0) == 0)
    def _():
        c = c_ref[...]
        s_scr[...] = (c * _sigmoid(c)).astype(BF16)

    o_ref[...] = _dot(s_scr[...], w_ref[...].astype(BF16)) + b_ref[...]


def _ada(c_all, w_ada, b_ada):
    m = c_all.shape[0]
    n = w_ada.shape[1]
    tn = 1024
    return pl.pallas_call(
        _ada_kernel,
        grid=(n // tn,),
        in_specs=[
            pl.BlockSpec((m, D_MODEL), lambda j: (0, 0)),
            pl.BlockSpec((D_MODEL, tn), lambda j: (0, j)),
            pl.BlockSpec((1, tn), lambda j: (0, j)),
        ],
        out_specs=pl.BlockSpec((m, tn), lambda j: (0, j)),
        out_shape=jax.ShapeDtypeStruct((m, n), F32),
        scratch_shapes=[pltpu.VMEM((m, D_MODEL), BF16)],
        compiler_params=_cparams(("arbitrary",)),
        name="ada",
    )(c_all, w_ada, b_ada)


def _rope_store(acc, cos, sa, sb, out_ref, ncols, scale):
    for c in range(ncols // LANES):
        xc = acc[:, LANES * c:LANES * (c + 1)]
        rot = xc * cos + pltpu.roll(xc, LANES - 8, 1) * sa + pltpu.roll(xc, 8, 1) * sb
        if scale != 1.0:
            rot = rot * scale
        out_ref[:, LANES * c:LANES * (c + 1)] = rot.astype(out_ref.dtype)


def _inproj_kernel(x_ref, sh_ref, sc_ref, nw_ref, wq_ref, win_ref, wg_ref, cos_ref, sa_ref, sb_ref,
                   q_ref, kv_ref, kv32_ref, m_ref, g_ref, gt_ref, *rest, per_row, emit_w):
    h_scr = rest[-1]
    j = pl.program_id(1)

    def weight_tile():
        wt = win_ref[...].astype(BF16)
        if emit_w:
            rest[0][...] = wt
        return wt

    @pl.when(j == 0)
    def _():
        x = x_ref[...]
        r = lax.rsqrt(jnp.mean(x * x, axis=-1, keepdims=True) + EPS)
        gain = nw_ref[...] * (1.0 + _mod_row(sc_ref, per_row))
        h = (x * r * gain + _mod_row(sh_ref, per_row)).astype(BF16)
        h_scr[...] = h
        wg = jnp.concatenate([wg_ref[...].astype(BF16), jnp.zeros((LANES - 16, D_MODEL), BF16)], axis=0)
        g = _dot_nt(h, wg)
        g_ref[...] = g
        gt_ref[...] = g.T[0:16, :]

    @pl.when(j < 2)
    def _():
        acc = _dot_nt(h_scr[...], wq_ref[...])
        _rope_store(acc, cos_ref[...], sa_ref[...], sb_ref[...], q_ref, 512, ATT_HEAD_DIM ** -0.5)

    @pl.when(j == 2)
    def _():
        acc = _dot_nt(h_scr[...], weight_tile())
        _rope_store(acc, cos_ref[...], sa_ref[...], sb_ref[...], kv32_ref, KV_WIDTH, 1.0)
        kv32_ref[:, KV_WIDTH:] = acc[:, KV_WIDTH:]
        kv_ref[...] = kv32_ref[...].astype(BF16)

    @pl.when(j > 2)
    def _():
        m_ref[...] = _dot_nt(h_scr[...], weight_tile()).astype(BF16)


def _inproj(x, mod, mod_row_block, norm_w, wq_t, w_in_t, wg, cos, sa, sb, *, tm, per_row, emit_w):
    rows = x.shape[0]
    tn = 512
    nj = MAIN_WIDTH // tn
    mod_rows = tm if per_row else 8
    mod_idx = (lambda i: i) if per_row else (lambda i: mod_row_block)
    kern = functools.partial(_inproj_kernel, per_row=per_row, emit_w=emit_w)
    copy_spec = pl.BlockSpec((tn, D_MODEL), lambda i, j: (jnp.maximum(j, 2) - 2, 0))
    w_spec = pl.BlockSpec((tn, D_MODEL), lambda i, j: (jnp.maximum(j, 2), 0)) if emit_w else copy_spec
    extra_specs = [copy_spec] if emit_w else []
    extra_shapes = [jax.ShapeDtypeStruct((MAIN_WIDTH - ATT_WIDTH, D_MODEL), BF16)] if emit_w else []
    return pl.pallas_call(
        kern,
        grid=(rows // tm, nj),
        in_specs=[
            pl.BlockSpec((tm, D_MODEL), lambda i, j: (jnp.minimum(i + jnp.minimum(j, 1), rows // tm - 1), 0)),
            pl.BlockSpec((mod_rows, D_MODEL), lambda i, j: (mod_idx(i), 0)),
            pl.BlockSpec((mod_rows, D_MODEL), lambda i, j: (mod_idx(i), 1)),
            pl.BlockSpec((1, D_MODEL), lambda i, j: (0, 0)),
            pl.BlockSpec((tn, D_MODEL), lambda i, j: (jnp.minimum(j, 1), 0)),
            w_spec,
            pl.BlockSpec((16, D_MODEL), lambda i, j: (0, 0)),
            pl.BlockSpec((tm, LANES), lambda i, j: (i, 0)),
            pl.BlockSpec((tm, LANES), lambda i, j: (i, 0)),
            pl.BlockSpec((tm, LANES), lambda i, j: (i, 0)),
        ],
        out_specs=[
            pl.BlockSpec((tm, tn), lambda i, j: (i, jnp.minimum(j, 1))),
            pl.BlockSpec((tm, tn), lambda i, j: (i, 0)),
            pl.BlockSpec((tm, tn), lambda i, j: (0, 0)),
            pl.BlockSpec((tm, tn), lambda i, j: (i, jnp.clip(j - 3, 0, 7))),
            pl.BlockSpec((tm, LANES), lambda i, j: (i, 0)),
            pl.BlockSpec((16, tm), lambda i, j: (0, i)),
        ] + extra_specs,
        out_shape=[
            jax.ShapeDtypeStruct((rows, ATT_WIDTH), BF16),
            jax.ShapeDtypeStruct((rows, 2 * KV_WIDTH), BF16),
            jax.ShapeDtypeStruct((tm, 2 * KV_WIDTH), F32),
            jax.ShapeDtypeStruct((rows, 4 * MLSTM_WIDTH), BF16),
            jax.ShapeDtypeStruct((rows, LANES), F32),
            jax.ShapeDtypeStruct((16, rows), F32),
        ] + extra_shapes,
        scratch_shapes=[pltpu.VMEM((tm, D_MODEL), BF16)],
        compiler_params=_cparams(("arbitrary", "arbitrary")),
        name="inproj_s" if per_row else "inproj_p",
    )(x, mod, mod, norm_w, wq_t, w_in_t, wg, cos, sa, sb)


def _attn_block(sink_ref, q_ref, row0, kv2, allowed, store):
    w = WINDOW
    grp = ATT_GROUP
    rows = grp * w
    member = lax.broadcasted_iota(jnp.int32, (rows, 1), 0) // w
    low = lax.broadcasted_iota(jnp.int32, (2 * w, LANES), 1) < ATT_HEAD_DIM
    low_o = lax.broadcasted_iota(jnp.int32, (rows, LANES), 1) < ATT_HEAD_DIM
    key_row = lax.broadcasted_iota(jnp.int32, (4 * w, LANES), 0)
    key_lane = lax.broadcasted_iota(jnp.int32, (4 * w, LANES), 1)
    ones_bd = (((key_row < 2 * w) & (key_lane < ATT_HEAD_DIM))
               | ((key_row >= 2 * w) & (key_lane >= ATT_HEAD_DIM))).astype(BF16)
    zero = jnp.zeros((2 * w, LANES), BF16)
    for cp in range(2):
        k128 = kv2[:, LANES * cp:LANES * (cp + 1)]
        v128 = kv2[:, KV_WIDTH + LANES * cp:KV_WIDTH + LANES * (cp + 1)]
        kbd = jnp.concatenate([jnp.where(low, k128, zero), jnp.where(low, zero, k128)], axis=0)
        vbd = jnp.concatenate([jnp.where(low, v128, zero), jnp.where(low, zero, v128)], axis=0)
        v_aug = jnp.concatenate([vbd, ones_bd], axis=1)
        q4 = jnp.concatenate([q_ref[row0:row0 + w, 256 * r + LANES * cp:256 * r + LANES * (cp + 1)]
                              for r in range(grp)], axis=0)
        s = _dot_nt(q4, kbd)
        es, tails = [], []
        for half in range(2):
            sh = jnp.where(allowed, s[:, 2 * w * half:2 * w * (half + 1)], -jnp.inf)
            head0 = (2 * cp + half) * grp
            sink = jnp.full((rows, 1), sink_ref[head0], F32)
            for r in range(1, grp):
                sink = jnp.where(member == r, sink_ref[head0 + r], sink)
            m = jnp.maximum(jnp.max(sh, axis=-1, keepdims=True), sink)
            es.append(jnp.exp(sh - m).astype(BF16))
            tails.append(jnp.exp(sink - m))
        oa = _dot(jnp.concatenate(es, axis=1), v_aug)
        l = oa[:, LANES:] + jnp.where(low_o, tails[0], tails[1])
        o = (oa[:, :LANES] / l).astype(BF16)
        for r in range(grp):
            store(256 * r + LANES * cp, o[w * r:w * (r + 1)])


MIX_TM = 512


def _mix_p_kernel(sink_ref, q_ref, kvp_ref, kvc_ref, mq_ref, mk_ref, mv_ref, mo_ref, g_ref, gt_ref, brow_ref, bcol_ref,
                  mnw_ref, x_ref, g1_ref, sh_ref, sc_ref, nw_ref, wa_ref, wm_ref,
                  x1_ref, h2_ref, c_out, n_out, m_out, att_scr, mh_scr, c_scr, n_scr, m_scr):
    s = pl.program_id(0)
    w = WINDOW
    tiles = pl.num_programs(0) - 1
    slot = s % 2

    @pl.when(s == 0)
    def _():
        c_scr[...] = jnp.zeros_like(c_scr)
        n_scr[...] = jnp.zeros_like(n_scr)
        m_scr[...] = jnp.zeros_like(m_scr)

    def project():
        y = _dot(att_scr[1 - slot], wa_ref[...]) + _dot(mh_scr[1 - slot], wm_ref[...])
        x1 = x_ref[...] + g1_ref[0:1, :] * y
        x1_ref[...] = x1
        r = lax.rsqrt(jnp.mean(x1 * x1, axis=-1, keepdims=True) + EPS)
        gain = nw_ref[...] * (1.0 + sc_ref[0:1, :])
        h2_ref[...] = (x1 * r * gain + sh_ref[0:1, :]).astype(BF16)

    def mixers():
        rows = ATT_GROUP * w
        qi = lax.broadcasted_iota(jnp.int32, (rows, 2 * w), 0) % w
        kj = lax.broadcasted_iota(jnp.int32, (rows, 2 * w), 1)
        first_off = jnp.where(s > 0, 0, 4 * w)
        causal = (kj >= w) & (kj - w <= qi)
        for blk in range(MIX_TM // w):
            prev = kvp_ref[...] if blk == 0 else kvc_ref[w * (blk - 1):w * blk, :]
            kv2 = jnp.concatenate([prev, kvc_ref[w * blk:w * (blk + 1), :]], axis=0)
            allowed = ((kj < w) & (kj > qi + (first_off if blk == 0 else 0))) | causal

            def store(c0, val, blk=blk):
                att_scr[slot, w * blk:w * (blk + 1), c0:c0 + LANES] = val

            _attn_block(sink_ref, q_ref, w * blk, kv2, allowed, store)

        for ch in range(MIX_TM // MLSTM_CHUNK_P):
            r0 = MLSTM_CHUNK_P * ch

            def store_mh(cs, val, r0=r0):
                mh_scr[slot, r0:r0 + MLSTM_CHUNK_P, cs] = val

            _mlstm_chunk(mq_ref, mk_ref, mv_ref, mo_ref, g_ref, gt_ref, brow_ref, bcol_ref, mnw_ref,
                         c_scr, n_scr, m_scr, r0, store_mh)

    @pl.when(s == 0)
    def _():
        mixers()

    @pl.when((s > 0) & (s < tiles))
    def _():
        project()
        mixers()

    @pl.when(s == tiles)
    def _():
        project()
        c_out[...] = c_scr[...]
        n_out[...] = n_scr[...]
        m_out[...] = m_scr[...]


def _mix_p(sinks, q, kv, m_all, gates, gates_t, brow, bcol, mnw, x, mod, mod_row_block, norm_w, w_att, w_m):
    tm = MIX_TM
    tiles = SEQ // tm
    bpt = tm // WINDOW
    dh = MLSTM_HEAD_DIM
    att_tile = lambda s: jnp.minimum(s, tiles - 1)
    out_tile = lambda s: jnp.maximum(s - 1, 0)
    m_spec = lambda col: pl.BlockSpec((tm, MLSTM_WIDTH), lambda s: (att_tile(s), col))
    return pl.pallas_call(
        _mix_p_kernel,
        grid=(tiles + 1,),
        in_specs=[
            pl.BlockSpec(memory_space=pltpu.SMEM),
            pl.BlockSpec((tm, ATT_WIDTH), lambda s: (att_tile(s), 0)),
            pl.BlockSpec((WINDOW, 2 * KV_WIDTH), lambda s: (jnp.maximum(bpt * att_tile(s) - 1, 0), 0)),
            pl.BlockSpec((tm, 2 * KV_WIDTH), lambda s: (att_tile(s), 0)),
            m_spec(0), m_spec(1), m_spec(2), m_spec(3),
            pl.BlockSpec((tm, LANES), lambda s: (att_tile(s), 0)),
            pl.BlockSpec((16, tm), lambda s: (0, att_tile(s))),
            pl.BlockSpec((1, LANES), lambda s: (0, 0)),
            pl.BlockSpec((16, MLSTM_CHUNK_P), lambda s: (0, 0)),
            pl.BlockSpec((1, MLSTM_WIDTH), lambda s: (0, 0)),
            pl.BlockSpec((tm, D_MODEL), lambda s: (out_tile(s), 0)),
            pl.BlockSpec((8, D_MODEL), lambda s: (mod_row_block, 2)),
            pl.BlockSpec((8, D_MODEL), lambda s: (mod_row_block, 3)),
            pl.BlockSpec((8, D_MODEL), lambda s: (mod_row_block, 4)),
            pl.BlockSpec((1, D_MODEL), lambda s: (0, 0)),
            pl.BlockSpec((ATT_WIDTH, D_MODEL), lambda s: (0, 0)),
            pl.BlockSpec((MLSTM_WIDTH, D_MODEL), lambda s: (0, 0)),
        ],
        out_specs=[
            pl.BlockSpec((tm, D_MODEL), lambda s: (out_tile(s), 0)),
            pl.BlockSpec((tm, D_MODEL), lambda s: (out_tile(s), 0)),
            pl.BlockSpec((MLSTM_HEADS, dh, dh), lambda s: (0, 0, 0)),
            pl.BlockSpec((8, dh), lambda s: (0, 0)),
            pl.BlockSpec((8, LANES), lambda s: (0, 0)),
        ],
        out_shape=[
            jax.ShapeDtypeStruct((SEQ, D_MODEL), F32),
            jax.ShapeDtypeStruct((SEQ, D_MODEL), BF16),
            jax.ShapeDtypeStruct((MLSTM_HEADS, dh, dh), F32),
            jax.ShapeDtypeStruct((8, dh), F32),
            jax.ShapeDtypeStruct((8, LANES), F32),
        ],
        scratch_shapes=[
            pltpu.VMEM((2, tm, ATT_WIDTH), BF16),
            pltpu.VMEM((2, tm, MLSTM_WIDTH), BF16),
            pltpu.VMEM((MLSTM_HEADS, dh, dh), F32),
            pltpu.VMEM((8, dh), F32),
            pltpu.VMEM((8, LANES), F32),
        ],
        compiler_params=_cparams(("arbitrary",)),
        name="mix_p",
    )(sinks, q, kv, kv, m_all, m_all, m_all, m_all, gates, gates_t, brow, bcol, mnw, x, mod, mod, mod, norm_w,
      w_att, w_m)


ATT_S_BB = 8


def _attn_s_kernel(sink_ref, q_ref, kv32_ref, ck_ref, cv_ref, o_ref, ko_ref, vo_ref, q32_scr):
    t_new = DEC_SEQ
    w = WINDOW
    q32_scr[...] = q_ref[...].astype(F32)
    rows = 4 * 4 * 8
    row = lax.broadcasted_iota(jnp.int32, (rows, w), 0)
    slot = lax.broadcasted_iota(jnp.int32, (rows, w), 1)
    t_row = row % t_new
    second = (row % 8) >= t_new
    win_ok = (slot < w - t_new) | (slot - (w - t_new) <= t_row)
    old_ok = (slot >= 1) & (slot < t_new) & (slot > t_row)
    lane256 = lax.broadcasted_iota(jnp.int32, (32, 2 * LANES), 1)
    sink = sink_ref[...][:, 0:1]
    kv_new = jnp.concatenate([kv32_ref[...], jnp.zeros((w - ATT_S_BB * t_new, 2 * KV_WIDTH), F32)], axis=0)
    kv_t = kv_new.T
    new_slot = lax.broadcasted_iota(jnp.int32, (KV_WIDTH, w), 1) >= w - t_new
    for b in range(ATT_S_BB):
        cols = pltpu.roll(kv_t, w - t_new - t_new * b, 1)
        k_shift = pltpu.roll(ck_ref[b].reshape(KV_WIDTH, w), w - t_new, 1)
        v_shift = pltpu.roll(cv_ref[b].reshape(KV_WIDTH, w), w - t_new, 1)
        ko_ref[b] = jnp.where(new_slot, cols[:KV_WIDTH], k_shift).reshape(ATT_KV_HEADS, ATT_HEAD_DIM, w)
        vo_ref[b] = jnp.where(new_slot, cols[KV_WIDTH:], v_shift).reshape(ATT_KV_HEADS, ATT_HEAD_DIM, w)
    for pair in range(ATT_S_BB // 2):
        b0, b1 = 2 * pair, 2 * pair + 1
        q32 = jnp.concatenate([q32_scr[8 * pair:8 * (pair + 1), 256 * r:256 * (r + 1)] for r in range(ATT_GROUP)],
                              axis=0)
        qpad = jnp.concatenate(
            [jnp.where((lane256 // ATT_HEAD_DIM) == g, q32, 0.0) for g in range(ATT_KV_HEADS)], axis=0).astype(BF16)
        kw = [ko_ref[b].reshape(KV_WIDTH, w).astype(BF16) for b in (b0, b1)]
        vw = [vo_ref[b].reshape(KV_WIDTH, w).astype(BF16) for b in (b0, b1)]
        kc = [ck_ref[b].reshape(KV_WIDTH, w).astype(BF16) for b in (b0, b1)]
        vc = [cv_ref[b].reshape(KV_WIDTH, w).astype(BF16) for b in (b0, b1)]
        s_w = jnp.where(second, _dot(qpad, kw[1]), _dot(qpad, kw[0]))
        s_c = jnp.where(second, _dot(qpad, kc[1]), _dot(qpad, kc[0]))
        s_w = jnp.where(win_ok, s_w, -jnp.inf)
        s_c = jnp.where(old_ok, s_c, -jnp.inf)
        m = jnp.maximum(jnp.maximum(jnp.max(s_w, axis=-1, keepdims=True), jnp.max(s_c, axis=-1, keepdims=True)), sink)
        e_w = jnp.exp(s_w - m)
        e_c = jnp.exp(s_c - m)
        l = jnp.sum(e_w, axis=-1, keepdims=True) + jnp.sum(e_c, axis=-1, keepdims=True) + jnp.exp(sink - m)
        p_w = e_w / l
        p_c = e_c / l
        zero = jnp.zeros_like(p_w)
        o = (_dot_nt(jnp.where(second, zero, p_w).astype(BF16), vw[0])
             + _dot_nt(jnp.where(second, p_w, zero).astype(BF16), vw[1])
             + _dot_nt(jnp.where(second, zero, p_c).astype(BF16), vc[0])
             + _dot_nt(jnp.where(second, p_c, zero).astype(BF16), vc[1]))
        o32 = jnp.zeros((32, 2 * LANES), F32)
        for g in range(ATT_KV_HEADS):
            o32 = jnp.where((lane256 // ATT_HEAD_DIM) == g, o[32 * g:32 * (g + 1), :], o32)
        for r in range(ATT_GROUP):
            o_ref[8 * pair:8 * (pair + 1), 256 * r:256 * (r + 1)] = o32[8 * r:8 * (r + 1), :]


def _attn_s(sink_col, q, kv32, ck, cv):
    bb = ATT_S_BB
    rows = bb * DEC_SEQ
    cache_block = (bb, ATT_KV_HEADS, ATT_HEAD_DIM, WINDOW)
    cache_shape = (DEC_BATCH, ATT_KV_HEADS, ATT_HEAD_DIM, WINDOW)
    return pl.pallas_call(
        _attn_s_kernel,
        grid=(DEC_BATCH // bb,),
        in_specs=[
            pl.BlockSpec((128, LANES), lambda i: (0, 0)),
            pl.BlockSpec((rows, ATT_WIDTH), lambda i: (i, 0)),
            pl.BlockSpec((rows, 2 * KV_WIDTH), lambda i: (i, 0)),
            pl.BlockSpec(cache_block, lambda i: (i, 0, 0, 0)),
            pl.BlockSpec(cache_block, lambda i: (i, 0, 0, 0)),
        ],
        out_specs=[
            pl.BlockSpec((rows, ATT_WIDTH), lambda i: (i, 0)),
            pl.BlockSpec(cache_block, lambda i: (i, 0, 0, 0)),
            pl.BlockSpec(cache_block, lambda i: (i, 0, 0, 0)),
        ],
        out_shape=[
            jax.ShapeDtypeStruct((S_ROWS, ATT_WIDTH), F32),
            jax.ShapeDtypeStruct(cache_shape, F32),
            jax.ShapeDtypeStruct(cache_shape, F32),
        ],
        scratch_shapes=[pltpu.VMEM((rows, ATT_WIDTH), F32)],
        compiler_params=_cparams(("arbitrary",)),
        name="attn_s",
    )(sink_col, q, kv32, ck, cv)


def _head_norm_gate(h, nw, mo):
    hn = h * lax.rsqrt(jnp.mean(h * h, axis=-1, keepdims=True) + EPS) * nw
    return hn * _sigmoid(mo.astype(F32))


def _mlstm_chunk(q_ref, k_ref, v_ref, mo_ref, g_ref, gt_ref, brow_ref, bcol_ref, nw_ref, c_scr, n_scr, m_scr,
                 r0, store):
    L = MLSTM_CHUNK_P
    dh = MLSTM_HEAD_DIM
    rs = slice(r0, r0 + L)
    ti = lax.broadcasted_iota(jnp.int32, (L, L), 0)
    si = lax.broadcasted_iota(jnp.int32, (L, L), 1)
    causal = si <= ti
    tri = causal.astype(F32)
    tri_t = (ti <= si).astype(F32)
    gates = g_ref[rs, :] + brow_ref[...]
    gates_t = gt_ref[:, rs] + bcol_ref[...]
    b_col = jnp.dot(tri, _log_sigmoid(gates), precision=HI, preferred_element_type=F32)
    b_row = jnp.dot(_log_sigmoid(gates_t), tri_t, precision=HI, preferred_element_type=F32)
    for hd in range(MLSTM_HEADS):
        cs = slice(dh * hd, dh * (hd + 1))
        b_c = b_col[:, MLSTM_HEADS + hd:MLSTM_HEADS + hd + 1]
        li_c = gates[:, hd:hd + 1]
        b_r = b_row[MLSTM_HEADS + hd:MLSTM_HEADS + hd + 1, :]
        li_r = gates_t[hd:hd + 1, :]
        dm = jnp.where(causal, b_c - b_r + li_r, -jnp.inf)
        m_prev = m_scr[hd:hd + 1, 0:1]
        m_inter = b_c + m_prev
        m_t = jnp.maximum(m_inter, jnp.max(dm, axis=-1, keepdims=True))
        q = q_ref[rs, cs] * (dh ** -0.5)
        k = k_ref[rs, cs]
        v = v_ref[rs, cs]
        sm = _dot_nt(q, k) * jnp.exp(dm - m_t)
        a = jnp.exp(m_inter - m_t)
        c_old = c_scr[hd]
        n_old = n_scr[hd:hd + 1, :]
        num = a * _dot_nt(q, c_old.astype(BF16)) + _dot(sm.astype(BF16), v)
        qn = jnp.sum(q.astype(F32) * n_old, axis=-1, keepdims=True)
        den = a * qn + jnp.sum(sm, axis=-1, keepdims=True)
        h = num / jnp.maximum(jnp.abs(den), jnp.exp(-m_t))
        store(cs, _head_norm_gate(h, nw_ref[:, cs], mo_ref[rs, cs]).astype(BF16))
        m_new = m_t[L - 1:L, :]
        b_last = b_c[L - 1:L, :]
        g = jnp.exp(b_last - b_c + li_c - m_new)
        decay = jnp.exp(b_last + m_prev - m_new)
        gv = (g * v.astype(F32)).astype(BF16)
        c_scr[hd] = decay * c_old + _dot_tn(gv, k)
        n_scr[hd:hd + 1, :] = decay * n_old + jnp.sum(g * k.astype(F32), axis=0, keepdims=True)
        m_scr[hd:hd + 1, :] = jnp.broadcast_to(m_new, (1, LANES))


MLSTM_S_BB = 4


def _mlstm_s_kernel(q_ref, k_ref, v_ref, mo_ref, g_ref, nrep_ref, mrep_ref, brow_ref, nw_ref, c_ref,
                    mh_ref, c_out, nrow_ref, mrow_ref):
    T = DEC_SEQ
    R = MLSTM_S_BB * T
    H = MLSTM_HEADS
    dh = MLSTM_HEAD_DIM

    def shift(x, d):
        return pltpu.roll(x, d, 0)

    lanes = lax.broadcasted_iota(jnp.int32, (R, LANES), 1)
    tmod = lax.broadcasted_iota(jnp.int32, (R, LANES), 0) % T
    tmod_w = lax.broadcasted_iota(jnp.int32, (R, MLSTM_WIDTH), 0) % T
    head_ok = lanes < H
    gates = g_ref[...] + brow_ref[...]
    li = jnp.where(head_ok, gates, 0.0)
    lf = jnp.where(head_ok, pltpu.roll(_log_sigmoid(gates), LANES - H, 1), 0.0)
    bcum = lf
    for d in range(1, T):
        bcum = bcum + jnp.where(tmod >= d, shift(lf, d), 0.0)
    m0 = mrep_ref[...]
    m_inter = bcum + m0
    dms = [li] + [jnp.where(tmod >= d, bcum - shift(bcum, d) + shift(li, d), -jnp.inf) for d in range(1, T)]
    m_t = m_inter
    for dm in dms:
        m_t = jnp.maximum(m_t, dm)
    a = jnp.exp(m_inter - m_t)
    ws = [jnp.exp(dm - m_t) for dm in dms]

    q_bf = q_ref[...] * (dh ** -0.5)
    q = q_bf.astype(F32)
    k = k_ref[...].astype(F32)
    v = v_ref[...].astype(F32)
    seg = (lax.broadcasted_iota(jnp.int32, (MLSTM_WIDTH, LANES), 0) // dh
           == lax.broadcasted_iota(jnp.int32, (MLSTM_WIDTH, LANES), 1)).astype(F32)
    ex = (lax.broadcasted_iota(jnp.int32, (LANES, MLSTM_WIDTH), 1) // dh
          == lax.broadcasted_iota(jnp.int32, (LANES, MLSTM_WIDTH), 0)).astype(F32)

    def segsum(x):
        return jnp.dot(x, seg, precision=HI, preferred_element_type=F32)

    def expand(x):
        return jnp.dot(x, ex, precision=HI, preferred_element_type=F32)

    ks = [k] + [shift(k, d) for d in range(1, T)]
    vs = [v] + [shift(v, d) for d in range(1, T)]
    sms = [segsum(q * ks[d]) * ws[d] for d in range(T)]
    den = a * segsum(q * nrep_ref[...])
    for sm in sms:
        den = den + sm
    inv = 1.0 / jnp.maximum(jnp.abs(den), jnp.exp(-m_t))

    def last(x):
        out = jnp.zeros_like(x)
        for jj in range(T):
            out = jnp.where(tmod == T - 1 - jj, x if jj == 0 else pltpu.roll(x, R - jj, 0), out)
        return out

    m_new = last(m_t)
    b_last = last(bcum)
    g = jnp.where(head_ok, jnp.exp(b_last - bcum + li - m_new), 0.0)
    decay = jnp.where(head_ok, jnp.exp(b_last + m0 - m_new), 0.0)
    a_f = expand(a * inv)
    w_f = [expand(sm * inv) for sm in sms]
    g_f = expand(g)
    d_f = expand(decay)
    gv = (g_f * v).astype(BF16)
    rowb = lax.broadcasted_iota(jnp.int32, (R, dh), 0) // T
    for hd in range(H):
        cs = slice(dh * hd, dh * (hd + 1))
        qh = q_bf[:, cs]
        kh = k_ref[:, cs]
        gvh = gv[:, cs]
        qc = jnp.zeros((R, dh), F32)
        for bb in range(MLSTM_S_BB):
            c_old = c_ref[bb, hd]
            qc = jnp.where(rowb == bb, _dot_nt(qh, c_old.astype(BF16)), qc)
            upd = _dot_tn(jnp.where(rowb == bb, gvh, jnp.zeros_like(gvh)), kh)
            c_out[bb, hd] = d_f[T * bb:T * bb + 1, cs] * c_old + upd
        h = a_f[:, cs] * qc
        for d in range(T):
            h = h + w_f[d][:, cs] * vs[d][:, cs]
        mh_ref[:, cs] = _head_norm_gate(h, nw_ref[:, cs], mo_ref[:, cs]).astype(BF16)
    gk = g_f * k
    nsum = gk
    for d in range(1, T):
        nsum = nsum + jnp.where(tmod_w >= d, shift(gk, d), 0.0)
    nrow_ref[...] = d_f * nrep_ref[...] + nsum
    mrow_ref[...] = m_t


def _mlstm_s(m_all, gates, n_rep, m_rep, brow, nw, state_c):
    bb = MLSTM_S_BB
    R = bb * DEC_SEQ
    dh = MLSTM_HEAD_DIM
    H = MLSTM_HEADS
    return pl.pallas_call(
        _mlstm_s_kernel,
        grid=(DEC_BATCH // bb,),
        in_specs=[
            pl.BlockSpec((R, MLSTM_WIDTH), lambda i: (i, 0)),
            pl.BlockSpec((R, MLSTM_WIDTH), lambda i: (i, 1)),
            pl.BlockSpec((R, MLSTM_WIDTH), lambda i: (i, 2)),
            pl.BlockSpec((R, MLSTM_WIDTH), lambda i: (i, 3)),
            pl.BlockSpec((R, LANES), lambda i: (i, 0)),
            pl.BlockSpec((R, MLSTM_WIDTH), lambda i: (i, 0)),
            pl.BlockSpec((R, LANES), lambda i: (i, 0)),
            pl.BlockSpec((1, LANES), lambda i: (0, 0)),
            pl.BlockSpec((1, MLSTM_WIDTH), lambda i: (0, 0)),
            pl.BlockSpec((bb, H, dh, dh), lambda i: (i, 0, 0, 0)),
        ],
        out_specs=[
            pl.BlockSpec((R, MLSTM_WIDTH), lambda i: (i, 0)),
            pl.BlockSpec((bb, H, dh, dh), lambda i: (i, 0, 0, 0)),
            pl.BlockSpec((R, MLSTM_WIDTH), lambda i: (i, 0)),
            pl.BlockSpec((R, LANES), lambda i: (i, 0)),
        ],
        out_shape=[
            jax.ShapeDtypeStruct((S_ROWS, MLSTM_WIDTH), BF16),
            jax.ShapeDtypeStruct((DEC_BATCH, H, dh, dh), F32),
            jax.ShapeDtypeStruct((S_ROWS, MLSTM_WIDTH), F32),
            jax.ShapeDtypeStruct((S_ROWS, LANES), F32),
        ],
        compiler_params=_cparams(("arbitrary",)),
        name="mlstm_s",
    )(m_all, m_all, m_all, m_all, gates, n_rep, m_rep, brow, nw, state_c)


def _outproj_kernel(att_ref, mh_ref, x_ref, g1_ref, sh_ref, sc_ref, nw_ref, wa_ref, wm_ref, x1_ref, h2_ref, *,
                    per_row):
    y = _dot(att_ref[...].astype(BF16), wa_ref[...]) + _dot(mh_ref[...], wm_ref[...])
    x1 = x_ref[...] + _mod_row(g1_ref, per_row) * y
    x1_ref[...] = x1
    r = lax.rsqrt(jnp.mean(x1 * x1, axis=-1, keepdims=True) + EPS)
    h2 = (x1 * r * nw_ref[...]) * (1.0 + _mod_row(sc_ref, per_row)) + _mod_row(sh_ref, per_row)
    h2_ref[...] = h2.astype(BF16)


def _outproj(att, mh, x, mod, mod_row_block, norm_w, w_att, w_m, *, tm, per_row):
    rows = x.shape[0]
    mod_rows = tm if per_row else 8
    mod_idx = (lambda i: i) if per_row else (lambda i: mod_row_block)
    kern = functools.partial(_outproj_kernel, per_row=per_row)
    return pl.pallas_call(
        kern,
        grid=(rows // tm,),
        in_specs=[
            pl.BlockSpec((tm, ATT_WIDTH), lambda i: (i, 0)),
            pl.BlockSpec((tm, MLSTM_WIDTH), lambda i: (i, 0)),
            pl.BlockSpec((tm, D_MODEL), lambda i: (i, 0)),
            pl.BlockSpec((mod_rows, D_MODEL), lambda i: (mod_idx(i), 2)),
            pl.BlockSpec((mod_rows, D_MODEL), lambda i: (mod_idx(i), 3)),
            pl.BlockSpec((mod_rows, D_MODEL), lambda i: (mod_idx(i), 4)),
            pl.BlockSpec((1, D_MODEL), lambda i: (0, 0)),
            pl.BlockSpec((ATT_WIDTH, D_MODEL), lambda i: (0, 0)),
            pl.BlockSpec((MLSTM_WIDTH, D_MODEL), lambda i: (0, 0)),
        ],
        out_specs=[
            pl.BlockSpec((tm, D_MODEL), lambda i: (i, 0)),
            pl.BlockSpec((tm, D_MODEL), lambda i: (i, 0)),
        ],
        out_shape=[
            jax.ShapeDtypeStruct((rows, D_MODEL), F32),
            jax.ShapeDtypeStruct((rows, D_MODEL), BF16),
        ],
        compiler_params=_cparams(("arbitrary",)),
        name="outproj_s" if per_row else "outproj_p",
    )(att, mh, x, mod, mod, mod, norm_w, w_att, w_m)


def _ffn_a_kernel(hp_ref, hs_ref, wg_ref, wu_ref, ap_ref, as_ref, wg_scr, wu_scr):
    i = pl.program_id(1)

    def swiglu(h_ref, a_ref):
        h = h_ref[...]
        g = _dot(h, wg_scr[...])
        u = _dot(h, wu_scr[...])
        a_ref[...] = (g * _sigmoid(g) * u).astype(BF16)

    @pl.when(i == 0)
    def _():
        wg_scr[...] = wg_ref[...].astype(BF16)
        wu_scr[...] = wu_ref[...].astype(BF16)
        swiglu(hs_ref, as_ref)

    @pl.when(i > 0)
    def _():
        swiglu(hp_ref, ap_ref)


def _ffn_a(h2_p, h2_s, w_gate, w_up, *, tm):
    rows_p, rows_s = h2_p.shape[0], h2_s.shape[0]
    tn = 512
    ni = rows_p // tm
    tile = lambda i: jnp.maximum(i - 1, 0)
    return pl.pallas_call(
        _ffn_a_kernel,
        grid=(D_FF // tn, ni + 1),
        in_specs=[
            pl.BlockSpec((tm, D_MODEL), lambda j, i: (tile(i), 0)),
            pl.BlockSpec((rows_s, D_MODEL), lambda j, i: (0, 0)),
            pl.BlockSpec((D_MODEL, tn), lambda j, i: (0, j)),
            pl.BlockSpec((D_MODEL, tn), lambda j, i: (0, j)),
        ],
        out_specs=[
            pl.BlockSpec((tm, tn), lambda j, i: (tile(i), j)),
            pl.BlockSpec((rows_s, tn), lambda j, i: (0, j)),
        ],
        out_shape=[
            jax.ShapeDtypeStruct((rows_p, D_FF), BF16),
            jax.ShapeDtypeStruct((rows_s, D_FF), BF16),
        ],
        scratch_shapes=[pltpu.VMEM((D_MODEL, tn), BF16), pltpu.VMEM((D_MODEL, tn), BF16)],
        compiler_params=_cparams(("arbitrary", "arbitrary")),
        name="ffn_a",
    )(h2_p, h2_s, w_gate, w_up)


def _ffn_b_kernel(a_ref, wd_ref, x1_ref, g2_ref, fw_ref, y_ref, *rest, per_row, emit_w):
    kk = pl.program_id(1)

    def weight_tile():
        wt = wd_ref[...].astype(BF16)
        if emit_w:
            rest[0][...] = wt
        return wt

    @pl.when(kk == 0)
    def _():
        y_ref[...] = _dot(a_ref[...], weight_tile())

    @pl.when(kk > 0)
    def _():
        y_ref[...] += _dot(a_ref[...], weight_tile())

    @pl.when(kk == pl.num_programs(1) - 1)
    def _():
        x2 = x1_ref[...] + _mod_row(g2_ref, per_row) * y_ref[...]
        y_ref[...] = x2 * lax.rsqrt(jnp.mean(x2 * x2, axis=-1, keepdims=True) + EPS) * fw_ref[...]


def _ffn_b(a, w_down, x1, mod, mod_row_block, final_w, *, tm, per_row, emit_w):
    rows = a.shape[0]
    tk = 512
    mod_rows = tm if per_row else 8
    mod_idx = (lambda i: i) if per_row else (lambda i: mod_row_block)
    kern = functools.partial(_ffn_b_kernel, per_row=per_row, emit_w=emit_w)
    w_spec = pl.BlockSpec((tk, D_MODEL), lambda i, k: (k, 0))
    y_spec = pl.BlockSpec((tm, D_MODEL), lambda i, k: (i, 0))
    y_shape = jax.ShapeDtypeStruct((rows, D_MODEL), F32)
    return pl.pallas_call(
        kern,
        grid=(rows // tm, D_FF // tk),
        in_specs=[
            pl.BlockSpec((tm, tk), lambda i, k: (i, k)),
            w_spec,
            pl.BlockSpec((tm, D_MODEL), lambda i, k: (i, 0)),
            pl.BlockSpec((mod_rows, D_MODEL), lambda i, k: (mod_idx(i), 5)),
            pl.BlockSpec((1, D_MODEL), lambda i, k: (0, 0)),
        ],
        out_specs=[y_spec, w_spec] if emit_w else y_spec,
        out_shape=[y_shape, jax.ShapeDtypeStruct((D_FF, D_MODEL), BF16)] if emit_w else y_shape,
        compiler_params=_cparams(("arbitrary", "arbitrary")),
        name="ffn_b_s" if per_row else "ffn_b_p",
    )(a, w_down, x1, mod, final_w)


def _rope_tables(pos):
    half = ROPE_DIM // 2
    inv = np.float32(ROPE_THETA) ** (-np.arange(0, ROPE_DIM, 2, dtype=np.float32) / np.float32(ROPE_DIM))
    d = np.arange(LANES) % ATT_HEAD_DIM
    ang = pos.astype(np.float32)[:, None] * inv[d % half][None, :].astype(np.float32)
    cos, sin = np.cos(ang), np.sin(ang)
    d = d[None, :]
    tables = (np.where(d < ROPE_DIM, cos, 1.0), np.where(d < half, -sin, 0.0),
              np.where((d >= half) & (d < ROPE_DIM), sin, 0.0))
    return tuple(jnp.asarray(t.astype(np.float32)) for t in tables)


def kernel(x_prompt, x_sample, cache_k_win, cache_v_win, state_C, state_n, state_m, c_prompt, c_sample,
           norm1_w, norm2_w, final_norm_w, w_ada, b_ada, w_in, b_ig, b_fg, attn_sinks, mh_norm_w,
           w_out, w_gate, w_up, w_down):
    assert w_in.shape[0] == 1, "single-layer trunk"
    T = DEC_SEQ
    xp = x_prompt[0]
    xs = x_sample.reshape(S_ROWS, D_MODEL)

    c_all = jnp.concatenate([jnp.repeat(c_sample, T, axis=0), c_prompt, jnp.zeros((15, D_MODEL), F32)], axis=0)
    mod = _ada(c_all, w_ada[0], b_ada)
    prompt_mod_block = S_ROWS // 8

    w_in_t = jnp.transpose(w_in[0])
    wq_t = (w_in_t[:ATT_WIDTH].reshape(ATT_KV_HEADS, ATT_GROUP, ATT_HEAD_DIM, D_MODEL)
            .transpose(1, 0, 2, 3).reshape(ATT_WIDTH, D_MODEL).astype(BF16))
    w_gates_t = w_in_t[MAIN_WIDTH:]
    wg = jnp.pad(w_gates_t, ((0, 16 - 2 * MLSTM_HEADS), (0, 0)))
    w_out_att = (w_out[0, :ATT_WIDTH].reshape(ATT_KV_HEADS, ATT_GROUP, ATT_HEAD_DIM, D_MODEL)
                 .transpose(1, 0, 2, 3).reshape(ATT_WIDTH, D_MODEL).astype(BF16))
    w_out_m = w_out[0, ATT_WIDTH:].astype(BF16)
    n1 = norm1_w.reshape(1, D_MODEL)
    n2 = norm2_w.reshape(1, D_MODEL)
    fw = final_norm_w.reshape(1, D_MODEL)
    nw = mh_norm_w.reshape(1, MLSTM_WIDTH)
    gate_bias = jnp.concatenate([b_ig[0], b_fg[0]])
    brow = jnp.pad(gate_bias, (0, LANES - 2 * MLSTM_HEADS)).reshape(1, LANES)
    bcol = jnp.broadcast_to(jnp.pad(gate_bias, (0, 16 - 2 * MLSTM_HEADS))[:, None], (16, MLSTM_CHUNK_P))

    rope_p = _rope_tables(np.arange(SEQ))
    rope_s = _rope_tables(np.tile(PAST_LEN + np.arange(T), DEC_BATCH))

    tm_p = 1024
    q_s, _, kv32_s, m_s, g_s, _, w_in_bf = _inproj(xs, mod, 0, n1, wq_t, w_in_t, wg, *rope_s,
                                                   tm=S_ROWS, per_row=True, emit_w=True)
    q_p, kv_p, kv32_p, m_p, g_p, gt_p = _inproj(xp, mod, prompt_mod_block, n1, wq_t, w_in_bf, wg, *rope_p,
                                                tm=tm_p, per_row=False, emit_w=False)

    sinks = attn_sinks[0]
    sink_col = jnp.broadcast_to(sinks.reshape(ATT_HEADS, 1, 1), (ATT_HEADS, 8, LANES)).reshape(128, LANES)
    ck = jnp.transpose(cache_k_win[0], (0, 2, 3, 1))
    cv = jnp.transpose(cache_v_win[0], (0, 2, 3, 1))
    att_s, kwin_s, vwin_s = _attn_s(sink_col, q_s, kv32_s, ck, cv)

    n_rep = jnp.repeat(state_n[0].reshape(DEC_BATCH, MLSTM_WIDTH), T, axis=0)
    m_rep = jnp.pad(jnp.repeat(state_m[0], T, axis=0), ((0, 0), (0, LANES - MLSTM_HEADS)))
    mh_s, c_s, nrow_s, mrow_s = _mlstm_s(m_s, g_s, n_rep, m_rep, brow, nw, state_C[0])

    x1_p, h2_p, c_p, n_p, mm_p = _mix_p(sinks, q_p, kv_p, m_p, g_p, gt_p, brow, bcol, nw, xp, mod, prompt_mod_block,
                                        n2, w_out_att, w_out_m)
    x1_s, h2_s = _outproj(att_s, mh_s, xs, mod, 0, n2, w_out_att, w_out_m, tm=S_ROWS, per_row=True)
    a_p, a_s = _ffn_a(h2_p, h2_s, w_gate[0], w_up[0], tm=tm_p)
    y_s, w_down_bf = _ffn_b(a_s, w_down[0], x1_s, mod, 0, fw, tm=S_ROWS, per_row=True, emit_w=True)
    y_p = _ffn_b(a_p, w_down_bf, x1_p, mod, prompt_mod_block, fw, tm=tm_p, per_row=False, emit_w=False)

    kv_shape = (1, 1, WINDOW, ATT_KV_HEADS, ATT_HEAD_DIM)
    kv_last = kv32_p[tm_p - WINDOW:]
    dh = MLSTM_HEAD_DIM
    return (
        y_p.reshape(1, SEQ, D_MODEL),
        y_s.reshape(DEC_BATCH, T, D_MODEL),
        kv_last[:, :KV_WIDTH].reshape(kv_shape),
        kv_last[:, KV_WIDTH:].reshape(kv_shape),
        c_p.reshape(1, 1, MLSTM_HEADS, dh, dh),
        n_p[:MLSTM_HEADS].reshape(1, 1, MLSTM_HEADS, dh),
        mm_p[:MLSTM_HEADS, 0].reshape(1, 1, MLSTM_HEADS),
        jnp.transpose(kwin_s, (0, 3, 1, 2))[None],
        jnp.transpose(vwin_s, (0, 3, 1, 2))[None],
        c_s.reshape(1, DEC_BATCH, MLSTM_HEADS, dh, dh),
        nrow_s[T - 1::T].reshape(1, DEC_BATCH, MLSTM_HEADS, dh),
        mrow_s[T - 1::T, :MLSTM_HEADS].reshape(1, DEC_BATCH, MLSTM_HEADS),
    )
```

```python
import functools

import jax
import jax.numpy as jnp
import numpy as np
from jax import lax
from jax.experimental import pallas as pl
from jax.experimental.pallas import tpu as pltpu

F32 = jnp.float32
BF16 = jnp.bfloat16

D_MODEL = 2048
SEQ = 8192
DEC_BATCH = 128
DEC_SEQ = 4
S_ROWS = DEC_BATCH * DEC_SEQ
PAST_LEN = 16384
ATT_HEADS = 16
ATT_KV_HEADS = 4
ATT_GROUP = 4
ATT_HEAD_DIM = 64
WINDOW = 128
ROPE_THETA = 500000.0
ROPE_DIM = 16
MLSTM_HEADS = 4
MLSTM_HEAD_DIM = 256
ATT_WIDTH = 1024
KV_WIDTH = 256
MLSTM_WIDTH = 1024
MAIN_WIDTH = ATT_WIDTH + 2 * KV_WIDTH + 4 * MLSTM_WIDTH
D_FF = 5632
N_MOD = 6
EPS = 1e-6

LANES = 128
MLSTM_CHUNK_P = 256
VMEM_LIMIT = 56 * 1024 * 1024

NT_DIMS = (((1,), (1,)), ((), ()))
TN_DIMS = (((0,), (0,)), ((), ()))
HI = lax.Precision.HIGHEST


def _cparams(sem):
    return pltpu.CompilerParams(dimension_semantics=sem, vmem_limit_bytes=VMEM_LIMIT)


def _dot(a, b):
    return jnp.dot(a, b, preferred_element_type=F32)


def _dot_nt(a, b):
    return lax.dot_general(a, b, NT_DIMS, preferred_element_type=F32)


def _dot_tn(a, b):
    return lax.dot_general(a, b, TN_DIMS, preferred_element_type=F32)


def _sigmoid(x):
    return 1.0 / (1.0 + jnp.exp(-x))


def _log_sigmoid(x):
    return jnp.minimum(x, 0.0) - jnp.log(1.0 + jnp.exp(-jnp.abs(x)))


def _mod_row(ref, per_row):
    return ref[...] if per_row else ref[0:1, :]


def _ada_kernel(c_ref, w_ref, b_ref, o_ref, s_scr):
    @pl.when(pl.program_id(0) == 0)
    def _():
        c = c_ref[...]
        s_scr[...] = (c * _sigmoid(c)).astype(BF16)

    o_ref[...] = _dot(s_scr[...], w_ref[...].astype(BF16)) + b_ref[...]


def _ada(c_all, w_ada, b_ada):
    m = c_all.shape[0]
    n = w_ada.shape[1]
    tn = 1024
    return pl.pallas_call(
        _ada_kernel,
        grid=(n // tn,),
        in_specs=[
            pl.BlockSpec((m, D_MODEL), lambda j: (0, 0)),
            pl.BlockSpec((D_MODEL, tn), lambda j: (0, j)),
            pl.BlockSpec((1, tn), lambda j: (0, j)),
        ],
        out_specs=pl.BlockSpec((m, tn), lambda j: (0, j)),
        out_shape=jax.ShapeDtypeStruct((m, n), F32),
        scratch_shapes=[pltpu.VMEM((m, D_MODEL), BF16)],
        compiler_params=_cparams(("arbitrary",)),
        name="ada",
    )(c_all, w_ada, b_ada)


def _rope_store(acc, cos, sa, sb, out_ref, ncols, scale):
    for c in range(ncols // LANES):
        xc = acc[:, LANES * c:LANES * (c + 1)]
        rot = xc * cos + pltpu.roll(xc, LANES - 8, 1) * sa + pltpu.roll(xc, 8, 1) * sb
        if scale != 1.0:
            rot = rot * scale
        out_ref[:, LANES * c:LANES * (c + 1)] = rot.astype(out_ref.dtype)


def _inproj_kernel(x_ref, sh_ref, sc_ref, nw_ref, wq_ref, win_ref, wg_ref, cos_ref, sa_ref, sb_ref,
                   q_ref, kv_ref, kv32_ref, m_ref, g_ref, gt_ref, *rest, per_row, emit_w):
    h_scr = rest[-1]
    j = pl.program_id(1)

    def weight_tile():
        wt = win_ref[...].astype(BF16)
        if emit_w:
            rest[0][...] = wt
        return wt

    @pl.when(j == 0)
    def _():
        x = x_ref[...]
        r = lax.rsqrt(jnp.mean(x * x, axis=-1, keepdims=True) + EPS)
        gain = nw_ref[...] * (1.0 + _mod_row(sc_ref, per_row))
        h = (x * r * gain + _mod_row(sh_ref, per_row)).astype(BF16)
        h_scr[...] = h
        wg = jnp.concatenate([wg_ref[...].astype(BF16), jnp.zeros((LANES - 16, D_MODEL), BF16)], axis=0)
        g = _dot_nt(h, wg)
        g_ref[...] = g
        gt_ref[...] = g.T[0:16, :]

    @pl.when(j < 2)
    def _():
        acc = _dot_nt(h_scr[...], wq_ref[...])
        _rope_store(acc, cos_ref[...], sa_ref[...], sb_ref[...], q_ref, 512, ATT_HEAD_DIM ** -0.5)

    @pl.when(j == 2)
    def _():
        acc = _dot_nt(h_scr[...], weight_tile())
        _rope_store(acc, cos_ref[...], sa_ref[...], sb_ref[...], kv32_ref, KV_WIDTH, 1.0)
        kv32_ref[:, KV_WIDTH:] = acc[:, KV_WIDTH:]
        kv_ref[...] = kv32_ref[...].astype(BF16)

    @pl.when(j > 2)
    def _():
        m_ref[...] = _dot_nt(h_scr[...], weight_tile()).astype(BF16)


def _inproj(x, mod, mod_row_block, norm_w, wq_t, w_in_t, wg, cos, sa, sb, *, tm, per_row, emit_w):
    rows = x.shape[0]
    tn = 512
    nj = MAIN_WIDTH // tn
    mod_rows = tm if per_row else 8
    mod_idx = (lambda i: i) if per_row else (lambda i: mod_row_block)
    kern = functools.partial(_inproj_kernel, per_row=per_row, emit_w=emit_w)
    copy_spec = pl.BlockSpec((tn, D_MODEL), lambda i, j: (jnp.maximum(j, 2) - 2, 0))
    w_spec = pl.BlockSpec((tn, D_MODEL), lambda i, j: (jnp.maximum(j, 2), 0)) if emit_w else copy_spec
    extra_specs = [copy_spec] if emit_w else []
    extra_shapes = [jax.ShapeDtypeStruct((MAIN_WIDTH - ATT_WIDTH, D_MODEL), BF16)] if emit_w else []
    return pl.pallas_call(
        kern,
        grid=(rows // tm, nj),
        in_specs=[
            pl.BlockSpec((tm, D_MODEL), lambda i, j: (jnp.minimum(i + jnp.minimum(j, 1), rows // tm - 1), 0)),
            pl.BlockSpec((mod_rows, D_MODEL), lambda i, j: (mod_idx(i), 0)),
            pl.BlockSpec((mod_rows, D_MODEL), lambda i, j: (mod_idx(i), 1)),
            pl.BlockSpec((1, D_MODEL), lambda i, j: (0, 0)),
            pl.BlockSpec((tn, D_MODEL), lambda i, j: (jnp.minimum(j, 1), 0)),
            w_spec,
            pl.BlockSpec((16, D_MODEL), lambda i, j: (0, 0)),
            pl.BlockSpec((tm, LANES), lambda i, j: (i, 0)),
            pl.BlockSpec((tm, LANES), lambda i, j: (i, 0)),
            pl.BlockSpec((tm, LANES), lambda i, j: (i, 0)),
        ],
        out_specs=[
            pl.BlockSpec((tm, tn), lambda i, j: (i, jnp.minimum(j, 1))),
            pl.BlockSpec((tm, tn), lambda i, j: (i, 0)),
            pl.BlockSpec((tm, tn), lambda i, j: (0, 0)),
            pl.BlockSpec((tm, tn), lambda i, j: (i, jnp.clip(j - 3, 0, 7))),
            pl.BlockSpec((tm, LANES), lambda i, j: (i, 0)),
            pl.BlockSpec((16, tm), lambda i, j: (0, i)),
        ] + extra_specs,
        out_shape=[
            jax.ShapeDtypeStruct((rows, ATT_WIDTH), BF16),
            jax.ShapeDtypeStruct((rows, 2 * KV_WIDTH), BF16),
            jax.ShapeDtypeStruct((tm, 2 * KV_WIDTH), F32),
            jax.ShapeDtypeStruct((rows, 4 * MLSTM_WIDTH), BF16),
            jax.ShapeDtypeStruct((rows, LANES), F32),
            jax.ShapeDtypeStruct((16, rows), F32),
        ] + extra_shapes,
        scratch_shapes=[pltpu.VMEM((tm, D_MODEL), BF16)],
        compiler_params=_cparams(("arbitrary", "arbitrary")),
        name="inproj_s" if per_row else "inproj_p",
    )(x, mod, mod, norm_w, wq_t, w_in_t, wg, cos, sa, sb)


def _attn_block(sink_ref, q_ref, row0, kv2, allowed, store):
    w = WINDOW
    grp = ATT_GROUP
    rows = grp * w
    member = lax.broadcasted_iota(jnp.int32, (rows, 1), 0) // w
    low = lax.broadcasted_iota(jnp.int32, (2 * w, LANES), 1) < ATT_HEAD_DIM
    low_o = lax.broadcasted_iota(jnp.int32, (rows, LANES), 1) < ATT_HEAD_DIM
    key_row = lax.broadcasted_iota(jnp.int32, (4 * w, LANES), 0)
    key_lane = lax.broadcasted_iota(jnp.int32, (4 * w, LANES), 1)
    ones_bd = (((key_row < 2 * w) & (key_lane < ATT_HEAD_DIM))
               | ((key_row >= 2 * w) & (key_lane >= ATT_HEAD_DIM))).astype(BF16)
    zero = jnp.zeros((2 * w, LANES), BF16)
    for cp in range(2):
        k128 = kv2[:, LANES * cp:LANES * (cp + 1)]
        v128 = kv2[:, KV_WIDTH + LANES * cp:KV_WIDTH + LANES * (cp + 1)]
        kbd = jnp.concatenate([jnp.where(low, k128, zero), jnp.where(low, zero, k128)], axis=0)
        vbd = jnp.concatenate([jnp.where(low, v128, zero), jnp.where(low, zero, v128)], axis=0)
        v_aug = jnp.concatenate([vbd, ones_bd], axis=1)
        q4 = jnp.concatenate([q_ref[row0:row0 + w, 256 * r + LANES * cp:256 * r + LANES * (cp + 1)]
                              for r in range(grp)], axis=0)
        s = _dot_nt(q4, kbd)
        es, tails = [], []
        for half in range(2):
            sh = jnp.where(allowed, s[:, 2 * w * half:2 * w * (half + 1)], -jnp.inf)
            head0 = (2 * cp + half) * grp
            sink = jnp.full((rows, 1), sink_ref[head0], F32)
            for r in range(1, grp):
                sink = jnp.where(member == r, sink_ref[head0 + r], sink)
            m = jnp.maximum(jnp.max(sh, axis=-1, keepdims=True), sink)
            es.append(jnp.exp(sh - m).astype(BF16))
            tails.append(jnp.exp(sink - m))
        oa = _dot(jnp.concatenate(es, axis=1), v_aug)
        l = oa[:, LANES:] + jnp.where(low_o, tails[0], tails[1])
        o = (oa[:, :LANES] / l).astype(BF16)
        for r in range(grp):
            store(256 * r + LANES * cp, o[w * r:w * (r + 1)])


MIX_TM = 512


def _mix_p_kernel(sink_ref, q_ref, kvp_ref, kvc_ref, mq_ref, mk_ref, mv_ref, mo_ref, g_ref, gt_ref, brow_ref, bcol_ref,
                  mnw_ref, x_ref, g1_ref, sh_ref, sc_ref, nw_ref, wa_ref, wm_ref,
                  x1_ref, h2_ref, c_out, n_out, m_out, att_scr, mh_scr, c_scr, n_scr, m_scr):
    s = pl.program_id(0)
    w = WINDOW
    tiles = pl.num_programs(0) - 1
    slot = s % 2

    @pl.when(s == 0)
    def _():
        c_scr[...] = jnp.zeros_like(c_scr)
        n_scr[...] = jnp.zeros_like(n_scr)
        m_scr[...] = jnp.zeros_like(m_scr)

    def project():
        y = _dot(att_scr[1 - slot], wa_ref[...]) + _dot(mh_scr[1 - slot], wm_ref[...])
        x1 = x_ref[...] + g1_ref[0:1, :] * y
        x1_ref[...] = x1
        r = lax.rsqrt(jnp.mean(x1 * x1, axis=-1, keepdims=True) + EPS)
        gain = nw_ref[...] * (1.0 + sc_ref[0:1, :])
        h2_ref[...] = (x1 * r * gain + sh_ref[0:1, :]).astype(BF16)

    def mixers():
        rows = ATT_GROUP * w
        qi = lax.broadcasted_iota(jnp.int32, (rows, 2 * w), 0) % w
        kj = lax.broadcasted_iota(jnp.int32, (rows, 2 * w), 1)
        first_off = jnp.where(s > 0, 0, 4 * w)
        causal = (kj >= w) & (kj - w <= qi)
        for blk in range(MIX_TM // w):
            prev = kvp_ref[...] if blk == 0 else kvc_ref[w * (blk - 1):w * blk, :]
            kv2 = jnp.concatenate([prev, kvc_ref[w * blk:w * (blk + 1), :]], axis=0)
            allowed = ((kj < w) & (kj > qi + (first_off if blk == 0 else 0))) | causal

            def store(c0, val, blk=blk):
                att_scr[slot, w * blk:w * (blk + 1), c0:c0 + LANES] = val

            _attn_block(sink_ref, q_ref, w * blk, kv2, allowed, store)

        for ch in range(MIX_TM // MLSTM_CHUNK_P):
            r0 = MLSTM_CHUNK_P * ch

            def store_mh(cs, val, r0=r0):
                mh_scr[slot, r0:r0 + MLSTM_CHUNK_P, cs] = val

            _mlstm_chunk(mq_ref, mk_ref, mv_ref, mo_ref, g_ref, gt_ref, brow_ref, bcol_ref, mnw_ref,
                         c_scr, n_scr, m_scr, r0, store_mh)

    @pl.when(s == 0)
    def _():
        mixers()

    @pl.when((s > 0) & (s < tiles))
    def _():
        project()
        mixers()

    @pl.when(s == tiles)
    def _():
        project()
        c_out[...] = c_scr[...]
        n_out[...] = n_scr[...]
        m_out[...] = m_scr[...]


def _mix_p(sinks, q, kv, m_all, gates, gates_t, brow, bcol, mnw, x, mod, mod_row_block, norm_w, w_att, w_m):
    tm = MIX_TM
    tiles = SEQ // tm
    bpt = tm // WINDOW
    dh = MLSTM_HEAD_DIM
    att_tile = lambda s: jnp.minimum(s, tiles - 1)
    out_tile = lambda s: jnp.maximum(s - 1, 0)
    m_spec = lambda col: pl.BlockSpec((tm, MLSTM_WIDTH), lambda s: (att_tile(s), col))
    return pl.pallas_call(
        _mix_p_kernel,
        grid=(tiles + 1,),
        in_specs=[
            pl.BlockSpec(memory_space=pltpu.SMEM),
            pl.BlockSpec((tm, ATT_WIDTH), lambda s: (att_tile(s), 0)),
            pl.BlockSpec((WINDOW, 2 * KV_WIDTH), lambda s: (jnp.maximum(bpt * att_tile(s) - 1, 0), 0)),
            pl.BlockSpec((tm, 2 * KV_WIDTH), lambda s: (att_tile(s), 0)),
            m_spec(0), m_spec(1), m_spec(2), m_spec(3),
            pl.BlockSpec((tm, LANES), lambda s: (att_tile(s), 0)),
            pl.BlockSpec((16, tm), lambda s: (0, att_tile(s))),
            pl.BlockSpec((1, LANES), lambda s: (0, 0)),
            pl.BlockSpec((16, MLSTM_CHUNK_P), lambda s: (0, 0)),
            pl.BlockSpec((1, MLSTM_WIDTH), lambda s: (0, 0)),
            pl.BlockSpec((tm, D_MODEL), lambda s: (out_tile(s), 0)),
            pl.BlockSpec((8, D_MODEL), lambda s: (mod_row_block, 2)),
            pl.BlockSpec((8, D_MODEL), lambda s: (mod_row_block, 3)),
            pl.BlockSpec((8, D_MODEL), lambda s: (mod_row_block, 4)),
            pl.BlockSpec((1, D_MODEL), lambda s: (0, 0)),
            pl.BlockSpec((ATT_WIDTH, D_MODEL), lambda s: (0, 0)),
            pl.BlockSpec((MLSTM_WIDTH, D_MODEL), lambda s: (0, 0)),
        ],
        out_specs=[
            pl.BlockSpec((tm, D_MODEL), lambda s: (out_tile(s), 0)),
            pl.BlockSpec((tm, D_MODEL), lambda s: (out_tile(s), 0)),
            pl.BlockSpec((MLSTM_HEADS, dh, dh), lambda s: (0, 0, 0)),
            pl.BlockSpec((8, dh), lambda s: (0, 0)),
            pl.BlockSpec((8, LANES), lambda s: (0, 0)),
        ],
        out_shape=[
            jax.ShapeDtypeStruct((SEQ, D_MODEL), F32),
            jax.ShapeDtypeStruct((SEQ, D_MODEL), BF16),
            jax.ShapeDtypeStruct((MLSTM_HEADS, dh, dh), F32),
            jax.ShapeDtypeStruct((8, dh), F32),
            jax.ShapeDtypeStruct((8, LANES), F32),
        ],
        scratch_shapes=[
            pltpu.VMEM((2, tm, ATT_WIDTH), BF16),
            pltpu.VMEM((2, tm, MLSTM_WIDTH), BF16),
            pltpu.VMEM((MLSTM_HEADS, dh, dh), F32),
            pltpu.VMEM((8, dh), F32),
            pltpu.VMEM((8, LANES), F32),
        ],
        compiler_params=_cparams(("arbitrary",)),
        name="mix_p",
    )(sinks, q, kv, kv, m_all, m_all, m_all, m_all, gates, gates_t, brow, bcol, mnw, x, mod, mod, mod, norm_w,
      w_att, w_m)


ATT_S_BB = 16


def _attn_s_kernel(sink_ref, q_ref, kv32_ref, ck_ref, cv_ref, o_ref, ko_ref, vo_ref, q32_scr):
    t_new = DEC_SEQ
    w = WINDOW
    q32_scr[...] = q_ref[...].astype(F32)
    rows = 4 * 4 * 8
    row = lax.broadcasted_iota(jnp.int32, (rows, w), 0)
    slot = lax.broadcasted_iota(jnp.int32, (rows, w), 1)
    t_row = row % t_new
    second = (row % 8) >= t_new
    win_ok = (slot < w - t_new) | (slot - (w - t_new) <= t_row)
    old_ok = (slot >= 1) & (slot < t_new) & (slot > t_row)
    lane256 = lax.broadcasted_iota(jnp.int32, (32, 2 * LANES), 1)
    sink = sink_ref[...][:, 0:1]
    kv_new = jnp.concatenate([kv32_ref[...], jnp.zeros((w - ATT_S_BB * t_new, 2 * KV_WIDTH), F32)], axis=0)
    kv_t = kv_new.T
    new_slot = lax.broadcasted_iota(jnp.int32, (KV_WIDTH, w), 1) >= w - t_new
    for b in range(ATT_S_BB):
        cols = pltpu.roll(kv_t, w - t_new - t_new * b, 1)
        k_shift = pltpu.roll(ck_ref[b].reshape(KV_WIDTH, w), w - t_new, 1)
        v_shift = pltpu.roll(cv_ref[b].reshape(KV_WIDTH, w), w - t_new, 1)
        ko_ref[b] = jnp.where(new_slot, cols[:KV_WIDTH], k_shift).reshape(ATT_KV_HEADS, ATT_HEAD_DIM, w)
        vo_ref[b] = jnp.where(new_slot, cols[KV_WIDTH:], v_shift).reshape(ATT_KV_HEADS, ATT_HEAD_DIM, w)
    for pair in range(ATT_S_BB // 2):
        b0, b1 = 2 * pair, 2 * pair + 1
        q32 = jnp.concatenate([q32_scr[8 * pair:8 * (pair + 1), 256 * r:256 * (r + 1)] for r in range(ATT_GROUP)],
                              axis=0)
        qpad = jnp.concatenate(
            [jnp.where((lane256 // ATT_HEAD_DIM) == g, q32, 0.0) for g in range(ATT_KV_HEADS)], axis=0).astype(BF16)
        kw = [ko_ref[b].reshape(KV_WIDTH, w).astype(BF16) for b in (b0, b1)]
        vw = [vo_ref[b].reshape(KV_WIDTH, w).astype(BF16) for b in (b0, b1)]
        kc = [ck_ref[b].reshape(KV_WIDTH, w).astype(BF16) for b in (b0, b1)]
        vc = [cv_ref[b].reshape(KV_WIDTH, w).astype(BF16) for b in (b0, b1)]
        s_w = jnp.where(second, _dot(qpad, kw[1]), _dot(qpad, kw[0]))
        s_c = jnp.where(second, _dot(qpad, kc[1]), _dot(qpad, kc[0]))
        s_w = jnp.where(win_ok, s_w, -jnp.inf)
        s_c = jnp.where(old_ok, s_c, -jnp.inf)
        m = jnp.maximum(jnp.maximum(jnp.max(s_w, axis=-1, keepdims=True), jnp.max(s_c, axis=-1, keepdims=True)), sink)
        e_w = jnp.exp(s_w - m)
        e_c = jnp.exp(s_c - m)
        l = jnp.sum(e_w, axis=-1, keepdims=True) + jnp.sum(e_c, axis=-1, keepdims=True) + jnp.exp(sink - m)
        p_w = e_w / l
        p_c = e_c / l
        zero = jnp.zeros_like(p_w)
        o = (_dot_nt(jnp.where(second, zero, p_w).astype(BF16), vw[0])
             + _dot_nt(jnp.where(second, p_w, zero).astype(BF16), vw[1])
             + _dot_nt(jnp.where(second, zero, p_c).astype(BF16), vc[0])
             + _dot_nt(jnp.where(second, p_c, zero).astype(BF16), vc[1]))
        o32 = jnp.zeros((32, 2 * LANES), F32)
        for g in range(ATT_KV_HEADS):
            o32 = jnp.where((lane256 // ATT_HEAD_DIM) == g, o[32 * g:32 * (g + 1), :], o32)
        for r in range(ATT_GROUP):
            o_ref[8 * pair:8 * (pair + 1), 256 * r:256 * (r + 1)] = o32[8 * r:8 * (r + 1), :]


def _attn_s(sink_col, q, kv32, ck, cv):
    bb = ATT_S_BB
    rows = bb * DEC_SEQ
    cache_block = (bb, ATT_KV_HEADS, ATT_HEAD_DIM, WINDOW)
    cache_shape = (DEC_BATCH, ATT_KV_HEADS, ATT_HEAD_DIM, WINDOW)
    return pl.pallas_call(
        _attn_s_kernel,
        grid=(DEC_BATCH // bb,),
        in_specs=[
            pl.BlockSpec((128, LANES), lambda i: (0, 0)),
            pl.BlockSpec((rows, ATT_WIDTH), lambda i: (i, 0)),
            pl.BlockSpec((rows, 2 * KV_WIDTH), lambda i: (i, 0)),
            pl.BlockSpec(cache_block, lambda i: (i, 0, 0, 0)),
            pl.BlockSpec(cache_block, lambda i: (i, 0, 0, 0)),
        ],
        out_specs=[
            pl.BlockSpec((rows, ATT_WIDTH), lambda i: (i, 0)),
            pl.BlockSpec(cache_block, lambda i: (i, 0, 0, 0)),
            pl.BlockSpec(cache_block, lambda i: (i, 0, 0, 0)),
        ],
        out_shape=[
            jax.ShapeDtypeStruct((S_ROWS, ATT_WIDTH), F32),
            jax.ShapeDtypeStruct(cache_shape, F32),
            jax.ShapeDtypeStruct(cache_shape, F32),
        ],
        scratch_shapes=[pltpu.VMEM((rows, ATT_WIDTH), F32)],
        compiler_params=_cparams(("arbitrary",)),
        name="attn_s",
    )(sink_col, q, kv32, ck, cv)


def _head_norm_gate(h, nw, mo):
    hn = h * lax.rsqrt(jnp.mean(h * h, axis=-1, keepdims=True) + EPS) * nw
    return hn * _sigmoid(mo.astype(F32))


def _mlstm_chunk(q_ref, k_ref, v_ref, mo_ref, g_ref, gt_ref, brow_ref, bcol_ref, nw_ref, c_scr, n_scr, m_scr,
                 r0, store):
    L = MLSTM_CHUNK_P
    dh = MLSTM_HEAD_DIM
    rs = slice(r0, r0 + L)
    ti = lax.broadcasted_iota(jnp.int32, (L, L), 0)
    si = lax.broadcasted_iota(jnp.int32, (L, L), 1)
    causal = si <= ti
    tri = causal.astype(F32)
    tri_t = (ti <= si).astype(F32)
    gates = g_ref[rs, :] + brow_ref[...]
    gates_t = gt_ref[:, rs] + bcol_ref[...]
    b_col = jnp.dot(tri, _log_sigmoid(gates), precision=HI, preferred_element_type=F32)
    b_row = jnp.dot(_log_sigmoid(gates_t), tri_t, precision=HI, preferred_element_type=F32)
    for hd in range(MLSTM_HEADS):
        cs = slice(dh * hd, dh * (hd + 1))
        b_c = b_col[:, MLSTM_HEADS + hd:MLSTM_HEADS + hd + 1]
        li_c = gates[:, hd:hd + 1]
        b_r = b_row[MLSTM_HEADS + hd:MLSTM_HEADS + hd + 1, :]
        li_r = gates_t[hd:hd + 1, :]
        dm = jnp.where(causal, b_c - b_r + li_r, -jnp.inf)
        m_prev = m_scr[hd:hd + 1, 0:1]
        m_inter = b_c + m_prev
        m_t = jnp.maximum(m_inter, jnp.max(dm, axis=-1, keepdims=True))
        q = q_ref[rs, cs] * (dh ** -0.5)
        k = k_ref[rs, cs]
        v = v_ref[rs, cs]
        sm = _dot_nt(q, k) * jnp.exp(dm - m_t)
        a = jnp.exp(m_inter - m_t)
        c_old = c_scr[hd]
        n_old = n_scr[hd:hd + 1, :]
        num = a * _dot_nt(q, c_old.astype(BF16)) + _dot(sm.astype(BF16), v)
        qn = jnp.sum(q.astype(F32) * n_old, axis=-1, keepdims=True)
        den = a * qn + jnp.sum(sm, axis=-1, keepdims=True)
        h = num / jnp.maximum(jnp.abs(den), jnp.exp(-m_t))
        store(cs, _head_norm_gate(h, nw_ref[:, cs], mo_ref[rs, cs]).astype(BF16))
        m_new = m_t[L - 1:L, :]
        b_last = b_c[L - 1:L, :]
        g = jnp.exp(b_last - b_c + li_c - m_new)
        decay = jnp.exp(b_last + m_prev - m_new)
        gv = (g * v.astype(F32)).astype(BF16)
        c_scr[hd] = decay * c_old + _dot_tn(gv, k)
        n_scr[hd:hd + 1, :] = decay * n_old + jnp.sum(g * k.astype(F32), axis=0, keepdims=True)
        m_scr[hd:hd + 1, :] = jnp.broadcast_to(m_new, (1, LANES))


MLSTM_S_BB = 8


def _mlstm_s_kernel(q_ref, k_ref, v_ref, mo_ref, g_ref, nrep_ref, mrep_ref, brow_ref, nw_ref, c_ref,
                    mh_ref, c_out, nrow_ref, mrow_ref):
    T = DEC_SEQ
    R = MLSTM_S_BB * T
    H = MLSTM_HEADS
    dh = MLSTM_HEAD_DIM

    def shift(x, d):
        return pltpu.roll(x, d, 0)

    lanes = lax.broadcasted_iota(jnp.int32, (R, LANES), 1)
    tmod = lax.broadcasted_iota(jnp.int32, (R, LANES), 0) % T
    tmod_w = lax.broadcasted_iota(jnp.int32, (R, MLSTM_WIDTH), 0) % T
    head_ok = lanes < H
    gates = g_ref[...] + brow_ref[...]
    li = jnp.where(head_ok, gates, 0.0)
    lf = jnp.where(head_ok, pltpu.roll(_log_sigmoid(gates), LANES - H, 1), 0.0)
    bcum = lf
    for d in range(1, T):
        bcum = bcum + jnp.where(tmod >= d, shift(lf, d), 0.0)
    m0 = mrep_ref[...]
    m_inter = bcum + m0
    dms = [li] + [jnp.where(tmod >= d, bcum - shift(bcum, d) + shift(li, d), -jnp.inf) for d in range(1, T)]
    m_t = m_inter
    for dm in dms:
        m_t = jnp.maximum(m_t, dm)
    a = jnp.exp(m_inter - m_t)
    ws = [jnp.exp(dm - m_t) for dm in dms]

    q_bf = q_ref[...] * (dh ** -0.5)
    q = q_bf.astype(F32)
    k = k_ref[...].astype(F32)
    v = v_ref[...].astype(F32)
    seg = (lax.broadcasted_iota(jnp.int32, (MLSTM_WIDTH, LANES), 0) // dh
           == lax.broadcasted_iota(jnp.int32, (MLSTM_WIDTH, LANES), 1)).astype(F32)
    ex = (lax.broadcasted_iota(jnp.int32, (LANES, MLSTM_WIDTH), 1) // dh
          == lax.broadcasted_iota(jnp.int32, (LANES, MLSTM_WIDTH), 0)).astype(F32)

    def segsum(x):
        return jnp.dot(x, seg, precision=HI, preferred_element_type=F32)

    def expand(x):
        return jnp.dot(x, ex, precision=HI, preferred_element_type=F32)

    ks = [k] + [shift(k, d) for d in range(1, T)]
    vs = [v] + [shift(v, d) for d in range(1, T)]
    sms = [segsum(q * ks[d]) * ws[d] for d in range(T)]
    den = a * segsum(q * nrep_ref[...])
    for sm in sms:
        den = den + sm
    inv = 1.0 / jnp.maximum(jnp.abs(den), jnp.exp(-m_t))

    def last(x):
        out = jnp.zeros_like(x)
        for jj in range(T):
            out = jnp.where(tmod == T - 1 - jj, x if jj == 0 else pltpu.roll(x, R - jj, 0), out)
        return out

    m_new = last(m_t)
    b_last = last(bcum)
    g = jnp.where(head_ok, jnp.exp(b_last - bcum + li - m_new), 0.0)
    decay = jnp.where(head_ok, jnp.exp(b_last + m0 - m_new), 0.0)
    a_f = expand(a * inv)
    w_f = [expand(sm * inv) for sm in sms]
    g_f = expand(g)
    d_f = expand(decay)
    gv = (g_f * v).astype(BF16)
    rowb = lax.broadcasted_iota(jnp.int32, (R, dh), 0) // T
    for hd in range(H):
        cs = slice(dh * hd, dh * (hd + 1))
        qh = q_bf[:, cs]
        kh = k_ref[:, cs]
        gvh = gv[:, cs]
        qc = jnp.zeros((R, dh), F32)
        for bb in range(MLSTM_S_BB):
            c_old = c_ref[bb, hd]
            qc = jnp.where(rowb == bb, _dot_nt(qh, c_old.astype(BF16)), qc)
            upd = _dot_tn(jnp.where(rowb == bb, gvh, jnp.zeros_like(gvh)), kh)
            c_out[bb, hd] = d_f[T * bb:T * bb + 1, cs] * c_old + upd
        h = a_f[:, cs] * qc
        for d in range(T):
            h = h + w_f[d][:, cs] * vs[d][:, cs]
        mh_ref[:, cs] = _head_norm_gate(h, nw_ref[:, cs], mo_ref[:, cs]).astype(BF16)
    gk = g_f * k
    nsum = gk
    for d in range(1, T):
        nsum = nsum + jnp.where(tmod_w >= d, shift(gk, d), 0.0)
    nrow_ref[...] = d_f * nrep_ref[...] + nsum
    mrow_ref[...] = m_t


def _mlstm_s(m_all, gates, n_rep, m_rep, brow, nw, state_c):
    bb = MLSTM_S_BB
    R = bb * DEC_SEQ
    dh = MLSTM_HEAD_DIM
    H = MLSTM_HEADS
    return pl.pallas_call(
        _mlstm_s_kernel,
        grid=(DEC_BATCH // bb,),
        in_specs=[
            pl.BlockSpec((R, MLSTM_WIDTH), lambda i: (i, 0)),
            pl.BlockSpec((R, MLSTM_WIDTH), lambda i: (i, 1)),
            pl.BlockSpec((R, MLSTM_WIDTH), lambda i: (i, 2)),
            pl.BlockSpec((R, MLSTM_WIDTH), lambda i: (i, 3)),
            pl.BlockSpec((R, LANES), lambda i: (i, 0)),
            pl.BlockSpec((R, MLSTM_WIDTH), lambda i: (i, 0)),
            pl.BlockSpec((R, LANES), lambda i: (i, 0)),
            pl.BlockSpec((1, LANES), lambda i: (0, 0)),
            pl.BlockSpec((1, MLSTM_WIDTH), lambda i: (0, 0)),
            pl.BlockSpec((bb, H, dh, dh), lambda i: (i, 0, 0, 0)),
        ],
        out_specs=[
            pl.BlockSpec((R, MLSTM_WIDTH), lambda i: (i, 0)),
            pl.BlockSpec((bb, H, dh, dh), lambda i: (i, 0, 0, 0)),
            pl.BlockSpec((R, MLSTM_WIDTH), lambda i: (i, 0)),
            pl.BlockSpec((R, LANES), lambda i: (i, 0)),
        ],
        out_shape=[
            jax.ShapeDtypeStruct((S_ROWS, MLSTM_WIDTH), BF16),
            jax.ShapeDtypeStruct((DEC_BATCH, H, dh, dh), F32),
            jax.ShapeDtypeStruct((S_ROWS, MLSTM_WIDTH), F32),
            jax.ShapeDtypeStruct((S_ROWS, LANES), F32),
        ],
        compiler_params=_cparams(("arbitrary",)),
        name="mlstm_s",
    )(m_all, m_all, m_all, m_all, gates, n_rep, m_rep, brow, nw, state_c)


def _outproj_kernel(att_ref, mh_ref, x_ref, g1_ref, sh_ref, sc_ref, nw_ref, wa_ref, wm_ref, x1_ref, h2_ref, *,
                    per_row):
    y = _dot(att_ref[...].astype(BF16), wa_ref[...]) + _dot(mh_ref[...], wm_ref[...])
    x1 = x_ref[...] + _mod_row(g1_ref, per_row) * y
    x1_ref[...] = x1
    r = lax.rsqrt(jnp.mean(x1 * x1, axis=-1, keepdims=True) + EPS)
    h2 = (x1 * r * nw_ref[...]) * (1.0 + _mod_row(sc_ref, per_row)) + _mod_row(sh_ref, per_row)
    h2_ref[...] = h2.astype(BF16)


def _outproj(att, mh, x, mod, mod_row_block, norm_w, w_att, w_m, *, tm, per_row):
    rows = x.shape[0]
    mod_rows = tm if per_row else 8
    mod_idx = (lambda i: i) if per_row else (lambda i: mod_row_block)
    kern = functools.partial(_outproj_kernel, per_row=per_row)
    return pl.pallas_call(
        kern,
        grid=(rows // tm,),
        in_specs=[
            pl.BlockSpec((tm, ATT_WIDTH), lambda i: (i, 0)),
            pl.BlockSpec((tm, MLSTM_WIDTH), lambda i: (i, 0)),
            pl.BlockSpec((tm, D_MODEL), lambda i: (i, 0)),
            pl.BlockSpec((mod_rows, D_MODEL), lambda i: (mod_idx(i), 2)),
            pl.BlockSpec((mod_rows, D_MODEL), lambda i: (mod_idx(i), 3)),
            pl.BlockSpec((mod_rows, D_MODEL), lambda i: (mod_idx(i), 4)),
            pl.BlockSpec((1, D_MODEL), lambda i: (0, 0)),
            pl.BlockSpec((ATT_WIDTH, D_MODEL), lambda i: (0, 0)),
            pl.BlockSpec((MLSTM_WIDTH, D_MODEL), lambda i: (0, 0)),
        ],
        out_specs=[
            pl.BlockSpec((tm, D_MODEL), lambda i: (i, 0)),
            pl.BlockSpec((tm, D_MODEL), lambda i: (i, 0)),
        ],
        out_shape=[
            jax.ShapeDtypeStruct((rows, D_MODEL), F32),
            jax.ShapeDtypeStruct((rows, D_MODEL), BF16),
        ],
        compiler_params=_cparams(("arbitrary",)),
        name="outproj_s" if per_row else "outproj_p",
    )(att, mh, x, mod, mod, mod, norm_w, w_att, w_m)


def _ffn_a_kernel(hp_ref, hs_ref, wg_ref, wu_ref, ap_ref, as_ref, wg_scr, wu_scr):
    i = pl.program_id(1)

    def swiglu(h_ref, a_ref):
        h = h_ref[...]
        g = _dot(h, wg_scr[...])
        u = _dot(h, wu_scr[...])
        a_ref[...] = (g * _sigmoid(g) * u).astype(BF16)

    @pl.when(i == 0)
    def _():
        wg_scr[...] = wg_ref[...].astype(BF16)
        wu_scr[...] = wu_ref[...].astype(BF16)
        swiglu(hs_ref, as_ref)

    @pl.when(i > 0)
    def _():
        swiglu(hp_ref, ap_ref)


def _ffn_a(h2_p, h2_s, w_gate, w_up, *, tm):
    rows_p, rows_s = h2_p.shape[0], h2_s.shape[0]
    tn = 512
    ni = rows_p // tm
    tile = lambda i: jnp.maximum(i - 1, 0)
    return pl.pallas_call(
        _ffn_a_kernel,
        grid=(D_FF // tn, ni + 1),
        in_specs=[
            pl.BlockSpec((tm, D_MODEL), lambda j, i: (tile(i), 0)),
            pl.BlockSpec((rows_s, D_MODEL), lambda j, i: (0, 0)),
            pl.BlockSpec((D_MODEL, tn), lambda j, i: (0, j)),
            pl.BlockSpec((D_MODEL, tn), lambda j, i: (0, j)),
        ],
        out_specs=[
            pl.BlockSpec((tm, tn), lambda j, i: (tile(i), j)),
            pl.BlockSpec((rows_s, tn), lambda j, i: (0, j)),
        ],
        out_shape=[
            jax.ShapeDtypeStruct((rows_p, D_FF), BF16),
            jax.ShapeDtypeStruct((rows_s, D_FF), BF16),
        ],
        scratch_shapes=[pltpu.VMEM((D_MODEL, tn), BF16), pltpu.VMEM((D_MODEL, tn), BF16)],
        compiler_params=_cparams(("arbitrary", "arbitrary")),
        name="ffn_a",
    )(h2_p, h2_s, w_gate, w_up)


def _ffn_b_kernel(a_ref, wd_ref, x1_ref, g2_ref, fw_ref, y_ref, *rest, per_row, emit_w):
    kk = pl.program_id(1)

    def weight_tile():
        wt = wd_ref[...].astype(BF16)
        if emit_w:
            rest[0][...] = wt
        return wt

    @pl.when(kk == 0)
    def _():
        y_ref[...] = _dot(a_ref[...], weight_tile())

    @pl.when(kk > 0)
    def _():
        y_ref[...] += _dot(a_ref[...], weight_tile())

    @pl.when(kk == pl.num_programs(1) - 1)
    def _():
        x2 = x1_ref[...] + _mod_row(g2_ref, per_row) * y_ref[...]
        y_ref[...] = x2 * lax.rsqrt(jnp.mean(x2 * x2, axis=-1, keepdims=True) + EPS) * fw_ref[...]


def _ffn_b(a, w_down, x1, mod, mod_row_block, final_w, *, tm, per_row, emit_w):
    rows = a.shape[0]
    tk = 512
    mod_rows = tm if per_row else 8
    mod_idx = (lambda i: i) if per_row else (lambda i: mod_row_block)
    kern = functools.partial(_ffn_b_kernel, per_row=per_row, emit_w=emit_w)
    w_spec = pl.BlockSpec((tk, D_MODEL), lambda i, k: (k, 0))
    y_spec = pl.BlockSpec((tm, D_MODEL), lambda i, k: (i, 0))
    y_shape = jax.ShapeDtypeStruct((rows, D_MODEL), F32)
    return pl.pallas_call(
        kern,
        grid=(rows // tm, D_FF // tk),
        in_specs=[
            pl.BlockSpec((tm, tk), lambda i, k: (i, k)),
            w_spec,
            pl.BlockSpec((tm, D_MODEL), lambda i, k: (i, 0)),
            pl.BlockSpec((mod_rows, D_MODEL), lambda i, k: (mod_idx(i), 5)),
            pl.BlockSpec((1, D_MODEL), lambda i, k: (0, 0)),
        ],
        out_specs=[y_spec, w_spec] if emit_w else y_spec,
        out_shape=[y_shape, jax.ShapeDtypeStruct((D_FF, D_MODEL), BF16)] if emit_w else y_shape,
        compiler_params=_cparams(("arbitrary", "arbitrary")),
        name="ffn_b_s" if per_row else "ffn_b_p",
    )(a, w_down, x1, mod, final_w)


def _rope_tables(pos):
    half = ROPE_DIM // 2
    inv = np.float32(ROPE_THETA) ** (-np.arange(0, ROPE_DIM, 2, dtype=np.float32) / np.float32(ROPE_DIM))
    d = np.arange(LANES) % ATT_HEAD_DIM
    ang = pos.astype(np.float32)[:, None] * inv[d % half][None, :].astype(np.float32)
    cos, sin = np.cos(ang), np.sin(ang)
    d = d[None, :]
    tables = (np.where(d < ROPE_DIM, cos, 1.0), np.where(d < half, -sin, 0.0),
              np.where((d >= half) & (d < ROPE_DIM), sin, 0.0))
    return tuple(jnp.asarray(t.astype(np.float32)) for t in tables)


def kernel(x_prompt, x_sample, cache_k_win, cache_v_win, state_C, state_n, state_m, c_prompt, c_sample,
           norm1_w, norm2_w, final_norm_w, w_ada, b_ada, w_in, b_ig, b_fg, attn_sinks, mh_norm_w,
           w_out, w_gate, w_up, w_down):
    assert w_in.shape[0] == 1, "single-layer trunk"
    T = DEC_SEQ
    xp = x_prompt[0]
    xs = x_sample.reshape(S_ROWS, D_MODEL)

    c_all = jnp.concatenate([jnp.repeat(c_sample, T, axis=0), c_prompt, jnp.zeros((15, D_MODEL), F32)], axis=0)
    mod = _ada(c_all, w_ada[0], b_ada)
    prompt_mod_block = S_ROWS // 8

    w_in_t = jnp.transpose(w_in[0])
    wq_t = (w_in_t[:ATT_WIDTH].reshape(ATT_KV_HEADS, ATT_GROUP, ATT_HEAD_DIM, D_MODEL)
            .transpose(1, 0, 2, 3).reshape(ATT_WIDTH, D_MODEL).astype(BF16))
    w_gates_t = w_in_t[MAIN_WIDTH:]
    wg = jnp.pad(w_gates_t, ((0, 16 - 2 * MLSTM_HEADS), (0, 0)))
    w_out_att = (w_out[0, :ATT_WIDTH].reshape(ATT_KV_HEADS, ATT_GROUP, ATT_HEAD_DIM, D_MODEL)
                 .transpose(1, 0, 2, 3).reshape(ATT_WIDTH, D_MODEL).astype(BF16))
    w_out_m = w_out[0, ATT_WIDTH:].astype(BF16)
    n1 = norm1_w.reshape(1, D_MODEL)
    n2 = norm2_w.reshape(1, D_MODEL)
    fw = final_norm_w.reshape(1, D_MODEL)
    nw = mh_norm_w.reshape(1, MLSTM_WIDTH)
    gate_bias = jnp.concatenate([b_ig[0], b_fg[0]])
    brow = jnp.pad(gate_bias, (0, LANES - 2 * MLSTM_HEADS)).reshape(1, LANES)
    bcol = jnp.broadcast_to(jnp.pad(gate_bias, (0, 16 - 2 * MLSTM_HEADS))[:, None], (16, MLSTM_CHUNK_P))

    rope_p = _rope_tables(np.arange(SEQ))
    rope_s = _rope_tables(np.tile(PAST_LEN + np.arange(T), DEC_BATCH))

    tm_p = 1024
    q_s, _, kv32_s, m_s, g_s, _, w_in_bf = _inproj(xs, mod, 0, n1, wq_t, w_in_t, wg, *rope_s,
                                                   tm=S_ROWS, per_row=True, emit_w=True)
    q_p, kv_p, kv32_p, m_p, g_p, gt_p = _inproj(xp, mod, prompt_mod_block, n1, wq_t, w_in_bf, wg, *rope_p,
                                                tm=tm_p, per_row=False, emit_w=False)

    sinks = attn_sinks[0]
    sink_col = jnp.broadcast_to(sinks.reshape(ATT_HEADS, 1, 1), (ATT_HEADS, 8, LANES)).reshape(128, LANES)
    ck = jnp.transpose(cache_k_win[0], (0, 2, 3, 1))
    cv = jnp.transpose(cache_v_win[0], (0, 2, 3, 1))
    att_s, kwin_s, vwin_s = _attn_s(sink_col, q_s, kv32_s, ck, cv)

    n_rep = jnp.repeat(state_n[0].reshape(DEC_BATCH, MLSTM_WIDTH), T, axis=0)
    m_rep = jnp.pad(jnp.repeat(state_m[0], T, axis=0), ((0, 0), (0, LANES - MLSTM_HEADS)))
    mh_s, c_s, nrow_s, mrow_s = _mlstm_s(m_s, g_s, n_rep, m_rep, brow, nw, state_C[0])

    x1_p, h2_p, c_p, n_p, mm_p = _mix_p(sinks, q_p, kv_p, m_p, g_p, gt_p, brow, bcol, nw, xp, mod, prompt_mod_block,
                                        n2, w_out_att, w_out_m)
    x1_s, h2_s = _outproj(att_s, mh_s, xs, mod, 0, n2, w_out_att, w_out_m, tm=S_ROWS, per_row=True)
    a_p, a_s = _ffn_a(h2_p, h2_s, w_gate[0], w_up[0], tm=tm_p)
    y_s, w_down_bf = _ffn_b(a_s, w_down[0], x1_s, mod, 0, fw, tm=S_ROWS, per_row=True, emit_w=True)
    y_p = _ffn_b(a_p, w_down_bf, x1_p, mod, prompt_mod_block, fw, tm=tm_p, per_row=False, emit_w=False)

    kv_shape = (1, 1, WINDOW, ATT_KV_HEADS, ATT_HEAD_DIM)
    kv_last = kv32_p[tm_p - WINDOW:]
    dh = MLSTM_HEAD_DIM
    return (
        y_p.reshape(1, SEQ, D_MODEL),
        y_s.reshape(DEC_BATCH, T, D_MODEL),
        kv_last[:, :KV_WIDTH].reshape(kv_shape),
        kv_last[:, KV_WIDTH:].reshape(kv_shape),
        c_p.reshape(1, 1, MLSTM_HEADS, dh, dh),
        n_p[:MLSTM_HEADS].reshape(1, 1, MLSTM_HEADS, dh),
        mm_p[:MLSTM_HEADS, 0].reshape(1, 1, MLSTM_HEADS),
        jnp.transpose(kwin_s, (0, 3, 1, 2))[None],
        jnp.transpose(vwin_s, (0, 3, 1, 2))[None],
        c_s.reshape(1, DEC_BATCH, MLSTM_HEADS, dh, dh),
        nrow_s[T - 1::T].reshape(1, DEC_BATCH, MLSTM_HEADS, dh),
        mrow_s[T - 1::T, :MLSTM_HEADS].reshape(1, DEC_BATCH, MLSTM_HEADS),
    )
```

```python
import functools

import jax
import jax.numpy as jnp
import numpy as np
from jax import lax
from jax.experimental import pallas as pl
from jax.experimental.pallas import tpu as pltpu

F32 = jnp.float32
BF16 = jnp.bfloat16

D_MODEL = 2048
SEQ = 8192
DEC_BATCH = 128
DEC_SEQ = 4
S_ROWS = DEC_BATCH * DEC_SEQ
PAST_LEN = 16384
ATT_HEADS = 16
ATT_KV_HEADS = 4
ATT_GROUP = 4
ATT_HEAD_DIM = 64
WINDOW = 128
ROPE_THETA = 500000.0
ROPE_DIM = 16
MLSTM_HEADS = 4
MLSTM_HEAD_DIM = 256
ATT_WIDTH = 1024
KV_WIDTH = 256
MLSTM_WIDTH = 1024
MAIN_WIDTH = ATT_WIDTH + 2 * KV_WIDTH + 4 * MLSTM_WIDTH
D_FF = 5632
N_MOD = 6
EPS = 1e-6

LANES = 128
MLSTM_CHUNK_P = 256
VMEM_LIMIT = 56 * 1024 * 1024

NT_DIMS = (((1,), (1,)), ((), ()))
TN_DIMS = (((0,), (0,)), ((), ()))
HI = lax.Precision.HIGHEST


def _cparams(sem):
    return pltpu.CompilerParams(dimension_semantics=sem, vmem_limit_bytes=VMEM_LIMIT)


def _dot(a, b):
    return jnp.dot(a, b, preferred_element_type=F32)


def _dot_nt(a, b):
    return lax.dot_general(a, b, NT_DIMS, preferred_element_type=F32)


def _dot_tn(a, b):
    return lax.dot_general(a, b, TN_DIMS, preferred_element_type=F32)


def _sigmoid(x):
    return 1.0 / (1.0 + jnp.exp(-x))


def _log_sigmoid(x):
    return jnp.minimum(x, 0.0) - jnp.log(1.0 + jnp.exp(-jnp.abs(x)))


def _mod_row(ref, per_row):
    return ref[...] if per_row else ref[0:1, :]


def _ada_kernel(c_ref, w_ref, b_ref, o_ref, s_scr, rep_scr):
    @pl.when(pl.program_id(0) == 0)
    def _():
        c = c_ref[...]
        s_scr[...] = (c * _sigmoid(c)).astype(BF16)

    mod = _dot(s_scr[...], w_ref[...].astype(BF16)) + b_ref[...]
    for c in range(o_ref.shape[1] // LANES):
        cols = slice(LANES * c, LANES * (c + 1))
        for t in range(DEC_SEQ):
            rep_scr[pl.ds(t, DEC_BATCH, stride=DEC_SEQ), :] = mod[0:DEC_BATCH, cols]
        o_ref[0:S_ROWS, cols] = rep_scr[...]
    o_ref[S_ROWS:, :] = mod[DEC_BATCH:]


def _ada(c_all, w_ada, b_ada):
    m = c_all.shape[0]
    m_out = S_ROWS + m - DEC_BATCH
    n = w_ada.shape[1]
    tn = 1024
    return pl.pallas_call(
        _ada_kernel,
        grid=(n // tn,),
        in_specs=[
            pl.BlockSpec((m, D_MODEL), lambda j: (0, 0)),
            pl.BlockSpec((D_MODEL, tn), lambda j: (0, j)),
            pl.BlockSpec((1, tn), lambda j: (0, j)),
        ],
        out_specs=pl.BlockSpec((m_out, tn), lambda j: (0, j)),
        out_shape=jax.ShapeDtypeStruct((m_out, n), F32),
        scratch_shapes=[pltpu.VMEM((m, D_MODEL), BF16), pltpu.VMEM((S_ROWS, LANES), F32)],
        compiler_params=_cparams(("arbitrary",)),
        name="ada",
    )(c_all, w_ada, b_ada)


def _rope_store(acc, cos, sa, sb, out_ref, ncols, scale):
    for c in range(ncols // LANES):
        xc = acc[:, LANES * c:LANES * (c + 1)]
        rot = xc * cos + pltpu.roll(xc, LANES - 8, 1) * sa + pltpu.roll(xc, 8, 1) * sb
        if scale != 1.0:
            rot = rot * scale
        out_ref[:, LANES * c:LANES * (c + 1)] = rot.astype(out_ref.dtype)


def _inproj_kernel(x_ref, sh_ref, sc_ref, nw_ref, wq_ref, win_ref, wg_ref, cos_ref, sa_ref, sb_ref,
                   q_ref, kv_ref, kv32_ref, m_ref, g_ref, gt_ref, *rest, per_row, emit_w):
    h_scr = rest[-1]
    j = pl.program_id(1)

    def weight_tile():
        wt = win_ref[...].astype(BF16)
        if emit_w:
            rest[0][...] = wt
        return wt

    @pl.when(j == 0)
    def _():
        x = x_ref[...]
        r = lax.rsqrt(jnp.mean(x * x, axis=-1, keepdims=True) + EPS)
        gain = nw_ref[...] * (1.0 + _mod_row(sc_ref, per_row))
        h = (x * r * gain + _mod_row(sh_ref, per_row)).astype(BF16)
        h_scr[...] = h
        wg = jnp.concatenate([wg_ref[...].astype(BF16), jnp.zeros((LANES - 16, D_MODEL), BF16)], axis=0)
        g = _dot_nt(h, wg)
        g_ref[...] = g
        gt_ref[...] = g.T[0:16, :]

    @pl.when(j < 2)
    def _():
        acc = _dot_nt(h_scr[...], wq_ref[...])
        _rope_store(acc, cos_ref[...], sa_ref[...], sb_ref[...], q_ref, 512, ATT_HEAD_DIM ** -0.5)

    @pl.when(j == 2)
    def _():
        acc = _dot_nt(h_scr[...], weight_tile())
        _rope_store(acc, cos_ref[...], sa_ref[...], sb_ref[...], kv32_ref, KV_WIDTH, 1.0)
        kv32_ref[:, KV_WIDTH:] = acc[:, KV_WIDTH:]
        kv_ref[...] = kv32_ref[...].astype(BF16)

    @pl.when(j > 2)
    def _():
        m_ref[...] = _dot_nt(h_scr[...], weight_tile()).astype(BF16)


def _inproj(x, mod, mod_row_block, norm_w, wq_t, w_in_t, wg, cos, sa, sb, *, tm, per_row, emit_w):
    rows = x.shape[0]
    tn = 512
    nj = MAIN_WIDTH // tn
    mod_rows = tm if per_row else 8
    mod_idx = (lambda i: i) if per_row else (lambda i: mod_row_block)
    kern = functools.partial(_inproj_kernel, per_row=per_row, emit_w=emit_w)
    copy_spec = pl.BlockSpec((tn, D_MODEL), lambda i, j: (jnp.maximum(j, 2) - 2, 0))
    w_spec = pl.BlockSpec((tn, D_MODEL), lambda i, j: (jnp.maximum(j, 2), 0)) if emit_w else copy_spec
    extra_specs = [copy_spec] if emit_w else []
    extra_shapes = [jax.ShapeDtypeStruct((MAIN_WIDTH - ATT_WIDTH, D_MODEL), BF16)] if emit_w else []
    return pl.pallas_call(
        kern,
        grid=(rows // tm, nj),
        in_specs=[
            pl.BlockSpec((tm, D_MODEL), lambda i, j: (jnp.minimum(i + jnp.minimum(j, 1), rows // tm - 1), 0)),
            pl.BlockSpec((mod_rows, D_MODEL), lambda i, j: (mod_idx(i), 0)),
            pl.BlockSpec((mod_rows, D_MODEL), lambda i, j: (mod_idx(i), 1)),
            pl.BlockSpec((1, D_MODEL), lambda i, j: (0, 0)),
            pl.BlockSpec((tn, D_MODEL), lambda i, j: (jnp.minimum(j, 1), 0)),
            w_spec,
            pl.BlockSpec((16, D_MODEL), lambda i, j: (0, 0)),
            pl.BlockSpec((tm, LANES), lambda i, j: (i, 0)),
            pl.BlockSpec((tm, LANES), lambda i, j: (i, 0)),
            pl.BlockSpec((tm, LANES), lambda i, j: (i, 0)),
        ],
        out_specs=[
            pl.BlockSpec((tm, tn), lambda i, j: (i, jnp.minimum(j, 1))),
            pl.BlockSpec((tm, tn), lambda i, j: (i, 0)),
            pl.BlockSpec((tm, tn), lambda i, j: (0, 0)),
            pl.BlockSpec((tm, tn), lambda i, j: (i, jnp.clip(j - 3, 0, 7))),
            pl.BlockSpec((tm, LANES), lambda i, j: (i, 0)),
            pl.BlockSpec((16, tm), lambda i, j: (0, i)),
        ] + extra_specs,
        out_shape=[
            jax.ShapeDtypeStruct((rows, ATT_WIDTH), BF16),
            jax.ShapeDtypeStruct((rows, 2 * KV_WIDTH), BF16),
            jax.ShapeDtypeStruct((tm, 2 * KV_WIDTH), F32),
            jax.ShapeDtypeStruct((rows, 4 * MLSTM_WIDTH), BF16),
            jax.ShapeDtypeStruct((rows, LANES), F32),
            jax.ShapeDtypeStruct((16, rows), F32),
        ] + extra_shapes,
        scratch_shapes=[pltpu.VMEM((tm, D_MODEL), BF16)],
        compiler_params=_cparams(("arbitrary", "arbitrary")),
        name="inproj_s" if per_row else "inproj_p",
    )(x, mod, mod, norm_w, wq_t, w_in_t, wg, cos, sa, sb)


def _attn_block(sink_ref, q_ref, row0, kv2, allowed, store):
    w = WINDOW
    grp = ATT_GROUP
    rows = grp * w
    member = lax.broadcasted_iota(jnp.int32, (rows, 1), 0) // w
    low = lax.broadcasted_iota(jnp.int32, (2 * w, LANES), 1) < ATT_HEAD_DIM
    low_o = lax.broadcasted_iota(jnp.int32, (rows, LANES), 1) < ATT_HEAD_DIM
    key_row = lax.broadcasted_iota(jnp.int32, (4 * w, LANES), 0)
    key_lane = lax.broadcasted_iota(jnp.int32, (4 * w, LANES), 1)
    ones_bd = (((key_row < 2 * w) & (key_lane < ATT_HEAD_DIM))
               | ((key_row >= 2 * w) & (key_lane >= ATT_HEAD_DIM))).astype(BF16)
    zero = jnp.zeros((2 * w, LANES), BF16)
    for cp in range(2):
        k128 = kv2[:, LANES * cp:LANES * (cp + 1)]
        v128 = kv2[:, KV_WIDTH + LANES * cp:KV_WIDTH + LANES * (cp + 1)]
        kbd = jnp.concatenate([jnp.where(low, k128, zero), jnp.where(low, zero, k128)], axis=0)
        vbd = jnp.concatenate([jnp.where(low, v128, zero), jnp.where(low, zero, v128)], axis=0)
        v_aug = jnp.concatenate([vbd, ones_bd], axis=1)
        q4 = jnp.concatenate([q_ref[row0:row0 + w, 256 * r + LANES * cp:256 * r + LANES * (cp + 1)]
                              for r in range(grp)], axis=0)
        s = _dot_nt(q4, kbd)
        es, tails = [], []
        for half in range(2):
            sh = jnp.where(allowed, s[:, 2 * w * half:2 * w * (half + 1)], -jnp.inf)
            head0 = (2 * cp + half) * grp
            sink = jnp.full((rows, 1), sink_ref[head0], F32)
            for r in range(1, grp):
                sink = jnp.where(member == r, sink_ref[head0 + r], sink)
            m = jnp.maximum(jnp.max(sh, axis=-1, keepdims=True), sink)
            es.append(jnp.exp(sh - m).astype(BF16))
            tails.append(jnp.exp(sink - m))
        oa = _dot(jnp.concatenate(es, axis=1), v_aug)
        l = oa[:, LANES:] + jnp.where(low_o, tails[0], tails[1])
        o = (oa[:, :LANES] / l).astype(BF16)
        for r in range(grp):
            store(256 * r + LANES * cp, o[w * r:w * (r + 1)])


MIX_TM = 512


def _mix_p_kernel(sink_ref, q_ref, kvp_ref, kvc_ref, mq_ref, mk_ref, mv_ref, mo_ref, g_ref, gt_ref, brow_ref, bcol_ref,
                  mnw_ref, x_ref, g1_ref, sh_ref, sc_ref, nw_ref, wa_ref, wm_ref,
                  x1_ref, h2_ref, c_out, n_out, m_out, att_scr, mh_scr, c_scr, n_scr, m_scr):
    s = pl.program_id(0)
    w = WINDOW
    tiles = pl.num_programs(0) - 1
    slot = s % 2

    @pl.when(s == 0)
    def _():
        c_scr[...] = jnp.zeros_like(c_scr)
        n_scr[...] = jnp.zeros_like(n_scr)
        m_scr[...] = jnp.zeros_like(m_scr)

    def project():
        y = _dot(att_scr[1 - slot], wa_ref[...]) + _dot(mh_scr[1 - slot], wm_ref[...])
        x1 = x_ref[...] + g1_ref[0:1, :] * y
        x1_ref[...] = x1
        r = lax.rsqrt(jnp.mean(x1 * x1, axis=-1, keepdims=True) + EPS)
        gain = nw_ref[...] * (1.0 + sc_ref[0:1, :])
        h2_ref[...] = (x1 * r * gain + sh_ref[0:1, :]).astype(BF16)

    def mixers():
        rows = ATT_GROUP * w
        qi = lax.broadcasted_iota(jnp.int32, (rows, 2 * w), 0) % w
        kj = lax.broadcasted_iota(jnp.int32, (rows, 2 * w), 1)
        first_off = jnp.where(s > 0, 0, 4 * w)
        causal = (kj >= w) & (kj - w <= qi)
        for blk in range(MIX_TM // w):
            prev = kvp_ref[...] if blk == 0 else kvc_ref[w * (blk - 1):w * blk, :]
            kv2 = jnp.concatenate([prev, kvc_ref[w * blk:w * (blk + 1), :]], axis=0)
            allowed = ((kj < w) & (kj > qi + (first_off if blk == 0 else 0))) | causal

            def store(c0, val, blk=blk):
                att_scr[slot, w * blk:w * (blk + 1), c0:c0 + LANES] = val

            _attn_block(sink_ref, q_ref, w * blk, kv2, allowed, store)

        for ch in range(MIX_TM // MLSTM_CHUNK_P):
            r0 = MLSTM_CHUNK_P * ch

            def store_mh(cs, val, r0=r0):
                mh_scr[slot, r0:r0 + MLSTM_CHUNK_P, cs] = val

            _mlstm_chunk(mq_ref, mk_ref, mv_ref, mo_ref, g_ref, gt_ref, brow_ref, bcol_ref, mnw_ref,
                         c_scr, n_scr, m_scr, r0, store_mh)

    @pl.when(s == 0)
    def _():
        mixers()

    @pl.when((s > 0) & (s < tiles))
    def _():
        project()
        mixers()

    @pl.when(s == tiles)
    def _():
        project()
        c_out[...] = c_scr[...]
        n_out[...] = n_scr[...]
        m_out[...] = m_scr[...]


def _mix_p(sinks, q, kv, m_all, gates, gates_t, brow, bcol, mnw, x, mod, mod_row_block, norm_w, w_att, w_m):
    tm = MIX_TM
    tiles = SEQ // tm
    bpt = tm // WINDOW
    dh = MLSTM_HEAD_DIM
    att_tile = lambda s: jnp.minimum(s, tiles - 1)
    out_tile = lambda s: jnp.maximum(s - 1, 0)
    m_spec = lambda col: pl.BlockSpec((tm, MLSTM_WIDTH), lambda s: (att_tile(s), col))
    return pl.pallas_call(
        _mix_p_kernel,
        grid=(tiles + 1,),
        in_specs=[
            pl.BlockSpec(memory_space=pltpu.SMEM),
            pl.BlockSpec((tm, ATT_WIDTH), lambda s: (att_tile(s), 0)),
            pl.BlockSpec((WINDOW, 2 * KV_WIDTH), lambda s: (jnp.maximum(bpt * att_tile(s) - 1, 0), 0)),
            pl.BlockSpec((tm, 2 * KV_WIDTH), lambda s: (att_tile(s), 0)),
            m_spec(0), m_spec(1), m_spec(2), m_spec(3),
            pl.BlockSpec((tm, LANES), lambda s: (att_tile(s), 0)),
            pl.BlockSpec((16, tm), lambda s: (0, att_tile(s))),
            pl.BlockSpec((1, LANES), lambda s: (0, 0)),
            pl.BlockSpec((16, MLSTM_CHUNK_P), lambda s: (0, 0)),
            pl.BlockSpec((1, MLSTM_WIDTH), lambda s: (0, 0)),
            pl.BlockSpec((tm, D_MODEL), lambda s: (out_tile(s), 0)),
            pl.BlockSpec((8, D_MODEL), lambda s: (mod_row_block, 2)),
            pl.BlockSpec((8, D_MODEL), lambda s: (mod_row_block, 3)),
            pl.BlockSpec((8, D_MODEL), lambda s: (mod_row_block, 4)),
            pl.BlockSpec((1, D_MODEL), lambda s: (0, 0)),
            pl.BlockSpec((ATT_WIDTH, D_MODEL), lambda s: (0, 0)),
            pl.BlockSpec((MLSTM_WIDTH, D_MODEL), lambda s: (0, 0)),
        ],
        out_specs=[
            pl.BlockSpec((tm, D_MODEL), lambda s: (out_tile(s), 0)),
            pl.BlockSpec((tm, D_MODEL), lambda s: (out_tile(s), 0)),
            pl.BlockSpec((MLSTM_HEADS, dh, dh), lambda s: (0, 0, 0)),
            pl.BlockSpec((8, dh), lambda s: (0, 0)),
            pl.BlockSpec((8, LANES), lambda s: (0, 0)),
        ],
        out_shape=[
            jax.ShapeDtypeStruct((SEQ, D_MODEL), F32),
            jax.ShapeDtypeStruct((SEQ, D_MODEL), BF16),
            jax.ShapeDtypeStruct((MLSTM_HEADS, dh, dh), F32),
            jax.ShapeDtypeStruct((8, dh), F32),
            jax.ShapeDtypeStruct((8, LANES), F32),
        ],
        scratch_shapes=[
            pltpu.VMEM((2, tm, ATT_WIDTH), BF16),
            pltpu.VMEM((2, tm, MLSTM_WIDTH), BF16),
            pltpu.VMEM((MLSTM_HEADS, dh, dh), F32),
            pltpu.VMEM((8, dh), F32),
            pltpu.VMEM((8, LANES), F32),
        ],
        compiler_params=_cparams(("arbitrary",)),
        name="mix_p",
    )(sinks, q, kv, kv, m_all, m_all, m_all, m_all, gates, gates_t, brow, bcol, mnw, x, mod, mod, mod, norm_w,
      w_att, w_m)


ATT_S_BB = 16


def _attn_s_kernel(sink_ref, q_ref, kv32_ref, ck_ref, cv_ref, o_ref, ko_ref, vo_ref, q32_scr):
    t_new = DEC_SEQ
    w = WINDOW
    q32_scr[...] = q_ref[...].astype(F32)
    rows = 4 * 4 * 8
    row = lax.broadcasted_iota(jnp.int32, (rows, w), 0)
    slot = lax.broadcasted_iota(jnp.int32, (rows, w), 1)
    t_row = row % t_new
    second = (row % 8) >= t_new
    win_ok = (slot < w - t_new) | (slot - (w - t_new) <= t_row)
    old_ok = (slot >= 1) & (slot < t_new) & (slot > t_row)
    lane256 = lax.broadcasted_iota(jnp.int32, (32, 2 * LANES), 1)
    sink = sink_ref[...][:, 0:1]
    kv_new = jnp.concatenate([kv32_ref[...], jnp.zeros((w - ATT_S_BB * t_new, 2 * KV_WIDTH), F32)], axis=0)
    kv_t = kv_new.T
    new_slot = lax.broadcasted_iota(jnp.int32, (KV_WIDTH, w), 1) >= w - t_new
    for b in range(ATT_S_BB):
        cols = pltpu.roll(kv_t, w - t_new - t_new * b, 1)
        k_shift = pltpu.roll(ck_ref[b].reshape(KV_WIDTH, w), w - t_new, 1)
        v_shift = pltpu.roll(cv_ref[b].reshape(KV_WIDTH, w), w - t_new, 1)
        ko_ref[b] = jnp.where(new_slot, cols[:KV_WIDTH], k_shift).reshape(ATT_KV_HEADS, ATT_HEAD_DIM, w)
        vo_ref[b] = jnp.where(new_slot, cols[KV_WIDTH:], v_shift).reshape(ATT_KV_HEADS, ATT_HEAD_DIM, w)
    for pair in range(ATT_S_BB // 2):
        b0, b1 = 2 * pair, 2 * pair + 1
        q32 = jnp.concatenate([q32_scr[8 * pair:8 * (pair + 1), 256 * r:256 * (r + 1)] for r in range(ATT_GROUP)],
                              axis=0)
        qpad = jnp.concatenate(
            [jnp.where((lane256 // ATT_HEAD_DIM) == g, q32, 0.0) for g in range(ATT_KV_HEADS)], axis=0).astype(BF16)
        kw = [ko_ref[b].reshape(KV_WIDTH, w).astype(BF16) for b in (b0, b1)]
        vw = [vo_ref[b].reshape(KV_WIDTH, w).astype(BF16) for b in (b0, b1)]
        kc = [ck_ref[b].reshape(KV_WIDTH, w).astype(BF16) for b in (b0, b1)]
        vc = [cv_ref[b].reshape(KV_WIDTH, w).astype(BF16) for b in (b0, b1)]
        s_w = jnp.where(second, _dot(qpad, kw[1]), _dot(qpad, kw[0]))
        s_c = jnp.where(second, _dot(qpad, kc[1]), _dot(qpad, kc[0]))
        s_w = jnp.where(win_ok, s_w, -jnp.inf)
        s_c = jnp.where(old_ok, s_c, -jnp.inf)
        m = jnp.maximum(jnp.maximum(jnp.max(s_w, axis=-1, keepdims=True), jnp.max(s_c, axis=-1, keepdims=True)), sink)
        e_w = jnp.exp(s_w - m)
        e_c = jnp.exp(s_c - m)
        l = jnp.sum(e_w, axis=-1, keepdims=True) + jnp.sum(e_c, axis=-1, keepdims=True) + jnp.exp(sink - m)
        p_w = e_w / l
        p_c = e_c / l
        zero = jnp.zeros_like(p_w)
        o = (_dot_nt(jnp.where(second, zero, p_w).astype(BF16), vw[0])
             + _dot_nt(jnp.where(second, p_w, zero).astype(BF16), vw[1])
             + _dot_nt(jnp.where(second, zero, p_c).astype(BF16), vc[0])
             + _dot_nt(jnp.where(second, p_c, zero).astype(BF16), vc[1]))
        o32 = jnp.zeros((32, 2 * LANES), F32)
        for g in range(ATT_KV_HEADS):
            o32 = jnp.where((lane256 // ATT_HEAD_DIM) == g, o[32 * g:32 * (g + 1), :], o32)
        for r in range(ATT_GROUP):
            o_ref[8 * pair:8 * (pair + 1), 256 * r:256 * (r + 1)] = o32[8 * r:8 * (r + 1), :]


def _attn_s(sink_col, q, kv32, ck, cv):
    bb = ATT_S_BB
    rows = bb * DEC_SEQ
    cache_block = (bb, ATT_KV_HEADS, ATT_HEAD_DIM, WINDOW)
    cache_shape = (DEC_BATCH, ATT_KV_HEADS, ATT_HEAD_DIM, WINDOW)
    return pl.pallas_call(
        _attn_s_kernel,
        grid=(DEC_BATCH // bb,),
        in_specs=[
            pl.BlockSpec((128, LANES), lambda i: (0, 0)),
            pl.BlockSpec((rows, ATT_WIDTH), lambda i: (i, 0)),
            pl.BlockSpec((rows, 2 * KV_WIDTH), lambda i: (i, 0)),
            pl.BlockSpec(cache_block, lambda i: (i, 0, 0, 0)),
            pl.BlockSpec(cache_block, lambda i: (i, 0, 0, 0)),
        ],
        out_specs=[
            pl.BlockSpec((rows, ATT_WIDTH), lambda i: (i, 0)),
            pl.BlockSpec(cache_block, lambda i: (i, 0, 0, 0)),
            pl.BlockSpec(cache_block, lambda i: (i, 0, 0, 0)),
        ],
        out_shape=[
            jax.ShapeDtypeStruct((S_ROWS, ATT_WIDTH), F32),
            jax.ShapeDtypeStruct(cache_shape, F32),
            jax.ShapeDtypeStruct(cache_shape, F32),
        ],
        scratch_shapes=[pltpu.VMEM((rows, ATT_WIDTH), F32)],
        compiler_params=_cparams(("arbitrary",)),
        name="attn_s",
    )(sink_col, q, kv32, ck, cv)


def _head_norm_gate(h, nw, mo):
    hn = h * lax.rsqrt(jnp.mean(h * h, axis=-1, keepdims=True) + EPS) * nw
    return hn * _sigmoid(mo.astype(F32))


def _mlstm_chunk(q_ref, k_ref, v_ref, mo_ref, g_ref, gt_ref, brow_ref, bcol_ref, nw_ref, c_scr, n_scr, m_scr,
                 r0, store):
    L = MLSTM_CHUNK_P
    dh = MLSTM_HEAD_DIM
    rs = slice(r0, r0 + L)
    ti = lax.broadcasted_iota(jnp.int32, (L, L), 0)
    si = lax.broadcasted_iota(jnp.int32, (L, L), 1)
    causal = si <= ti
    tri = causal.astype(F32)
    tri_t = (ti <= si).astype(F32)
    gates = g_ref[rs, :] + brow_ref[...]
    gates_t = gt_ref[:, rs] + bcol_ref[...]
    b_col = jnp.dot(tri, _log_sigmoid(gates), precision=HI, preferred_element_type=F32)
    b_row = jnp.dot(_log_sigmoid(gates_t), tri_t, precision=HI, preferred_element_type=F32)
    for hd in range(MLSTM_HEADS):
        cs = slice(dh * hd, dh * (hd + 1))
        b_c = b_col[:, MLSTM_HEADS + hd:MLSTM_HEADS + hd + 1]
        li_c = gates[:, hd:hd + 1]
        b_r = b_row[MLSTM_HEADS + hd:MLSTM_HEADS + hd + 1, :]
        li_r = gates_t[hd:hd + 1, :]
        dm = jnp.where(causal, b_c - b_r + li_r, -jnp.inf)
        m_prev = m_scr[hd:hd + 1, 0:1]
        m_inter = b_c + m_prev
        m_t = jnp.maximum(m_inter, jnp.max(dm, axis=-1, keepdims=True))
        q = q_ref[rs, cs] * (dh ** -0.5)
        k = k_ref[rs, cs]
        v = v_ref[rs, cs]
        sm = _dot_nt(q, k) * jnp.exp(dm - m_t)
        a = jnp.exp(m_inter - m_t)
        c_old = c_scr[hd]
        n_old = n_scr[hd:hd + 1, :]
        num = a * _dot_nt(q, c_old.astype(BF16)) + _dot(sm.astype(BF16), v)
        qn = jnp.sum(q.astype(F32) * n_old, axis=-1, keepdims=True)
        den = a * qn + jnp.sum(sm, axis=-1, keepdims=True)
        h = num / jnp.maximum(jnp.abs(den), jnp.exp(-m_t))
        store(cs, _head_norm_gate(h, nw_ref[:, cs], mo_ref[rs, cs]).astype(BF16))
        m_new = m_t[L - 1:L, :]
        b_last = b_c[L - 1:L, :]
        g = jnp.exp(b_last - b_c + li_c - m_new)
        decay = jnp.exp(b_last + m_prev - m_new)
        gv = (g * v.astype(F32)).astype(BF16)
        c_scr[hd] = decay * c_old + _dot_tn(gv, k)
        n_scr[hd:hd + 1, :] = decay * n_old + jnp.sum(g * k.astype(F32), axis=0, keepdims=True)
        m_scr[hd:hd + 1, :] = jnp.broadcast_to(m_new, (1, LANES))


MLSTM_S_BB = 8


def _mlstm_s_kernel(q_ref, k_ref, v_ref, mo_ref, g_ref, nrep_ref, mrep_ref, brow_ref, nw_ref, c_ref,
                    mh_ref, c_out, nrow_ref, mrow_ref):
    T = DEC_SEQ
    R = MLSTM_S_BB * T
    H = MLSTM_HEADS
    dh = MLSTM_HEAD_DIM

    def shift(x, d):
        return pltpu.roll(x, d, 0)

    lanes = lax.broadcasted_iota(jnp.int32, (R, LANES), 1)
    tmod = lax.broadcasted_iota(jnp.int32, (R, LANES), 0) % T
    tmod_w = lax.broadcasted_iota(jnp.int32, (R, MLSTM_WIDTH), 0) % T
    head_ok = lanes < H
    gates = g_ref[...] + brow_ref[...]
    li = jnp.where(head_ok, gates, 0.0)
    lf = jnp.where(head_ok, pltpu.roll(_log_sigmoid(gates), LANES - H, 1), 0.0)
    bcum = lf
    for d in range(1, T):
        bcum = bcum + jnp.where(tmod >= d, shift(lf, d), 0.0)
    m0 = mrep_ref[...]
    m_inter = bcum + m0
    dms = [li] + [jnp.where(tmod >= d, bcum - shift(bcum, d) + shift(li, d), -jnp.inf) for d in range(1, T)]
    m_t = m_inter
    for dm in dms:
        m_t = jnp.maximum(m_t, dm)
    a = jnp.exp(m_inter - m_t)
    ws = [jnp.exp(dm - m_t) for dm in dms]

    q_bf = q_ref[...] * (dh ** -0.5)
    q = q_bf.astype(F32)
    k = k_ref[...].astype(F32)
    v = v_ref[...].astype(F32)
    seg = (lax.broadcasted_iota(jnp.int32, (MLSTM_WIDTH, LANES), 0) // dh
           == lax.broadcasted_iota(jnp.int32, (MLSTM_WIDTH, LANES), 1)).astype(F32)
    ex = (lax.broadcasted_iota(jnp.int32, (LANES, MLSTM_WIDTH), 1) // dh
          == lax.broadcasted_iota(jnp.int32, (LANES, MLSTM_WIDTH), 0)).astype(F32)

    def segsum(x):
        return jnp.dot(x, seg, precision=HI, preferred_element_type=F32)

    def expand(x):
        return jnp.dot(x, ex, precision=HI, preferred_element_type=F32)

    ks = [k] + [shift(k, d) for d in range(1, T)]
    vs = [v] + [shift(v, d) for d in range(1, T)]
    sms = [segsum(q * ks[d]) * ws[d] for d in range(T)]
    den = a * segsum(q * nrep_ref[...])
    for sm in sms:
        den = den + sm
    inv = 1.0 / jnp.maximum(jnp.abs(den), jnp.exp(-m_t))

    def last(x):
        out = jnp.zeros_like(x)
        for jj in range(T):
            out = jnp.where(tmod == T - 1 - jj, x if jj == 0 else pltpu.roll(x, R - jj, 0), out)
        return out

    m_new = last(m_t)
    b_last = last(bcum)
    g = jnp.where(head_ok, jnp.exp(b_last - bcum + li - m_new), 0.0)
    decay = jnp.where(head_ok, jnp.exp(b_last + m0 - m_new), 0.0)
    a_f = expand(a * inv)
    w_f = [expand(sm * inv) for sm in sms]
    g_f = expand(g)
    d_f = expand(decay)
    gv = (g_f * v).astype(BF16)
    rowb = lax.broadcasted_iota(jnp.int32, (R, dh), 0) // T
    for hd in range(H):
        cs = slice(dh * hd, dh * (hd + 1))
        qh = q_bf[:, cs]
        kh = k_ref[:, cs]
        gvh = gv[:, cs]
        qc = jnp.zeros((R, dh), F32)
        for bb in range(MLSTM_S_BB):
            c_old = c_ref[bb, hd]
            qc = jnp.where(rowb == bb, _dot_nt(qh, c_old.astype(BF16)), qc)
            upd = _dot_tn(jnp.where(rowb == bb, gvh, jnp.zeros_like(gvh)), kh)
            c_out[bb, hd] = d_f[T * bb:T * bb + 1, cs] * c_old + upd
        h = a_f[:, cs] * qc
        for d in range(T):
            h = h + w_f[d][:, cs] * vs[d][:, cs]
        mh_ref[:, cs] = _head_norm_gate(h, nw_ref[:, cs], mo_ref[:, cs]).astype(BF16)
    gk = g_f * k
    nsum = gk
    for d in range(1, T):
        nsum = nsum + jnp.where(tmod_w >= d, shift(gk, d), 0.0)
    nrow_ref[...] = d_f * nrep_ref[...] + nsum
    mrow_ref[...] = m_t


def _mlstm_s(m_all, gates, n_rep, m_rep, brow, nw, state_c):
    bb = MLSTM_S_BB
    R = bb * DEC_SEQ
    dh = MLSTM_HEAD_DIM
    H = MLSTM_HEADS
    return pl.pallas_call(
        _mlstm_s_kernel,
        grid=(DEC_BATCH // bb,),
        in_specs=[
            pl.BlockSpec((R, MLSTM_WIDTH), lambda i: (i, 0)),
            pl.BlockSpec((R, MLSTM_WIDTH), lambda i: (i, 1)),
            pl.BlockSpec((R, MLSTM_WIDTH), lambda i: (i, 2)),
            pl.BlockSpec((R, MLSTM_WIDTH), lambda i: (i, 3)),
            pl.BlockSpec((R, LANES), lambda i: (i, 0)),
            pl.BlockSpec((R, MLSTM_WIDTH), lambda i: (i, 0)),
            pl.BlockSpec((R, LANES), lambda i: (i, 0)),
            pl.BlockSpec((1, LANES), lambda i: (0, 0)),
            pl.BlockSpec((1, MLSTM_WIDTH), lambda i: (0, 0)),
            pl.BlockSpec((bb, H, dh, dh), lambda i: (i, 0, 0, 0)),
        ],
        out_specs=[
            pl.BlockSpec((R, MLSTM_WIDTH), lambda i: (i, 0)),
            pl.BlockSpec((bb, H, dh, dh), lambda i: (i, 0, 0, 0)),
            pl.BlockSpec((R, MLSTM_WIDTH), lambda i: (i, 0)),
            pl.BlockSpec((R, LANES), lambda i: (i, 0)),
        ],
        out_shape=[
            jax.ShapeDtypeStruct((S_ROWS, MLSTM_WIDTH), BF16),
            jax.ShapeDtypeStruct((DEC_BATCH, H, dh, dh), F32),
            jax.ShapeDtypeStruct((S_ROWS, MLSTM_WIDTH), F32),
            jax.ShapeDtypeStruct((S_ROWS, LANES), F32),
        ],
        compiler_params=_cparams(("arbitrary",)),
        name="mlstm_s",
    )(m_all, m_all, m_all, m_all, gates, n_rep, m_rep, brow, nw, state_c)


def _outproj_kernel(att_ref, mh_ref, x_ref, g1_ref, sh_ref, sc_ref, nw_ref, wa_ref, wm_ref, x1_ref, h2_ref, *,
                    per_row):
    y = _dot(att_ref[...].astype(BF16), wa_ref[...]) + _dot(mh_ref[...], wm_ref[...])
    x1 = x_ref[...] + _mod_row(g1_ref, per_row) * y
    x1_ref[...] = x1
    r = lax.rsqrt(jnp.mean(x1 * x1, axis=-1, keepdims=True) + EPS)
    h2 = (x1 * r * nw_ref[...]) * (1.0 + _mod_row(sc_ref, per_row)) + _mod_row(sh_ref, per_row)
    h2_ref[...] = h2.astype(BF16)


def _outproj(att, mh, x, mod, mod_row_block, norm_w, w_att, w_m, *, tm, per_row):
    rows = x.shape[0]
    mod_rows = tm if per_row else 8
    mod_idx = (lambda i: i) if per_row else (lambda i: mod_row_block)
    kern = functools.partial(_outproj_kernel, per_row=per_row)
    return pl.pallas_call(
        kern,
        grid=(rows // tm,),
        in_specs=[
            pl.BlockSpec((tm, ATT_WIDTH), lambda i: (i, 0)),
            pl.BlockSpec((tm, MLSTM_WIDTH), lambda i: (i, 0)),
            pl.BlockSpec((tm, D_MODEL), lambda i: (i, 0)),
            pl.BlockSpec((mod_rows, D_MODEL), lambda i: (mod_idx(i), 2)),
            pl.BlockSpec((mod_rows, D_MODEL), lambda i: (mod_idx(i), 3)),
            pl.BlockSpec((mod_rows, D_MODEL), lambda i: (mod_idx(i), 4)),
            pl.BlockSpec((1, D_MODEL), lambda i: (0, 0)),
            pl.BlockSpec((ATT_WIDTH, D_MODEL), lambda i: (0, 0)),
            pl.BlockSpec((MLSTM_WIDTH, D_MODEL), lambda i: (0, 0)),
        ],
        out_specs=[
            pl.BlockSpec((tm, D_MODEL), lambda i: (i, 0)),
            pl.BlockSpec((tm, D_MODEL), lambda i: (i, 0)),
        ],
        out_shape=[
            jax.ShapeDtypeStruct((rows, D_MODEL), F32),
            jax.ShapeDtypeStruct((rows, D_MODEL), BF16),
        ],
        compiler_params=_cparams(("arbitrary",)),
        name="outproj_s" if per_row else "outproj_p",
    )(att, mh, x, mod, mod, mod, norm_w, w_att, w_m)


def _ffn_a_kernel(hp_ref, hs_ref, wg_ref, wu_ref, ap_ref, as_ref, wg_scr, wu_scr):
    i = pl.program_id(1)

    def swiglu(h_ref, a_ref):
        h = h_ref[...]
        g = _dot(h, wg_scr[...])
        u = _dot(h, wu_scr[...])
        a_ref[...] = (g * _sigmoid(g) * u).astype(BF16)

    @pl.when(i == 0)
    def _():
        wg_scr[...] = wg_ref[...].astype(BF16)
        wu_scr[...] = wu_ref[...].astype(BF16)
        swiglu(hs_ref, as_ref)

    @pl.when(i > 0)
    def _():
        swiglu(hp_ref, ap_ref)


def _ffn_a(h2_p, h2_s, w_gate, w_up, *, tm):
    rows_p, rows_s = h2_p.shape[0], h2_s.shape[0]
    tn = 512
    ni = rows_p // tm
    tile = lambda i: jnp.maximum(i - 1, 0)
    return pl.pallas_call(
        _ffn_a_kernel,
        grid=(D_FF // tn, ni + 1),
        in_specs=[
            pl.BlockSpec((tm, D_MODEL), lambda j, i: (tile(i), 0)),
            pl.BlockSpec((rows_s, D_MODEL), lambda j, i: (0, 0)),
            pl.BlockSpec((D_MODEL, tn), lambda j, i: (0, j)),
            pl.BlockSpec((D_MODEL, tn), lambda j, i: (0, j)),
        ],
        out_specs=[
            pl.BlockSpec((tm, tn), lambda j, i: (tile(i), j)),
            pl.BlockSpec((rows_s, tn), lambda j, i: (0, j)),
        ],
        out_shape=[
            jax.ShapeDtypeStruct((rows_p, D_FF), BF16),
            jax.ShapeDtypeStruct((rows_s, D_FF), BF16),
        ],
        scratch_shapes=[pltpu.VMEM((D_MODEL, tn), BF16), pltpu.VMEM((D_MODEL, tn), BF16)],
        compiler_params=_cparams(("arbitrary", "arbitrary")),
        name="ffn_a",
    )(h2_p, h2_s, w_gate, w_up)


def _ffn_b_kernel(a_ref, wd_ref, x1_ref, g2_ref, fw_ref, y_ref, *rest, per_row, emit_w):
    kk = pl.program_id(1)

    def weight_tile():
        wt = wd_ref[...].astype(BF16)
        if emit_w:
            rest[0][...] = wt
        return wt

    @pl.when(kk == 0)
    def _():
        y_ref[...] = _dot(a_ref[...], weight_tile())

    @pl.when(kk > 0)
    def _():
        y_ref[...] += _dot(a_ref[...], weight_tile())

    @pl.when(kk == pl.num_programs(1) - 1)
    def _():
        x2 = x1_ref[...] + _mod_row(g2_ref, per_row) * y_ref[...]
        y_ref[...] = x2 * lax.rsqrt(jnp.mean(x2 * x2, axis=-1, keepdims=True) + EPS) * fw_ref[...]


def _ffn_b(a, w_down, x1, mod, mod_row_block, final_w, *, tm, per_row, emit_w):
    rows = a.shape[0]
    tk = 512
    mod_rows = tm if per_row else 8
    mod_idx = (lambda i: i) if per_row else (lambda i: mod_row_block)
    kern = functools.partial(_ffn_b_kernel, per_row=per_row, emit_w=emit_w)
    w_spec = pl.BlockSpec((tk, D_MODEL), lambda i, k: (k, 0))
    y_spec = pl.BlockSpec((tm, D_MODEL), lambda i, k: (i, 0))
    y_shape = jax.ShapeDtypeStruct((rows, D_MODEL), F32)
    return pl.pallas_call(
        kern,
        grid=(rows // tm, D_FF // tk),
        in_specs=[
            pl.BlockSpec((tm, tk), lambda i, k: (i, k)),
            w_spec,
            pl.BlockSpec((tm, D_MODEL), lambda i, k: (i, 0)),
            pl.BlockSpec((mod_rows, D_MODEL), lambda i, k: (mod_idx(i), 5)),
            pl.BlockSpec((1, D_MODEL), lambda i, k: (0, 0)),
        ],
        out_specs=[y_spec, w_spec] if emit_w else y_spec,
        out_shape=[y_shape, jax.ShapeDtypeStruct((D_FF, D_MODEL), BF16)] if emit_w else y_shape,
        compiler_params=_cparams(("arbitrary", "arbitrary")),
        name="ffn_b_s" if per_row else "ffn_b_p",
    )(a, w_down, x1, mod, final_w)


def _rope_tables(pos):
    half = ROPE_DIM // 2
    inv = np.float32(ROPE_THETA) ** (-np.arange(0, ROPE_DIM, 2, dtype=np.float32) / np.float32(ROPE_DIM))
    d = np.arange(LANES) % ATT_HEAD_DIM
    ang = pos.astype(np.float32)[:, None] * inv[d % half][None, :].astype(np.float32)
    cos, sin = np.cos(ang), np.sin(ang)
    d = d[None, :]
    tables = (np.where(d < ROPE_DIM, cos, 1.0), np.where(d < half, -sin, 0.0),
              np.where((d >= half) & (d < ROPE_DIM), sin, 0.0))
    return tuple(jnp.asarray(t.astype(np.float32)) for t in tables)


def kernel(x_prompt, x_sample, cache_k_win, cache_v_win, state_C, state_n, state_m, c_prompt, c_sample,
           norm1_w, norm2_w, final_norm_w, w_ada, b_ada, w_in, b_ig, b_fg, attn_sinks, mh_norm_w,
           w_out, w_gate, w_up, w_down):
    assert w_in.shape[0] == 1, "single-layer trunk"
    T = DEC_SEQ
    xp = x_prompt[0]
    xs = x_sample.reshape(S_ROWS, D_MODEL)

    c_all = jnp.concatenate([c_sample, c_prompt, jnp.zeros((15, D_MODEL), F32)], axis=0)
    mod = _ada(c_all, w_ada[0], b_ada)
    prompt_mod_block = S_ROWS // 8

    w_in_t = jnp.transpose(w_in[0])
    wq_t = (w_in_t[:ATT_WIDTH].reshape(ATT_KV_HEADS, ATT_GROUP, ATT_HEAD_DIM, D_MODEL)
            .transpose(1, 0, 2, 3).reshape(ATT_WIDTH, D_MODEL).astype(BF16))
    w_gates_t = w_in_t[MAIN_WIDTH:]
    wg = jnp.pad(w_gates_t, ((0, 16 - 2 * MLSTM_HEADS), (0, 0)))
    w_out_att = (w_out[0, :ATT_WIDTH].reshape(ATT_KV_HEADS, ATT_GROUP, ATT_HEAD_DIM, D_MODEL)
                 .transpose(1, 0, 2, 3).reshape(ATT_WIDTH, D_MODEL).astype(BF16))
    w_out_m = w_out[0, ATT_WIDTH:].astype(BF16)
    n1 = norm1_w.reshape(1, D_MODEL)
    n2 = norm2_w.reshape(1, D_MODEL)
    fw = final_norm_w.reshape(1, D_MODEL)
    nw = mh_norm_w.reshape(1, MLSTM_WIDTH)
    gate_bias = jnp.concatenate([b_ig[0], b_fg[0]])
    brow = jnp.pad(gate_bias, (0, LANES - 2 * MLSTM_HEADS)).reshape(1, LANES)
    bcol = jnp.broadcast_to(jnp.pad(gate_bias, (0, 16 - 2 * MLSTM_HEADS))[:, None], (16, MLSTM_CHUNK_P))

    rope_p = _rope_tables(np.arange(SEQ))
    rope_s = _rope_tables(np.tile(PAST_LEN + np.arange(T), DEC_BATCH))

    tm_p = 1024
    q_s, _, kv32_s, m_s, g_s, _, w_in_bf = _inproj(xs, mod, 0, n1, wq_t, w_in_t, wg, *rope_s,
                                                   tm=S_ROWS, per_row=True, emit_w=True)
    q_p, kv_p, kv32_p, m_p, g_p, gt_p = _inproj(xp, mod, prompt_mod_block, n1, wq_t, w_in_bf, wg, *rope_p,
                                                tm=tm_p, per_row=False, emit_w=False)

    sinks = attn_sinks[0]
    sink_col = jnp.broadcast_to(sinks.reshape(ATT_HEADS, 1, 1), (ATT_HEADS, 8, LANES)).reshape(128, LANES)
    ck = jnp.transpose(cache_k_win[0], (0, 2, 3, 1))
    cv = jnp.transpose(cache_v_win[0], (0, 2, 3, 1))
    att_s, kwin_s, vwin_s = _attn_s(sink_col, q_s, kv32_s, ck, cv)

    n_rep = jnp.repeat(state_n[0].reshape(DEC_BATCH, MLSTM_WIDTH), T, axis=0)
    m_rep = jnp.pad(jnp.repeat(state_m[0], T, axis=0), ((0, 0), (0, LANES - MLSTM_HEADS)))
    mh_s, c_s, nrow_s, mrow_s = _mlstm_s(m_s, g_s, n_rep, m_rep, brow, nw, state_C[0])

    x1_p, h2_p, c_p, n_p, mm_p = _mix_p(sinks, q_p, kv_p, m_p, g_p, gt_p, brow, bcol, nw, xp, mod, prompt_mod_block,
                                        n2, w_out_att, w_out_m)
    x1_s, h2_s = _outproj(att_s, mh_s, xs, mod, 0, n2, w_out_att, w_out_m, tm=S_ROWS, per_row=True)
    a_p, a_s = _ffn_a(h2_p, h2_s, w_gate[0], w_up[0], tm=tm_p)
    y_s, w_down_bf = _ffn_b(a_s, w_down[0], x1_s, mod, 0, fw, tm=S_ROWS, per_row=True, emit_w=True)
    y_p = _ffn_b(a_p, w_down_bf, x1_p, mod, prompt_mod_block, fw, tm=tm_p, per_row=False, emit_w=False)

    kv_shape = (1, 1, WINDOW, ATT_KV_HEADS, ATT_HEAD_DIM)
    kv_last = kv32_p[tm_p - WINDOW:]
    dh = MLSTM_HEAD_DIM
    return (
        y_p.reshape(1, SEQ, D_MODEL),
        y_s.reshape(DEC_BATCH, T, D_MODEL),
        kv_last[:, :KV_WIDTH].reshape(kv_shape),
        kv_last[:, KV_WIDTH:].reshape(kv_shape),
        c_p.reshape(1, 1, MLSTM_HEADS, dh, dh),
        n_p[:MLSTM_HEADS].reshape(1, 1, MLSTM_HEADS, dh),
        mm_p[:MLSTM_HEADS, 0].reshape(1, 1, MLSTM_HEADS),
        jnp.transpose(kwin_s, (0, 3, 1, 2))[None],
        jnp.transpose(vwin_s, (0, 3, 1, 2))[None],
        c_s.reshape(1, DEC_BATCH, MLSTM_HEADS, dh, dh),
        nrow_s[T - 1::T].reshape(1, DEC_BATCH, MLSTM_HEADS, dh),
        mrow_s[T - 1::T, :MLSTM_HEADS].reshape(1, DEC_BATCH, MLSTM_HEADS),
    )
```

```python
import functools

import jax
import jax.numpy as jnp
import numpy as np
from jax import lax
from jax.experimental import pallas as pl
from jax.experimental.pallas import tpu as pltpu

F32 = jnp.float32
BF16 = jnp.bfloat16

D_MODEL = 2048
SEQ = 8192
DEC_BATCH = 128
DEC_SEQ = 4
S_ROWS = DEC_BATCH * DEC_SEQ
PAST_LEN = 16384
ATT_HEADS = 16
ATT_KV_HEADS = 4
ATT_GROUP = 4
ATT_HEAD_DIM = 64
WINDOW = 128
ROPE_THETA = 500000.0
ROPE_DIM = 16
MLSTM_HEADS = 4
MLSTM_HEAD_DIM = 256
ATT_WIDTH = 1024
KV_WIDTH = 256
MLSTM_WIDTH = 1024
MAIN_WIDTH = ATT_WIDTH + 2 * KV_WIDTH + 4 * MLSTM_WIDTH
D_FF = 5632
N_MOD = 6
EPS = 1e-6

LANES = 128
MLSTM_CHUNK_P = 256
VMEM_LIMIT = 56 * 1024 * 1024

NT_DIMS = (((1,), (1,)), ((), ()))
TN_DIMS = (((0,), (0,)), ((), ()))
HI = lax.Precision.HIGHEST


def _cparams(sem):
    return pltpu.CompilerParams(dimension_semantics=sem, vmem_limit_bytes=VMEM_LIMIT)


def _dot(a, b):
    return jnp.dot(a, b, preferred_element_type=F32)


def _dot_nt(a, b):
    return lax.dot_general(a, b, NT_DIMS, preferred_element_type=F32)


def _dot_tn(a, b):
    return lax.dot_general(a, b, TN_DIMS, preferred_element_type=F32)


def _sigmoid(x):
    return 1.0 / (1.0 + jnp.exp(-x))


def _log_sigmoid(x):
    return jnp.minimum(x, 0.0) - jnp.log(1.0 + jnp.exp(-jnp.abs(x)))


def _mod_row(ref, per_row):
    return ref[...] if per_row else ref[0:1, :]


def _ada_kernel(c_ref, w_ref, b_ref, o_ref, s_scr, rep_scr):
    @pl.when(pl.program_id(0) == 0)
    def _():
        c = c_ref[...]
        s_scr[...] = (c * _sigmoid(c)).astype(BF16)

    mod = _dot(s_scr[...], w_ref[...].astype(BF16)) + b_ref[...]
    for c in range(o_ref.shape[1] // LANES):
        cols = slice(LANES * c, LANES * (c + 1))
        for t in range(DEC_SEQ):
            rep_scr[pl.ds(t, DEC_BATCH, stride=DEC_SEQ), :] = mod[0:DEC_BATCH, cols]
        o_ref[0:S_ROWS, cols] = rep_scr[...]
    o_ref[S_ROWS:, :] = mod[DEC_BATCH:]


def _ada(c_all, w_ada, b_ada):
    m = c_all.shape[0]
    m_out = S_ROWS + m - DEC_BATCH
    n = w_ada.shape[1]
    tn = 2048
    return pl.pallas_call(
        _ada_kernel,
        grid=(n // tn,),
        in_specs=[
            pl.BlockSpec((m, D_MODEL), lambda j: (0, 0)),
            pl.BlockSpec((D_MODEL, tn), lambda j: (0, j)),
            pl.BlockSpec((1, tn), lambda j: (0, j)),
        ],
        out_specs=pl.BlockSpec((m_out, tn), lambda j: (0, j)),
        out_shape=jax.ShapeDtypeStruct((m_out, n), F32),
        scratch_shapes=[pltpu.VMEM((m, D_MODEL), BF16), pltpu.VMEM((S_ROWS, LANES), F32)],
        compiler_params=_cparams(("arbitrary",)),
        name="ada",
    )(c_all, w_ada, b_ada)


def _rope_store(acc, cos, sa, sb, out_ref, ncols, scale):
    for c in range(ncols // LANES):
        xc = acc[:, LANES * c:LANES * (c + 1)]
        rot = xc * cos + pltpu.roll(xc, LANES - 8, 1) * sa + pltpu.roll(xc, 8, 1) * sb
        if scale != 1.0:
            rot = rot * scale
        out_ref[:, LANES * c:LANES * (c + 1)] = rot.astype(out_ref.dtype)


def _inproj_kernel(x_ref, sh_ref, sc_ref, nw_ref, wq_ref, win_ref, wg_ref, cos_ref, sa_ref, sb_ref,
                   q_ref, kv_ref, kv32_ref, m_ref, g_ref, gt_ref, *rest, per_row, emit_w):
    h_scr = rest[-1]
    j = pl.program_id(1)

    def weight_tile():
        wt = win_ref[...].astype(BF16)
        if emit_w:
            rest[0][...] = wt
        return wt

    @pl.when(j == 0)
    def _():
        x = x_ref[...]
        r = lax.rsqrt(jnp.mean(x * x, axis=-1, keepdims=True) + EPS)
        gain = nw_ref[...] * (1.0 + _mod_row(sc_ref, per_row))
        h = (x * r * gain + _mod_row(sh_ref, per_row)).astype(BF16)
        h_scr[...] = h
        wg = jnp.concatenate([wg_ref[...].astype(BF16), jnp.zeros((LANES - 16, D_MODEL), BF16)], axis=0)
        g = _dot_nt(h, wg)
        g_ref[...] = g
        gt_ref[...] = g.T[0:16, :]

    @pl.when(j < 2)
    def _():
        acc = _dot_nt(h_scr[...], wq_ref[...])
        _rope_store(acc, cos_ref[...], sa_ref[...], sb_ref[...], q_ref, 512, ATT_HEAD_DIM ** -0.5)

    @pl.when(j == 2)
    def _():
        acc = _dot_nt(h_scr[...], weight_tile())
        _rope_store(acc, cos_ref[...], sa_ref[...], sb_ref[...], kv32_ref, KV_WIDTH, 1.0)
        kv32_ref[:, KV_WIDTH:] = acc[:, KV_WIDTH:]
        kv_ref[...] = kv32_ref[...].astype(BF16)

    @pl.when(j > 2)
    def _():
        m_ref[...] = _dot_nt(h_scr[...], weight_tile()).astype(BF16)


def _inproj(x, mod, mod_row_block, norm_w, wq_t, w_in_t, wg, cos, sa, sb, *, tm, per_row, emit_w):
    rows = x.shape[0]
    tn = 512
    nj = MAIN_WIDTH // tn
    mod_rows = tm if per_row else 8
    mod_idx = (lambda i: i) if per_row else (lambda i: mod_row_block)
    kern = functools.partial(_inproj_kernel, per_row=per_row, emit_w=emit_w)
    copy_spec = pl.BlockSpec((tn, D_MODEL), lambda i, j: (jnp.maximum(j, 2) - 2, 0))
    w_spec = pl.BlockSpec((tn, D_MODEL), lambda i, j: (jnp.maximum(j, 2), 0)) if emit_w else copy_spec
    extra_specs = [copy_spec] if emit_w else []
    extra_shapes = [jax.ShapeDtypeStruct((MAIN_WIDTH - ATT_WIDTH, D_MODEL), BF16)] if emit_w else []
    return pl.pallas_call(
        kern,
        grid=(rows // tm, nj),
        in_specs=[
            pl.BlockSpec((tm, D_MODEL), lambda i, j: (jnp.minimum(i + jnp.minimum(j, 1), rows // tm - 1), 0)),
            pl.BlockSpec((mod_rows, D_MODEL), lambda i, j: (mod_idx(i), 0)),
            pl.BlockSpec((mod_rows, D_MODEL), lambda i, j: (mod_idx(i), 1)),
            pl.BlockSpec((1, D_MODEL), lambda i, j: (0, 0)),
            pl.BlockSpec((tn, D_MODEL), lambda i, j: (jnp.minimum(j, 1), 0)),
            w_spec,
            pl.BlockSpec((16, D_MODEL), lambda i, j: (0, 0)),
            pl.BlockSpec((tm, LANES), lambda i, j: (i, 0)),
            pl.BlockSpec((tm, LANES), lambda i, j: (i, 0)),
            pl.BlockSpec((tm, LANES), lambda i, j: (i, 0)),
        ],
        out_specs=[
            pl.BlockSpec((tm, tn), lambda i, j: (i, jnp.minimum(j, 1))),
            pl.BlockSpec((tm, tn), lambda i, j: (i, 0)),
            pl.BlockSpec((tm, tn), lambda i, j: (0, 0)),
            pl.BlockSpec((tm, tn), lambda i, j: (i, jnp.clip(j - 3, 0, 7))),
            pl.BlockSpec((tm, LANES), lambda i, j: (i, 0)),
            pl.BlockSpec((16, tm), lambda i, j: (0, i)),
        ] + extra_specs,
        out_shape=[
            jax.ShapeDtypeStruct((rows, ATT_WIDTH), BF16),
            jax.ShapeDtypeStruct((rows, 2 * KV_WIDTH), BF16),
            jax.ShapeDtypeStruct((tm, 2 * KV_WIDTH), F32),
            jax.ShapeDtypeStruct((rows, 4 * MLSTM_WIDTH), BF16),
            jax.ShapeDtypeStruct((rows, LANES), F32),
            jax.ShapeDtypeStruct((16, rows), F32),
        ] + extra_shapes,
        scratch_shapes=[pltpu.VMEM((tm, D_MODEL), BF16)],
        compiler_params=_cparams(("arbitrary", "arbitrary")),
        name="inproj_s" if per_row else "inproj_p",
    )(x, mod, mod, norm_w, wq_t, w_in_t, wg, cos, sa, sb)


def _attn_block(sink_ref, q_ref, row0, kv2, allowed, store):
    w = WINDOW
    grp = ATT_GROUP
    rows = grp * w
    member = lax.broadcasted_iota(jnp.int32, (rows, 1), 0) // w
    low = lax.broadcasted_iota(jnp.int32, (2 * w, LANES), 1) < ATT_HEAD_DIM
    low_o = lax.broadcasted_iota(jnp.int32, (rows, LANES), 1) < ATT_HEAD_DIM
    key_row = lax.broadcasted_iota(jnp.int32, (4 * w, LANES), 0)
    key_lane = lax.broadcasted_iota(jnp.int32, (4 * w, LANES), 1)
    ones_bd = (((key_row < 2 * w) & (key_lane < ATT_HEAD_DIM))
               | ((key_row >= 2 * w) & (key_lane >= ATT_HEAD_DIM))).astype(BF16)
    zero = jnp.zeros((2 * w, LANES), BF16)
    for cp in range(2):
        k128 = kv2[:, LANES * cp:LANES * (cp + 1)]
        v128 = kv2[:, KV_WIDTH + LANES * cp:KV_WIDTH + LANES * (cp + 1)]
        kbd = jnp.concatenate([jnp.where(low, k128, zero), jnp.where(low, zero, k128)], axis=0)
        vbd = jnp.concatenate([jnp.where(low, v128, zero), jnp.where(low, zero, v128)], axis=0)
        v_aug = jnp.concatenate([vbd, ones_bd], axis=1)
        q4 = jnp.concatenate([q_ref[row0:row0 + w, 256 * r + LANES * cp:256 * r + LANES * (cp + 1)]
                              for r in range(grp)], axis=0)
        s = _dot_nt(q4, kbd)
        es, tails = [], []
        for half in range(2):
            sh = jnp.where(allowed, s[:, 2 * w * half:2 * w * (half + 1)], -jnp.inf)
            head0 = (2 * cp + half) * grp
            sink = jnp.full((rows, 1), sink_ref[head0], F32)
            for r in range(1, grp):
                sink = jnp.where(member == r, sink_ref[head0 + r], sink)
            m = jnp.maximum(jnp.max(sh, axis=-1, keepdims=True), sink)
            es.append(jnp.exp(sh - m).astype(BF16))
            tails.append(jnp.exp(sink - m))
        oa = _dot(jnp.concatenate(es, axis=1), v_aug)
        l = oa[:, LANES:] + jnp.where(low_o, tails[0], tails[1])
        o = (oa[:, :LANES] / l).astype(BF16)
        for r in range(grp):
            store(256 * r + LANES * cp, o[w * r:w * (r + 1)])


MIX_TM = 512


def _mix_p_kernel(sink_ref, q_ref, kvp_ref, kvc_ref, mq_ref, mk_ref, mv_ref, mo_ref, g_ref, gt_ref, brow_ref, bcol_ref,
                  mnw_ref, x_ref, g1_ref, sh_ref, sc_ref, nw_ref, wa_ref, wm_ref,
                  x1_ref, h2_ref, c_out, n_out, m_out, att_scr, mh_scr, c_scr, n_scr, m_scr):
    s = pl.program_id(0)
    w = WINDOW
    tiles = pl.num_programs(0) - 1
    slot = s % 2

    @pl.when(s == 0)
    def _():
        c_scr[...] = jnp.zeros_like(c_scr)
        n_scr[...] = jnp.zeros_like(n_scr)
        m_scr[...] = jnp.zeros_like(m_scr)

    def project():
        y = _dot(att_scr[1 - slot], wa_ref[...]) + _dot(mh_scr[1 - slot], wm_ref[...])
        x1 = x_ref[...] + g1_ref[0:1, :] * y
        x1_ref[...] = x1
        r = lax.rsqrt(jnp.mean(x1 * x1, axis=-1, keepdims=True) + EPS)
        gain = nw_ref[...] * (1.0 + sc_ref[0:1, :])
        h2_ref[...] = (x1 * r * gain + sh_ref[0:1, :]).astype(BF16)

    def mixers():
        rows = ATT_GROUP * w
        qi = lax.broadcasted_iota(jnp.int32, (rows, 2 * w), 0) % w
        kj = lax.broadcasted_iota(jnp.int32, (rows, 2 * w), 1)
        first_off = jnp.where(s > 0, 0, 4 * w)
        causal = (kj >= w) & (kj - w <= qi)
        for blk in range(MIX_TM // w):
            prev = kvp_ref[...] if blk == 0 else kvc_ref[w * (blk - 1):w * blk, :]
            kv2 = jnp.concatenate([prev, kvc_ref[w * blk:w * (blk + 1), :]], axis=0)
            allowed = ((kj < w) & (kj > qi + (first_off if blk == 0 else 0))) | causal

            def store(c0, val, blk=blk):
                att_scr[slot, w * blk:w * (blk + 1), c0:c0 + LANES] = val

            _attn_block(sink_ref, q_ref, w * blk, kv2, allowed, store)

        for ch in range(MIX_TM // MLSTM_CHUNK_P):
            r0 = MLSTM_CHUNK_P * ch

            def store_mh(cs, val, r0=r0):
                mh_scr[slot, r0:r0 + MLSTM_CHUNK_P, cs] = val

            _mlstm_chunk(mq_ref, mk_ref, mv_ref, mo_ref, g_ref, gt_ref, brow_ref, bcol_ref, mnw_ref,
                         c_scr, n_scr, m_scr, r0, store_mh)

    @pl.when(s == 0)
    def _():
        mixers()

    @pl.when((s > 0) & (s < tiles))
    def _():
        project()
        mixers()

    @pl.when(s == tiles)
    def _():
        project()
        c_out[...] = c_scr[...]
        n_out[...] = n_scr[...]
        m_out[...] = m_scr[...]


def _mix_p(sinks, q, kv, m_all, gates, gates_t, brow, bcol, mnw, x, mod, mod_row_block, norm_w, w_att, w_m):
    tm = MIX_TM
    tiles = SEQ // tm
    bpt = tm // WINDOW
    dh = MLSTM_HEAD_DIM
    att_tile = lambda s: jnp.minimum(s, tiles - 1)
    out_tile = lambda s: jnp.maximum(s - 1, 0)
    m_spec = lambda col: pl.BlockSpec((tm, MLSTM_WIDTH), lambda s: (att_tile(s), col))
    return pl.pallas_call(
        _mix_p_kernel,
        grid=(tiles + 1,),
        in_specs=[
            pl.BlockSpec(memory_space=pltpu.SMEM),
            pl.BlockSpec((tm, ATT_WIDTH), lambda s: (att_tile(s), 0)),
            pl.BlockSpec((WINDOW, 2 * KV_WIDTH), lambda s: (jnp.maximum(bpt * att_tile(s) - 1, 0), 0)),
            pl.BlockSpec((tm, 2 * KV_WIDTH), lambda s: (att_tile(s), 0)),
            m_spec(0), m_spec(1), m_spec(2), m_spec(3),
            pl.BlockSpec((tm, LANES), lambda s: (att_tile(s), 0)),
            pl.BlockSpec((16, tm), lambda s: (0, att_tile(s))),
            pl.BlockSpec((1, LANES), lambda s: (0, 0)),
            pl.BlockSpec((16, MLSTM_CHUNK_P), lambda s: (0, 0)),
            pl.BlockSpec((1, MLSTM_WIDTH), lambda s: (0, 0)),
            pl.BlockSpec((tm, D_MODEL), lambda s: (out_tile(s), 0)),
            pl.BlockSpec((8, D_MODEL), lambda s: (mod_row_block, 2)),
            pl.BlockSpec((8, D_MODEL), lambda s: (mod_row_block, 3)),
            pl.BlockSpec((8, D_MODEL), lambda s: (mod_row_block, 4)),
            pl.BlockSpec((1, D_MODEL), lambda s: (0, 0)),
            pl.BlockSpec((ATT_WIDTH, D_MODEL), lambda s: (0, 0)),
            pl.BlockSpec((MLSTM_WIDTH, D_MODEL), lambda s: (0, 0)),
        ],
        out_specs=[
            pl.BlockSpec((tm, D_MODEL), lambda s: (out_tile(s), 0)),
            pl.BlockSpec((tm, D_MODEL), lambda s: (out_tile(s), 0)),
            pl.BlockSpec((MLSTM_HEADS, dh, dh), lambda s: (0, 0, 0)),
            pl.BlockSpec((8, dh), lambda s: (0, 0)),
            pl.BlockSpec((8, LANES), lambda s: (0, 0)),
        ],
        out_shape=[
            jax.ShapeDtypeStruct((SEQ, D_MODEL), F32),
            jax.ShapeDtypeStruct((SEQ, D_MODEL), BF16),
            jax.ShapeDtypeStruct((MLSTM_HEADS, dh, dh), F32),
            jax.ShapeDtypeStruct((8, dh), F32),
            jax.ShapeDtypeStruct((8, LANES), F32),
        ],
        scratch_shapes=[
            pltpu.VMEM((2, tm, ATT_WIDTH), BF16),
            pltpu.VMEM((2, tm, MLSTM_WIDTH), BF16),
            pltpu.VMEM((MLSTM_HEADS, dh, dh), F32),
            pltpu.VMEM((8, dh), F32),
            pltpu.VMEM((8, LANES), F32),
        ],
        compiler_params=_cparams(("arbitrary",)),
        name="mix_p",
    )(sinks, q, kv, kv, m_all, m_all, m_all, m_all, gates, gates_t, brow, bcol, mnw, x, mod, mod, mod, norm_w,
      w_att, w_m)


ATT_S_BB = 16


def _attn_s_kernel(sink_ref, q_ref, kv32_ref, ck_ref, cv_ref, o_ref, ko_ref, vo_ref, q32_scr):
    t_new = DEC_SEQ
    w = WINDOW
    q32_scr[...] = q_ref[...].astype(F32)
    rows = 4 * 4 * 8
    row = lax.broadcasted_iota(jnp.int32, (rows, w), 0)
    slot = lax.broadcasted_iota(jnp.int32, (rows, w), 1)
    t_row = row % t_new
    second = (row % 8) >= t_new
    win_ok = (slot < w - t_new) | (slot - (w - t_new) <= t_row)
    old_ok = (slot >= 1) & (slot < t_new) & (slot > t_row)
    lane256 = lax.broadcasted_iota(jnp.int32, (32, 2 * LANES), 1)
    sink = sink_ref[...][:, 0:1]
    kv_new = jnp.concatenate([kv32_ref[...], jnp.zeros((w - ATT_S_BB * t_new, 2 * KV_WIDTH), F32)], axis=0)
    kv_t = kv_new.T
    new_slot = lax.broadcasted_iota(jnp.int32, (KV_WIDTH, w), 1) >= w - t_new
    for b in range(ATT_S_BB):
        cols = pltpu.roll(kv_t, w - t_new - t_new * b, 1)
        k_shift = pltpu.roll(ck_ref[b].reshape(KV_WIDTH, w), w - t_new, 1)
        v_shift = pltpu.roll(cv_ref[b].reshape(KV_WIDTH, w), w - t_new, 1)
        ko_ref[b] = jnp.where(new_slot, cols[:KV_WIDTH], k_shift).reshape(ATT_KV_HEADS, ATT_HEAD_DIM, w)
        vo_ref[b] = jnp.where(new_slot, cols[KV_WIDTH:], v_shift).reshape(ATT_KV_HEADS, ATT_HEAD_DIM, w)
    for pair in range(ATT_S_BB // 2):
        b0, b1 = 2 * pair, 2 * pair + 1
        q32 = jnp.concatenate([q32_scr[8 * pair:8 * (pair + 1), 256 * r:256 * (r + 1)] for r in range(ATT_GROUP)],
                              axis=0)
        qpad = jnp.concatenate(
            [jnp.where((lane256 // ATT_HEAD_DIM) == g, q32, 0.0) for g in range(ATT_KV_HEADS)], axis=0).astype(BF16)
        kw = [ko_ref[b].reshape(KV_WIDTH, w).astype(BF16) for b in (b0, b1)]
        vw = [vo_ref[b].reshape(KV_WIDTH, w).astype(BF16) for b in (b0, b1)]
        kc = [ck_ref[b].reshape(KV_WIDTH, w).astype(BF16) for b in (b0, b1)]
        vc = [cv_ref[b].reshape(KV_WIDTH, w).astype(BF16) for b in (b0, b1)]
        s_w = jnp.where(second, _dot(qpad, kw[1]), _dot(qpad, kw[0]))
        s_c = jnp.where(second, _dot(qpad, kc[1]), _dot(qpad, kc[0]))
        s_w = jnp.where(win_ok, s_w, -jnp.inf)
        s_c = jnp.where(old_ok, s_c, -jnp.inf)
        m = jnp.maximum(jnp.maximum(jnp.max(s_w, axis=-1, keepdims=True), jnp.max(s_c, axis=-1, keepdims=True)), sink)
        e_w = jnp.exp(s_w - m)
        e_c = jnp.exp(s_c - m)
        l = jnp.sum(e_w, axis=-1, keepdims=True) + jnp.sum(e_c, axis=-1, keepdims=True) + jnp.exp(sink - m)
        p_w = e_w / l
        p_c = e_c / l
        zero = jnp.zeros_like(p_w)
        o = (_dot_nt(jnp.where(second, zero, p_w).astype(BF16), vw[0])
             + _dot_nt(jnp.where(second, p_w, zero).astype(BF16), vw[1])
             + _dot_nt(jnp.where(second, zero, p_c).astype(BF16), vc[0])
             + _dot_nt(jnp.where(second, p_c, zero).astype(BF16), vc[1]))
        o32 = jnp.zeros((32, 2 * LANES), F32)
        for g in range(ATT_KV_HEADS):
            o32 = jnp.where((lane256 // ATT_HEAD_DIM) == g, o[32 * g:32 * (g + 1), :], o32)
        for r in range(ATT_GROUP):
            o_ref[8 * pair:8 * (pair + 1), 256 * r:256 * (r + 1)] = o32[8 * r:8 * (r + 1), :]


def _attn_s(sink_col, q, kv32, ck, cv):
    bb = ATT_S_BB
    rows = bb * DEC_SEQ
    cache_block = (bb, ATT_KV_HEADS, ATT_HEAD_DIM, WINDOW)
    cache_shape = (DEC_BATCH, ATT_KV_HEADS, ATT_HEAD_DIM, WINDOW)
    return pl.pallas_call(
        _attn_s_kernel,
        grid=(DEC_BATCH // bb,),
        in_specs=[
            pl.BlockSpec((128, LANES), lambda i: (0, 0)),
            pl.BlockSpec((rows, ATT_WIDTH), lambda i: (i, 0)),
            pl.BlockSpec((rows, 2 * KV_WIDTH), lambda i: (i, 0)),
            pl.BlockSpec(cache_block, lambda i: (i, 0, 0, 0)),
            pl.BlockSpec(cache_block, lambda i: (i, 0, 0, 0)),
        ],
        out_specs=[
            pl.BlockSpec((rows, ATT_WIDTH), lambda i: (i, 0)),
            pl.BlockSpec(cache_block, lambda i: (i, 0, 0, 0)),
            pl.BlockSpec(cache_block, lambda i: (i, 0, 0, 0)),
        ],
        out_shape=[
            jax.ShapeDtypeStruct((S_ROWS, ATT_WIDTH), F32),
            jax.ShapeDtypeStruct(cache_shape, F32),
            jax.ShapeDtypeStruct(cache_shape, F32),
        ],
        scratch_shapes=[pltpu.VMEM((rows, ATT_WIDTH), F32)],
        compiler_params=_cparams(("arbitrary",)),
        name="attn_s",
    )(sink_col, q, kv32, ck, cv)


def _head_norm_gate(h, nw, mo):
    hn = h * lax.rsqrt(jnp.mean(h * h, axis=-1, keepdims=True) + EPS) * nw
    return hn * _sigmoid(mo.astype(F32))


def _mlstm_chunk(q_ref, k_ref, v_ref, mo_ref, g_ref, gt_ref, brow_ref, bcol_ref, nw_ref, c_scr, n_scr, m_scr,
                 r0, store):
    L = MLSTM_CHUNK_P
    dh = MLSTM_HEAD_DIM
    rs = slice(r0, r0 + L)
    ti = lax.broadcasted_iota(jnp.int32, (L, L), 0)
    si = lax.broadcasted_iota(jnp.int32, (L, L), 1)
    causal = si <= ti
    tri = causal.astype(F32)
    tri_t = (ti <= si).astype(F32)
    gates = g_ref[rs, :] + brow_ref[...]
    gates_t = gt_ref[:, rs] + bcol_ref[...]
    b_col = jnp.dot(tri, _log_sigmoid(gates), precision=HI, preferred_element_type=F32)
    b_row = jnp.dot(_log_sigmoid(gates_t), tri_t, precision=HI, preferred_element_type=F32)
    for hd in range(MLSTM_HEADS):
        cs = slice(dh * hd, dh * (hd + 1))
        b_c = b_col[:, MLSTM_HEADS + hd:MLSTM_HEADS + hd + 1]
        li_c = gates[:, hd:hd + 1]
        b_r = b_row[MLSTM_HEADS + hd:MLSTM_HEADS + hd + 1, :]
        li_r = gates_t[hd:hd + 1, :]
        dm = jnp.where(causal, b_c - b_r + li_r, -jnp.inf)
        m_prev = m_scr[hd:hd + 1, 0:1]
        m_inter = b_c + m_prev
        m_t = jnp.maximum(m_inter, jnp.max(dm, axis=-1, keepdims=True))
        q = q_ref[rs, cs] * (dh ** -0.5)
        k = k_ref[rs, cs]
        v = v_ref[rs, cs]
        sm = _dot_nt(q, k) * jnp.exp(dm - m_t)
        a = jnp.exp(m_inter - m_t)
        c_old = c_scr[hd]
        n_old = n_scr[hd:hd + 1, :]
        num = a * _dot_nt(q, c_old.astype(BF16)) + _dot(sm.astype(BF16), v)
        qn = jnp.sum(q.astype(F32) * n_old, axis=-1, keepdims=True)
        den = a * qn + jnp.sum(sm, axis=-1, keepdims=True)
        h = num / jnp.maximum(jnp.abs(den), jnp.exp(-m_t))
        store(cs, _head_norm_gate(h, nw_ref[:, cs], mo_ref[rs, cs]).astype(BF16))
        m_new = m_t[L - 1:L, :]
        b_last = b_c[L - 1:L, :]
        g = jnp.exp(b_last - b_c + li_c - m_new)
        decay = jnp.exp(b_last + m_prev - m_new)
        gv = (g * v.astype(F32)).astype(BF16)
        c_scr[hd] = decay * c_old + _dot_tn(gv, k)
        n_scr[hd:hd + 1, :] = decay * n_old + jnp.sum(g * k.astype(F32), axis=0, keepdims=True)
        m_scr[hd:hd + 1, :] = jnp.broadcast_to(m_new, (1, LANES))


MLSTM_S_BB = 8


def _mlstm_s_kernel(q_ref, k_ref, v_ref, mo_ref, g_ref, nrep_ref, mrep_ref, brow_ref, nw_ref, c_ref,
                    mh_ref, c_out, nrow_ref, mrow_ref):
    T = DEC_SEQ
    R = MLSTM_S_BB * T
    H = MLSTM_HEADS
    dh = MLSTM_HEAD_DIM

    def shift(x, d):
        return pltpu.roll(x, d, 0)

    lanes = lax.broadcasted_iota(jnp.int32, (R, LANES), 1)
    tmod = lax.broadcasted_iota(jnp.int32, (R, LANES), 0) % T
    tmod_w = lax.broadcasted_iota(jnp.int32, (R, MLSTM_WIDTH), 0) % T
    head_ok = lanes < H
    gates = g_ref[...] + brow_ref[...]
    li = jnp.where(head_ok, gates, 0.0)
    lf = jnp.where(head_ok, pltpu.roll(_log_sigmoid(gates), LANES - H, 1), 0.0)
    bcum = lf
    for d in range(1, T):
        bcum = bcum + jnp.where(tmod >= d, shift(lf, d), 0.0)
    m0 = mrep_ref[...]
    m_inter = bcum + m0
    dms = [li] + [jnp.where(tmod >= d, bcum - shift(bcum, d) + shift(li, d), -jnp.inf) for d in range(1, T)]
    m_t = m_inter
    for dm in dms:
        m_t = jnp.maximum(m_t, dm)
    a = jnp.exp(m_inter - m_t)
    ws = [jnp.exp(dm - m_t) for dm in dms]

    q_bf = q_ref[...] * (dh ** -0.5)
    q = q_bf.astype(F32)
    k = k_ref[...].astype(F32)
    v = v_ref[...].astype(F32)
    seg = (lax.broadcasted_iota(jnp.int32, (MLSTM_WIDTH, LANES), 0) // dh
           == lax.broadcasted_iota(jnp.int32, (MLSTM_WIDTH, LANES), 1)).astype(F32)
    ex = (lax.broadcasted_iota(jnp.int32, (LANES, MLSTM_WIDTH), 1) // dh
          == lax.broadcasted_iota(jnp.int32, (LANES, MLSTM_WIDTH), 0)).astype(F32)

    def segsum(x):
        return jnp.dot(x, seg, precision=HI, preferred_element_type=F32)

    def expand(x):
        return jnp.dot(x, ex, precision=HI, preferred_element_type=F32)

    ks = [k] + [shift(k, d) for d in range(1, T)]
    vs = [v] + [shift(v, d) for d in range(1, T)]
    sms = [segsum(q * ks[d]) * ws[d] for d in range(T)]
    den = a * segsum(q * nrep_ref[...])
    for sm in sms:
        den = den + sm
    inv = 1.0 / jnp.maximum(jnp.abs(den), jnp.exp(-m_t))

    def last(x):
        out = jnp.zeros_like(x)
        for jj in range(T):
            out = jnp.where(tmod == T - 1 - jj, x if jj == 0 else pltpu.roll(x, R - jj, 0), out)
        return out

    m_new = last(m_t)
    b_last = last(bcum)
    g = jnp.where(head_ok, jnp.exp(b_last - bcum + li - m_new), 0.0)
    decay = jnp.where(head_ok, jnp.exp(b_last + m0 - m_new), 0.0)
    a_f = expand(a * inv)
    w_f = [expand(sm * inv) for sm in sms]
    g_f = expand(g)
    d_f = expand(decay)
    gv = (g_f * v).astype(BF16)
    rowb = lax.broadcasted_iota(jnp.int32, (R, dh), 0) // T
    for hd in range(H):
        cs = slice(dh * hd, dh * (hd + 1))
        qh = q_bf[:, cs]
        kh = k_ref[:, cs]
        gvh = gv[:, cs]
        qc = jnp.zeros((R, dh), F32)
        for bb in range(MLSTM_S_BB):
            c_old = c_ref[bb, hd]
            qc = jnp.where(rowb == bb, _dot_nt(qh, c_old.astype(BF16)), qc)
            upd = _dot_tn(jnp.where(rowb == bb, gvh, jnp.zeros_like(gvh)), kh)
            c_out[bb, hd] = d_f[T * bb:T * bb + 1, cs] * c_old + upd
        h = a_f[:, cs] * qc
        for d in range(T):
            h = h + w_f[d][:, cs] * vs[d][:, cs]
        mh_ref[:, cs] = _head_norm_gate(h, nw_ref[:, cs], mo_ref[:, cs]).astype(BF16)
    gk = g_f * k
    nsum = gk
    for d in range(1, T):
        nsum = nsum + jnp.where(tmod_w >= d, shift(gk, d), 0.0)
    nrow_ref[...] = d_f * nrep_ref[...] + nsum
    mrow_ref[...] = m_t


def _mlstm_s(m_all, gates, n_rep, m_rep, brow, nw, state_c):
    bb = MLSTM_S_BB
    R = bb * DEC_SEQ
    dh = MLSTM_HEAD_DIM
    H = MLSTM_HEADS
    return pl.pallas_call(
        _mlstm_s_kernel,
        grid=(DEC_BATCH // bb,),
        in_specs=[
            pl.BlockSpec((R, MLSTM_WIDTH), lambda i: (i, 0)),
            pl.BlockSpec((R, MLSTM_WIDTH), lambda i: (i, 1)),
            pl.BlockSpec((R, MLSTM_WIDTH), lambda i: (i, 2)),
            pl.BlockSpec((R, MLSTM_WIDTH), lambda i: (i, 3)),
            pl.BlockSpec((R, LANES), lambda i: (i, 0)),
            pl.BlockSpec((R, MLSTM_WIDTH), lambda i: (i, 0)),
            pl.BlockSpec((R, LANES), lambda i: (i, 0)),
            pl.BlockSpec((1, LANES), lambda i: (0, 0)),
            pl.BlockSpec((1, MLSTM_WIDTH), lambda i: (0, 0)),
            pl.BlockSpec((bb, H, dh, dh), lambda i: (i, 0, 0, 0)),
        ],
        out_specs=[
            pl.BlockSpec((R, MLSTM_WIDTH), lambda i: (i, 0)),
            pl.BlockSpec((bb, H, dh, dh), lambda i: (i, 0, 0, 0)),
            pl.BlockSpec((R, MLSTM_WIDTH), lambda i: (i, 0)),
            pl.BlockSpec((R, LANES), lambda i: (i, 0)),
        ],
        out_shape=[
            jax.ShapeDtypeStruct((S_ROWS, MLSTM_WIDTH), BF16),
            jax.ShapeDtypeStruct((DEC_BATCH, H, dh, dh), F32),
            jax.ShapeDtypeStruct((S_ROWS, MLSTM_WIDTH), F32),
            jax.ShapeDtypeStruct((S_ROWS, LANES), F32),
        ],
        compiler_params=_cparams(("arbitrary",)),
        name="mlstm_s",
    )(m_all, m_all, m_all, m_all, gates, n_rep, m_rep, brow, nw, state_c)


def _outproj_kernel(att_ref, mh_ref, x_ref, g1_ref, sh_ref, sc_ref, nw_ref, wa_ref, wm_ref, x1_ref, h2_ref, *,
                    per_row):
    y = _dot(att_ref[...].astype(BF16), wa_ref[...]) + _dot(mh_ref[...], wm_ref[...])
    x1 = x_ref[...] + _mod_row(g1_ref, per_row) * y
    x1_ref[...] = x1
    r = lax.rsqrt(jnp.mean(x1 * x1, axis=-1, keepdims=True) + EPS)
    h2 = (x1 * r * nw_ref[...]) * (1.0 + _mod_row(sc_ref, per_row)) + _mod_row(sh_ref, per_row)
    h2_ref[...] = h2.astype(BF16)


def _outproj(att, mh, x, mod, mod_row_block, norm_w, w_att, w_m, *, tm, per_row):
    rows = x.shape[0]
    mod_rows = tm if per_row else 8
    mod_idx = (lambda i: i) if per_row else (lambda i: mod_row_block)
    kern = functools.partial(_outproj_kernel, per_row=per_row)
    return pl.pallas_call(
        kern,
        grid=(rows // tm,),
        in_specs=[
            pl.BlockSpec((tm, ATT_WIDTH), lambda i: (i, 0)),
            pl.BlockSpec((tm, MLSTM_WIDTH), lambda i: (i, 0)),
            pl.BlockSpec((tm, D_MODEL), lambda i: (i, 0)),
            pl.BlockSpec((mod_rows, D_MODEL), lambda i: (mod_idx(i), 2)),
            pl.BlockSpec((mod_rows, D_MODEL), lambda i: (mod_idx(i), 3)),
            pl.BlockSpec((mod_rows, D_MODEL), lambda i: (mod_idx(i), 4)),
            pl.BlockSpec((1, D_MODEL), lambda i: (0, 0)),
            pl.BlockSpec((ATT_WIDTH, D_MODEL), lambda i: (0, 0)),
            pl.BlockSpec((MLSTM_WIDTH, D_MODEL), lambda i: (0, 0)),
        ],
        out_specs=[
            pl.BlockSpec((tm, D_MODEL), lambda i: (i, 0)),
            pl.BlockSpec((tm, D_MODEL), lambda i: (i, 0)),
        ],
        out_shape=[
            jax.ShapeDtypeStruct((rows, D_MODEL), F32),
            jax.ShapeDtypeStruct((rows, D_MODEL), BF16),
        ],
        compiler_params=_cparams(("arbitrary",)),
        name="outproj_s" if per_row else "outproj_p",
    )(att, mh, x, mod, mod, mod, norm_w, w_att, w_m)


def _ffn_a_kernel(hp_ref, hs_ref, wg_ref, wu_ref, ap_ref, as_ref, wg_scr, wu_scr):
    i = pl.program_id(1)

    def swiglu(h_ref, a_ref):
        h = h_ref[...]
        g = _dot(h, wg_scr[...])
        u = _dot(h, wu_scr[...])
        a_ref[...] = (g * _sigmoid(g) * u).astype(BF16)

    @pl.when(i == 0)
    def _():
        wg_scr[...] = wg_ref[...].astype(BF16)
        wu_scr[...] = wu_ref[...].astype(BF16)
        swiglu(hs_ref, as_ref)

    @pl.when(i > 0)
    def _():
        swiglu(hp_ref, ap_ref)


def _ffn_a(h2_p, h2_s, w_gate, w_up, *, tm):
    rows_p, rows_s = h2_p.shape[0], h2_s.shape[0]
    tn = 512
    ni = rows_p // tm
    tile = lambda i: jnp.maximum(i - 1, 0)
    return pl.pallas_call(
        _ffn_a_kernel,
        grid=(D_FF // tn, ni + 1),
        in_specs=[
            pl.BlockSpec((tm, D_MODEL), lambda j, i: (tile(i), 0)),
            pl.BlockSpec((rows_s, D_MODEL), lambda j, i: (0, 0)),
            pl.BlockSpec((D_MODEL, tn), lambda j, i: (0, j)),
            pl.BlockSpec((D_MODEL, tn), lambda j, i: (0, j)),
        ],
        out_specs=[
            pl.BlockSpec((tm, tn), lambda j, i: (tile(i), j)),
            pl.BlockSpec((rows_s, tn), lambda j, i: (0, j)),
        ],
        out_shape=[
            jax.ShapeDtypeStruct((rows_p, D_FF), BF16),
            jax.ShapeDtypeStruct((rows_s, D_FF), BF16),
        ],
        scratch_shapes=[pltpu.VMEM((D_MODEL, tn), BF16), pltpu.VMEM((D_MODEL, tn), BF16)],
        compiler_params=_cparams(("arbitrary", "arbitrary")),
        name="ffn_a",
    )(h2_p, h2_s, w_gate, w_up)


def _ffn_b_kernel(a_ref, wd_ref, x1_ref, g2_ref, fw_ref, y_ref, *rest, per_row, emit_w):
    kk = pl.program_id(1)

    def weight_tile():
        wt = wd_ref[...].astype(BF16)
        if emit_w:
            rest[0][...] = wt
        return wt

    @pl.when(kk == 0)
    def _():
        y_ref[...] = _dot(a_ref[...], weight_tile())

    @pl.when(kk > 0)
    def _():
        y_ref[...] += _dot(a_ref[...], weight_tile())

    @pl.when(kk == pl.num_programs(1) - 1)
    def _():
        x2 = x1_ref[...] + _mod_row(g2_ref, per_row) * y_ref[...]
        y_ref[...] = x2 * lax.rsqrt(jnp.mean(x2 * x2, axis=-1, keepdims=True) + EPS) * fw_ref[...]


def _ffn_b(a, w_down, x1, mod, mod_row_block, final_w, *, tm, tk, per_row, emit_w):
    rows = a.shape[0]
    mod_rows = tm if per_row else 8
    mod_idx = (lambda i: i) if per_row else (lambda i: mod_row_block)
    kern = functools.partial(_ffn_b_kernel, per_row=per_row, emit_w=emit_w)
    w_spec = pl.BlockSpec((tk, D_MODEL), lambda i, k: (k, 0))
    y_spec = pl.BlockSpec((tm, D_MODEL), lambda i, k: (i, 0))
    y_shape = jax.ShapeDtypeStruct((rows, D_MODEL), F32)
    return pl.pallas_call(
        kern,
        grid=(rows // tm, D_FF // tk),
        in_specs=[
            pl.BlockSpec((tm, tk), lambda i, k: (i, k)),
            w_spec,
            pl.BlockSpec((tm, D_MODEL), lambda i, k: (i, 0)),
            pl.BlockSpec((mod_rows, D_MODEL), lambda i, k: (mod_idx(i), 5)),
            pl.BlockSpec((1, D_MODEL), lambda i, k: (0, 0)),
        ],
        out_specs=[y_spec, w_spec] if emit_w else y_spec,
        out_shape=[y_shape, jax.ShapeDtypeStruct((D_FF, D_MODEL), BF16)] if emit_w else y_shape,
        compiler_params=_cparams(("arbitrary", "arbitrary")),
        name="ffn_b_s" if per_row else "ffn_b_p",
    )(a, w_down, x1, mod, final_w)


def _rope_tables(pos):
    half = ROPE_DIM // 2
    inv = np.float32(ROPE_THETA) ** (-np.arange(0, ROPE_DIM, 2, dtype=np.float32) / np.float32(ROPE_DIM))
    d = np.arange(LANES) % ATT_HEAD_DIM
    ang = pos.astype(np.float32)[:, None] * inv[d % half][None, :].astype(np.float32)
    cos, sin = np.cos(ang), np.sin(ang)
    d = d[None, :]
    tables = (np.where(d < ROPE_DIM, cos, 1.0), np.where(d < half, -sin, 0.0),
              np.where((d >= half) & (d < ROPE_DIM), sin, 0.0))
    return tuple(jnp.asarray(t.astype(np.float32)) for t in tables)


def kernel(x_prompt, x_sample, cache_k_win, cache_v_win, state_C, state_n, state_m, c_prompt, c_sample,
           norm1_w, norm2_w, final_norm_w, w_ada, b_ada, w_in, b_ig, b_fg, attn_sinks, mh_norm_w,
           w_out, w_gate, w_up, w_down):
    assert w_in.shape[0] == 1, "single-layer trunk"
    T = DEC_SEQ
    xp = x_prompt[0]
    xs = x_sample.reshape(S_ROWS, D_MODEL)

    c_all = jnp.concatenate([c_sample, c_prompt, jnp.zeros((15, D_MODEL), F32)], axis=0)
    mod = _ada(c_all, w_ada[0], b_ada)
    prompt_mod_block = S_ROWS // 8

    w_in_t = jnp.transpose(w_in[0])
    wq_t = (w_in_t[:ATT_WIDTH].reshape(ATT_KV_HEADS, ATT_GROUP, ATT_HEAD_DIM, D_MODEL)
            .transpose(1, 0, 2, 3).reshape(ATT_WIDTH, D_MODEL).astype(BF16))
    w_gates_t = w_in_t[MAIN_WIDTH:]
    wg = jnp.pad(w_gates_t, ((0, 16 - 2 * MLSTM_HEADS), (0, 0)))
    w_out_att = (w_out[0, :ATT_WIDTH].reshape(ATT_KV_HEADS, ATT_GROUP, ATT_HEAD_DIM, D_MODEL)
                 .transpose(1, 0, 2, 3).reshape(ATT_WIDTH, D_MODEL).astype(BF16))
    w_out_m = w_out[0, ATT_WIDTH:].astype(BF16)
    n1 = norm1_w.reshape(1, D_MODEL)
    n2 = norm2_w.reshape(1, D_MODEL)
    fw = final_norm_w.reshape(1, D_MODEL)
    nw = mh_norm_w.reshape(1, MLSTM_WIDTH)
    gate_bias = jnp.concatenate([b_ig[0], b_fg[0]])
    brow = jnp.pad(gate_bias, (0, LANES - 2 * MLSTM_HEADS)).reshape(1, LANES)
    bcol = jnp.broadcast_to(jnp.pad(gate_bias, (0, 16 - 2 * MLSTM_HEADS))[:, None], (16, MLSTM_CHUNK_P))

    rope_p = _rope_tables(np.arange(SEQ))
    rope_s = _rope_tables(np.tile(PAST_LEN + np.arange(T), DEC_BATCH))

    tm_p = 1024
    q_s, _, kv32_s, m_s, g_s, _, w_in_bf = _inproj(xs, mod, 0, n1, wq_t, w_in_t, wg, *rope_s,
                                                   tm=S_ROWS, per_row=True, emit_w=True)
    q_p, kv_p, kv32_p, m_p, g_p, gt_p = _inproj(xp, mod, prompt_mod_block, n1, wq_t, w_in_bf, wg, *rope_p,
                                                tm=tm_p, per_row=False, emit_w=False)

    sinks = attn_sinks[0]
    sink_col = jnp.broadcast_to(sinks.reshape(ATT_HEADS, 1, 1), (ATT_HEADS, 8, LANES)).reshape(128, LANES)
    ck = jnp.transpose(cache_k_win[0], (0, 2, 3, 1))
    cv = jnp.transpose(cache_v_win[0], (0, 2, 3, 1))
    att_s, kwin_s, vwin_s = _attn_s(sink_col, q_s, kv32_s, ck, cv)

    n_rep = jnp.repeat(state_n[0].reshape(DEC_BATCH, MLSTM_WIDTH), T, axis=0)
    m_rep = jnp.pad(jnp.repeat(state_m[0], T, axis=0), ((0, 0), (0, LANES - MLSTM_HEADS)))
    mh_s, c_s, nrow_s, mrow_s = _mlstm_s(m_s, g_s, n_rep, m_rep, brow, nw, state_C[0])

    x1_p, h2_p, c_p, n_p, mm_p = _mix_p(sinks, q_p, kv_p, m_p, g_p, gt_p, brow, bcol, nw, xp, mod, prompt_mod_block,
                                        n2, w_out_att, w_out_m)
    x1_s, h2_s = _outproj(att_s, mh_s, xs, mod, 0, n2, w_out_att, w_out_m, tm=S_ROWS // 2, per_row=True)
    a_p, a_s = _ffn_a(h2_p, h2_s, w_gate[0], w_up[0], tm=tm_p)
    y_s, w_down_bf = _ffn_b(a_s, w_down[0], x1_s, mod, 0, fw, tm=S_ROWS, tk=1408, per_row=True, emit_w=True)
    y_p = _ffn_b(a_p, w_down_bf, x1_p, mod, prompt_mod_block, fw, tm=tm_p, tk=512, per_row=False, emit_w=False)

    kv_shape = (1, 1, WINDOW, ATT_KV_HEADS, ATT_HEAD_DIM)
    kv_last = kv32_p[tm_p - WINDOW:]
    dh = MLSTM_HEAD_DIM
    return (
        y_p.reshape(1, SEQ, D_MODEL),
        y_s.reshape(DEC_BATCH, T, D_MODEL),
        kv_last[:, :KV_WIDTH].reshape(kv_shape),
        kv_last[:, KV_WIDTH:].reshape(kv_shape),
        c_p.reshape(1, 1, MLSTM_HEADS, dh, dh),
        n_p[:MLSTM_HEADS].reshape(1, 1, MLSTM_HEADS, dh),
        mm_p[:MLSTM_HEADS, 0].reshape(1, 1, MLSTM_HEADS),
        jnp.transpose(kwin_s, (0, 3, 1, 2))[None],
        jnp.transpose(vwin_s, (0, 3, 1, 2))[None],
        c_s.reshape(1, DEC_BATCH, MLSTM_HEADS, dh, dh),
        nrow_s[T - 1::T].reshape(1, DEC_BATCH, MLSTM_HEADS, dh),
        mrow_s[T - 1::T, :MLSTM_HEADS].reshape(1, DEC_BATCH, MLSTM_HEADS),
    )
```

```python
import functools

import jax
import jax.numpy as jnp
import numpy as np
from jax import lax
from jax.experimental import pallas as pl
from jax.experimental.pallas import tpu as pltpu

F32 = jnp.float32
BF16 = jnp.bfloat16

D_MODEL = 2048
SEQ = 8192
DEC_BATCH = 128
DEC_SEQ = 4
S_ROWS = DEC_BATCH * DEC_SEQ
PAST_LEN = 16384
ATT_HEADS = 16
ATT_KV_HEADS = 4
ATT_GROUP = 4
ATT_HEAD_DIM = 64
WINDOW = 128
ROPE_THETA = 500000.0
ROPE_DIM = 16
MLSTM_HEADS = 4
MLSTM_HEAD_DIM = 256
ATT_WIDTH = 1024
KV_WIDTH = 256
MLSTM_WIDTH = 1024
MAIN_WIDTH = ATT_WIDTH + 2 * KV_WIDTH + 4 * MLSTM_WIDTH
D_FF = 5632
N_MOD = 6
EPS = 1e-6

LANES = 128
MLSTM_CHUNK_P = 256
VMEM_LIMIT = 56 * 1024 * 1024

NT_DIMS = (((1,), (1,)), ((), ()))
TN_DIMS = (((0,), (0,)), ((), ()))
HI = lax.Precision.HIGHEST


def _cparams(sem):
    return pltpu.CompilerParams(dimension_semantics=sem, vmem_limit_bytes=VMEM_LIMIT)


def _dot(a, b):
    return jnp.dot(a, b, preferred_element_type=F32)


def _dot_nt(a, b):
    return lax.dot_general(a, b, NT_DIMS, preferred_element_type=F32)


def _dot_tn(a, b):
    return lax.dot_general(a, b, TN_DIMS, preferred_element_type=F32)


def _sigmoid(x):
    return 1.0 / (1.0 + jnp.exp(-x))


def _log_sigmoid(x):
    return jnp.minimum(x, 0.0) - jnp.log(1.0 + jnp.exp(-jnp.abs(x)))


def _mod_row(ref, per_row):
    return ref[...] if per_row else ref[0:1, :]


def _ada_kernel(c_ref, w_ref, b_ref, o_ref, s_scr, rep_scr):
    @pl.when(pl.program_id(0) == 0)
    def _():
        c = c_ref[...]
        s_scr[...] = (c * _sigmoid(c)).astype(BF16)

    mod = _dot(s_scr[...], w_ref[...].astype(BF16)) + b_ref[...]
    for c in range(o_ref.shape[1] // LANES):
        cols = slice(LANES * c, LANES * (c + 1))
        for t in range(DEC_SEQ):
            rep_scr[pl.ds(t, DEC_BATCH, stride=DEC_SEQ), :] = mod[0:DEC_BATCH, cols]
        o_ref[0:S_ROWS, cols] = rep_scr[...]
    o_ref[S_ROWS:, :] = mod[DEC_BATCH:]


def _ada(c_all, w_ada, b_ada):
    m = c_all.shape[0]
    m_out = S_ROWS + m - DEC_BATCH
    n = w_ada.shape[1]
    tn = 1024
    return pl.pallas_call(
        _ada_kernel,
        grid=(n // tn,),
        in_specs=[
            pl.BlockSpec((m, D_MODEL), lambda j: (0, 0)),
            pl.BlockSpec((D_MODEL, tn), lambda j: (0, j)),
            pl.BlockSpec((1, tn), lambda j: (0, j)),
        ],
        out_specs=pl.BlockSpec((m_out, tn), lambda j: (0, j)),
        out_shape=jax.ShapeDtypeStruct((m_out, n), F32),
        scratch_shapes=[pltpu.VMEM((m, D_MODEL), BF16), pltpu.VMEM((S_ROWS, LANES), F32)],
        compiler_params=_cparams(("arbitrary",)),
        name="ada",
    )(c_all, w_ada, b_ada)


def _rope_store(acc, cos, sa, sb, out_ref, ncols, scale):
    for c in range(ncols // LANES):
        xc = acc[:, LANES * c:LANES * (c + 1)]
        rot = xc * cos + pltpu.roll(xc, LANES - 8, 1) * sa + pltpu.roll(xc, 8, 1) * sb
        if scale != 1.0:
            rot = rot * scale
        out_ref[:, LANES * c:LANES * (c + 1)] = rot.astype(out_ref.dtype)


def _inproj_kernel(x_ref, sh_ref, sc_ref, nw_ref, wq_ref, win_ref, wg_ref, cos_ref, sa_ref, sb_ref,
                   q_ref, kv_ref, kv32_ref, m_ref, g_ref, gt_ref, *rest, per_row, emit_w):
    h_scr = rest[-1]
    j = pl.program_id(1)

    def weight_tile():
        wt = win_ref[...].astype(BF16)
        if emit_w:
            rest[0][...] = wt
        return wt

    @pl.when(j == 0)
    def _():
        x = x_ref[...]
        r = lax.rsqrt(jnp.mean(x * x, axis=-1, keepdims=True) + EPS)
        gain = nw_ref[...] * (1.0 + _mod_row(sc_ref, per_row))
        h = (x * r * gain + _mod_row(sh_ref, per_row)).astype(BF16)
        h_scr[...] = h
        wg = jnp.concatenate([wg_ref[...].astype(BF16), jnp.zeros((LANES - 16, D_MODEL), BF16)], axis=0)
        g = _dot_nt(h, wg)
        g_ref[...] = g
        gt_ref[...] = g.T[0:16, :]

    @pl.when(j < 2)
    def _():
        acc = _dot_nt(h_scr[...], wq_ref[...])
        _rope_store(acc, cos_ref[...], sa_ref[...], sb_ref[...], q_ref, 512, ATT_HEAD_DIM ** -0.5)

    @pl.when(j == 2)
    def _():
        acc = _dot_nt(h_scr[...], weight_tile())
        _rope_store(acc, cos_ref[...], sa_ref[...], sb_ref[...], kv32_ref, KV_WIDTH, 1.0)
        kv32_ref[:, KV_WIDTH:] = acc[:, KV_WIDTH:]
        kv_ref[...] = kv32_ref[...].astype(BF16)

    @pl.when(j > 2)
    def _():
        m_ref[...] = _dot_nt(h_scr[...], weight_tile()).astype(BF16)


def _inproj(x, mod, mod_row_block, norm_w, wq_t, w_in_t, wg, cos, sa, sb, *, tm, per_row, emit_w):
    rows = x.shape[0]
    tn = 512
    nj = MAIN_WIDTH // tn
    mod_rows = tm if per_row else 8
    mod_idx = (lambda i: i) if per_row else (lambda i: mod_row_block)
    kern = functools.partial(_inproj_kernel, per_row=per_row, emit_w=emit_w)
    copy_spec = pl.BlockSpec((tn, D_MODEL), lambda i, j: (jnp.maximum(j, 2) - 2, 0))
    w_spec = pl.BlockSpec((tn, D_MODEL), lambda i, j: (jnp.maximum(j, 2), 0)) if emit_w else copy_spec
    extra_specs = [copy_spec] if emit_w else []
    extra_shapes = [jax.ShapeDtypeStruct((MAIN_WIDTH - ATT_WIDTH, D_MODEL), BF16)] if emit_w else []
    return pl.pallas_call(
        kern,
        grid=(rows // tm, nj),
        in_specs=[
            pl.BlockSpec((tm, D_MODEL), lambda i, j: (jnp.minimum(i + jnp.minimum(j, 1), rows // tm - 1), 0)),
            pl.BlockSpec((mod_rows, D_MODEL), lambda i, j: (mod_idx(i), 0)),
            pl.BlockSpec((mod_rows, D_MODEL), lambda i, j: (mod_idx(i), 1)),
            pl.BlockSpec((1, D_MODEL), lambda i, j: (0, 0)),
            pl.BlockSpec((tn, D_MODEL), lambda i, j: (jnp.minimum(j, 1), 0)),
            w_spec,
            pl.BlockSpec((16, D_MODEL), lambda i, j: (0, 0)),
            pl.BlockSpec((tm, LANES), lambda i, j: (i, 0)),
            pl.BlockSpec((tm, LANES), lambda i, j: (i, 0)),
            pl.BlockSpec((tm, LANES), lambda i, j: (i, 0)),
        ],
        out_specs=[
            pl.BlockSpec((tm, tn), lambda i, j: (i, jnp.minimum(j, 1))),
            pl.BlockSpec((tm, tn), lambda i, j: (i, 0)),
            pl.BlockSpec((tm, tn), lambda i, j: (0, 0)),
            pl.BlockSpec((tm, tn), lambda i, j: (i, jnp.clip(j - 3, 0, 7))),
            pl.BlockSpec((tm, LANES), lambda i, j: (i, 0)),
            pl.BlockSpec((16, tm), lambda i, j: (0, i)),
        ] + extra_specs,
        out_shape=[
            jax.ShapeDtypeStruct((rows, ATT_WIDTH), BF16),
            jax.ShapeDtypeStruct((rows, 2 * KV_WIDTH), BF16),
            jax.ShapeDtypeStruct((tm, 2 * KV_WIDTH), F32),
            jax.ShapeDtypeStruct((rows, 4 * MLSTM_WIDTH), BF16),
            jax.ShapeDtypeStruct((rows, LANES), F32),
            jax.ShapeDtypeStruct((16, rows), F32),
        ] + extra_shapes,
        scratch_shapes=[pltpu.VMEM((tm, D_MODEL), BF16)],
        compiler_params=_cparams(("arbitrary", "arbitrary")),
        name="inproj_s" if per_row else "inproj_p",
    )(x, mod, mod, norm_w, wq_t, w_in_t, wg, cos, sa, sb)


def _attn_block(sink_ref, q_ref, row0, kv2, allowed, store):
    w = WINDOW
    grp = ATT_GROUP
    rows = grp * w
    member = lax.broadcasted_iota(jnp.int32, (rows, 1), 0) // w
    low = lax.broadcasted_iota(jnp.int32, (2 * w, LANES), 1) < ATT_HEAD_DIM
    low_o = lax.broadcasted_iota(jnp.int32, (rows, LANES), 1) < ATT_HEAD_DIM
    key_row = lax.broadcasted_iota(jnp.int32, (4 * w, LANES), 0)
    key_lane = lax.broadcasted_iota(jnp.int32, (4 * w, LANES), 1)
    ones_bd = (((key_row < 2 * w) & (key_lane < ATT_HEAD_DIM))
               | ((key_row >= 2 * w) & (key_lane >= ATT_HEAD_DIM))).astype(BF16)
    zero = jnp.zeros((2 * w, LANES), BF16)
    for cp in range(2):
        k128 = kv2[:, LANES * cp:LANES * (cp + 1)]
        v128 = kv2[:, KV_WIDTH + LANES * cp:KV_WIDTH + LANES * (cp + 1)]
        kbd = jnp.concatenate([jnp.where(low, k128, zero), jnp.where(low, zero, k128)], axis=0)
        vbd = jnp.concatenate([jnp.where(low, v128, zero), jnp.where(low, zero, v128)], axis=0)
        v_aug = jnp.concatenate([vbd, ones_bd], axis=1)
        q4 = jnp.concatenate([q_ref[row0:row0 + w, 256 * r + LANES * cp:256 * r + LANES * (cp + 1)]
                              for r in range(grp)], axis=0)
        s = _dot_nt(q4, kbd)
        es, tails = [], []
        for half in range(2):
            sh = jnp.where(allowed, s[:, 2 * w * half:2 * w * (half + 1)], -jnp.inf)
            head0 = (2 * cp + half) * grp
            sink = jnp.full((rows, 1), sink_ref[head0], F32)
            for r in range(1, grp):
                sink = jnp.where(member == r, sink_ref[head0 + r], sink)
            m = jnp.maximum(jnp.max(sh, axis=-1, keepdims=True), sink)
            es.append(jnp.exp(sh - m).astype(BF16))
            tails.append(jnp.exp(sink - m))
        oa = _dot(jnp.concatenate(es, axis=1), v_aug)
        l = oa[:, LANES:] + jnp.where(low_o, tails[0], tails[1])
        o = (oa[:, :LANES] / l).astype(BF16)
        for r in range(grp):
            store(256 * r + LANES * cp, o[w * r:w * (r + 1)])


MIX_TM = 512


def _mix_p_kernel(sink_ref, q_ref, kvp_ref, kvc_ref, mq_ref, mk_ref, mv_ref, mo_ref, g_ref, gt_ref, brow_ref, bcol_ref,
                  mnw_ref, x_ref, g1_ref, sh_ref, sc_ref, nw_ref, wa_ref, wm_ref,
                  x1_ref, h2_ref, c_out, n_out, m_out, att_scr, mh_scr, c_scr, n_scr, m_scr):
    s = pl.program_id(0)
    w = WINDOW
    tiles = pl.num_programs(0) - 1
    slot = s % 2

    @pl.when(s == 0)
    def _():
        c_scr[...] = jnp.zeros_like(c_scr)
        n_scr[...] = jnp.zeros_like(n_scr)
        m_scr[...] = jnp.zeros_like(m_scr)

    def project():
        y = _dot(att_scr[1 - slot], wa_ref[...]) + _dot(mh_scr[1 - slot], wm_ref[...])
        x1 = x_ref[...] + g1_ref[0:1, :] * y
        x1_ref[...] = x1
        r = lax.rsqrt(jnp.mean(x1 * x1, axis=-1, keepdims=True) + EPS)
        gain = nw_ref[...] * (1.0 + sc_ref[0:1, :])
        h2_ref[...] = (x1 * r * gain + sh_ref[0:1, :]).astype(BF16)

    def mixers():
        rows = ATT_GROUP * w
        qi = lax.broadcasted_iota(jnp.int32, (rows, 2 * w), 0) % w
        kj = lax.broadcasted_iota(jnp.int32, (rows, 2 * w), 1)
        first_off = jnp.where(s > 0, 0, 4 * w)
        causal = (kj >= w) & (kj - w <= qi)
        for blk in range(MIX_TM // w):
            prev = kvp_ref[...] if blk == 0 else kvc_ref[w * (blk - 1):w * blk, :]
            kv2 = jnp.concatenate([prev, kvc_ref[w * blk:w * (blk + 1), :]], axis=0)
            allowed = ((kj < w) & (kj > qi + (first_off if blk == 0 else 0))) | causal

            def store(c0, val, blk=blk):
                att_scr[slot, w * blk:w * (blk + 1), c0:c0 + LANES] = val

            _attn_block(sink_ref, q_ref, w * blk, kv2, allowed, store)

        for ch in range(MIX_TM // MLSTM_CHUNK_P):
            r0 = MLSTM_CHUNK_P * ch

            def store_mh(cs, val, r0=r0):
                mh_scr[slot, r0:r0 + MLSTM_CHUNK_P, cs] = val

            _mlstm_chunk(mq_ref, mk_ref, mv_ref, mo_ref, g_ref, gt_ref, brow_ref, bcol_ref, mnw_ref,
                         c_scr, n_scr, m_scr, r0, store_mh)

    @pl.when(s == 0)
    def _():
        mixers()

    @pl.when((s > 0) & (s < tiles))
    def _():
        project()
        mixers()

    @pl.when(s == tiles)
    def _():
        project()
        c_out[...] = c_scr[...]
        n_out[...] = n_scr[...]
        m_out[...] = m_scr[...]


def _mix_p(sinks, q, kv, m_all, gates, gates_t, brow, bcol, mnw, x, mod, mod_row_block, norm_w, w_att, w_m):
    tm = MIX_TM
    tiles = SEQ // tm
    bpt = tm // WINDOW
    dh = MLSTM_HEAD_DIM
    att_tile = lambda s: jnp.minimum(s, tiles - 1)
    out_tile = lambda s: jnp.maximum(s - 1, 0)
    m_spec = lambda col: pl.BlockSpec((tm, MLSTM_WIDTH), lambda s: (att_tile(s), col))
    return pl.pallas_call(
        _mix_p_kernel,
        grid=(tiles + 1,),
        in_specs=[
            pl.BlockSpec(memory_space=pltpu.SMEM),
            pl.BlockSpec((tm, ATT_WIDTH), lambda s: (att_tile(s), 0)),
            pl.BlockSpec((WINDOW, 2 * KV_WIDTH), lambda s: (jnp.maximum(bpt * att_tile(s) - 1, 0), 0)),
            pl.BlockSpec((tm, 2 * KV_WIDTH), lambda s: (att_tile(s), 0)),
            m_spec(0), m_spec(1), m_spec(2), m_spec(3),
            pl.BlockSpec((tm, LANES), lambda s: (att_tile(s), 0)),
            pl.BlockSpec((16, tm), lambda s: (0, att_tile(s))),
            pl.BlockSpec((1, LANES), lambda s: (0, 0)),
            pl.BlockSpec((16, MLSTM_CHUNK_P), lambda s: (0, 0)),
            pl.BlockSpec((1, MLSTM_WIDTH), lambda s: (0, 0)),
            pl.BlockSpec((tm, D_MODEL), lambda s: (out_tile(s), 0)),
            pl.BlockSpec((8, D_MODEL), lambda s: (mod_row_block, 2)),
            pl.BlockSpec((8, D_MODEL), lambda s: (mod_row_block, 3)),
            pl.BlockSpec((8, D_MODEL), lambda s: (mod_row_block, 4)),
            pl.BlockSpec((1, D_MODEL), lambda s: (0, 0)),
            pl.BlockSpec((ATT_WIDTH, D_MODEL), lambda s: (0, 0)),
            pl.BlockSpec((MLSTM_WIDTH, D_MODEL), lambda s: (0, 0)),
        ],
        out_specs=[
            pl.BlockSpec((tm, D_MODEL), lambda s: (out_tile(s), 0)),
            pl.BlockSpec((tm, D_MODEL), lambda s: (out_tile(s), 0)),
            pl.BlockSpec((MLSTM_HEADS, dh, dh), lambda s: (0, 0, 0)),
            pl.BlockSpec((8, dh), lambda s: (0, 0)),
            pl.BlockSpec((8, LANES), lambda s: (0, 0)),
        ],
        out_shape=[
            jax.ShapeDtypeStruct((SEQ, D_MODEL), F32),
            jax.ShapeDtypeStruct((SEQ, D_MODEL), BF16),
            jax.ShapeDtypeStruct((MLSTM_HEADS, dh, dh), F32),
            jax.ShapeDtypeStruct((8, dh), F32),
            jax.ShapeDtypeStruct((8, LANES), F32),
        ],
        scratch_shapes=[
            pltpu.VMEM((2, tm, ATT_WIDTH), BF16),
            pltpu.VMEM((2, tm, MLSTM_WIDTH), BF16),
            pltpu.VMEM((MLSTM_HEADS, dh, dh), F32),
            pltpu.VMEM((8, dh), F32),
            pltpu.VMEM((8, LANES), F32),
        ],
        compiler_params=_cparams(("arbitrary",)),
        name="mix_p",
    )(sinks, q, kv, kv, m_all, m_all, m_all, m_all, gates, gates_t, brow, bcol, mnw, x, mod, mod, mod, norm_w,
      w_att, w_m)


ATT_S_BB = 16


def _attn_s_kernel(sink_ref, q_ref, kv32_ref, ck_ref, cv_ref, o_ref, ko_ref, vo_ref, q32_scr):
    t_new = DEC_SEQ
    w = WINDOW
    q32_scr[...] = q_ref[...].astype(F32)
    rows = 4 * 4 * 8
    row = lax.broadcasted_iota(jnp.int32, (rows, w), 0)
    slot = lax.broadcasted_iota(jnp.int32, (rows, w), 1)
    t_row = row % t_new
    second = (row % 8) >= t_new
    win_ok = (slot < w - t_new) | (slot - (w - t_new) <= t_row)
    old_ok = (slot >= 1) & (slot < t_new) & (slot > t_row)
    lane256 = lax.broadcasted_iota(jnp.int32, (32, 2 * LANES), 1)
    sink = sink_ref[...][:, 0:1]
    kv_new = jnp.concatenate([kv32_ref[...], jnp.zeros((w - ATT_S_BB * t_new, 2 * KV_WIDTH), F32)], axis=0)
    kv_t = kv_new.T
    new_slot = lax.broadcasted_iota(jnp.int32, (KV_WIDTH, w), 1) >= w - t_new
    for b in range(ATT_S_BB):
        cols = pltpu.roll(kv_t, w - t_new - t_new * b, 1)
        k_shift = pltpu.roll(ck_ref[b].reshape(KV_WIDTH, w), w - t_new, 1)
        v_shift = pltpu.roll(cv_ref[b].reshape(KV_WIDTH, w), w - t_new, 1)
        ko_ref[b] = jnp.where(new_slot, cols[:KV_WIDTH], k_shift).reshape(ATT_KV_HEADS, ATT_HEAD_DIM, w)
        vo_ref[b] = jnp.where(new_slot, cols[KV_WIDTH:], v_shift).reshape(ATT_KV_HEADS, ATT_HEAD_DIM, w)
    for pair in range(ATT_S_BB // 2):
        b0, b1 = 2 * pair, 2 * pair + 1
        q32 = jnp.concatenate([q32_scr[8 * pair:8 * (pair + 1), 256 * r:256 * (r + 1)] for r in range(ATT_GROUP)],
                              axis=0)
        qpad = jnp.concatenate(
            [jnp.where((lane256 // ATT_HEAD_DIM) == g, q32, 0.0) for g in range(ATT_KV_HEADS)], axis=0).astype(BF16)
        kw = [ko_ref[b].reshape(KV_WIDTH, w).astype(BF16) for b in (b0, b1)]
        vw = [vo_ref[b].reshape(KV_WIDTH, w).astype(BF16) for b in (b0, b1)]
        kc = [ck_ref[b].reshape(KV_WIDTH, w).astype(BF16) for b in (b0, b1)]
        vc = [cv_ref[b].reshape(KV_WIDTH, w).astype(BF16) for b in (b0, b1)]
        s_w = jnp.where(second, _dot(qpad, kw[1]), _dot(qpad, kw[0]))
        s_c = jnp.where(second, _dot(qpad, kc[1]), _dot(qpad, kc[0]))
        s_w = jnp.where(win_ok, s_w, -jnp.inf)
        s_c = jnp.where(old_ok, s_c, -jnp.inf)
        m = jnp.maximum(jnp.maximum(jnp.max(s_w, axis=-1, keepdims=True), jnp.max(s_c, axis=-1, keepdims=True)), sink)
        e_w = jnp.exp(s_w - m)
        e_c = jnp.exp(s_c - m)
        l = jnp.sum(e_w, axis=-1, keepdims=True) + jnp.sum(e_c, axis=-1, keepdims=True) + jnp.exp(sink - m)
        p_w = e_w / l
        p_c = e_c / l
        zero = jnp.zeros_like(p_w)
        o = (_dot_nt(jnp.where(second, zero, p_w).astype(BF16), vw[0])
             + _dot_nt(jnp.where(second, p_w, zero).astype(BF16), vw[1])
             + _dot_nt(jnp.where(second, zero, p_c).astype(BF16), vc[0])
             + _dot_nt(jnp.where(second, p_c, zero).astype(BF16), vc[1]))
        o32 = jnp.zeros((32, 2 * LANES), F32)
        for g in range(ATT_KV_HEADS):
            o32 = jnp.where((lane256 // ATT_HEAD_DIM) == g, o[32 * g:32 * (g + 1), :], o32)
        for r in range(ATT_GROUP):
            o_ref[8 * pair:8 * (pair + 1), 256 * r:256 * (r + 1)] = o32[8 * r:8 * (r + 1), :]


def _attn_s(sink_col, q, kv32, ck, cv):
    bb = ATT_S_BB
    rows = bb * DEC_SEQ
    cache_block = (bb, ATT_KV_HEADS, ATT_HEAD_DIM, WINDOW)
    cache_shape = (DEC_BATCH, ATT_KV_HEADS, ATT_HEAD_DIM, WINDOW)
    return pl.pallas_call(
        _attn_s_kernel,
        grid=(DEC_BATCH // bb,),
        in_specs=[
            pl.BlockSpec((128, LANES), lambda i: (0, 0)),
            pl.BlockSpec((rows, ATT_WIDTH), lambda i: (i, 0)),
            pl.BlockSpec((rows, 2 * KV_WIDTH), lambda i: (i, 0)),
            pl.BlockSpec(cache_block, lambda i: (i, 0, 0, 0)),
            pl.BlockSpec(cache_block, lambda i: (i, 0, 0, 0)),
        ],
        out_specs=[
            pl.BlockSpec((rows, ATT_WIDTH), lambda i: (i, 0)),
            pl.BlockSpec(cache_block, lambda i: (i, 0, 0, 0)),
            pl.BlockSpec(cache_block, lambda i: (i, 0, 0, 0)),
        ],
        out_shape=[
            jax.ShapeDtypeStruct((S_ROWS, ATT_WIDTH), F32),
            jax.ShapeDtypeStruct(cache_shape, F32),
            jax.ShapeDtypeStruct(cache_shape, F32),
        ],
        scratch_shapes=[pltpu.VMEM((rows, ATT_WIDTH), F32)],
        compiler_params=_cparams(("arbitrary",)),
        name="attn_s",
    )(sink_col, q, kv32, ck, cv)


def _head_norm_gate(h, nw, mo):
    hn = h * lax.rsqrt(jnp.mean(h * h, axis=-1, keepdims=True) + EPS) * nw
    return hn * _sigmoid(mo.astype(F32))


def _mlstm_chunk(q_ref, k_ref, v_ref, mo_ref, g_ref, gt_ref, brow_ref, bcol_ref, nw_ref, c_scr, n_scr, m_scr,
                 r0, store):
    L = MLSTM_CHUNK_P
    dh = MLSTM_HEAD_DIM
    rs = slice(r0, r0 + L)
    ti = lax.broadcasted_iota(jnp.int32, (L, L), 0)
    si = lax.broadcasted_iota(jnp.int32, (L, L), 1)
    causal = si <= ti
    tri = causal.astype(F32)
    tri_t = (ti <= si).astype(F32)
    gates = g_ref[rs, :] + brow_ref[...]
    gates_t = gt_ref[:, rs] + bcol_ref[...]
    b_col = jnp.dot(tri, _log_sigmoid(gates), precision=HI, preferred_element_type=F32)
    b_row = jnp.dot(_log_sigmoid(gates_t), tri_t, precision=HI, preferred_element_type=F32)
    for hd in range(MLSTM_HEADS):
        cs = slice(dh * hd, dh * (hd + 1))
        b_c = b_col[:, MLSTM_HEADS + hd:MLSTM_HEADS + hd + 1]
        li_c = gates[:, hd:hd + 1]
        b_r = b_row[MLSTM_HEADS + hd:MLSTM_HEADS + hd + 1, :]
        li_r = gates_t[hd:hd + 1, :]
        dm = jnp.where(causal, b_c - b_r + li_r, -jnp.inf)
        m_prev = m_scr[hd:hd + 1, 0:1]
        m_inter = b_c + m_prev
        m_t = jnp.maximum(m_inter, jnp.max(dm, axis=-1, keepdims=True))
        q = q_ref[rs, cs] * (dh ** -0.5)
        k = k_ref[rs, cs]
        v = v_ref[rs, cs]
        sm = _dot_nt(q, k) * jnp.exp(dm - m_t)
        a = jnp.exp(m_inter - m_t)
        c_old = c_scr[hd]
        n_old = n_scr[hd:hd + 1, :]
        num = a * _dot_nt(q, c_old.astype(BF16)) + _dot(sm.astype(BF16), v)
        qn = jnp.sum(q.astype(F32) * n_old, axis=-1, keepdims=True)
        den = a * qn + jnp.sum(sm, axis=-1, keepdims=True)
        h = num / jnp.maximum(jnp.abs(den), jnp.exp(-m_t))
        store(cs, _head_norm_gate(h, nw_ref[:, cs], mo_ref[rs, cs]).astype(BF16))
        m_new = m_t[L - 1:L, :]
        b_last = b_c[L - 1:L, :]
        g = jnp.exp(b_last - b_c + li_c - m_new)
        decay = jnp.exp(b_last + m_prev - m_new)
        gv = (g * v.astype(F32)).astype(BF16)
        c_scr[hd] = decay * c_old + _dot_tn(gv, k)
        n_scr[hd:hd + 1, :] = decay * n_old + jnp.sum(g * k.astype(F32), axis=0, keepdims=True)
        m_scr[hd:hd + 1, :] = jnp.broadcast_to(m_new, (1, LANES))


MLSTM_S_BB = 8


def _mlstm_s_kernel(q_ref, k_ref, v_ref, mo_ref, g_ref, nrep_ref, mrep_ref, brow_ref, nw_ref, c_ref,
                    mh_ref, c_out, nrow_ref, mrow_ref):
    T = DEC_SEQ
    R = MLSTM_S_BB * T
    H = MLSTM_HEADS
    dh = MLSTM_HEAD_DIM

    def shift(x, d):
        return pltpu.roll(x, d, 0)

    lanes = lax.broadcasted_iota(jnp.int32, (R, LANES), 1)
    tmod = lax.broadcasted_iota(jnp.int32, (R, LANES), 0) % T
    tmod_w = lax.broadcasted_iota(jnp.int32, (R, MLSTM_WIDTH), 0) % T
    head_ok = lanes < H
    gates = g_ref[...] + brow_ref[...]
    li = jnp.where(head_ok, gates, 0.0)
    lf = jnp.where(head_ok, pltpu.roll(_log_sigmoid(gates), LANES - H, 1), 0.0)
    bcum = lf
    for d in range(1, T):
        bcum = bcum + jnp.where(tmod >= d, shift(lf, d), 0.0)
    m0 = mrep_ref[...]
    m_inter = bcum + m0
    dms = [li] + [jnp.where(tmod >= d, bcum - shift(bcum, d) + shift(li, d), -jnp.inf) for d in range(1, T)]
    m_t = m_inter
    for dm in dms:
        m_t = jnp.maximum(m_t, dm)
    a = jnp.exp(m_inter - m_t)
    ws = [jnp.exp(dm - m_t) for dm in dms]

    q_bf = q_ref[...] * (dh ** -0.5)
    q = q_bf.astype(F32)
    k = k_ref[...].astype(F32)
    v = v_ref[...].astype(F32)
    seg = (lax.broadcasted_iota(jnp.int32, (MLSTM_WIDTH, LANES), 0) // dh
           == lax.broadcasted_iota(jnp.int32, (MLSTM_WIDTH, LANES), 1)).astype(F32)
    ex = (lax.broadcasted_iota(jnp.int32, (LANES, MLSTM_WIDTH), 1) // dh
          == lax.broadcasted_iota(jnp.int32, (LANES, MLSTM_WIDTH), 0)).astype(F32)

    def segsum(x):
        return jnp.dot(x, seg, precision=HI, preferred_element_type=F32)

    def expand(x):
        return jnp.dot(x, ex, precision=HI, preferred_element_type=F32)

    ks = [k] + [shift(k, d) for d in range(1, T)]
    vs = [v] + [shift(v, d) for d in range(1, T)]
    sms = [segsum(q * ks[d]) * ws[d] for d in range(T)]
    den = a * segsum(q * nrep_ref[...])
    for sm in sms:
        den = den + sm
    inv = 1.0 / jnp.maximum(jnp.abs(den), jnp.exp(-m_t))

    def last(x):
        out = jnp.zeros_like(x)
        for jj in range(T):
            out = jnp.where(tmod == T - 1 - jj, x if jj == 0 else pltpu.roll(x, R - jj, 0), out)
        return out

    m_new = last(m_t)
    b_last = last(bcum)
    g = jnp.where(head_ok, jnp.exp(b_last - bcum + li - m_new), 0.0)
    decay = jnp.where(head_ok, jnp.exp(b_last + m0 - m_new), 0.0)
    a_f = expand(a * inv)
    w_f = [expand(sm * inv) for sm in sms]
    g_f = expand(g)
    d_f = expand(decay)
    gv = (g_f * v).astype(BF16)
    rowb = lax.broadcasted_iota(jnp.int32, (R, dh), 0) // T
    for hd in range(H):
        cs = slice(dh * hd, dh * (hd + 1))
        qh = q_bf[:, cs]
        kh = k_ref[:, cs]
        gvh = gv[:, cs]
        qc = jnp.zeros((R, dh), F32)
        for bb in range(MLSTM_S_BB):
            c_old = c_ref[bb, hd]
            qc = jnp.where(rowb == bb, _dot_nt(qh, c_old.astype(BF16)), qc)
            upd = _dot_tn(jnp.where(rowb == bb, gvh, jnp.zeros_like(gvh)), kh)
            c_out[bb, hd] = d_f[T * bb:T * bb + 1, cs] * c_old + upd
        h = a_f[:, cs] * qc
        for d in range(T):
            h = h + w_f[d][:, cs] * vs[d][:, cs]
        mh_ref[:, cs] = _head_norm_gate(h, nw_ref[:, cs], mo_ref[:, cs]).astype(BF16)
    gk = g_f * k
    nsum = gk
    for d in range(1, T):
        nsum = nsum + jnp.where(tmod_w >= d, shift(gk, d), 0.0)
    nrow_ref[...] = d_f * nrep_ref[...] + nsum
    mrow_ref[...] = m_t


def _mlstm_s(m_all, gates, n_rep, m_rep, brow, nw, state_c):
    bb = MLSTM_S_BB
    R = bb * DEC_SEQ
    dh = MLSTM_HEAD_DIM
    H = MLSTM_HEADS
    return pl.pallas_call(
        _mlstm_s_kernel,
        grid=(DEC_BATCH // bb,),
        in_specs=[
            pl.BlockSpec((R, MLSTM_WIDTH), lambda i: (i, 0)),
            pl.BlockSpec((R, MLSTM_WIDTH), lambda i: (i, 1)),
            pl.BlockSpec((R, MLSTM_WIDTH), lambda i: (i, 2)),
            pl.BlockSpec((R, MLSTM_WIDTH), lambda i: (i, 3)),
            pl.BlockSpec((R, LANES), lambda i: (i, 0)),
            pl.BlockSpec((R, MLSTM_WIDTH), lambda i: (i, 0)),
            pl.BlockSpec((R, LANES), lambda i: (i, 0)),
            pl.BlockSpec((1, LANES), lambda i: (0, 0)),
            pl.BlockSpec((1, MLSTM_WIDTH), lambda i: (0, 0)),
            pl.BlockSpec((bb, H, dh, dh), lambda i: (i, 0, 0, 0)),
        ],
        out_specs=[
            pl.BlockSpec((R, MLSTM_WIDTH), lambda i: (i, 0)),
            pl.BlockSpec((bb, H, dh, dh), lambda i: (i, 0, 0, 0)),
            pl.BlockSpec((R, MLSTM_WIDTH), lambda i: (i, 0)),
            pl.BlockSpec((R, LANES), lambda i: (i, 0)),
        ],
        out_shape=[
            jax.ShapeDtypeStruct((S_ROWS, MLSTM_WIDTH), BF16),
            jax.ShapeDtypeStruct((DEC_BATCH, H, dh, dh), F32),
            jax.ShapeDtypeStruct((S_ROWS, MLSTM_WIDTH), F32),
            jax.ShapeDtypeStruct((S_ROWS, LANES), F32),
        ],
        compiler_params=_cparams(("arbitrary",)),
        name="mlstm_s",
    )(m_all, m_all, m_all, m_all, gates, n_rep, m_rep, brow, nw, state_c)


def _outproj_kernel(att_ref, mh_ref, x_ref, g1_ref, sh_ref, sc_ref, nw_ref, wa_ref, wm_ref, x1_ref, h2_ref, *,
                    per_row):
    y = _dot(att_ref[...].astype(BF16), wa_ref[...]) + _dot(mh_ref[...], wm_ref[...])
    x1 = x_ref[...] + _mod_row(g1_ref, per_row) * y
    x1_ref[...] = x1
    r = lax.rsqrt(jnp.mean(x1 * x1, axis=-1, keepdims=True) + EPS)
    h2 = (x1 * r * nw_ref[...]) * (1.0 + _mod_row(sc_ref, per_row)) + _mod_row(sh_ref, per_row)
    h2_ref[...] = h2.astype(BF16)


def _outproj(att, mh, x, mod, mod_row_block, norm_w, w_att, w_m, *, tm, per_row):
    rows = x.shape[0]
    mod_rows = tm if per_row else 8
    mod_idx = (lambda i: i) if per_row else (lambda i: mod_row_block)
    kern = functools.partial(_outproj_kernel, per_row=per_row)
    return pl.pallas_call(
        kern,
        grid=(rows // tm,),
        in_specs=[
            pl.BlockSpec((tm, ATT_WIDTH), lambda i: (i, 0)),
            pl.BlockSpec((tm, MLSTM_WIDTH), lambda i: (i, 0)),
            pl.BlockSpec((tm, D_MODEL), lambda i: (i, 0)),
            pl.BlockSpec((mod_rows, D_MODEL), lambda i: (mod_idx(i), 2)),
            pl.BlockSpec((mod_rows, D_MODEL), lambda i: (mod_idx(i), 3)),
            pl.BlockSpec((mod_rows, D_MODEL), lambda i: (mod_idx(i), 4)),
            pl.BlockSpec((1, D_MODEL), lambda i: (0, 0)),
            pl.BlockSpec((ATT_WIDTH, D_MODEL), lambda i: (0, 0)),
            pl.BlockSpec((MLSTM_WIDTH, D_MODEL), lambda i: (0, 0)),
        ],
        out_specs=[
            pl.BlockSpec((tm, D_MODEL), lambda i: (i, 0)),
            pl.BlockSpec((tm, D_MODEL), lambda i: (i, 0)),
        ],
        out_shape=[
            jax.ShapeDtypeStruct((rows, D_MODEL), F32),
            jax.ShapeDtypeStruct((rows, D_MODEL), BF16),
        ],
        compiler_params=_cparams(("arbitrary",)),
        name="outproj_s" if per_row else "outproj_p",
    )(att, mh, x, mod, mod, mod, norm_w, w_att, w_m)


def _ffn_a_kernel(hp_ref, hs_ref, wg_ref, wu_ref, ap_ref, as_ref, wg_scr, wu_scr):
    i = pl.program_id(1)

    def swiglu(h_ref, a_ref):
        h = h_ref[...]
        g = _dot(h, wg_scr[...])
        u = _dot(h, wu_scr[...])
        a_ref[...] = (g * _sigmoid(g) * u).astype(BF16)

    @pl.when(i == 0)
    def _():
        wg_scr[...] = wg_ref[...].astype(BF16)
        wu_scr[...] = wu_ref[...].astype(BF16)
        swiglu(hs_ref, as_ref)

    @pl.when(i > 0)
    def _():
        swiglu(hp_ref, ap_ref)


def _ffn_a(h2_p, h2_s, w_gate, w_up, *, tm):
    rows_p, rows_s = h2_p.shape[0], h2_s.shape[0]
    tn = 512
    ni = rows_p // tm
    tile = lambda i: jnp.maximum(i - 1, 0)
    return pl.pallas_call(
        _ffn_a_kernel,
        grid=(D_FF // tn, ni + 1),
        in_specs=[
            pl.BlockSpec((tm, D_MODEL), lambda j, i: (tile(i), 0)),
            pl.BlockSpec((rows_s, D_MODEL), lambda j, i: (0, 0)),
            pl.BlockSpec((D_MODEL, tn), lambda j, i: (0, j)),
            pl.BlockSpec((D_MODEL, tn), lambda j, i: (0, j)),
        ],
        out_specs=[
            pl.BlockSpec((tm, tn), lambda j, i: (tile(i), j)),
            pl.BlockSpec((rows_s, tn), lambda j, i: (0, j)),
        ],
        out_shape=[
            jax.ShapeDtypeStruct((rows_p, D_FF), BF16),
            jax.ShapeDtypeStruct((rows_s, D_FF), BF16),
        ],
        scratch_shapes=[pltpu.VMEM((D_MODEL, tn), BF16), pltpu.VMEM((D_MODEL, tn), BF16)],
        compiler_params=_cparams(("arbitrary", "arbitrary")),
        name="ffn_a",
    )(h2_p, h2_s, w_gate, w_up)


def _ffn_b_kernel(a_ref, wd_ref, x1_ref, g2_ref, fw_ref, y_ref, *rest, per_row, emit_w):
    kk = pl.program_id(1)

    def weight_tile():
        wt = wd_ref[...].astype(BF16)
        if emit_w:
            rest[0][...] = wt
        return wt

    @pl.when(kk == 0)
    def _():
        y_ref[...] = _dot(a_ref[...], weight_tile())

    @pl.when(kk > 0)
    def _():
        y_ref[...] += _dot(a_ref[...], weight_tile())

    @pl.when(kk == pl.num_programs(1) - 1)
    def _():
        x2 = x1_ref[...] + _mod_row(g2_ref, per_row) * y_ref[...]
        y_ref[...] = x2 * lax.rsqrt(jnp.mean(x2 * x2, axis=-1, keepdims=True) + EPS) * fw_ref[...]


def _ffn_b(a, w_down, x1, mod, mod_row_block, final_w, *, tm, tk, per_row, emit_w):
    rows = a.shape[0]
    mod_rows = tm if per_row else 8
    mod_idx = (lambda i: i) if per_row else (lambda i: mod_row_block)
    kern = functools.partial(_ffn_b_kernel, per_row=per_row, emit_w=emit_w)
    w_spec = pl.BlockSpec((tk, D_MODEL), lambda i, k: (k, 0))
    y_spec = pl.BlockSpec((tm, D_MODEL), lambda i, k: (i, 0))
    y_shape = jax.ShapeDtypeStruct((rows, D_MODEL), F32)
    return pl.pallas_call(
        kern,
        grid=(rows // tm, D_FF // tk),
        in_specs=[
            pl.BlockSpec((tm, tk), lambda i, k: (i, k)),
            w_spec,
            pl.BlockSpec((tm, D_MODEL), lambda i, k: (i, 0)),
            pl.BlockSpec((mod_rows, D_MODEL), lambda i, k: (mod_idx(i), 5)),
            pl.BlockSpec((1, D_MODEL), lambda i, k: (0, 0)),
        ],
        out_specs=[y_spec, w_spec] if emit_w else y_spec,
        out_shape=[y_shape, jax.ShapeDtypeStruct((D_FF, D_MODEL), BF16)] if emit_w else y_shape,
        compiler_params=_cparams(("arbitrary", "arbitrary")),
        name="ffn_b_s" if per_row else "ffn_b_p",
    )(a, w_down, x1, mod, final_w)


def _rope_tables(pos):
    half = ROPE_DIM // 2
    inv = np.float32(ROPE_THETA) ** (-np.arange(0, ROPE_DIM, 2, dtype=np.float32) / np.float32(ROPE_DIM))
    d = np.arange(LANES) % ATT_HEAD_DIM
    ang = pos.astype(np.float32)[:, None] * inv[d % half][None, :].astype(np.float32)
    cos, sin = np.cos(ang), np.sin(ang)
    d = d[None, :]
    tables = (np.where(d < ROPE_DIM, cos, 1.0), np.where(d < half, -sin, 0.0),
              np.where((d >= half) & (d < ROPE_DIM), sin, 0.0))
    return tuple(jnp.asarray(t.astype(np.float32)) for t in tables)


def kernel(x_prompt, x_sample, cache_k_win, cache_v_win, state_C, state_n, state_m, c_prompt, c_sample,
           norm1_w, norm2_w, final_norm_w, w_ada, b_ada, w_in, b_ig, b_fg, attn_sinks, mh_norm_w,
           w_out, w_gate, w_up, w_down):
    assert w_in.shape[0] == 1, "single-layer trunk"
    T = DEC_SEQ
    xp = x_prompt[0]
    xs = x_sample.reshape(S_ROWS, D_MODEL)

    c_all = jnp.concatenate([c_sample, c_prompt, jnp.zeros((15, D_MODEL), F32)], axis=0)
    mod = _ada(c_all, w_ada[0], b_ada)
    prompt_mod_block = S_ROWS // 8

    w_in_t = jnp.transpose(w_in[0])
    wq_t = (w_in_t[:ATT_WIDTH].reshape(ATT_KV_HEADS, ATT_GROUP, ATT_HEAD_DIM, D_MODEL)
            .transpose(1, 0, 2, 3).reshape(ATT_WIDTH, D_MODEL).astype(BF16))
    w_gates_t = w_in_t[MAIN_WIDTH:]
    wg = jnp.pad(w_gates_t, ((0, 16 - 2 * MLSTM_HEADS), (0, 0)))
    w_out_att = (w_out[0, :ATT_WIDTH].reshape(ATT_KV_HEADS, ATT_GROUP, ATT_HEAD_DIM, D_MODEL)
                 .transpose(1, 0, 2, 3).reshape(ATT_WIDTH, D_MODEL).astype(BF16))
    w_out_m = w_out[0, ATT_WIDTH:].astype(BF16)
    n1 = norm1_w.reshape(1, D_MODEL)
    n2 = norm2_w.reshape(1, D_MODEL)
    fw = final_norm_w.reshape(1, D_MODEL)
    nw = mh_norm_w.reshape(1, MLSTM_WIDTH)
    gate_bias = jnp.concatenate([b_ig[0], b_fg[0]])
    brow = jnp.pad(gate_bias, (0, LANES - 2 * MLSTM_HEADS)).reshape(1, LANES)
    bcol = jnp.broadcast_to(jnp.pad(gate_bias, (0, 16 - 2 * MLSTM_HEADS))[:, None], (16, MLSTM_CHUNK_P))

    rope_p = _rope_tables(np.arange(SEQ))
    rope_s = _rope_tables(np.tile(PAST_LEN + np.arange(T), DEC_BATCH))

    tm_p = 1024
    q_s, _, kv32_s, m_s, g_s, _, w_in_bf = _inproj(xs, mod, 0, n1, wq_t, w_in_t, wg, *rope_s,
                                                   tm=S_ROWS, per_row=True, emit_w=True)
    q_p, kv_p, kv32_p, m_p, g_p, gt_p = _inproj(xp, mod, prompt_mod_block, n1, wq_t, w_in_bf, wg, *rope_p,
                                                tm=tm_p, per_row=False, emit_w=False)

    sinks = attn_sinks[0]
    sink_col = jnp.broadcast_to(sinks.reshape(ATT_HEADS, 1, 1), (ATT_HEADS, 8, LANES)).reshape(128, LANES)
    ck = jnp.transpose(cache_k_win[0], (0, 2, 3, 1))
    cv = jnp.transpose(cache_v_win[0], (0, 2, 3, 1))
    att_s, kwin_s, vwin_s = _attn_s(sink_col, q_s, kv32_s, ck, cv)

    n_rep = jnp.repeat(state_n[0].reshape(DEC_BATCH, MLSTM_WIDTH), T, axis=0)
    m_rep = jnp.pad(jnp.repeat(state_m[0], T, axis=0), ((0, 0), (0, LANES - MLSTM_HEADS)))
    mh_s, c_s, nrow_s, mrow_s = _mlstm_s(m_s, g_s, n_rep, m_rep, brow, nw, state_C[0])

    x1_p, h2_p, c_p, n_p, mm_p = _mix_p(sinks, q_p, kv_p, m_p, g_p, gt_p, brow, bcol, nw, xp, mod, prompt_mod_block,
                                        n2, w_out_att, w_out_m)
    x1_s, h2_s = _outproj(att_s, mh_s, xs, mod, 0, n2, w_out_att, w_out_m, tm=S_ROWS // 2, per_row=True)
    a_p, a_s = _ffn_a(h2_p, h2_s, w_gate[0], w_up[0], tm=tm_p)
    y_s, w_down_bf = _ffn_b(a_s, w_down[0], x1_s, mod, 0, fw, tm=S_ROWS, tk=1408, per_row=True, emit_w=True)
    y_p = _ffn_b(a_p, w_down_bf, x1_p, mod, prompt_mod_block, fw, tm=tm_p, tk=512, per_row=False, emit_w=False)

    kv_shape = (1, 1, WINDOW, ATT_KV_HEADS, ATT_HEAD_DIM)
    kv_last = kv32_p[tm_p - WINDOW:]
    dh = MLSTM_HEAD_DIM
    return (
        y_p.reshape(1, SEQ, D_MODEL),
        y_s.reshape(DEC_BATCH, T, D_MODEL),
        kv_last[:, :KV_WIDTH].reshape(kv_shape),
        kv_last[:, KV_WIDTH:].reshape(kv_shape),
        c_p.reshape(1, 1, MLSTM_HEADS, dh, dh),
        n_p[:MLSTM_HEADS].reshape(1, 1, MLSTM_HEADS, dh),
        mm_p[:MLSTM_HEADS, 0].reshape(1, 1, MLSTM_HEADS),
        jnp.transpose(kwin_s, (0, 3, 1, 2))[None],
        jnp.transpose(vwin_s, (0, 3, 1, 2))[None],
        c_s.reshape(1, DEC_BATCH, MLSTM_HEADS, dh, dh),
        nrow_s[T - 1::T].reshape(1, DEC_BATCH, MLSTM_HEADS, dh),
        mrow_s[T - 1::T, :MLSTM_HEADS].reshape(1, DEC_BATCH, MLSTM_HEADS),
    )
```

```python
import functools

import jax
import jax.numpy as jnp
import numpy as np
from jax import lax
from jax.experimental import pallas as pl
from jax.experimental.pallas import tpu as pltpu

F32 = jnp.float32
BF16 = jnp.bfloat16

D_MODEL = 2048
SEQ = 8192
DEC_BATCH = 128
DEC_SEQ = 4
S_ROWS = DEC_BATCH * DEC_SEQ
PAST_LEN = 16384
ATT_HEADS = 16
ATT_KV_HEADS = 4
ATT_GROUP = 4
ATT_HEAD_DIM = 64
WINDOW = 128
ROPE_THETA = 500000.0
ROPE_DIM = 16
MLSTM_HEADS = 4
MLSTM_HEAD_DIM = 256
ATT_WIDTH = 1024
KV_WIDTH = 256
MLSTM_WIDTH = 1024
MAIN_WIDTH = ATT_WIDTH + 2 * KV_WIDTH + 4 * MLSTM_WIDTH
D_FF = 5632
N_MOD = 6
EPS = 1e-6

LANES = 128
MLSTM_CHUNK_P = 256
VMEM_LIMIT = 56 * 1024 * 1024

NT_DIMS = (((1,), (1,)), ((), ()))
TN_DIMS = (((0,), (0,)), ((), ()))
HI = lax.Precision.HIGHEST


def _cparams(sem):
    return pltpu.CompilerParams(dimension_semantics=sem, vmem_limit_bytes=VMEM_LIMIT)


def _dot(a, b):
    return jnp.dot(a, b, preferred_element_type=F32)


def _dot_nt(a, b):
    return lax.dot_general(a, b, NT_DIMS, preferred_element_type=F32)


def _dot_tn(a, b):
    return lax.dot_general(a, b, TN_DIMS, preferred_element_type=F32)


def _sigmoid(x):
    return 1.0 / (1.0 + jnp.exp(-x))


def _log_sigmoid(x):
    return jnp.minimum(x, 0.0) - jnp.log(1.0 + jnp.exp(-jnp.abs(x)))


def _mod_row(ref, per_row):
    return ref[...] if per_row else ref[0:1, :]


def _ada_kernel(c_ref, wa_ref, wb_ref, b_ref, o_ref, s_scr, rep_scr):
    @pl.when(pl.program_id(0) == 0)
    def _():
        c = c_ref[...]
        s_scr[...] = (c * _sigmoid(c)).astype(BF16)

    half = D_MODEL // 2
    mod = (_dot(s_scr[:, :half], wa_ref[...].astype(BF16)) + _dot(s_scr[:, half:], wb_ref[...].astype(BF16))
           + b_ref[...])
    for c in range(o_ref.shape[1] // LANES):
        cols = slice(LANES * c, LANES * (c + 1))
        for t in range(DEC_SEQ):
            rep_scr[pl.ds(t, DEC_BATCH, stride=DEC_SEQ), :] = mod[0:DEC_BATCH, cols]
        o_ref[0:S_ROWS, cols] = rep_scr[...]
    o_ref[S_ROWS:, :] = mod[DEC_BATCH:]


def _ada(c_all, w_ada, b_ada):
    m = c_all.shape[0]
    m_out = S_ROWS + m - DEC_BATCH
    n = w_ada.shape[1]
    tn = 1024
    return pl.pallas_call(
        _ada_kernel,
        grid=(n // tn,),
        in_specs=[
            pl.BlockSpec((m, D_MODEL), lambda j: (0, 0)),
            pl.BlockSpec((D_MODEL // 2, tn), lambda j: (0, j)),
            pl.BlockSpec((D_MODEL // 2, tn), lambda j: (1, j)),
            pl.BlockSpec((1, tn), lambda j: (0, j)),
        ],
        out_specs=pl.BlockSpec((m_out, tn), lambda j: (0, j)),
        out_shape=jax.ShapeDtypeStruct((m_out, n), F32),
        scratch_shapes=[pltpu.VMEM((m, D_MODEL), BF16), pltpu.VMEM((S_ROWS, LANES), F32)],
        compiler_params=_cparams(("arbitrary",)),
        name="ada",
    )(c_all, w_ada, w_ada, b_ada)


def _rope_store(acc, cos, sa, sb, out_ref, ncols, scale):
    for c in range(ncols // LANES):
        xc = acc[:, LANES * c:LANES * (c + 1)]
        rot = xc * cos + pltpu.roll(xc, LANES - 8, 1) * sa + pltpu.roll(xc, 8, 1) * sb
        if scale != 1.0:
            rot = rot * scale
        out_ref[:, LANES * c:LANES * (c + 1)] = rot.astype(out_ref.dtype)


def _inproj_kernel(x_ref, sh_ref, sc_ref, nw_ref, wq_ref, win_ref, wg_ref, cos_ref, sa_ref, sb_ref,
                   q_ref, kv_ref, kv32_ref, m_ref, g_ref, gt_ref, *rest, per_row, emit_w):
    h_scr = rest[-1]
    j = pl.program_id(1)

    def weight_tile():
        wt = win_ref[...].astype(BF16)
        if emit_w:
            rest[0][...] = wt
        return wt

    @pl.when(j == 0)
    def _():
        x = x_ref[...]
        r = lax.rsqrt(jnp.mean(x * x, axis=-1, keepdims=True) + EPS)
        gain = nw_ref[...] * (1.0 + _mod_row(sc_ref, per_row))
        h = (x * r * gain + _mod_row(sh_ref, per_row)).astype(BF16)
        h_scr[...] = h
        wg = jnp.concatenate([wg_ref[...].astype(BF16), jnp.zeros((LANES - 16, D_MODEL), BF16)], axis=0)
        g = _dot_nt(h, wg)
        g_ref[...] = g
        gt_ref[...] = g.T[0:16, :]

    @pl.when(j < 2)
    def _():
        acc = _dot_nt(h_scr[...], wq_ref[...])
        _rope_store(acc, cos_ref[...], sa_ref[...], sb_ref[...], q_ref, 512, ATT_HEAD_DIM ** -0.5)

    @pl.when(j == 2)
    def _():
        acc = _dot_nt(h_scr[...], weight_tile())
        _rope_store(acc, cos_ref[...], sa_ref[...], sb_ref[...], kv32_ref, KV_WIDTH, 1.0)
        kv32_ref[:, KV_WIDTH:] = acc[:, KV_WIDTH:]
        kv_ref[...] = kv32_ref[...].astype(BF16)

    @pl.when(j > 2)
    def _():
        m_ref[...] = _dot_nt(h_scr[...], weight_tile()).astype(BF16)


def _inproj(x, mod, mod_row_block, norm_w, wq_t, w_in_t, wg, cos, sa, sb, *, tm, per_row, emit_w):
    rows = x.shape[0]
    tn = 512
    nj = MAIN_WIDTH // tn
    mod_rows = tm if per_row else 8
    mod_idx = (lambda i: i) if per_row else (lambda i: mod_row_block)
    kern = functools.partial(_inproj_kernel, per_row=per_row, emit_w=emit_w)
    copy_spec = pl.BlockSpec((tn, D_MODEL), lambda i, j: (jnp.maximum(j, 2) - 2, 0))
    w_spec = pl.BlockSpec((tn, D_MODEL), lambda i, j: (jnp.maximum(j, 2), 0)) if emit_w else copy_spec
    extra_specs = [copy_spec] if emit_w else []
    extra_shapes = [jax.ShapeDtypeStruct((MAIN_WIDTH - ATT_WIDTH, D_MODEL), BF16)] if emit_w else []
    return pl.pallas_call(
        kern,
        grid=(rows // tm, nj),
        in_specs=[
            pl.BlockSpec((tm, D_MODEL), lambda i, j: (jnp.minimum(i + jnp.minimum(j, 1), rows // tm - 1), 0)),
            pl.BlockSpec((mod_rows, D_MODEL), lambda i, j: (mod_idx(i), 0)),
            pl.BlockSpec((mod_rows, D_MODEL), lambda i, j: (mod_idx(i), 1)),
            pl.BlockSpec((1, D_MODEL), lambda i, j: (0, 0)),
            pl.BlockSpec((tn, D_MODEL), lambda i, j: (jnp.minimum(j, 1), 0)),
            w_spec,
            pl.BlockSpec((16, D_MODEL), lambda i, j: (0, 0)),
            pl.BlockSpec((tm, LANES), lambda i, j: (i, 0)),
            pl.BlockSpec((tm, LANES), lambda i, j: (i, 0)),
            pl.BlockSpec((tm, LANES), lambda i, j: (i, 0)),
        ],
        out_specs=[
            pl.BlockSpec((tm, tn), lambda i, j: (i, jnp.minimum(j, 1))),
            pl.BlockSpec((tm, tn), lambda i, j: (i, 0)),
            pl.BlockSpec((tm, tn), lambda i, j: (0, 0)),
            pl.BlockSpec((tm, tn), lambda i, j: (i, jnp.clip(j - 3, 0, 7))),
            pl.BlockSpec((tm, LANES), lambda i, j: (i, 0)),
            pl.BlockSpec((16, tm), lambda i, j: (0, i)),
        ] + extra_specs,
        out_shape=[
            jax.ShapeDtypeStruct((rows, ATT_WIDTH), BF16),
            jax.ShapeDtypeStruct((rows, 2 * KV_WIDTH), BF16),
            jax.ShapeDtypeStruct((tm, 2 * KV_WIDTH), F32),
            jax.ShapeDtypeStruct((rows, 4 * MLSTM_WIDTH), BF16),
            jax.ShapeDtypeStruct((rows, LANES), F32),
            jax.ShapeDtypeStruct((16, rows), F32),
        ] + extra_shapes,
        scratch_shapes=[pltpu.VMEM((tm, D_MODEL), BF16)],
        compiler_params=_cparams(("arbitrary", "arbitrary")),
        name="inproj_s" if per_row else "inproj_p",
    )(x, mod, mod, norm_w, wq_t, w_in_t, wg, cos, sa, sb)


def _attn_block(sink_ref, q_ref, row0, kv2, allowed, store):
    w = WINDOW
    grp = ATT_GROUP
    rows = grp * w
    member = lax.broadcasted_iota(jnp.int32, (rows, 1), 0) // w
    low = lax.broadcasted_iota(jnp.int32, (2 * w, LANES), 1) < ATT_HEAD_DIM
    low_o = lax.broadcasted_iota(jnp.int32, (rows, LANES), 1) < ATT_HEAD_DIM
    key_row = lax.broadcasted_iota(jnp.int32, (4 * w, LANES), 0)
    key_lane = lax.broadcasted_iota(jnp.int32, (4 * w, LANES), 1)
    ones_bd = (((key_row < 2 * w) & (key_lane < ATT_HEAD_DIM))
               | ((key_row >= 2 * w) & (key_lane >= ATT_HEAD_DIM))).astype(BF16)
    zero = jnp.zeros((2 * w, LANES), BF16)
    for cp in range(2):
        k128 = kv2[:, LANES * cp:LANES * (cp + 1)]
        v128 = kv2[:, KV_WIDTH + LANES * cp:KV_WIDTH + LANES * (cp + 1)]
        kbd = jnp.concatenate([jnp.where(low, k128, zero), jnp.where(low, zero, k128)], axis=0)
        vbd = jnp.concatenate([jnp.where(low, v128, zero), jnp.where(low, zero, v128)], axis=0)
        v_aug = jnp.concatenate([vbd, ones_bd], axis=1)
        q4 = jnp.concatenate([q_ref[row0:row0 + w, 256 * r + LANES * cp:256 * r + LANES * (cp + 1)]
                              for r in range(grp)], axis=0)
        s = _dot_nt(q4, kbd)
        es, tails = [], []
        for half in range(2):
            sh = jnp.where(allowed, s[:, 2 * w * half:2 * w * (half + 1)], -jnp.inf)
            head0 = (2 * cp + half) * grp
            sink = jnp.full((rows, 1), sink_ref[head0], F32)
            for r in range(1, grp):
                sink = jnp.where(member == r, sink_ref[head0 + r], sink)
            m = jnp.maximum(jnp.max(sh, axis=-1, keepdims=True), sink)
            es.append(jnp.exp(sh - m).astype(BF16))
            tails.append(jnp.exp(sink - m))
        oa = _dot(jnp.concatenate(es, axis=1), v_aug)
        l = oa[:, LANES:] + jnp.where(low_o, tails[0], tails[1])
        o = (oa[:, :LANES] / l).astype(BF16)
        for r in range(grp):
            store(256 * r + LANES * cp, o[w * r:w * (r + 1)])


MIX_TM = 512


def _mix_p_kernel(sink_ref, q_ref, kvp_ref, kvc_ref, mq_ref, mk_ref, mv_ref, mo_ref, g_ref, gt_ref, brow_ref, bcol_ref,
                  mnw_ref, x_ref, g1_ref, sh_ref, sc_ref, nw_ref, wa_ref, wm_ref,
                  x1_ref, h2_ref, c_out, n_out, m_out, att_scr, mh_scr, c_scr, n_scr, m_scr):
    s = pl.program_id(0)
    w = WINDOW
    tiles = pl.num_programs(0) - 1
    slot = s % 2

    @pl.when(s == 0)
    def _():
        c_scr[...] = jnp.zeros_like(c_scr)
        n_scr[...] = jnp.zeros_like(n_scr)
        m_scr[...] = jnp.zeros_like(m_scr)

    def project():
        y = _dot(att_scr[1 - slot], wa_ref[...]) + _dot(mh_scr[1 - slot], wm_ref[...])
        x1 = x_ref[...] + g1_ref[0:1, :] * y
        x1_ref[...] = x1
        r = lax.rsqrt(jnp.mean(x1 * x1, axis=-1, keepdims=True) + EPS)
        gain = nw_ref[...] * (1.0 + sc_ref[0:1, :])
        h2_ref[...] = (x1 * r * gain + sh_ref[0:1, :]).astype(BF16)

    def mixers():
        rows = ATT_GROUP * w
        qi = lax.broadcasted_iota(jnp.int32, (rows, 2 * w), 0) % w
        kj = lax.broadcasted_iota(jnp.int32, (rows, 2 * w), 1)
        first_off = jnp.where(s > 0, 0, 4 * w)
        causal = (kj >= w) & (kj - w <= qi)
        for blk in range(MIX_TM // w):
            prev = kvp_ref[...] if blk == 0 else kvc_ref[w * (blk - 1):w * blk, :]
            kv2 = jnp.concatenate([prev, kvc_ref[w * blk:w * (blk + 1), :]], axis=0)
            allowed = ((kj < w) & (kj > qi + (first_off if blk == 0 else 0))) | causal

            def store(c0, val, blk=blk):
                att_scr[slot, w * blk:w * (blk + 1), c0:c0 + LANES] = val

            _attn_block(sink_ref, q_ref, w * blk, kv2, allowed, store)

        for ch in range(MIX_TM // MLSTM_CHUNK_P):
            r0 = MLSTM_CHUNK_P * ch

            def store_mh(cs, val, r0=r0):
                mh_scr[slot, r0:r0 + MLSTM_CHUNK_P, cs] = val

            _mlstm_chunk(mq_ref, mk_ref, mv_ref, mo_ref, g_ref, gt_ref, brow_ref, bcol_ref, mnw_ref,
                         c_scr, n_scr, m_scr, r0, store_mh)

    @pl.when(s == 0)
    def _():
        mixers()

    @pl.when((s > 0) & (s < tiles))
    def _():
        project()
        mixers()

    @pl.when(s == tiles)
    def _():
        project()
        c_out[...] = c_scr[...]
        n_out[...] = n_scr[...]
        m_out[...] = m_scr[...]


def _mix_p(sinks, q, kv, m_all, gates, gates_t, brow, bcol, mnw, x, mod, mod_row_block, norm_w, w_att, w_m):
    tm = MIX_TM
    tiles = SEQ // tm
    bpt = tm // WINDOW
    dh = MLSTM_HEAD_DIM
    att_tile = lambda s: jnp.minimum(s, tiles - 1)
    out_tile = lambda s: jnp.maximum(s - 1, 0)
    m_spec = lambda col: pl.BlockSpec((tm, MLSTM_WIDTH), lambda s: (att_tile(s), col))
    return pl.pallas_call(
        _mix_p_kernel,
        grid=(tiles + 1,),
        in_specs=[
            pl.BlockSpec(memory_space=pltpu.SMEM),
            pl.BlockSpec((tm, ATT_WIDTH), lambda s: (att_tile(s), 0)),
            pl.BlockSpec((WINDOW, 2 * KV_WIDTH), lambda s: (jnp.maximum(bpt * att_tile(s) - 1, 0), 0)),
            pl.BlockSpec((tm, 2 * KV_WIDTH), lambda s: (att_tile(s), 0)),
            m_spec(0), m_spec(1), m_spec(2), m_spec(3),
            pl.BlockSpec((tm, LANES), lambda s: (att_tile(s), 0)),
            pl.BlockSpec((16, tm), lambda s: (0, att_tile(s))),
            pl.BlockSpec((1, LANES), lambda s: (0, 0)),
            pl.BlockSpec((16, MLSTM_CHUNK_P), lambda s: (0, 0)),
            pl.BlockSpec((1, MLSTM_WIDTH), lambda s: (0, 0)),
            pl.BlockSpec((tm, D_MODEL), lambda s: (out_tile(s), 0)),
            pl.BlockSpec((8, D_MODEL), lambda s: (mod_row_block, 2)),
            pl.BlockSpec((8, D_MODEL), lambda s: (mod_row_block, 3)),
            pl.BlockSpec((8, D_MODEL), lambda s: (mod_row_block, 4)),
            pl.BlockSpec((1, D_MODEL), lambda s: (0, 0)),
            pl.BlockSpec((ATT_WIDTH, D_MODEL), lambda s: (0, 0)),
            pl.BlockSpec((MLSTM_WIDTH, D_MODEL), lambda s: (0, 0)),
        ],
        out_specs=[
            pl.BlockSpec((tm, D_MODEL), lambda s: (out_tile(s), 0)),
            pl.BlockSpec((tm, D_MODEL), lambda s: (out_tile(s), 0)),
            pl.BlockSpec((MLSTM_HEADS, dh, dh), lambda s: (0, 0, 0)),
            pl.BlockSpec((8, dh), lambda s: (0, 0)),
            pl.BlockSpec((8, LANES), lambda s: (0, 0)),
        ],
        out_shape=[
            jax.ShapeDtypeStruct((SEQ, D_MODEL), F32),
            jax.ShapeDtypeStruct((SEQ, D_MODEL), BF16),
            jax.ShapeDtypeStruct((MLSTM_HEADS, dh, dh), F32),
            jax.ShapeDtypeStruct((8, dh), F32),
            jax.ShapeDtypeStruct((8, LANES), F32),
        ],
        scratch_shapes=[
            pltpu.VMEM((2, tm, ATT_WIDTH), BF16),
            pltpu.VMEM((2, tm, MLSTM_WIDTH), BF16),
            pltpu.VMEM((MLSTM_HEADS, dh, dh), F32),
            pltpu.VMEM((8, dh), F32),
            pltpu.VMEM((8, LANES), F32),
        ],
        compiler_params=_cparams(("arbitrary",)),
        name="mix_p",
    )(sinks, q, kv, kv, m_all, m_all, m_all, m_all, gates, gates_t, brow, bcol, mnw, x, mod, mod, mod, norm_w,
      w_att, w_m)


ATT_S_BB = 16


def _attn_s_kernel(sink_ref, q_ref, kv32_ref, ck_ref, cv_ref, o_ref, ko_ref, vo_ref, q32_scr):
    t_new = DEC_SEQ
    w = WINDOW
    q32_scr[...] = q_ref[...].astype(F32)
    rows = 4 * 4 * 8
    row = lax.broadcasted_iota(jnp.int32, (rows, w), 0)
    slot = lax.broadcasted_iota(jnp.int32, (rows, w), 1)
    t_row = row % t_new
    second = (row % 8) >= t_new
    win_ok = (slot < w - t_new) | (slot - (w - t_new) <= t_row)
    old_ok = (slot >= 1) & (slot < t_new) & (slot > t_row)
    lane256 = lax.broadcasted_iota(jnp.int32, (32, 2 * LANES), 1)
    sink = sink_ref[...][:, 0:1]
    kv_new = jnp.concatenate([kv32_ref[...], jnp.zeros((w - ATT_S_BB * t_new, 2 * KV_WIDTH), F32)], axis=0)
    kv_t = kv_new.T
    new_slot = lax.broadcasted_iota(jnp.int32, (KV_WIDTH, w), 1) >= w - t_new
    for b in range(ATT_S_BB):
        cols = pltpu.roll(kv_t, w - t_new - t_new * b, 1)
        k_shift = pltpu.roll(ck_ref[b].reshape(KV_WIDTH, w), w - t_new, 1)
        v_shift = pltpu.roll(cv_ref[b].reshape(KV_WIDTH, w), w - t_new, 1)
        ko_ref[b] = jnp.where(new_slot, cols[:KV_WIDTH], k_shift).reshape(ATT_KV_HEADS, ATT_HEAD_DIM, w)
        vo_ref[b] = jnp.where(new_slot, cols[KV_WIDTH:], v_shift).reshape(ATT_KV_HEADS, ATT_HEAD_DIM, w)
    for pair in range(ATT_S_BB // 2):
        b0, b1 = 2 * pair, 2 * pair + 1
        q32 = jnp.concatenate([q32_scr[8 * pair:8 * (pair + 1), 256 * r:256 * (r + 1)] for r in range(ATT_GROUP)],
                              axis=0)
        qpad = jnp.concatenate(
            [jnp.where((lane256 // ATT_HEAD_DIM) == g, q32, 0.0) for g in range(ATT_KV_HEADS)], axis=0).astype(BF16)
        kw = [ko_ref[b].reshape(KV_WIDTH, w).astype(BF16) for b in (b0, b1)]
        vw = [vo_ref[b].reshape(KV_WIDTH, w).astype(BF16) for b in (b0, b1)]
        kc = [ck_ref[b].reshape(KV_WIDTH, w).astype(BF16) for b in (b0, b1)]
        vc = [cv_ref[b].reshape(KV_WIDTH, w).astype(BF16) for b in (b0, b1)]
        s_w = jnp.where(second, _dot(qpad, kw[1]), _dot(qpad, kw[0]))
        s_c = jnp.where(second, _dot(qpad, kc[1]), _dot(qpad, kc[0]))
        s_w = jnp.where(win_ok, s_w, -jnp.inf)
        s_c = jnp.where(old_ok, s_c, -jnp.inf)
        m = jnp.maximum(jnp.maximum(jnp.max(s_w, axis=-1, keepdims=True), jnp.max(s_c, axis=-1, keepdims=True)), sink)
        e_w = jnp.exp(s_w - m)
        e_c = jnp.exp(s_c - m)
        l = jnp.sum(e_w, axis=-1, keepdims=True) + jnp.sum(e_c, axis=-1, keepdims=True) + jnp.exp(sink - m)
        p_w = e_w / l
        p_c = e_c / l
        zero = jnp.zeros_like(p_w)
        o = (_dot_nt(jnp.where(second, zero, p_w).astype(BF16), vw[0])
             + _dot_nt(jnp.where(second, p_w, zero).astype(BF16), vw[1])
             + _dot_nt(jnp.where(second, zero, p_c).astype(BF16), vc[0])
             + _dot_nt(jnp.where(second, p_c, zero).astype(BF16), vc[1]))
        o32 = jnp.zeros((32, 2 * LANES), F32)
        for g in range(ATT_KV_HEADS):
            o32 = jnp.where((lane256 // ATT_HEAD_DIM) == g, o[32 * g:32 * (g + 1), :], o32)
        for r in range(ATT_GROUP):
            o_ref[8 * pair:8 * (pair + 1), 256 * r:256 * (r + 1)] = o32[8 * r:8 * (r + 1), :]


def _attn_s(sink_col, q, kv32, ck, cv):
    bb = ATT_S_BB
    rows = bb * DEC_SEQ
    cache_block = (bb, ATT_KV_HEADS, ATT_HEAD_DIM, WINDOW)
    cache_shape = (DEC_BATCH, ATT_KV_HEADS, ATT_HEAD_DIM, WINDOW)
    return pl.pallas_call(
        _attn_s_kernel,
        grid=(DEC_BATCH // bb,),
        in_specs=[
            pl.BlockSpec((128, LANES), lambda i: (0, 0)),
            pl.BlockSpec((rows, ATT_WIDTH), lambda i: (i, 0)),
            pl.BlockSpec((rows, 2 * KV_WIDTH), lambda i: (i, 0)),
            pl.BlockSpec(cache_block, lambda i: (i, 0, 0, 0)),
            pl.BlockSpec(cache_block, lambda i: (i, 0, 0, 0)),
        ],
        out_specs=[
            pl.BlockSpec((rows, ATT_WIDTH), lambda i: (i, 0)),
            pl.BlockSpec(cache_block, lambda i: (i, 0, 0, 0)),
            pl.BlockSpec(cache_block, lambda i: (i, 0, 0, 0)),
        ],
        out_shape=[
            jax.ShapeDtypeStruct((S_ROWS, ATT_WIDTH), F32),
            jax.ShapeDtypeStruct(cache_shape, F32),
            jax.ShapeDtypeStruct(cache_shape, F32),
        ],
        scratch_shapes=[pltpu.VMEM((rows, ATT_WIDTH), F32)],
        compiler_params=_cparams(("arbitrary",)),
        name="attn_s",
    )(sink_col, q, kv32, ck, cv)


def _head_norm_gate(h, nw, mo):
    hn = h * lax.rsqrt(jnp.mean(h * h, axis=-1, keepdims=True) + EPS) * nw
    return hn * _sigmoid(mo.astype(F32))


def _mlstm_chunk(q_ref, k_ref, v_ref, mo_ref, g_ref, gt_ref, brow_ref, bcol_ref, nw_ref, c_scr, n_scr, m_scr,
                 r0, store):
    L = MLSTM_CHUNK_P
    dh = MLSTM_HEAD_DIM
    rs = slice(r0, r0 + L)
    ti = lax.broadcasted_iota(jnp.int32, (L, L), 0)
    si = lax.broadcasted_iota(jnp.int32, (L, L), 1)
    causal = si <= ti
    tri = causal.astype(F32)
    tri_t = (ti <= si).astype(F32)
    gates = g_ref[rs, :] + brow_ref[...]
    gates_t = gt_ref[:, rs] + bcol_ref[...]
    b_col = jnp.dot(tri, _log_sigmoid(gates), precision=HI, preferred_element_type=F32)
    b_row = jnp.dot(_log_sigmoid(gates_t), tri_t, precision=HI, preferred_element_type=F32)
    for hd in range(MLSTM_HEADS):
        cs = slice(dh * hd, dh * (hd + 1))
        b_c = b_col[:, MLSTM_HEADS + hd:MLSTM_HEADS + hd + 1]
        li_c = gates[:, hd:hd + 1]
        b_r = b_row[MLSTM_HEADS + hd:MLSTM_HEADS + hd + 1, :]
        li_r = gates_t[hd:hd + 1, :]
        dm = jnp.where(causal, b_c - b_r + li_r, -jnp.inf)
        m_prev = m_scr[hd:hd + 1, 0:1]
        m_inter = b_c + m_prev
        m_t = jnp.maximum(m_inter, jnp.max(dm, axis=-1, keepdims=True))
        q = q_ref[rs, cs] * (dh ** -0.5)
        k = k_ref[rs, cs]
        v = v_ref[rs, cs]
        sm = _dot_nt(q, k) * jnp.exp(dm - m_t)
        a = jnp.exp(m_inter - m_t)
        c_old = c_scr[hd]
        n_old = n_scr[hd:hd + 1, :]
        num = a * _dot_nt(q, c_old.astype(BF16)) + _dot(sm.astype(BF16), v)
        qn = jnp.sum(q.astype(F32) * n_old, axis=-1, keepdims=True)
        den = a * qn + jnp.sum(sm, axis=-1, keepdims=True)
        h = num / jnp.maximum(jnp.abs(den), jnp.exp(-m_t))
        store(cs, _head_norm_gate(h, nw_ref[:, cs], mo_ref[rs, cs]).astype(BF16))
        m_new = m_t[L - 1:L, :]
        b_last = b_c[L - 1:L, :]
        g = jnp.exp(b_last - b_c + li_c - m_new)
        decay = jnp.exp(b_last + m_prev - m_new)
        gv = (g * v.astype(F32)).astype(BF16)
        c_scr[hd] = decay * c_old + _dot_tn(gv, k)
        n_scr[hd:hd + 1, :] = decay * n_old + jnp.sum(g * k.astype(F32), axis=0, keepdims=True)
        m_scr[hd:hd + 1, :] = jnp.broadcast_to(m_new, (1, LANES))


MLSTM_S_BB = 8


def _mlstm_s_kernel(q_ref, k_ref, v_ref, mo_ref, g_ref, nrep_ref, mrep_ref, brow_ref, nw_ref, c_ref,
                    mh_ref, c_out, nrow_ref, mrow_ref):
    T = DEC_SEQ
    R = MLSTM_S_BB * T
    H = MLSTM_HEADS
    dh = MLSTM_HEAD_DIM

    def shift(x, d):
        return pltpu.roll(x, d, 0)

    lanes = lax.broadcasted_iota(jnp.int32, (R, LANES), 1)
    tmod = lax.broadcasted_iota(jnp.int32, (R, LANES), 0) % T
    tmod_w = lax.broadcasted_iota(jnp.int32, (R, MLSTM_WIDTH), 0) % T
    head_ok = lanes < H
    gates = g_ref[...] + brow_ref[...]
    li = jnp.where(head_ok, gates, 0.0)
    lf = jnp.where(head_ok, pltpu.roll(_log_sigmoid(gates), LANES - H, 1), 0.0)
    bcum = lf
    for d in range(1, T):
        bcum = bcum + jnp.where(tmod >= d, shift(lf, d), 0.0)
    m0 = mrep_ref[...]
    m_inter = bcum + m0
    dms = [li] + [jnp.where(tmod >= d, bcum - shift(bcum, d) + shift(li, d), -jnp.inf) for d in range(1, T)]
    m_t = m_inter
    for dm in dms:
        m_t = jnp.maximum(m_t, dm)
    a = jnp.exp(m_inter - m_t)
    ws = [jnp.exp(dm - m_t) for dm in dms]

    q_bf = q_ref[...] * (dh ** -0.5)
    q = q_bf.astype(F32)
    k = k_ref[...].astype(F32)
    v = v_ref[...].astype(F32)
    seg = (lax.broadcasted_iota(jnp.int32, (MLSTM_WIDTH, LANES), 0) // dh
           == lax.broadcasted_iota(jnp.int32, (MLSTM_WIDTH, LANES), 1)).astype(F32)
    ex = (lax.broadcasted_iota(jnp.int32, (LANES, MLSTM_WIDTH), 1) // dh
          == lax.broadcasted_iota(jnp.int32, (LANES, MLSTM_WIDTH), 0)).astype(F32)

    def segsum(x):
        return jnp.dot(x, seg, precision=HI, preferred_element_type=F32)

    def expand(x):
        return jnp.dot(x, ex, precision=HI, preferred_element_type=F32)

    ks = [k] + [shift(k, d) for d in range(1, T)]
    vs = [v] + [shift(v, d) for d in range(1, T)]
    sms = [segsum(q * ks[d]) * ws[d] for d in range(T)]
    den = a * segsum(q * nrep_ref[...])
    for sm in sms:
        den = den + sm
    inv = 1.0 / jnp.maximum(jnp.abs(den), jnp.exp(-m_t))

    def last(x):
        out = jnp.zeros_like(x)
        for jj in range(T):
            out = jnp.where(tmod == T - 1 - jj, x if jj == 0 else pltpu.roll(x, R - jj, 0), out)
        return out

    m_new = last(m_t)
    b_last = last(bcum)
    g = jnp.where(head_ok, jnp.exp(b_last - bcum + li - m_new), 0.0)
    decay = jnp.where(head_ok, jnp.exp(b_last + m0 - m_new), 0.0)
    a_f = expand(a * inv)
    w_f = [expand(sm * inv) for sm in sms]
    g_f = expand(g)
    d_f = expand(decay)
    gv = (g_f * v).astype(BF16)
    rowb = lax.broadcasted_iota(jnp.int32, (R, dh), 0) // T
    for hd in range(H):
        cs = slice(dh * hd, dh * (hd + 1))
        qh = q_bf[:, cs]
        kh = k_ref[:, cs]
        gvh = gv[:, cs]
        qc = jnp.zeros((R, dh), F32)
        for bb in range(MLSTM_S_BB):
            c_old = c_ref[bb, hd]
            qc = jnp.where(rowb == bb, _dot_nt(qh, c_old.astype(BF16)), qc)
            upd = _dot_tn(jnp.where(rowb == bb, gvh, jnp.zeros_like(gvh)), kh)
            c_out[bb, hd] = d_f[T * bb:T * bb + 1, cs] * c_old + upd
        h = a_f[:, cs] * qc
        for d in range(T):
            h = h + w_f[d][:, cs] * vs[d][:, cs]
        mh_ref[:, cs] = _head_norm_gate(h, nw_ref[:, cs], mo_ref[:, cs]).astype(BF16)
    gk = g_f * k
    nsum = gk
    for d in range(1, T):
        nsum = nsum + jnp.where(tmod_w >= d, shift(gk, d), 0.0)
    nrow_ref[...] = d_f * nrep_ref[...] + nsum
    mrow_ref[...] = m_t


def _mlstm_s(m_all, gates, n_rep, m_rep, brow, nw, state_c):
    bb = MLSTM_S_BB
    R = bb * DEC_SEQ
    dh = MLSTM_HEAD_DIM
    H = MLSTM_HEADS
    return pl.pallas_call(
        _mlstm_s_kernel,
        grid=(DEC_BATCH // bb,),
        in_specs=[
            pl.BlockSpec((R, MLSTM_WIDTH), lambda i: (i, 0)),
            pl.BlockSpec((R, MLSTM_WIDTH), lambda i: (i, 1)),
            pl.BlockSpec((R, MLSTM_WIDTH), lambda i: (i, 2)),
            pl.BlockSpec((R, MLSTM_WIDTH), lambda i: (i, 3)),
            pl.BlockSpec((R, LANES), lambda i: (i, 0)),
            pl.BlockSpec((R, MLSTM_WIDTH), lambda i: (i, 0)),
            pl.BlockSpec((R, LANES), lambda i: (i, 0)),
            pl.BlockSpec((1, LANES), lambda i: (0, 0)),
            pl.BlockSpec((1, MLSTM_WIDTH), lambda i: (0, 0)),
            pl.BlockSpec((bb, H, dh, dh), lambda i: (i, 0, 0, 0)),
        ],
        out_specs=[
            pl.BlockSpec((R, MLSTM_WIDTH), lambda i: (i, 0)),
            pl.BlockSpec((bb, H, dh, dh), lambda i: (i, 0, 0, 0)),
            pl.BlockSpec((R, MLSTM_WIDTH), lambda i: (i, 0)),
            pl.BlockSpec((R, LANES), lambda i: (i, 0)),
        ],
        out_shape=[
            jax.ShapeDtypeStruct((S_ROWS, MLSTM_WIDTH), BF16),
            jax.ShapeDtypeStruct((DEC_BATCH, H, dh, dh), F32),
            jax.ShapeDtypeStruct((S_ROWS, MLSTM_WIDTH), F32),
            jax.ShapeDtypeStruct((S_ROWS, LANES), F32),
        ],
        compiler_params=_cparams(("arbitrary",)),
        name="mlstm_s",
    )(m_all, m_all, m_all, m_all, gates, n_rep, m_rep, brow, nw, state_c)


def _outproj_kernel(att_ref, mh_ref, x_ref, g1_ref, sh_ref, sc_ref, nw_ref, wa_ref, wm_ref, x1_ref, h2_ref, *,
                    per_row):
    y = _dot(att_ref[...].astype(BF16), wa_ref[...]) + _dot(mh_ref[...], wm_ref[...])
    x1 = x_ref[...] + _mod_row(g1_ref, per_row) * y
    x1_ref[...] = x1
    r = lax.rsqrt(jnp.mean(x1 * x1, axis=-1, keepdims=True) + EPS)
    h2 = (x1 * r * nw_ref[...]) * (1.0 + _mod_row(sc_ref, per_row)) + _mod_row(sh_ref, per_row)
    h2_ref[...] = h2.astype(BF16)


def _outproj(att, mh, x, mod, mod_row_block, norm_w, w_att, w_m, *, tm, per_row):
    rows = x.shape[0]
    mod_rows = tm if per_row else 8
    mod_idx = (lambda i: i) if per_row else (lambda i: mod_row_block)
    kern = functools.partial(_outproj_kernel, per_row=per_row)
    return pl.pallas_call(
        kern,
        grid=(rows // tm,),
        in_specs=[
            pl.BlockSpec((tm, ATT_WIDTH), lambda i: (i, 0)),
            pl.BlockSpec((tm, MLSTM_WIDTH), lambda i: (i, 0)),
            pl.BlockSpec((tm, D_MODEL), lambda i: (i, 0)),
            pl.BlockSpec((mod_rows, D_MODEL), lambda i: (mod_idx(i), 2)),
            pl.BlockSpec((mod_rows, D_MODEL), lambda i: (mod_idx(i), 3)),
            pl.BlockSpec((mod_rows, D_MODEL), lambda i: (mod_idx(i), 4)),
            pl.BlockSpec((1, D_MODEL), lambda i: (0, 0)),
            pl.BlockSpec((ATT_WIDTH, D_MODEL), lambda i: (0, 0)),
            pl.BlockSpec((MLSTM_WIDTH, D_MODEL), lambda i: (0, 0)),
        ],
        out_specs=[
            pl.BlockSpec((tm, D_MODEL), lambda i: (i, 0)),
            pl.BlockSpec((tm, D_MODEL), lambda i: (i, 0)),
        ],
        out_shape=[
            jax.ShapeDtypeStruct((rows, D_MODEL), F32),
            jax.ShapeDtypeStruct((rows, D_MODEL), BF16),
        ],
        compiler_params=_cparams(("arbitrary",)),
        name="outproj_s" if per_row else "outproj_p",
    )(att, mh, x, mod, mod, mod, norm_w, w_att, w_m)


def _ffn_a_kernel(hp_ref, hs_ref, wg_ref, wu_ref, ap_ref, as_ref, wg_scr, wu_scr):
    i = pl.program_id(1)

    def swiglu(h_ref, a_ref):
        h = h_ref[...]
        g = _dot(h, wg_scr[...])
        u = _dot(h, wu_scr[...])
        a_ref[...] = (g * _sigmoid(g) * u).astype(BF16)

    @pl.when(i == 0)
    def _():
        wg_scr[...] = wg_ref[...].astype(BF16)
        wu_scr[...] = wu_ref[...].astype(BF16)
        swiglu(hs_ref, as_ref)

    @pl.when(i > 0)
    def _():
        swiglu(hp_ref, ap_ref)


def _ffn_a(h2_p, h2_s, w_gate, w_up, *, tm):
    rows_p, rows_s = h2_p.shape[0], h2_s.shape[0]
    tn = 512
    ni = rows_p // tm
    tile = lambda i: jnp.maximum(i - 1, 0)
    return pl.pallas_call(
        _ffn_a_kernel,
        grid=(D_FF // tn, ni + 1),
        in_specs=[
            pl.BlockSpec((tm, D_MODEL), lambda j, i: (tile(i), 0)),
            pl.BlockSpec((rows_s, D_MODEL), lambda j, i: (0, 0)),
            pl.BlockSpec((D_MODEL, tn), lambda j, i: (0, j)),
            pl.BlockSpec((D_MODEL, tn), lambda j, i: (0, j)),
        ],
        out_specs=[
            pl.BlockSpec((tm, tn), lambda j, i: (tile(i), j)),
            pl.BlockSpec((rows_s, tn), lambda j, i: (0, j)),
        ],
        out_shape=[
            jax.ShapeDtypeStruct((rows_p, D_FF), BF16),
            jax.ShapeDtypeStruct((rows_s, D_FF), BF16),
        ],
        scratch_shapes=[pltpu.VMEM((D_MODEL, tn), BF16), pltpu.VMEM((D_MODEL, tn), BF16)],
        compiler_params=_cparams(("arbitrary", "arbitrary")),
        name="ffn_a",
    )(h2_p, h2_s, w_gate, w_up)


def _ffn_b_kernel(a_ref, wd_ref, x1_ref, g2_ref, fw_ref, y_ref, *rest, per_row, emit_w):
    kk = pl.program_id(1)

    def weight_tile():
        wt = wd_ref[...].astype(BF16)
        if emit_w:
            rest[0][...] = wt
        return wt

    @pl.when(kk == 0)
    def _():
        y_ref[...] = _dot(a_ref[...], weight_tile())

    @pl.when(kk > 0)
    def _():
        y_ref[...] += _dot(a_ref[...], weight_tile())

    @pl.when(kk == pl.num_programs(1) - 1)
    def _():
        x2 = x1_ref[...] + _mod_row(g2_ref, per_row) * y_ref[...]
        y_ref[...] = x2 * lax.rsqrt(jnp.mean(x2 * x2, axis=-1, keepdims=True) + EPS) * fw_ref[...]


def _ffn_b(a, w_down, x1, mod, mod_row_block, final_w, *, tm, tk, per_row, emit_w):
    rows = a.shape[0]
    mod_rows = tm if per_row else 8
    mod_idx = (lambda i: i) if per_row else (lambda i: mod_row_block)
    kern = functools.partial(_ffn_b_kernel, per_row=per_row, emit_w=emit_w)
    w_spec = pl.BlockSpec((tk, D_MODEL), lambda i, k: (k, 0))
    y_spec = pl.BlockSpec((tm, D_MODEL), lambda i, k: (i, 0))
    y_shape = jax.ShapeDtypeStruct((rows, D_MODEL), F32)
    return pl.pallas_call(
        kern,
        grid=(rows // tm, D_FF // tk),
        in_specs=[
            pl.BlockSpec((tm, tk), lambda i, k: (i, k)),
            w_spec,
            pl.BlockSpec((tm, D_MODEL), lambda i, k: (i, 0)),
            pl.BlockSpec((mod_rows, D_MODEL), lambda i, k: (mod_idx(i), 5)),
            pl.BlockSpec((1, D_MODEL), lambda i, k: (0, 0)),
        ],
        out_specs=[y_spec, w_spec] if emit_w else y_spec,
        out_shape=[y_shape, jax.ShapeDtypeStruct((D_FF, D_MODEL), BF16)] if emit_w else y_shape,
        compiler_params=_cparams(("arbitrary", "arbitrary")),
        name="ffn_b_s" if per_row else "ffn_b_p",
    )(a, w_down, x1, mod, final_w)


def _rope_tables(pos):
    half = ROPE_DIM // 2
    inv = np.float32(ROPE_THETA) ** (-np.arange(0, ROPE_DIM, 2, dtype=np.float32) / np.float32(ROPE_DIM))
    d = np.arange(LANES) % ATT_HEAD_DIM
    ang = pos.astype(np.float32)[:, None] * inv[d % half][None, :].astype(np.float32)
    cos, sin = np.cos(ang), np.sin(ang)
    d = d[None, :]
    tables = (np.where(d < ROPE_DIM, cos, 1.0), np.where(d < half, -sin, 0.0),
              np.where((d >= half) & (d < ROPE_DIM), sin, 0.0))
    return tuple(jnp.asarray(t.astype(np.float32)) for t in tables)


def kernel(x_prompt, x_sample, cache_k_win, cache_v_win, state_C, state_n, state_m, c_prompt, c_sample,
           norm1_w, norm2_w, final_norm_w, w_ada, b_ada, w_in, b_ig, b_fg, attn_sinks, mh_norm_w,
           w_out, w_gate, w_up, w_down):
    assert w_in.shape[0] == 1, "single-layer trunk"
    T = DEC_SEQ
    xp = x_prompt[0]
    xs = x_sample.reshape(S_ROWS, D_MODEL)

    c_all = jnp.concatenate([c_sample, c_prompt, jnp.zeros((15, D_MODEL), F32)], axis=0)
    mod = _ada(c_all, w_ada[0], b_ada)
    prompt_mod_block = S_ROWS // 8

    w_in_t = jnp.transpose(w_in[0])
    wq_t = (w_in_t[:ATT_WIDTH].reshape(ATT_KV_HEADS, ATT_GROUP, ATT_HEAD_DIM, D_MODEL)
            .transpose(1, 0, 2, 3).reshape(ATT_WIDTH, D_MODEL).astype(BF16))
    w_gates_t = w_in_t[MAIN_WIDTH:]
    wg = jnp.pad(w_gates_t, ((0, 16 - 2 * MLSTM_HEADS), (0, 0)))
    w_out_att = (w_out[0, :ATT_WIDTH].reshape(ATT_KV_HEADS, ATT_GROUP, ATT_HEAD_DIM, D_MODEL)
                 .transpose(1, 0, 2, 3).reshape(ATT_WIDTH, D_MODEL).astype(BF16))
    w_out_m = w_out[0, ATT_WIDTH:].astype(BF16)
    n1 = norm1_w.reshape(1, D_MODEL)
    n2 = norm2_w.reshape(1, D_MODEL)
    fw = final_norm_w.reshape(1, D_MODEL)
    nw = mh_norm_w.reshape(1, MLSTM_WIDTH)
    gate_bias = jnp.concatenate([b_ig[0], b_fg[0]])
    brow = jnp.pad(gate_bias, (0, LANES - 2 * MLSTM_HEADS)).reshape(1, LANES)
    bcol = jnp.broadcast_to(jnp.pad(gate_bias, (0, 16 - 2 * MLSTM_HEADS))[:, None], (16, MLSTM_CHUNK_P))

    rope_p = _rope_tables(np.arange(SEQ))
    rope_s = _rope_tables(np.tile(PAST_LEN + np.arange(T), DEC_BATCH))

    tm_p = 1024
    q_s, _, kv32_s, m_s, g_s, _, w_in_bf = _inproj(xs, mod, 0, n1, wq_t, w_in_t, wg, *rope_s,
                                                   tm=S_ROWS, per_row=True, emit_w=True)
    q_p, kv_p, kv32_p, m_p, g_p, gt_p = _inproj(xp, mod, prompt_mod_block, n1, wq_t, w_in_bf, wg, *rope_p,
                                                tm=tm_p, per_row=False, emit_w=False)

    sinks = attn_sinks[0]
    sink_col = jnp.broadcast_to(sinks.reshape(ATT_HEADS, 1, 1), (ATT_HEADS, 8, LANES)).reshape(128, LANES)
    ck = jnp.transpose(cache_k_win[0], (0, 2, 3, 1))
    cv = jnp.transpose(cache_v_win[0], (0, 2, 3, 1))
    att_s, kwin_s, vwin_s = _attn_s(sink_col, q_s, kv32_s, ck, cv)

    n_rep = jnp.repeat(state_n[0].reshape(DEC_BATCH, MLSTM_WIDTH), T, axis=0)
    m_rep = jnp.pad(jnp.repeat(state_m[0], T, axis=0), ((0, 0), (0, LANES - MLSTM_HEADS)))
    mh_s, c_s, nrow_s, mrow_s = _mlstm_s(m_s, g_s, n_rep, m_rep, brow, nw, state_C[0])

    x1_p, h2_p, c_p, n_p, mm_p = _mix_p(sinks, q_p, kv_p, m_p, g_p, gt_p, brow, bcol, nw, xp, mod, prompt_mod_block,
                                        n2, w_out_att, w_out_m)
    x1_s, h2_s = _outproj(att_s, mh_s, xs, mod, 0, n2, w_out_att, w_out_m, tm=S_ROWS // 2, per_row=True)
    a_p, a_s = _ffn_a(h2_p, h2_s, w_gate[0], w_up[0], tm=tm_p)
    y_s, w_down_bf = _ffn_b(a_s, w_down[0], x1_s, mod, 0, fw, tm=S_ROWS, tk=1408, per_row=True, emit_w=True)
    y_p = _ffn_b(a_p, w_down_bf, x1_p, mod, prompt_mod_block, fw, tm=tm_p, tk=512, per_row=False, emit_w=False)

    kv_shape = (1, 1, WINDOW, ATT_KV_HEADS, ATT_HEAD_DIM)
    kv_last = kv32_p[tm_p - WINDOW:]
    dh = MLSTM_HEAD_DIM
    return (
        y_p.reshape(1, SEQ, D_MODEL),
        y_s.reshape(DEC_BATCH, T, D_MODEL),
        kv_last[:, :KV_WIDTH].reshape(kv_shape),
        kv_last[:, KV_WIDTH:].reshape(kv_shape),
        c_p.reshape(1, 1, MLSTM_HEADS, dh, dh),
        n_p[:MLSTM_HEADS].reshape(1, 1, MLSTM_HEADS, dh),
        mm_p[:MLSTM_HEADS, 0].reshape(1, 1, MLSTM_HEADS),
        jnp.transpose(kwin_s, (0, 3, 1, 2))[None],
        jnp.transpose(vwin_s, (0, 3, 1, 2))[None],
        c_s.reshape(1, DEC_BATCH, MLSTM_HEADS, dh, dh),
        nrow_s[T - 1::T].reshape(1, DEC_BATCH, MLSTM_HEADS, dh),
        mrow_s[T - 1::T, :MLSTM_HEADS].reshape(1, DEC_BATCH, MLSTM_HEADS),
    )
```

```python
import functools

import jax
import jax.numpy as jnp
import numpy as np
from jax import lax
from jax.experimental import pallas as pl
from jax.experimental.pallas import tpu as pltpu

F32 = jnp.float32
BF16 = jnp.bfloat16

D_MODEL = 2048
SEQ = 8192
DEC_BATCH = 128
DEC_SEQ = 4
S_ROWS = DEC_BATCH * DEC_SEQ
PAST_LEN = 16384
ATT_HEADS = 16
ATT_KV_HEADS = 4
ATT_GROUP = 4
ATT_HEAD_DIM = 64
WINDOW = 128
ROPE_THETA = 500000.0
ROPE_DIM = 16
MLSTM_HEADS = 4
MLSTM_HEAD_DIM = 256
ATT_WIDTH = 1024
KV_WIDTH = 256
MLSTM_WIDTH = 1024
MAIN_WIDTH = ATT_WIDTH + 2 * KV_WIDTH + 4 * MLSTM_WIDTH
D_FF = 5632
EPS = 1e-6

LANES = 128
MLSTM_CHUNK_P = 256
VMEM_LIMIT = 56 * 1024 * 1024

NT_DIMS = (((1,), (1,)), ((), ()))
TN_DIMS = (((0,), (0,)), ((), ()))
HI = lax.Precision.HIGHEST


def _cparams(sem):
    return pltpu.CompilerParams(dimension_semantics=sem, vmem_limit_bytes=VMEM_LIMIT)


def _dot(a, b):
    return jnp.dot(a, b, preferred_element_type=F32)


def _dot_nt(a, b):
    return lax.dot_general(a, b, NT_DIMS, preferred_element_type=F32)


def _dot_tn(a, b):
    return lax.dot_general(a, b, TN_DIMS, preferred_element_type=F32)


def _sigmoid(x):
    return 1.0 / (1.0 + jnp.exp(-x))


def _log_sigmoid(x):
    return jnp.minimum(x, 0.0) - jnp.log(1.0 + jnp.exp(-jnp.abs(x)))


def _mod_row(ref, per_row):
    return ref[...] if per_row else ref[0:1, :]


def _ada_kernel(c_ref, w_ref, b_ref, o_ref, s_scr, rep_scr):
    @pl.when(pl.program_id(0) == 0)
    def _():
        c = c_ref[...]
        s_scr[...] = (c * _sigmoid(c)).astype(BF16)

    mod = _dot(s_scr[...], w_ref[...].astype(BF16)) + b_ref[...]
    for c in range(o_ref.shape[1] // LANES):
        cols = slice(LANES * c, LANES * (c + 1))
        for t in range(DEC_SEQ):
            rep_scr[pl.ds(t, DEC_BATCH, stride=DEC_SEQ), :] = mod[0:DEC_BATCH, cols]
        o_ref[0:S_ROWS, cols] = rep_scr[...]
    o_ref[S_ROWS:, :] = mod[DEC_BATCH:]


def _ada(c_all, w_ada, b_ada):
    m = c_all.shape[0]
    m_out = S_ROWS + m - DEC_BATCH
    n = w_ada.shape[1]
    tn = 1024
    return pl.pallas_call(
        _ada_kernel,
        grid=(n // tn,),
        in_specs=[
            pl.BlockSpec((m, D_MODEL), lambda j: (0, 0)),
            pl.BlockSpec((D_MODEL, tn), lambda j: (0, j)),
            pl.BlockSpec((1, tn), lambda j: (0, j)),
        ],
        out_specs=pl.BlockSpec((m_out, tn), lambda j: (0, j)),
        out_shape=jax.ShapeDtypeStruct((m_out, n), F32),
        scratch_shapes=[pltpu.VMEM((m, D_MODEL), BF16), pltpu.VMEM((S_ROWS, LANES), F32)],
        compiler_params=_cparams(("arbitrary",)),
        name="ada",
    )(c_all, w_ada, b_ada)


def _rope_store(acc, cos, sa, sb, out_ref, ncols, scale):
    for c in range(ncols // LANES):
        xc = acc[:, LANES * c:LANES * (c + 1)]
        rot = xc * cos + pltpu.roll(xc, LANES - 8, 1) * sa + pltpu.roll(xc, 8, 1) * sb
        if scale != 1.0:
            rot = rot * scale
        out_ref[:, LANES * c:LANES * (c + 1)] = rot.astype(out_ref.dtype)


def _inproj_kernel(x_ref, sh_ref, sc_ref, nw_ref, wq_ref, win_ref, wg_ref, cos_ref, sa_ref, sb_ref,
                   q_ref, kv_ref, kv32_ref, m_ref, g_ref, gt_ref, *rest, per_row, emit_w):
    h_scr = rest[-1]
    j = pl.program_id(1)

    def weight_tile():
        wt = win_ref[...].astype(BF16)
        if emit_w:
            rest[0][...] = wt
        return wt

    @pl.when(j == 0)
    def _():
        x = x_ref[...]
        r = lax.rsqrt(jnp.mean(x * x, axis=-1, keepdims=True) + EPS)
        gain = nw_ref[...] * (1.0 + _mod_row(sc_ref, per_row))
        h = (x * r * gain + _mod_row(sh_ref, per_row)).astype(BF16)
        h_scr[...] = h
        wg = jnp.concatenate([wg_ref[...].astype(BF16), jnp.zeros((LANES - 16, D_MODEL), BF16)], axis=0)
        g = _dot_nt(h, wg)
        g_ref[...] = g
        gt_ref[...] = g.T[0:16, :]

    @pl.when(j < 2)
    def _():
        acc = _dot_nt(h_scr[...], wq_ref[...])
        _rope_store(acc, cos_ref[...], sa_ref[...], sb_ref[...], q_ref, 512, ATT_HEAD_DIM ** -0.5)

    @pl.when(j == 2)
    def _():
        acc = _dot_nt(h_scr[...], weight_tile())
        _rope_store(acc, cos_ref[...], sa_ref[...], sb_ref[...], kv32_ref, KV_WIDTH, 1.0)
        kv32_ref[:, KV_WIDTH:] = acc[:, KV_WIDTH:]
        kv_ref[...] = kv32_ref[...].astype(BF16)

    @pl.when(j > 2)
    def _():
        m_ref[...] = _dot_nt(h_scr[...], weight_tile()).astype(BF16)


def _inproj(x, mod, mod_row_block, norm_w, wq_t, w_in_t, wg, cos, sa, sb, *, tm, per_row, emit_w):
    rows = x.shape[0]
    tn = 512
    nj = MAIN_WIDTH // tn
    mod_rows = tm if per_row else 8
    mod_idx = (lambda i: i) if per_row else (lambda i: mod_row_block)
    kern = functools.partial(_inproj_kernel, per_row=per_row, emit_w=emit_w)
    copy_spec = pl.BlockSpec((tn, D_MODEL), lambda i, j: (jnp.maximum(j, 2) - 2, 0))
    w_spec = pl.BlockSpec((tn, D_MODEL), lambda i, j: (jnp.maximum(j, 2), 0)) if emit_w else copy_spec
    extra_specs = [copy_spec] if emit_w else []
    extra_shapes = [jax.ShapeDtypeStruct((MAIN_WIDTH - ATT_WIDTH, D_MODEL), BF16)] if emit_w else []
    return pl.pallas_call(
        kern,
        grid=(rows // tm, nj),
        in_specs=[
            pl.BlockSpec((tm, D_MODEL), lambda i, j: (jnp.minimum(i + jnp.minimum(j, 1), rows // tm - 1), 0)),
            pl.BlockSpec((mod_rows, D_MODEL), lambda i, j: (mod_idx(i), 0)),
            pl.BlockSpec((mod_rows, D_MODEL), lambda i, j: (mod_idx(i), 1)),
            pl.BlockSpec((1, D_MODEL), lambda i, j: (0, 0)),
            pl.BlockSpec((tn, D_MODEL), lambda i, j: (jnp.minimum(j, 1), 0)),
            w_spec,
            pl.BlockSpec((16, D_MODEL), lambda i, j: (0, 0)),
            pl.BlockSpec((tm, LANES), lambda i, j: (i, 0)),
            pl.BlockSpec((tm, LANES), lambda i, j: (i, 0)),
            pl.BlockSpec((tm, LANES), lambda i, j: (i, 0)),
        ],
        out_specs=[
            pl.BlockSpec((tm, tn), lambda i, j: (i, jnp.minimum(j, 1))),
            pl.BlockSpec((tm, tn), lambda i, j: (i, 0)),
            pl.BlockSpec((tm, tn), lambda i, j: (0, 0)),
            pl.BlockSpec((tm, tn), lambda i, j: (i, jnp.clip(j - 3, 0, 7))),
            pl.BlockSpec((tm, LANES), lambda i, j: (i, 0)),
            pl.BlockSpec((16, tm), lambda i, j: (0, i)),
        ] + extra_specs,
        out_shape=[
            jax.ShapeDtypeStruct((rows, ATT_WIDTH), BF16),
            jax.ShapeDtypeStruct((rows, 2 * KV_WIDTH), BF16),
            jax.ShapeDtypeStruct((tm, 2 * KV_WIDTH), F32),
            jax.ShapeDtypeStruct((rows, 4 * MLSTM_WIDTH), BF16),
            jax.ShapeDtypeStruct((rows, LANES), F32),
            jax.ShapeDtypeStruct((16, rows), F32),
        ] + extra_shapes,
        scratch_shapes=[pltpu.VMEM((tm, D_MODEL), BF16)],
        compiler_params=_cparams(("arbitrary", "arbitrary")),
        name="inproj_s" if per_row else "inproj_p",
    )(x, mod, mod, norm_w, wq_t, w_in_t, wg, cos, sa, sb)


def _attn_block(sink_ref, q_ref, row0, kv2, allowed, store):
    w = WINDOW
    grp = ATT_GROUP
    rows = grp * w
    member = lax.broadcasted_iota(jnp.int32, (rows, 1), 0) // w
    low = lax.broadcasted_iota(jnp.int32, (2 * w, LANES), 1) < ATT_HEAD_DIM
    low_o = lax.broadcasted_iota(jnp.int32, (rows, LANES), 1) < ATT_HEAD_DIM
    key_row = lax.broadcasted_iota(jnp.int32, (4 * w, LANES), 0)
    key_lane = lax.broadcasted_iota(jnp.int32, (4 * w, LANES), 1)
    ones_bd = (((key_row < 2 * w) & (key_lane < ATT_HEAD_DIM))
               | ((key_row >= 2 * w) & (key_lane >= ATT_HEAD_DIM))).astype(BF16)
    zero = jnp.zeros((2 * w, LANES), BF16)
    for cp in range(2):
        k128 = kv2[:, LANES * cp:LANES * (cp + 1)]
        v128 = kv2[:, KV_WIDTH + LANES * cp:KV_WIDTH + LANES * (cp + 1)]
        kbd = jnp.concatenate([jnp.where(low, k128, zero), jnp.where(low, zero, k128)], axis=0)
        vbd = jnp.concatenate([jnp.where(low, v128, zero), jnp.where(low, zero, v128)], axis=0)
        v_aug = jnp.concatenate([vbd, ones_bd], axis=1)
        q4 = jnp.concatenate([q_ref[row0:row0 + w, 256 * r + LANES * cp:256 * r + LANES * (cp + 1)]
                              for r in range(grp)], axis=0)
        s = _dot_nt(q4, kbd)
        es, tails = [], []
        for half in range(2):
            sh = jnp.where(allowed, s[:, 2 * w * half:2 * w * (half + 1)], -jnp.inf)
            head0 = (2 * cp + half) * grp
            sink = jnp.full((rows, 1), sink_ref[head0], F32)
            for r in range(1, grp):
                sink = jnp.where(member == r, sink_ref[head0 + r], sink)
            m = jnp.maximum(jnp.max(sh, axis=-1, keepdims=True), sink)
            es.append(jnp.exp(sh - m).astype(BF16))
            tails.append(jnp.exp(sink - m))
        oa = _dot(jnp.concatenate(es, axis=1), v_aug)
        l = oa[:, LANES:] + jnp.where(low_o, tails[0], tails[1])
        o = (oa[:, :LANES] / l).astype(BF16)
        for r in range(grp):
            store(256 * r + LANES * cp, o[w * r:w * (r + 1)])


MIX_TM = 512


def _mix_p_kernel(sink_ref, q_ref, kvp_ref, kvc_ref, mq_ref, mk_ref, mv_ref, mo_ref, g_ref, gt_ref, brow_ref, bcol_ref,
                  mnw_ref, x_ref, g1_ref, sh_ref, sc_ref, nw_ref, wa_ref, wm_ref,
                  x1_ref, h2_ref, c_out, n_out, m_out, att_scr, mh_scr, c_scr, n_scr, m_scr):
    s = pl.program_id(0)
    w = WINDOW
    tiles = pl.num_programs(0) - 1
    slot = s % 2

    @pl.when(s == 0)
    def _():
        c_scr[...] = jnp.zeros_like(c_scr)
        n_scr[...] = jnp.zeros_like(n_scr)
        m_scr[...] = jnp.zeros_like(m_scr)

    def project():
        y = _dot(att_scr[1 - slot], wa_ref[...]) + _dot(mh_scr[1 - slot], wm_ref[...])
        x1 = x_ref[...] + g1_ref[0:1, :] * y
        x1_ref[...] = x1
        r = lax.rsqrt(jnp.mean(x1 * x1, axis=-1, keepdims=True) + EPS)
        gain = nw_ref[...] * (1.0 + sc_ref[0:1, :])
        h2_ref[...] = (x1 * r * gain + sh_ref[0:1, :]).astype(BF16)

    def mixers():
        rows = ATT_GROUP * w
        qi = lax.broadcasted_iota(jnp.int32, (rows, 2 * w), 0) % w
        kj = lax.broadcasted_iota(jnp.int32, (rows, 2 * w), 1)
        first_off = jnp.where(s > 0, 0, 4 * w)
        causal = (kj >= w) & (kj - w <= qi)
        for blk in range(MIX_TM // w):
            prev = kvp_ref[...] if blk == 0 else kvc_ref[w * (blk - 1):w * blk, :]
            kv2 = jnp.concatenate([prev, kvc_ref[w * blk:w * (blk + 1), :]], axis=0)
            allowed = ((kj < w) & (kj > qi + (first_off if blk == 0 else 0))) | causal

            def store(c0, val, blk=blk):
                att_scr[slot, w * blk:w * (blk + 1), c0:c0 + LANES] = val

            _attn_block(sink_ref, q_ref, w * blk, kv2, allowed, store)

        for ch in range(MIX_TM // MLSTM_CHUNK_P):
            r0 = MLSTM_CHUNK_P * ch

            def store_mh(cs, val, r0=r0):
                mh_scr[slot, r0:r0 + MLSTM_CHUNK_P, cs] = val

            _mlstm_chunk(mq_ref, mk_ref, mv_ref, mo_ref, g_ref, gt_ref, brow_ref, bcol_ref, mnw_ref,
                         c_scr, n_scr, m_scr, r0, store_mh)

    @pl.when(s == 0)
    def _():
        mixers()

    @pl.when((s > 0) & (s < tiles))
    def _():
        project()
        mixers()

    @pl.when(s == tiles)
    def _():
        project()
        c_out[...] = c_scr[...]
        n_out[...] = n_scr[...]
        m_out[...] = m_scr[...]


def _mix_p(sinks, q, kv, m_all, gates, gates_t, brow, bcol, mnw, x, mod, mod_row_block, norm_w, w_att, w_m):
    tm = MIX_TM
    tiles = SEQ // tm
    bpt = tm // WINDOW
    dh = MLSTM_HEAD_DIM
    att_tile = lambda s: jnp.minimum(s, tiles - 1)
    out_tile = lambda s: jnp.maximum(s - 1, 0)
    m_spec = lambda col: pl.BlockSpec((tm, MLSTM_WIDTH), lambda s: (att_tile(s), col))
    return pl.pallas_call(
        _mix_p_kernel,
        grid=(tiles + 1,),
        in_specs=[
            pl.BlockSpec(memory_space=pltpu.SMEM),
            pl.BlockSpec((tm, ATT_WIDTH), lambda s: (att_tile(s), 0)),
            pl.BlockSpec((WINDOW, 2 * KV_WIDTH), lambda s: (jnp.maximum(bpt * att_tile(s) - 1, 0), 0)),
            pl.BlockSpec((tm, 2 * KV_WIDTH), lambda s: (att_tile(s), 0)),
            m_spec(0), m_spec(1), m_spec(2), m_spec(3),
            pl.BlockSpec((tm, LANES), lambda s: (att_tile(s), 0)),
            pl.BlockSpec((16, tm), lambda s: (0, att_tile(s))),
            pl.BlockSpec((1, LANES), lambda s: (0, 0)),
            pl.BlockSpec((16, MLSTM_CHUNK_P), lambda s: (0, 0)),
            pl.BlockSpec((1, MLSTM_WIDTH), lambda s: (0, 0)),
            pl.BlockSpec((tm, D_MODEL), lambda s: (out_tile(s), 0)),
            pl.BlockSpec((8, D_MODEL), lambda s: (mod_row_block, 2)),
            pl.BlockSpec((8, D_MODEL), lambda s: (mod_row_block, 3)),
            pl.BlockSpec((8, D_MODEL), lambda s: (mod_row_block, 4)),
            pl.BlockSpec((1, D_MODEL), lambda s: (0, 0)),
            pl.BlockSpec((ATT_WIDTH, D_MODEL), lambda s: (0, 0)),
            pl.BlockSpec((MLSTM_WIDTH, D_MODEL), lambda s: (0, 0)),
        ],
        out_specs=[
            pl.BlockSpec((tm, D_MODEL), lambda s: (out_tile(s), 0)),
            pl.BlockSpec((tm, D_MODEL), lambda s: (out_tile(s), 0)),
            pl.BlockSpec((MLSTM_HEADS, dh, dh), lambda s: (0, 0, 0)),
            pl.BlockSpec((8, dh), lambda s: (0, 0)),
            pl.BlockSpec((8, LANES), lambda s: (0, 0)),
        ],
        out_shape=[
            jax.ShapeDtypeStruct((SEQ, D_MODEL), F32),
            jax.ShapeDtypeStruct((SEQ, D_MODEL), BF16),
            jax.ShapeDtypeStruct((MLSTM_HEADS, dh, dh), F32),
            jax.ShapeDtypeStruct((8, dh), F32),
            jax.ShapeDtypeStruct((8, LANES), F32),
        ],
        scratch_shapes=[
            pltpu.VMEM((2, tm, ATT_WIDTH), BF16),
            pltpu.VMEM((2, tm, MLSTM_WIDTH), BF16),
            pltpu.VMEM((MLSTM_HEADS, dh, dh), F32),
            pltpu.VMEM((8, dh), F32),
            pltpu.VMEM((8, LANES), F32),
        ],
        compiler_params=_cparams(("arbitrary",)),
        name="mix_p",
    )(sinks, q, kv, kv, m_all, m_all, m_all, m_all, gates, gates_t, brow, bcol, mnw, x, mod, mod, mod, norm_w,
      w_att, w_m)


ATT_S_BB = 16


def _attn_s_kernel(sink_ref, q_ref, kv32_ref, ck_ref, cv_ref, o_ref, ko_ref, vo_ref, q32_scr):
    t_new = DEC_SEQ
    w = WINDOW
    q32_scr[...] = q_ref[...].astype(F32)
    rows = 4 * 4 * 8
    row = lax.broadcasted_iota(jnp.int32, (rows, w), 0)
    slot = lax.broadcasted_iota(jnp.int32, (rows, w), 1)
    t_row = row % t_new
    second = (row % 8) >= t_new
    win_ok = (slot < w - t_new) | (slot - (w - t_new) <= t_row)
    old_ok = (slot >= 1) & (slot < t_new) & (slot > t_row)
    lane256 = lax.broadcasted_iota(jnp.int32, (32, 2 * LANES), 1)
    sink = sink_ref[...][:, 0:1]
    kv_new = jnp.concatenate([kv32_ref[...], jnp.zeros((w - ATT_S_BB * t_new, 2 * KV_WIDTH), F32)], axis=0)
    kv_t = kv_new.T
    new_slot = lax.broadcasted_iota(jnp.int32, (KV_WIDTH, w), 1) >= w - t_new
    for b in range(ATT_S_BB):
        cols = pltpu.roll(kv_t, w - t_new - t_new * b, 1)
        k_shift = pltpu.roll(ck_ref[b].reshape(KV_WIDTH, w), w - t_new, 1)
        v_shift = pltpu.roll(cv_ref[b].reshape(KV_WIDTH, w), w - t_new, 1)
        ko_ref[b] = jnp.where(new_slot, cols[:KV_WIDTH], k_shift).reshape(ATT_KV_HEADS, ATT_HEAD_DIM, w)
        vo_ref[b] = jnp.where(new_slot, cols[KV_WIDTH:], v_shift).reshape(ATT_KV_HEADS, ATT_HEAD_DIM, w)
    for pair in range(ATT_S_BB // 2):
        b0, b1 = 2 * pair, 2 * pair + 1
        q32 = jnp.concatenate([q32_scr[8 * pair:8 * (pair + 1), 256 * r:256 * (r + 1)] for r in range(ATT_GROUP)],
                              axis=0)
        qpad = jnp.concatenate(
            [jnp.where((lane256 // ATT_HEAD_DIM) == g, q32, 0.0) for g in range(ATT_KV_HEADS)], axis=0).astype(BF16)
        kw = [ko_ref[b].reshape(KV_WIDTH, w).astype(BF16) for b in (b0, b1)]
        vw = [vo_ref[b].reshape(KV_WIDTH, w).astype(BF16) for b in (b0, b1)]
        kc = [ck_ref[b].reshape(KV_WIDTH, w).astype(BF16) for b in (b0, b1)]
        vc = [cv_ref[b].reshape(KV_WIDTH, w).astype(BF16) for b in (b0, b1)]
        s_w = jnp.where(second, _dot(qpad, kw[1]), _dot(qpad, kw[0]))
        s_c = jnp.where(second, _dot(qpad, kc[1]), _dot(qpad, kc[0]))
        s_w = jnp.where(win_ok, s_w, -jnp.inf)
        s_c = jnp.where(old_ok, s_c, -jnp.inf)
        m = jnp.maximum(jnp.maximum(jnp.max(s_w, axis=-1, keepdims=True), jnp.max(s_c, axis=-1, keepdims=True)), sink)
        e_w = jnp.exp(s_w - m)
        e_c = jnp.exp(s_c - m)
        l = jnp.sum(e_w, axis=-1, keepdims=True) + jnp.sum(e_c, axis=-1, keepdims=True) + jnp.exp(sink - m)
        p_w = e_w / l
        p_c = e_c / l
        zero = jnp.zeros_like(p_w)
        o = (_dot_nt(jnp.where(second, zero, p_w).astype(BF16), vw[0])
             + _dot_nt(jnp.where(second, p_w, zero).astype(BF16), vw[1])
             + _dot_nt(jnp.where(second, zero, p_c).astype(BF16), vc[0])
             + _dot_nt(jnp.where(second, p_c, zero).astype(BF16), vc[1]))
        o32 = jnp.zeros((32, 2 * LANES), F32)
        for g in range(ATT_KV_HEADS):
            o32 = jnp.where((lane256 // ATT_HEAD_DIM) == g, o[32 * g:32 * (g + 1), :], o32)
        for r in range(ATT_GROUP):
            o_ref[8 * pair:8 * (pair + 1), 256 * r:256 * (r + 1)] = o32[8 * r:8 * (r + 1), :]


def _attn_s(sink_col, q, kv32, ck, cv):
    bb = ATT_S_BB
    rows = bb * DEC_SEQ
    cache_block = (bb, ATT_KV_HEADS, ATT_HEAD_DIM, WINDOW)
    cache_shape = (DEC_BATCH, ATT_KV_HEADS, ATT_HEAD_DIM, WINDOW)
    return pl.pallas_call(
        _attn_s_kernel,
        grid=(DEC_BATCH // bb,),
        in_specs=[
            pl.BlockSpec((128, LANES), lambda i: (0, 0)),
            pl.BlockSpec((rows, ATT_WIDTH), lambda i: (i, 0)),
            pl.BlockSpec((rows, 2 * KV_WIDTH), lambda i: (i, 0)),
            pl.BlockSpec(cache_block, lambda i: (i, 0, 0, 0)),
            pl.BlockSpec(cache_block, lambda i: (i, 0, 0, 0)),
        ],
        out_specs=[
            pl.BlockSpec((rows, ATT_WIDTH), lambda i: (i, 0)),
            pl.BlockSpec(cache_block, lambda i: (i, 0, 0, 0)),
            pl.BlockSpec(cache_block, lambda i: (i, 0, 0, 0)),
        ],
        out_shape=[
            jax.ShapeDtypeStruct((S_ROWS, ATT_WIDTH), F32),
            jax.ShapeDtypeStruct(cache_shape, F32),
            jax.ShapeDtypeStruct(cache_shape, F32),
        ],
        scratch_shapes=[pltpu.VMEM((rows, ATT_WIDTH), F32)],
        compiler_params=_cparams(("arbitrary",)),
        name="attn_s",
    )(sink_col, q, kv32, ck, cv)


def _head_norm_gate(h, nw, mo):
    hn = h * lax.rsqrt(jnp.mean(h * h, axis=-1, keepdims=True) + EPS) * nw
    return hn * _sigmoid(mo.astype(F32))


def _mlstm_chunk(q_ref, k_ref, v_ref, mo_ref, g_ref, gt_ref, brow_ref, bcol_ref, nw_ref, c_scr, n_scr, m_scr,
                 r0, store):
    L = MLSTM_CHUNK_P
    dh = MLSTM_HEAD_DIM
    rs = slice(r0, r0 + L)
    ti = lax.broadcasted_iota(jnp.int32, (L, L), 0)
    si = lax.broadcasted_iota(jnp.int32, (L, L), 1)
    causal = si <= ti
    tri = causal.astype(F32)
    tri_t = (ti <= si).astype(F32)
    gates = g_ref[rs, :] + brow_ref[...]
    gates_t = gt_ref[:, rs] + bcol_ref[...]
    b_col = jnp.dot(tri, _log_sigmoid(gates), precision=HI, preferred_element_type=F32)
    b_row = jnp.dot(_log_sigmoid(gates_t), tri_t, precision=HI, preferred_element_type=F32)
    for hd in range(MLSTM_HEADS):
        cs = slice(dh * hd, dh * (hd + 1))
        b_c = b_col[:, MLSTM_HEADS + hd:MLSTM_HEADS + hd + 1]
        li_c = gates[:, hd:hd + 1]
        b_r = b_row[MLSTM_HEADS + hd:MLSTM_HEADS + hd + 1, :]
        li_r = gates_t[hd:hd + 1, :]
        dm = jnp.where(causal, b_c - b_r + li_r, -jnp.inf)
        m_prev = m_scr[hd:hd + 1, 0:1]
        m_inter = b_c + m_prev
        m_t = jnp.maximum(m_inter, jnp.max(dm, axis=-1, keepdims=True))
        q = q_ref[rs, cs] * (dh ** -0.5)
        k = k_ref[rs, cs]
        v = v_ref[rs, cs]
        sm = _dot_nt(q, k) * jnp.exp(dm - m_t)
        a = jnp.exp(m_inter - m_t)
        c_old = c_scr[hd]
        n_old = n_scr[hd:hd + 1, :]
        num = a * _dot_nt(q, c_old.astype(BF16)) + _dot(sm.astype(BF16), v)
        qn = jnp.sum(q.astype(F32) * n_old, axis=-1, keepdims=True)
        den = a * qn + jnp.sum(sm, axis=-1, keepdims=True)
        h = num / jnp.maximum(jnp.abs(den), jnp.exp(-m_t))
        store(cs, _head_norm_gate(h, nw_ref[:, cs], mo_ref[rs, cs]).astype(BF16))
        m_new = m_t[L - 1:L, :]
        b_last = b_c[L - 1:L, :]
        g = jnp.exp(b_last - b_c + li_c - m_new)
        decay = jnp.exp(b_last + m_prev - m_new)
        gv = (g * v.astype(F32)).astype(BF16)
        c_scr[hd] = decay * c_old + _dot_tn(gv, k)
        n_scr[hd:hd + 1, :] = decay * n_old + jnp.sum(g * k.astype(F32), axis=0, keepdims=True)
        m_scr[hd:hd + 1, :] = jnp.broadcast_to(m_new, (1, LANES))


MLSTM_S_BB = 8


def _mlstm_s_kernel(q_ref, k_ref, v_ref, mo_ref, g_ref, nrep_ref, mrep_ref, brow_ref, nw_ref, c_ref,
                    mh_ref, c_out, nrow_ref, mrow_ref):
    T = DEC_SEQ
    R = MLSTM_S_BB * T
    H = MLSTM_HEADS
    dh = MLSTM_HEAD_DIM

    def shift(x, d):
        return pltpu.roll(x, d, 0)

    lanes = lax.broadcasted_iota(jnp.int32, (R, LANES), 1)
    tmod = lax.broadcasted_iota(jnp.int32, (R, LANES), 0) % T
    tmod_w = lax.broadcasted_iota(jnp.int32, (R, MLSTM_WIDTH), 0) % T
    head_ok = lanes < H
    gates = g_ref[...] + brow_ref[...]
    li = jnp.where(head_ok, gates, 0.0)
    lf = jnp.where(head_ok, pltpu.roll(_log_sigmoid(gates), LANES - H, 1), 0.0)
    bcum = lf
    for d in range(1, T):
        bcum = bcum + jnp.where(tmod >= d, shift(lf, d), 0.0)
    m0 = mrep_ref[...]
    m_inter = bcum + m0
    dms = [li] + [jnp.where(tmod >= d, bcum - shift(bcum, d) + shift(li, d), -jnp.inf) for d in range(1, T)]
    m_t = m_inter
    for dm in dms:
        m_t = jnp.maximum(m_t, dm)
    a = jnp.exp(m_inter - m_t)
    ws = [jnp.exp(dm - m_t) for dm in dms]

    q_bf = q_ref[...] * (dh ** -0.5)
    q = q_bf.astype(F32)
    k = k_ref[...].astype(F32)
    v = v_ref[...].astype(F32)
    seg = (lax.broadcasted_iota(jnp.int32, (MLSTM_WIDTH, LANES), 0) // dh
           == lax.broadcasted_iota(jnp.int32, (MLSTM_WIDTH, LANES), 1)).astype(F32)
    ex = (lax.broadcasted_iota(jnp.int32, (LANES, MLSTM_WIDTH), 1) // dh
          == lax.broadcasted_iota(jnp.int32, (LANES, MLSTM_WIDTH), 0)).astype(F32)

    def segsum(x):
        return jnp.dot(x, seg, precision=HI, preferred_element_type=F32)

    def expand(x):
        return jnp.dot(x, ex, precision=HI, preferred_element_type=F32)

    ks = [k] + [shift(k, d) for d in range(1, T)]
    vs = [v] + [shift(v, d) for d in range(1, T)]
    sms = [segsum(q * ks[d]) * ws[d] for d in range(T)]
    den = a * segsum(q * nrep_ref[...])
    for sm in sms:
        den = den + sm
    inv = 1.0 / jnp.maximum(jnp.abs(den), jnp.exp(-m_t))

    def last(x):
        out = jnp.zeros_like(x)
        for jj in range(T):
            out = jnp.where(tmod == T - 1 - jj, x if jj == 0 else pltpu.roll(x, R - jj, 0), out)
        return out

    m_new = last(m_t)
    b_last = last(bcum)
    g = jnp.where(head_ok, jnp.exp(b_last - bcum + li - m_new), 0.0)
    decay = jnp.where(head_ok, jnp.exp(b_last + m0 - m_new), 0.0)
    a_f = expand(a * inv)
    w_f = [expand(sm * inv) for sm in sms]
    g_f = expand(g)
    d_f = expand(decay)
    gv = (g_f * v).astype(BF16)
    rowb = lax.broadcasted_iota(jnp.int32, (R, dh), 0) // T
    for hd in range(H):
        cs = slice(dh * hd, dh * (hd + 1))
        qh = q_bf[:, cs]
        kh = k_ref[:, cs]
        gvh = gv[:, cs]
        qc = jnp.zeros((R, dh), F32)
        for bb in range(MLSTM_S_BB):
            c_old = c_ref[bb, hd]
            qc = jnp.where(rowb == bb, _dot_nt(qh, c_old.astype(BF16)), qc)
            upd = _dot_tn(jnp.where(rowb == bb, gvh, jnp.zeros_like(gvh)), kh)
            c_out[bb, hd] = d_f[T * bb:T * bb + 1, cs] * c_old + upd
        h = a_f[:, cs] * qc
        for d in range(T):
            h = h + w_f[d][:, cs] * vs[d][:, cs]
        mh_ref[:, cs] = _head_norm_gate(h, nw_ref[:, cs], mo_ref[:, cs]).astype(BF16)
    gk = g_f * k
    nsum = gk
    for d in range(1, T):
        nsum = nsum + jnp.where(tmod_w >= d, shift(gk, d), 0.0)
    nrow_ref[...] = d_f * nrep_ref[...] + nsum
    mrow_ref[...] = m_t


def _mlstm_s(m_all, gates, n_rep, m_rep, brow, nw, state_c):
    bb = MLSTM_S_BB
    R = bb * DEC_SEQ
    dh = MLSTM_HEAD_DIM
    H = MLSTM_HEADS
    return pl.pallas_call(
        _mlstm_s_kernel,
        grid=(DEC_BATCH // bb,),
        in_specs=[
            pl.BlockSpec((R, MLSTM_WIDTH), lambda i: (i, 0)),
            pl.BlockSpec((R, MLSTM_WIDTH), lambda i: (i, 1)),
            pl.BlockSpec((R, MLSTM_WIDTH), lambda i: (i, 2)),
            pl.BlockSpec((R, MLSTM_WIDTH), lambda i: (i, 3)),
            pl.BlockSpec((R, LANES), lambda i: (i, 0)),
            pl.BlockSpec((R, MLSTM_WIDTH), lambda i: (i, 0)),
            pl.BlockSpec((R, LANES), lambda i: (i, 0)),
            pl.BlockSpec((1, LANES), lambda i: (0, 0)),
            pl.BlockSpec((1, MLSTM_WIDTH), lambda i: (0, 0)),
            pl.BlockSpec((bb, H, dh, dh), lambda i: (i, 0, 0, 0)),
        ],
        out_specs=[
            pl.BlockSpec((R, MLSTM_WIDTH), lambda i: (i, 0)),
            pl.BlockSpec((bb, H, dh, dh), lambda i: (i, 0, 0, 0)),
            pl.BlockSpec((R, MLSTM_WIDTH), lambda i: (i, 0)),
            pl.BlockSpec((R, LANES), lambda i: (i, 0)),
        ],
        out_shape=[
            jax.ShapeDtypeStruct((S_ROWS, MLSTM_WIDTH), BF16),
            jax.ShapeDtypeStruct((DEC_BATCH, H, dh, dh), F32),
            jax.ShapeDtypeStruct((S_ROWS, MLSTM_WIDTH), F32),
            jax.ShapeDtypeStruct((S_ROWS, LANES), F32),
        ],
        compiler_params=_cparams(("arbitrary",)),
        name="mlstm_s",
    )(m_all, m_all, m_all, m_all, gates, n_rep, m_rep, brow, nw, state_c)


def _outproj_kernel(att_ref, mh_ref, x_ref, g1_ref, sh_ref, sc_ref, nw_ref, wa_ref, wm_ref, x1_ref, h2_ref, *,
                    per_row):
    y = _dot(att_ref[...].astype(BF16), wa_ref[...]) + _dot(mh_ref[...], wm_ref[...])
    x1 = x_ref[...] + _mod_row(g1_ref, per_row) * y
    x1_ref[...] = x1
    r = lax.rsqrt(jnp.mean(x1 * x1, axis=-1, keepdims=True) + EPS)
    h2 = (x1 * r * nw_ref[...]) * (1.0 + _mod_row(sc_ref, per_row)) + _mod_row(sh_ref, per_row)
    h2_ref[...] = h2.astype(BF16)


def _outproj(att, mh, x, mod, mod_row_block, norm_w, w_att, w_m, *, tm, per_row):
    rows = x.shape[0]
    mod_rows = tm if per_row else 8
    mod_idx = (lambda i: i) if per_row else (lambda i: mod_row_block)
    kern = functools.partial(_outproj_kernel, per_row=per_row)
    return pl.pallas_call(
        kern,
        grid=(rows // tm,),
        in_specs=[
            pl.BlockSpec((tm, ATT_WIDTH), lambda i: (i, 0)),
            pl.BlockSpec((tm, MLSTM_WIDTH), lambda i: (i, 0)),
            pl.BlockSpec((tm, D_MODEL), lambda i: (i, 0)),
            pl.BlockSpec((mod_rows, D_MODEL), lambda i: (mod_idx(i), 2)),
            pl.BlockSpec((mod_rows, D_MODEL), lambda i: (mod_idx(i), 3)),
            pl.BlockSpec((mod_rows, D_MODEL), lambda i: (mod_idx(i), 4)),
            pl.BlockSpec((1, D_MODEL), lambda i: (0, 0)),
            pl.BlockSpec((ATT_WIDTH, D_MODEL), lambda i: (0, 0)),
            pl.BlockSpec((MLSTM_WIDTH, D_MODEL), lambda i: (0, 0)),
        ],
        out_specs=[
            pl.BlockSpec((tm, D_MODEL), lambda i: (i, 0)),
            pl.BlockSpec((tm, D_MODEL), lambda i: (i, 0)),
        ],
        out_shape=[
            jax.ShapeDtypeStruct((rows, D_MODEL), F32),
            jax.ShapeDtypeStruct((rows, D_MODEL), BF16),
        ],
        compiler_params=_cparams(("arbitrary",)),
        name="outproj_s" if per_row else "outproj_p",
    )(att, mh, x, mod, mod, mod, norm_w, w_att, w_m)


def _ffn_a_kernel(hp_ref, hs_ref, wg_ref, wu_ref, ap_ref, as_ref, wg_scr, wu_scr):
    i = pl.program_id(1)

    def swiglu(h_ref, a_ref):
        h = h_ref[...]
        g = _dot(h, wg_scr[...])
        u = _dot(h, wu_scr[...])
        a_ref[...] = (g * _sigmoid(g) * u).astype(BF16)

    @pl.when(i == 0)
    def _():
        wg_scr[...] = wg_ref[...].astype(BF16)
        wu_scr[...] = wu_ref[...].astype(BF16)
        swiglu(hs_ref, as_ref)

    @pl.when(i > 0)
    def _():
        swiglu(hp_ref, ap_ref)


def _ffn_a(h2_p, h2_s, w_gate, w_up, *, tm):
    rows_p, rows_s = h2_p.shape[0], h2_s.shape[0]
    tn = 512
    ni = rows_p // tm
    tile = lambda i: jnp.maximum(i - 1, 0)
    return pl.pallas_call(
        _ffn_a_kernel,
        grid=(D_FF // tn, ni + 1),
        in_specs=[
            pl.BlockSpec((tm, D_MODEL), lambda j, i: (tile(i), 0)),
            pl.BlockSpec((rows_s, D_MODEL), lambda j, i: (0, 0)),
            pl.BlockSpec((D_MODEL, tn), lambda j, i: (0, j)),
            pl.BlockSpec((D_MODEL, tn), lambda j, i: (0, j)),
        ],
        out_specs=[
            pl.BlockSpec((tm, tn), lambda j, i: (tile(i), j)),
            pl.BlockSpec((rows_s, tn), lambda j, i: (0, j)),
        ],
        out_shape=[
            jax.ShapeDtypeStruct((rows_p, D_FF), BF16),
            jax.ShapeDtypeStruct((rows_s, D_FF), BF16),
        ],
        scratch_shapes=[pltpu.VMEM((D_MODEL, tn), BF16), pltpu.VMEM((D_MODEL, tn), BF16)],
        compiler_params=_cparams(("arbitrary", "arbitrary")),
        name="ffn_a",
    )(h2_p, h2_s, w_gate, w_up)


def _ffn_b_kernel(a_ref, wd_ref, x1_ref, g2_ref, fw_ref, y_ref, *rest, per_row, emit_w):
    kk = pl.program_id(1)

    def weight_tile():
        wt = wd_ref[...].astype(BF16)
        if emit_w:
            rest[0][...] = wt
        return wt

    @pl.when(kk == 0)
    def _():
        y_ref[...] = _dot(a_ref[...], weight_tile())

    @pl.when(kk > 0)
    def _():
        y_ref[...] += _dot(a_ref[...], weight_tile())

    @pl.when(kk == pl.num_programs(1) - 1)
    def _():
        x2 = x1_ref[...] + _mod_row(g2_ref, per_row) * y_ref[...]
        y_ref[...] = x2 * lax.rsqrt(jnp.mean(x2 * x2, axis=-1, keepdims=True) + EPS) * fw_ref[...]


def _ffn_b(a, w_down, x1, mod, mod_row_block, final_w, *, tm, tk, per_row, emit_w):
    rows = a.shape[0]
    mod_rows = tm if per_row else 8
    mod_idx = (lambda i: i) if per_row else (lambda i: mod_row_block)
    kern = functools.partial(_ffn_b_kernel, per_row=per_row, emit_w=emit_w)
    w_spec = pl.BlockSpec((tk, D_MODEL), lambda i, k: (k, 0))
    y_spec = pl.BlockSpec((tm, D_MODEL), lambda i, k: (i, 0))
    y_shape = jax.ShapeDtypeStruct((rows, D_MODEL), F32)
    return pl.pallas_call(
        kern,
        grid=(rows // tm, D_FF // tk),
        in_specs=[
            pl.BlockSpec((tm, tk), lambda i, k: (i, k)),
            w_spec,
            pl.BlockSpec((tm, D_MODEL), lambda i, k: (i, 0)),
            pl.BlockSpec((mod_rows, D_MODEL), lambda i, k: (mod_idx(i), 5)),
            pl.BlockSpec((1, D_MODEL), lambda i, k: (0, 0)),
        ],
        out_specs=[y_spec, w_spec] if emit_w else y_spec,
        out_shape=[y_shape, jax.ShapeDtypeStruct((D_FF, D_MODEL), BF16)] if emit_w else y_shape,
        compiler_params=_cparams(("arbitrary", "arbitrary")),
        name="ffn_b_s" if per_row else "ffn_b_p",
    )(a, w_down, x1, mod, final_w)


def _rope_tables(pos):
    half = ROPE_DIM // 2
    inv = np.float32(ROPE_THETA) ** (-np.arange(0, ROPE_DIM, 2, dtype=np.float32) / np.float32(ROPE_DIM))
    d = np.arange(LANES) % ATT_HEAD_DIM
    ang = pos.astype(np.float32)[:, None] * inv[d % half][None, :].astype(np.float32)
    cos, sin = np.cos(ang), np.sin(ang)
    d = d[None, :]
    tables = (np.where(d < ROPE_DIM, cos, 1.0), np.where(d < half, -sin, 0.0),
              np.where((d >= half) & (d < ROPE_DIM), sin, 0.0))
    return tuple(jnp.asarray(t.astype(np.float32)) for t in tables)


def kernel(x_prompt, x_sample, cache_k_win, cache_v_win, state_C, state_n, state_m, c_prompt, c_sample,
           norm1_w, norm2_w, final_norm_w, w_ada, b_ada, w_in, b_ig, b_fg, attn_sinks, mh_norm_w,
           w_out, w_gate, w_up, w_down):
    assert w_in.shape[0] == 1, "single-layer trunk"
    T = DEC_SEQ
    xp = x_prompt[0]
    xs = x_sample.reshape(S_ROWS, D_MODEL)

    c_all = jnp.concatenate([c_sample, c_prompt, jnp.zeros((15, D_MODEL), F32)], axis=0)
    mod = _ada(c_all, w_ada[0], b_ada)
    prompt_mod_block = S_ROWS // 8

    w_in_t = jnp.transpose(w_in[0])
    wq_t = (w_in_t[:ATT_WIDTH].reshape(ATT_KV_HEADS, ATT_GROUP, ATT_HEAD_DIM, D_MODEL)
            .transpose(1, 0, 2, 3).reshape(ATT_WIDTH, D_MODEL).astype(BF16))
    w_gates_t = w_in_t[MAIN_WIDTH:]
    wg = jnp.pad(w_gates_t, ((0, 16 - 2 * MLSTM_HEADS), (0, 0)))
    w_out_att = (w_out[0, :ATT_WIDTH].reshape(ATT_KV_HEADS, ATT_GROUP, ATT_HEAD_DIM, D_MODEL)
                 .transpose(1, 0, 2, 3).reshape(ATT_WIDTH, D_MODEL).astype(BF16))
    w_out_m = w_out[0, ATT_WIDTH:].astype(BF16)
    n1 = norm1_w.reshape(1, D_MODEL)
    n2 = norm2_w.reshape(1, D_MODEL)
    fw = final_norm_w.reshape(1, D_MODEL)
    nw = mh_norm_w.reshape(1, MLSTM_WIDTH)
    gate_bias = jnp.concatenate([b_ig[0], b_fg[0]])
    brow = jnp.pad(gate_bias, (0, LANES - 2 * MLSTM_HEADS)).reshape(1, LANES)
    bcol = jnp.broadcast_to(jnp.pad(gate_bias, (0, 16 - 2 * MLSTM_HEADS))[:, None], (16, MLSTM_CHUNK_P))

    rope_p = _rope_tables(np.arange(SEQ))
    rope_s = _rope_tables(np.tile(PAST_LEN + np.arange(T), DEC_BATCH))

    tm_p = 1024
    q_s, _, kv32_s, m_s, g_s, _, w_in_bf = _inproj(xs, mod, 0, n1, wq_t, w_in_t, wg, *rope_s,
                                                   tm=S_ROWS, per_row=True, emit_w=True)
    q_p, kv_p, kv32_p, m_p, g_p, gt_p = _inproj(xp, mod, prompt_mod_block, n1, wq_t, w_in_bf, wg, *rope_p,
                                                tm=tm_p, per_row=False, emit_w=False)

    sinks = attn_sinks[0]
    sink_col = jnp.broadcast_to(sinks.reshape(ATT_HEADS, 1, 1), (ATT_HEADS, 8, LANES)).reshape(128, LANES)
    ck = jnp.transpose(cache_k_win[0], (0, 2, 3, 1))
    cv = jnp.transpose(cache_v_win[0], (0, 2, 3, 1))
    att_s, kwin_s, vwin_s = _attn_s(sink_col, q_s, kv32_s, ck, cv)

    n_rep = jnp.repeat(state_n[0].reshape(DEC_BATCH, MLSTM_WIDTH), T, axis=0)
    m_rep = jnp.pad(jnp.repeat(state_m[0], T, axis=0), ((0, 0), (0, LANES - MLSTM_HEADS)))
    mh_s, c_s, nrow_s, mrow_s = _mlstm_s(m_s, g_s, n_rep, m_rep, brow, nw, state_C[0])

    x1_p, h2_p, c_p, n_p, mm_p = _mix_p(sinks, q_p, kv_p, m_p, g_p, gt_p, brow, bcol, nw, xp, mod, prompt_mod_block,
                                        n2, w_out_att, w_out_m)
    x1_s, h2_s = _outproj(att_s, mh_s, xs, mod, 0, n2, w_out_att, w_out_m, tm=S_ROWS // 2, per_row=True)
    a_p, a_s = _ffn_a(h2_p, h2_s, w_gate[0], w_up[0], tm=tm_p)
    y_s, w_down_bf = _ffn_b(a_s, w_down[0], x1_s, mod, 0, fw, tm=S_ROWS, tk=1408, per_row=True, emit_w=True)
    y_p = _ffn_b(a_p, w_down_bf, x1_p, mod, prompt_mod_block, fw, tm=tm_p, tk=512, per_row=False, emit_w=False)

    kv_shape = (1, 1, WINDOW, ATT_KV_HEADS, ATT_HEAD_DIM)
    kv_last = kv32_p[tm_p - WINDOW:]
    dh = MLSTM_HEAD_DIM
    return (
        y_p.reshape(1, SEQ, D_MODEL),
        y_s.reshape(DEC_BATCH, T, D_MODEL),
        kv_last[:, :KV_WIDTH].reshape(kv_shape),
        kv_last[:, KV_WIDTH:].reshape(kv_shape),
        c_p.reshape(1, 1, MLSTM_HEADS, dh, dh),
        n_p[:MLSTM_HEADS].reshape(1, 1, MLSTM_HEADS, dh),
        mm_p[:MLSTM_HEADS, 0].reshape(1, 1, MLSTM_HEADS),
        jnp.transpose(kwin_s, (0, 3, 1, 2))[None],
        jnp.transpose(vwin_s, (0, 3, 1, 2))[None],
        c_s.reshape(1, DEC_BATCH, MLSTM_HEADS, dh, dh),
        nrow_s[T - 1::T].reshape(1, DEC_BATCH, MLSTM_HEADS, dh),
        mrow_s[T - 1::T, :MLSTM_HEADS].reshape(1, DEC_BATCH, MLSTM_HEADS),
    )
```

```python
import functools

import jax
import jax.numpy as jnp
import numpy as np
from jax import lax
from jax.experimental import pallas as pl
from jax.experimental.pallas import tpu as pltpu

F32 = jnp.float32
BF16 = jnp.bfloat16

D_MODEL = 2048
SEQ = 8192
DEC_BATCH = 128
DEC_SEQ = 4
S_ROWS = DEC_BATCH * DEC_SEQ
PAST_LEN = 16384
ATT_HEADS = 16
ATT_KV_HEADS = 4
ATT_GROUP = 4
ATT_HEAD_DIM = 64
WINDOW = 128
ROPE_THETA = 500000.0
ROPE_DIM = 16
MLSTM_HEADS = 4
MLSTM_HEAD_DIM = 256
ATT_WIDTH = 1024
KV_WIDTH = 256
MLSTM_WIDTH = 1024
MAIN_WIDTH = ATT_WIDTH + 2 * KV_WIDTH + 4 * MLSTM_WIDTH
D_FF = 5632
EPS = 1e-6

LANES = 128
MLSTM_CHUNK_P = 256
VMEM_LIMIT = 56 * 1024 * 1024

NT_DIMS = (((1,), (1,)), ((), ()))
TN_DIMS = (((0,), (0,)), ((), ()))
HI = lax.Precision.HIGHEST


def _cparams(sem):
    return pltpu.CompilerParams(dimension_semantics=sem, vmem_limit_bytes=VMEM_LIMIT)


def _dot(a, b):
    return jnp.dot(a, b, preferred_element_type=F32)


def _dot_nt(a, b):
    return lax.dot_general(a, b, NT_DIMS, preferred_element_type=F32)


def _dot_tn(a, b):
    return lax.dot_general(a, b, TN_DIMS, preferred_element_type=F32)


def _sigmoid(x):
    return 1.0 / (1.0 + jnp.exp(-x))


def _log_sigmoid(x):
    return jnp.minimum(x, 0.0) - jnp.log(1.0 + jnp.exp(-jnp.abs(x)))


def _mod_row(ref, per_row):
    return ref[...] if per_row else ref[0:1, :]


def _ada_kernel(c_ref, w_ref, b_ref, o_ref, s_scr, rep_scr):
    @pl.when(pl.program_id(0) == 0)
    def _():
        c = c_ref[...]
        s_scr[...] = (c * _sigmoid(c)).astype(BF16)

    mod = _dot(s_scr[...], w_ref[...].astype(BF16)) + b_ref[...]
    for c in range(o_ref.shape[1] // LANES):
        cols = slice(LANES * c, LANES * (c + 1))
        for t in range(DEC_SEQ):
            rep_scr[pl.ds(t, DEC_BATCH, stride=DEC_SEQ), :] = mod[0:DEC_BATCH, cols]
        o_ref[0:S_ROWS, cols] = rep_scr[...]
    o_ref[S_ROWS:, :] = mod[DEC_BATCH:]


def _ada(c_all, w_ada, b_ada):
    m = c_all.shape[0]
    m_out = S_ROWS + m - DEC_BATCH
    n = w_ada.shape[1]
    tn = 1024
    return pl.pallas_call(
        _ada_kernel,
        grid=(n // tn,),
        in_specs=[
            pl.BlockSpec((m, D_MODEL), lambda j: (0, 0)),
            pl.BlockSpec((D_MODEL, tn), lambda j: (0, j)),
            pl.BlockSpec((1, tn), lambda j: (0, j)),
        ],
        out_specs=pl.BlockSpec((m_out, tn), lambda j: (0, j)),
        out_shape=jax.ShapeDtypeStruct((m_out, n), F32),
        scratch_shapes=[pltpu.VMEM((m, D_MODEL), BF16), pltpu.VMEM((S_ROWS, LANES), F32)],
        compiler_params=_cparams(("arbitrary",)),
        name="ada",
    )(c_all, w_ada, b_ada)


def _rope_store(acc, cos, sa, sb, out_ref, ncols, scale):
    for c in range(ncols // LANES):
        xc = acc[:, LANES * c:LANES * (c + 1)]
        rot = xc * cos + pltpu.roll(xc, LANES - 8, 1) * sa + pltpu.roll(xc, 8, 1) * sb
        if scale != 1.0:
            rot = rot * scale
        out_ref[:, LANES * c:LANES * (c + 1)] = rot.astype(out_ref.dtype)


def _inproj_kernel(x_ref, sh_ref, sc_ref, nw_ref, wq_ref, win_ref, wg_ref, cos_ref, sa_ref, sb_ref,
                   q_ref, kv_ref, kv32_ref, m_ref, g_ref, gt_ref, *rest, per_row, emit_w):
    h_scr = rest[-1]
    j = pl.program_id(1)

    def weight_tile():
        wt = win_ref[...].astype(BF16)
        if emit_w:
            rest[0][...] = wt
        return wt

    def q_tile():
        acc = _dot_nt(h_scr[...], wq_ref[...])
        _rope_store(acc, cos_ref[...], sa_ref[...], sb_ref[...], q_ref, 512, ATT_HEAD_DIM ** -0.5)

    @pl.when(j == 0)
    def _():
        x = x_ref[...]
        r = lax.rsqrt(jnp.mean(x * x, axis=-1, keepdims=True) + EPS)
        gain = nw_ref[...] * (1.0 + _mod_row(sc_ref, per_row))
        h = (x * r * gain + _mod_row(sh_ref, per_row)).astype(BF16)
        h_scr[...] = h
        wg = jnp.concatenate([wg_ref[...].astype(BF16), jnp.zeros((LANES - 16, D_MODEL), BF16)], axis=0)
        g = _dot_nt(h, wg)
        g_ref[...] = g
        gt_ref[...] = g.T[0:16, :]
        q_tile()

    @pl.when(j == 1)
    def _():
        q_tile()

    @pl.when(j == 2)
    def _():
        acc = _dot_nt(h_scr[...], weight_tile())
        _rope_store(acc, cos_ref[...], sa_ref[...], sb_ref[...], kv32_ref, KV_WIDTH, 1.0)
        kv32_ref[:, KV_WIDTH:] = acc[:, KV_WIDTH:]
        kv_ref[...] = kv32_ref[...].astype(BF16)

    @pl.when(j > 2)
    def _():
        m_ref[...] = _dot_nt(h_scr[...], weight_tile()).astype(BF16)


def _inproj(x, mod, mod_row_block, norm_w, wq_t, w_in_t, wg, cos, sa, sb, *, tm, per_row, emit_w):
    rows = x.shape[0]
    tn = 512
    nj = MAIN_WIDTH // tn
    mod_rows = tm if per_row else 8
    mod_idx = (lambda i: i) if per_row else (lambda i: mod_row_block)
    kern = functools.partial(_inproj_kernel, per_row=per_row, emit_w=emit_w)
    copy_spec = pl.BlockSpec((tn, D_MODEL), lambda i, j: (jnp.maximum(j, 2) - 2, 0))
    w_spec = pl.BlockSpec((tn, D_MODEL), lambda i, j: (jnp.maximum(j, 2), 0)) if emit_w else copy_spec
    extra_specs = [copy_spec] if emit_w else []
    extra_shapes = [jax.ShapeDtypeStruct((MAIN_WIDTH - ATT_WIDTH, D_MODEL), BF16)] if emit_w else []
    return pl.pallas_call(
        kern,
        grid=(rows // tm, nj),
        in_specs=[
            pl.BlockSpec((tm, D_MODEL), lambda i, j: (jnp.minimum(i + jnp.minimum(j, 1), rows // tm - 1), 0)),
            pl.BlockSpec((mod_rows, D_MODEL), lambda i, j: (mod_idx(i), 0)),
            pl.BlockSpec((mod_rows, D_MODEL), lambda i, j: (mod_idx(i), 1)),
            pl.BlockSpec((1, D_MODEL), lambda i, j: (0, 0)),
            pl.BlockSpec((tn, D_MODEL), lambda i, j: (jnp.minimum(j, 1), 0)),
            w_spec,
            pl.BlockSpec((16, D_MODEL), lambda i, j: (0, 0)),
            pl.BlockSpec((tm, LANES), lambda i, j: (i, 0)),
            pl.BlockSpec((tm, LANES), lambda i, j: (i, 0)),
            pl.BlockSpec((tm, LANES), lambda i, j: (i, 0)),
        ],
        out_specs=[
            pl.BlockSpec((tm, tn), lambda i, j: (i, jnp.minimum(j, 1))),
            pl.BlockSpec((tm, tn), lambda i, j: (i, 0)),
            pl.BlockSpec((tm, tn), lambda i, j: (0, 0)),
            pl.BlockSpec((tm, tn), lambda i, j: (i, jnp.clip(j - 3, 0, 7))),
            pl.BlockSpec((tm, LANES), lambda i, j: (i, 0)),
            pl.BlockSpec((16, tm), lambda i, j: (0, i)),
        ] + extra_specs,
        out_shape=[
            jax.ShapeDtypeStruct((rows, ATT_WIDTH), BF16),
            jax.ShapeDtypeStruct((rows, 2 * KV_WIDTH), BF16),
            jax.ShapeDtypeStruct((tm, 2 * KV_WIDTH), F32),
            jax.ShapeDtypeStruct((rows, 4 * MLSTM_WIDTH), BF16),
            jax.ShapeDtypeStruct((rows, LANES), F32),
            jax.ShapeDtypeStruct((16, rows), F32),
        ] + extra_shapes,
        scratch_shapes=[pltpu.VMEM((tm, D_MODEL), BF16)],
        compiler_params=_cparams(("arbitrary", "arbitrary")),
        name="inproj_s" if per_row else "inproj_p",
    )(x, mod, mod, norm_w, wq_t, w_in_t, wg, cos, sa, sb)


def _attn_block(sink_ref, q_ref, row0, kv2, allowed, store):
    w = WINDOW
    grp = ATT_GROUP
    rows = grp * w
    member = lax.broadcasted_iota(jnp.int32, (rows, 1), 0) // w
    low = lax.broadcasted_iota(jnp.int32, (2 * w, LANES), 1) < ATT_HEAD_DIM
    low_o = lax.broadcasted_iota(jnp.int32, (rows, LANES), 1) < ATT_HEAD_DIM
    key_row = lax.broadcasted_iota(jnp.int32, (4 * w, LANES), 0)
    key_lane = lax.broadcasted_iota(jnp.int32, (4 * w, LANES), 1)
    ones_bd = (((key_row < 2 * w) & (key_lane < ATT_HEAD_DIM))
               | ((key_row >= 2 * w) & (key_lane >= ATT_HEAD_DIM))).astype(BF16)
    zero = jnp.zeros((2 * w, LANES), BF16)
    for cp in range(2):
        k128 = kv2[:, LANES * cp:LANES * (cp + 1)]
        v128 = kv2[:, KV_WIDTH + LANES * cp:KV_WIDTH + LANES * (cp + 1)]
        kbd = jnp.concatenate([jnp.where(low, k128, zero), jnp.where(low, zero, k128)], axis=0)
        vbd = jnp.concatenate([jnp.where(low, v128, zero), jnp.where(low, zero, v128)], axis=0)
        v_aug = jnp.concatenate([vbd, ones_bd], axis=1)
        q4 = jnp.concatenate([q_ref[row0:row0 + w, 256 * r + LANES * cp:256 * r + LANES * (cp + 1)]
                              for r in range(grp)], axis=0)
        s = _dot_nt(q4, kbd)
        es, tails = [], []
        for half in range(2):
            sh = jnp.where(allowed, s[:, 2 * w * half:2 * w * (half + 1)], -jnp.inf)
            head0 = (2 * cp + half) * grp
            sink = jnp.full((rows, 1), sink_ref[head0], F32)
            for r in range(1, grp):
                sink = jnp.where(member == r, sink_ref[head0 + r], sink)
            m = jnp.maximum(jnp.max(sh, axis=-1, keepdims=True), sink)
            es.append(jnp.exp(sh - m).astype(BF16))
            tails.append(jnp.exp(sink - m))
        oa = _dot(jnp.concatenate(es, axis=1), v_aug)
        l = oa[:, LANES:] + jnp.where(low_o, tails[0], tails[1])
        o = (oa[:, :LANES] / l).astype(BF16)
        for r in range(grp):
            store(256 * r + LANES * cp, o[w * r:w * (r + 1)])


MIX_TM = 512


def _mix_p_kernel(sink_ref, q_ref, kvp_ref, kvc_ref, mq_ref, mk_ref, mv_ref, mo_ref, g_ref, gt_ref, brow_ref, bcol_ref,
                  mnw_ref, x_ref, g1_ref, sh_ref, sc_ref, nw_ref, wa_ref, wm_ref,
                  x1_ref, h2_ref, c_out, n_out, m_out, att_scr, mh_scr, c_scr, n_scr, m_scr):
    s = pl.program_id(0)
    w = WINDOW
    tiles = pl.num_programs(0) - 1
    slot = s % 2

    @pl.when(s == 0)
    def _():
        c_scr[...] = jnp.zeros_like(c_scr)
        n_scr[...] = jnp.zeros_like(n_scr)
        m_scr[...] = jnp.zeros_like(m_scr)

    def project():
        y = _dot(att_scr[1 - slot], wa_ref[...]) + _dot(mh_scr[1 - slot], wm_ref[...])
        x1 = x_ref[...] + g1_ref[0:1, :] * y
        x1_ref[...] = x1
        r = lax.rsqrt(jnp.mean(x1 * x1, axis=-1, keepdims=True) + EPS)
        gain = nw_ref[...] * (1.0 + sc_ref[0:1, :])
        h2_ref[...] = (x1 * r * gain + sh_ref[0:1, :]).astype(BF16)

    def mixers():
        rows = ATT_GROUP * w
        qi = lax.broadcasted_iota(jnp.int32, (rows, 2 * w), 0) % w
        kj = lax.broadcasted_iota(jnp.int32, (rows, 2 * w), 1)
        first_off = jnp.where(s > 0, 0, 4 * w)
        causal = (kj >= w) & (kj - w <= qi)
        for blk in range(MIX_TM // w):
            prev = kvp_ref[...] if blk == 0 else kvc_ref[w * (blk - 1):w * blk, :]
            kv2 = jnp.concatenate([prev, kvc_ref[w * blk:w * (blk + 1), :]], axis=0)
            allowed = ((kj < w) & (kj > qi + (first_off if blk == 0 else 0))) | causal

            def store(c0, val, blk=blk):
                att_scr[slot, w * blk:w * (blk + 1), c0:c0 + LANES] = val

            _attn_block(sink_ref, q_ref, w * blk, kv2, allowed, store)

        for ch in range(MIX_TM // MLSTM_CHUNK_P):
            r0 = MLSTM_CHUNK_P * ch

            def store_mh(cs, val, r0=r0):
                mh_scr[slot, r0:r0 + MLSTM_CHUNK_P, cs] = val

            _mlstm_chunk(mq_ref, mk_ref, mv_ref, mo_ref, g_ref, gt_ref, brow_ref, bcol_ref, mnw_ref,
                         c_scr, n_scr, m_scr, r0, store_mh)

    @pl.when(s == 0)
    def _():
        mixers()

    @pl.when((s > 0) & (s < tiles))
    def _():
        project()
        mixers()

    @pl.when(s == tiles)
    def _():
        project()
        c_out[...] = c_scr[...]
        n_out[...] = n_scr[...]
        m_out[...] = m_scr[...]


def _mix_p(sinks, q, kv, m_all, gates, gates_t, brow, bcol, mnw, x, mod, mod_row_block, norm_w, w_att, w_m):
    tm = MIX_TM
    tiles = SEQ // tm
    bpt = tm // WINDOW
    dh = MLSTM_HEAD_DIM
    att_tile = lambda s: jnp.minimum(s, tiles - 1)
    out_tile = lambda s: jnp.maximum(s - 1, 0)
    m_spec = lambda col: pl.BlockSpec((tm, MLSTM_WIDTH), lambda s: (att_tile(s), col))
    return pl.pallas_call(
        _mix_p_kernel,
        grid=(tiles + 1,),
        in_specs=[
            pl.BlockSpec(memory_space=pltpu.SMEM),
            pl.BlockSpec((tm, ATT_WIDTH), lambda s: (att_tile(s), 0)),
            pl.BlockSpec((WINDOW, 2 * KV_WIDTH), lambda s: (jnp.maximum(bpt * att_tile(s) - 1, 0), 0)),
            pl.BlockSpec((tm, 2 * KV_WIDTH), lambda s: (att_tile(s), 0)),
            m_spec(0), m_spec(1), m_spec(2), m_spec(3),
            pl.BlockSpec((tm, LANES), lambda s: (att_tile(s), 0)),
            pl.BlockSpec((16, tm), lambda s: (0, att_tile(s))),
            pl.BlockSpec((1, LANES), lambda s: (0, 0)),
            pl.BlockSpec((16, MLSTM_CHUNK_P), lambda s: (0, 0)),
            pl.BlockSpec((1, MLSTM_WIDTH), lambda s: (0, 0)),
            pl.BlockSpec((tm, D_MODEL), lambda s: (out_tile(s), 0)),
            pl.BlockSpec((8, D_MODEL), lambda s: (mod_row_block, 2)),
            pl.BlockSpec((8, D_MODEL), lambda s: (mod_row_block, 3)),
            pl.BlockSpec((8, D_MODEL), lambda s: (mod_row_block, 4)),
            pl.BlockSpec((1, D_MODEL), lambda s: (0, 0)),
            pl.BlockSpec((ATT_WIDTH, D_MODEL), lambda s: (0, 0)),
            pl.BlockSpec((MLSTM_WIDTH, D_MODEL), lambda s: (0, 0)),
        ],
        out_specs=[
            pl.BlockSpec((tm, D_MODEL), lambda s: (out_tile(s), 0)),
            pl.BlockSpec((tm, D_MODEL), lambda s: (out_tile(s), 0)),
            pl.BlockSpec((MLSTM_HEADS, dh, dh), lambda s: (0, 0, 0)),
            pl.BlockSpec((8, dh), lambda s: (0, 0)),
            pl.BlockSpec((8, LANES), lambda s: (0, 0)),
        ],
        out_shape=[
            jax.ShapeDtypeStruct((SEQ, D_MODEL), F32),
            jax.ShapeDtypeStruct((SEQ, D_MODEL), BF16),
            jax.ShapeDtypeStruct((MLSTM_HEADS, dh, dh), F32),
            jax.ShapeDtypeStruct((8, dh), F32),
            jax.ShapeDtypeStruct((8, LANES), F32),
        ],
        scratch_shapes=[
            pltpu.VMEM((2, tm, ATT_WIDTH), BF16),
            pltpu.VMEM((2, tm, MLSTM_WIDTH), BF16),
            pltpu.VMEM((MLSTM_HEADS, dh, dh), F32),
            pltpu.VMEM((8, dh), F32),
            pltpu.VMEM((8, LANES), F32),
        ],
        compiler_params=_cparams(("arbitrary",)),
        name="mix_p",
    )(sinks, q, kv, kv, m_all, m_all, m_all, m_all, gates, gates_t, brow, bcol, mnw, x, mod, mod, mod, norm_w,
      w_att, w_m)


ATT_S_BB = 16


def _attn_s_kernel(sink_ref, q_ref, kv32_ref, ck_ref, cv_ref, o_ref, ko_ref, vo_ref, q32_scr):
    t_new = DEC_SEQ
    w = WINDOW
    q32_scr[...] = q_ref[...].astype(F32)
    rows = 4 * 4 * 8
    row = lax.broadcasted_iota(jnp.int32, (rows, w), 0)
    slot = lax.broadcasted_iota(jnp.int32, (rows, w), 1)
    t_row = row % t_new
    second = (row % 8) >= t_new
    win_ok = (slot < w - t_new) | (slot - (w - t_new) <= t_row)
    old_ok = (slot >= 1) & (slot < t_new) & (slot > t_row)
    lane256 = lax.broadcasted_iota(jnp.int32, (32, 2 * LANES), 1)
    sink = sink_ref[...][:, 0:1]
    kv_new = jnp.concatenate([kv32_ref[...], jnp.zeros((w - ATT_S_BB * t_new, 2 * KV_WIDTH), F32)], axis=0)
    kv_t = kv_new.T
    new_slot = lax.broadcasted_iota(jnp.int32, (KV_WIDTH, w), 1) >= w - t_new
    for b in range(ATT_S_BB):
        cols = pltpu.roll(kv_t, w - t_new - t_new * b, 1)
        k_shift = pltpu.roll(ck_ref[b].reshape(KV_WIDTH, w), w - t_new, 1)
        v_shift = pltpu.roll(cv_ref[b].reshape(KV_WIDTH, w), w - t_new, 1)
        ko_ref[b] = jnp.where(new_slot, cols[:KV_WIDTH], k_shift).reshape(ATT_KV_HEADS, ATT_HEAD_DIM, w)
        vo_ref[b] = jnp.where(new_slot, cols[KV_WIDTH:], v_shift).reshape(ATT_KV_HEADS, ATT_HEAD_DIM, w)
    for pair in range(ATT_S_BB // 2):
        b0, b1 = 2 * pair, 2 * pair + 1
        q32 = jnp.concatenate([q32_scr[8 * pair:8 * (pair + 1), 256 * r:256 * (r + 1)] for r in range(ATT_GROUP)],
                              axis=0)
        qpad = jnp.concatenate(
            [jnp.where((lane256 // ATT_HEAD_DIM) == g, q32, 0.0) for g in range(ATT_KV_HEADS)], axis=0).astype(BF16)
        kw = [ko_ref[b].reshape(KV_WIDTH, w).astype(BF16) for b in (b0, b1)]
        vw = [vo_ref[b].reshape(KV_WIDTH, w).astype(BF16) for b in (b0, b1)]
        kc = [ck_ref[b].reshape(KV_WIDTH, w).astype(BF16) for b in (b0, b1)]
        vc = [cv_ref[b].reshape(KV_WIDTH, w).astype(BF16) for b in (b0, b1)]
        s_w = jnp.where(second, _dot(qpad, kw[1]), _dot(qpad, kw[0]))
        s_c = jnp.where(second, _dot(qpad, kc[1]), _dot(qpad, kc[0]))
        s_w = jnp.where(win_ok, s_w, -jnp.inf)
        s_c = jnp.where(old_ok, s_c, -jnp.inf)
        m = jnp.maximum(jnp.maximum(jnp.max(s_w, axis=-1, keepdims=True), jnp.max(s_c, axis=-1, keepdims=True)), sink)
        e_w = jnp.exp(s_w - m)
        e_c = jnp.exp(s_c - m)
        l = jnp.sum(e_w, axis=-1, keepdims=True) + jnp.sum(e_c, axis=-1, keepdims=True) + jnp.exp(sink - m)
        p_w = e_w / l
        p_c = e_c / l
        zero = jnp.zeros_like(p_w)
        o = (_dot_nt(jnp.where(second, zero, p_w).astype(BF16), vw[0])
             + _dot_nt(jnp.where(second, p_w, zero).astype(BF16), vw[1])
             + _dot_nt(jnp.where(second, zero, p_c).astype(BF16), vc[0])
             + _dot_nt(jnp.where(second, p_c, zero).astype(BF16), vc[1]))
        o32 = jnp.zeros((32, 2 * LANES), F32)
        for g in range(ATT_KV_HEADS):
            o32 = jnp.where((lane256 // ATT_HEAD_DIM) == g, o[32 * g:32 * (g + 1), :], o32)
        for r in range(ATT_GROUP):
            o_ref[8 * pair:8 * (pair + 1), 256 * r:256 * (r + 1)] = o32[8 * r:8 * (r + 1), :]


def _attn_s(sink_col, q, kv32, ck, cv):
    bb = ATT_S_BB
    rows = bb * DEC_SEQ
    cache_block = (bb, ATT_KV_HEADS, ATT_HEAD_DIM, WINDOW)
    cache_shape = (DEC_BATCH, ATT_KV_HEADS, ATT_HEAD_DIM, WINDOW)
    return pl.pallas_call(
        _attn_s_kernel,
        grid=(DEC_BATCH // bb,),
        in_specs=[
            pl.BlockSpec((128, LANES), lambda i: (0, 0)),
            pl.BlockSpec((rows, ATT_WIDTH), lambda i: (i, 0)),
            pl.BlockSpec((rows, 2 * KV_WIDTH), lambda i: (i, 0)),
            pl.BlockSpec(cache_block, lambda i: (i, 0, 0, 0)),
            pl.BlockSpec(cache_block, lambda i: (i, 0, 0, 0)),
        ],
        out_specs=[
            pl.BlockSpec((rows, ATT_WIDTH), lambda i: (i, 0)),
            pl.BlockSpec(cache_block, lambda i: (i, 0, 0, 0)),
            pl.BlockSpec(cache_block, lambda i: (i, 0, 0, 0)),
        ],
        out_shape=[
            jax.ShapeDtypeStruct((S_ROWS, ATT_WIDTH), F32),
            jax.ShapeDtypeStruct(cache_shape, F32),
            jax.ShapeDtypeStruct(cache_shape, F32),
        ],
        scratch_shapes=[pltpu.VMEM((rows, ATT_WIDTH), F32)],
        compiler_params=_cparams(("arbitrary",)),
        name="attn_s",
    )(sink_col, q, kv32, ck, cv)


def _head_norm_gate(h, nw, mo):
    hn = h * lax.rsqrt(jnp.mean(h * h, axis=-1, keepdims=True) + EPS) * nw
    return hn * _sigmoid(mo.astype(F32))


def _mlstm_chunk(q_ref, k_ref, v_ref, mo_ref, g_ref, gt_ref, brow_ref, bcol_ref, nw_ref, c_scr, n_scr, m_scr,
                 r0, store):
    L = MLSTM_CHUNK_P
    dh = MLSTM_HEAD_DIM
    rs = slice(r0, r0 + L)
    ti = lax.broadcasted_iota(jnp.int32, (L, L), 0)
    si = lax.broadcasted_iota(jnp.int32, (L, L), 1)
    causal = si <= ti
    tri = causal.astype(F32)
    tri_t = (ti <= si).astype(F32)
    gates = g_ref[rs, :] + brow_ref[...]
    gates_t = gt_ref[:, rs] + bcol_ref[...]
    b_col = jnp.dot(tri, _log_sigmoid(gates), precision=HI, preferred_element_type=F32)
    b_row = jnp.dot(_log_sigmoid(gates_t), tri_t, precision=HI, preferred_element_type=F32)
    for hd in range(MLSTM_HEADS):
        cs = slice(dh * hd, dh * (hd + 1))
        b_c = b_col[:, MLSTM_HEADS + hd:MLSTM_HEADS + hd + 1]
        li_c = gates[:, hd:hd + 1]
        b_r = b_row[MLSTM_HEADS + hd:MLSTM_HEADS + hd + 1, :]
        li_r = gates_t[hd:hd + 1, :]
        dm = jnp.where(causal, b_c - b_r + li_r, -jnp.inf)
        m_prev = m_scr[hd:hd + 1, 0:1]
        m_inter = b_c + m_prev
        m_t = jnp.maximum(m_inter, jnp.max(dm, axis=-1, keepdims=True))
        q = q_ref[rs, cs] * (dh ** -0.5)
        k = k_ref[rs, cs]
        v = v_ref[rs, cs]
        sm = _dot_nt(q, k) * jnp.exp(dm - m_t)
        a = jnp.exp(m_inter - m_t)
        c_old = c_scr[hd]
        n_old = n_scr[hd:hd + 1, :]
        num = a * _dot_nt(q, c_old.astype(BF16)) + _dot(sm.astype(BF16), v)
        qn = jnp.sum(q.astype(F32) * n_old, axis=-1, keepdims=True)
        den = a * qn + jnp.sum(sm, axis=-1, keepdims=True)
        h = num / jnp.maximum(jnp.abs(den), jnp.exp(-m_t))
        store(cs, _head_norm_gate(h, nw_ref[:, cs], mo_ref[rs, cs]).astype(BF16))
        m_new = m_t[L - 1:L, :]
        b_last = b_c[L - 1:L, :]
        g = jnp.exp(b_last - b_c + li_c - m_new)
        decay = jnp.exp(b_last + m_prev - m_new)
        gv = (g * v.astype(F32)).astype(BF16)
        c_scr[hd] = decay * c_old + _dot_tn(gv, k)
        n_scr[hd:hd + 1, :] = decay * n_old + jnp.sum(g * k.astype(F32), axis=0, keepdims=True)
        m_scr[hd:hd + 1, :] = jnp.broadcast_to(m_new, (1, LANES))


MLSTM_S_BB = 8


def _mlstm_s_kernel(q_ref, k_ref, v_ref, mo_ref, g_ref, nrep_ref, mrep_ref, brow_ref, nw_ref, c_ref,
                    mh_ref, c_out, nrow_ref, mrow_ref):
    T = DEC_SEQ
    R = MLSTM_S_BB * T
    H = MLSTM_HEADS
    dh = MLSTM_HEAD_DIM

    def shift(x, d):
        return pltpu.roll(x, d, 0)

    lanes = lax.broadcasted_iota(jnp.int32, (R, LANES), 1)
    tmod = lax.broadcasted_iota(jnp.int32, (R, LANES), 0) % T
    tmod_w = lax.broadcasted_iota(jnp.int32, (R, MLSTM_WIDTH), 0) % T
    head_ok = lanes < H
    gates = g_ref[...] + brow_ref[...]
    li = jnp.where(head_ok, gates, 0.0)
    lf = jnp.where(head_ok, pltpu.roll(_log_sigmoid(gates), LANES - H, 1), 0.0)
    bcum = lf
    for d in range(1, T):
        bcum = bcum + jnp.where(tmod >= d, shift(lf, d), 0.0)
    m0 = mrep_ref[...]
    m_inter = bcum + m0
    dms = [li] + [jnp.where(tmod >= d, bcum - shift(bcum, d) + shift(li, d), -jnp.inf) for d in range(1, T)]
    m_t = m_inter
    for dm in dms:
        m_t = jnp.maximum(m_t, dm)
    a = jnp.exp(m_inter - m_t)
    ws = [jnp.exp(dm - m_t) for dm in dms]

    q_bf = q_ref[...] * (dh ** -0.5)
    q = q_bf.astype(F32)
    k = k_ref[...].astype(F32)
    v = v_ref[...].astype(F32)
    seg = (lax.broadcasted_iota(jnp.int32, (MLSTM_WIDTH, LANES), 0) // dh
           == lax.broadcasted_iota(jnp.int32, (MLSTM_WIDTH, LANES), 1)).astype(F32)
    ex = (lax.broadcasted_iota(jnp.int32, (LANES, MLSTM_WIDTH), 1) // dh
          == lax.broadcasted_iota(jnp.int32, (LANES, MLSTM_WIDTH), 0)).astype(F32)

    def segsum(x):
        return jnp.dot(x, seg, precision=HI, preferred_element_type=F32)

    def expand(x):
        return jnp.dot(x, ex, precision=HI, preferred_element_type=F32)

    ks = [k] + [shift(k, d) for d in range(1, T)]
    vs = [v] + [shift(v, d) for d in range(1, T)]
    sms = [segsum(q * ks[d]) * ws[d] for d in range(T)]
    den = a * segsum(q * nrep_ref[...])
    for sm in sms:
        den = den + sm
    inv = 1.0 / jnp.maximum(jnp.abs(den), jnp.exp(-m_t))

    def last(x):
        out = jnp.zeros_like(x)
        for jj in range(T):
            out = jnp.where(tmod == T - 1 - jj, x if jj == 0 else pltpu.roll(x, R - jj, 0), out)
        return out

    m_new = last(m_t)
    b_last = last(bcum)
    g = jnp.where(head_ok, jnp.exp(b_last - bcum + li - m_new), 0.0)
    decay = jnp.where(head_ok, jnp.exp(b_last + m0 - m_new), 0.0)
    a_f = expand(a * inv)
    w_f = [expand(sm * inv) for sm in sms]
    g_f = expand(g)
    d_f = expand(decay)
    gv = (g_f * v).astype(BF16)
    rowb = lax.broadcasted_iota(jnp.int32, (R, dh), 0) // T
    for hd in range(H):
        cs = slice(dh * hd, dh * (hd + 1))
        qh = q_bf[:, cs]
        kh = k_ref[:, cs]
        gvh = gv[:, cs]
        qc = jnp.zeros((R, dh), F32)
        for bb in range(MLSTM_S_BB):
            c_old = c_ref[bb, hd]
            qc = jnp.where(rowb == bb, _dot_nt(qh, c_old.astype(BF16)), qc)
            upd = _dot_tn(jnp.where(rowb == bb, gvh, jnp.zeros_like(gvh)), kh)
            c_out[bb, hd] = d_f[T * bb:T * bb + 1, cs] * c_old + upd
        h = a_f[:, cs] * qc
        for d in range(T):
            h = h + w_f[d][:, cs] * vs[d][:, cs]
        mh_ref[:, cs] = _head_norm_gate(h, nw_ref[:, cs], mo_ref[:, cs]).astype(BF16)
    gk = g_f * k
    nsum = gk
    for d in range(1, T):
        nsum = nsum + jnp.where(tmod_w >= d, shift(gk, d), 0.0)
    nrow_ref[...] = d_f * nrep_ref[...] + nsum
    mrow_ref[...] = m_t


def _mlstm_s(m_all, gates, n_rep, m_rep, brow, nw, state_c):
    bb = MLSTM_S_BB
    R = bb * DEC_SEQ
    dh = MLSTM_HEAD_DIM
    H = MLSTM_HEADS
    return pl.pallas_call(
        _mlstm_s_kernel,
        grid=(DEC_BATCH // bb,),
        in_specs=[
            pl.BlockSpec((R, MLSTM_WIDTH), lambda i: (i, 0)),
            pl.BlockSpec((R, MLSTM_WIDTH), lambda i: (i, 1)),
            pl.BlockSpec((R, MLSTM_WIDTH), lambda i: (i, 2)),
            pl.BlockSpec((R, MLSTM_WIDTH), lambda i: (i, 3)),
            pl.BlockSpec((R, LANES), lambda i: (i, 0)),
            pl.BlockSpec((R, MLSTM_WIDTH), lambda i: (i, 0)),
            pl.BlockSpec((R, LANES), lambda i: (i, 0)),
            pl.BlockSpec((1, LANES), lambda i: (0, 0)),
            pl.BlockSpec((1, MLSTM_WIDTH), lambda i: (0, 0)),
            pl.BlockSpec((bb, H, dh, dh), lambda i: (i, 0, 0, 0)),
        ],
        out_specs=[
            pl.BlockSpec((R, MLSTM_WIDTH), lambda i: (i, 0)),
            pl.BlockSpec((bb, H, dh, dh), lambda i: (i, 0, 0, 0)),
            pl.BlockSpec((R, MLSTM_WIDTH), lambda i: (i, 0)),
            pl.BlockSpec((R, LANES), lambda i: (i, 0)),
        ],
        out_shape=[
            jax.ShapeDtypeStruct((S_ROWS, MLSTM_WIDTH), BF16),
            jax.ShapeDtypeStruct((DEC_BATCH, H, dh, dh), F32),
            jax.ShapeDtypeStruct((S_ROWS, MLSTM_WIDTH), F32),
            jax.ShapeDtypeStruct((S_ROWS, LANES), F32),
        ],
        compiler_params=_cparams(("arbitrary",)),
        name="mlstm_s",
    )(m_all, m_all, m_all, m_all, gates, n_rep, m_rep, brow, nw, state_c)


def _outproj_kernel(att_ref, mh_ref, x_ref, g1_ref, sh_ref, sc_ref, nw_ref, wa_ref, wm_ref, x1_ref, h2_ref, *,
                    per_row):
    y = _dot(att_ref[...].astype(BF16), wa_ref[...]) + _dot(mh_ref[...], wm_ref[...])
    x1 = x_ref[...] + _mod_row(g1_ref, per_row) * y
    x1_ref[...] = x1
    r = lax.rsqrt(jnp.mean(x1 * x1, axis=-1, keepdims=True) + EPS)
    h2 = (x1 * r * nw_ref[...]) * (1.0 + _mod_row(sc_ref, per_row)) + _mod_row(sh_ref, per_row)
    h2_ref[...] = h2.astype(BF16)


def _outproj(att, mh, x, mod, mod_row_block, norm_w, w_att, w_m, *, tm, per_row):
    rows = x.shape[0]
    mod_rows = tm if per_row else 8
    mod_idx = (lambda i: i) if per_row else (lambda i: mod_row_block)
    kern = functools.partial(_outproj_kernel, per_row=per_row)
    return pl.pallas_call(
        kern,
        grid=(rows // tm,),
        in_specs=[
            pl.BlockSpec((tm, ATT_WIDTH), lambda i: (i, 0)),
            pl.BlockSpec((tm, MLSTM_WIDTH), lambda i: (i, 0)),
            pl.BlockSpec((tm, D_MODEL), lambda i: (i, 0)),
            pl.BlockSpec((mod_rows, D_MODEL), lambda i: (mod_idx(i), 2)),
            pl.BlockSpec((mod_rows, D_MODEL), lambda i: (mod_idx(i), 3)),
            pl.BlockSpec((mod_rows, D_MODEL), lambda i: (mod_idx(i), 4)),
            pl.BlockSpec((1, D_MODEL), lambda i: (0, 0)),
            pl.BlockSpec((ATT_WIDTH, D_MODEL), lambda i: (0, 0)),
            pl.BlockSpec((MLSTM_WIDTH, D_MODEL), lambda i: (0, 0)),
        ],
        out_specs=[
            pl.BlockSpec((tm, D_MODEL), lambda i: (i, 0)),
            pl.BlockSpec((tm, D_MODEL), lambda i: (i, 0)),
        ],
        out_shape=[
            jax.ShapeDtypeStruct((rows, D_MODEL), F32),
            jax.ShapeDtypeStruct((rows, D_MODEL), BF16),
        ],
        compiler_params=_cparams(("arbitrary",)),
        name="outproj_s" if per_row else "outproj_p",
    )(att, mh, x, mod, mod, mod, norm_w, w_att, w_m)


def _ffn_a_kernel(hp_ref, hs_ref, wg_ref, wu_ref, ap_ref, as_ref, wg_scr, wu_scr):
    i = pl.program_id(1)

    def swiglu(h_ref, a_ref):
        h = h_ref[...]
        g = _dot(h, wg_scr[...])
        u = _dot(h, wu_scr[...])
        a_ref[...] = (g * _sigmoid(g) * u).astype(BF16)

    @pl.when(i == 0)
    def _():
        wg_scr[...] = wg_ref[...].astype(BF16)
        wu_scr[...] = wu_ref[...].astype(BF16)
        swiglu(hs_ref, as_ref)

    @pl.when(i > 0)
    def _():
        swiglu(hp_ref, ap_ref)


def _ffn_a(h2_p, h2_s, w_gate, w_up, *, tm):
    rows_p, rows_s = h2_p.shape[0], h2_s.shape[0]
    tn = 512
    ni = rows_p // tm
    tile = lambda i: jnp.maximum(i - 1, 0)
    return pl.pallas_call(
        _ffn_a_kernel,
        grid=(D_FF // tn, ni + 1),
        in_specs=[
            pl.BlockSpec((tm, D_MODEL), lambda j, i: (tile(i), 0)),
            pl.BlockSpec((rows_s, D_MODEL), lambda j, i: (0, 0)),
            pl.BlockSpec((D_MODEL, tn), lambda j, i: (0, j)),
            pl.BlockSpec((D_MODEL, tn), lambda j, i: (0, j)),
        ],
        out_specs=[
            pl.BlockSpec((tm, tn), lambda j, i: (tile(i), j)),
            pl.BlockSpec((rows_s, tn), lambda j, i: (0, j)),
        ],
        out_shape=[
            jax.ShapeDtypeStruct((rows_p, D_FF), BF16),
            jax.ShapeDtypeStruct((rows_s, D_FF), BF16),
        ],
        scratch_shapes=[pltpu.VMEM((D_MODEL, tn), BF16), pltpu.VMEM((D_MODEL, tn), BF16)],
        compiler_params=_cparams(("arbitrary", "arbitrary")),
        name="ffn_a",
    )(h2_p, h2_s, w_gate, w_up)


def _ffn_b_kernel(a_ref, wd_ref, x1_ref, g2_ref, fw_ref, y_ref, *rest, per_row, emit_w):
    kk = pl.program_id(1)

    def weight_tile():
        wt = wd_ref[...].astype(BF16)
        if emit_w:
            rest[0][...] = wt
        return wt

    @pl.when(kk == 0)
    def _():
        y_ref[...] = _dot(a_ref[...], weight_tile())

    @pl.when(kk > 0)
    def _():
        y_ref[...] += _dot(a_ref[...], weight_tile())

    @pl.when(kk == pl.num_programs(1) - 1)
    def _():
        x2 = x1_ref[...] + _mod_row(g2_ref, per_row) * y_ref[...]
        y_ref[...] = x2 * lax.rsqrt(jnp.mean(x2 * x2, axis=-1, keepdims=True) + EPS) * fw_ref[...]


def _ffn_b(a, w_down, x1, mod, mod_row_block, final_w, *, tm, tk, per_row, emit_w):
    rows = a.shape[0]
    mod_rows = tm if per_row else 8
    mod_idx = (lambda i: i) if per_row else (lambda i: mod_row_block)
    kern = functools.partial(_ffn_b_kernel, per_row=per_row, emit_w=emit_w)
    w_spec = pl.BlockSpec((tk, D_MODEL), lambda i, k: (k, 0))
    y_spec = pl.BlockSpec((tm, D_MODEL), lambda i, k: (i, 0))
    y_shape = jax.ShapeDtypeStruct((rows, D_MODEL), F32)
    return pl.pallas_call(
        kern,
        grid=(rows // tm, D_FF // tk),
        in_specs=[
            pl.BlockSpec((tm, tk), lambda i, k: (i, k)),
            w_spec,
            pl.BlockSpec((tm, D_MODEL), lambda i, k: (i, 0)),
            pl.BlockSpec((mod_rows, D_MODEL), lambda i, k: (mod_idx(i), 5)),
            pl.BlockSpec((1, D_MODEL), lambda i, k: (0, 0)),
        ],
        out_specs=[y_spec, w_spec] if emit_w else y_spec,
        out_shape=[y_shape, jax.ShapeDtypeStruct((D_FF, D_MODEL), BF16)] if emit_w else y_shape,
        compiler_params=_cparams(("arbitrary", "arbitrary")),
        name="ffn_b_s" if per_row else "ffn_b_p",
    )(a, w_down, x1, mod, final_w)


def _rope_tables(pos):
    half = ROPE_DIM // 2
    inv = np.float32(ROPE_THETA) ** (-np.arange(0, ROPE_DIM, 2, dtype=np.float32) / np.float32(ROPE_DIM))
    d = np.arange(LANES) % ATT_HEAD_DIM
    ang = pos.astype(np.float32)[:, None] * inv[d % half][None, :].astype(np.float32)
    cos, sin = np.cos(ang), np.sin(ang)
    d = d[None, :]
    tables = (np.where(d < ROPE_DIM, cos, 1.0), np.where(d < half, -sin, 0.0),
              np.where((d >= half) & (d < ROPE_DIM), sin, 0.0))
    return tuple(jnp.asarray(t.astype(np.float32)) for t in tables)


def kernel(x_prompt, x_sample, cache_k_win, cache_v_win, state_C, state_n, state_m, c_prompt, c_sample,
           norm1_w, norm2_w, final_norm_w, w_ada, b_ada, w_in, b_ig, b_fg, attn_sinks, mh_norm_w,
           w_out, w_gate, w_up, w_down):
    assert w_in.shape[0] == 1, "single-layer trunk"
    T = DEC_SEQ
    xp = x_prompt[0]
    xs = x_sample.reshape(S_ROWS, D_MODEL)

    c_all = jnp.concatenate([c_sample, c_prompt, jnp.zeros((15, D_MODEL), F32)], axis=0)
    mod = _ada(c_all, w_ada[0], b_ada)
    prompt_mod_block = S_ROWS // 8

    w_in_t = jnp.transpose(w_in[0])
    wq_t = (w_in_t[:ATT_WIDTH].reshape(ATT_KV_HEADS, ATT_GROUP, ATT_HEAD_DIM, D_MODEL)
            .transpose(1, 0, 2, 3).reshape(ATT_WIDTH, D_MODEL).astype(BF16))
    w_gates_t = w_in_t[MAIN_WIDTH:]
    wg = jnp.pad(w_gates_t, ((0, 16 - 2 * MLSTM_HEADS), (0, 0)))
    w_out_att = (w_out[0, :ATT_WIDTH].reshape(ATT_KV_HEADS, ATT_GROUP, ATT_HEAD_DIM, D_MODEL)
                 .transpose(1, 0, 2, 3).reshape(ATT_WIDTH, D_MODEL).astype(BF16))
    w_out_m = w_out[0, ATT_WIDTH:].astype(BF16)
    n1 = norm1_w.reshape(1, D_MODEL)
    n2 = norm2_w.reshape(1, D_MODEL)
    fw = final_norm_w.reshape(1, D_MODEL)
    nw = mh_norm_w.reshape(1, MLSTM_WIDTH)
    gate_bias = jnp.concatenate([b_ig[0], b_fg[0]])
    brow = jnp.pad(gate_bias, (0, LANES - 2 * MLSTM_HEADS)).reshape(1, LANES)
    bcol = jnp.broadcast_to(jnp.pad(gate_bias, (0, 16 - 2 * MLSTM_HEADS))[:, None], (16, MLSTM_CHUNK_P))

    rope_p = _rope_tables(np.arange(SEQ))
    rope_s = _rope_tables(np.tile(PAST_LEN + np.arange(T), DEC_BATCH))

    tm_p = 1024
    q_s, _, kv32_s, m_s, g_s, _, w_in_bf = _inproj(xs, mod, 0, n1, wq_t, w_in_t, wg, *rope_s,
                                                   tm=S_ROWS, per_row=True, emit_w=True)
    q_p, kv_p, kv32_p, m_p, g_p, gt_p = _inproj(xp, mod, prompt_mod_block, n1, wq_t, w_in_bf, wg, *rope_p,
                                                tm=tm_p, per_row=False, emit_w=False)

    sinks = attn_sinks[0]
    sink_col = jnp.broadcast_to(sinks.reshape(ATT_HEADS, 1, 1), (ATT_HEADS, 8, LANES)).reshape(128, LANES)
    ck = jnp.transpose(cache_k_win[0], (0, 2, 3, 1))
    cv = jnp.transpose(cache_v_win[0], (0, 2, 3, 1))
    att_s, kwin_s, vwin_s = _attn_s(sink_col, q_s, kv32_s, ck, cv)

    n_rep = jnp.repeat(state_n[0].reshape(DEC_BATCH, MLSTM_WIDTH), T, axis=0)
    m_rep = jnp.pad(jnp.repeat(state_m[0], T, axis=0), ((0, 0), (0, LANES - MLSTM_HEADS)))
    mh_s, c_s, nrow_s, mrow_s = _mlstm_s(m_s, g_s, n_rep, m_rep, brow, nw, state_C[0])

    x1_p, h2_p, c_p, n_p, mm_p = _mix_p(sinks, q_p, kv_p, m_p, g_p, gt_p, brow, bcol, nw, xp, mod, prompt_mod_block,
                                        n2, w_out_att, w_out_m)
    x1_s, h2_s = _outproj(att_s, mh_s, xs, mod, 0, n2, w_out_att, w_out_m, tm=S_ROWS // 2, per_row=True)
    a_p, a_s = _ffn_a(h2_p, h2_s, w_gate[0], w_up[0], tm=tm_p)
    y_s, w_down_bf = _ffn_b(a_s, w_down[0], x1_s, mod, 0, fw, tm=S_ROWS, tk=1408, per_row=True, emit_w=True)
    y_p = _ffn_b(a_p, w_down_bf, x1_p, mod, prompt_mod_block, fw, tm=tm_p, tk=512, per_row=False, emit_w=False)

    kv_shape = (1, 1, WINDOW, ATT_KV_HEADS, ATT_HEAD_DIM)
    kv_last = kv32_p[tm_p - WINDOW:]
    dh = MLSTM_HEAD_DIM
    return (
        y_p.reshape(1, SEQ, D_MODEL),
        y_s.reshape(DEC_BATCH, T, D_MODEL),
        kv_last[:, :KV_WIDTH].reshape(kv_shape),
        kv_last[:, KV_WIDTH:].reshape(kv_shape),
        c_p.reshape(1, 1, MLSTM_HEADS, dh, dh),
        n_p[:MLSTM_HEADS].reshape(1, 1, MLSTM_HEADS, dh),
        mm_p[:MLSTM_HEADS, 0].reshape(1, 1, MLSTM_HEADS),
        jnp.transpose(kwin_s, (0, 3, 1, 2))[None],
        jnp.transpose(vwin_s, (0, 3, 1, 2))[None],
        c_s.reshape(1, DEC_BATCH, MLSTM_HEADS, dh, dh),
        nrow_s[T - 1::T].reshape(1, DEC_BATCH, MLSTM_HEADS, dh),
        mrow_s[T - 1::T, :MLSTM_HEADS].reshape(1, DEC_BATCH, MLSTM_HEADS),
    )
```

```python
import functools

import jax
import jax.numpy as jnp
import numpy as np
from jax import lax
from jax.experimental import pallas as pl
from jax.experimental.pallas import tpu as pltpu

F32 = jnp.float32
BF16 = jnp.bfloat16

D_MODEL = 2048
SEQ = 8192
DEC_BATCH = 128
DEC_SEQ = 4
S_ROWS = DEC_BATCH * DEC_SEQ
PAST_LEN = 16384
ATT_HEADS = 16
ATT_KV_HEADS = 4
ATT_GROUP = 4
ATT_HEAD_DIM = 64
WINDOW = 128
ROPE_THETA = 500000.0
ROPE_DIM = 16
MLSTM_HEADS = 4
MLSTM_HEAD_DIM = 256
ATT_WIDTH = 1024
KV_WIDTH = 256
MLSTM_WIDTH = 1024
MAIN_WIDTH = ATT_WIDTH + 2 * KV_WIDTH + 4 * MLSTM_WIDTH
D_FF = 5632
EPS = 1e-6

LANES = 128
MLSTM_CHUNK_P = 256
VMEM_LIMIT = 56 * 1024 * 1024

NT_DIMS = (((1,), (1,)), ((), ()))
TN_DIMS = (((0,), (0,)), ((), ()))
HI = lax.Precision.HIGHEST


def _cparams(sem):
    return pltpu.CompilerParams(dimension_semantics=sem, vmem_limit_bytes=VMEM_LIMIT)


def _dot(a, b):
    return jnp.dot(a, b, preferred_element_type=F32)


def _dot_nt(a, b):
    return lax.dot_general(a, b, NT_DIMS, preferred_element_type=F32)


def _dot_tn(a, b):
    return lax.dot_general(a, b, TN_DIMS, preferred_element_type=F32)


def _sigmoid(x):
    return 1.0 / (1.0 + jnp.exp(-x))


def _log_sigmoid(x):
    return jnp.minimum(x, 0.0) - jnp.log(1.0 + jnp.exp(-jnp.abs(x)))


def _mod_row(ref, per_row):
    return ref[...] if per_row else ref[0:1, :]


def _ada_kernel(c_ref, w_ref, b_ref, o_ref, s_scr, rep_scr):
    @pl.when(pl.program_id(0) == 0)
    def _():
        c = c_ref[...]
        s_scr[...] = (c * _sigmoid(c)).astype(BF16)

    mod = _dot(s_scr[...], w_ref[...].astype(BF16)) + b_ref[...]
    for c in range(o_ref.shape[1] // LANES):
        cols = slice(LANES * c, LANES * (c + 1))
        for t in range(DEC_SEQ):
            rep_scr[pl.ds(t, DEC_BATCH, stride=DEC_SEQ), :] = mod[0:DEC_BATCH, cols]
        o_ref[0:S_ROWS, cols] = rep_scr[...]
    o_ref[S_ROWS:, :] = mod[DEC_BATCH:]


def _ada(c_all, w_ada, b_ada):
    m = c_all.shape[0]
    m_out = S_ROWS + m - DEC_BATCH
    n = w_ada.shape[1]
    tn = 1024
    return pl.pallas_call(
        _ada_kernel,
        grid=(n // tn,),
        in_specs=[
            pl.BlockSpec((m, D_MODEL), lambda j: (0, 0)),
            pl.BlockSpec((D_MODEL, tn), lambda j: (0, j)),
            pl.BlockSpec((1, tn), lambda j: (0, j)),
        ],
        out_specs=pl.BlockSpec((m_out, tn), lambda j: (0, j)),
        out_shape=jax.ShapeDtypeStruct((m_out, n), F32),
        scratch_shapes=[pltpu.VMEM((m, D_MODEL), BF16), pltpu.VMEM((S_ROWS, LANES), F32)],
        compiler_params=_cparams(("arbitrary",)),
        name="ada",
    )(c_all, w_ada, b_ada)


def _rope_store(acc, cos, sa, sb, out_ref, ncols, scale):
    for c in range(ncols // LANES):
        xc = acc[:, LANES * c:LANES * (c + 1)]
        rot = xc * cos + pltpu.roll(xc, LANES - 8, 1) * sa + pltpu.roll(xc, 8, 1) * sb
        if scale != 1.0:
            rot = rot * scale
        out_ref[:, LANES * c:LANES * (c + 1)] = rot.astype(out_ref.dtype)


def _inproj_kernel(x_ref, sh_ref, sc_ref, nw_ref, wq_ref, win_ref, wg_ref, cos_ref, sa_ref, sb_ref,
                   q_ref, kv_ref, kv32_ref, m_ref, g_ref, gt_ref, *rest, per_row, emit_w):
    h_scr = rest[-1]
    j = pl.program_id(1)

    def weight_tile():
        wt = win_ref[...].astype(BF16)
        if emit_w:
            rest[0][...] = wt
        return wt

    def q_tile():
        acc = _dot_nt(h_scr[...], wq_ref[...])
        _rope_store(acc, cos_ref[...], sa_ref[...], sb_ref[...], q_ref, 512, ATT_HEAD_DIM ** -0.5)

    @pl.when(j == 0)
    def _():
        x = x_ref[...]
        r = lax.rsqrt(jnp.mean(x * x, axis=-1, keepdims=True) + EPS)
        gain = nw_ref[...] * (1.0 + _mod_row(sc_ref, per_row))
        h = (x * r * gain + _mod_row(sh_ref, per_row)).astype(BF16)
        h_scr[...] = h
        wg = jnp.concatenate([wg_ref[...].astype(BF16), jnp.zeros((LANES - 16, D_MODEL), BF16)], axis=0)
        g = _dot_nt(h, wg)
        g_ref[...] = g
        gt_ref[...] = g.T[0:16, :]
        q_tile()

    @pl.when(j == 1)
    def _():
        q_tile()

    @pl.when(j == 2)
    def _():
        acc = _dot_nt(h_scr[...], weight_tile())
        _rope_store(acc, cos_ref[...], sa_ref[...], sb_ref[...], kv32_ref, KV_WIDTH, 1.0)
        kv32_ref[:, KV_WIDTH:] = acc[:, KV_WIDTH:]
        kv_ref[...] = kv32_ref[...].astype(BF16)

    @pl.when(j > 2)
    def _():
        m_ref[...] = _dot_nt(h_scr[...], weight_tile()).astype(BF16)


def _inproj(x, mod, mod_row_block, norm_w, wq_t, w_in_t, wg, cos, sa, sb, *, tm, per_row, emit_w):
    rows = x.shape[0]
    tn = 512
    nj = MAIN_WIDTH // tn
    mod_rows = tm if per_row else 8
    mod_idx = (lambda i: i) if per_row else (lambda i: mod_row_block)
    kern = functools.partial(_inproj_kernel, per_row=per_row, emit_w=emit_w)
    copy_spec = pl.BlockSpec((tn, D_MODEL), lambda i, j: (jnp.maximum(j, 2) - 2, 0))
    w_spec = pl.BlockSpec((tn, D_MODEL), lambda i, j: (jnp.maximum(j, 2), 0)) if emit_w else copy_spec
    extra_specs = [copy_spec] if emit_w else []
    extra_shapes = [jax.ShapeDtypeStruct((MAIN_WIDTH - ATT_WIDTH, D_MODEL), BF16)] if emit_w else []
    return pl.pallas_call(
        kern,
        grid=(rows // tm, nj),
        in_specs=[
            pl.BlockSpec((tm, D_MODEL), lambda i, j: (jnp.minimum(i + jnp.minimum(j, 1), rows // tm - 1), 0)),
            pl.BlockSpec((mod_rows, D_MODEL), lambda i, j: (mod_idx(i), 0)),
            pl.BlockSpec((mod_rows, D_MODEL), lambda i, j: (mod_idx(i), 1)),
            pl.BlockSpec((1, D_MODEL), lambda i, j: (0, 0)),
            pl.BlockSpec((tn, D_MODEL), lambda i, j: (jnp.minimum(j, 1), 0)),
            w_spec,
            pl.BlockSpec((16, D_MODEL), lambda i, j: (0, 0)),
            pl.BlockSpec((tm, LANES), lambda i, j: (i, 0)),
            pl.BlockSpec((tm, LANES), lambda i, j: (i, 0)),
            pl.BlockSpec((tm, LANES), lambda i, j: (i, 0)),
        ],
        out_specs=[
            pl.BlockSpec((tm, tn), lambda i, j: (i, jnp.minimum(j, 1))),
            pl.BlockSpec((tm, tn), lambda i, j: (i, 0)),
            pl.BlockSpec((tm, tn), lambda i, j: (0, 0)),
            pl.BlockSpec((tm, tn), lambda i, j: (i, jnp.clip(j - 3, 0, 7))),
            pl.BlockSpec((tm, LANES), lambda i, j: (i, 0)),
            pl.BlockSpec((16, tm), lambda i, j: (0, i)),
        ] + extra_specs,
        out_shape=[
            jax.ShapeDtypeStruct((rows, ATT_WIDTH), BF16),
            jax.ShapeDtypeStruct((rows, 2 * KV_WIDTH), BF16),
            jax.ShapeDtypeStruct((tm, 2 * KV_WIDTH), F32),
            jax.ShapeDtypeStruct((rows, 4 * MLSTM_WIDTH), BF16),
            jax.ShapeDtypeStruct((rows, LANES), F32),
            jax.ShapeDtypeStruct((16, rows), F32),
        ] + extra_shapes,
        scratch_shapes=[pltpu.VMEM((tm, D_MODEL), BF16)],
        compiler_params=_cparams(("arbitrary", "arbitrary")),
        name="inproj_s" if per_row else "inproj_p",
    )(x, mod, mod, norm_w, wq_t, w_in_t, wg, cos, sa, sb)


def _attn_block(sink_ref, q_ref, row0, kv2, allowed, store):
    w = WINDOW
    grp = ATT_GROUP
    rows = grp * w
    member = lax.broadcasted_iota(jnp.int32, (rows, 1), 0) // w
    low = lax.broadcasted_iota(jnp.int32, (2 * w, LANES), 1) < ATT_HEAD_DIM
    low_o = lax.broadcasted_iota(jnp.int32, (rows, LANES), 1) < ATT_HEAD_DIM
    key_row = lax.broadcasted_iota(jnp.int32, (4 * w, LANES), 0)
    key_lane = lax.broadcasted_iota(jnp.int32, (4 * w, LANES), 1)
    ones_bd = (((key_row < 2 * w) & (key_lane < ATT_HEAD_DIM))
               | ((key_row >= 2 * w) & (key_lane >= ATT_HEAD_DIM))).astype(BF16)
    zero = jnp.zeros((2 * w, LANES), BF16)
    for cp in range(2):
        k128 = kv2[:, LANES * cp:LANES * (cp + 1)]
        v128 = kv2[:, KV_WIDTH + LANES * cp:KV_WIDTH + LANES * (cp + 1)]
        kbd = jnp.concatenate([jnp.where(low, k128, zero), jnp.where(low, zero, k128)], axis=0)
        vbd = jnp.concatenate([jnp.where(low, v128, zero), jnp.where(low, zero, v128)], axis=0)
        v_aug = jnp.concatenate([vbd, ones_bd], axis=1)
        q4 = jnp.concatenate([q_ref[row0:row0 + w, 256 * r + LANES * cp:256 * r + LANES * (cp + 1)]
                              for r in range(grp)], axis=0)
        s = _dot_nt(q4, kbd)
        es, tails = [], []
        for half in range(2):
            sh = jnp.where(allowed, s[:, 2 * w * half:2 * w * (half + 1)], -jnp.inf)
            head0 = (2 * cp + half) * grp
            sink = jnp.full((rows, 1), sink_ref[head0], F32)
            for r in range(1, grp):
                sink = jnp.where(member == r, sink_ref[head0 + r], sink)
            m = jnp.maximum(jnp.max(sh, axis=-1, keepdims=True), sink)
            es.append(jnp.exp(sh - m).astype(BF16))
            tails.append(jnp.exp(sink - m))
        oa = _dot(jnp.concatenate(es, axis=1), v_aug)
        l = oa[:, LANES:] + jnp.where(low_o, tails[0], tails[1])
        o = (oa[:, :LANES] / l).astype(BF16)
        for r in range(grp):
            store(256 * r + LANES * cp, o[w * r:w * (r + 1)])


MIX_TM = 512


def _mix_p_kernel(sink_ref, q_ref, kvp_ref, kvc_ref, mq_ref, mk_ref, mv_ref, mo_ref, g_ref, gt_ref, brow_ref, bcol_ref,
                  mnw_ref, x_ref, g1_ref, sh_ref, sc_ref, nw_ref, wa_ref, wm_ref,
                  x1_ref, h2_ref, c_out, n_out, m_out, att_scr, mh_scr, c_scr, n_scr, m_scr):
    s = pl.program_id(0)
    w = WINDOW
    tiles = pl.num_programs(0) - 1
    slot = s % 2

    @pl.when(s == 0)
    def _():
        c_scr[...] = jnp.zeros_like(c_scr)
        n_scr[...] = jnp.zeros_like(n_scr)
        m_scr[...] = jnp.zeros_like(m_scr)

    def project():
        y = _dot(att_scr[1 - slot], wa_ref[...]) + _dot(mh_scr[1 - slot], wm_ref[...])
        x1 = x_ref[...] + g1_ref[0:1, :] * y
        x1_ref[...] = x1
        r = lax.rsqrt(jnp.mean(x1 * x1, axis=-1, keepdims=True) + EPS)
        gain = nw_ref[...] * (1.0 + sc_ref[0:1, :])
        h2_ref[...] = (x1 * r * gain + sh_ref[0:1, :]).astype(BF16)

    def mixers():
        rows = ATT_GROUP * w
        qi = lax.broadcasted_iota(jnp.int32, (rows, 2 * w), 0) % w
        kj = lax.broadcasted_iota(jnp.int32, (rows, 2 * w), 1)
        first_off = jnp.where(s > 0, 0, 4 * w)
        causal = (kj >= w) & (kj - w <= qi)
        for blk in range(MIX_TM // w):
            prev = kvp_ref[...] if blk == 0 else kvc_ref[w * (blk - 1):w * blk, :]
            kv2 = jnp.concatenate([prev, kvc_ref[w * blk:w * (blk + 1), :]], axis=0)
            allowed = ((kj < w) & (kj > qi + (first_off if blk == 0 else 0))) | causal

            def store(c0, val, blk=blk):
                att_scr[slot, w * blk:w * (blk + 1), c0:c0 + LANES] = val

            _attn_block(sink_ref, q_ref, w * blk, kv2, allowed, store)

        for ch in range(MIX_TM // MLSTM_CHUNK_P):
            r0 = MLSTM_CHUNK_P * ch

            def store_mh(cs, val, r0=r0):
                mh_scr[slot, r0:r0 + MLSTM_CHUNK_P, cs] = val

            _mlstm_chunk(mq_ref, mk_ref, mv_ref, mo_ref, g_ref, gt_ref, brow_ref, bcol_ref, mnw_ref,
                         c_scr, n_scr, m_scr, r0, store_mh)

    @pl.when(s == 0)
    def _():
        mixers()

    @pl.when((s > 0) & (s < tiles))
    def _():
        project()
        mixers()

    @pl.when(s == tiles)
    def _():
        project()
        c_out[...] = c_scr[...]
        n_out[...] = n_scr[...]
        m_out[...] = m_scr[...]


def _mix_p(sinks, q, kv, m_all, gates, gates_t, brow, bcol, mnw, x, mod, mod_row_block, norm_w, w_att, w_m):
    tm = MIX_TM
    tiles = SEQ // tm
    bpt = tm // WINDOW
    dh = MLSTM_HEAD_DIM
    att_tile = lambda s: jnp.minimum(s, tiles - 1)
    out_tile = lambda s: jnp.maximum(s - 1, 0)
    m_spec = lambda col: pl.BlockSpec((tm, MLSTM_WIDTH), lambda s: (att_tile(s), col))
    return pl.pallas_call(
        _mix_p_kernel,
        grid=(tiles + 1,),
        in_specs=[
            pl.BlockSpec(memory_space=pltpu.SMEM),
            pl.BlockSpec((tm, ATT_WIDTH), lambda s: (att_tile(s), 0)),
            pl.BlockSpec((WINDOW, 2 * KV_WIDTH), lambda s: (jnp.maximum(bpt * att_tile(s) - 1, 0), 0)),
            pl.BlockSpec((tm, 2 * KV_WIDTH), lambda s: (att_tile(s), 0)),
            m_spec(0), m_spec(1), m_spec(2), m_spec(3),
            pl.BlockSpec((tm, LANES), lambda s: (att_tile(s), 0)),
            pl.BlockSpec((16, tm), lambda s: (0, att_tile(s))),
            pl.BlockSpec((1, LANES), lambda s: (0, 0)),
            pl.BlockSpec((16, MLSTM_CHUNK_P), lambda s: (0, 0)),
            pl.BlockSpec((1, MLSTM_WIDTH), lambda s: (0, 0)),
            pl.BlockSpec((tm, D_MODEL), lambda s: (out_tile(s), 0)),
            pl.BlockSpec((8, D_MODEL), lambda s: (mod_row_block, 2)),
            pl.BlockSpec((8, D_MODEL), lambda s: (mod_row_block, 3)),
            pl.BlockSpec((8, D_MODEL), lambda s: (mod_row_block, 4)),
            pl.BlockSpec((1, D_MODEL), lambda s: (0, 0)),
            pl.BlockSpec((ATT_WIDTH, D_MODEL), lambda s: (0, 0)),
            pl.BlockSpec((MLSTM_WIDTH, D_MODEL), lambda s: (0, 0)),
        ],
        out_specs=[
            pl.BlockSpec((tm, D_MODEL), lambda s: (out_tile(s), 0)),
            pl.BlockSpec((tm, D_MODEL), lambda s: (out_tile(s), 0)),
            pl.BlockSpec((MLSTM_HEADS, dh, dh), lambda s: (0, 0, 0)),
            pl.BlockSpec((8, dh), lambda s: (0, 0)),
            pl.BlockSpec((8, LANES), lambda s: (0, 0)),
        ],
        out_shape=[
            jax.ShapeDtypeStruct((SEQ, D_MODEL), F32),
            jax.ShapeDtypeStruct((SEQ, D_MODEL), BF16),
            jax.ShapeDtypeStruct((MLSTM_HEADS, dh, dh), F32),
            jax.ShapeDtypeStruct((8, dh), F32),
            jax.ShapeDtypeStruct((8, LANES), F32),
        ],
        scratch_shapes=[
            pltpu.VMEM((2, tm, ATT_WIDTH), BF16),
            pltpu.VMEM((2, tm, MLSTM_WIDTH), BF16),
            pltpu.VMEM((MLSTM_HEADS, dh, dh), F32),
            pltpu.VMEM((8, dh), F32),
            pltpu.VMEM((8, LANES), F32),
        ],
        compiler_params=_cparams(("arbitrary",)),
        name="mix_p",
    )(sinks, q, kv, kv, m_all, m_all, m_all, m_all, gates, gates_t, brow, bcol, mnw, x, mod, mod, mod, norm_w,
      w_att, w_m)


ATT_S_BB = 16


def _attn_s_kernel(sink_ref, q_ref, kv32_ref, ck_ref, cv_ref, o_ref, ko_ref, vo_ref, q32_scr):
    t_new = DEC_SEQ
    w = WINDOW
    q32_scr[...] = q_ref[...].astype(F32)
    rows = 4 * 4 * 8
    row = lax.broadcasted_iota(jnp.int32, (rows, w), 0)
    slot = lax.broadcasted_iota(jnp.int32, (rows, w), 1)
    t_row = row % t_new
    second = (row % 8) >= t_new
    win_ok = (slot < w - t_new) | (slot - (w - t_new) <= t_row)
    old_ok = (slot >= 1) & (slot < t_new) & (slot > t_row)
    lane256 = lax.broadcasted_iota(jnp.int32, (32, 2 * LANES), 1)
    sink = sink_ref[...][:, 0:1]
    kv_new = jnp.concatenate([kv32_ref[...], jnp.zeros((w - ATT_S_BB * t_new, 2 * KV_WIDTH), F32)], axis=0)
    kv_t = kv_new.T
    new_slot = lax.broadcasted_iota(jnp.int32, (KV_WIDTH, w), 1) >= w - t_new
    for b in range(ATT_S_BB):
        cols = pltpu.roll(kv_t, w - t_new - t_new * b, 1)
        k_shift = pltpu.roll(ck_ref[b].reshape(KV_WIDTH, w), w - t_new, 1)
        v_shift = pltpu.roll(cv_ref[b].reshape(KV_WIDTH, w), w - t_new, 1)
        ko_ref[b] = jnp.where(new_slot, cols[:KV_WIDTH], k_shift).reshape(ATT_KV_HEADS, ATT_HEAD_DIM, w)
        vo_ref[b] = jnp.where(new_slot, cols[KV_WIDTH:], v_shift).reshape(ATT_KV_HEADS, ATT_HEAD_DIM, w)
    for pair in range(ATT_S_BB // 2):
        b0, b1 = 2 * pair, 2 * pair + 1
        q32 = jnp.concatenate([q32_scr[8 * pair:8 * (pair + 1), 256 * r:256 * (r + 1)] for r in range(ATT_GROUP)],
                              axis=0)
        qpad = jnp.concatenate(
            [jnp.where((lane256 // ATT_HEAD_DIM) == g, q32, 0.0) for g in range(ATT_KV_HEADS)], axis=0).astype(BF16)
        kw = [ko_ref[b].reshape(KV_WIDTH, w).astype(BF16) for b in (b0, b1)]
        vw = [vo_ref[b].reshape(KV_WIDTH, w).astype(BF16) for b in (b0, b1)]
        kc = [ck_ref[b].reshape(KV_WIDTH, w).astype(BF16) for b in (b0, b1)]
        vc = [cv_ref[b].reshape(KV_WIDTH, w).astype(BF16) for b in (b0, b1)]
        s_w = jnp.where(second, _dot(qpad, kw[1]), _dot(qpad, kw[0]))
        s_c = jnp.where(second, _dot(qpad, kc[1]), _dot(qpad, kc[0]))
        s_w = jnp.where(win_ok, s_w, -jnp.inf)
        s_c = jnp.where(old_ok, s_c, -jnp.inf)
        m = jnp.maximum(jnp.maximum(jnp.max(s_w, axis=-1, keepdims=True), jnp.max(s_c, axis=-1, keepdims=True)), sink)
        e_w = jnp.exp(s_w - m)
        e_c = jnp.exp(s_c - m)
        l = jnp.sum(e_w, axis=-1, keepdims=True) + jnp.sum(e_c, axis=-1, keepdims=True) + jnp.exp(sink - m)
        p_w = e_w / l
        p_c = e_c / l
        zero = jnp.zeros_like(p_w)
        o = (_dot_nt(jnp.where(second, zero, p_w).astype(BF16), vw[0])
             + _dot_nt(jnp.where(second, p_w, zero).astype(BF16), vw[1])
             + _dot_nt(jnp.where(second, zero, p_c).astype(BF16), vc[0])
             + _dot_nt(jnp.where(second, p_c, zero).astype(BF16), vc[1]))
        o32 = jnp.zeros((32, 2 * LANES), F32)
        for g in range(ATT_KV_HEADS):
            o32 = jnp.where((lane256 // ATT_HEAD_DIM) == g, o[32 * g:32 * (g + 1), :], o32)
        for r in range(ATT_GROUP):
            o_ref[8 * pair:8 * (pair + 1), 256 * r:256 * (r + 1)] = o32[8 * r:8 * (r + 1), :]


def _attn_s(sink_col, q, kv32, ck, cv):
    bb = ATT_S_BB
    rows = bb * DEC_SEQ
    cache_block = (bb, ATT_KV_HEADS, ATT_HEAD_DIM, WINDOW)
    cache_shape = (DEC_BATCH, ATT_KV_HEADS, ATT_HEAD_DIM, WINDOW)
    return pl.pallas_call(
        _attn_s_kernel,
        grid=(DEC_BATCH // bb,),
        in_specs=[
            pl.BlockSpec((128, LANES), lambda i: (0, 0)),
            pl.BlockSpec((rows, ATT_WIDTH), lambda i: (i, 0)),
            pl.BlockSpec((rows, 2 * KV_WIDTH), lambda i: (i, 0)),
            pl.BlockSpec(cache_block, lambda i: (i, 0, 0, 0)),
            pl.BlockSpec(cache_block, lambda i: (i, 0, 0, 0)),
        ],
        out_specs=[
            pl.BlockSpec((rows, ATT_WIDTH), lambda i: (i, 0)),
            pl.BlockSpec(cache_block, lambda i: (i, 0, 0, 0)),
            pl.BlockSpec(cache_block, lambda i: (i, 0, 0, 0)),
        ],
        out_shape=[
            jax.ShapeDtypeStruct((S_ROWS, ATT_WIDTH), F32),
            jax.ShapeDtypeStruct(cache_shape, F32),
            jax.ShapeDtypeStruct(cache_shape, F32),
        ],
        scratch_shapes=[pltpu.VMEM((rows, ATT_WIDTH), F32)],
        compiler_params=_cparams(("arbitrary",)),
        name="attn_s",
    )(sink_col, q, kv32, ck, cv)


def _head_norm_gate(h, nw, mo):
    hn = h * lax.rsqrt(jnp.mean(h * h, axis=-1, keepdims=True) + EPS) * nw
    return hn * _sigmoid(mo.astype(F32))


def _mlstm_chunk(q_ref, k_ref, v_ref, mo_ref, g_ref, gt_ref, brow_ref, bcol_ref, nw_ref, c_scr, n_scr, m_scr,
                 r0, store):
    L = MLSTM_CHUNK_P
    dh = MLSTM_HEAD_DIM
    rs = slice(r0, r0 + L)
    ti = lax.broadcasted_iota(jnp.int32, (L, L), 0)
    si = lax.broadcasted_iota(jnp.int32, (L, L), 1)
    causal = si <= ti
    tri = causal.astype(F32)
    tri_t = (ti <= si).astype(F32)
    gates = g_ref[rs, :] + brow_ref[...]
    gates_t = gt_ref[:, rs] + bcol_ref[...]
    b_col = jnp.dot(tri, _log_sigmoid(gates), precision=HI, preferred_element_type=F32)
    b_row = jnp.dot(_log_sigmoid(gates_t), tri_t, precision=HI, preferred_element_type=F32)
    for hd in range(MLSTM_HEADS):
        cs = slice(dh * hd, dh * (hd + 1))
        b_c = b_col[:, MLSTM_HEADS + hd:MLSTM_HEADS + hd + 1]
        li_c = gates[:, hd:hd + 1]
        b_r = b_row[MLSTM_HEADS + hd:MLSTM_HEADS + hd + 1, :]
        li_r = gates_t[hd:hd + 1, :]
        dm = jnp.where(causal, b_c - b_r + li_r, -jnp.inf)
        m_prev = m_scr[hd:hd + 1, 0:1]
        m_inter = b_c + m_prev
        m_t = jnp.maximum(m_inter, jnp.max(dm, axis=-1, keepdims=True))
        q = q_ref[rs, cs] * (dh ** -0.5)
        k = k_ref[rs, cs]
        v = v_ref[rs, cs]
        sm = _dot_nt(q, k) * jnp.exp(dm - m_t)
        a = jnp.exp(m_inter - m_t)
        c_old = c_scr[hd]
        n_old = n_scr[hd:hd + 1, :]
        num = a * _dot_nt(q, c_old.astype(BF16)) + _dot(sm.astype(BF16), v)
        qn = jnp.sum(q.astype(F32) * n_old, axis=-1, keepdims=True)
        den = a * qn + jnp.sum(sm, axis=-1, keepdims=True)
        h = num / jnp.maximum(jnp.abs(den), jnp.exp(-m_t))
        store(cs, _head_norm_gate(h, nw_ref[:, cs], mo_ref[rs, cs]).astype(BF16))
        m_new = m_t[L - 1:L, :]
        b_last = b_c[L - 1:L, :]
        g = jnp.exp(b_last - b_c + li_c - m_new)
        decay = jnp.exp(b_last + m_prev - m_new)
        gv = (g * v.astype(F32)).astype(BF16)
        c_scr[hd] = decay * c_old + _dot_tn(gv, k)
        n_scr[hd:hd + 1, :] = decay * n_old + jnp.sum(g * k.astype(F32), axis=0, keepdims=True)
        m_scr[hd:hd + 1, :] = jnp.broadcast_to(m_new, (1, LANES))


MLSTM_S_BB = 8


def _mlstm_s_kernel(q_ref, k_ref, v_ref, mo_ref, g_ref, nrep_ref, mrep_ref, brow_ref, nw_ref, c_ref,
                    mh_ref, c_out, nrow_ref, mrow_ref):
    T = DEC_SEQ
    R = MLSTM_S_BB * T
    H = MLSTM_HEADS
    dh = MLSTM_HEAD_DIM

    def shift(x, d):
        return pltpu.roll(x, d, 0)

    lanes = lax.broadcasted_iota(jnp.int32, (R, LANES), 1)
    tmod = lax.broadcasted_iota(jnp.int32, (R, LANES), 0) % T
    tmod_w = lax.broadcasted_iota(jnp.int32, (R, MLSTM_WIDTH), 0) % T
    head_ok = lanes < H
    gates = g_ref[...] + brow_ref[...]
    li = jnp.where(head_ok, gates, 0.0)
    lf = jnp.where(head_ok, pltpu.roll(_log_sigmoid(gates), LANES - H, 1), 0.0)
    bcum = lf
    for d in range(1, T):
        bcum = bcum + jnp.where(tmod >= d, shift(lf, d), 0.0)
    m0 = mrep_ref[...]
    m_inter = bcum + m0
    dms = [li] + [jnp.where(tmod >= d, bcum - shift(bcum, d) + shift(li, d), -jnp.inf) for d in range(1, T)]
    m_t = m_inter
    for dm in dms:
        m_t = jnp.maximum(m_t, dm)
    a = jnp.exp(m_inter - m_t)
    ws = [jnp.exp(dm - m_t) for dm in dms]

    q_bf = q_ref[...] * (dh ** -0.5)
    q = q_bf.astype(F32)
    k = k_ref[...].astype(F32)
    v = v_ref[...].astype(F32)
    seg = (lax.broadcasted_iota(jnp.int32, (MLSTM_WIDTH, LANES), 0) // dh
           == lax.broadcasted_iota(jnp.int32, (MLSTM_WIDTH, LANES), 1)).astype(F32)
    ex = (lax.broadcasted_iota(jnp.int32, (LANES, MLSTM_WIDTH), 1) // dh
          == lax.broadcasted_iota(jnp.int32, (LANES, MLSTM_WIDTH), 0)).astype(F32)

    def segsum(x):
        return jnp.dot(x, seg, precision=HI, preferred_element_type=F32)

    def expand(x):
        return jnp.dot(x, ex, precision=HI, preferred_element_type=F32)

    ks = [k] + [shift(k, d) for d in range(1, T)]
    vs = [v] + [shift(v, d) for d in range(1, T)]
    sms = [segsum(q * ks[d]) * ws[d] for d in range(T)]
    den = a * segsum(q * nrep_ref[...])
    for sm in sms:
        den = den + sm
    inv = 1.0 / jnp.maximum(jnp.abs(den), jnp.exp(-m_t))

    def last(x):
        out = jnp.zeros_like(x)
        for jj in range(T):
            out = jnp.where(tmod == T - 1 - jj, x if jj == 0 else pltpu.roll(x, R - jj, 0), out)
        return out

    m_new = last(m_t)
    b_last = last(bcum)
    g = jnp.where(head_ok, jnp.exp(b_last - bcum + li - m_new), 0.0)
    decay = jnp.where(head_ok, jnp.exp(b_last + m0 - m_new), 0.0)
    a_f = expand(a * inv)
    w_f = [expand(sm * inv) for sm in sms]
    g_f = expand(g)
    d_f = expand(decay)
    gv = (g_f * v).astype(BF16)
    rowb = lax.broadcasted_iota(jnp.int32, (R, dh), 0) // T
    for hd in range(H):
        cs = slice(dh * hd, dh * (hd + 1))
        qh = q_bf[:, cs]
        kh = k_ref[:, cs]
        gvh = gv[:, cs]
        qc = jnp.zeros((R, dh), F32)
        for bb in range(MLSTM_S_BB):
            c_old = c_ref[bb, hd]
            qc = jnp.where(rowb == bb, _dot_nt(qh, c_old.astype(BF16)), qc)
            upd = _dot_tn(jnp.where(rowb == bb, gvh, jnp.zeros_like(gvh)), kh)
            c_out[bb, hd] = d_f[T * bb:T * bb + 1, cs] * c_old + upd
        h = a_f[:, cs] * qc
        for d in range(T):
            h = h + w_f[d][:, cs] * vs[d][:, cs]
        mh_ref[:, cs] = _head_norm_gate(h, nw_ref[:, cs], mo_ref[:, cs]).astype(BF16)
    gk = g_f * k
    nsum = gk
    for d in range(1, T):
        nsum = nsum + jnp.where(tmod_w >= d, shift(gk, d), 0.0)
    nrow_ref[...] = d_f * nrep_ref[...] + nsum
    mrow_ref[...] = m_t


def _mlstm_s(m_all, gates, n_rep, m_rep, brow, nw, state_c):
    bb = MLSTM_S_BB
    R = bb * DEC_SEQ
    dh = MLSTM_HEAD_DIM
    H = MLSTM_HEADS
    return pl.pallas_call(
        _mlstm_s_kernel,
        grid=(DEC_BATCH // bb,),
        in_specs=[
            pl.BlockSpec((R, MLSTM_WIDTH), lambda i: (i, 0)),
            pl.BlockSpec((R, MLSTM_WIDTH), lambda i: (i, 1)),
            pl.BlockSpec((R, MLSTM_WIDTH), lambda i: (i, 2)),
            pl.BlockSpec((R, MLSTM_WIDTH), lambda i: (i, 3)),
            pl.BlockSpec((R, LANES), lambda i: (i, 0)),
            pl.BlockSpec((R, MLSTM_WIDTH), lambda i: (i, 0)),
            pl.BlockSpec((R, LANES), lambda i: (i, 0)),
            pl.BlockSpec((1, LANES), lambda i: (0, 0)),
            pl.BlockSpec((1, MLSTM_WIDTH), lambda i: (0, 0)),
            pl.BlockSpec((bb, H, dh, dh), lambda i: (i, 0, 0, 0)),
        ],
        out_specs=[
            pl.BlockSpec((R, MLSTM_WIDTH), lambda i: (i, 0)),
            pl.BlockSpec((bb, H, dh, dh), lambda i: (i, 0, 0, 0)),
            pl.BlockSpec((R, MLSTM_WIDTH), lambda i: (i, 0)),
            pl.BlockSpec((R, LANES), lambda i: (i, 0)),
        ],
        out_shape=[
            jax.ShapeDtypeStruct((S_ROWS, MLSTM_WIDTH), BF16),
            jax.ShapeDtypeStruct((DEC_BATCH, H, dh, dh), F32),
            jax.ShapeDtypeStruct((S_ROWS, MLSTM_WIDTH), F32),
            jax.ShapeDtypeStruct((S_ROWS, LANES), F32),
        ],
        compiler_params=_cparams(("arbitrary",)),
        name="mlstm_s",
    )(m_all, m_all, m_all, m_all, gates, n_rep, m_rep, brow, nw, state_c)


def _outproj_kernel(att_ref, mh_ref, x_ref, g1_ref, sh_ref, sc_ref, nw_ref, wa_ref, wm_ref, x1_ref, h2_ref, *,
                    per_row):
    y = _dot(att_ref[...].astype(BF16), wa_ref[...]) + _dot(mh_ref[...], wm_ref[...])
    x1 = x_ref[...] + _mod_row(g1_ref, per_row) * y
    x1_ref[...] = x1
    r = lax.rsqrt(jnp.mean(x1 * x1, axis=-1, keepdims=True) + EPS)
    h2 = (x1 * r * nw_ref[...]) * (1.0 + _mod_row(sc_ref, per_row)) + _mod_row(sh_ref, per_row)
    h2_ref[...] = h2.astype(BF16)


def _outproj(att, mh, x, mod, mod_row_block, norm_w, w_att, w_m, *, tm, per_row):
    rows = x.shape[0]
    mod_rows = tm if per_row else 8
    mod_idx = (lambda i: i) if per_row else (lambda i: mod_row_block)
    kern = functools.partial(_outproj_kernel, per_row=per_row)
    return pl.pallas_call(
        kern,
        grid=(rows // tm,),
        in_specs=[
            pl.BlockSpec((tm, ATT_WIDTH), lambda i: (i, 0)),
            pl.BlockSpec((tm, MLSTM_WIDTH), lambda i: (i, 0)),
            pl.BlockSpec((tm, D_MODEL), lambda i: (i, 0)),
            pl.BlockSpec((mod_rows, D_MODEL), lambda i: (mod_idx(i), 2)),
            pl.BlockSpec((mod_rows, D_MODEL), lambda i: (mod_idx(i), 3)),
            pl.BlockSpec((mod_rows, D_MODEL), lambda i: (mod_idx(i), 4)),
            pl.BlockSpec((1, D_MODEL), lambda i: (0, 0)),
            pl.BlockSpec((ATT_WIDTH, D_MODEL), lambda i: (0, 0)),
            pl.BlockSpec((MLSTM_WIDTH, D_MODEL), lambda i: (0, 0)),
        ],
        out_specs=[
            pl.BlockSpec((tm, D_MODEL), lambda i: (i, 0)),
            pl.BlockSpec((tm, D_MODEL), lambda i: (i, 0)),
        ],
        out_shape=[
            jax.ShapeDtypeStruct((rows, D_MODEL), F32),
            jax.ShapeDtypeStruct((rows, D_MODEL), BF16),
        ],
        compiler_params=_cparams(("arbitrary",)),
        name="outproj_s" if per_row else "outproj_p",
    )(att, mh, x, mod, mod, mod, norm_w, w_att, w_m)


def _ffn_a_kernel(hp_ref, hs_ref, wg_ref, wu_ref, ap_ref, as_ref, wg_scr, wu_scr):
    i = pl.program_id(1)

    def swiglu(h_ref, a_ref):
        h = h_ref[...]
        g = _dot(h, wg_scr[...])
        u = _dot(h, wu_scr[...])
        a_ref[...] = (g * _sigmoid(g) * u).astype(BF16)

    @pl.when(i == 0)
    def _():
        wg_scr[...] = wg_ref[...].astype(BF16)
        wu_scr[...] = wu_ref[...].astype(BF16)
        swiglu(hs_ref, as_ref)

    @pl.when(i > 0)
    def _():
        swiglu(hp_ref, ap_ref)


def _ffn_a(h2_p, h2_s, w_gate, w_up, *, tm):
    rows_p, rows_s = h2_p.shape[0], h2_s.shape[0]
    tn = 512
    ni = rows_p // tm
    tile = lambda i: jnp.maximum(i - 1, 0)
    return pl.pallas_call(
        _ffn_a_kernel,
        grid=(D_FF // tn, ni + 1),
        in_specs=[
            pl.BlockSpec((tm, D_MODEL), lambda j, i: (tile(i), 0)),
            pl.BlockSpec((rows_s, D_MODEL), lambda j, i: (0, 0)),
            pl.BlockSpec((D_MODEL, tn), lambda j, i: (0, j)),
            pl.BlockSpec((D_MODEL, tn), lambda j, i: (0, j)),
        ],
        out_specs=[
            pl.BlockSpec((tm, tn), lambda j, i: (tile(i), j)),
            pl.BlockSpec((rows_s, tn), lambda j, i: (0, j)),
        ],
        out_shape=[
            jax.ShapeDtypeStruct((rows_p, D_FF), BF16),
            jax.ShapeDtypeStruct((rows_s, D_FF), BF16),
        ],
        scratch_shapes=[pltpu.VMEM((D_MODEL, tn), BF16), pltpu.VMEM((D_MODEL, tn), BF16)],
        compiler_params=_cparams(("arbitrary", "arbitrary")),
        name="ffn_a",
    )(h2_p, h2_s, w_gate, w_up)


def _ffn_b_kernel(a_ref, wd_ref, x1_ref, g2_ref, fw_ref, y_ref, *rest, per_row, emit_w):
    kk = pl.program_id(1)

    def weight_tile():
        wt = wd_ref[...].astype(BF16)
        if emit_w:
            rest[0][...] = wt
        return wt

    @pl.when(kk == 0)
    def _():
        y_ref[...] = _dot(a_ref[...], weight_tile())

    last = pl.num_programs(1) - 1

    @pl.when((kk > 0) & (kk < last))
    def _():
        y_ref[...] += _dot(a_ref[...], weight_tile())

    @pl.when(kk == last)
    def _():
        x2 = x1_ref[...] + _mod_row(g2_ref, per_row) * (y_ref[...] + _dot(a_ref[...], weight_tile()))
        y_ref[...] = x2 * lax.rsqrt(jnp.mean(x2 * x2, axis=-1, keepdims=True) + EPS) * fw_ref[...]


def _ffn_b(a, w_down, x1, mod, mod_row_block, final_w, *, tm, tk, per_row, emit_w):
    rows = a.shape[0]
    assert D_FF // tk >= 2, "the first and the last k step are different code paths"
    mod_rows = tm if per_row else 8
    mod_idx = (lambda i: i) if per_row else (lambda i: mod_row_block)
    kern = functools.partial(_ffn_b_kernel, per_row=per_row, emit_w=emit_w)
    w_spec = pl.BlockSpec((tk, D_MODEL), lambda i, k: (k, 0))
    y_spec = pl.BlockSpec((tm, D_MODEL), lambda i, k: (i, 0))
    y_shape = jax.ShapeDtypeStruct((rows, D_MODEL), F32)
    return pl.pallas_call(
        kern,
        grid=(rows // tm, D_FF // tk),
        in_specs=[
            pl.BlockSpec((tm, tk), lambda i, k: (i, k)),
            w_spec,
            pl.BlockSpec((tm, D_MODEL), lambda i, k: (i, 0)),
            pl.BlockSpec((mod_rows, D_MODEL), lambda i, k: (mod_idx(i), 5)),
            pl.BlockSpec((1, D_MODEL), lambda i, k: (0, 0)),
        ],
        out_specs=[y_spec, w_spec] if emit_w else y_spec,
        out_shape=[y_shape, jax.ShapeDtypeStruct((D_FF, D_MODEL), BF16)] if emit_w else y_shape,
        compiler_params=_cparams(("arbitrary", "arbitrary")),
        name="ffn_b_s" if per_row else "ffn_b_p",
    )(a, w_down, x1, mod, final_w)


def _rope_tables(pos):
    half = ROPE_DIM // 2
    inv = np.float32(ROPE_THETA) ** (-np.arange(0, ROPE_DIM, 2, dtype=np.float32) / np.float32(ROPE_DIM))
    d = np.arange(LANES) % ATT_HEAD_DIM
    ang = pos.astype(np.float32)[:, None] * inv[d % half][None, :].astype(np.float32)
    cos, sin = np.cos(ang), np.sin(ang)
    d = d[None, :]
    tables = (np.where(d < ROPE_DIM, cos, 1.0), np.where(d < half, -sin, 0.0),
              np.where((d >= half) & (d < ROPE_DIM), sin, 0.0))
    return tuple(jnp.asarray(t.astype(np.float32)) for t in tables)


def kernel(x_prompt, x_sample, cache_k_win, cache_v_win, state_C, state_n, state_m, c_prompt, c_sample,
           norm1_w, norm2_w, final_norm_w, w_ada, b_ada, w_in, b_ig, b_fg, attn_sinks, mh_norm_w,
           w_out, w_gate, w_up, w_down):
    assert w_in.shape[0] == 1, "single-layer trunk"
    T = DEC_SEQ
    xp = x_prompt[0]
    xs = x_sample.reshape(S_ROWS, D_MODEL)

    c_all = jnp.concatenate([c_sample, c_prompt, jnp.zeros((15, D_MODEL), F32)], axis=0)
    mod = _ada(c_all, w_ada[0], b_ada)
    prompt_mod_block = S_ROWS // 8

    w_in_t = jnp.transpose(w_in[0])
    wq_t = (w_in_t[:ATT_WIDTH].reshape(ATT_KV_HEADS, ATT_GROUP, ATT_HEAD_DIM, D_MODEL)
            .transpose(1, 0, 2, 3).reshape(ATT_WIDTH, D_MODEL).astype(BF16))
    w_gates_t = w_in_t[MAIN_WIDTH:]
    wg = jnp.pad(w_gates_t, ((0, 16 - 2 * MLSTM_HEADS), (0, 0)))
    w_out_att = (w_out[0, :ATT_WIDTH].reshape(ATT_KV_HEADS, ATT_GROUP, ATT_HEAD_DIM, D_MODEL)
                 .transpose(1, 0, 2, 3).reshape(ATT_WIDTH, D_MODEL).astype(BF16))
    w_out_m = w_out[0, ATT_WIDTH:].astype(BF16)
    n1 = norm1_w.reshape(1, D_MODEL)
    n2 = norm2_w.reshape(1, D_MODEL)
    fw = final_norm_w.reshape(1, D_MODEL)
    nw = mh_norm_w.reshape(1, MLSTM_WIDTH)
    gate_bias = jnp.concatenate([b_ig[0], b_fg[0]])
    brow = jnp.pad(gate_bias, (0, LANES - 2 * MLSTM_HEADS)).reshape(1, LANES)
    bcol = jnp.broadcast_to(jnp.pad(gate_bias, (0, 16 - 2 * MLSTM_HEADS))[:, None], (16, MLSTM_CHUNK_P))

    rope_p = _rope_tables(np.arange(SEQ))
    rope_s = _rope_tables(np.tile(PAST_LEN + np.arange(T), DEC_BATCH))

    tm_p = 1024
    q_s, _, kv32_s, m_s, g_s, _, w_in_bf = _inproj(xs, mod, 0, n1, wq_t, w_in_t, wg, *rope_s,
                                                   tm=S_ROWS, per_row=True, emit_w=True)
    q_p, kv_p, kv32_p, m_p, g_p, gt_p = _inproj(xp, mod, prompt_mod_block, n1, wq_t, w_in_bf, wg, *rope_p,
                                                tm=tm_p, per_row=False, emit_w=False)

    sinks = attn_sinks[0]
    sink_col = jnp.broadcast_to(sinks.reshape(ATT_HEADS, 1, 1), (ATT_HEADS, 8, LANES)).reshape(128, LANES)
    ck = jnp.transpose(cache_k_win[0], (0, 2, 3, 1))
    cv = jnp.transpose(cache_v_win[0], (0, 2, 3, 1))
    att_s, kwin_s, vwin_s = _attn_s(sink_col, q_s, kv32_s, ck, cv)

    n_rep = jnp.repeat(state_n[0].reshape(DEC_BATCH, MLSTM_WIDTH), T, axis=0)
    m_rep = jnp.pad(jnp.repeat(state_m[0], T, axis=0), ((0, 0), (0, LANES - MLSTM_HEADS)))
    mh_s, c_s, nrow_s, mrow_s = _mlstm_s(m_s, g_s, n_rep, m_rep, brow, nw, state_C[0])

    x1_p, h2_p, c_p, n_p, mm_p = _mix_p(sinks, q_p, kv_p, m_p, g_p, gt_p, brow, bcol, nw, xp, mod, prompt_mod_block,
                                        n2, w_out_att, w_out_m)
    x1_s, h2_s = _outproj(att_s, mh_s, xs, mod, 0, n2, w_out_att, w_out_m, tm=S_ROWS // 2, per_row=True)
    a_p, a_s = _ffn_a(h2_p, h2_s, w_gate[0], w_up[0], tm=tm_p)
    y_s, w_down_bf = _ffn_b(a_s, w_down[0], x1_s, mod, 0, fw, tm=S_ROWS, tk=1408, per_row=True, emit_w=True)
    y_p = _ffn_b(a_p, w_down_bf, x1_p, mod, prompt_mod_block, fw, tm=tm_p, tk=512, per_row=False, emit_w=False)

    kv_shape = (1, 1, WINDOW, ATT_KV_HEADS, ATT_HEAD_DIM)
    kv_last = kv32_p[tm_p - WINDOW:]
    dh = MLSTM_HEAD_DIM
    return (
        y_p.reshape(1, SEQ, D_MODEL),
        y_s.reshape(DEC_BATCH, T, D_MODEL),
        kv_last[:, :KV_WIDTH].reshape(kv_shape),
        kv_last[:, KV_WIDTH:].reshape(kv_shape),
        c_p.reshape(1, 1, MLSTM_HEADS, dh, dh),
        n_p[:MLSTM_HEADS].reshape(1, 1, MLSTM_HEADS, dh),
        mm_p[:MLSTM_HEADS, 0].reshape(1, 1, MLSTM_HEADS),
        jnp.transpose(kwin_s, (0, 3, 1, 2))[None],
        jnp.transpose(vwin_s, (0, 3, 1, 2))[None],
        c_s.reshape(1, DEC_BATCH, MLSTM_HEADS, dh, dh),
        nrow_s[T - 1::T].reshape(1, DEC_BATCH, MLSTM_HEADS, dh),
        mrow_s[T - 1::T, :MLSTM_HEADS].reshape(1, DEC_BATCH, MLSTM_HEADS),
    )
```

```python
import functools

import jax
import jax.numpy as jnp
import numpy as np
from jax import lax
from jax.experimental import pallas as pl
from jax.experimental.pallas import tpu as pltpu

F32 = jnp.float32
BF16 = jnp.bfloat16

D_MODEL = 2048
SEQ = 8192
DEC_BATCH = 128
DEC_SEQ = 4
S_ROWS = DEC_BATCH * DEC_SEQ
PAST_LEN = 16384
ATT_HEADS = 16
ATT_KV_HEADS = 4
ATT_GROUP = 4
ATT_HEAD_DIM = 64
WINDOW = 128
ROPE_THETA = 500000.0
ROPE_DIM = 16
MLSTM_HEADS = 4
MLSTM_HEAD_DIM = 256
ATT_WIDTH = 1024
KV_WIDTH = 256
MLSTM_WIDTH = 1024
MAIN_WIDTH = ATT_WIDTH + 2 * KV_WIDTH + 4 * MLSTM_WIDTH
D_FF = 5632
EPS = 1e-6

LANES = 128
MLSTM_CHUNK_P = 256
VMEM_LIMIT = 56 * 1024 * 1024

NT_DIMS = (((1,), (1,)), ((), ()))
TN_DIMS = (((0,), (0,)), ((), ()))
HI = lax.Precision.HIGHEST


def _cparams(sem):
    return pltpu.CompilerParams(dimension_semantics=sem, vmem_limit_bytes=VMEM_LIMIT)


def _dot(a, b):
    return jnp.dot(a, b, preferred_element_type=F32)


def _dot_nt(a, b):
    return lax.dot_general(a, b, NT_DIMS, preferred_element_type=F32)


def _dot_tn(a, b):
    return lax.dot_general(a, b, TN_DIMS, preferred_element_type=F32)


def _sigmoid(x):
    return 1.0 / (1.0 + jnp.exp(-x))


def _log_sigmoid(x):
    return jnp.minimum(x, 0.0) - jnp.log(1.0 + jnp.exp(-jnp.abs(x)))


def _mod_row(ref, per_row):
    return ref[...] if per_row else ref[0:1, :]


ADA_BUFS = 3


def _ada_kernel(c_ref, w_hbm, b_ref, o_ref, s_scr, rep_scr, w_buf, sem):
    j = pl.program_id(0)
    n = pl.num_programs(0)
    tn = o_ref.shape[1]

    def tile_copy(t):
        col = pl.multiple_of(t * tn, tn)
        return pltpu.make_async_copy(w_hbm.at[:, pl.ds(col, tn)], w_buf.at[t % ADA_BUFS], sem.at[t % ADA_BUFS])

    @pl.when(j == 0)
    def _():
        for t in range(ADA_BUFS - 1):
            tile_copy(t).start()
        c = c_ref[...]
        s_scr[...] = (c * _sigmoid(c)).astype(BF16)

    @pl.when(j + ADA_BUFS - 1 < n)
    def _():
        tile_copy(j + ADA_BUFS - 1).start()

    tile_copy(j).wait()
    mod = _dot(s_scr[...], w_buf[j % ADA_BUFS].astype(BF16)) + b_ref[...]
    for c in range(o_ref.shape[1] // LANES):
        cols = slice(LANES * c, LANES * (c + 1))
        for t in range(DEC_SEQ):
            rep_scr[pl.ds(t, DEC_BATCH, stride=DEC_SEQ), :] = mod[0:DEC_BATCH, cols]
        o_ref[0:S_ROWS, cols] = rep_scr[...]
    o_ref[S_ROWS:, :] = mod[DEC_BATCH:]


def _ada(c_all, w_ada, b_ada):
    m = c_all.shape[0]
    m_out = S_ROWS + m - DEC_BATCH
    n = w_ada.shape[1]
    tn = 1024
    return pl.pallas_call(
        _ada_kernel,
        grid=(n // tn,),
        in_specs=[
            pl.BlockSpec((m, D_MODEL), lambda j: (0, 0)),
            pl.BlockSpec(memory_space=pl.ANY),
            pl.BlockSpec((1, tn), lambda j: (0, j)),
        ],
        out_specs=pl.BlockSpec((m_out, tn), lambda j: (0, j)),
        out_shape=jax.ShapeDtypeStruct((m_out, n), F32),
        scratch_shapes=[pltpu.VMEM((m, D_MODEL), BF16), pltpu.VMEM((S_ROWS, LANES), F32),
                        pltpu.VMEM((ADA_BUFS, D_MODEL, tn), F32), pltpu.SemaphoreType.DMA((ADA_BUFS,))],
        compiler_params=_cparams(("arbitrary",)),
        name="ada",
    )(c_all, w_ada, b_ada)


def _rope_store(acc, cos, sa, sb, out_ref, ncols, scale):
    for c in range(ncols // LANES):
        xc = acc[:, LANES * c:LANES * (c + 1)]
        rot = xc * cos + pltpu.roll(xc, LANES - 8, 1) * sa + pltpu.roll(xc, 8, 1) * sb
        if scale != 1.0:
            rot = rot * scale
        out_ref[:, LANES * c:LANES * (c + 1)] = rot.astype(out_ref.dtype)


def _inproj_kernel(x_ref, sh_ref, sc_ref, nw_ref, wq_ref, win_ref, wg_ref, cos_ref, sa_ref, sb_ref,
                   q_ref, kv_ref, kv32_ref, m_ref, g_ref, gt_ref, *rest, per_row, emit_w):
    h_scr = rest[-1]
    j = pl.program_id(1)

    def weight_tile():
        wt = win_ref[...].astype(BF16)
        if emit_w:
            rest[0][...] = wt
        return wt

    def q_tile():
        acc = _dot_nt(h_scr[...], wq_ref[...])
        _rope_store(acc, cos_ref[...], sa_ref[...], sb_ref[...], q_ref, 512, ATT_HEAD_DIM ** -0.5)

    @pl.when(j == 0)
    def _():
        x = x_ref[...]
        r = lax.rsqrt(jnp.mean(x * x, axis=-1, keepdims=True) + EPS)
        gain = nw_ref[...] * (1.0 + _mod_row(sc_ref, per_row))
        h = (x * r * gain + _mod_row(sh_ref, per_row)).astype(BF16)
        h_scr[...] = h
        wg = jnp.concatenate([wg_ref[...].astype(BF16), jnp.zeros((LANES - 16, D_MODEL), BF16)], axis=0)
        g = _dot_nt(h, wg)
        g_ref[...] = g
        gt_ref[...] = g.T[0:16, :]
        q_tile()

    @pl.when(j == 1)
    def _():
        q_tile()

    @pl.when(j == 2)
    def _():
        acc = _dot_nt(h_scr[...], weight_tile())
        _rope_store(acc, cos_ref[...], sa_ref[...], sb_ref[...], kv32_ref, KV_WIDTH, 1.0)
        kv32_ref[:, KV_WIDTH:] = acc[:, KV_WIDTH:]
        kv_ref[...] = kv32_ref[...].astype(BF16)

    @pl.when(j > 2)
    def _():
        m_ref[...] = _dot_nt(h_scr[...], weight_tile()).astype(BF16)


def _inproj(x, mod, mod_row_block, norm_w, wq_t, w_in_t, wg, cos, sa, sb, *, tm, per_row, emit_w):
    rows = x.shape[0]
    tn = 512
    nj = MAIN_WIDTH // tn
    mod_rows = tm if per_row else 8
    mod_idx = (lambda i: i) if per_row else (lambda i: mod_row_block)
    kern = functools.partial(_inproj_kernel, per_row=per_row, emit_w=emit_w)
    copy_spec = pl.BlockSpec((tn, D_MODEL), lambda i, j: (jnp.maximum(j, 2) - 2, 0))
    w_spec = pl.BlockSpec((tn, D_MODEL), lambda i, j: (jnp.maximum(j, 2), 0)) if emit_w else copy_spec
    extra_specs = [copy_spec] if emit_w else []
    extra_shapes = [jax.ShapeDtypeStruct((MAIN_WIDTH - ATT_WIDTH, D_MODEL), BF16)] if emit_w else []
    return pl.pallas_call(
        kern,
        grid=(rows // tm, nj),
        in_specs=[
            pl.BlockSpec((tm, D_MODEL), lambda i, j: (jnp.minimum(i + jnp.minimum(j, 1), rows // tm - 1), 0)),
            pl.BlockSpec((mod_rows, D_MODEL), lambda i, j: (mod_idx(i), 0)),
            pl.BlockSpec((mod_rows, D_MODEL), lambda i, j: (mod_idx(i), 1)),
            pl.BlockSpec((1, D_MODEL), lambda i, j: (0, 0)),
            pl.BlockSpec((tn, D_MODEL), lambda i, j: (jnp.minimum(j, 1), 0)),
            w_spec,
            pl.BlockSpec((16, D_MODEL), lambda i, j: (0, 0)),
            pl.BlockSpec((tm, LANES), lambda i, j: (i, 0)),
            pl.BlockSpec((tm, LANES), lambda i, j: (i, 0)),
            pl.BlockSpec((tm, LANES), lambda i, j: (i, 0)),
        ],
        out_specs=[
            pl.BlockSpec((tm, tn), lambda i, j: (i, jnp.minimum(j, 1))),
            pl.BlockSpec((tm, tn), lambda i, j: (i, 0)),
            pl.BlockSpec((tm, tn), lambda i, j: (0, 0)),
            pl.BlockSpec((tm, tn), lambda i, j: (i, jnp.clip(j - 3, 0, 7))),
            pl.BlockSpec((tm, LANES), lambda i, j: (i, 0)),
            pl.BlockSpec((16, tm), lambda i, j: (0, i)),
        ] + extra_specs,
        out_shape=[
            jax.ShapeDtypeStruct((rows, ATT_WIDTH), BF16),
            jax.ShapeDtypeStruct((rows, 2 * KV_WIDTH), BF16),
            jax.ShapeDtypeStruct((tm, 2 * KV_WIDTH), F32),
            jax.ShapeDtypeStruct((rows, 4 * MLSTM_WIDTH), BF16),
            jax.ShapeDtypeStruct((rows, LANES), F32),
            jax.ShapeDtypeStruct((16, rows), F32),
        ] + extra_shapes,
        scratch_shapes=[pltpu.VMEM((tm, D_MODEL), BF16)],
        compiler_params=_cparams(("arbitrary", "arbitrary")),
        name="inproj_s" if per_row else "inproj_p",
    )(x, mod, mod, norm_w, wq_t, w_in_t, wg, cos, sa, sb)


def _attn_block(sink_ref, q_ref, row0, kv2, allowed, store):
    w = WINDOW
    grp = ATT_GROUP
    rows = grp * w
    member = lax.broadcasted_iota(jnp.int32, (rows, 1), 0) // w
    low = lax.broadcasted_iota(jnp.int32, (2 * w, LANES), 1) < ATT_HEAD_DIM
    low_o = lax.broadcasted_iota(jnp.int32, (rows, LANES), 1) < ATT_HEAD_DIM
    key_row = lax.broadcasted_iota(jnp.int32, (4 * w, LANES), 0)
    key_lane = lax.broadcasted_iota(jnp.int32, (4 * w, LANES), 1)
    ones_bd = (((key_row < 2 * w) & (key_lane < ATT_HEAD_DIM))
               | ((key_row >= 2 * w) & (key_lane >= ATT_HEAD_DIM))).astype(BF16)
    zero = jnp.zeros((2 * w, LANES), BF16)
    for cp in range(2):
        k128 = kv2[:, LANES * cp:LANES * (cp + 1)]
        v128 = kv2[:, KV_WIDTH + LANES * cp:KV_WIDTH + LANES * (cp + 1)]
        kbd = jnp.concatenate([jnp.where(low, k128, zero), jnp.where(low, zero, k128)], axis=0)
        vbd = jnp.concatenate([jnp.where(low, v128, zero), jnp.where(low, zero, v128)], axis=0)
        v_aug = jnp.concatenate([vbd, ones_bd], axis=1)
        q4 = jnp.concatenate([q_ref[row0:row0 + w, 256 * r + LANES * cp:256 * r + LANES * (cp + 1)]
                              for r in range(grp)], axis=0)
        s = _dot_nt(q4, kbd)
        es, tails = [], []
        for half in range(2):
            sh = jnp.where(allowed, s[:, 2 * w * half:2 * w * (half + 1)], -jnp.inf)
            head0 = (2 * cp + half) * grp
            sink = jnp.full((rows, 1), sink_ref[head0], F32)
            for r in range(1, grp):
                sink = jnp.where(member == r, sink_ref[head0 + r], sink)
            m = jnp.maximum(jnp.max(sh, axis=-1, keepdims=True), sink)
            es.append(jnp.exp(sh - m).astype(BF16))
            tails.append(jnp.exp(sink - m))
        oa = _dot(jnp.concatenate(es, axis=1), v_aug)
        l = oa[:, LANES:] + jnp.where(low_o, tails[0], tails[1])
        o = (oa[:, :LANES] / l).astype(BF16)
        for r in range(grp):
            store(256 * r + LANES * cp, o[w * r:w * (r + 1)])


MIX_TM = 512


def _mix_p_kernel(sink_ref, q_ref, kvp_ref, kvc_ref, mq_ref, mk_ref, mv_ref, mo_ref, g_ref, gt_ref, brow_ref, bcol_ref,
                  mnw_ref, x_ref, g1_ref, sh_ref, sc_ref, nw_ref, wa_ref, wm_ref,
                  x1_ref, h2_ref, c_out, n_out, m_out, att_scr, mh_scr, c_scr, n_scr, m_scr):
    s = pl.program_id(0)
    w = WINDOW
    tiles = pl.num_programs(0) - 1
    slot = s % 2

    @pl.when(s == 0)
    def _():
        c_scr[...] = jnp.zeros_like(c_scr)
        n_scr[...] = jnp.zeros_like(n_scr)
        m_scr[...] = jnp.zeros_like(m_scr)

    def project():
        y = _dot(att_scr[1 - slot], wa_ref[...]) + _dot(mh_scr[1 - slot], wm_ref[...])
        x1 = x_ref[...] + g1_ref[0:1, :] * y
        x1_ref[...] = x1
        r = lax.rsqrt(jnp.mean(x1 * x1, axis=-1, keepdims=True) + EPS)
        gain = nw_ref[...] * (1.0 + sc_ref[0:1, :])
        h2_ref[...] = (x1 * r * gain + sh_ref[0:1, :]).astype(BF16)

    def mixers():
        rows = ATT_GROUP * w
        qi = lax.broadcasted_iota(jnp.int32, (rows, 2 * w), 0) % w
        kj = lax.broadcasted_iota(jnp.int32, (rows, 2 * w), 1)
        first_off = jnp.where(s > 0, 0, 4 * w)
        causal = (kj >= w) & (kj - w <= qi)
        for blk in range(MIX_TM // w):
            prev = kvp_ref[...] if blk == 0 else kvc_ref[w * (blk - 1):w * blk, :]
            kv2 = jnp.concatenate([prev, kvc_ref[w * blk:w * (blk + 1), :]], axis=0)
            allowed = ((kj < w) & (kj > qi + (first_off if blk == 0 else 0))) | causal

            def store(c0, val, blk=blk):
                att_scr[slot, w * blk:w * (blk + 1), c0:c0 + LANES] = val

            _attn_block(sink_ref, q_ref, w * blk, kv2, allowed, store)

        for ch in range(MIX_TM // MLSTM_CHUNK_P):
            r0 = MLSTM_CHUNK_P * ch

            def store_mh(cs, val, r0=r0):
                mh_scr[slot, r0:r0 + MLSTM_CHUNK_P, cs] = val

            _mlstm_chunk(mq_ref, mk_ref, mv_ref, mo_ref, g_ref, gt_ref, brow_ref, bcol_ref, mnw_ref,
                         c_scr, n_scr, m_scr, r0, store_mh)

    @pl.when(s == 0)
    def _():
        mixers()

    @pl.when((s > 0) & (s < tiles))
    def _():
        project()
        mixers()

    @pl.when(s == tiles)
    def _():
        project()
        c_out[...] = c_scr[...]
        n_out[...] = n_scr[...]
        m_out[...] = m_scr[...]


def _mix_p(sinks, q, kv, m_all, gates, gates_t, brow, bcol, mnw, x, mod, mod_row_block, norm_w, w_att, w_m):
    tm = MIX_TM
    tiles = SEQ // tm
    bpt = tm // WINDOW
    dh = MLSTM_HEAD_DIM
    att_tile = lambda s: jnp.minimum(s, tiles - 1)
    out_tile = lambda s: jnp.maximum(s - 1, 0)
    m_spec = lambda col: pl.BlockSpec((tm, MLSTM_WIDTH), lambda s: (att_tile(s), col))
    return pl.pallas_call(
        _mix_p_kernel,
        grid=(tiles + 1,),
        in_specs=[
            pl.BlockSpec(memory_space=pltpu.SMEM),
            pl.BlockSpec((tm, ATT_WIDTH), lambda s: (att_tile(s), 0)),
            pl.BlockSpec((WINDOW, 2 * KV_WIDTH), lambda s: (jnp.maximum(bpt * att_tile(s) - 1, 0), 0)),
            pl.BlockSpec((tm, 2 * KV_WIDTH), lambda s: (att_tile(s), 0)),
            m_spec(0), m_spec(1), m_spec(2), m_spec(3),
            pl.BlockSpec((tm, LANES), lambda s: (att_tile(s), 0)),
            pl.BlockSpec((16, tm), lambda s: (0, att_tile(s))),
            pl.BlockSpec((1, LANES), lambda s: (0, 0)),
            pl.BlockSpec((16, MLSTM_CHUNK_P), lambda s: (0, 0)),
            pl.BlockSpec((1, MLSTM_WIDTH), lambda s: (0, 0)),
            pl.BlockSpec((tm, D_MODEL), lambda s: (out_tile(s), 0)),
            pl.BlockSpec((8, D_MODEL), lambda s: (mod_row_block, 2)),
            pl.BlockSpec((8, D_MODEL), lambda s: (mod_row_block, 3)),
            pl.BlockSpec((8, D_MODEL), lambda s: (mod_row_block, 4)),
            pl.BlockSpec((1, D_MODEL), lambda s: (0, 0)),
            pl.BlockSpec((ATT_WIDTH, D_MODEL), lambda s: (0, 0)),
            pl.BlockSpec((MLSTM_WIDTH, D_MODEL), lambda s: (0, 0)),
        ],
        out_specs=[
            pl.BlockSpec((tm, D_MODEL), lambda s: (out_tile(s), 0)),
            pl.BlockSpec((tm, D_MODEL), lambda s: (out_tile(s), 0)),
            pl.BlockSpec((MLSTM_HEADS, dh, dh), lambda s: (0, 0, 0)),
            pl.BlockSpec((8, dh), lambda s: (0, 0)),
            pl.BlockSpec((8, LANES), lambda s: (0, 0)),
        ],
        out_shape=[
            jax.ShapeDtypeStruct((SEQ, D_MODEL), F32),
            jax.ShapeDtypeStruct((SEQ, D_MODEL), BF16),
            jax.ShapeDtypeStruct((MLSTM_HEADS, dh, dh), F32),
            jax.ShapeDtypeStruct((8, dh), F32),
            jax.ShapeDtypeStruct((8, LANES), F32),
        ],
        scratch_shapes=[
            pltpu.VMEM((2, tm, ATT_WIDTH), BF16),
            pltpu.VMEM((2, tm, MLSTM_WIDTH), BF16),
            pltpu.VMEM((MLSTM_HEADS, dh, dh), F32),
            pltpu.VMEM((8, dh), F32),
            pltpu.VMEM((8, LANES), F32),
        ],
        compiler_params=_cparams(("arbitrary",)),
        name="mix_p",
    )(sinks, q, kv, kv, m_all, m_all, m_all, m_all, gates, gates_t, brow, bcol, mnw, x, mod, mod, mod, norm_w,
      w_att, w_m)


ATT_S_BB = 16


def _attn_s_kernel(sink_ref, q_ref, kv32_ref, ck_ref, cv_ref, o_ref, ko_ref, vo_ref, q32_scr):
    t_new = DEC_SEQ
    w = WINDOW
    q32_scr[...] = q_ref[...].astype(F32)
    rows = 4 * 4 * 8
    row = lax.broadcasted_iota(jnp.int32, (rows, w), 0)
    slot = lax.broadcasted_iota(jnp.int32, (rows, w), 1)
    t_row = row % t_new
    second = (row % 8) >= t_new
    win_ok = (slot < w - t_new) | (slot - (w - t_new) <= t_row)
    old_ok = (slot >= 1) & (slot < t_new) & (slot > t_row)
    lane256 = lax.broadcasted_iota(jnp.int32, (32, 2 * LANES), 1)
    sink = sink_ref[...][:, 0:1]
    kv_new = jnp.concatenate([kv32_ref[...], jnp.zeros((w - ATT_S_BB * t_new, 2 * KV_WIDTH), F32)], axis=0)
    kv_t = kv_new.T
    new_slot = lax.broadcasted_iota(jnp.int32, (KV_WIDTH, w), 1) >= w - t_new
    for b in range(ATT_S_BB):
        cols = pltpu.roll(kv_t, w - t_new - t_new * b, 1)
        k_shift = pltpu.roll(ck_ref[b].reshape(KV_WIDTH, w), w - t_new, 1)
        v_shift = pltpu.roll(cv_ref[b].reshape(KV_WIDTH, w), w - t_new, 1)
        ko_ref[b] = jnp.where(new_slot, cols[:KV_WIDTH], k_shift).reshape(ATT_KV_HEADS, ATT_HEAD_DIM, w)
        vo_ref[b] = jnp.where(new_slot, cols[KV_WIDTH:], v_shift).reshape(ATT_KV_HEADS, ATT_HEAD_DIM, w)
    for pair in range(ATT_S_BB // 2):
        b0, b1 = 2 * pair, 2 * pair + 1
        q32 = jnp.concatenate([q32_scr[8 * pair:8 * (pair + 1), 256 * r:256 * (r + 1)] for r in range(ATT_GROUP)],
                              axis=0)
        qpad = jnp.concatenate(
            [jnp.where((lane256 // ATT_HEAD_DIM) == g, q32, 0.0) for g in range(ATT_KV_HEADS)], axis=0).astype(BF16)
        kw = [ko_ref[b].reshape(KV_WIDTH, w).astype(BF16) for b in (b0, b1)]
        vw = [vo_ref[b].reshape(KV_WIDTH, w).astype(BF16) for b in (b0, b1)]
        kc = [ck_ref[b].reshape(KV_WIDTH, w).astype(BF16) for b in (b0, b1)]
        vc = [cv_ref[b].reshape(KV_WIDTH, w).astype(BF16) for b in (b0, b1)]
        s_w = jnp.where(second, _dot(qpad, kw[1]), _dot(qpad, kw[0]))
        s_c = jnp.where(second, _dot(qpad, kc[1]), _dot(qpad, kc[0]))
        s_w = jnp.where(win_ok, s_w, -jnp.inf)
        s_c = jnp.where(old_ok, s_c, -jnp.inf)
        m = jnp.maximum(jnp.maximum(jnp.max(s_w, axis=-1, keepdims=True), jnp.max(s_c, axis=-1, keepdims=True)), sink)
        e_w = jnp.exp(s_w - m)
        e_c = jnp.exp(s_c - m)
        l = jnp.sum(e_w, axis=-1, keepdims=True) + jnp.sum(e_c, axis=-1, keepdims=True) + jnp.exp(sink - m)
        p_w = e_w / l
        p_c = e_c / l
        zero = jnp.zeros_like(p_w)
        o = (_dot_nt(jnp.where(second, zero, p_w).astype(BF16), vw[0])
             + _dot_nt(jnp.where(second, p_w, zero).astype(BF16), vw[1])
             + _dot_nt(jnp.where(second, zero, p_c).astype(BF16), vc[0])
             + _dot_nt(jnp.where(second, p_c, zero).astype(BF16), vc[1]))
        o32 = jnp.zeros((32, 2 * LANES), F32)
        for g in range(ATT_KV_HEADS):
            o32 = jnp.where((lane256 // ATT_HEAD_DIM) == g, o[32 * g:32 * (g + 1), :], o32)
        for r in range(ATT_GROUP):
            o_ref[8 * pair:8 * (pair + 1), 256 * r:256 * (r + 1)] = o32[8 * r:8 * (r + 1), :]


def _attn_s(sink_col, q, kv32, ck, cv):
    bb = ATT_S_BB
    rows = bb * DEC_SEQ
    cache_block = (bb, ATT_KV_HEADS, ATT_HEAD_DIM, WINDOW)
    cache_shape = (DEC_BATCH, ATT_KV_HEADS, ATT_HEAD_DIM, WINDOW)
    return pl.pallas_call(
        _attn_s_kernel,
        grid=(DEC_BATCH // bb,),
        in_specs=[
            pl.BlockSpec((128, LANES), lambda i: (0, 0)),
            pl.BlockSpec((rows, ATT_WIDTH), lambda i: (i, 0)),
            pl.BlockSpec((rows, 2 * KV_WIDTH), lambda i: (i, 0)),
            pl.BlockSpec(cache_block, lambda i: (i, 0, 0, 0)),
            pl.BlockSpec(cache_block, lambda i: (i, 0, 0, 0)),
        ],
        out_specs=[
            pl.BlockSpec((rows, ATT_WIDTH), lambda i: (i, 0)),
            pl.BlockSpec(cache_block, lambda i: (i, 0, 0, 0)),
            pl.BlockSpec(cache_block, lambda i: (i, 0, 0, 0)),
        ],
        out_shape=[
            jax.ShapeDtypeStruct((S_ROWS, ATT_WIDTH), F32),
            jax.ShapeDtypeStruct(cache_shape, F32),
            jax.ShapeDtypeStruct(cache_shape, F32),
        ],
        scratch_shapes=[pltpu.VMEM((rows, ATT_WIDTH), F32)],
        compiler_params=_cparams(("arbitrary",)),
        name="attn_s",
    )(sink_col, q, kv32, ck, cv)


def _head_norm_gate(h, nw, mo):
    hn = h * lax.rsqrt(jnp.mean(h * h, axis=-1, keepdims=True) + EPS) * nw
    return hn * _sigmoid(mo.astype(F32))


def _mlstm_chunk(q_ref, k_ref, v_ref, mo_ref, g_ref, gt_ref, brow_ref, bcol_ref, nw_ref, c_scr, n_scr, m_scr,
                 r0, store):
    L = MLSTM_CHUNK_P
    dh = MLSTM_HEAD_DIM
    rs = slice(r0, r0 + L)
    ti = lax.broadcasted_iota(jnp.int32, (L, L), 0)
    si = lax.broadcasted_iota(jnp.int32, (L, L), 1)
    causal = si <= ti
    tri = causal.astype(F32)
    tri_t = (ti <= si).astype(F32)
    gates = g_ref[rs, :] + brow_ref[...]
    gates_t = gt_ref[:, rs] + bcol_ref[...]
    b_col = jnp.dot(tri, _log_sigmoid(gates), precision=HI, preferred_element_type=F32)
    b_row = jnp.dot(_log_sigmoid(gates_t), tri_t, precision=HI, preferred_element_type=F32)
    for hd in range(MLSTM_HEADS):
        cs = slice(dh * hd, dh * (hd + 1))
        b_c = b_col[:, MLSTM_HEADS + hd:MLSTM_HEADS + hd + 1]
        li_c = gates[:, hd:hd + 1]
        b_r = b_row[MLSTM_HEADS + hd:MLSTM_HEADS + hd + 1, :]
        li_r = gates_t[hd:hd + 1, :]
        dm = jnp.where(causal, b_c - b_r + li_r, -jnp.inf)
        m_prev = m_scr[hd:hd + 1, 0:1]
        m_inter = b_c + m_prev
        m_t = jnp.maximum(m_inter, jnp.max(dm, axis=-1, keepdims=True))
        q = q_ref[rs, cs] * (dh ** -0.5)
        k = k_ref[rs, cs]
        v = v_ref[rs, cs]
        sm = _dot_nt(q, k) * jnp.exp(dm - m_t)
        a = jnp.exp(m_inter - m_t)
        c_old = c_scr[hd]
        n_old = n_scr[hd:hd + 1, :]
        num = a * _dot_nt(q, c_old.astype(BF16)) + _dot(sm.astype(BF16), v)
        qn = jnp.sum(q.astype(F32) * n_old, axis=-1, keepdims=True)
        den = a * qn + jnp.sum(sm, axis=-1, keepdims=True)
        h = num / jnp.maximum(jnp.abs(den), jnp.exp(-m_t))
        store(cs, _head_norm_gate(h, nw_ref[:, cs], mo_ref[rs, cs]).astype(BF16))
        m_new = m_t[L - 1:L, :]
        b_last = b_c[L - 1:L, :]
        g = jnp.exp(b_last - b_c + li_c - m_new)
        decay = jnp.exp(b_last + m_prev - m_new)
        gv = (g * v.astype(F32)).astype(BF16)
        c_scr[hd] = decay * c_old + _dot_tn(gv, k)
        n_scr[hd:hd + 1, :] = decay * n_old + jnp.sum(g * k.astype(F32), axis=0, keepdims=True)
        m_scr[hd:hd + 1, :] = jnp.broadcast_to(m_new, (1, LANES))


MLSTM_S_BB = 8


def _mlstm_s_kernel(q_ref, k_ref, v_ref, mo_ref, g_ref, nrep_ref, mrep_ref, brow_ref, nw_ref, c_ref,
                    mh_ref, c_out, nrow_ref, mrow_ref):
    T = DEC_SEQ
    R = MLSTM_S_BB * T
    H = MLSTM_HEADS
    dh = MLSTM_HEAD_DIM

    def shift(x, d):
        return pltpu.roll(x, d, 0)

    lanes = lax.broadcasted_iota(jnp.int32, (R, LANES), 1)
    tmod = lax.broadcasted_iota(jnp.int32, (R, LANES), 0) % T
    tmod_w = lax.broadcasted_iota(jnp.int32, (R, MLSTM_WIDTH), 0) % T
    head_ok = lanes < H
    gates = g_ref[...] + brow_ref[...]
    li = jnp.where(head_ok, gates, 0.0)
    lf = jnp.where(head_ok, pltpu.roll(_log_sigmoid(gates), LANES - H, 1), 0.0)
    bcum = lf
    for d in range(1, T):
        bcum = bcum + jnp.where(tmod >= d, shift(lf, d), 0.0)
    m0 = mrep_ref[...]
    m_inter = bcum + m0
    dms = [li] + [jnp.where(tmod >= d, bcum - shift(bcum, d) + shift(li, d), -jnp.inf) for d in range(1, T)]
    m_t = m_inter
    for dm in dms:
        m_t = jnp.maximum(m_t, dm)
    a = jnp.exp(m_inter - m_t)
    ws = [jnp.exp(dm - m_t) for dm in dms]

    q_bf = q_ref[...] * (dh ** -0.5)
    q = q_bf.astype(F32)
    k = k_ref[...].astype(F32)
    v = v_ref[...].astype(F32)
    seg = (lax.broadcasted_iota(jnp.int32, (MLSTM_WIDTH, LANES), 0) // dh
           == lax.broadcasted_iota(jnp.int32, (MLSTM_WIDTH, LANES), 1)).astype(F32)
    ex = (lax.broadcasted_iota(jnp.int32, (LANES, MLSTM_WIDTH), 1) // dh
          == lax.broadcasted_iota(jnp.int32, (LANES, MLSTM_WIDTH), 0)).astype(F32)

    def segsum(x):
        return jnp.dot(x, seg, precision=HI, preferred_element_type=F32)

    def expand(x):
        return jnp.dot(x, ex, precision=HI, preferred_element_type=F32)

    ks = [k] + [shift(k, d) for d in range(1, T)]
    vs = [v] + [shift(v, d) for d in range(1, T)]
    sms = [segsum(q * ks[d]) * ws[d] for d in range(T)]
    den = a * segsum(q * nrep_ref[...])
    for sm in sms:
        den = den + sm
    inv = 1.0 / jnp.maximum(jnp.abs(den), jnp.exp(-m_t))

    def last(x):
        out = jnp.zeros_like(x)
        for jj in range(T):
            out = jnp.where(tmod == T - 1 - jj, x if jj == 0 else pltpu.roll(x, R - jj, 0), out)
        return out

    m_new = last(m_t)
    b_last = last(bcum)
    g = jnp.where(head_ok, jnp.exp(b_last - bcum + li - m_new), 0.0)
    decay = jnp.where(head_ok, jnp.exp(b_last + m0 - m_new), 0.0)
    a_f = expand(a * inv)
    w_f = [expand(sm * inv) for sm in sms]
    g_f = expand(g)
    d_f = expand(decay)
    gv = (g_f * v).astype(BF16)
    rowb = lax.broadcasted_iota(jnp.int32, (R, dh), 0) // T
    for hd in range(H):
        cs = slice(dh * hd, dh * (hd + 1))
        qh = q_bf[:, cs]
        kh = k_ref[:, cs]
        gvh = gv[:, cs]
        qc = jnp.zeros((R, dh), F32)
        for bb in range(MLSTM_S_BB):
            c_old = c_ref[bb, hd]
            qc = jnp.where(rowb == bb, _dot_nt(qh, c_old.astype(BF16)), qc)
            upd = _dot_tn(jnp.where(rowb == bb, gvh, jnp.zeros_like(gvh)), kh)
            c_out[bb, hd] = d_f[T * bb:T * bb + 1, cs] * c_old + upd
        h = a_f[:, cs] * qc
        for d in range(T):
            h = h + w_f[d][:, cs] * vs[d][:, cs]
        mh_ref[:, cs] = _head_norm_gate(h, nw_ref[:, cs], mo_ref[:, cs]).astype(BF16)
    gk = g_f * k
    nsum = gk
    for d in range(1, T):
        nsum = nsum + jnp.where(tmod_w >= d, shift(gk, d), 0.0)
    nrow_ref[...] = d_f * nrep_ref[...] + nsum
    mrow_ref[...] = m_t


def _mlstm_s(m_all, gates, n_rep, m_rep, brow, nw, state_c):
    bb = MLSTM_S_BB
    R = bb * DEC_SEQ
    dh = MLSTM_HEAD_DIM
    H = MLSTM_HEADS
    return pl.pallas_call(
        _mlstm_s_kernel,
        grid=(DEC_BATCH // bb,),
        in_specs=[
            pl.BlockSpec((R, MLSTM_WIDTH), lambda i: (i, 0)),
            pl.BlockSpec((R, MLSTM_WIDTH), lambda i: (i, 1)),
            pl.BlockSpec((R, MLSTM_WIDTH), lambda i: (i, 2)),
            pl.BlockSpec((R, MLSTM_WIDTH), lambda i: (i, 3)),
            pl.BlockSpec((R, LANES), lambda i: (i, 0)),
            pl.BlockSpec((R, MLSTM_WIDTH), lambda i: (i, 0)),
            pl.BlockSpec((R, LANES), lambda i: (i, 0)),
            pl.BlockSpec((1, LANES), lambda i: (0, 0)),
            pl.BlockSpec((1, MLSTM_WIDTH), lambda i: (0, 0)),
            pl.BlockSpec((bb, H, dh, dh), lambda i: (i, 0, 0, 0)),
        ],
        out_specs=[
            pl.BlockSpec((R, MLSTM_WIDTH), lambda i: (i, 0)),
            pl.BlockSpec((bb, H, dh, dh), lambda i: (i, 0, 0, 0)),
            pl.BlockSpec((R, MLSTM_WIDTH), lambda i: (i, 0)),
            pl.BlockSpec((R, LANES), lambda i: (i, 0)),
        ],
        out_shape=[
            jax.ShapeDtypeStruct((S_ROWS, MLSTM_WIDTH), BF16),
            jax.ShapeDtypeStruct((DEC_BATCH, H, dh, dh), F32),
            jax.ShapeDtypeStruct((S_ROWS, MLSTM_WIDTH), F32),
            jax.ShapeDtypeStruct((S_ROWS, LANES), F32),
        ],
        compiler_params=_cparams(("arbitrary",)),
        name="mlstm_s",
    )(m_all, m_all, m_all, m_all, gates, n_rep, m_rep, brow, nw, state_c)


def _outproj_kernel(att_ref, mh_ref, x_ref, g1_ref, sh_ref, sc_ref, nw_ref, wa_ref, wm_ref, x1_ref, h2_ref, *,
                    per_row):
    y = _dot(att_ref[...].astype(BF16), wa_ref[...]) + _dot(mh_ref[...], wm_ref[...])
    x1 = x_ref[...] + _mod_row(g1_ref, per_row) * y
    x1_ref[...] = x1
    r = lax.rsqrt(jnp.mean(x1 * x1, axis=-1, keepdims=True) + EPS)
    h2 = (x1 * r * nw_ref[...]) * (1.0 + _mod_row(sc_ref, per_row)) + _mod_row(sh_ref, per_row)
    h2_ref[...] = h2.astype(BF16)


def _outproj(att, mh, x, mod, mod_row_block, norm_w, w_att, w_m, *, tm, per_row):
    rows = x.shape[0]
    mod_rows = tm if per_row else 8
    mod_idx = (lambda i: i) if per_row else (lambda i: mod_row_block)
    kern = functools.partial(_outproj_kernel, per_row=per_row)
    return pl.pallas_call(
        kern,
        grid=(rows // tm,),
        in_specs=[
            pl.BlockSpec((tm, ATT_WIDTH), lambda i: (i, 0)),
            pl.BlockSpec((tm, MLSTM_WIDTH), lambda i: (i, 0)),
            pl.BlockSpec((tm, D_MODEL), lambda i: (i, 0)),
            pl.BlockSpec((mod_rows, D_MODEL), lambda i: (mod_idx(i), 2)),
            pl.BlockSpec((mod_rows, D_MODEL), lambda i: (mod_idx(i), 3)),
            pl.BlockSpec((mod_rows, D_MODEL), lambda i: (mod_idx(i), 4)),
            pl.BlockSpec((1, D_MODEL), lambda i: (0, 0)),
            pl.BlockSpec((ATT_WIDTH, D_MODEL), lambda i: (0, 0)),
            pl.BlockSpec((MLSTM_WIDTH, D_MODEL), lambda i: (0, 0)),
        ],
        out_specs=[
            pl.BlockSpec((tm, D_MODEL), lambda i: (i, 0)),
            pl.BlockSpec((tm, D_MODEL), lambda i: (i, 0)),
        ],
        out_shape=[
            jax.ShapeDtypeStruct((rows, D_MODEL), F32),
            jax.ShapeDtypeStruct((rows, D_MODEL), BF16),
        ],
        compiler_params=_cparams(("arbitrary",)),
        name="outproj_s" if per_row else "outproj_p",
    )(att, mh, x, mod, mod, mod, norm_w, w_att, w_m)


def _ffn_a_kernel(hp_ref, hs_ref, wg_ref, wu_ref, ap_ref, as_ref, wg_scr, wu_scr):
    i = pl.program_id(1)

    def swiglu(h_ref, a_ref):
        h = h_ref[...]
        g = _dot(h, wg_scr[...])
        u = _dot(h, wu_scr[...])
        a_ref[...] = (g * _sigmoid(g) * u).astype(BF16)

    @pl.when(i == 0)
    def _():
        wg_scr[...] = wg_ref[...].astype(BF16)
        wu_scr[...] = wu_ref[...].astype(BF16)
        swiglu(hs_ref, as_ref)

    @pl.when(i > 0)
    def _():
        swiglu(hp_ref, ap_ref)


def _ffn_a(h2_p, h2_s, w_gate, w_up, *, tm):
    rows_p, rows_s = h2_p.shape[0], h2_s.shape[0]
    tn = 512
    ni = rows_p // tm
    tile = lambda i: jnp.maximum(i - 1, 0)
    return pl.pallas_call(
        _ffn_a_kernel,
        grid=(D_FF // tn, ni + 1),
        in_specs=[
            pl.BlockSpec((tm, D_MODEL), lambda j, i: (tile(i), 0)),
            pl.BlockSpec((rows_s, D_MODEL), lambda j, i: (0, 0)),
            pl.BlockSpec((D_MODEL, tn), lambda j, i: (0, j)),
            pl.BlockSpec((D_MODEL, tn), lambda j, i: (0, j)),
        ],
        out_specs=[
            pl.BlockSpec((tm, tn), lambda j, i: (tile(i), j)),
            pl.BlockSpec((rows_s, tn), lambda j, i: (0, j)),
        ],
        out_shape=[
            jax.ShapeDtypeStruct((rows_p, D_FF), BF16),
            jax.ShapeDtypeStruct((rows_s, D_FF), BF16),
        ],
        scratch_shapes=[pltpu.VMEM((D_MODEL, tn), BF16), pltpu.VMEM((D_MODEL, tn), BF16)],
        compiler_params=_cparams(("arbitrary", "arbitrary")),
        name="ffn_a",
    )(h2_p, h2_s, w_gate, w_up)


def _ffn_b_kernel(a_ref, wd_ref, x1_ref, g2_ref, fw_ref, y_ref, *rest, per_row, emit_w):
    kk = pl.program_id(1)

    def weight_tile():
        wt = wd_ref[...].astype(BF16)
        if emit_w:
            rest[0][...] = wt
        return wt

    @pl.when(kk == 0)
    def _():
        y_ref[...] = _dot(a_ref[...], weight_tile())

    last = pl.num_programs(1) - 1

    @pl.when((kk > 0) & (kk < last))
    def _():
        y_ref[...] += _dot(a_ref[...], weight_tile())

    @pl.when(kk == last)
    def _():
        x2 = x1_ref[...] + _mod_row(g2_ref, per_row) * (y_ref[...] + _dot(a_ref[...], weight_tile()))
        y_ref[...] = x2 * lax.rsqrt(jnp.mean(x2 * x2, axis=-1, keepdims=True) + EPS) * fw_ref[...]


def _ffn_b(a, w_down, x1, mod, mod_row_block, final_w, *, tm, tk, per_row, emit_w):
    rows = a.shape[0]
    assert D_FF // tk >= 2, "the first and the last k step are different code paths"
    mod_rows = tm if per_row else 8
    mod_idx = (lambda i: i) if per_row else (lambda i: mod_row_block)
    kern = functools.partial(_ffn_b_kernel, per_row=per_row, emit_w=emit_w)
    w_spec = pl.BlockSpec((tk, D_MODEL), lambda i, k: (k, 0))
    y_spec = pl.BlockSpec((tm, D_MODEL), lambda i, k: (i, 0))
    y_shape = jax.ShapeDtypeStruct((rows, D_MODEL), F32)
    return pl.pallas_call(
        kern,
        grid=(rows // tm, D_FF // tk),
        in_specs=[
            pl.BlockSpec((tm, tk), lambda i, k: (i, k)),
            w_spec,
            pl.BlockSpec((tm, D_MODEL), lambda i, k: (i, 0)),
            pl.BlockSpec((mod_rows, D_MODEL), lambda i, k: (mod_idx(i), 5)),
            pl.BlockSpec((1, D_MODEL), lambda i, k: (0, 0)),
        ],
        out_specs=[y_spec, w_spec] if emit_w else y_spec,
        out_shape=[y_shape, jax.ShapeDtypeStruct((D_FF, D_MODEL), BF16)] if emit_w else y_shape,
        compiler_params=_cparams(("arbitrary", "arbitrary")),
        name="ffn_b_s" if per_row else "ffn_b_p",
    )(a, w_down, x1, mod, final_w)


def _rope_tables(pos):
    half = ROPE_DIM // 2
    inv = np.float32(ROPE_THETA) ** (-np.arange(0, ROPE_DIM, 2, dtype=np.float32) / np.float32(ROPE_DIM))
    d = np.arange(LANES) % ATT_HEAD_DIM
    ang = pos.astype(np.float32)[:, None] * inv[d % half][None, :].astype(np.float32)
    cos, sin = np.cos(ang), np.sin(ang)
    d = d[None, :]
    tables = (np.where(d < ROPE_DIM, cos, 1.0), np.where(d < half, -sin, 0.0),
              np.where((d >= half) & (d < ROPE_DIM), sin, 0.0))
    return tuple(jnp.asarray(t.astype(np.float32)) for t in tables)


def kernel(x_prompt, x_sample, cache_k_win, cache_v_win, state_C, state_n, state_m, c_prompt, c_sample,
           norm1_w, norm2_w, final_norm_w, w_ada, b_ada, w_in, b_ig, b_fg, attn_sinks, mh_norm_w,
           w_out, w_gate, w_up, w_down):
    assert w_in.shape[0] == 1, "single-layer trunk"
    T = DEC_SEQ
    xp = x_prompt[0]
    xs = x_sample.reshape(S_ROWS, D_MODEL)

    c_all = jnp.concatenate([c_sample, c_prompt, jnp.zeros((15, D_MODEL), F32)], axis=0)
    mod = _ada(c_all, w_ada[0], b_ada)
    prompt_mod_block = S_ROWS // 8

    w_in_t = jnp.transpose(w_in[0])
    wq_t = (w_in_t[:ATT_WIDTH].reshape(ATT_KV_HEADS, ATT_GROUP, ATT_HEAD_DIM, D_MODEL)
            .transpose(1, 0, 2, 3).reshape(ATT_WIDTH, D_MODEL).astype(BF16))
    w_gates_t = w_in_t[MAIN_WIDTH:]
    wg = jnp.pad(w_gates_t, ((0, 16 - 2 * MLSTM_HEADS), (0, 0)))
    w_out_att = (w_out[0, :ATT_WIDTH].reshape(ATT_KV_HEADS, ATT_GROUP, ATT_HEAD_DIM, D_MODEL)
                 .transpose(1, 0, 2, 3).reshape(ATT_WIDTH, D_MODEL).astype(BF16))
    w_out_m = w_out[0, ATT_WIDTH:].astype(BF16)
    n1 = norm1_w.reshape(1, D_MODEL)
    n2 = norm2_w.reshape(1, D_MODEL)
    fw = final_norm_w.reshape(1, D_MODEL)
    nw = mh_norm_w.reshape(1, MLSTM_WIDTH)
    gate_bias = jnp.concatenate([b_ig[0], b_fg[0]])
    brow = jnp.pad(gate_bias, (0, LANES - 2 * MLSTM_HEADS)).reshape(1, LANES)
    bcol = jnp.broadcast_to(jnp.pad(gate_bias, (0, 16 - 2 * MLSTM_HEADS))[:, None], (16, MLSTM_CHUNK_P))

    rope_p = _rope_tables(np.arange(SEQ))
    rope_s = _rope_tables(np.tile(PAST_LEN + np.arange(T), DEC_BATCH))

    tm_p = 1024
    q_s, _, kv32_s, m_s, g_s, _, w_in_bf = _inproj(xs, mod, 0, n1, wq_t, w_in_t, wg, *rope_s,
                                                   tm=S_ROWS, per_row=True, emit_w=True)
    q_p, kv_p, kv32_p, m_p, g_p, gt_p = _inproj(xp, mod, prompt_mod_block, n1, wq_t, w_in_bf, wg, *rope_p,
                                                tm=tm_p, per_row=False, emit_w=False)

    sinks = attn_sinks[0]
    sink_col = jnp.broadcast_to(sinks.reshape(ATT_HEADS, 1, 1), (ATT_HEADS, 8, LANES)).reshape(128, LANES)
    ck = jnp.transpose(cache_k_win[0], (0, 2, 3, 1))
    cv = jnp.transpose(cache_v_win[0], (0, 2, 3, 1))
    att_s, kwin_s, vwin_s = _attn_s(sink_col, q_s, kv32_s, ck, cv)

    n_rep = jnp.repeat(state_n[0].reshape(DEC_BATCH, MLSTM_WIDTH), T, axis=0)
    m_rep = jnp.pad(jnp.repeat(state_m[0], T, axis=0), ((0, 0), (0, LANES - MLSTM_HEADS)))
    mh_s, c_s, nrow_s, mrow_s = _mlstm_s(m_s, g_s, n_rep, m_rep, brow, nw, state_C[0])

    x1_p, h2_p, c_p, n_p, mm_p = _mix_p(sinks, q_p, kv_p, m_p, g_p, gt_p, brow, bcol, nw, xp, mod, prompt_mod_block,
                                        n2, w_out_att, w_out_m)
    x1_s, h2_s = _outproj(att_s, mh_s, xs, mod, 0, n2, w_out_att, w_out_m, tm=S_ROWS // 2, per_row=True)
    a_p, a_s = _ffn_a(h2_p, h2_s, w_gate[0], w_up[0], tm=tm_p)
    y_s, w_down_bf = _ffn_b(a_s, w_down[0], x1_s, mod, 0, fw, tm=S_ROWS, tk=1408, per_row=True, emit_w=True)
    y_p = _ffn_b(a_p, w_down_bf, x1_p, mod, prompt_mod_block, fw, tm=tm_p, tk=512, per_row=False, emit_w=False)

    kv_shape = (1, 1, WINDOW, ATT_KV_HEADS, ATT_HEAD_DIM)
    kv_last = kv32_p[tm_p - WINDOW:]
    dh = MLSTM_HEAD_DIM
    return (
        y_p.reshape(1, SEQ, D_MODEL),
        y_s.reshape(DEC_BATCH, T, D_MODEL),
        kv_last[:, :KV_WIDTH].reshape(kv_shape),
        kv_last[:, KV_WIDTH:].reshape(kv_shape),
        c_p.reshape(1, 1, MLSTM_HEADS, dh, dh),
        n_p[:MLSTM_HEADS].reshape(1, 1, MLSTM_HEADS, dh),
        mm_p[:MLSTM_HEADS, 0].reshape(1, 1, MLSTM_HEADS),
        jnp.transpose(kwin_s, (0, 3, 1, 2))[None],
        jnp.transpose(vwin_s, (0, 3, 1, 2))[None],
        c_s.reshape(1, DEC_BATCH, MLSTM_HEADS, dh, dh),
        nrow_s[T - 1::T].reshape(1, DEC_BATCH, MLSTM_HEADS, dh),
        mrow_s[T - 1::T, :MLSTM_HEADS].reshape(1, DEC_BATCH, MLSTM_HEADS),
    )
```
